```python
import math, functools
import jax, jax.numpy as jnp
from jax import lax
import numpy as np

D_MODEL = 1024
BATCH = 2
SEQ = 8192
DEPTH = 4
DEC_BATCH = 128
DEC_SEQ = 4
PAST_LEN = 8192
PAGE_SIZE = 128

D_CONV = 512
CONV_W = 31
N_HEADS = 8
N_KV = 2
HEAD_DIM = 64
Q_PER_KV = N_HEADS // N_KV
WINDOW = 128
N_BUCKETS = 32
MAX_DIST = 128
M_HEADS = 4
M_DK = 128
M_DV = 128
M_CHUNK = 64
F_BIAS_INIT = 3.0
D_FF = 2816
N_EXPERTS = 8
TOP_K = 2
D_EXPERT = 3584
MOE_BLOCK = 128
N_DENSE = (DEPTH + 1) // 2
N_MOE = DEPTH // 2
N_BRANCH = 3
ALPHA = (2 * DEPTH) ** 0.25
BETA = (8 * DEPTH) ** -0.25
LN_EPS = 1e-5
N_IN = 2 * D_CONV + (N_HEADS + 2 * N_KV) * HEAD_DIM + M_HEADS * (2 * M_DK + 2 * M_DV + 2) + N_BRANCH * D_MODEL

kernel_name = "hybrid_conv_swa_mlstm_decoder_step"


def _in_bounds():
    sizes = [D_CONV, D_CONV, N_HEADS * HEAD_DIM, N_KV * HEAD_DIM, N_KV * HEAD_DIM,
             M_HEADS * M_DK, M_HEADS * M_DK, M_HEADS * M_DV, M_HEADS * M_DV,
             M_HEADS, M_HEADS, N_BRANCH * D_MODEL]
    bounds, start = [], 0
    for n in sizes:
        bounds.append((start, start + n))
        start += n
    return bounds


def _layer_norm(x, g=None, b=None):
    xf = x.astype(jnp.float32)
    mu = xf.mean(-1, keepdims=True)
    var = jnp.square(xf - mu).mean(-1, keepdims=True)
    y = (xf - mu) * lax.rsqrt(var + LN_EPS)
    if g is not None:
        y = y * g.astype(jnp.float32) + b.astype(jnp.float32)
    return y.astype(x.dtype)


def _t5_bucket(dist):
    max_exact = N_BUCKETS // 2
    d = jnp.maximum(dist, 0)
    large = max_exact + (jnp.log(jnp.maximum(d, 1).astype(jnp.float32) / max_exact)
                         / math.log(MAX_DIST / max_exact) * (N_BUCKETS - max_exact)).astype(jnp.int32)
    return jnp.where(d < max_exact, d, jnp.minimum(large, N_BUCKETS - 1))


def _swa_attend(q, k, v, sinks, rel_bias, dist, valid):
    Qn, Kn = dist.shape
    s = jnp.einsum('bnqgrd,bnkgd->bngrqk', q, k).astype(jnp.float32) * (HEAD_DIM ** -0.5)
    bias = rel_bias[_t5_bucket(dist)].astype(jnp.float32)
    s = s + jnp.transpose(bias, (2, 0, 1)).reshape(N_KV, Q_PER_KV, Qn, Kn)
    s = jnp.where(valid[None, :, None, None], s, -jnp.inf)
    sink = sinks.astype(jnp.float32).reshape(N_KV, Q_PER_KV)[:, :, None, None]
    mx = jnp.maximum(s.max(-1, keepdims=True), sink)
    p = jnp.exp(s - mx)
    p = p / (p.sum(-1, keepdims=True) + jnp.exp(sink - mx))
    return jnp.einsum('bngrqk,bnkgd->bnqgrd', p.astype(v.dtype), v)


def _mlstm(q, k, v, i_pre, f_pre, C0, n0, m0):
    B, T = q.shape[:2]
    L = math.gcd(T, M_CHUNK)
    nc = T // L
    f32 = jnp.float32

    def to_chunks(a):
        a = a.astype(f32).reshape((B, nc, L) + a.shape[2:])
        return jnp.moveaxis(jnp.moveaxis(a, 1, 0), 3, 2)

    qc = to_chunks(q)
    kc = to_chunks(k) * (M_DK ** -0.5)
    vc = to_chunks(v)
    ic = to_chunks(i_pre)
    lfc = to_chunks(jax.nn.log_sigmoid(f_pre.astype(f32)))
    causal = jnp.tril(jnp.ones((L, L), dtype=bool))

    def step(carry, inp):
        C, n, m = carry
        qb, kb, vb, ib, lfb = inp
        F = jnp.cumsum(lfb, axis=-1)
        Dm = jnp.where(causal, F[..., :, None] - F[..., None, :] + ib[..., None, :], -jnp.inf)
        m_t = jnp.maximum(m[..., None] + F, Dm.max(-1))
        inter = jnp.exp(m[..., None] + F - m_t)
        S = jnp.einsum('bhtd,bhsd->bhts', qb, kb) * jnp.exp(Dm - m_t[..., None])
        num = inter[..., None] * jnp.einsum('bhtd,bhde->bhte', qb, C) + jnp.einsum('bhts,bhse->bhte', S, vb)
        qn = inter * jnp.einsum('bhtd,bhd->bht', qb, n) + S.sum(-1)
        h = num / jnp.maximum(jnp.abs(qn), jnp.exp(-m_t))[..., None]
        m_end = m_t[..., -1]
        carry_decay = jnp.exp(m + F[..., -1] - m_end)
        w_s = jnp.exp(F[..., -1:] - F + ib - m_end[..., None])
        C = carry_decay[..., None, None] * C + jnp.einsum('bhs,bhsd,bhse->bhde', w_s, kb, vb)
        n = carry_decay[..., None] * n + jnp.einsum('bhs,bhsd->bhd', w_s, kb)
        return (C, n, m_end), h

    (C1, n1, m1), hs = lax.scan(step, (C0.astype(f32), n0.astype(f32), m0.astype(f32)),
                                (qc, kc, vc, ic, lfc))
    hs = hs.transpose(1, 0, 3, 2, 4).reshape(B, T, M_HEADS, M_DV)
    return hs, C1, n1, m1


def _token_mixers(h, w_in, b_in, conv_w, conv_b, conv_ln_g, conv_ln_b, w_conv_out,
                  attn_sinks, rel_bias, w_attn_out, m_norm_g, w_m_out, w_out,
                  conv_buf, kv_buf, C0, n0, m0):
    B, T, _ = h.shape
    z = h @ w_in + b_in
    u_a, u_b, q, k, v, mq, mk, mv, mo, mi, mf, gates = [z[..., s:e] for s, e in _in_bounds()]

    a = u_a * jax.nn.sigmoid(u_b)
    ext = jnp.concatenate([conv_buf.astype(a.dtype), a], axis=1)
    yc = lax.conv_general_dilated(ext, conv_w[:, None, :].astype(a.dtype), (1,), 'VALID',
                                  dimension_numbers=('NWC', 'WIO', 'NWC'),
                                  feature_group_count=D_CONV) + conv_b
    y_conv = jax.nn.silu(_layer_norm(yc, conv_ln_g, conv_ln_b)) @ w_conv_out
    new_conv = ext[:, -(CONV_W - 1):]

    q = q.reshape(B, T, N_KV, Q_PER_KV, HEAD_DIM)
    k = k.reshape(B, T, N_KV, HEAD_DIM)
    v = v.reshape(B, T, N_KV, HEAD_DIM)
    if kv_buf is None:
        nb = T // WINDOW
        qb = q.reshape(B, nb, WINDOW, N_KV, Q_PER_KV, HEAD_DIM)
        kb = k.reshape(B, nb, WINDOW, N_KV, HEAD_DIM)
        vb = v.reshape(B, nb, WINDOW, N_KV, HEAD_DIM)
        shift = ((0, 0), (1, 0), (0, 0), (0, 0), (0, 0))
        kb = jnp.concatenate([jnp.pad(kb[:, :-1], shift), kb], axis=2)
        vb = jnp.concatenate([jnp.pad(vb[:, :-1], shift), vb], axis=2)
        qi = jnp.arange(WINDOW)[:, None]
        kj = jnp.arange(2 * WINDOW)[None, :]
        dist = qi + WINDOW - kj
        kpos = jnp.arange(nb)[:, None, None] * WINDOW + kj[None] - WINDOW
        valid = (dist >= 0) & (dist < WINDOW) & (kpos >= 0)
        n_keep = min(WINDOW, T)
        new_k, new_v = k[:, T - n_keep:], v[:, T - n_keep:]
    else:
        k_buf, v_buf = kv_buf
        wb = k_buf.shape[1]
        k_all = jnp.concatenate([k_buf.astype(k.dtype), k], axis=1)
        v_all = jnp.concatenate([v_buf.astype(v.dtype), v], axis=1)
        qb, kb, vb = q[:, None], k_all[:, None], v_all[:, None]
        dist = jnp.arange(T)[:, None] + wb - jnp.arange(wb + T)[None, :]
        valid = ((dist >= 0) & (dist < WINDOW))[None]
        new_k, new_v = k_all[:, T:], v_all[:, T:]
    o = _swa_attend(qb, kb, vb, attn_sinks, rel_bias, dist, valid).reshape(B, T, N_HEADS * HEAD_DIM)
    y_swa = o @ w_attn_out

    hm, C1, n1, m1 = _mlstm(mq.reshape(B, T, M_HEADS, M_DK), mk.reshape(B, T, M_HEADS, M_DK),
                            mv.reshape(B, T, M_HEADS, M_DV), mi, mf, C0, n0, m0)
    hm = _layer_norm(hm) * m_norm_g.reshape(M_HEADS, M_DV).astype(jnp.float32)
    hm = (jax.nn.sigmoid(mo.astype(jnp.float32)).reshape(B, T, M_HEADS, M_DV) * hm)
    y_m = hm.astype(h.dtype).reshape(B, T, M_HEADS * M_DV) @ w_m_out

    g = jax.nn.sigmoid(gates).reshape(B, T, N_BRANCH, D_MODEL)
    merged = g[..., 0, :] * y_conv + g[..., 1, :] * y_swa + g[..., 2, :] * y_m
    return merged @ w_out, (new_k, new_v, new_conv, C1, n1, m1)


def _dense_ffn(x, w_gate, w_up, w_down):
    return (jax.nn.silu(x @ w_gate) * (x @ w_up)) @ w_down


def _moe_ffn(x, router_w, router_b, w_gate, w_up, w_down):
    n_tok = x.shape[0]
    n_assign = n_tok * TOP_K
    f32 = jnp.float32
    logits = x.astype(f32) @ router_w.astype(f32) + router_b.astype(f32)
    top_logit, top_idx = lax.top_k(logits, TOP_K)
    top_w = jax.nn.softmax(top_logit, axis=-1)
    flat_e = top_idx.reshape(-1)
    order = jnp.argsort(flat_e)
    sorted_e = flat_e[order]
    counts = jnp.bincount(flat_e, length=N_EXPERTS)
    padded = (counts + MOE_BLOCK - 1) // MOE_BLOCK * MOE_BLOCK
    pad_end = jnp.cumsum(padded)
    pad_start = pad_end - padded
    raw_start = jnp.cumsum(counts) - counts
    dest = pad_start[sorted_e] + jnp.arange(n_assign) - raw_start[sorted_e]
    n_blocks = -(-(n_assign + N_EXPERTS * (MOE_BLOCK - 1)) // MOE_BLOCK)
    tok = order // TOP_K
    xs = jnp.zeros((n_blocks * MOE_BLOCK, D_MODEL), x.dtype).at[dest].set(x[tok])
    blk_start = jnp.arange(n_blocks) * MOE_BLOCK
    blk_e = jnp.minimum(jnp.sum(pad_end[None, :] <= blk_start[:, None], axis=1), N_EXPERTS - 1)

    def expert_block(args):
        xb, e = args
        return (jax.nn.silu(xb @ w_gate[e]) * (xb @ w_up[e])) @ w_down[e]

    ys = lax.map(expert_block, (xs.reshape(n_blocks, MOE_BLOCK, D_MODEL), blk_e)).reshape(-1, D_MODEL)
    contrib = ys[dest].astype(f32) * top_w.reshape(-1)[order][:, None]
    y = jnp.zeros((n_tok, D_MODEL), f32).at[tok].add(contrib)
    return y.astype(x.dtype)


def _layer(x, c, w_ada, b_ada, post_g, post_b, mix_w, ffn, conv_buf, kv_buf, C0, n0, m0):
    mod = jax.nn.silu(c) @ w_ada + b_ada
    sh1, sc1, g1, sh2, sc2, g2 = jnp.split(mod[:, None, :], 6, axis=-1)
    h = _layer_norm(x) * (1 + sc1) + sh1
    mix, new_state = _token_mixers(h, *mix_w, conv_buf, kv_buf, C0, n0, m0)
    x = _layer_norm(ALPHA * x + g1 * mix, post_g[0], post_b[0])
    h = _layer_norm(x) * (1 + sc2) + sh2
    f = ffn(h.reshape(-1, D_MODEL)).reshape(x.shape)
    x = _layer_norm(ALPHA * x + g2 * f, post_g[1], post_b[1])
    return x, new_state


def setup_inputs(seed: int = 0) -> dict:
    key = jax.random.key(seed)
    keys = jax.random.split(key, 48)
    counter = [0]

    def nrm(shape, scale=1.0):
        kk = keys[counter[0]]
        counter[0] += 1
        return jax.random.normal(kk, shape, jnp.float32) * scale

    w_buf = min(WINDOW, PAST_LEN)
    x_prompt = nrm((BATCH, SEQ, D_MODEL))
    x_sample = nrm((DEC_BATCH, DEC_SEQ, D_MODEL))
    cache_swa_k = nrm((DEPTH, DEC_BATCH, w_buf, N_KV, HEAD_DIM))
    cache_swa_v = nrm((DEPTH, DEC_BATCH, w_buf, N_KV, HEAD_DIM))
    state_conv = nrm((DEPTH, DEC_BATCH, CONV_W - 1, D_CONV), 0.5)
    state_mlstm_C = nrm((DEPTH, DEC_BATCH, M_HEADS, M_DK, M_DV), 0.5)
    state_mlstm_n = nrm((DEPTH, DEC_BATCH, M_HEADS, M_DK), 0.5)
    state_mlstm_m = nrm((DEPTH, DEC_BATCH, M_HEADS))
    c_prompt = nrm((BATCH, D_MODEL))
    c_sample = nrm((DEC_BATCH, D_MODEL))
    w_ada = nrm((DEPTH, D_MODEL, 6 * D_MODEL), D_MODEL ** -0.5)
    b_ada = nrm((DEPTH, 6 * D_MODEL), 0.02)
    w_in = nrm((DEPTH, D_MODEL, N_IN), D_MODEL ** -0.5)
    fs, fe = _in_bounds()[10]
    b_in = nrm((DEPTH, N_IN), 0.02).at[:, fs:fe].add(F_BIAS_INIT)
    conv_w = nrm((DEPTH, CONV_W, D_CONV), CONV_W ** -0.5)
    conv_b = nrm((DEPTH, D_CONV), 0.02)
    conv_ln_g = 1.0 + nrm((DEPTH, D_CONV), 0.02)
    conv_ln_b = nrm((DEPTH, D_CONV), 0.02)
    w_conv_out = nrm((DEPTH, D_CONV, D_MODEL), D_CONV ** -0.5)
    attn_sinks = nrm((DEPTH, N_HEADS))
    rel_bias = nrm((N_BUCKETS, N_HEADS), 0.5)
    w_attn_out = nrm((DEPTH, N_HEADS * HEAD_DIM, D_MODEL), (N_HEADS * HEAD_DIM) ** -0.5)
    m_norm_g = 1.0 + nrm((DEPTH, M_HEADS * M_DV), 0.02)
    w_m_out = nrm((DEPTH, M_HEADS * M_DV, D_MODEL), (M_HEADS * M_DV) ** -0.5)
    w_out = nrm((DEPTH, D_MODEL, D_MODEL), BETA * D_MODEL ** -0.5)
    post_ln_g = 1.0 + nrm((DEPTH, 2, D_MODEL), 0.02)
    post_ln_b = nrm((DEPTH, 2, D_MODEL), 0.02)
    ffn_w_gate = nrm((N_DENSE, D_MODEL, D_FF), D_MODEL ** -0.5)
    ffn_w_up = nrm((N_DENSE, D_MODEL, D_FF), D_MODEL ** -0.5)
    ffn_w_down = nrm((N_DENSE, D_FF, D_MODEL), BETA * D_FF ** -0.5)
    router_w = nrm((N_MOE, D_MODEL, N_EXPERTS), D_MODEL ** -0.5)
    router_b = nrm((N_MOE, N_EXPERTS), 0.01)
    moe_w_gate = nrm((N_MOE, N_EXPERTS, D_MODEL, D_EXPERT), D_MODEL ** -0.5)
    moe_w_up = nrm((N_MOE, N_EXPERTS, D_MODEL, D_EXPERT), D_MODEL ** -0.5)
    moe_w_down = nrm((N_MOE, N_EXPERTS, D_EXPERT, D_MODEL), BETA * D_EXPERT ** -0.5)
    return {"x_prompt": x_prompt, "x_sample": x_sample,
            "cache_swa_k": cache_swa_k, "cache_swa_v": cache_swa_v, "state_conv": state_conv,
            "state_mlstm_C": state_mlstm_C, "state_mlstm_n": state_mlstm_n, "state_mlstm_m": state_mlstm_m,
            "c_prompt": c_prompt, "c_sample": c_sample,
            "w_ada": w_ada, "b_ada": b_ada, "w_in": w_in, "b_in": b_in,
            "conv_w": conv_w, "conv_b": conv_b, "conv_ln_g": conv_ln_g, "conv_ln_b": conv_ln_b,
            "w_conv_out": w_conv_out, "attn_sinks": attn_sinks, "rel_bias": rel_bias,
            "w_attn_out": w_attn_out, "m_norm_g": m_norm_g, "w_m_out": w_m_out, "w_out": w_out,
            "post_ln_g": post_ln_g, "post_ln_b": post_ln_b,
            "ffn_w_gate": ffn_w_gate, "ffn_w_up": ffn_w_up, "ffn_w_down": ffn_w_down,
            "router_w": router_w, "router_b": router_b,
            "moe_w_gate": moe_w_gate, "moe_w_up": moe_w_up, "moe_w_down": moe_w_down}


def reference(x_prompt, x_sample, cache_swa_k, cache_swa_v, state_conv, state_mlstm_C, state_mlstm_n,
              state_mlstm_m, c_prompt, c_sample, w_ada, b_ada, w_in, b_in, conv_w, conv_b, conv_ln_g,
              conv_ln_b, w_conv_out, attn_sinks, rel_bias, w_attn_out, m_norm_g, w_m_out, w_out,
              post_ln_g, post_ln_b, ffn_w_gate, ffn_w_up, ffn_w_down, router_w, router_b,
              moe_w_gate, moe_w_up, moe_w_down):
    bp = x_prompt.shape[0]
    xp, xs = x_prompt, x_sample
    new_p = [[] for _ in range(6)]
    new_s = [[] for _ in range(6)]
    for l in range(DEPTH):
        mix_w = (w_in[l], b_in[l], conv_w[l], conv_b[l], conv_ln_g[l], conv_ln_b[l], w_conv_out[l],
                 attn_sinks[l], rel_bias, w_attn_out[l], m_norm_g[l], w_m_out[l], w_out[l])
        j = l // 2
        if l % 2 == 0:
            ffn = functools.partial(_dense_ffn, w_gate=ffn_w_gate[j], w_up=ffn_w_up[j], w_down=ffn_w_down[j])
        else:
            ffn = functools.partial(_moe_ffn, router_w=router_w[j], router_b=router_b[j],
                                    w_gate=moe_w_gate[j], w_up=moe_w_up[j], w_down=moe_w_down[j])
        conv0 = jnp.zeros((bp, CONV_W - 1, D_CONV), xp.dtype)
        C0 = jnp.zeros((bp, M_HEADS, M_DK, M_DV), jnp.float32)
        n0 = jnp.zeros((bp, M_HEADS, M_DK), jnp.float32)
        m0 = jnp.zeros((bp, M_HEADS), jnp.float32)
        xp, st = _layer(xp, c_prompt, w_ada[l], b_ada[l], post_ln_g[l], post_ln_b[l], mix_w, ffn,
                        conv0, None, C0, n0, m0)
        for lst, arr in zip(new_p, st):
            lst.append(arr)
        xs, st = _layer(xs, c_sample, w_ada[l], b_ada[l], post_ln_g[l], post_ln_b[l], mix_w, ffn,
                        state_conv[l], (cache_swa_k[l], cache_swa_v[l]),
                        state_mlstm_C[l], state_mlstm_n[l], state_mlstm_m[l])
        for lst, arr in zip(new_s, st):
            lst.append(arr)
    p_k, p_v, p_conv, p_C, p_n, p_m = [jnp.stack(a) for a in new_p]
    s_k, s_v, s_conv, s_C, s_n, s_m = [jnp.stack(a) for a in new_s]
    return (xp, xs, p_k, p_v, p_conv, p_C, p_n, p_m, s_k, s_v, s_conv, s_C, s_n, s_m)
```

```python
import functools
import math

import jax
import jax.numpy as jnp
from jax import lax
from jax.experimental import pallas as pl
from jax.experimental.pallas import tpu as pltpu

F32 = jnp.float32
BF16 = jnp.bfloat16

D_MODEL = 1024
D_CONV = 512
CONV_W = 31
CONV_PAD = 32
N_HEADS = 8
N_KV = 2
HEAD_DIM = 64
Q_PER_KV = N_HEADS // N_KV
WINDOW = 128
N_BUCKETS = 32
MAX_DIST = 128
M_HEADS = 4
M_DK = 128
M_DV = 128
N_EXPERTS = 8
LN_EPS = 1e-5
LANES = 128
NEG_INF = float("-inf")
VMEM_LIMIT = 56 * 1024 * 1024

Z_UA, Z_UB, Z_Q, Z_K, Z_V, Z_IF = 0, 512, 1024, 1536, 1664, 1792
Z_MQ, Z_MK, Z_MV, Z_MO, Z_G = 2048, 2560, 3072, 3584, 4096
Z_W = 7168
TN_IN = 1024


def _cparams(*sem):
    return pltpu.CompilerParams(dimension_semantics=sem, vmem_limit_bytes=VMEM_LIMIT)


def _sigmoid(x):
    return 1.0 / (1.0 + jnp.exp(-x))


def _silu(x):
    return x * _sigmoid(x)


def _log_sigmoid(x):
    return jnp.minimum(x, 0.0) - jnp.log(1.0 + jnp.exp(-jnp.abs(x)))


def _norm(x):
    mu = jnp.mean(x, axis=-1, keepdims=True)
    xc = x - mu
    var = jnp.mean(xc * xc, axis=-1, keepdims=True)
    return xc * lax.rsqrt(var + LN_EPS)


def _dot(a, b):
    return jnp.dot(a, b, preferred_element_type=F32)


def _dot_nt(a, b):
    return lax.dot_general(a, b, (((1,), (1,)), ((), ())), preferred_element_type=F32)


def _dot_tn(a, b):
    return lax.dot_general(a, b, (((0,), (0,)), ((), ())), preferred_element_type=F32)


def _dot_hi(a, b):
    return jnp.dot(a, b, preferred_element_type=F32, precision=lax.Precision.HIGHEST)


def _ada_kernel(c_ref, w_ref, b_ref, o_ref):
    s = _silu(c_ref[...]).astype(BF16)
    o_ref[0] = _dot(s, w_ref[0].astype(BF16)) + b_ref[0]


def _ada_mod(c_all, w_ada, b_ada):
    depth, d, n6 = w_ada.shape
    rows = c_all.shape[0]
    return pl.pallas_call(
        _ada_kernel,
        grid=(depth, n6 // d),
        in_specs=[pl.BlockSpec((rows, d), lambda l, j: (0, 0)),
                  pl.BlockSpec((1, d, d), lambda l, j: (l, 0, j)),
                  pl.BlockSpec((1, 1, d), lambda l, j: (l, 0, j))],
        out_specs=pl.BlockSpec((1, rows, d), lambda l, j: (l, 0, j)),
        out_shape=jax.ShapeDtypeStruct((depth, rows, n6), F32),
        compiler_params=_cparams("parallel", "parallel"),
        name="ada_mod",
    )(c_all, w_ada, b_ada.reshape(depth, 1, n6))


def _ln_proj_kernel(x_ref, sh_ref, sc_ref, w_ref, b_ref, z_ref, h_scr):
    @pl.when(pl.program_id(1) == 0)
    def _():
        h = _norm(x_ref[...]) * (1.0 + sc_ref[0]) + sh_ref[0]
        h_scr[...] = h.astype(BF16)

    z_ref[...] = _dot(h_scr[...], w_ref[...]) + b_ref[...]


class _RowMap:
    def __init__(self, t, bp, tp, n_p):
        self.t = t
        self.bp = bp
        self.per_seq = tp // t
        self.np_tiles = n_p // t

    def mod(self, k):
        return lambda i, *_: (jnp.minimum(i // self.per_seq, self.bp), jnp.maximum(i - self.np_tiles, 0), k)

    def prompt(self, i):
        return jnp.minimum(i, self.np_tiles - 1)

    def sample(self, i):
        return jnp.maximum(i - self.np_tiles, 0)


def _ln_proj(x, modx, w, b, rm):
    n, d = x.shape
    zw = w.shape[1]
    tm = rm.t
    return pl.pallas_call(
        _ln_proj_kernel,
        grid=(n // tm, zw // TN_IN),
        in_specs=[pl.BlockSpec((tm, d), lambda i, j: (i, 0)),
                  pl.BlockSpec((1, tm, d), rm.mod(0)),
                  pl.BlockSpec((1, tm, d), rm.mod(1)),
                  pl.BlockSpec((d, TN_IN), lambda i, j: (0, j)),
                  pl.BlockSpec((1, TN_IN), lambda i, j: (0, j))],
        out_specs=pl.BlockSpec((tm, TN_IN), lambda i, j: (i, j)),
        out_shape=jax.ShapeDtypeStruct((n, zw), F32),
        scratch_shapes=[pltpu.VMEM((tm, d), BF16)],
        compiler_params=_cparams("parallel", "arbitrary"),
        name="ln_proj",
    )(x, modx, modx, w, b)


def _conv_tail(yc, g_ref, b_ref):
    y = _norm(yc) * g_ref[...] + b_ref[...]
    return _silu(y).astype(BF16)


def _conv_prompt_kernel(ua_ref, ub_ref, st_ref, cw_ref, cb_ref, g_ref, b_ref, o_ref, ns_ref, ext):
    t = pl.program_id(1)
    tc = ua_ref.shape[0]

    @pl.when(t == 0)
    def _():
        ext[0:CONV_PAD, :] = st_ref[0]

    @pl.when(t > 0)
    def _():
        ext[0:CONV_PAD, :] = ext[tc:tc + CONV_PAD, :]

    ext[CONV_PAD:, :] = ua_ref[...] * _sigmoid(ub_ref[...])
    off = CONV_PAD - (CONV_W - 1)
    acc = jnp.zeros((tc, D_CONV), F32) + cb_ref[...]
    for w in range(CONV_W):
        acc = acc + ext[off + w:off + w + tc, :] * cw_ref[w:w + 1, :]
    o_ref[...] = _conv_tail(acc, g_ref, b_ref)

    @pl.when(t == pl.num_programs(1) - 1)
    def _():
        ns_ref[0] = ext[tc:tc + CONV_PAD, :]


def _conv_prompt(z, state_pad, cw, cb, g, b, bp, tp, tc):
    nt = tp // tc
    vec = pl.BlockSpec((1, D_CONV), lambda bb, t: (0, 0))
    return pl.pallas_call(
        _conv_prompt_kernel,
        grid=(bp, nt),
        in_specs=[pl.BlockSpec((tc, D_CONV), lambda bb, t: (bb * nt + t, Z_UA // D_CONV)),
                  pl.BlockSpec((tc, D_CONV), lambda bb, t: (bb * nt + t, Z_UB // D_CONV)),
                  pl.BlockSpec((1, CONV_PAD, D_CONV), lambda bb, t: (bb, 0, 0)),
                  pl.BlockSpec((CONV_PAD, D_CONV), lambda bb, t: (0, 0)),
                  vec, vec, vec],
        out_specs=[pl.BlockSpec((tc, D_CONV), lambda bb, t: (bb * nt + t, 0)),
                   pl.BlockSpec((1, CONV_PAD, D_CONV), lambda bb, t: (bb, 0, 0))],
        out_shape=[jax.ShapeDtypeStruct((bp * tp, D_CONV), BF16),
                   jax.ShapeDtypeStruct((bp, CONV_PAD, D_CONV), F32)],
        scratch_shapes=[pltpu.VMEM((tc + CONV_PAD, D_CONV), F32)],
        compiler_params=_cparams("parallel", "arbitrary"),
        name="conv_prompt",
    )(z, z, state_pad, cw, cb, g, b)


def _conv_sample_kernel(ua_ref, ub_ref, st_ref, cw_ref, cb_ref, g_ref, b_ref, o_ref, a_ref):
    ts = ua_ref.shape[0]
    off = CONV_PAD - (CONV_W - 1)
    a = ua_ref[...] * _sigmoid(ub_ref[...])
    a_ref[...] = a
    st = st_ref[...]
    row = lax.broadcasted_iota(jnp.int32, (CONV_PAD, D_CONV), 0)
    for t in range(ts):
        wt = jnp.zeros((CONV_PAD, D_CONV), F32)
        for j in range(off + t, CONV_PAD):
            wt = jnp.where(row == j, cw_ref[j - off - t:j - off - t + 1, :], wt)
        yc = jnp.sum(st * wt[None], axis=1) + cb_ref[...]
        for t2 in range(t + 1):
            wi = CONV_W - 1 - (t - t2)
            yc = yc + a[t2] * cw_ref[wi:wi + 1, :]
        o_ref[t] = _conv_tail(yc, g_ref, b_ref)


def _conv_sample(zs3, state_pad, cw, cb, g, b, bs_blk):
    ts, bs, _ = zs3.shape
    vec = pl.BlockSpec((1, D_CONV), lambda i: (0, 0))
    return pl.pallas_call(
        _conv_sample_kernel,
        grid=(bs // bs_blk,),
        in_specs=[pl.BlockSpec((ts, bs_blk, D_CONV), lambda i: (0, i, Z_UA // D_CONV)),
                  pl.BlockSpec((ts, bs_blk, D_CONV), lambda i: (0, i, Z_UB // D_CONV)),
                  pl.BlockSpec((bs_blk, CONV_PAD, D_CONV), lambda i: (i, 0, 0)),
                  pl.BlockSpec((CONV_PAD, D_CONV), lambda i: (0, 0)),
                  vec, vec, vec],
        out_specs=[pl.BlockSpec((ts, bs_blk, D_CONV), lambda i: (0, i, 0)),
                   pl.BlockSpec((ts, bs_blk, D_CONV), lambda i: (0, i, 0))],
        out_shape=[jax.ShapeDtypeStruct((ts, bs, D_CONV), BF16),
                   jax.ShapeDtypeStruct((ts, bs, D_CONV), F32)],
        compiler_params=_cparams("parallel"),
        name="conv_sample",
    )(zs3, zs3, state_pad, cw, cb, g, b)


def _t5_bucket(dist):
    max_exact = N_BUCKETS // 2
    d = jnp.maximum(dist, 0)
    large = max_exact + (jnp.log(jnp.maximum(d, 1).astype(F32) / max_exact)
                         / math.log(MAX_DIST / max_exact) * (N_BUCKETS - max_exact)).astype(jnp.int32)
    return jnp.where(d < max_exact, d, jnp.minimum(large, N_BUCKETS - 1))


def _bias_table(rel_bias, dist, valid):
    qn, kn = dist.shape
    bias = rel_bias[_t5_bucket(dist)].astype(F32)
    bias = jnp.where(valid[..., None], bias, NEG_INF)
    return jnp.transpose(bias, (2, 0, 1)).reshape(N_KV, Q_PER_KV * qn, kn)


def _softmax_sink(s, sink):
    mx = jnp.maximum(jnp.max(s, axis=-1, keepdims=True), sink)
    p = jnp.exp(s - mx)
    den = jnp.sum(p, axis=-1, keepdims=True) + jnp.exp(sink - mx)
    return p, den


def _attn_prompt_kernel(q_ref, kc_ref, kp_ref, vc_ref, vp_ref, bias_ref, sink_ref, o_ref):
    first = pl.program_id(1) == 0
    w = q_ref.shape[0]
    q = q_ref[...] * (HEAD_DIM ** -0.5)
    kcol = lax.broadcasted_iota(jnp.int32, (Q_PER_KV * w, 2 * w), 1)
    for g in range(N_KV):
        lo = g * HEAD_DIM
        qs = jnp.concatenate(
            [q[:, (g * Q_PER_KV + r) * HEAD_DIM:(g * Q_PER_KV + r + 1) * HEAD_DIM] for r in range(Q_PER_KV)],
            axis=0).astype(BF16)
        kk = jnp.concatenate([kp_ref[:, lo:lo + HEAD_DIM], kc_ref[:, lo:lo + HEAD_DIM]], axis=0).astype(BF16)
        vv = jnp.concatenate([vp_ref[:, lo:lo + HEAD_DIM], vc_ref[:, lo:lo + HEAD_DIM]], axis=0).astype(BF16)
        s = _dot_nt(qs, kk) + bias_ref[g]
        s = jnp.where(jnp.logical_and(first, kcol < w), NEG_INF, s)
        p, den = _softmax_sink(s, sink_ref[g])
        o = _dot((p / den).astype(BF16), vv)
        for r in range(Q_PER_KV):
            c0 = (g * Q_PER_KV + r) * HEAD_DIM
            o_ref[:, c0:c0 + HEAD_DIM] = o[r * w:(r + 1) * w, :].astype(BF16)


def _attn_prompt(z, bias, sinks, bp, tp):
    w = WINDOW
    nb = tp // w
    kvw = N_KV * HEAD_DIM
    qw = N_HEADS * HEAD_DIM

    def cur(col):
        return lambda bb, i: (bb * nb + i, col)

    def prev(col):
        return lambda bb, i: (bb * nb + jnp.maximum(i - 1, 0), col)

    return pl.pallas_call(
        _attn_prompt_kernel,
        grid=(bp, nb),
        in_specs=[pl.BlockSpec((w, qw), cur(Z_Q // qw)),
                  pl.BlockSpec((w, kvw), cur(Z_K // kvw)),
                  pl.BlockSpec((w, kvw), prev(Z_K // kvw)),
                  pl.BlockSpec((w, kvw), cur(Z_V // kvw)),
                  pl.BlockSpec((w, kvw), prev(Z_V // kvw)),
                  pl.BlockSpec((N_KV, Q_PER_KV * w, 2 * w), lambda bb, i: (0, 0, 0)),
                  pl.BlockSpec((N_KV, Q_PER_KV * w, 1), lambda bb, i: (0, 0, 0))],
        out_specs=pl.BlockSpec((w, qw), lambda bb, i: (bb * nb + i, 0)),
        out_shape=jax.ShapeDtypeStruct((bp * tp, qw), BF16),
        compiler_params=_cparams("parallel", "parallel"),
        name="attn_prompt",
    )(z, z, z, z, z, bias, sinks)


def _attn_sample_kernel(q_ref, kc_ref, vc_ref, kn_ref, vn_ref, bias_ref, sink_ref, o_ref):
    wb = kc_ref.shape[1]
    ts = kn_ref.shape[2]
    for g in range(N_KV):
        lo = g * HEAD_DIM
        q = q_ref[:, g] * (HEAD_DIM ** -0.5)
        kc = kc_ref[:, :, lo:lo + HEAD_DIM].astype(BF16)
        vc = vc_ref[:, :, lo:lo + HEAD_DIM].astype(BF16)
        kn = kn_ref[:, g].astype(BF16).astype(F32)
        vn = vn_ref[:, g].astype(BF16).astype(F32)
        qb = q.astype(BF16)
        bias = bias_ref[g]
        s_c = jnp.einsum("bqd,bkd->bqk", qb, kc, preferred_element_type=F32) + bias[None, :, :wb]
        qf = qb.astype(F32)
        s_n = [jnp.sum(qf * kn[:, j:j + 1, :], axis=-1, keepdims=True) + bias[None, :, wb + j:wb + j + 1]
               for j in range(ts)]
        sink = sink_ref[g][None]
        mx = jnp.maximum(jnp.max(s_c, axis=-1, keepdims=True), sink)
        for sj in s_n:
            mx = jnp.maximum(mx, sj)
        p_c = jnp.exp(s_c - mx)
        p_n = [jnp.exp(sj - mx) for sj in s_n]
        den = jnp.sum(p_c, axis=-1, keepdims=True) + jnp.exp(sink - mx)
        for pj in p_n:
            den = den + pj
        o = jnp.einsum("bqk,bkd->bqd", (p_c / den).astype(BF16), vc, preferred_element_type=F32)
        for j in range(ts):
            o = o + (p_n[j] / den).astype(BF16).astype(F32) * vn[:, j:j + 1, :]
        o_ref[:, g] = o


def _attn_sample(q4, kc, vc, kn, vn, bias, sinks, bs_blk):
    bs, _, rt, _ = q4.shape
    wb = kc.shape[1]
    ts = kn.shape[2]
    kvw = N_KV * HEAD_DIM
    return pl.pallas_call(
        _attn_sample_kernel,
        grid=(bs // bs_blk,),
        in_specs=[pl.BlockSpec((bs_blk, N_KV, rt, HEAD_DIM), lambda i: (i, 0, 0, 0)),
                  pl.BlockSpec((bs_blk, wb, kvw), lambda i: (i, 0, 0)),
                  pl.BlockSpec((bs_blk, wb, kvw), lambda i: (i, 0, 0)),
                  pl.BlockSpec((bs_blk, N_KV, ts, HEAD_DIM), lambda i: (i, 0, 0, 0)),
                  pl.BlockSpec((bs_blk, N_KV, ts, HEAD_DIM), lambda i: (i, 0, 0, 0)),
                  pl.BlockSpec((N_KV, rt, wb + ts), lambda i: (0, 0, 0)),
                  pl.BlockSpec((N_KV, rt, 1), lambda i: (0, 0, 0))],
        out_specs=pl.BlockSpec((bs_blk, N_KV, rt, HEAD_DIM), lambda i: (i, 0, 0, 0)),
        out_shape=jax.ShapeDtypeStruct((bs, N_KV, rt, HEAD_DIM), F32),
        compiler_params=_cparams("parallel"),
        name="attn_sample",
    )(q4, kc, vc, kn, vn, bias, sinks)


def _mlstm_kernel(if_ref, ifr_ref, q_ref, k_ref, v_ref, o_ref, g_ref, c0_ref, n0_ref, m0_ref,
                  h_ref, c1_ref, n1_ref, m1_ref, c_scr, n_scr, m_scr, *, t_valid):
    c = pl.program_id(1)
    L = q_ref.shape[0]

    @pl.when(c == 0)
    def _():
        c_scr[...] = c0_ref[0]
        n_scr[...] = n0_ref[0]
        m_scr[...] = m0_ref[0]

    ifc = if_ref[:, 0:LANES]
    ifr = ifr_ref[0]
    lf_c = _log_sigmoid(ifc)
    lf_r = _log_sigmoid(ifr)
    i_c, i_r = ifc, ifr
    if t_valid < L:
        rc = lax.broadcasted_iota(jnp.int32, (L, LANES), 0) < t_valid
        rr = lax.broadcasted_iota(jnp.int32, (2 * M_HEADS, L), 1) < t_valid
        lf_c = jnp.where(rc, lf_c, 0.0)
        lf_r = jnp.where(rr, lf_r, 0.0)
        i_c = jnp.where(rc, i_c, NEG_INF)
        i_r = jnp.where(rr, i_r, NEG_INF)
    tt = lax.broadcasted_iota(jnp.int32, (L, L), 0)
    ss = lax.broadcasted_iota(jnp.int32, (L, L), 1)
    causal = ss <= tt
    tril = causal.astype(F32)
    triu = (tt <= ss).astype(F32)
    f_c = _dot_hi(tril, lf_c)
    f_r = _dot_hi(lf_r, triu)
    for h in range(M_HEADS):
        lo = h * M_DK
        fc = f_c[:, M_HEADS + h:M_HEADS + h + 1]
        fr = f_r[M_HEADS + h:M_HEADS + h + 1, :]
        ir = i_r[h:h + 1, :]
        ic = i_c[:, h:h + 1]
        m0 = m_scr[h:h + 1, 0:1]
        dm = jnp.where(causal, fc - fr + ir, NEG_INF)
        m_t = jnp.maximum(m0 + fc, jnp.max(dm, axis=-1, keepdims=True))
        inter = jnp.exp(m0 + fc - m_t)
        qf = q_ref[:, lo:lo + M_DK]
        kf = k_ref[:, lo:lo + M_DK] * (M_DK ** -0.5)
        vf = v_ref[:, lo:lo + M_DV]
        qb, kb, vb = qf.astype(BF16), kf.astype(BF16), vf.astype(BF16)
        s = _dot_nt(qb, kb) * jnp.exp(dm - m_t)
        cm = c_scr[h]
        nrow = n_scr[h]
        num = inter * _dot(qb, cm.astype(BF16)) + _dot(s.astype(BF16), vb)
        qn = inter * jnp.sum(qf * nrow, axis=-1, keepdims=True) + jnp.sum(s, axis=-1, keepdims=True)
        hh = num / jnp.maximum(jnp.abs(qn), jnp.exp(-m_t))
        m_end = m_t[L - 1:L, :]
        f_end = fc[L - 1:L, :]
        decay = jnp.exp(m0 + f_end - m_end)
        w_s = jnp.exp(f_end - fc + ic - m_end)
        c_scr[h] = decay * cm + _dot_tn(kb, (w_s * vf).astype(BF16))
        n_scr[h] = decay * nrow + jnp.sum(w_s * kf, axis=0, keepdims=True)
        m_scr[h:h + 1, :] = jnp.broadcast_to(m_end, (1, LANES))
        hn = _norm(hh) * g_ref[:, lo:lo + M_DV]
        h_ref[:, lo:lo + M_DV] = (_sigmoid(o_ref[:, lo:lo + M_DV]) * hn).astype(BF16)

    @pl.when(c == pl.num_programs(1) - 1)
    def _():
        c1_ref[0] = c_scr[...]
        n1_ref[0] = n_scr[...]
        m1_ref[0] = m_scr[...]


def _mlstm(zsrc, ifr, gamma, c0, n0, m0x, nb, nc, L, t_valid, col_if, col_q):
    hw = M_HEADS * M_DK

    def rows(col):
        return lambda b, c: (b * nc + c, col)

    st4 = lambda b, c: (b, 0, 0, 0)
    st3 = lambda b, c: (b, 0, 0)
    kern = functools.partial(_mlstm_kernel, t_valid=t_valid)
    return pl.pallas_call(
        kern,
        grid=(nb, nc),
        in_specs=[pl.BlockSpec((L, 2 * LANES), rows(col_if)),
                  pl.BlockSpec((1, 2 * M_HEADS, L), lambda b, c: (b * nc + c, 0, 0)),
                  pl.BlockSpec((L, hw), rows(col_q)),
                  pl.BlockSpec((L, hw), rows(col_q + 1)),
                  pl.BlockSpec((L, hw), rows(col_q + 2)),
                  pl.BlockSpec((L, hw), rows(col_q + 3)),
                  pl.BlockSpec((1, hw), lambda b, c: (0, 0)),
                  pl.BlockSpec((1, M_HEADS, M_DK, M_DV), st4),
                  pl.BlockSpec((1, M_HEADS, 1, M_DK), st4),
                  pl.BlockSpec((1, 2 * M_HEADS, LANES), st3)],
        out_specs=[pl.BlockSpec((L, hw), lambda b, c: (b * nc + c, 0)),
                   pl.BlockSpec((1, M_HEADS, M_DK, M_DV), st4),
                   pl.BlockSpec((1, M_HEADS, 1, M_DK), st4),
                   pl.BlockSpec((1, 2 * M_HEADS, LANES), st3)],
        out_shape=[jax.ShapeDtypeStruct((nb * nc * L, hw), BF16),
                   jax.ShapeDtypeStruct((nb, M_HEADS, M_DK, M_DV), F32),
                   jax.ShapeDtypeStruct((nb, M_HEADS, 1, M_DK), F32),
                   jax.ShapeDtypeStruct((nb, 2 * M_HEADS, LANES), F32)],
        scratch_shapes=[pltpu.VMEM((M_HEADS, M_DK, M_DV), F32),
                        pltpu.VMEM((M_HEADS, 1, M_DK), F32),
                        pltpu.VMEM((2 * M_HEADS, LANES), F32)],
        compiler_params=_cparams("parallel", "arbitrary"),
        name="mlstm",
    )(zsrc, ifr, zsrc, zsrc, zsrc, zsrc, gamma, c0, n0, m0x)


def _m_state_in(m):
    nb = m.shape[0]
    mx = jnp.zeros((nb, 2 * M_HEADS, LANES), F32)
    return mx.at[:, :M_HEADS, :].set(jnp.broadcast_to(m[:, :, None], (nb, M_HEADS, LANES)))


def _merge_kernel(*refs, np_tiles, alpha, route):
    (cp_ref, cs_ref, ap_ref, as_ref, mp_ref, ms_ref, g0_ref, g1_ref, g2_ref, x_ref,
     gate_ref, sh_ref, sc_ref, wc_ref, wa_ref, wm_ref, wo_ref, pg_ref, pb_ref) = refs[:19]
    if route:
        rw_ref, rb_ref, x1_ref, h_ref, route_ref = refs[19:]
    else:
        x1_ref, h_ref = refs[19:]
    is_s = pl.program_id(0) >= np_tiles

    def pick(p_ref, s_ref):
        return jnp.where(is_s, s_ref[...], p_ref[...])

    y = (_sigmoid(g0_ref[...]) * _dot(pick(cp_ref, cs_ref), wc_ref[...])
         + _sigmoid(g1_ref[...]) * _dot(pick(ap_ref, as_ref), wa_ref[...])
         + _sigmoid(g2_ref[...]) * _dot(pick(mp_ref, ms_ref), wm_ref[...]))
    mix = _dot(y.astype(BF16), wo_ref[...])
    x1 = _norm(alpha * x_ref[...] + gate_ref[0] * mix) * pg_ref[...] + pb_ref[...]
    x1_ref[...] = x1
    h = _norm(x1) * (1.0 + sc_ref[0]) + sh_ref[0]
    h_ref[...] = h.astype(h_ref.dtype)
    if route:
        logits = _dot_hi(h, rw_ref[...]) + rb_ref[...]
        lane = lax.broadcasted_iota(jnp.int32, logits.shape, 1)
        m1 = jnp.max(logits, axis=-1, keepdims=True)
        e1 = jnp.min(jnp.where(logits == m1, lane, LANES), axis=-1, keepdims=True)
        l2 = jnp.where(lane == e1, NEG_INF, logits)
        m2 = jnp.max(l2, axis=-1, keepdims=True)
        e2 = jnp.min(jnp.where(l2 == m2, lane, LANES), axis=-1, keepdims=True)
        ex = jnp.exp(m2 - m1)
        w1 = 1.0 / (1.0 + ex)
        w2 = ex / (1.0 + ex)
        out = jnp.where(lane == 0, e1.astype(F32),
                        jnp.where(lane == 1, e2.astype(F32),
                                  jnp.where(lane == 2, w1, jnp.where(lane == 3, w2, 0.0))))
        route_ref[...] = out


def _merge(cp, cs, ap, as_, mp, ms, z, x, modx, wc, wa, wm, wo, pg, pb, rm, alpha, router):
    n, d = x.shape
    hw = D_CONV
    route = router is not None
    tm = rm.t
    np_tiles = rm.np_tiles

    def pblk():
        return pl.BlockSpec((tm, hw), lambda i: (rm.prompt(i), 0))

    def sblk():
        return pl.BlockSpec((tm, hw), lambda i: (rm.sample(i), 0))

    def zg(k):
        return pl.BlockSpec((tm, d), lambda i: (i, Z_G // d + k))

    def modb(k):
        return pl.BlockSpec((1, tm, d), rm.mod(k))

    def full(a):
        return pl.BlockSpec(a.shape, lambda i: (0,) * a.ndim)

    in_specs = [pblk(), sblk(), pblk(), sblk(), pblk(), sblk(), zg(0), zg(1), zg(2),
                pl.BlockSpec((tm, d), lambda i: (i, 0)),
                modb(2), modb(3), modb(4),
                full(wc), full(wa), full(wm), full(wo), full(pg), full(pb)]
    args = [cp, cs, ap, as_, mp, ms, z, z, z, x, modx, modx, modx, wc, wa, wm, wo, pg, pb]
    out_specs = [pl.BlockSpec((tm, d), lambda i: (i, 0)), pl.BlockSpec((tm, d), lambda i: (i, 0))]
    out_shape = [jax.ShapeDtypeStruct((n, d), F32), jax.ShapeDtypeStruct((n, d), F32 if route else BF16)]
    if route:
        rw, rb = router
        in_specs += [full(rw), full(rb)]
        args += [rw, rb]
        out_specs.append(pl.BlockSpec((tm, LANES), lambda i: (i, 0)))
        out_shape.append(jax.ShapeDtypeStruct((n, LANES), F32))
    kern = functools.partial(_merge_kernel, np_tiles=np_tiles, alpha=alpha, route=route)
    return pl.pallas_call(
        kern, grid=(n // tm,), in_specs=in_specs, out_specs=out_specs, out_shape=out_shape,
        compiler_params=_cparams("parallel"), name="merge_route" if route else "merge",
    )(*args)


def _ffn_kernel(h_ref, wg_ref, wu_ref, wd_ref, x_ref, gate_ref, pg_ref, pb_ref, o_ref, acc, *, alpha):
    j = pl.program_id(1)
    h = h_ref[...]
    a = (_silu(_dot(h, wg_ref[...])) * _dot(h, wu_ref[...])).astype(BF16)
    part = _dot(a, wd_ref[...])

    @pl.when(j == 0)
    def _():
        acc[...] = part

    @pl.when(j > 0)
    def _():
        acc[...] = acc[...] + part

    @pl.when(j == pl.num_programs(1) - 1)
    def _():
        o_ref[...] = _norm(alpha * x_ref[...] + gate_ref[0] * acc[...]) * pg_ref[...] + pb_ref[...]


def _ffn_dense(h, wg, wu, wd, x, modx, pg, pb, rm, tf, alpha):
    n, d = x.shape
    f = wg.shape[1]
    tm = rm.t
    vec = pl.BlockSpec((1, d), lambda i, j: (0, 0))
    return pl.pallas_call(
        functools.partial(_ffn_kernel, alpha=alpha),
        grid=(n // tm, f // tf),
        in_specs=[pl.BlockSpec((tm, d), lambda i, j: (i, 0)),
                  pl.BlockSpec((d, tf), lambda i, j: (0, j)),
                  pl.BlockSpec((d, tf), lambda i, j: (0, j)),
                  pl.BlockSpec((tf, d), lambda i, j: (j, 0)),
                  pl.BlockSpec((tm, d), lambda i, j: (i, 0)),
                  pl.BlockSpec((1, tm, d), rm.mod(5)),
                  vec, vec],
        out_specs=pl.BlockSpec((tm, d), lambda i, j: (i, 0)),
        out_shape=jax.ShapeDtypeStruct((n, d), F32),
        scratch_shapes=[pltpu.VMEM((tm, d), F32)],
        compiler_params=_cparams("parallel", "arbitrary"),
        name="ffn_dense",
    )(h, wg, wu, wd, x, modx, pg, pb)


def _rank_kernel(route_ref, rank_ref, tot_ref, carry):
    i = pl.program_id(0)
    tm = route_ref.shape[0]

    @pl.when(i == 0)
    def _():
        carry[...] = jnp.zeros_like(carry)

    r = route_ref[...]
    lane = lax.broadcasted_iota(jnp.int32, (tm, LANES), 1)
    e1 = r[:, 0:1].astype(jnp.int32)
    e2 = r[:, 1:2].astype(jnp.int32)
    hit1 = lane == e1
    hit2 = lane == e2
    onehot = jnp.where(jnp.logical_or(hit1, hit2), 1.0, 0.0)
    tt = lax.broadcasted_iota(jnp.int32, (tm, tm), 0)
    ss = lax.broadcasted_iota(jnp.int32, (tm, tm), 1)
    before = jnp.where(ss < tt, 1.0, 0.0).astype(BF16)
    cnt = _dot(before, onehot.astype(BF16)) + carry[0:1, :]
    r1 = jnp.sum(jnp.where(hit1, cnt, 0.0), axis=-1, keepdims=True)
    r2 = jnp.sum(jnp.where(hit2, cnt, 0.0), axis=-1, keepdims=True)
    rank_ref[...] = jnp.where(lane == 0, r1, jnp.where(lane == 1, r2, 0.0))
    carry[...] = carry[...] + jnp.sum(onehot, axis=0, keepdims=True)
    tot_ref[...] = carry[...]


def _moe_rank(route, tm):
    n = route.shape[0]
    return pl.pallas_call(
        _rank_kernel,
        grid=(n // tm,),
        in_specs=[pl.BlockSpec((tm, LANES), lambda i: (i, 0))],
        out_specs=[pl.BlockSpec((tm, LANES), lambda i: (i, 0)),
                   pl.BlockSpec((8, LANES), lambda i: (0, 0))],
        out_shape=[jax.ShapeDtypeStruct((n, LANES), F32), jax.ShapeDtypeStruct((8, LANES), F32)],
        scratch_shapes=[pltpu.VMEM((8, LANES), F32)],
        compiler_params=_cparams("arbitrary"),
        name="moe_rank",
    )(route)


def _gather_rows(idx_ref, base, src_hbm, dst, sem, nrows):
    def issue(r, carry):
        pltpu.make_async_copy(src_hbm.at[pl.ds(idx_ref[base + r], 1)], dst.at[pl.ds(r, 1)], sem).start()
        return carry

    lax.fori_loop(0, nrows, issue, 0)
    pltpu.make_async_copy(src_hbm.at[pl.ds(0, nrows)], dst, sem).wait()


def _expert_kernel(blk_e_ref, src_ref, nused_ref, x_hbm, wg_ref, wu_ref, wd_ref, y_ref, xg, xb, acc, sem):
    b = pl.program_id(0)
    j = pl.program_id(1)
    tb = xg.shape[0]
    used = b * tb < nused_ref[0]

    @pl.when(jnp.logical_and(used, j == 0))
    def _():
        _gather_rows(src_ref, b * tb, x_hbm, xg, sem, tb)
        xb[...] = xg[...].astype(BF16)

    @pl.when(used)
    def _():
        x = xb[...]
        a = (_silu(_dot(x, wg_ref[0])) * _dot(x, wu_ref[0])).astype(BF16)
        part = _dot(a, wd_ref[0])

        @pl.when(j == 0)
        def _():
            acc[...] = part

        @pl.when(j > 0)
        def _():
            acc[...] = acc[...] + part

    @pl.when(j == pl.num_programs(1) - 1)
    def _():
        y_ref[...] = jnp.where(used, acc[...], 0.0)


def _moe_experts(blk_e, src_tok, nused, x, wg, wu, wd, tb, tf):
    n_rows = src_tok.shape[0]
    d = x.shape[1]
    f = wg.shape[2]
    grid_spec = pltpu.PrefetchScalarGridSpec(
        num_scalar_prefetch=3,
        grid=(n_rows // tb, f // tf),
        in_specs=[pl.BlockSpec(memory_space=pl.ANY),
                  pl.BlockSpec((1, d, tf), lambda b, j, be, st, nu: (be[b], 0, j)),
                  pl.BlockSpec((1, d, tf), lambda b, j, be, st, nu: (be[b], 0, j)),
                  pl.BlockSpec((1, tf, d), lambda b, j, be, st, nu: (be[b], j, 0))],
        out_specs=pl.BlockSpec((tb, d), lambda b, j, be, st, nu: (b, 0)),
        scratch_shapes=[pltpu.VMEM((tb, d), F32), pltpu.VMEM((tb, d), BF16), pltpu.VMEM((tb, d), F32),
                        pltpu.SemaphoreType.DMA(())],
    )
    return pl.pallas_call(
        _expert_kernel, grid_spec=grid_spec,
        out_shape=jax.ShapeDtypeStruct((n_rows, d), F32),
        compiler_params=_cparams("arbitrary", "arbitrary"),
        name="moe_experts",
    )(blk_e, src_tok, nused, x, wg, wu, wd)


def _combine_kernel(d1_ref, d2_ref, ys_hbm, route_ref, x_ref, gate_ref, pg_ref, pb_ref, o_ref, y1, y2, sem1, sem2,
                    *, alpha):
    i = pl.program_id(0)
    tm = x_ref.shape[0]
    _gather_rows(d1_ref, i * tm, ys_hbm, y1, sem1, tm)
    _gather_rows(d2_ref, i * tm, ys_hbm, y2, sem2, tm)
    r = route_ref[...]
    f = y1[...] * r[:, 2:3] + y2[...] * r[:, 3:4]
    o_ref[...] = _norm(alpha * x_ref[...] + gate_ref[0] * f) * pg_ref[...] + pb_ref[...]


def _moe_combine(dest1, dest2, ys, route, x, modx, pg, pb, rm, alpha):
    n, d = x.shape
    tm = rm.t
    vec = pl.BlockSpec((1, d), lambda i, a, b: (0, 0))
    grid_spec = pltpu.PrefetchScalarGridSpec(
        num_scalar_prefetch=2,
        grid=(n // tm,),
        in_specs=[pl.BlockSpec(memory_space=pl.ANY),
                  pl.BlockSpec((tm, LANES), lambda i, a, b: (i, 0)),
                  pl.BlockSpec((tm, d), lambda i, a, b: (i, 0)),
                  pl.BlockSpec((1, tm, d), rm.mod(5)),
                  vec, vec],
        out_specs=pl.BlockSpec((tm, d), lambda i, a, b: (i, 0)),
        scratch_shapes=[pltpu.VMEM((tm, d), F32), pltpu.VMEM((tm, d), F32),
                        pltpu.SemaphoreType.DMA(()), pltpu.SemaphoreType.DMA(())],
    )
    return pl.pallas_call(
        functools.partial(_combine_kernel, alpha=alpha), grid_spec=grid_spec,
        out_shape=jax.ShapeDtypeStruct((n, d), F32),
        compiler_params=_cparams("arbitrary"),
        name="moe_combine",
    )(dest1, dest2, ys, route, x, modx, pg, pb)


def _moe_ffn(h, route, wg, wu, wd, x, modx, pg, pb, rm, tb, tf, alpha):
    n = h.shape[0]
    rank, tot = _moe_rank(route, rm.t)
    counts = tot[0, :N_EXPERTS].astype(jnp.int32)
    padded = (counts + tb - 1) // tb * tb
    pad_end = jnp.cumsum(padded)
    pad_start = pad_end - padded
    e1 = route[:, 0].astype(jnp.int32)
    e2 = route[:, 1].astype(jnp.int32)
    dest1 = pad_start[e1] + rank[:, 0].astype(jnp.int32)
    dest2 = pad_start[e2] + rank[:, 1].astype(jnp.int32)
    n_blocks = -(-(2 * n + N_EXPERTS * (tb - 1)) // tb)
    blk_start = jnp.arange(n_blocks, dtype=jnp.int32) * tb
    blk_e = jnp.minimum(jnp.sum(pad_end[None, :] <= blk_start[:, None], axis=1), N_EXPERTS - 1).astype(jnp.int32)
    tok = jnp.arange(n, dtype=jnp.int32)
    src_tok = jnp.zeros((n_blocks * tb,), jnp.int32).at[dest1].set(tok).at[dest2].set(tok)
    nused = pad_end[-1:].astype(jnp.int32)
    ys = _moe_experts(blk_e, src_tok, nused, h, wg, wu, wd, tb, tf)
    return _moe_combine(dest1, dest2, ys, route, x, modx, pg, pb, rm, alpha)


def _pack_w_in(w_in, b_in):
    depth, d, _ = w_in.shape
    a_end = 2 * D_CONV + (N_HEADS + 2 * N_KV) * HEAD_DIM
    m_end = a_end + 4 * M_HEADS * M_DK
    if_end = m_end + 2 * M_HEADS

    def pack(a, lead):
        pad = jnp.zeros(lead + (Z_MQ - Z_IF - 2 * M_HEADS,), a.dtype)
        return jnp.concatenate([a[..., :a_end], a[..., m_end:if_end], pad, a[..., a_end:m_end], a[..., if_end:]],
                               axis=-1)

    return pack(w_in, (depth, d)).astype(BF16), pack(b_in, (depth,)).reshape(depth, 1, Z_W)


def kernel(x_prompt, x_sample, cache_swa_k, cache_swa_v, state_conv, state_mlstm_C, state_mlstm_n, state_mlstm_m, c_prompt, c_sample, w_ada, b_ada, w_in, b_in, conv_w, conv_b, conv_ln_g, conv_ln_b, w_conv_out, attn_sinks, rel_bias, w_attn_out, m_norm_g, w_m_out, w_out, post_ln_g, post_ln_b, ffn_w_gate, ffn_w_up, ffn_w_down, router_w, router_b, moe_w_gate, moe_w_up, moe_w_down):
    bp, tp, d = x_prompt.shape
    bs, ts, _ = x_sample.shape
    depth = w_ada.shape[0]
    alpha = (2 * depth) ** 0.25
    n_p, n_s = bp * tp, bs * ts
    n = n_p + n_s
    tm = n_s
    assert d == D_MODEL and tp % tm == 0 and tp % WINDOW == 0 and tm % 8 == 0
    rm = _RowMap(tm, bp, tp, n_p)
    rm_half = _RowMap(tm // 2, bp, tp, n_p)
    wb = cache_swa_k.shape[2]
    big = n_p >= 4096
    tc = 512 if big else tm
    lm = 256 if big else min(tp, 128)
    tb = 512 if big else 64
    bs_blk = 32 if bs % 32 == 0 else bs
    bs_att = 16 if bs % 16 == 0 else bs

    x = jnp.concatenate([x_prompt.reshape(n_p, d), jnp.transpose(x_sample, (1, 0, 2)).reshape(n_s, d)], axis=0)

    nc_rows = -(-(bp + bs) // 8) * 8
    c_all = jnp.zeros((nc_rows, d), F32).at[:bp].set(c_prompt).at[bp:bp + bs].set(c_sample)
    mod = _ada_mod(c_all, w_ada, b_ada)

    w_in_p, b_in_p = _pack_w_in(w_in, b_in)
    wc_b, wa_b, wm_b, wo_b = (w.astype(BF16) for w in (w_conv_out, w_attn_out, w_m_out, w_out))
    fg_b, fu_b, fd_b = (w.astype(BF16) for w in (ffn_w_gate, ffn_w_up, ffn_w_down))
    mg_b, mu_b, md_b = (w.astype(BF16) for w in (moe_w_gate, moe_w_up, moe_w_down))
    cw_pad = jnp.pad(conv_w, ((0, 0), (0, CONV_PAD - CONV_W), (0, 0)))
    rw_pad = jnp.pad(router_w, ((0, 0), (0, 0), (0, LANES - N_EXPERTS)))
    rb_pad = jnp.pad(router_b, ((0, 0), (0, LANES - N_EXPERTS)), constant_values=NEG_INF)

    qi = jnp.arange(WINDOW)[:, None]
    kj = jnp.arange(2 * WINDOW)[None, :]
    dist_p = qi + WINDOW - kj
    bias_p = _bias_table(rel_bias, dist_p, (dist_p >= 0) & (dist_p < WINDOW))
    dist_s = jnp.arange(ts)[:, None] + wb - jnp.arange(wb + ts)[None, :]
    bias_s = _bias_table(rel_bias, dist_s, (dist_s >= 0) & (dist_s < WINDOW))

    f_dense = ffn_w_gate.shape[2]
    tf_dense = f_dense // 2 if (f_dense // 2) % LANES == 0 else f_dense
    f_moe = moe_w_gate.shape[3]
    tf_moe = 512 if f_moe % 512 == 0 else f_moe

    lts = 16
    new_p = [[] for _ in range(6)]
    new_s = [[] for _ in range(6)]
    for l in range(depth):
        j = l // 2
        mp_, ms_ = mod[l, :bp], mod[l, bp:bp + bs]
        modx = jnp.concatenate([jnp.broadcast_to(mp_[:, None, :], (bp, tm, 6 * d)),
                                jnp.tile(ms_, (ts, 1))[None]], axis=0)

        z = _ln_proj(x, modx, w_in_p[l], b_in_p[l], rm)
        zs3 = z[n_p:].reshape(ts, bs, Z_W)

        cvec = [v[l].reshape(1, D_CONV) for v in (conv_b, conv_ln_g, conv_ln_b)]
        cp, ns_p = _conv_prompt(z, jnp.zeros((bp, CONV_PAD, D_CONV), F32), cw_pad[l], *cvec, bp, tp, tc)
        st_pad = jnp.pad(state_conv[l], ((0, 0), (CONV_PAD - CONV_W + 1, 0), (0, 0)))
        cs3, a_s3 = _conv_sample(zs3, st_pad, cw_pad[l], *cvec, bs_blk)
        new_p[2].append(ns_p[:, CONV_PAD - CONV_W + 1:])
        new_s[2].append(jnp.concatenate([state_conv[l][:, ts:], jnp.transpose(a_s3, (1, 0, 2))], axis=1))

        sink_h = attn_sinks[l].astype(F32).reshape(N_KV, Q_PER_KV, 1)
        sink_p = jnp.broadcast_to(sink_h, (N_KV, Q_PER_KV, WINDOW)).reshape(N_KV, Q_PER_KV * WINDOW, 1)
        sink_s = jnp.broadcast_to(sink_h, (N_KV, Q_PER_KV, ts)).reshape(N_KV, Q_PER_KV * ts, 1)
        ap = _attn_prompt(z, bias_p, sink_p, bp, tp)
        k_p = z[:n_p, Z_K:Z_K + N_KV * HEAD_DIM].reshape(bp, tp, N_KV, HEAD_DIM)
        v_p = z[:n_p, Z_V:Z_V + N_KV * HEAD_DIM].reshape(bp, tp, N_KV, HEAD_DIM)
        new_p[0].append(k_p[:, tp - min(WINDOW, tp):])
        new_p[1].append(v_p[:, tp - min(WINDOW, tp):])
        q_s = zs3[:, :, Z_Q:Z_Q + N_HEADS * HEAD_DIM].reshape(ts, bs, N_KV, Q_PER_KV, HEAD_DIM)
        q4 = jnp.transpose(q_s, (1, 2, 3, 0, 4)).reshape(bs, N_KV, Q_PER_KV * ts, HEAD_DIM)
        k_s = jnp.transpose(zs3[:, :, Z_K:Z_K + N_KV * HEAD_DIM].reshape(ts, bs, N_KV, HEAD_DIM), (1, 0, 2, 3))
        v_s = jnp.transpose(zs3[:, :, Z_V:Z_V + N_KV * HEAD_DIM].reshape(ts, bs, N_KV, HEAD_DIM), (1, 0, 2, 3))
        kc, vc = cache_swa_k[l], cache_swa_v[l]
        o4 = _attn_sample(q4, kc.reshape(bs, wb, N_KV * HEAD_DIM), vc.reshape(bs, wb, N_KV * HEAD_DIM),
                          jnp.transpose(k_s, (0, 2, 1, 3)), jnp.transpose(v_s, (0, 2, 1, 3)),
                          bias_s, sink_s, bs_att)
        as_ = jnp.transpose(o4.reshape(bs, N_KV, Q_PER_KV, ts, HEAD_DIM), (3, 0, 1, 2, 4)).reshape(n_s, -1).astype(BF16)
        new_s[0].append(jnp.concatenate([kc, k_s], axis=1)[:, ts:])
        new_s[1].append(jnp.concatenate([vc, v_s], axis=1)[:, ts:])

        gamma = m_norm_g[l].reshape(1, M_HEADS * M_DV)
        ncp = tp // lm
        if_p = z[:n_p, Z_IF:Z_IF + 2 * M_HEADS].reshape(bp * ncp, lm, 2 * M_HEADS)
        zero = lambda *s: jnp.zeros(s, F32)
        mp, c1p, n1p, m1p = _mlstm(z, jnp.transpose(if_p, (0, 2, 1)), gamma,
                                   zero(bp, M_HEADS, M_DK, M_DV), zero(bp, M_HEADS, 1, M_DK),
                                   zero(bp, 2 * M_HEADS, LANES), bp, ncp, lm, lm,
                                   Z_IF // (2 * LANES), Z_MQ // (M_HEADS * M_DK))
        new_p[3].append(c1p)
        new_p[4].append(n1p[:, :, 0])
        new_p[5].append(m1p[:, :M_HEADS, 0])
        zm = jnp.transpose(zs3[:, :, Z_K:Z_G], (1, 0, 2))
        zm = jnp.pad(zm, ((0, 0), (0, lts - ts), (0, 0))).reshape(bs * lts, Z_G - Z_K)
        if_s = zm[:, Z_IF - Z_K:Z_IF - Z_K + 2 * M_HEADS].reshape(bs, lts, 2 * M_HEADS)
        ms, c1s, n1s, m1s = _mlstm(zm, jnp.transpose(if_s, (0, 2, 1)), gamma,
                                   state_mlstm_C[l], state_mlstm_n[l][:, :, None, :],
                                   _m_state_in(state_mlstm_m[l]), bs, 1, lts, ts,
                                   (Z_IF - Z_K) // (2 * LANES), (Z_MQ - Z_K) // (M_HEADS * M_DK))
        ms = jnp.transpose(ms.reshape(bs, lts, -1)[:, :ts], (1, 0, 2)).reshape(n_s, -1)
        new_s[3].append(c1s)
        new_s[4].append(n1s[:, :, 0])
        new_s[5].append(m1s[:, :M_HEADS, 0])

        pg = post_ln_g[l].reshape(2, 1, d)
        pb = post_ln_b[l].reshape(2, 1, d)
        moe = l % 2 == 1
        router = (rw_pad[j], rb_pad[j].reshape(1, LANES)) if moe else None
        outs = _merge(cp, cs3.reshape(n_s, D_CONV), ap, as_, mp, ms, z, x, modx,
                      wc_b[l], wa_b[l], wm_b[l], wo_b[l], pg[0], pb[0], rm_half, alpha, router)
        if moe:
            x1, h2, route = outs
            x = _moe_ffn(h2, route, mg_b[j], mu_b[j], md_b[j], x1, modx, pg[1], pb[1], rm, tb, tf_moe, alpha)
        else:
            x1, h2 = outs
            x = _ffn_dense(h2, fg_b[j], fu_b[j], fd_b[j], x1, modx, pg[1], pb[1], rm, tf_dense, alpha)

    y_p = x[:n_p].reshape(bp, tp, d)
    y_s = jnp.transpose(x[n_p:].reshape(ts, bs, d), (1, 0, 2))
    p_k, p_v, p_conv, p_c, p_n, p_m = [jnp.stack(a) for a in new_p]
    s_k, s_v, s_conv, s_c, s_n, s_m = [jnp.stack(a) for a in new_s]
    return (y_p, y_s, p_k, p_v, p_conv, p_c, p_n, p_m, s_k, s_v, s_conv, s_c, s_n, s_m)
```

```python
import functools
import math

import jax
import jax.numpy as jnp
from jax import lax
from jax.experimental import pallas as pl
from jax.experimental.pallas import tpu as pltpu

F32 = jnp.float32
BF16 = jnp.bfloat16

D_MODEL = 1024
D_CONV = 512
CONV_W = 31
CONV_PAD = 32
N_HEADS = 8
N_KV = 2
HEAD_DIM = 64
Q_PER_KV = N_HEADS // N_KV
WINDOW = 128
N_BUCKETS = 32
MAX_DIST = 128
M_HEADS = 4
M_DK = 128
M_DV = 128
N_EXPERTS = 8
LN_EPS = 1e-5
LANES = 128
NEG_INF = float("-inf")
VMEM_LIMIT = 56 * 1024 * 1024

Z_G, Z_UA, Z_UB, Z_Q = 0, 3072, 3584, 4096
Z_MQ, Z_MK, Z_MV, Z_MO = 4608, 5120, 5632, 6144
Z_K, Z_V, Z_W = 6656, 6784, 6912
TN_IN = 2304
ATT_QB = 4
MOE_SUB = 256


def _cparams(*sem):
    return pltpu.CompilerParams(dimension_semantics=sem, vmem_limit_bytes=VMEM_LIMIT)


def _sigmoid(x):
    return 1.0 / (1.0 + jnp.exp(-x))


def _silu(x):
    return x * _sigmoid(x)


def _log_sigmoid(x):
    return jnp.minimum(x, 0.0) - jnp.log(1.0 + jnp.exp(-jnp.abs(x)))


def _norm(x):
    mu = jnp.mean(x, axis=-1, keepdims=True)
    xc = x - mu
    var = jnp.mean(xc * xc, axis=-1, keepdims=True)
    return xc * lax.rsqrt(var + LN_EPS)


def _dot(a, b):
    return jnp.dot(a, b, preferred_element_type=F32)


def _dot_nt(a, b):
    return lax.dot_general(a, b, (((1,), (1,)), ((), ())), preferred_element_type=F32)


def _dot_tn(a, b):
    return lax.dot_general(a, b, (((0,), (0,)), ((), ())), preferred_element_type=F32)


def _dot_hi(a, b):
    return jnp.dot(a, b, preferred_element_type=F32, precision=lax.Precision.HIGHEST)


def _split_bf16(a):
    hi = a.astype(BF16)
    return hi, (a - hi.astype(F32)).astype(BF16)


class _RowMap:
    def __init__(self, t, bp, tp, n_p):
        self.t = t
        self.bp = bp
        self.per_seq = tp // t
        self.np_tiles = n_p // t

    def seq(self, i):
        return jnp.minimum(i // self.per_seq, self.bp - 1)

    def prompt(self, i):
        return jnp.minimum(i, self.np_tiles - 1)

    def sample(self, i):
        return jnp.maximum(i - self.np_tiles, 0)

    def mod_specs(self, l, k, d):
        return (pl.BlockSpec((None, None, 1, d), lambda i, *_: (l, self.seq(i), 0, k)),
                pl.BlockSpec((None, self.t, d), lambda i, *_: (l, self.sample(i), k)))


def _pick(is_s, p_ref, s_ref):
    return jnp.where(is_s, s_ref[...], p_ref[...])


def _ada_kernel(c_ref, w_ref, b_ref, o_ref):
    s = _silu(c_ref[...]).astype(BF16)
    o_ref[0] = _dot(s, w_ref[0].astype(BF16)) + b_ref[0]


def _ada_mod(c_all, w_ada, b_ada):
    depth, d, n6 = w_ada.shape
    rows = c_all.shape[0]
    return pl.pallas_call(
        _ada_kernel,
        grid=(depth, n6 // d),
        in_specs=[pl.BlockSpec((rows, d), lambda l, j: (0, 0)),
                  pl.BlockSpec((1, d, d), lambda l, j: (l, 0, j)),
                  pl.BlockSpec((1, 1, d), lambda l, j: (l, 0, j))],
        out_specs=pl.BlockSpec((1, rows, d), lambda l, j: (l, 0, j)),
        out_shape=jax.ShapeDtypeStruct((depth, rows, n6), F32),
        compiler_params=_cparams("parallel", "parallel"),
        name="ada_mod",
    )(c_all, w_ada, b_ada.reshape(depth, 1, n6))


def _ln_proj_kernel(x_ref, shp_ref, shs_ref, scp_ref, scs_ref, w_ref, b_ref, wif_ref, bif_ref,
                    z_ref, zif_ref, h_scr, *, np_tiles):
    is_s = pl.program_id(0) >= np_tiles

    @pl.when(pl.program_id(1) == 0)
    def _():
        h = _norm(x_ref[...]) * (1.0 + _pick(is_s, scp_ref, scs_ref)) + _pick(is_s, shp_ref, shs_ref)
        h = h.astype(BF16)
        h_scr[...] = h
        zif_ref[...] = _dot(h, wif_ref[...]) + bif_ref[...]

    z_ref[...] = (_dot(h_scr[...], w_ref[...]) + b_ref[...]).astype(BF16)


def _ln_proj(x, mod_p, mod_s, w, b, wif, bif, l, rm):
    n, d = x.shape
    zw = w.shape[2]
    tm = rm.t
    shp, shs = rm.mod_specs(l, 0, d)
    scp, scs = rm.mod_specs(l, 1, d)
    return pl.pallas_call(
        functools.partial(_ln_proj_kernel, np_tiles=rm.np_tiles),
        grid=(n // tm, zw // TN_IN),
        in_specs=[pl.BlockSpec((tm, d), lambda i, j: (i, 0)),
                  shp, shs, scp, scs,
                  pl.BlockSpec((None, d, TN_IN), lambda i, j: (l, 0, j)),
                  pl.BlockSpec((None, 1, TN_IN), lambda i, j: (l, 0, j)),
                  pl.BlockSpec((None, d, LANES), lambda i, j: (l, 0, 0)),
                  pl.BlockSpec((None, 1, LANES), lambda i, j: (l, 0, 0))],
        out_specs=[pl.BlockSpec((tm, TN_IN), lambda i, j: (i, j)),
                   pl.BlockSpec((tm, LANES), lambda i, j: (i, 0))],
        out_shape=[jax.ShapeDtypeStruct((n, zw), BF16), jax.ShapeDtypeStruct((n, LANES), F32)],
        scratch_shapes=[pltpu.VMEM((tm, d), BF16)],
        compiler_params=_cparams("parallel", "arbitrary"),
        name="ln_proj",
    )(x, mod_p, mod_s, mod_p, mod_s, w, b, wif, bif)


def _conv_tail(yc, g_ref, b_ref):
    y = _norm(yc) * g_ref[...] + b_ref[...]
    return _silu(y).astype(BF16)


def _conv_prompt_kernel(ua_ref, ub_ref, st_ref, cw_ref, cb_ref, g_ref, b_ref, o_ref, ns_ref, ext):
    t = pl.program_id(1)
    tc = ua_ref.shape[0]

    @pl.when(t == 0)
    def _():
        ext[0:CONV_PAD, :] = st_ref[0]

    @pl.when(t > 0)
    def _():
        ext[0:CONV_PAD, :] = ext[tc:tc + CONV_PAD, :]

    ext[CONV_PAD:, :] = ua_ref[...].astype(F32) * _sigmoid(ub_ref[...].astype(F32))
    off = CONV_PAD - (CONV_W - 1)
    acc = jnp.zeros((tc, D_CONV), F32) + cb_ref[...]
    for w in range(CONV_W):
        acc = acc + ext[off + w:off + w + tc, :] * cw_ref[w:w + 1, :]
    o_ref[...] = _conv_tail(acc, g_ref, b_ref)

    @pl.when(t == pl.num_programs(1) - 1)
    def _():
        ns_ref[0] = ext[tc:tc + CONV_PAD, :]


def _conv_vec_specs(l, nargs):
    return [pl.BlockSpec((None, 1, D_CONV), lambda *_: (l, 0, 0)) for _ in range(nargs)]


def _conv_prompt(z, state_pad, cw, cb, g, b, l, bp, tp, tc):
    nt = tp // tc
    return pl.pallas_call(
        _conv_prompt_kernel,
        grid=(bp, nt),
        in_specs=[pl.BlockSpec((tc, D_CONV), lambda bb, t: (bb * nt + t, Z_UA // D_CONV)),
                  pl.BlockSpec((tc, D_CONV), lambda bb, t: (bb * nt + t, Z_UB // D_CONV)),
                  pl.BlockSpec((1, CONV_PAD, D_CONV), lambda bb, t: (bb, 0, 0)),
                  pl.BlockSpec((None, CONV_PAD, D_CONV), lambda bb, t: (l, 0, 0))] + _conv_vec_specs(l, 3),
        out_specs=[pl.BlockSpec((tc, D_CONV), lambda bb, t: (bb * nt + t, 0)),
                   pl.BlockSpec((1, CONV_PAD, D_CONV), lambda bb, t: (bb, 0, 0))],
        out_shape=[jax.ShapeDtypeStruct((bp * tp, D_CONV), BF16),
                   jax.ShapeDtypeStruct((bp, CONV_PAD, D_CONV), F32)],
        scratch_shapes=[pltpu.VMEM((tc + CONV_PAD, D_CONV), F32)],
        compiler_params=_cparams("parallel", "arbitrary"),
        name="conv_prompt",
    )(z, z, state_pad, cw, cb, g, b)


def _conv_sample_kernel(ua_ref, ub_ref, st_ref, cw_ref, cb_ref, g_ref, b_ref, o_ref, a_ref):
    ts = ua_ref.shape[0]
    ns = CONV_W - 1
    a = ua_ref[...].astype(F32) * _sigmoid(ub_ref[...].astype(F32))
    a_ref[...] = a
    st = st_ref[...]
    row = lax.broadcasted_iota(jnp.int32, (ns, D_CONV), 0)
    for t in range(ts):
        wt = jnp.zeros((ns, D_CONV), F32)
        for j in range(t, ns):
            wt = jnp.where(row == j, cw_ref[j - t:j - t + 1, :], wt)
        yc = jnp.sum(st * wt[None], axis=1) + cb_ref[...]
        for t2 in range(t + 1):
            wi = CONV_W - 1 - (t - t2)
            yc = yc + a[t2] * cw_ref[wi:wi + 1, :]
        o_ref[t] = _conv_tail(yc, g_ref, b_ref)


def _conv_sample(zs3, state, cw, cb, g, b, l, bs_blk):
    ts, bs, _ = zs3.shape
    ns = CONV_W - 1
    return pl.pallas_call(
        _conv_sample_kernel,
        grid=(bs // bs_blk,),
        in_specs=[pl.BlockSpec((ts, bs_blk, D_CONV), lambda i: (0, i, Z_UA // D_CONV)),
                  pl.BlockSpec((ts, bs_blk, D_CONV), lambda i: (0, i, Z_UB // D_CONV)),
                  pl.BlockSpec((None, bs_blk, ns, D_CONV), lambda i: (l, i, 0, 0)),
                  pl.BlockSpec((None, CONV_PAD, D_CONV), lambda i: (l, 0, 0))] + _conv_vec_specs(l, 3),
        out_specs=[pl.BlockSpec((ts, bs_blk, D_CONV), lambda i: (0, i, 0)),
                   pl.BlockSpec((ts, bs_blk, D_CONV), lambda i: (0, i, 0))],
        out_shape=[jax.ShapeDtypeStruct((ts, bs, D_CONV), BF16),
                   jax.ShapeDtypeStruct((ts, bs, D_CONV), F32)],
        compiler_params=_cparams("parallel"),
        name="conv_sample",
    )(zs3, zs3, state, cw, cb, g, b)


def _t5_bucket(dist):
    max_exact = N_BUCKETS // 2
    d = jnp.maximum(dist, 0)
    large = max_exact + (jnp.log(jnp.maximum(d, 1).astype(F32) / max_exact)
                         / math.log(MAX_DIST / max_exact) * (N_BUCKETS - max_exact)).astype(jnp.int32)
    return jnp.where(d < max_exact, d, jnp.minimum(large, N_BUCKETS - 1))


def _bias_heads(rel_bias, dist, valid):
    bias = rel_bias[_t5_bucket(dist)].astype(F32)
    bias = jnp.where(valid[..., None], bias, NEG_INF)
    return jnp.transpose(bias, (2, 0, 1))


def _attn_prompt_kernel(sink_ref, q_ref, kc_ref, kp_ref, vc_ref, vp_ref, bias_ref, o_ref, *, l):
    first = pl.program_id(1) == 0
    w = WINDOW
    nq = q_ref.shape[0] // w
    kall = jnp.concatenate([kp_ref[...], kc_ref[...]], axis=0).astype(F32)
    vall = jnp.concatenate([vp_ref[...], vc_ref[...]], axis=0).astype(F32)
    lane = lax.broadcasted_iota(jnp.int32, kall.shape, 1)
    lo = lane < HEAD_DIM
    kroll = pltpu.roll(kall, HEAD_DIM, 1)
    vroll = pltpu.roll(vall, HEAD_DIM, 1)

    def halves(a, aroll, g):
        if g == 0:
            return jnp.where(lo, a, 0.0).astype(BF16), jnp.where(lo, 0.0, aroll).astype(BF16)
        return jnp.where(lo, aroll, 0.0).astype(BF16), jnp.where(lo, 0.0, a).astype(BF16)

    kh = [halves(kall, kroll, g) for g in range(N_KV)]
    vh = [halves(vall, vroll, g) for g in range(N_KV)]
    col = lax.broadcasted_iota(jnp.int32, (w, 4 * w), 1)
    prev_col = (col % (2 * w)) < w
    tiles_per_g = Q_PER_KV // 2
    for qi in range(nq):
        r0 = qi * w
        for tile in range(N_HEADS // 2):
            g = tile // tiles_per_g
            q = q_ref[r0:r0 + w, tile * LANES:(tile + 1) * LANES]
            kk = jnp.concatenate([kh[g][0][r0:r0 + 2 * w], kh[g][1][r0:r0 + 2 * w]], axis=0)
            vv = jnp.concatenate([vh[g][0][r0:r0 + 2 * w], vh[g][1][r0:r0 + 2 * w]], axis=0)
            s = _dot_nt(q, kk) * (HEAD_DIM ** -0.5) + bias_ref[tile]
            if qi == 0:
                s = jnp.where(jnp.logical_and(first, prev_col), NEG_INF, s)
            ps = []
            for half in range(2):
                sh = s[:, half * 2 * w:(half + 1) * 2 * w]
                sink = sink_ref[l, 2 * tile + half]
                mx = jnp.maximum(jnp.max(sh, axis=-1, keepdims=True), sink)
                p = jnp.exp(sh - mx)
                den = jnp.sum(p, axis=-1, keepdims=True) + jnp.exp(sink - mx)
                ps.append((p * (1.0 / den)).astype(BF16))
            o = _dot(jnp.concatenate(ps, axis=1), vv)
            o_ref[r0:r0 + w, tile * LANES:(tile + 1) * LANES] = o.astype(BF16)


def _attn_prompt(z, bias, sinks, l, bp, tp):
    w = WINDOW
    qb = ATT_QB if tp % (ATT_QB * w) == 0 else 1
    ns = tp // (qb * w)
    nb = tp // w
    kvw = N_KV * HEAD_DIM
    qw = N_HEADS * HEAD_DIM

    def cur(col):
        return lambda bb, i: (bb * ns + i, col)

    def prev(col):
        return lambda bb, i: (bb * nb + jnp.maximum(i * qb - 1, 0), col)

    return pl.pallas_call(
        functools.partial(_attn_prompt_kernel, l=l),
        grid=(bp, ns),
        in_specs=[pl.BlockSpec(memory_space=pltpu.SMEM),
                  pl.BlockSpec((qb * w, qw), cur(Z_Q // qw)),
                  pl.BlockSpec((qb * w, kvw), cur(Z_K // kvw)),
                  pl.BlockSpec((w, kvw), prev(Z_K // kvw)),
                  pl.BlockSpec((qb * w, kvw), cur(Z_V // kvw)),
                  pl.BlockSpec((w, kvw), prev(Z_V // kvw)),
                  pl.BlockSpec((N_HEADS // 2, w, 4 * w), lambda bb, i: (0, 0, 0))],
        out_specs=pl.BlockSpec((qb * w, qw), lambda bb, i: (bb * ns + i, 0)),
        out_shape=jax.ShapeDtypeStruct((bp * tp, qw), BF16),
        compiler_params=_cparams("parallel", "parallel"),
        name="attn_prompt",
    )(sinks, z, z, z, z, z, bias)


def _attn_sample_kernel(q_ref, kc_ref, vc_ref, kn_ref, vn_ref, bias_ref, sink_ref, o_ref):
    wb = kc_ref.shape[1]
    ts = kn_ref.shape[2]
    for g in range(N_KV):
        lo = g * HEAD_DIM
        qb = (q_ref[:, g].astype(F32) * (HEAD_DIM ** -0.5)).astype(BF16)
        kc = kc_ref[:, :, lo:lo + HEAD_DIM].astype(BF16)
        vc = vc_ref[:, :, lo:lo + HEAD_DIM].astype(BF16)
        kn = kn_ref[:, g].astype(F32)
        vn = vn_ref[:, g].astype(F32)
        bias = bias_ref[g]
        s_c = jnp.einsum("bqd,bkd->bqk", qb, kc, preferred_element_type=F32) + bias[None, :, :wb]
        qf = qb.astype(F32)
        s_n = [jnp.sum(qf * kn[:, j:j + 1, :], axis=-1, keepdims=True) + bias[None, :, wb + j:wb + j + 1]
               for j in range(ts)]
        sink = sink_ref[g][None]
        mx = jnp.maximum(jnp.max(s_c, axis=-1, keepdims=True), sink)
        for sj in s_n:
            mx = jnp.maximum(mx, sj)
        p_c = jnp.exp(s_c - mx)
        p_n = [jnp.exp(sj - mx) for sj in s_n]
        den = jnp.sum(p_c, axis=-1, keepdims=True) + jnp.exp(sink - mx)
        for pj in p_n:
            den = den + pj
        o = jnp.einsum("bqk,bkd->bqd", (p_c / den).astype(BF16), vc, preferred_element_type=F32)
        for j in range(ts):
            o = o + (p_n[j] / den).astype(BF16).astype(F32) * vn[:, j:j + 1, :]
        o_ref[:, g] = o


def _attn_sample(q4, kc, vc, kn, vn, bias, sinks, l, bs_blk):
    bs, _, rt, _ = q4.shape
    wb = kc.shape[2]
    ts = kn.shape[2]
    kvw = N_KV * HEAD_DIM
    return pl.pallas_call(
        _attn_sample_kernel,
        grid=(bs // bs_blk,),
        in_specs=[pl.BlockSpec((bs_blk, N_KV, rt, HEAD_DIM), lambda i: (i, 0, 0, 0)),
                  pl.BlockSpec((None, bs_blk, wb, kvw), lambda i: (l, i, 0, 0)),
                  pl.BlockSpec((None, bs_blk, wb, kvw), lambda i: (l, i, 0, 0)),
                  pl.BlockSpec((bs_blk, N_KV, ts, HEAD_DIM), lambda i: (i, 0, 0, 0)),
                  pl.BlockSpec((bs_blk, N_KV, ts, HEAD_DIM), lambda i: (i, 0, 0, 0)),
                  pl.BlockSpec((N_KV, rt, wb + ts), lambda i: (0, 0, 0)),
                  pl.BlockSpec((N_KV, rt, 1), lambda i: (0, 0, 0))],
        out_specs=pl.BlockSpec((bs_blk, N_KV, rt, HEAD_DIM), lambda i: (i, 0, 0, 0)),
        out_shape=jax.ShapeDtypeStruct((bs, N_KV, rt, HEAD_DIM), F32),
        compiler_params=_cparams("parallel"),
        name="attn_sample",
    )(q4, kc, vc, kn, vn, bias, sinks)


def _mlstm_kernel(*refs, t_valid, nseq, aliased, carried, first_layer=None):
    if aliased:
        refs = refs[:10] + refs[11:]
    (if_ref, ifr_ref, q_ref, k_ref, v_ref, o_ref, g_ref, c0_ref, n0_ref, m0_ref,
     h_ref, c1_ref, n1_ref, m1_ref) = refs[:14]
    c = pl.program_id(1)
    L = ifr_ref.shape[-1]
    if carried:
        c_in, n_in, m_in = c_out, n_out, m_out = refs[14:]

        @pl.when(c == 0)
        def _():
            c_in[...] = c0_ref[...]
            n_in[...] = n0_ref[...]
            m_in[...] = m0_ref[...]
    else:
        (c_in, n_in, m_in), (c_out, n_out, m_out) = (c0_ref, n0_ref, m0_ref), (c1_ref, n1_ref, m1_ref)
        if first_layer is not None:
            for dd in range(c1_ref.shape[0]):
                if dd != first_layer:
                    c1_ref[dd] = jnp.zeros(c1_ref.shape[1:], F32)
            c_out = c1_ref.at[first_layer]

    def seq(ref, s):
        return ref.at[s] if len(ref.shape) == 3 else ref

    tt = lax.broadcasted_iota(jnp.int32, (L, L), 0)
    ss = lax.broadcasted_iota(jnp.int32, (L, L), 1)
    causal = ss <= tt
    tril = causal.astype(F32)
    triu = (tt <= ss).astype(F32)
    for s_i in range(nseq):
        ifc = seq(if_ref, s_i)[...]
        ifr = ifr_ref[s_i]
        qr, kr, vr, orr, hr = (seq(r, s_i) for r in (q_ref, k_ref, v_ref, o_ref, h_ref))
        lf_c = _log_sigmoid(ifc)
        lf_r = _log_sigmoid(ifr)
        i_c, i_r = ifc, ifr
        if t_valid < L:
            rc = lax.broadcasted_iota(jnp.int32, (L, LANES), 0) < t_valid
            rr = lax.broadcasted_iota(jnp.int32, (2 * M_HEADS, L), 1) < t_valid
            lf_c = jnp.where(rc, lf_c, 0.0)
            lf_r = jnp.where(rr, lf_r, 0.0)
            i_c = jnp.where(rc, i_c, NEG_INF)
            i_r = jnp.where(rr, i_r, NEG_INF)
        f_c = _dot_hi(tril, lf_c)
        f_r = _dot_hi(lf_r, triu)
        state = [(c_in[s_i, h], n_in[s_i, h], m_in[s_i, h:h + 1, 0:1]) for h in range(M_HEADS)]
        new_state = []
        for h in range(M_HEADS):
            lo = h * M_DK
            cm, nrow, m0 = state[h]
            fc = f_c[:, M_HEADS + h:M_HEADS + h + 1]
            fr = f_r[M_HEADS + h:M_HEADS + h + 1, :]
            ir = i_r[h:h + 1, :]
            ic = i_c[:, h:h + 1]
            dm = jnp.where(causal, fc - fr + ir, NEG_INF)
            m_t = jnp.maximum(m0 + fc, jnp.max(dm, axis=-1, keepdims=True))
            inter = jnp.exp(m0 + fc - m_t)
            qb = qr[:, lo:lo + M_DK]
            qf = qb.astype(F32)
            kf = kr[:, lo:lo + M_DK].astype(F32) * (M_DK ** -0.5)
            vf = vr[:, lo:lo + M_DV].astype(F32)
            kb, vb = kf.astype(BF16), vf.astype(BF16)
            sc = _dot_nt(qb, kb) * jnp.exp(dm - m_t)
            num = inter * _dot(qb, cm.astype(BF16)) + _dot(sc.astype(BF16), vb)
            qn = inter * jnp.sum(qf * nrow, axis=-1, keepdims=True) + jnp.sum(sc, axis=-1, keepdims=True)
            hh = num / jnp.maximum(jnp.abs(qn), jnp.exp(-m_t))
            m_end = m_t[L - 1:L, :]
            f_end = fc[L - 1:L, :]
            decay = jnp.exp(m0 + f_end - m_end)
            w_s = jnp.exp(f_end - fc + ic - m_end)
            new_state.append((decay * cm + _dot_tn(kb, (w_s * vf).astype(BF16)),
                              decay * nrow + jnp.sum(w_s * kf, axis=0, keepdims=True),
                              jnp.broadcast_to(m_end, (1, LANES))))
            hn = _norm(hh) * g_ref[:, lo:lo + M_DV]
            hr[:, lo:lo + M_DV] = (_sigmoid(orr[:, lo:lo + M_DV].astype(F32)) * hn).astype(BF16)
        for h in range(M_HEADS):
            c_out[s_i, h], n_out[s_i, h], m_out[s_i, h:h + 1, :] = new_state[h]
        if not carried:
            m_out[s_i, M_HEADS:, :] = jnp.zeros((M_HEADS, LANES), F32)

    if carried:
        @pl.when(c == pl.num_programs(1) - 1)
        def _():
            c1_ref[...] = c_out[...]
            n1_ref[...] = n_out[...]
            m1_ref[...] = m_out[...]


def _mlstm_state_specs(nseq, l_state):
    if l_state is None:
        c_spec = pl.BlockSpec((nseq, M_HEADS, M_DK, M_DV), lambda b, c: (b, 0, 0, 0))
    else:
        c_spec = pl.BlockSpec((None, nseq, M_HEADS, M_DK, M_DV), lambda b, c: (l_state, b, 0, 0, 0))
    return (c_spec,
            pl.BlockSpec((nseq, M_HEADS, 1, M_DK), lambda b, c: (b, 0, 0, 0)),
            pl.BlockSpec((nseq, 2 * M_HEADS, LANES), lambda b, c: (b, 0, 0)))


def _mlstm_scratch(nseq):
    return [pltpu.VMEM((nseq, M_HEADS, M_DK, M_DV), F32),
            pltpu.VMEM((nseq, M_HEADS, 1, M_DK), F32),
            pltpu.VMEM((nseq, 2 * M_HEADS, LANES), F32)]


def _mlstm_prompt(z, zif, ifr, gamma, l, bp, nc, L):
    hw = M_HEADS * M_DK
    zero = lambda *s: jnp.zeros(s, F32)

    def rows(col):
        return lambda b, c: (b * nc + c, col)

    kern = functools.partial(_mlstm_kernel, t_valid=L, nseq=1, aliased=False, carried=True)
    return pl.pallas_call(
        kern,
        grid=(bp, nc),
        in_specs=[pl.BlockSpec((L, LANES), rows(0)),
                  pl.BlockSpec((1, 2 * M_HEADS, L), lambda b, c: (b * nc + c, 0, 0)),
                  pl.BlockSpec((L, hw), rows(Z_MQ // hw)),
                  pl.BlockSpec((L, hw), rows(Z_MK // hw)),
                  pl.BlockSpec((L, hw), rows(Z_MV // hw)),
                  pl.BlockSpec((L, hw), rows(Z_MO // hw)),
                  pl.BlockSpec((None, 1, hw), lambda b, c: (l, 0, 0)),
                  *_mlstm_state_specs(1, None)],
        out_specs=[pl.BlockSpec((L, hw), lambda b, c: (b * nc + c, 0)),
                   *_mlstm_state_specs(1, None)],
        out_shape=[jax.ShapeDtypeStruct((bp * nc * L, hw), BF16),
                   jax.ShapeDtypeStruct((bp, M_HEADS, M_DK, M_DV), F32),
                   jax.ShapeDtypeStruct((bp, M_HEADS, 1, M_DK), F32),
                   jax.ShapeDtypeStruct((bp, 2 * M_HEADS, LANES), F32)],
        scratch_shapes=_mlstm_scratch(1),
        compiler_params=_cparams("parallel", "arbitrary"),
        name="mlstm_prompt",
    )(zif, ifr, z, z, z, z, gamma, zero(bp, M_HEADS, M_DK, M_DV), zero(bp, M_HEADS, 1, M_DK),
      zero(bp, 2 * M_HEADS, LANES))


def _mlstm_sample(zm3, if3, ifr, gamma, c_all, n0, m0x, c_out_prev, l, t_valid, nseq):
    bs, L, _ = zm3.shape
    depth = c_all.shape[0]
    hw = M_HEADS * M_DK
    aliased = c_out_prev is not None

    def blk(col):
        return pl.BlockSpec((nseq, L, hw), lambda b, c: (b, 0, col))

    c_in, n_spec, m_spec = _mlstm_state_specs(nseq, l)
    in_specs = [pl.BlockSpec((nseq, L, LANES), lambda b, c: (b, 0, 0)),
                pl.BlockSpec((nseq, 2 * M_HEADS, L), lambda b, c: (b, 0, 0)),
                blk(0), blk(1), blk(2), blk(3),
                pl.BlockSpec((None, 1, hw), lambda b, c: (l, 0, 0)),
                c_in, n_spec, m_spec]
    args = [if3, ifr, zm3, zm3, zm3, zm3, gamma, c_all, n0, m0x]
    aliases = {}
    if aliased:
        in_specs.append(pl.BlockSpec(memory_space=pl.ANY))
        args.append(c_out_prev)
        aliases = {len(args) - 1: 1}
        c_out = c_in
    else:
        c_out = pl.BlockSpec((depth, nseq, M_HEADS, M_DK, M_DV), lambda b, c: (0, b, 0, 0, 0))
    kern = functools.partial(_mlstm_kernel, t_valid=t_valid, nseq=nseq, aliased=aliased, carried=False,
                             first_layer=None if aliased else l)
    return pl.pallas_call(
        kern,
        grid=(bs // nseq, 1),
        in_specs=in_specs,
        out_specs=[pl.BlockSpec((nseq, L, hw), lambda b, c: (b, 0, 0)), c_out, n_spec, m_spec],
        out_shape=[jax.ShapeDtypeStruct((bs, L, hw), BF16),
                   jax.ShapeDtypeStruct((depth, bs, M_HEADS, M_DK, M_DV), F32),
                   jax.ShapeDtypeStruct((bs, M_HEADS, 1, M_DK), F32),
                   jax.ShapeDtypeStruct((bs, 2 * M_HEADS, LANES), F32)],
        input_output_aliases=aliases,
        compiler_params=_cparams("parallel", "arbitrary"),
        name="mlstm_sample",
    )(*args)


def _m_state_in(m):
    lead = m.shape[:-1]
    mx = jnp.zeros(lead + (2 * M_HEADS, LANES), F32)
    return mx.at[..., :M_HEADS, :].set(jnp.broadcast_to(m[..., None], lead + (M_HEADS, LANES)))


def _merge_kernel(*refs, np_tiles, alpha, route):
    (cp_ref, cs_ref, ap_ref, as_ref, mp_ref, ms_ref, g0_ref, g1_ref, g2_ref, x_ref,
     gtp_ref, gts_ref, shp_ref, shs_ref, scp_ref, scs_ref,
     wc_ref, wa_ref, wm_ref, wo_ref, pg_ref, pb_ref) = refs[:22]
    if route:
        rw_ref, rb_ref, x1_ref, h_ref, route_ref = refs[22:]
    else:
        x1_ref, h_ref = refs[22:]
    is_s = pl.program_id(0) >= np_tiles

    def gate(ref):
        return _sigmoid(ref[...].astype(F32))

    y = (gate(g0_ref) * _dot(_pick(is_s, cp_ref, cs_ref), wc_ref[...])
         + gate(g1_ref) * _dot(_pick(is_s, ap_ref, as_ref), wa_ref[...])
         + gate(g2_ref) * _dot(_pick(is_s, mp_ref, ms_ref), wm_ref[...]))
    mix = _dot(y.astype(BF16), wo_ref[...])
    x1 = _norm(alpha * x_ref[...] + _pick(is_s, gtp_ref, gts_ref) * mix) * pg_ref[...] + pb_ref[...]
    x1_ref[...] = x1
    h = _norm(x1) * (1.0 + _pick(is_s, scp_ref, scs_ref)) + _pick(is_s, shp_ref, shs_ref)
    h_ref[...] = h.astype(h_ref.dtype)
    if route:
        h_hi, h_lo = _split_bf16(h)
        w_hi, w_lo = _split_bf16(rw_ref[...])
        logits = _dot(h_hi, w_hi) + (_dot(h_hi, w_lo) + _dot(h_lo, w_hi)) + rb_ref[...]
        lane = lax.broadcasted_iota(jnp.int32, logits.shape, 1)
        m1 = jnp.max(logits, axis=-1, keepdims=True)
        e1 = jnp.min(jnp.where(logits == m1, lane, LANES), axis=-1, keepdims=True)
        l2 = jnp.where(lane == e1, NEG_INF, logits)
        m2 = jnp.max(l2, axis=-1, keepdims=True)
        e2 = jnp.min(jnp.where(l2 == m2, lane, LANES), axis=-1, keepdims=True)
        ex = jnp.exp(m2 - m1)
        w1 = 1.0 / (1.0 + ex)
        w2 = ex / (1.0 + ex)
        out = jnp.where(lane == 0, e1.astype(F32),
                        jnp.where(lane == 1, e2.astype(F32),
                                  jnp.where(lane == 2, w1, jnp.where(lane == 3, w2, 0.0))))
        route_ref[...] = out


def _merge(cp, cs, ap, as_, mp, ms, z, x, mod_p, mod_s, wc, wa, wm, wo, pg, pb, l, rm, alpha, router):
    n, d = x.shape
    hw = D_CONV
    route = router is not None
    tm = rm.t

    def pblk():
        return pl.BlockSpec((tm, hw), lambda i: (rm.prompt(i), 0))

    def sblk():
        return pl.BlockSpec((tm, hw), lambda i: (rm.sample(i), 0))

    def zg(k):
        return pl.BlockSpec((tm, d), lambda i: (i, Z_G // d + k))

    def lw(a):
        return pl.BlockSpec((None,) + a.shape[1:], lambda i: (l,) + (0,) * (a.ndim - 1))

    post = pl.BlockSpec((None, None, 1, d), lambda i: (l, 0, 0, 0))
    in_specs = [pblk(), sblk(), pblk(), sblk(), pblk(), sblk(), zg(0), zg(1), zg(2),
                pl.BlockSpec((tm, d), lambda i: (i, 0)),
                *rm.mod_specs(l, 2, d), *rm.mod_specs(l, 3, d), *rm.mod_specs(l, 4, d),
                lw(wc), lw(wa), lw(wm), lw(wo), post, post]
    args = [cp, cs, ap, as_, mp, ms, z, z, z, x, mod_p, mod_s, mod_p, mod_s, mod_p, mod_s,
            wc, wa, wm, wo, pg, pb]
    out_specs = [pl.BlockSpec((tm, d), lambda i: (i, 0)), pl.BlockSpec((tm, d), lambda i: (i, 0))]
    out_shape = [jax.ShapeDtypeStruct((n, d), F32), jax.ShapeDtypeStruct((n, d), F32 if route else BF16)]
    if route:
        rw, rb, lj = router
        in_specs += [pl.BlockSpec((None, d, LANES), lambda i: (lj, 0, 0)),
                     pl.BlockSpec((None, 1, LANES), lambda i: (lj, 0, 0))]
        args += [rw, rb]
        out_specs.append(pl.BlockSpec((tm, LANES), lambda i: (i, 0)))
        out_shape.append(jax.ShapeDtypeStruct((n, LANES), F32))
    kern = functools.partial(_merge_kernel, np_tiles=rm.np_tiles, alpha=alpha, route=route)
    return pl.pallas_call(
        kern, grid=(n // tm,), in_specs=in_specs, out_specs=out_specs, out_shape=out_shape,
        compiler_params=_cparams("parallel"), name="merge_route" if route else "merge",
    )(*args)


def _post_residual(x_ref, gp_ref, gs_ref, f, pg_ref, pb_ref, is_s, alpha):
    return _norm(alpha * x_ref[...] + _pick(is_s, gp_ref, gs_ref) * f) * pg_ref[...] + pb_ref[...]


def _ffn_kernel(h_ref, wg_ref, wu_ref, wd_ref, x_ref, gp_ref, gs_ref, pg_ref, pb_ref, o_ref, acc,
                *, alpha, np_tiles):
    j = pl.program_id(1)
    h = h_ref[...]
    a = (_silu(_dot(h, wg_ref[...])) * _dot(h, wu_ref[...])).astype(BF16)
    part = _dot(a, wd_ref[...])

    @pl.when(j == 0)
    def _():
        acc[...] = part

    @pl.when(j > 0)
    def _():
        acc[...] = acc[...] + part

    @pl.when(j == pl.num_programs(1) - 1)
    def _():
        is_s = pl.program_id(0) >= np_tiles
        o_ref[...] = _post_residual(x_ref, gp_ref, gs_ref, acc[...], pg_ref, pb_ref, is_s, alpha)


def _ffn_dense(h, wg, wu, wd, x, mod_p, mod_s, pg, pb, l, lj, rm, tf, alpha):
    n, d = x.shape
    f = wg.shape[2]
    tm = rm.t
    post = pl.BlockSpec((None, None, 1, d), lambda i, j: (l, 1, 0, 0))
    return pl.pallas_call(
        functools.partial(_ffn_kernel, alpha=alpha, np_tiles=rm.np_tiles),
        grid=(n // tm, f // tf),
        in_specs=[pl.BlockSpec((tm, d), lambda i, j: (i, 0)),
                  pl.BlockSpec((None, d, tf), lambda i, j: (lj, 0, j)),
                  pl.BlockSpec((None, d, tf), lambda i, j: (lj, 0, j)),
                  pl.BlockSpec((None, tf, d), lambda i, j: (lj, j, 0)),
                  pl.BlockSpec((tm, d), lambda i, j: (i, 0)),
                  *rm.mod_specs(l, 5, d), post, post],
        out_specs=pl.BlockSpec((tm, d), lambda i, j: (i, 0)),
        out_shape=jax.ShapeDtypeStruct((n, d), F32),
        scratch_shapes=[pltpu.VMEM((tm, d), F32)],
        compiler_params=_cparams("parallel", "arbitrary"),
        name="ffn_dense",
    )(h, wg, wu, wd, x, mod_p, mod_s, pg, pb)


def _rank_kernel(route_ref, rank_ref, tot_ref, carry):
    i = pl.program_id(0)
    tm = route_ref.shape[0]

    @pl.when(i == 0)
    def _():
        carry[...] = jnp.zeros_like(carry)

    r = route_ref[...]
    lane = lax.broadcasted_iota(jnp.int32, (tm, LANES), 1)
    e1 = r[:, 0:1].astype(jnp.int32)
    e2 = r[:, 1:2].astype(jnp.int32)
    hit1 = lane == e1
    hit2 = lane == e2
    onehot = jnp.where(jnp.logical_or(hit1, hit2), 1.0, 0.0)
    tt = lax.broadcasted_iota(jnp.int32, (tm, tm), 0)
    ss = lax.broadcasted_iota(jnp.int32, (tm, tm), 1)
    before = jnp.where(ss < tt, 1.0, 0.0).astype(BF16)
    cnt = _dot(before, onehot.astype(BF16)) + carry[0:1, :]
    r1 = jnp.sum(jnp.where(hit1, cnt, 0.0), axis=-1, keepdims=True)
    r2 = jnp.sum(jnp.where(hit2, cnt, 0.0), axis=-1, keepdims=True)
    rank_ref[...] = jnp.where(lane == 0, r1, jnp.where(lane == 1, r2, 0.0))
    carry[...] = carry[...] + jnp.sum(onehot, axis=0, keepdims=True)
    tot_ref[...] = carry[...]


def _moe_rank(route, tm):
    n = route.shape[0]
    return pl.pallas_call(
        _rank_kernel,
        grid=(n // tm,),
        in_specs=[pl.BlockSpec((tm, LANES), lambda i: (i, 0))],
        out_specs=[pl.BlockSpec((tm, LANES), lambda i: (i, 0)),
                   pl.BlockSpec((8, LANES), lambda i: (0, 0))],
        out_shape=[jax.ShapeDtypeStruct((n, LANES), F32), jax.ShapeDtypeStruct((8, LANES), F32)],
        scratch_shapes=[pltpu.VMEM((8, LANES), F32)],
        compiler_params=_cparams("arbitrary"),
        name="moe_rank",
    )(route)


def _row_copy(src, s, dst, t, sem):
    return pltpu.make_async_copy(src.at[pl.ds(s, 1)], dst.at[pl.ds(t, 1)], sem)


def _dispatch_kernel(d1_ref, d2_ref, zl_ref, h_ref, xs_hbm, zbuf, sem, zsem):
    i = pl.program_id(0)
    tm = h_ref.shape[0]

    @pl.when(i == 0)
    def _():
        zbuf[...] = jnp.zeros_like(zbuf)

        def zero_copy(k):
            row = pl.multiple_of(jnp.maximum(zl_ref[k], 0), MOE_SUB)
            return pltpu.make_async_copy(zbuf, xs_hbm.at[pl.ds(row, MOE_SUB)], zsem)

        def start(k, carry):
            @pl.when(zl_ref[k] >= 0)
            def _():
                zero_copy(k).start()
            return carry

        def wait(k, carry):
            @pl.when(zl_ref[k] >= 0)
            def _():
                zero_copy(k).wait()
            return carry

        lax.fori_loop(0, zl_ref.shape[0], start, 0)
        lax.fori_loop(0, zl_ref.shape[0], wait, 0)

    def issue(r, carry):
        _row_copy(h_ref, r, xs_hbm, d1_ref[i * tm + r], sem).start()
        _row_copy(h_ref, r, xs_hbm, d2_ref[i * tm + r], sem).start()
        return carry

    lax.fori_loop(0, tm, issue, 0)
    for _ in range(2):
        pltpu.make_async_copy(h_ref, xs_hbm.at[pl.ds(0, tm)], sem).wait()


def _moe_dispatch(dest1, dest2, zlist, h, n_rows, tm):
    n, d = h.shape
    grid_spec = pltpu.PrefetchScalarGridSpec(
        num_scalar_prefetch=3,
        grid=(n // tm,),
        in_specs=[pl.BlockSpec((tm, d), lambda i, *_: (i, 0))],
        out_specs=pl.BlockSpec(memory_space=pl.ANY),
        scratch_shapes=[pltpu.VMEM((MOE_SUB, d), F32), pltpu.SemaphoreType.DMA(()), pltpu.SemaphoreType.DMA(())],
    )
    return pl.pallas_call(
        _dispatch_kernel, grid_spec=grid_spec,
        out_shape=jax.ShapeDtypeStruct((n_rows, d), F32),
        compiler_params=_cparams("arbitrary"),
        name="moe_dispatch",
    )(dest1, dest2, zlist, h)


def _expert_kernel(blk_e_ref, nvalid_ref, xs_ref, wg_ref, wu_ref, wd_ref, y_ref, xb, acc):
    b = pl.program_id(0)
    j = pl.program_id(1)
    nsub = xb.shape[0] // MOE_SUB
    nv = nvalid_ref[b]
    wg = wg_ref[...].astype(BF16)
    wu = wu_ref[...].astype(BF16)
    wd = wd_ref[...].astype(BF16)
    for s in range(nsub):
        rows = pl.ds(s * MOE_SUB, MOE_SUB)

        @pl.when(s * MOE_SUB < nv)
        def _():
            @pl.when(j == 0)
            def _():
                xb[rows, :] = xs_ref[rows, :].astype(BF16)

            x = xb[rows, :]
            a = (_silu(_dot(x, wg)) * _dot(x, wu)).astype(BF16)
            part = _dot(a, wd)

            @pl.when(j == 0)
            def _():
                acc[rows, :] = part

            @pl.when(j > 0)
            def _():
                acc[rows, :] = acc[rows, :] + part

            @pl.when(j == pl.num_programs(1) - 1)
            def _():
                y_ref[rows, :] = acc[rows, :]

        @pl.when(jnp.logical_and(s * MOE_SUB >= nv, j == pl.num_programs(1) - 1))
        def _():
            y_ref[rows, :] = jnp.zeros((MOE_SUB, y_ref.shape[1]), F32)


def _moe_experts(blk_e, nvalid, xs, wg, wu, wd, lj, tb, tf):
    n_rows, d = xs.shape
    f = wg.shape[3]
    nj = f // tf

    def jm(b, j, nv):
        return jnp.where(nv[b] > 0, j, nj - 1)

    grid_spec = pltpu.PrefetchScalarGridSpec(
        num_scalar_prefetch=2,
        grid=(n_rows // tb, nj),
        in_specs=[pl.BlockSpec((tb, d), lambda b, j, be, nv: (b, 0)),
                  pl.BlockSpec((None, None, d, tf), lambda b, j, be, nv: (lj, be[b], 0, jm(b, j, nv))),
                  pl.BlockSpec((None, None, d, tf), lambda b, j, be, nv: (lj, be[b], 0, jm(b, j, nv))),
                  pl.BlockSpec((None, None, tf, d), lambda b, j, be, nv: (lj, be[b], jm(b, j, nv), 0))],
        out_specs=pl.BlockSpec((tb, d), lambda b, j, be, nv: (b, 0)),
        scratch_shapes=[pltpu.VMEM((tb, d), BF16), pltpu.VMEM((tb, d), F32)],
    )
    return pl.pallas_call(
        _expert_kernel, grid_spec=grid_spec,
        out_shape=jax.ShapeDtypeStruct((n_rows, d), F32),
        compiler_params=_cparams("parallel", "arbitrary"),
        name="moe_experts",
    )(blk_e, nvalid, xs, wg, wu, wd)


def _combine_kernel(d1_ref, d2_ref, ys_hbm, route_ref, x_ref, gp_ref, gs_ref, pg_ref, pb_ref, o_ref,
                    y1, y2, sem1, sem2, *, alpha, np_tiles):
    i = pl.program_id(0)
    tm = x_ref.shape[0]

    def issue(r, carry):
        _row_copy(ys_hbm, d1_ref[i * tm + r], y1, r, sem1).start()
        _row_copy(ys_hbm, d2_ref[i * tm + r], y2, r, sem2).start()
        return carry

    lax.fori_loop(0, tm, issue, 0)
    pltpu.make_async_copy(ys_hbm.at[pl.ds(0, tm)], y1, sem1).wait()
    pltpu.make_async_copy(ys_hbm.at[pl.ds(0, tm)], y2, sem2).wait()
    r = route_ref[...]
    f = y1[...] * r[:, 2:3] + y2[...] * r[:, 3:4]
    o_ref[...] = _post_residual(x_ref, gp_ref, gs_ref, f, pg_ref, pb_ref, i >= np_tiles, alpha)


def _moe_combine(dest1, dest2, ys, route, x, mod_p, mod_s, pg, pb, l, rm, alpha):
    n, d = x.shape
    tm = rm.t
    post = pl.BlockSpec((None, None, 1, d), lambda i, *_: (l, 1, 0, 0))
    grid_spec = pltpu.PrefetchScalarGridSpec(
        num_scalar_prefetch=2,
        grid=(n // tm,),
        in_specs=[pl.BlockSpec(memory_space=pl.ANY),
                  pl.BlockSpec((tm, LANES), lambda i, *_: (i, 0)),
                  pl.BlockSpec((tm, d), lambda i, *_: (i, 0)),
                  *rm.mod_specs(l, 5, d), post, post],
        out_specs=pl.BlockSpec((tm, d), lambda i, *_: (i, 0)),
        scratch_shapes=[pltpu.VMEM((tm, d), F32), pltpu.VMEM((tm, d), F32),
                        pltpu.SemaphoreType.DMA(()), pltpu.SemaphoreType.DMA(())],
    )
    return pl.pallas_call(
        functools.partial(_combine_kernel, alpha=alpha, np_tiles=rm.np_tiles), grid_spec=grid_spec,
        out_shape=jax.ShapeDtypeStruct((n, d), F32),
        compiler_params=_cparams("arbitrary"),
        name="moe_combine",
    )(dest1, dest2, ys, route, x, mod_p, mod_s, pg, pb)


def _moe_ffn(h, route, wg, wu, wd, x, mod_p, mod_s, pg, pb, l, lj, rm, tb, tf, alpha):
    n = h.shape[0]
    rank, tot = _moe_rank(route, rm.t)
    counts = tot[0, :N_EXPERTS].astype(jnp.int32)
    padded = (counts + tb - 1) // tb * tb
    pad_end = jnp.cumsum(padded)
    pad_start = pad_end - padded
    experts = jnp.arange(N_EXPERTS, dtype=jnp.int32)

    def slot(col):
        e = route[:, col].astype(jnp.int32)
        start = jnp.sum(jnp.where(e[:, None] == experts[None, :], pad_start[None, :], 0), axis=1)
        return start + rank[:, col].astype(jnp.int32)

    dest1, dest2 = slot(0), slot(1)
    n_blocks = -(-(2 * n + N_EXPERTS * (tb - 1)) // tb)
    blk_start = jnp.arange(n_blocks, dtype=jnp.int32) * tb
    blk_e = jnp.minimum(jnp.sum(pad_end[None, :] <= blk_start[:, None], axis=1), N_EXPERTS - 1).astype(jnp.int32)
    nvalid = jnp.clip(pad_start[blk_e] + counts[blk_e] - blk_start, 0, tb).astype(jnp.int32)
    nvalid = jnp.where(blk_start < pad_end[-1], nvalid, 0)
    sub_start = jnp.arange(n_blocks * tb // MOE_SUB, dtype=jnp.int32) * MOE_SUB
    sub_blk = sub_start // tb
    sub_room = blk_start[sub_blk] + nvalid[sub_blk] - sub_start
    zlist = jnp.where(sub_room < MOE_SUB, sub_start, -1).astype(jnp.int32)
    xs = _moe_dispatch(dest1, dest2, zlist, h, n_blocks * tb, rm.t)
    ys = _moe_experts(blk_e, nvalid, xs, wg, wu, wd, lj, tb, tf)
    return _moe_combine(dest1, dest2, ys, route, x, mod_p, mod_s, pg, pb, l, rm, alpha)


def _pack_w_in(w_in, b_in):
    depth, d, _ = w_in.shape
    q_end = 2 * D_CONV + N_HEADS * HEAD_DIM
    k_end = q_end + N_KV * HEAD_DIM
    a_end = k_end + N_KV * HEAD_DIM
    m_end = a_end + 4 * M_HEADS * M_DK
    if_end = m_end + 2 * M_HEADS

    def pack(a):
        return jnp.concatenate([a[..., if_end:], a[..., :q_end], a[..., a_end:m_end], a[..., q_end:a_end]], axis=-1)

    def gates(a):
        return jnp.pad(a[..., m_end:if_end], [(0, 0)] * (a.ndim - 1) + [(0, LANES - 2 * M_HEADS)])

    return (pack(w_in).astype(BF16), pack(b_in).reshape(depth, 1, Z_W),
            gates(w_in).astype(BF16), gates(b_in).reshape(depth, 1, LANES))


def kernel(x_prompt, x_sample, cache_swa_k, cache_swa_v, state_conv, state_mlstm_C, state_mlstm_n, state_mlstm_m, c_prompt, c_sample, w_ada, b_ada, w_in, b_in, conv_w, conv_b, conv_ln_g, conv_ln_b, w_conv_out, attn_sinks, rel_bias, w_attn_out, m_norm_g, w_m_out, w_out, post_ln_g, post_ln_b, ffn_w_gate, ffn_w_up, ffn_w_down, router_w, router_b, moe_w_gate, moe_w_up, moe_w_down):
    bp, tp, d = x_prompt.shape
    bs, ts, _ = x_sample.shape
    depth = w_ada.shape[0]
    alpha = (2 * depth) ** 0.25
    n_p, n_s = bp * tp, bs * ts
    tm = n_s
    assert d == D_MODEL and tp % tm == 0 and tp % WINDOW == 0 and tm % 32 == 0
    rm = _RowMap(tm, bp, tp, n_p)
    rm_half = _RowMap(tm // 2, bp, tp, n_p)
    wb = cache_swa_k.shape[2]
    big = n_p >= 4096
    tc = 512 if big else tm
    lm = 256 if big else min(tp, 128)
    tb = 1024 if big else 2 * MOE_SUB
    bs_blk = 32 if bs % 32 == 0 else bs
    bs_att = 16 if bs % 16 == 0 else bs
    bs_m = 4 if bs % 4 == 0 else 1
    lts = 16

    x = jnp.concatenate([x_prompt.reshape(n_p, d), jnp.transpose(x_sample, (1, 0, 2)).reshape(n_s, d)], axis=0)

    nc_rows = -(-(bp + bs) // 8) * 8
    c_all = jnp.zeros((nc_rows, d), F32).at[:bp].set(c_prompt).at[bp:bp + bs].set(c_sample)
    mod = _ada_mod(c_all, w_ada, b_ada)
    mod_p = mod[:, :bp].reshape(depth, bp, 1, 6 * d)
    mod_s = jnp.tile(mod[:, bp:bp + bs], (1, ts, 1))

    w_in_p, b_in_p, w_if, b_if = _pack_w_in(w_in, b_in)
    wc_b, wa_b, wm_b, wo_b = (w.astype(BF16) for w in (w_conv_out, w_attn_out, w_m_out, w_out))
    fg_b, fu_b, fd_b = (w.astype(BF16) for w in (ffn_w_gate, ffn_w_up, ffn_w_down))
    cw_pad = jnp.pad(conv_w, ((0, 0), (0, CONV_PAD - CONV_W), (0, 0)))
    cvecs = [v.reshape(depth, 1, D_CONV) for v in (conv_b, conv_ln_g, conv_ln_b)]
    rw_pad = jnp.pad(router_w, ((0, 0), (0, 0), (0, LANES - N_EXPERTS)))
    rb_pad = jnp.pad(router_b, ((0, 0), (0, LANES - N_EXPERTS)), constant_values=NEG_INF)[:, None, :]
    pg = post_ln_g.reshape(depth, 2, 1, d)
    pb = post_ln_b.reshape(depth, 2, 1, d)
    gamma = m_norm_g.reshape(depth, 1, M_HEADS * M_DV)
    sinks = attn_sinks.astype(F32)

    qi = jnp.arange(WINDOW)[:, None]
    kj = jnp.arange(2 * WINDOW)[None, :]
    dist_p = qi + WINDOW - kj
    bh = _bias_heads(rel_bias, dist_p, (dist_p >= 0) & (dist_p < WINDOW))
    bias_p = bh.reshape(N_HEADS // 2, 2, WINDOW, 2 * WINDOW).transpose(0, 2, 1, 3).reshape(
        N_HEADS // 2, WINDOW, 4 * WINDOW)
    dist_s = jnp.arange(ts)[:, None] + wb - jnp.arange(wb + ts)[None, :]
    bias_s = _bias_heads(rel_bias, dist_s, (dist_s >= 0) & (dist_s < WINDOW)).reshape(
        N_KV, Q_PER_KV * ts, wb + ts)
    kc_all = cache_swa_k.reshape(depth, bs, wb, N_KV * HEAD_DIM)
    vc_all = cache_swa_v.reshape(depth, bs, wb, N_KV * HEAD_DIM)

    f_dense = ffn_w_gate.shape[2]
    tf_dense = f_dense // 2 if (f_dense // 2) % LANES == 0 else f_dense
    f_moe = moe_w_gate.shape[3]
    tf_moe = 512 if f_moe % 512 == 0 else f_moe

    n0_all = state_mlstm_n[:, :, :, None, :]
    m0_all = _m_state_in(state_mlstm_m)
    s_c = None
    new_p = [[] for _ in range(6)]
    new_s = [[] for _ in range(6)]
    for l in range(depth):
        j = l // 2
        z, zif = _ln_proj(x, mod_p, mod_s, w_in_p, b_in_p, w_if, b_if, l, rm)
        zs3 = z[n_p:].reshape(ts, bs, Z_W)

        cp, ns_p = _conv_prompt(z, jnp.zeros((bp, CONV_PAD, D_CONV), F32), cw_pad, *cvecs, l, bp, tp, tc)
        cs3, a_s3 = _conv_sample(zs3, state_conv, cw_pad, *cvecs, l, bs_blk)
        new_p[2].append(ns_p[:, CONV_PAD - CONV_W + 1:])
        new_s[2].append(jnp.concatenate([state_conv[l][:, ts:], jnp.transpose(a_s3, (1, 0, 2))], axis=1))

        sink_h = sinks[l].reshape(N_KV, Q_PER_KV, 1)
        sink_s = jnp.broadcast_to(sink_h, (N_KV, Q_PER_KV, ts)).reshape(N_KV, Q_PER_KV * ts, 1)
        ap = _attn_prompt(z, bias_p, sinks, l, bp, tp)
        nk = min(WINDOW, tp)
        kv_tail = z[:n_p].reshape(bp, tp, Z_W)[:, tp - nk:, Z_K:Z_K + 2 * N_KV * HEAD_DIM].astype(F32)
        new_p[0].append(kv_tail[..., :N_KV * HEAD_DIM].reshape(bp, nk, N_KV, HEAD_DIM))
        new_p[1].append(kv_tail[..., N_KV * HEAD_DIM:].reshape(bp, nk, N_KV, HEAD_DIM))
        q_s = zs3[:, :, Z_Q:Z_Q + N_HEADS * HEAD_DIM].reshape(ts, bs, N_KV, Q_PER_KV, HEAD_DIM)
        q4 = jnp.transpose(q_s, (1, 2, 3, 0, 4)).reshape(bs, N_KV, Q_PER_KV * ts, HEAD_DIM)
        k_s = jnp.transpose(zs3[:, :, Z_K:Z_K + N_KV * HEAD_DIM].reshape(ts, bs, N_KV, HEAD_DIM), (1, 0, 2, 3))
        v_s = jnp.transpose(zs3[:, :, Z_V:Z_V + N_KV * HEAD_DIM].reshape(ts, bs, N_KV, HEAD_DIM), (1, 0, 2, 3))
        o4 = _attn_sample(q4, kc_all, vc_all, jnp.transpose(k_s, (0, 2, 1, 3)), jnp.transpose(v_s, (0, 2, 1, 3)),
                          bias_s, sink_s, l, bs_att)
        as_ = jnp.transpose(o4.reshape(bs, N_KV, Q_PER_KV, ts, HEAD_DIM), (3, 0, 1, 2, 4)).reshape(n_s, -1).astype(BF16)
        new_s[0].append(jnp.concatenate([cache_swa_k[l][:, ts:], k_s.astype(F32)], axis=1))
        new_s[1].append(jnp.concatenate([cache_swa_v[l][:, ts:], v_s.astype(F32)], axis=1))

        ncp = tp // lm
        if_p = zif[:n_p, :2 * M_HEADS].reshape(bp * ncp, lm, 2 * M_HEADS)
        mp, c1p, n1p, m1p = _mlstm_prompt(z, zif, jnp.transpose(if_p, (0, 2, 1)), gamma, l, bp, ncp, lm)
        new_p[3].append(c1p)
        new_p[4].append(n1p[:, :, 0])
        new_p[5].append(m1p[:, :M_HEADS, 0])
        tpad = ((0, 0), (0, lts - ts), (0, 0))
        zm3 = jnp.pad(jnp.transpose(zs3[:, :, Z_MQ:Z_K], (1, 0, 2)), tpad)
        if3 = jnp.pad(jnp.transpose(zif[n_p:].reshape(ts, bs, LANES), (1, 0, 2)), tpad)
        ms, s_c, n1s, m1s = _mlstm_sample(zm3, if3, jnp.transpose(if3[:, :, :2 * M_HEADS], (0, 2, 1)), gamma,
                                          state_mlstm_C, n0_all[l], m0_all[l], s_c, l, ts, bs_m)
        ms = jnp.transpose(ms[:, :ts], (1, 0, 2)).reshape(n_s, -1)
        new_s[4].append(n1s[:, :, 0])
        new_s[5].append(m1s[:, :M_HEADS, 0])

        moe = l % 2 == 1
        router = (rw_pad, rb_pad, j) if moe else None
        outs = _merge(cp, cs3.reshape(n_s, D_CONV), ap, as_, mp, ms, z, x, mod_p, mod_s,
                      wc_b, wa_b, wm_b, wo_b, pg, pb, l, rm_half, alpha, router)
        if moe:
            x1, h2, route = outs
            x = _moe_ffn(h2, route, moe_w_gate, moe_w_up, moe_w_down, x1, mod_p, mod_s, pg, pb,
                         l, j, rm, tb, tf_moe, alpha)
        else:
            x1, h2 = outs
            x = _ffn_dense(h2, fg_b, fu_b, fd_b, x1, mod_p, mod_s, pg, pb, l, j, rm, tf_dense, alpha)

    y_p = x[:n_p].reshape(bp, tp, d)
    y_s = jnp.transpose(x[n_p:].reshape(ts, bs, d), (1, 0, 2))
    p_k, p_v, p_conv, p_c, p_n, p_m = [jnp.stack(a) for a in new_p]
    s_k, s_v, s_conv = [jnp.stack(a) for a in new_s[:3]]
    s_n, s_m = jnp.stack(new_s[4]), jnp.stack(new_s[5])
    return (y_p, y_s, p_k, p_v, p_conv, p_c, p_n, p_m, s_k, s_v, s_conv, s_c, s_n, s_m)
```

```python
import functools
import math

import jax
import jax.numpy as jnp
from jax import lax
from jax.experimental import pallas as pl
from jax.experimental.pallas import tpu as pltpu

F32 = jnp.float32
BF16 = jnp.bfloat16

D_MODEL = 1024
D_CONV = 512
CONV_W = 31
CONV_PAD = 32
N_HEADS = 8
N_KV = 2
HEAD_DIM = 64
Q_PER_KV = N_HEADS // N_KV
WINDOW = 128
N_BUCKETS = 32
MAX_DIST = 128
M_HEADS = 4
M_DK = 128
M_DV = 128
N_EXPERTS = 8
LN_EPS = 1e-5
LANES = 128
NEG_INF = float("-inf")
VMEM_LIMIT = 56 * 1024 * 1024

Z_G, Z_UA, Z_UB, Z_Q = 0, 3072, 3584, 4096
Z_MQ, Z_MK, Z_MV, Z_MO = 4608, 5120, 5632, 6144
Z_K, Z_V, Z_W = 6656, 6784, 6912
TN_IN = 2304
ATT_QB = 4
MOE_SUB = 256
CONV_CHUNK = 32


def _cparams(*sem):
    return pltpu.CompilerParams(dimension_semantics=sem, vmem_limit_bytes=VMEM_LIMIT)


def _sigmoid(x):
    return 1.0 / (1.0 + jnp.exp(-x))


def _silu(x):
    return x * _sigmoid(x)


def _log_sigmoid(x):
    return jnp.minimum(x, 0.0) - jnp.log(1.0 + jnp.exp(-jnp.abs(x)))


def _norm(x):
    mu = jnp.mean(x, axis=-1, keepdims=True)
    xc = x - mu
    var = jnp.mean(xc * xc, axis=-1, keepdims=True)
    return xc * lax.rsqrt(var + LN_EPS)


def _dot(a, b):
    return jnp.dot(a, b, preferred_element_type=F32)


def _dot_nt(a, b):
    return lax.dot_general(a, b, (((1,), (1,)), ((), ())), preferred_element_type=F32)


def _dot_tn(a, b):
    return lax.dot_general(a, b, (((0,), (0,)), ((), ())), preferred_element_type=F32)


def _dot_hi(a, b):
    return jnp.dot(a, b, preferred_element_type=F32, precision=lax.Precision.HIGHEST)


def _split_bf16(a):
    hi = a.astype(BF16)
    return hi, (a - hi.astype(F32)).astype(BF16)


class _RowMap:
    def __init__(self, t, bp, tp, n_p):
        self.t = t
        self.bp = bp
        self.per_seq = tp // t
        self.np_tiles = n_p // t

    def seq(self, i):
        return jnp.minimum(i // self.per_seq, self.bp - 1)

    def prompt(self, i):
        return jnp.minimum(i, self.np_tiles - 1)

    def sample(self, i):
        return jnp.maximum(i - self.np_tiles, 0)

    def mod_specs(self, l, k, d):
        return (pl.BlockSpec((None, None, 1, d), lambda i, *_: (l, self.seq(i), 0, k)),
                pl.BlockSpec((None, self.t, d), lambda i, *_: (l, self.sample(i), k)))


def _pick(is_s, p_ref, s_ref):
    return jnp.where(is_s, s_ref[...], p_ref[...])


def _ada_kernel(c_ref, w_ref, b_ref, o_ref):
    s = _silu(c_ref[...]).astype(BF16)
    o_ref[0] = _dot(s, w_ref[0].astype(BF16)) + b_ref[0]


def _ada_mod(c_all, w_ada, b_ada):
    depth, d, n6 = w_ada.shape
    rows = c_all.shape[0]
    return pl.pallas_call(
        _ada_kernel,
        grid=(depth, n6 // d),
        in_specs=[pl.BlockSpec((rows, d), lambda l, j: (0, 0)),
                  pl.BlockSpec((1, d, d), lambda l, j: (l, 0, j)),
                  pl.BlockSpec((1, 1, d), lambda l, j: (l, 0, j))],
        out_specs=pl.BlockSpec((1, rows, d), lambda l, j: (l, 0, j)),
        out_shape=jax.ShapeDtypeStruct((depth, rows, n6), F32),
        compiler_params=_cparams("parallel", "parallel"),
        name="ada_mod",
    )(c_all, w_ada, b_ada.reshape(depth, 1, n6))


def _ln_proj_kernel(x_ref, shp_ref, shs_ref, scp_ref, scs_ref, w_ref, b_ref, wif_ref, bif_ref,
                    z_ref, zif_ref, h_scr, *, np_tiles):
    is_s = pl.program_id(0) >= np_tiles

    @pl.when(pl.program_id(1) == 0)
    def _():
        h = _norm(x_ref[...]) * (1.0 + _pick(is_s, scp_ref, scs_ref)) + _pick(is_s, shp_ref, shs_ref)
        h = h.astype(BF16)
        h_scr[...] = h
        zif_ref[...] = _dot(h, wif_ref[...]) + bif_ref[...]

    z_ref[...] = (_dot(h_scr[...], w_ref[...]) + b_ref[...]).astype(BF16)


def _ln_proj(x, mod_p, mod_s, w, b, wif, bif, l, rm):
    n, d = x.shape
    zw = w.shape[2]
    tm = rm.t
    shp, shs = rm.mod_specs(l, 0, d)
    scp, scs = rm.mod_specs(l, 1, d)
    return pl.pallas_call(
        functools.partial(_ln_proj_kernel, np_tiles=rm.np_tiles),
        grid=(n // tm, zw // TN_IN),
        in_specs=[pl.BlockSpec((tm, d), lambda i, j: (i, 0)),
                  shp, shs, scp, scs,
                  pl.BlockSpec((None, d, TN_IN), lambda i, j: (l, 0, j)),
                  pl.BlockSpec((None, 1, TN_IN), lambda i, j: (l, 0, j)),
                  pl.BlockSpec((None, d, LANES), lambda i, j: (l, 0, 0)),
                  pl.BlockSpec((None, 1, LANES), lambda i, j: (l, 0, 0))],
        out_specs=[pl.BlockSpec((tm, TN_IN), lambda i, j: (i, j)),
                   pl.BlockSpec((tm, LANES), lambda i, j: (i, 0))],
        out_shape=[jax.ShapeDtypeStruct((n, zw), BF16), jax.ShapeDtypeStruct((n, LANES), F32)],
        scratch_shapes=[pltpu.VMEM((tm, d), BF16)],
        compiler_params=_cparams("parallel", "arbitrary"),
        name="ln_proj",
    )(x, mod_p, mod_s, mod_p, mod_s, w, b, wif, bif)


def _conv_tail(yc, g_ref, b_ref):
    y = _norm(yc) * g_ref[...] + b_ref[...]
    return _silu(y).astype(BF16)


def _conv_prompt_kernel(ua_ref, ub_ref, st_ref, cw_ref, cb_ref, g_ref, b_ref, o_ref, ns_ref,
                        ext, shifted, yc, wrep):
    t = pl.program_id(1)
    tc = ua_ref.shape[0]
    sub = 8

    @pl.when(t == 0)
    def _():
        ext[0:CONV_PAD, :] = st_ref[0]

    @pl.when(t > 0)
    def _():
        ext[0:CONV_PAD, :] = ext[tc:tc + CONV_PAD, :]

    ext[CONV_PAD:, :] = ua_ref[...].astype(F32) * _sigmoid(ub_ref[...].astype(F32))
    for s in range(1, sub):
        shifted[s - 1] = ext[s:s + tc + CONV_PAD - sub, :]
    off = CONV_PAD - (CONV_W - 1)
    for w in range(CONV_W):
        wrep[w] = jnp.broadcast_to(cw_ref[w:w + 1, :], (sub, D_CONV))
    groups = CONV_CHUNK // sub

    def chunk(c, carry):
        r0 = c * CONV_CHUNK
        acc = jnp.broadcast_to(cb_ref[...].reshape(1, 1, D_CONV), (groups, sub, D_CONV))
        for w in range(CONV_W):
            base, s = (off + w) // sub * sub, (off + w) % sub
            src = ext if s == 0 else shifted.at[s - 1]
            win = src[pl.ds(pl.multiple_of(r0 + base, sub), CONV_CHUNK), :]
            acc = acc + win.reshape(groups, sub, D_CONV) * wrep[w][None]
        yc[pl.ds(pl.multiple_of(r0, CONV_CHUNK), CONV_CHUNK), :] = acc.reshape(CONV_CHUNK, D_CONV)
        return carry

    lax.fori_loop(0, tc // CONV_CHUNK, chunk, 0)
    o_ref[...] = _conv_tail(yc[...], g_ref, b_ref)

    @pl.when(t == pl.num_programs(1) - 1)
    def _():
        ns_ref[0] = ext[tc:tc + CONV_PAD, :]


def _conv_vec_specs(l, nargs):
    return [pl.BlockSpec((None, 1, D_CONV), lambda *_: (l, 0, 0)) for _ in range(nargs)]


def _conv_prompt(z, state_pad, cw, cb, g, b, l, bp, tp, tc):
    nt = tp // tc
    return pl.pallas_call(
        _conv_prompt_kernel,
        grid=(bp, nt),
        in_specs=[pl.BlockSpec((tc, D_CONV), lambda bb, t: (bb * nt + t, Z_UA // D_CONV)),
                  pl.BlockSpec((tc, D_CONV), lambda bb, t: (bb * nt + t, Z_UB // D_CONV)),
                  pl.BlockSpec((1, CONV_PAD, D_CONV), lambda bb, t: (bb, 0, 0)),
                  pl.BlockSpec((None, CONV_PAD, D_CONV), lambda bb, t: (l, 0, 0))] + _conv_vec_specs(l, 3),
        out_specs=[pl.BlockSpec((tc, D_CONV), lambda bb, t: (bb * nt + t, 0)),
                   pl.BlockSpec((1, CONV_PAD, D_CONV), lambda bb, t: (bb, 0, 0))],
        out_shape=[jax.ShapeDtypeStruct((bp * tp, D_CONV), BF16),
                   jax.ShapeDtypeStruct((bp, CONV_PAD, D_CONV), F32)],
        scratch_shapes=[pltpu.VMEM((tc + CONV_PAD, D_CONV), F32),
                        pltpu.VMEM((7, tc + CONV_PAD - 8, D_CONV), F32),
                        pltpu.VMEM((tc, D_CONV), F32),
                        pltpu.VMEM((CONV_PAD, 8, D_CONV), F32)],
        compiler_params=_cparams("parallel", "arbitrary"),
        name="conv_prompt",
    )(z, z, state_pad, cw, cb, g, b)


def _conv_sample_kernel(ua_ref, ub_ref, st_ref, cw_ref, cb_ref, g_ref, b_ref, o_ref, a_ref):
    ts = ua_ref.shape[0]
    ns = CONV_W - 1
    a = ua_ref[...].astype(F32) * _sigmoid(ub_ref[...].astype(F32))
    a_ref[...] = a
    st = st_ref[...]
    row = lax.broadcasted_iota(jnp.int32, (ns, D_CONV), 0)
    for t in range(ts):
        wt = jnp.zeros((ns, D_CONV), F32)
        for j in range(t, ns):
            wt = jnp.where(row == j, cw_ref[j - t:j - t + 1, :], wt)
        yc = jnp.sum(st * wt[None], axis=1) + cb_ref[...]
        for t2 in range(t + 1):
            wi = CONV_W - 1 - (t - t2)
            yc = yc + a[t2] * cw_ref[wi:wi + 1, :]
        o_ref[t] = _conv_tail(yc, g_ref, b_ref)


def _conv_sample(zs3, state, cw, cb, g, b, l, bs_blk):
    ts, bs, _ = zs3.shape
    ns = CONV_W - 1
    return pl.pallas_call(
        _conv_sample_kernel,
        grid=(bs // bs_blk,),
        in_specs=[pl.BlockSpec((ts, bs_blk, D_CONV), lambda i: (0, i, Z_UA // D_CONV)),
                  pl.BlockSpec((ts, bs_blk, D_CONV), lambda i: (0, i, Z_UB // D_CONV)),
                  pl.BlockSpec((None, bs_blk, ns, D_CONV), lambda i: (l, i, 0, 0)),
                  pl.BlockSpec((None, CONV_PAD, D_CONV), lambda i: (l, 0, 0))] + _conv_vec_specs(l, 3),
        out_specs=[pl.BlockSpec((ts, bs_blk, D_CONV), lambda i: (0, i, 0)),
                   pl.BlockSpec((ts, bs_blk, D_CONV), lambda i: (0, i, 0))],
        out_shape=[jax.ShapeDtypeStruct((ts, bs, D_CONV), BF16),
                   jax.ShapeDtypeStruct((ts, bs, D_CONV), F32)],
        compiler_params=_cparams("parallel"),
        name="conv_sample",
    )(zs3, zs3, state, cw, cb, g, b)


def _t5_bucket(dist):
    max_exact = N_BUCKETS // 2
    d = jnp.maximum(dist, 0)
    large = max_exact + (jnp.log(jnp.maximum(d, 1).astype(F32) / max_exact)
                         / math.log(MAX_DIST / max_exact) * (N_BUCKETS - max_exact)).astype(jnp.int32)
    return jnp.where(d < max_exact, d, jnp.minimum(large, N_BUCKETS - 1))


def _bias_heads(rel_bias, dist, valid):
    onehot = (_t5_bucket(dist)[..., None] == jnp.arange(N_BUCKETS)).astype(F32)
    bias = jnp.einsum("qkb,bh->qkh", onehot, rel_bias.astype(F32), precision=lax.Precision.HIGHEST)
    bias = jnp.where(valid[..., None], bias, NEG_INF)
    return jnp.transpose(bias, (2, 0, 1))


def _attn_prompt_kernel(sink_ref, q_ref, kc_ref, kp_ref, vc_ref, vp_ref, bias_ref, o_ref, *, l):
    first = pl.program_id(1) == 0
    w = WINDOW
    nq = q_ref.shape[0] // w
    kall = jnp.concatenate([kp_ref[...], kc_ref[...]], axis=0).astype(F32)
    vall = jnp.concatenate([vp_ref[...], vc_ref[...]], axis=0).astype(F32)
    lane = lax.broadcasted_iota(jnp.int32, kall.shape, 1)
    lo = lane < HEAD_DIM
    kroll = pltpu.roll(kall, HEAD_DIM, 1)
    vroll = pltpu.roll(vall, HEAD_DIM, 1)

    def halves(a, aroll, g):
        if g == 0:
            return jnp.where(lo, a, 0.0).astype(BF16), jnp.where(lo, 0.0, aroll).astype(BF16)
        return jnp.where(lo, aroll, 0.0).astype(BF16), jnp.where(lo, 0.0, a).astype(BF16)

    kh = [halves(kall, kroll, g) for g in range(N_KV)]
    vh = [halves(vall, vroll, g) for g in range(N_KV)]
    col = lax.broadcasted_iota(jnp.int32, (w, 4 * w), 1)
    prev_col = (col % (2 * w)) < w
    tiles_per_g = Q_PER_KV // 2
    for qi in range(nq):
        r0 = qi * w
        for tile in range(N_HEADS // 2):
            g = tile // tiles_per_g
            q = q_ref[r0:r0 + w, tile * LANES:(tile + 1) * LANES]
            kk = jnp.concatenate([kh[g][0][r0:r0 + 2 * w], kh[g][1][r0:r0 + 2 * w]], axis=0)
            vv = jnp.concatenate([vh[g][0][r0:r0 + 2 * w], vh[g][1][r0:r0 + 2 * w]], axis=0)
            s = _dot_nt(q, kk) * (HEAD_DIM ** -0.5) + bias_ref[tile]
            if qi == 0:
                s = jnp.where(jnp.logical_and(first, prev_col), NEG_INF, s)
            ps = []
            for half in range(2):
                sh = s[:, half * 2 * w:(half + 1) * 2 * w]
                sink = sink_ref[l, 2 * tile + half]
                mx = jnp.maximum(jnp.max(sh, axis=-1, keepdims=True), sink)
                p = jnp.exp(sh - mx)
                den = jnp.sum(p, axis=-1, keepdims=True) + jnp.exp(sink - mx)
                ps.append((p * (1.0 / den)).astype(BF16))
            o = _dot(jnp.concatenate(ps, axis=1), vv)
            o_ref[r0:r0 + w, tile * LANES:(tile + 1) * LANES] = o.astype(BF16)


def _attn_prompt(z, bias, sinks, l, bp, tp):
    w = WINDOW
    qb = ATT_QB if tp % (ATT_QB * w) == 0 else 1
    ns = tp // (qb * w)
    nb = tp // w
    kvw = N_KV * HEAD_DIM
    qw = N_HEADS * HEAD_DIM

    def cur(col):
        return lambda bb, i: (bb * ns + i, col)

    def prev(col):
        return lambda bb, i: (bb * nb + jnp.maximum(i * qb - 1, 0), col)

    return pl.pallas_call(
        functools.partial(_attn_prompt_kernel, l=l),
        grid=(bp, ns),
        in_specs=[pl.BlockSpec(memory_space=pltpu.SMEM),
                  pl.BlockSpec((qb * w, qw), cur(Z_Q // qw)),
                  pl.BlockSpec((qb * w, kvw), cur(Z_K // kvw)),
                  pl.BlockSpec((w, kvw), prev(Z_K // kvw)),
                  pl.BlockSpec((qb * w, kvw), cur(Z_V // kvw)),
                  pl.BlockSpec((w, kvw), prev(Z_V // kvw)),
                  pl.BlockSpec((N_HEADS // 2, w, 4 * w), lambda bb, i: (0, 0, 0))],
        out_specs=pl.BlockSpec((qb * w, qw), lambda bb, i: (bb * ns + i, 0)),
        out_shape=jax.ShapeDtypeStruct((bp * tp, qw), BF16),
        compiler_params=_cparams("parallel", "parallel"),
        name="attn_prompt",
    )(sinks, z, z, z, z, z, bias)


def _attn_sample_kernel(q_ref, kc_ref, vc_ref, kn_ref, vn_ref, bias_ref, sink_ref, o_ref):
    wb = kc_ref.shape[1]
    ts = kn_ref.shape[2]
    for g in range(N_KV):
        lo = g * HEAD_DIM
        qb = (q_ref[:, g].astype(F32) * (HEAD_DIM ** -0.5)).astype(BF16)
        kc = kc_ref[:, :, lo:lo + HEAD_DIM].astype(BF16)
        vc = vc_ref[:, :, lo:lo + HEAD_DIM].astype(BF16)
        kn = kn_ref[:, g].astype(F32)
        vn = vn_ref[:, g].astype(F32)
        bias = bias_ref[g]
        s_c = jnp.einsum("bqd,bkd->bqk", qb, kc, preferred_element_type=F32) + bias[None, :, :wb]
        qf = qb.astype(F32)
        s_n = [jnp.sum(qf * kn[:, j:j + 1, :], axis=-1, keepdims=True) + bias[None, :, wb + j:wb + j + 1]
               for j in range(ts)]
        sink = sink_ref[g][None]
        mx = jnp.maximum(jnp.max(s_c, axis=-1, keepdims=True), sink)
        for sj in s_n:
            mx = jnp.maximum(mx, sj)
        p_c = jnp.exp(s_c - mx)
        p_n = [jnp.exp(sj - mx) for sj in s_n]
        den = jnp.sum(p_c, axis=-1, keepdims=True) + jnp.exp(sink - mx)
        for pj in p_n:
            den = den + pj
        o = jnp.einsum("bqk,bkd->bqd", (p_c / den).astype(BF16), vc, preferred_element_type=F32)
        for j in range(ts):
            o = o + (p_n[j] / den).astype(BF16).astype(F32) * vn[:, j:j + 1, :]
        o_ref[:, g] = o


def _attn_sample(q4, kc, vc, kn, vn, bias, sinks, l, bs_blk):
    bs, _, rt, _ = q4.shape
    wb = kc.shape[2]
    ts = kn.shape[2]
    kvw = N_KV * HEAD_DIM
    return pl.pallas_call(
        _attn_sample_kernel,
        grid=(bs // bs_blk,),
        in_specs=[pl.BlockSpec((bs_blk, N_KV, rt, HEAD_DIM), lambda i: (i, 0, 0, 0)),
                  pl.BlockSpec((None, bs_blk, wb, kvw), lambda i: (l, i, 0, 0)),
                  pl.BlockSpec((None, bs_blk, wb, kvw), lambda i: (l, i, 0, 0)),
                  pl.BlockSpec((bs_blk, N_KV, ts, HEAD_DIM), lambda i: (i, 0, 0, 0)),
                  pl.BlockSpec((bs_blk, N_KV, ts, HEAD_DIM), lambda i: (i, 0, 0, 0)),
                  pl.BlockSpec((N_KV, rt, wb + ts), lambda i: (0, 0, 0)),
                  pl.BlockSpec((N_KV, rt, 1), lambda i: (0, 0, 0))],
        out_specs=pl.BlockSpec((bs_blk, N_KV, rt, HEAD_DIM), lambda i: (i, 0, 0, 0)),
        out_shape=jax.ShapeDtypeStruct((bs, N_KV, rt, HEAD_DIM), F32),
        compiler_params=_cparams("parallel"),
        name="attn_sample",
    )(q4, kc, vc, kn, vn, bias, sinks)


def _mlstm_kernel(*refs, t_valid, nseq, aliased, carried, first_layer=None):
    if aliased:
        refs = refs[:10] + refs[11:]
    (if_ref, ifr_ref, q_ref, k_ref, v_ref, o_ref, g_ref, c0_ref, n0_ref, m0_ref,
     h_ref, c1_ref, n1_ref, m1_ref) = refs[:14]
    c = pl.program_id(1)
    L = ifr_ref.shape[-1]
    if carried:
        c_in, n_in, m_in = c_out, n_out, m_out = refs[14:]

        @pl.when(c == 0)
        def _():
            c_in[...] = c0_ref[...]
            n_in[...] = n0_ref[...]
            m_in[...] = m0_ref[...]
    else:
        (c_in, n_in, m_in), (c_out, n_out, m_out) = (c0_ref, n0_ref, m0_ref), (c1_ref, n1_ref, m1_ref)
        if first_layer is not None:
            for dd in range(c1_ref.shape[0]):
                if dd != first_layer:
                    c1_ref[dd] = jnp.zeros(c1_ref.shape[1:], F32)
            c_out = c1_ref.at[first_layer]

    def seq(ref, s):
        return ref.at[s] if len(ref.shape) == 3 else ref

    tt = lax.broadcasted_iota(jnp.int32, (L, L), 0)
    ss = lax.broadcasted_iota(jnp.int32, (L, L), 1)
    causal = ss <= tt
    tril = causal.astype(F32)
    triu = (tt <= ss).astype(F32)
    for s_i in range(nseq):
        ifc = seq(if_ref, s_i)[...]
        ifr = ifr_ref[s_i]
        qr, kr, vr, orr, hr = (seq(r, s_i) for r in (q_ref, k_ref, v_ref, o_ref, h_ref))
        lf_c = _log_sigmoid(ifc)
        lf_r = _log_sigmoid(ifr)
        i_c, i_r = ifc, ifr
        if t_valid < L:
            rc = lax.broadcasted_iota(jnp.int32, (L, LANES), 0) < t_valid
            rr = lax.broadcasted_iota(jnp.int32, (2 * M_HEADS, L), 1) < t_valid
            lf_c = jnp.where(rc, lf_c, 0.0)
            lf_r = jnp.where(rr, lf_r, 0.0)
            i_c = jnp.where(rc, i_c, NEG_INF)
            i_r = jnp.where(rr, i_r, NEG_INF)
        f_c = _dot_hi(tril, lf_c)
        f_r = _dot_hi(lf_r, triu)
        state = [(c_in[s_i, h], n_in[s_i, h], m_in[s_i, h:h + 1, 0:1]) for h in range(M_HEADS)]
        new_state = []
        for h in range(M_HEADS):
            lo = h * M_DK
            cm, nrow, m0 = state[h]
            fc = f_c[:, M_HEADS + h:M_HEADS + h + 1]
            fr = f_r[M_HEADS + h:M_HEADS + h + 1, :]
            ir = i_r[h:h + 1, :]
            ic = i_c[:, h:h + 1]
            dm = jnp.where(causal, fc - fr + ir, NEG_INF)
            m_t = jnp.maximum(m0 + fc, jnp.max(dm, axis=-1, keepdims=True))
            inter = jnp.exp(m0 + fc - m_t)
            qb = qr[:, lo:lo + M_DK]
            qf = qb.astype(F32)
            kf = kr[:, lo:lo + M_DK].astype(F32) * (M_DK ** -0.5)
            vf = vr[:, lo:lo + M_DV].astype(F32)
            kb, vb = kf.astype(BF16), vf.astype(BF16)
            sc = _dot_nt(qb, kb) * jnp.exp(dm - m_t)
            num = inter * _dot(qb, cm.astype(BF16)) + _dot(sc.astype(BF16), vb)
            qn = inter * jnp.sum(qf * nrow, axis=-1, keepdims=True) + jnp.sum(sc, axis=-1, keepdims=True)
            hh = num / jnp.maximum(jnp.abs(qn), jnp.exp(-m_t))
            m_end = m_t[L - 1:L, :]
            f_end = fc[L - 1:L, :]
            decay = jnp.exp(m0 + f_end - m_end)
            w_s = jnp.exp(f_end - fc + ic - m_end)
            new_state.append((decay * cm + _dot_tn(kb, (w_s * vf).astype(BF16)),
                              decay * nrow + jnp.sum(w_s * kf, axis=0, keepdims=True),
                              jnp.broadcast_to(m_end, (1, LANES))))
            hn = _norm(hh) * g_ref[:, lo:lo + M_DV]
            hr[:, lo:lo + M_DV] = (_sigmoid(orr[:, lo:lo + M_DV].astype(F32)) * hn).astype(BF16)
        for h in range(M_HEADS):
            c_out[s_i, h], n_out[s_i, h], m_out[s_i, h:h + 1, :] = new_state[h]
        if not carried:
            m_out[s_i, M_HEADS:, :] = jnp.zeros((M_HEADS, LANES), F32)

    if carried:
        @pl.when(c == pl.num_programs(1) - 1)
        def _():
            c1_ref[...] = c_out[...]
            n1_ref[...] = n_out[...]
            m1_ref[...] = m_out[...]


def _mlstm_state_specs(nseq, l_state):
    if l_state is None:
        c_spec = pl.BlockSpec((nseq, M_HEADS, M_DK, M_DV), lambda b, c: (b, 0, 0, 0))
    else:
        c_spec = pl.BlockSpec((None, nseq, M_HEADS, M_DK, M_DV), lambda b, c: (l_state, b, 0, 0, 0))
    return (c_spec,
            pl.BlockSpec((nseq, M_HEADS, 1, M_DK), lambda b, c: (b, 0, 0, 0)),
            pl.BlockSpec((nseq, 2 * M_HEADS, LANES), lambda b, c: (b, 0, 0)))


def _mlstm_scratch(nseq):
    return [pltpu.VMEM((nseq, M_HEADS, M_DK, M_DV), F32),
            pltpu.VMEM((nseq, M_HEADS, 1, M_DK), F32),
            pltpu.VMEM((nseq, 2 * M_HEADS, LANES), F32)]


def _mlstm_prompt(z, zif, ifr, gamma, l, bp, nc, L):
    hw = M_HEADS * M_DK
    zero = lambda *s: jnp.zeros(s, F32)

    def rows(col):
        return lambda b, c: (b * nc + c, col)

    kern = functools.partial(_mlstm_kernel, t_valid=L, nseq=1, aliased=False, carried=True)
    return pl.pallas_call(
        kern,
        grid=(bp, nc),
        in_specs=[pl.BlockSpec((L, LANES), rows(0)),
                  pl.BlockSpec((1, 2 * M_HEADS, L), lambda b, c: (b * nc + c, 0, 0)),
                  pl.BlockSpec((L, hw), rows(Z_MQ // hw)),
                  pl.BlockSpec((L, hw), rows(Z_MK // hw)),
                  pl.BlockSpec((L, hw), rows(Z_MV // hw)),
                  pl.BlockSpec((L, hw), rows(Z_MO // hw)),
                  pl.BlockSpec((None, 1, hw), lambda b, c: (l, 0, 0)),
                  *_mlstm_state_specs(1, None)],
        out_specs=[pl.BlockSpec((L, hw), lambda b, c: (b * nc + c, 0)),
                   *_mlstm_state_specs(1, None)],
        out_shape=[jax.ShapeDtypeStruct((bp * nc * L, hw), BF16),
                   jax.ShapeDtypeStruct((bp, M_HEADS, M_DK, M_DV), F32),
                   jax.ShapeDtypeStruct((bp, M_HEADS, 1, M_DK), F32),
                   jax.ShapeDtypeStruct((bp, 2 * M_HEADS, LANES), F32)],
        scratch_shapes=_mlstm_scratch(1),
        compiler_params=_cparams("parallel", "arbitrary"),
        name="mlstm_prompt",
    )(zif, ifr, z, z, z, z, gamma, zero(bp, M_HEADS, M_DK, M_DV), zero(bp, M_HEADS, 1, M_DK),
      zero(bp, 2 * M_HEADS, LANES))


def _mlstm_sample(zm3, if3, ifr, gamma, c_all, n0, m0x, c_out_prev, l, t_valid, nseq):
    bs, L, _ = zm3.shape
    depth = c_all.shape[0]
    hw = M_HEADS * M_DK
    aliased = c_out_prev is not None

    def blk(col):
        return pl.BlockSpec((nseq, L, hw), lambda b, c: (b, 0, col))

    c_in, n_spec, m_spec = _mlstm_state_specs(nseq, l)
    in_specs = [pl.BlockSpec((nseq, L, LANES), lambda b, c: (b, 0, 0)),
                pl.BlockSpec((nseq, 2 * M_HEADS, L), lambda b, c: (b, 0, 0)),
                blk(0), blk(1), blk(2), blk(3),
                pl.BlockSpec((None, 1, hw), lambda b, c: (l, 0, 0)),
                c_in, n_spec, m_spec]
    args = [if3, ifr, zm3, zm3, zm3, zm3, gamma, c_all, n0, m0x]
    aliases = {}
    if aliased:
        in_specs.append(pl.BlockSpec(memory_space=pl.ANY))
        args.append(c_out_prev)
        aliases = {len(args) - 1: 1}
        c_out = c_in
    else:
        c_out = pl.BlockSpec((depth, nseq, M_HEADS, M_DK, M_DV), lambda b, c: (0, b, 0, 0, 0))
    kern = functools.partial(_mlstm_kernel, t_valid=t_valid, nseq=nseq, aliased=aliased, carried=False,
                             first_layer=None if aliased else l)
    return pl.pallas_call(
        kern,
        grid=(bs // nseq, 1),
        in_specs=in_specs,
        out_specs=[pl.BlockSpec((nseq, L, hw), lambda b, c: (b, 0, 0)), c_out, n_spec, m_spec],
        out_shape=[jax.ShapeDtypeStruct((bs, L, hw), BF16),
                   jax.ShapeDtypeStruct((depth, bs, M_HEADS, M_DK, M_DV), F32),
                   jax.ShapeDtypeStruct((bs, M_HEADS, 1, M_DK), F32),
                   jax.ShapeDtypeStruct((bs, 2 * M_HEADS, LANES), F32)],
        input_output_aliases=aliases,
        compiler_params=_cparams("parallel", "arbitrary"),
        name="mlstm_sample",
    )(*args)


def _m_state_in(m):
    lead = m.shape[:-1]
    mx = jnp.zeros(lead + (2 * M_HEADS, LANES), F32)
    return mx.at[..., :M_HEADS, :].set(jnp.broadcast_to(m[..., None], lead + (M_HEADS, LANES)))


def _merge_kernel(*refs, np_tiles, alpha, route):
    (cp_ref, cs_ref, ap_ref, as_ref, mp_ref, ms_ref, g0_ref, g1_ref, g2_ref, x_ref,
     gtp_ref, gts_ref, shp_ref, shs_ref, scp_ref, scs_ref,
     wc_ref, wa_ref, wm_ref, wo_ref, pg_ref, pb_ref) = refs[:22]
    if route:
        rw_ref, rb_ref, x1_ref, h_ref, route_ref = refs[22:]
    else:
        x1_ref, h_ref = refs[22:]
    is_s = pl.program_id(0) >= np_tiles

    def gate(ref):
        return _sigmoid(ref[...].astype(F32))

    y = (gate(g0_ref) * _dot(_pick(is_s, cp_ref, cs_ref), wc_ref[...])
         + gate(g1_ref) * _dot(_pick(is_s, ap_ref, as_ref), wa_ref[...])
         + gate(g2_ref) * _dot(_pick(is_s, mp_ref, ms_ref), wm_ref[...]))
    mix = _dot(y.astype(BF16), wo_ref[...])
    x1 = _norm(alpha * x_ref[...] + _pick(is_s, gtp_ref, gts_ref) * mix) * pg_ref[...] + pb_ref[...]
    x1_ref[...] = x1
    h = _norm(x1) * (1.0 + _pick(is_s, scp_ref, scs_ref)) + _pick(is_s, shp_ref, shs_ref)
    h_ref[...] = h.astype(h_ref.dtype)
    if route:
        h_hi, h_lo = _split_bf16(h)
        w_hi, w_lo = _split_bf16(rw_ref[...])
        logits = _dot(h_hi, w_hi) + (_dot(h_hi, w_lo) + _dot(h_lo, w_hi)) + rb_ref[...]
        lane = lax.broadcasted_iota(jnp.int32, logits.shape, 1)
        m1 = jnp.max(logits, axis=-1, keepdims=True)
        e1 = jnp.min(jnp.where(logits == m1, lane, LANES), axis=-1, keepdims=True)
        l2 = jnp.where(lane == e1, NEG_INF, logits)
        m2 = jnp.max(l2, axis=-1, keepdims=True)
        e2 = jnp.min(jnp.where(l2 == m2, lane, LANES), axis=-1, keepdims=True)
        ex = jnp.exp(m2 - m1)
        w1 = 1.0 / (1.0 + ex)
        w2 = ex / (1.0 + ex)
        out = jnp.where(lane == 0, e1.astype(F32),
                        jnp.where(lane == 1, e2.astype(F32),
                                  jnp.where(lane == 2, w1, jnp.where(lane == 3, w2, 0.0))))
        route_ref[...] = out


def _merge(cp, cs, ap, as_, mp, ms, z, x, mod_p, mod_s, wc, wa, wm, wo, pg, pb, l, rm, alpha, router):
    n, d = x.shape
    hw = D_CONV
    route = router is not None
    tm = rm.t

    def pblk():
        return pl.BlockSpec((tm, hw), lambda i: (rm.prompt(i), 0))

    def sblk():
        return pl.BlockSpec((tm, hw), lambda i: (rm.sample(i), 0))

    def zg(k):
        return pl.BlockSpec((tm, d), lambda i: (i, Z_G // d + k))

    def lw(a):
        return pl.BlockSpec((None,) + a.shape[1:], lambda i: (l,) + (0,) * (a.ndim - 1))

    post = pl.BlockSpec((None, None, 1, d), lambda i: (l, 0, 0, 0))
    in_specs = [pblk(), sblk(), pblk(), sblk(), pblk(), sblk(), zg(0), zg(1), zg(2),
                pl.BlockSpec((tm, d), lambda i: (i, 0)),
                *rm.mod_specs(l, 2, d), *rm.mod_specs(l, 3, d), *rm.mod_specs(l, 4, d),
                lw(wc), lw(wa), lw(wm), lw(wo), post, post]
    args = [cp, cs, ap, as_, mp, ms, z, z, z, x, mod_p, mod_s, mod_p, mod_s, mod_p, mod_s,
            wc, wa, wm, wo, pg, pb]
    out_specs = [pl.BlockSpec((tm, d), lambda i: (i, 0)), pl.BlockSpec((tm, d), lambda i: (i, 0))]
    out_shape = [jax.ShapeDtypeStruct((n, d), F32), jax.ShapeDtypeStruct((n, d), F32 if route else BF16)]
    if route:
        rw, rb, lj = router
        in_specs += [pl.BlockSpec((None, d, LANES), lambda i: (lj, 0, 0)),
                     pl.BlockSpec((None, 1, LANES), lambda i: (lj, 0, 0))]
        args += [rw, rb]
        out_specs.append(pl.BlockSpec((tm, LANES), lambda i: (i, 0)))
        out_shape.append(jax.ShapeDtypeStruct((n, LANES), F32))
    kern = functools.partial(_merge_kernel, np_tiles=rm.np_tiles, alpha=alpha, route=route)
    return pl.pallas_call(
        kern, grid=(n // tm,), in_specs=in_specs, out_specs=out_specs, out_shape=out_shape,
        compiler_params=_cparams("parallel"), name="merge_route" if route else "merge",
    )(*args)


def _post_residual(x_ref, gp_ref, gs_ref, f, pg_ref, pb_ref, is_s, alpha):
    return _norm(alpha * x_ref[...] + _pick(is_s, gp_ref, gs_ref) * f) * pg_ref[...] + pb_ref[...]


def _ffn_kernel(h_ref, wg_ref, wu_ref, wd_ref, x_ref, gp_ref, gs_ref, pg_ref, pb_ref, o_ref, acc,
                *, alpha, np_tiles):
    j = pl.program_id(1)
    h = h_ref[...]
    a = (_silu(_dot(h, wg_ref[...])) * _dot(h, wu_ref[...])).astype(BF16)
    part = _dot(a, wd_ref[...])

    @pl.when(j == 0)
    def _():
        acc[...] = part

    @pl.when(j > 0)
    def _():
        acc[...] = acc[...] + part

    @pl.when(j == pl.num_programs(1) - 1)
    def _():
        is_s = pl.program_id(0) >= np_tiles
        o_ref[...] = _post_residual(x_ref, gp_ref, gs_ref, acc[...], pg_ref, pb_ref, is_s, alpha)


def _ffn_dense(h, wg, wu, wd, x, mod_p, mod_s, pg, pb, l, lj, rm, tf, alpha):
    n, d = x.shape
    f = wg.shape[2]
    tm = rm.t
    post = pl.BlockSpec((None, None, 1, d), lambda i, j: (l, 1, 0, 0))
    return pl.pallas_call(
        functools.partial(_ffn_kernel, alpha=alpha, np_tiles=rm.np_tiles),
        grid=(n // tm, f // tf),
        in_specs=[pl.BlockSpec((tm, d), lambda i, j: (i, 0)),
                  pl.BlockSpec((None, d, tf), lambda i, j: (lj, 0, j)),
                  pl.BlockSpec((None, d, tf), lambda i, j: (lj, 0, j)),
                  pl.BlockSpec((None, tf, d), lambda i, j: (lj, j, 0)),
                  pl.BlockSpec((tm, d), lambda i, j: (i, 0)),
                  *rm.mod_specs(l, 5, d), post, post],
        out_specs=pl.BlockSpec((tm, d), lambda i, j: (i, 0)),
        out_shape=jax.ShapeDtypeStruct((n, d), F32),
        scratch_shapes=[pltpu.VMEM((tm, d), F32)],
        compiler_params=_cparams("parallel", "arbitrary"),
        name="ffn_dense",
    )(h, wg, wu, wd, x, mod_p, mod_s, pg, pb)


def _rank_kernel(route_ref, rank_ref, tot_ref, carry):
    i = pl.program_id(0)
    tm = route_ref.shape[0]

    @pl.when(i == 0)
    def _():
        carry[...] = jnp.zeros_like(carry)

    r = route_ref[...]
    lane = lax.broadcasted_iota(jnp.int32, (tm, LANES), 1)
    e1 = r[:, 0:1].astype(jnp.int32)
    e2 = r[:, 1:2].astype(jnp.int32)
    hit1 = lane == e1
    hit2 = lane == e2
    onehot = jnp.where(jnp.logical_or(hit1, hit2), 1.0, 0.0)
    tt = lax.broadcasted_iota(jnp.int32, (tm, tm), 0)
    ss = lax.broadcasted_iota(jnp.int32, (tm, tm), 1)
    before = jnp.where(ss < tt, 1.0, 0.0).astype(BF16)
    cnt = _dot(before, onehot.astype(BF16)) + carry[0:1, :]
    r1 = jnp.sum(jnp.where(hit1, cnt, 0.0), axis=-1, keepdims=True)
    r2 = jnp.sum(jnp.where(hit2, cnt, 0.0), axis=-1, keepdims=True)
    rank_ref[...] = jnp.where(lane == 0, r1, jnp.where(lane == 1, r2, 0.0))
    carry[...] = carry[...] + jnp.sum(onehot, axis=0, keepdims=True)
    tot_ref[...] = carry[...]


def _moe_rank(route, tm):
    n = route.shape[0]
    return pl.pallas_call(
        _rank_kernel,
        grid=(n // tm,),
        in_specs=[pl.BlockSpec((tm, LANES), lambda i: (i, 0))],
        out_specs=[pl.BlockSpec((tm, LANES), lambda i: (i, 0)),
                   pl.BlockSpec((8, LANES), lambda i: (0, 0))],
        out_shape=[jax.ShapeDtypeStruct((n, LANES), F32), jax.ShapeDtypeStruct((8, LANES), F32)],
        scratch_shapes=[pltpu.VMEM((8, LANES), F32)],
        compiler_params=_cparams("arbitrary"),
        name="moe_rank",
    )(route)


def _row_copy(src, s, dst, t, sem):
    return pltpu.make_async_copy(src.at[pl.ds(s, 1)], dst.at[pl.ds(t, 1)], sem)


def _dispatch_kernel(d1_ref, d2_ref, zl_ref, h_ref, xs_hbm, zbuf, sem, zsem):
    i = pl.program_id(0)
    tm = h_ref.shape[0]

    @pl.when(i == 0)
    def _():
        zbuf[...] = jnp.zeros_like(zbuf)

        def zero_copy(k):
            row = pl.multiple_of(jnp.maximum(zl_ref[k], 0), MOE_SUB)
            return pltpu.make_async_copy(zbuf, xs_hbm.at[pl.ds(row, MOE_SUB)], zsem)

        def start(k, carry):
            @pl.when(zl_ref[k] >= 0)
            def _():
                zero_copy(k).start()
            return carry

        def wait(k, carry):
            @pl.when(zl_ref[k] >= 0)
            def _():
                zero_copy(k).wait()
            return carry

        lax.fori_loop(0, zl_ref.shape[0], start, 0)
        lax.fori_loop(0, zl_ref.shape[0], wait, 0)

    def issue(r, carry):
        _row_copy(h_ref, r, xs_hbm, d1_ref[i * tm + r], sem).start(priority=0)
        _row_copy(h_ref, r, xs_hbm, d2_ref[i * tm + r], sem).start(priority=1)
        return carry

    lax.fori_loop(0, tm, issue, 0)
    for _ in range(2):
        pltpu.make_async_copy(h_ref, xs_hbm.at[pl.ds(0, tm)], sem).wait()


def _moe_dispatch(dest1, dest2, zlist, h, n_rows, tm):
    n, d = h.shape
    grid_spec = pltpu.PrefetchScalarGridSpec(
        num_scalar_prefetch=3,
        grid=(n // tm,),
        in_specs=[pl.BlockSpec((tm, d), lambda i, *_: (i, 0))],
        out_specs=pl.BlockSpec(memory_space=pl.ANY),
        scratch_shapes=[pltpu.VMEM((MOE_SUB, d), F32), pltpu.SemaphoreType.DMA(()), pltpu.SemaphoreType.DMA(())],
    )
    return pl.pallas_call(
        _dispatch_kernel, grid_spec=grid_spec,
        out_shape=jax.ShapeDtypeStruct((n_rows, d), F32),
        compiler_params=_cparams("arbitrary"),
        name="moe_dispatch",
    )(dest1, dest2, zlist, h)


def _expert_kernel(blk_e_ref, nvalid_ref, xs_ref, wg_ref, wu_ref, wd_ref, y_ref, xb, acc):
    b = pl.program_id(0)
    j = pl.program_id(1)
    tb = xb.shape[0]
    nsub = tb // MOE_SUB
    nv = nvalid_ref[b]
    wg = wg_ref[...].astype(BF16)
    wu = wu_ref[...].astype(BF16)
    wd = wd_ref[...].astype(BF16)

    def run(rows):
        @pl.when(j == 0)
        def _():
            xb[rows, :] = xs_ref[rows, :].astype(BF16)

        x = xb[rows, :]
        a = (_silu(_dot(x, wg)) * _dot(x, wu)).astype(BF16)
        part = _dot(a, wd)

        @pl.when(j == 0)
        def _():
            acc[rows, :] = part

        @pl.when(j > 0)
        def _():
            acc[rows, :] = acc[rows, :] + part

        @pl.when(j == pl.num_programs(1) - 1)
        def _():
            y_ref[rows, :] = acc[rows, :]

    @pl.when(nv == tb)
    def _():
        run(pl.ds(0, tb))

    for s in range(nsub):
        rows = pl.ds(s * MOE_SUB, MOE_SUB)

        @pl.when(jnp.logical_and(s * MOE_SUB < nv, nv < tb))
        def _():
            run(rows)

        @pl.when(jnp.logical_and(s * MOE_SUB >= nv, j == pl.num_programs(1) - 1))
        def _():
            y_ref[rows, :] = jnp.zeros((MOE_SUB, y_ref.shape[1]), F32)


def _moe_experts(blk_e, nvalid, xs, wg, wu, wd, lj, tb, tf):
    n_rows, d = xs.shape
    f = wg.shape[3]
    nj = f // tf

    def jm(b, j, nv):
        return jnp.where(nv[b] > 0, j, nj - 1)

    grid_spec = pltpu.PrefetchScalarGridSpec(
        num_scalar_prefetch=2,
        grid=(n_rows // tb, nj),
        in_specs=[pl.BlockSpec((tb, d), lambda b, j, be, nv: (b, 0)),
                  pl.BlockSpec((None, None, d, tf), lambda b, j, be, nv: (lj, be[b], 0, jm(b, j, nv))),
                  pl.BlockSpec((None, None, d, tf), lambda b, j, be, nv: (lj, be[b], 0, jm(b, j, nv))),
                  pl.BlockSpec((None, None, tf, d), lambda b, j, be, nv: (lj, be[b], jm(b, j, nv), 0))],
        out_specs=pl.BlockSpec((tb, d), lambda b, j, be, nv: (b, 0)),
        scratch_shapes=[pltpu.VMEM((tb, d), BF16), pltpu.VMEM((tb, d), F32)],
    )
    return pl.pallas_call(
        _expert_kernel, grid_spec=grid_spec,
        out_shape=jax.ShapeDtypeStruct((n_rows, d), F32),
        compiler_params=_cparams("parallel", "arbitrary"),
        name="moe_experts",
    )(blk_e, nvalid, xs, wg, wu, wd)


def _combine_kernel(d1_ref, d2_ref, ys_hbm, route_ref, x_ref, gp_ref, gs_ref, pg_ref, pb_ref, o_ref,
                    y1, y2, sem1, sem2, *, alpha, np_tiles):
    i = pl.program_id(0)
    tm = x_ref.shape[0]

    def issue(r, carry):
        _row_copy(ys_hbm, d1_ref[i * tm + r], y1, r, sem1).start(priority=0)
        _row_copy(ys_hbm, d2_ref[i * tm + r], y2, r, sem2).start(priority=1)
        return carry

    lax.fori_loop(0, tm, issue, 0)
    pltpu.make_async_copy(ys_hbm.at[pl.ds(0, tm)], y1, sem1).wait()
    pltpu.make_async_copy(ys_hbm.at[pl.ds(0, tm)], y2, sem2).wait()
    r = route_ref[...]
    f = y1[...] * r[:, 2:3] + y2[...] * r[:, 3:4]
    o_ref[...] = _post_residual(x_ref, gp_ref, gs_ref, f, pg_ref, pb_ref, i >= np_tiles, alpha)


def _moe_combine(dest1, dest2, ys, route, x, mod_p, mod_s, pg, pb, l, rm, alpha):
    n, d = x.shape
    tm = rm.t
    post = pl.BlockSpec((None, None, 1, d), lambda i, *_: (l, 1, 0, 0))
    grid_spec = pltpu.PrefetchScalarGridSpec(
        num_scalar_prefetch=2,
        grid=(n // tm,),
        in_specs=[pl.BlockSpec(memory_space=pl.ANY),
                  pl.BlockSpec((tm, LANES), lambda i, *_: (i, 0)),
                  pl.BlockSpec((tm, d), lambda i, *_: (i, 0)),
                  *rm.mod_specs(l, 5, d), post, post],
        out_specs=pl.BlockSpec((tm, d), lambda i, *_: (i, 0)),
        scratch_shapes=[pltpu.VMEM((tm, d), F32), pltpu.VMEM((tm, d), F32),
                        pltpu.SemaphoreType.DMA(()), pltpu.SemaphoreType.DMA(())],
    )
    return pl.pallas_call(
        functools.partial(_combine_kernel, alpha=alpha, np_tiles=rm.np_tiles), grid_spec=grid_spec,
        out_shape=jax.ShapeDtypeStruct((n, d), F32),
        compiler_params=_cparams("arbitrary"),
        name="moe_combine",
    )(dest1, dest2, ys, route, x, mod_p, mod_s, pg, pb)


def _moe_ffn(h, route, wg, wu, wd, x, mod_p, mod_s, pg, pb, l, lj, rm, tb, tf, alpha):
    n = h.shape[0]
    rank, tot = _moe_rank(route, rm.t)
    counts = tot[0, :N_EXPERTS].astype(jnp.int32)
    padded = (counts + tb - 1) // tb * tb
    pad_end = jnp.cumsum(padded)
    pad_start = pad_end - padded
    experts = jnp.arange(N_EXPERTS, dtype=jnp.int32)

    def slot(col):
        e = route[:, col].astype(jnp.int32)
        start = jnp.sum(jnp.where(e[:, None] == experts[None, :], pad_start[None, :], 0), axis=1)
        return start + rank[:, col].astype(jnp.int32)

    dest1, dest2 = slot(0), slot(1)
    n_blocks = -(-(2 * n + N_EXPERTS * (tb - 1)) // tb)
    blk_start = jnp.arange(n_blocks, dtype=jnp.int32) * tb
    blk_e = jnp.minimum(jnp.sum(pad_end[None, :] <= blk_start[:, None], axis=1), N_EXPERTS - 1).astype(jnp.int32)
    own_end = jnp.sum(jnp.where(blk_e[:, None] == experts[None, :], (pad_start + counts)[None, :], 0), axis=1)
    nvalid = jnp.clip(own_end - blk_start, 0, tb).astype(jnp.int32)
    nvalid = jnp.where(blk_start < pad_end[-1], nvalid, 0)
    per_blk = tb // MOE_SUB
    sub_start = jnp.arange(n_blocks * per_blk, dtype=jnp.int32) * MOE_SUB
    sub_room = jnp.repeat(blk_start + nvalid, per_blk) - sub_start
    zlist = jnp.where(sub_room < MOE_SUB, sub_start, -1).astype(jnp.int32)
    xs = _moe_dispatch(dest1, dest2, zlist, h, n_blocks * tb, rm.t)
    ys = _moe_experts(blk_e, nvalid, xs, wg, wu, wd, lj, tb, tf)
    return _moe_combine(dest1, dest2, ys, route, x, mod_p, mod_s, pg, pb, l, rm, alpha)


def _pack_w_in(w_in, b_in):
    depth, d, _ = w_in.shape
    q_end = 2 * D_CONV + N_HEADS * HEAD_DIM
    k_end = q_end + N_KV * HEAD_DIM
    a_end = k_end + N_KV * HEAD_DIM
    m_end = a_end + 4 * M_HEADS * M_DK
    if_end = m_end + 2 * M_HEADS

    def pack(a):
        return jnp.concatenate([a[..., if_end:], a[..., :q_end], a[..., a_end:m_end], a[..., q_end:a_end]], axis=-1)

    def gates(a):
        return jnp.pad(a[..., m_end:if_end], [(0, 0)] * (a.ndim - 1) + [(0, LANES - 2 * M_HEADS)])

    return (pack(w_in).astype(BF16), pack(b_in).reshape(depth, 1, Z_W),
            gates(w_in).astype(BF16), gates(b_in).reshape(depth, 1, LANES))


def kernel(x_prompt, x_sample, cache_swa_k, cache_swa_v, state_conv, state_mlstm_C, state_mlstm_n, state_mlstm_m, c_prompt, c_sample, w_ada, b_ada, w_in, b_in, conv_w, conv_b, conv_ln_g, conv_ln_b, w_conv_out, attn_sinks, rel_bias, w_attn_out, m_norm_g, w_m_out, w_out, post_ln_g, post_ln_b, ffn_w_gate, ffn_w_up, ffn_w_down, router_w, router_b, moe_w_gate, moe_w_up, moe_w_down):
    bp, tp, d = x_prompt.shape
    bs, ts, _ = x_sample.shape
    depth = w_ada.shape[0]
    alpha = (2 * depth) ** 0.25
    n_p, n_s = bp * tp, bs * ts
    tm = n_s
    assert d == D_MODEL and tp % tm == 0 and tp % WINDOW == 0 and tm % 32 == 0
    rm = _RowMap(tm, bp, tp, n_p)
    rm_half = _RowMap(tm // 2, bp, tp, n_p)
    wb = cache_swa_k.shape[2]
    big = n_p >= 4096
    tc = 512 if big else tm
    lm = 256 if big else min(tp, 128)
    tb = 1024 if big else 2 * MOE_SUB
    bs_blk = 32 if bs % 32 == 0 else bs
    bs_att = 16 if bs % 16 == 0 else bs
    bs_m = 4 if bs % 4 == 0 else 1
    lts = 16

    x = jnp.concatenate([x_prompt.reshape(n_p, d), jnp.transpose(x_sample, (1, 0, 2)).reshape(n_s, d)], axis=0)

    nc_rows = -(-(bp + bs) // 8) * 8
    c_all = jnp.zeros((nc_rows, d), F32).at[:bp].set(c_prompt).at[bp:bp + bs].set(c_sample)
    mod = _ada_mod(c_all, w_ada, b_ada)
    mod_p = mod[:, :bp].reshape(depth, bp, 1, 6 * d)
    mod_s = jnp.tile(mod[:, bp:bp + bs], (1, ts, 1))

    w_in_p, b_in_p, w_if, b_if = _pack_w_in(w_in, b_in)
    wc_b, wa_b, wm_b, wo_b = (w.astype(BF16) for w in (w_conv_out, w_attn_out, w_m_out, w_out))
    fg_b, fu_b, fd_b = (w.astype(BF16) for w in (ffn_w_gate, ffn_w_up, ffn_w_down))
    cw_pad = jnp.pad(conv_w, ((0, 0), (0, CONV_PAD - CONV_W), (0, 0)))
    cvecs = [v.reshape(depth, 1, D_CONV) for v in (conv_b, conv_ln_g, conv_ln_b)]
    rw_pad = jnp.pad(router_w, ((0, 0), (0, 0), (0, LANES - N_EXPERTS)))
    rb_pad = jnp.pad(router_b, ((0, 0), (0, LANES - N_EXPERTS)), constant_values=NEG_INF)[:, None, :]
    pg = post_ln_g.reshape(depth, 2, 1, d)
    pb = post_ln_b.reshape(depth, 2, 1, d)
    gamma = m_norm_g.reshape(depth, 1, M_HEADS * M_DV)
    sinks = attn_sinks.astype(F32)

    qi = jnp.arange(WINDOW)[:, None]
    kj = jnp.arange(2 * WINDOW)[None, :]
    dist_p = qi + WINDOW - kj
    bh = _bias_heads(rel_bias, dist_p, (dist_p >= 0) & (dist_p < WINDOW))
    bias_p = bh.reshape(N_HEADS // 2, 2, WINDOW, 2 * WINDOW).transpose(0, 2, 1, 3).reshape(
        N_HEADS // 2, WINDOW, 4 * WINDOW)
    dist_s = jnp.arange(ts)[:, None] + wb - jnp.arange(wb + ts)[None, :]
    bias_s = _bias_heads(rel_bias, dist_s, (dist_s >= 0) & (dist_s < WINDOW)).reshape(
        N_KV, Q_PER_KV * ts, wb + ts)
    kc_all = cache_swa_k.reshape(depth, bs, wb, N_KV * HEAD_DIM)
    vc_all = cache_swa_v.reshape(depth, bs, wb, N_KV * HEAD_DIM)

    f_dense = ffn_w_gate.shape[2]
    tf_dense = f_dense // 2 if (f_dense // 2) % LANES == 0 else f_dense
    f_moe = moe_w_gate.shape[3]
    tf_moe = 512 if f_moe % 512 == 0 else f_moe

    n0_all = state_mlstm_n[:, :, :, None, :]
    m0_all = _m_state_in(state_mlstm_m)
    s_c = None
    new_p = [[] for _ in range(6)]
    new_s = [[] for _ in range(6)]
    for l in range(depth):
        j = l // 2
        z, zif = _ln_proj(x, mod_p, mod_s, w_in_p, b_in_p, w_if, b_if, l, rm)
        zs3 = z[n_p:].reshape(ts, bs, Z_W)

        cp, ns_p = _conv_prompt(z, jnp.zeros((bp, CONV_PAD, D_CONV), F32), cw_pad, *cvecs, l, bp, tp, tc)
        cs3, a_s3 = _conv_sample(zs3, state_conv, cw_pad, *cvecs, l, bs_blk)
        new_p[2].append(ns_p[:, CONV_PAD - CONV_W + 1:])
        new_s[2].append(jnp.concatenate([state_conv[l][:, ts:], jnp.transpose(a_s3, (1, 0, 2))], axis=1))

        sink_h = sinks[l].reshape(N_KV, Q_PER_KV, 1)
        sink_s = jnp.broadcast_to(sink_h, (N_KV, Q_PER_KV, ts)).reshape(N_KV, Q_PER_KV * ts, 1)
        ap = _attn_prompt(z, bias_p, sinks, l, bp, tp)
        nk = min(WINDOW, tp)
        kv_tail = jnp.stack([z[(b + 1) * tp - nk:(b + 1) * tp, Z_K:Z_K + 2 * N_KV * HEAD_DIM]
                             for b in range(bp)]).astype(F32)
        new_p[0].append(kv_tail[..., :N_KV * HEAD_DIM].reshape(bp, nk, N_KV, HEAD_DIM))
        new_p[1].append(kv_tail[..., N_KV * HEAD_DIM:].reshape(bp, nk, N_KV, HEAD_DIM))
        q_s = zs3[:, :, Z_Q:Z_Q + N_HEADS * HEAD_DIM].reshape(ts, bs, N_KV, Q_PER_KV, HEAD_DIM)
        q4 = jnp.transpose(q_s, (1, 2, 3, 0, 4)).reshape(bs, N_KV, Q_PER_KV * ts, HEAD_DIM)
        k_s = jnp.transpose(zs3[:, :, Z_K:Z_K + N_KV * HEAD_DIM].reshape(ts, bs, N_KV, HEAD_DIM), (1, 0, 2, 3))
        v_s = jnp.transpose(zs3[:, :, Z_V:Z_V + N_KV * HEAD_DIM].reshape(ts, bs, N_KV, HEAD_DIM), (1, 0, 2, 3))
        o4 = _attn_sample(q4, kc_all, vc_all, jnp.transpose(k_s, (0, 2, 1, 3)), jnp.transpose(v_s, (0, 2, 1, 3)),
                          bias_s, sink_s, l, bs_att)
        as_ = jnp.transpose(o4.reshape(bs, N_KV, Q_PER_KV, ts, HEAD_DIM), (3, 0, 1, 2, 4)).reshape(n_s, -1).astype(BF16)
        new_s[0].append(jnp.concatenate([cache_swa_k[l][:, ts:], k_s.astype(F32)], axis=1))
        new_s[1].append(jnp.concatenate([cache_swa_v[l][:, ts:], v_s.astype(F32)], axis=1))

        ncp = tp // lm
        if_p = zif[:n_p, :2 * M_HEADS].reshape(bp * ncp, lm, 2 * M_HEADS)
        mp, c1p, n1p, m1p = _mlstm_prompt(z, zif, jnp.transpose(if_p, (0, 2, 1)), gamma, l, bp, ncp, lm)
        new_p[3].append(c1p)
        new_p[4].append(n1p[:, :, 0])
        new_p[5].append(m1p[:, :M_HEADS, 0])
        tpad = ((0, 0), (0, lts - ts), (0, 0))
        zm3 = jnp.pad(jnp.transpose(zs3[:, :, Z_MQ:Z_K], (1, 0, 2)), tpad)
        if3 = jnp.pad(jnp.transpose(zif[n_p:].reshape(ts, bs, LANES), (1, 0, 2)), tpad)
        ms, s_c, n1s, m1s = _mlstm_sample(zm3, if3, jnp.transpose(if3[:, :, :2 * M_HEADS], (0, 2, 1)), gamma,
                                          state_mlstm_C, n0_all[l], m0_all[l], s_c, l, ts, bs_m)
        ms = jnp.transpose(ms[:, :ts], (1, 0, 2)).reshape(n_s, -1)
        new_s[4].append(n1s[:, :, 0])
        new_s[5].append(m1s[:, :M_HEADS, 0])

        moe = l % 2 == 1
        router = (rw_pad, rb_pad, j) if moe else None
        outs = _merge(cp, cs3.reshape(n_s, D_CONV), ap, as_, mp, ms, z, x, mod_p, mod_s,
                      wc_b, wa_b, wm_b, wo_b, pg, pb, l, rm_half, alpha, router)
        if moe:
            x1, h2, route = outs
            x = _moe_ffn(h2, route, moe_w_gate, moe_w_up, moe_w_down, x1, mod_p, mod_s, pg, pb,
                         l, j, rm, tb, tf_moe, alpha)
        else:
            x1, h2 = outs
            x = _ffn_dense(h2, fg_b, fu_b, fd_b, x1, mod_p, mod_s, pg, pb, l, j, rm, tf_dense, alpha)

    y_p = x[:n_p].reshape(bp, tp, d)
    y_s = jnp.transpose(x[n_p:].reshape(ts, bs, d), (1, 0, 2))
    p_k, p_v, p_conv, p_c, p_n, p_m = [jnp.stack(a) for a in new_p]
    s_k, s_v, s_conv = [jnp.stack(a) for a in new_s[:3]]
    s_n, s_m = jnp.stack(new_s[4]), jnp.stack(new_s[5])
    return (y_p, y_s, p_k, p_v, p_conv, p_c, p_n, p_m, s_k, s_v, s_conv, s_c, s_n, s_m)
```

```python
import functools
import math

import jax
import jax.numpy as jnp
from jax import lax
from jax.experimental import pallas as pl
from jax.experimental.pallas import tpu as pltpu

F32 = jnp.float32
BF16 = jnp.bfloat16

D_MODEL = 1024
D_CONV = 512
CONV_W = 31
CONV_PAD = 32
N_HEADS = 8
N_KV = 2
HEAD_DIM = 64
Q_PER_KV = N_HEADS // N_KV
WINDOW = 128
N_BUCKETS = 32
MAX_DIST = 128
M_HEADS = 4
M_DK = 128
M_DV = 128
N_EXPERTS = 8
LN_EPS = 1e-5
LANES = 128
NEG_INF = float("-inf")
VMEM_LIMIT = 56 * 1024 * 1024

Z_G, Z_UA, Z_UB, Z_Q = 0, 3072, 3584, 4096
Z_MQ, Z_MK, Z_MV, Z_MO = 4608, 5120, 5632, 6144
Z_K, Z_V, Z_W = 6656, 6784, 6912
TN_IN = 2304
ATT_QB = 4
ATT_GROUP = 8
MOE_SUB = 256
CONV_CHUNK = 32


def _cparams(*sem):
    return pltpu.CompilerParams(dimension_semantics=sem, vmem_limit_bytes=VMEM_LIMIT)


def _sigmoid(x):
    return 1.0 / (1.0 + jnp.exp(-x))


def _silu(x):
    return x * _sigmoid(x)


def _log_sigmoid(x):
    return jnp.minimum(x, 0.0) - jnp.log(1.0 + jnp.exp(-jnp.abs(x)))


def _norm(x):
    mu = jnp.mean(x, axis=-1, keepdims=True)
    xc = x - mu
    var = jnp.mean(xc * xc, axis=-1, keepdims=True)
    return xc * lax.rsqrt(var + LN_EPS)


def _dot(a, b):
    return jnp.dot(a, b, preferred_element_type=F32)


def _dot_nt(a, b):
    return lax.dot_general(a, b, (((1,), (1,)), ((), ())), preferred_element_type=F32)


def _dot_tn(a, b):
    return lax.dot_general(a, b, (((0,), (0,)), ((), ())), preferred_element_type=F32)


def _dot_hi(a, b):
    return jnp.dot(a, b, preferred_element_type=F32, precision=lax.Precision.HIGHEST)


def _split_bf16(a):
    hi = a.astype(BF16)
    return hi, (a - hi.astype(F32)).astype(BF16)


class _RowMap:
    def __init__(self, t, bp, tp, n_p):
        self.t = t
        self.bp = bp
        self.per_seq = tp // t
        self.np_tiles = n_p // t

    def seq(self, i):
        return jnp.minimum(i // self.per_seq, self.bp - 1)

    def prompt(self, i):
        return jnp.minimum(i, self.np_tiles - 1)

    def sample(self, i):
        return jnp.maximum(i - self.np_tiles, 0)

    def mod_specs(self, l, k, d):
        return (pl.BlockSpec((None, None, 1, d), lambda i, *_: (l, self.seq(i), 0, k)),
                pl.BlockSpec((None, self.t, d), lambda i, *_: (l, self.sample(i), k)))


def _pick(is_s, p_ref, s_ref):
    return jnp.where(is_s, s_ref[...], p_ref[...])


def _ada_kernel(c_ref, w_ref, b_ref, o_ref):
    s = _silu(c_ref[...]).astype(BF16)
    o_ref[0] = _dot(s, w_ref[0].astype(BF16)) + b_ref[0]


def _ada_mod(c_all, w_ada, b_ada):
    depth, d, n6 = w_ada.shape
    rows = c_all.shape[0]
    return pl.pallas_call(
        _ada_kernel,
        grid=(depth, n6 // d),
        in_specs=[pl.BlockSpec((rows, d), lambda l, j: (0, 0)),
                  pl.BlockSpec((1, d, d), lambda l, j: (l, 0, j)),
                  pl.BlockSpec((1, 1, d), lambda l, j: (l, 0, j))],
        out_specs=pl.BlockSpec((1, rows, d), lambda l, j: (l, 0, j)),
        out_shape=jax.ShapeDtypeStruct((depth, rows, n6), F32),
        compiler_params=_cparams("parallel", "parallel"),
        name="ada_mod",
    )(c_all, w_ada, b_ada.reshape(depth, 1, n6))


def _ln_proj_kernel(x_ref, shp_ref, shs_ref, scp_ref, scs_ref, w_ref, b_ref, wif_ref, bif_ref,
                    z_ref, zif_ref, h_scr, *, np_tiles):
    is_s = pl.program_id(0) >= np_tiles

    @pl.when(pl.program_id(1) == 0)
    def _():
        h = _norm(x_ref[...]) * (1.0 + _pick(is_s, scp_ref, scs_ref)) + _pick(is_s, shp_ref, shs_ref)
        h = h.astype(BF16)
        h_scr[...] = h
        zif_ref[...] = _dot(h, wif_ref[...]) + bif_ref[...]

    z_ref[...] = (_dot(h_scr[...], w_ref[...]) + b_ref[...]).astype(BF16)


def _ln_proj(x, mod_p, mod_s, w, b, wif, bif, l, rm):
    n, d = x.shape
    zw = w.shape[2]
    tm = rm.t
    shp, shs = rm.mod_specs(l, 0, d)
    scp, scs = rm.mod_specs(l, 1, d)
    return pl.pallas_call(
        functools.partial(_ln_proj_kernel, np_tiles=rm.np_tiles),
        grid=(n // tm, zw // TN_IN),
        in_specs=[pl.BlockSpec((tm, d), lambda i, j: (i, 0)),
                  shp, shs, scp, scs,
                  pl.BlockSpec((None, d, TN_IN), lambda i, j: (l, 0, j)),
                  pl.BlockSpec((None, 1, TN_IN), lambda i, j: (l, 0, j)),
                  pl.BlockSpec((None, d, LANES), lambda i, j: (l, 0, 0)),
                  pl.BlockSpec((None, 1, LANES), lambda i, j: (l, 0, 0))],
        out_specs=[pl.BlockSpec((tm, TN_IN), lambda i, j: (i, j)),
                   pl.BlockSpec((tm, LANES), lambda i, j: (i, 0))],
        out_shape=[jax.ShapeDtypeStruct((n, zw), BF16), jax.ShapeDtypeStruct((n, LANES), F32)],
        scratch_shapes=[pltpu.VMEM((tm, d), BF16)],
        compiler_params=_cparams("parallel", "arbitrary"),
        name="ln_proj",
    )(x, mod_p, mod_s, mod_p, mod_s, w, b, wif, bif)


def _conv_tail(yc, g_ref, b_ref):
    y = _norm(yc) * g_ref[...] + b_ref[...]
    return _silu(y).astype(BF16)


def _conv_prompt_kernel(ua_ref, ub_ref, st_ref, cw_ref, cb_ref, g_ref, b_ref, o_ref, ns_ref,
                        ext, shifted, yc, wrep):
    t = pl.program_id(1)
    tc = ua_ref.shape[0]
    sub = 8

    @pl.when(t == 0)
    def _():
        ext[0:CONV_PAD, :] = st_ref[0]

    @pl.when(t > 0)
    def _():
        ext[0:CONV_PAD, :] = ext[tc:tc + CONV_PAD, :]

    ext[CONV_PAD:, :] = ua_ref[...].astype(F32) * _sigmoid(ub_ref[...].astype(F32))
    for s in range(1, sub):
        shifted[s - 1] = ext[s:s + tc + CONV_PAD - sub, :]
    off = CONV_PAD - (CONV_W - 1)
    for w in range(CONV_W):
        wrep[w] = jnp.broadcast_to(cw_ref[w:w + 1, :], (sub, D_CONV))
    groups = CONV_CHUNK // sub

    def chunk(c, carry):
        r0 = c * CONV_CHUNK
        acc = jnp.broadcast_to(cb_ref[...].reshape(1, 1, D_CONV), (groups, sub, D_CONV))
        for w in range(CONV_W):
            base, s = (off + w) // sub * sub, (off + w) % sub
            src = ext if s == 0 else shifted.at[s - 1]
            win = src[pl.ds(pl.multiple_of(r0 + base, sub), CONV_CHUNK), :]
            acc = acc + win.reshape(groups, sub, D_CONV) * wrep[w][None]
        yc[pl.ds(pl.multiple_of(r0, CONV_CHUNK), CONV_CHUNK), :] = acc.reshape(CONV_CHUNK, D_CONV)
        return carry

    lax.fori_loop(0, tc // CONV_CHUNK, chunk, 0)
    o_ref[...] = _conv_tail(yc[...], g_ref, b_ref)

    @pl.when(t == pl.num_programs(1) - 1)
    def _():
        ns_ref[0] = ext[tc:tc + CONV_PAD, :]


def _conv_vec_specs(l, nargs):
    return [pl.BlockSpec((None, 1, D_CONV), lambda *_: (l, 0, 0)) for _ in range(nargs)]


def _conv_prompt(z, state_pad, cw, cb, g, b, l, bp, tp, tc):
    nt = tp // tc
    return pl.pallas_call(
        _conv_prompt_kernel,
        grid=(bp, nt),
        in_specs=[pl.BlockSpec((tc, D_CONV), lambda bb, t: (bb * nt + t, Z_UA // D_CONV)),
                  pl.BlockSpec((tc, D_CONV), lambda bb, t: (bb * nt + t, Z_UB // D_CONV)),
                  pl.BlockSpec((1, CONV_PAD, D_CONV), lambda bb, t: (bb, 0, 0)),
                  pl.BlockSpec((None, CONV_PAD, D_CONV), lambda bb, t: (l, 0, 0))] + _conv_vec_specs(l, 3),
        out_specs=[pl.BlockSpec((tc, D_CONV), lambda bb, t: (bb * nt + t, 0)),
                   pl.BlockSpec((1, CONV_PAD, D_CONV), lambda bb, t: (bb, 0, 0))],
        out_shape=[jax.ShapeDtypeStruct((bp * tp, D_CONV), BF16),
                   jax.ShapeDtypeStruct((bp, CONV_PAD, D_CONV), F32)],
        scratch_shapes=[pltpu.VMEM((tc + CONV_PAD, D_CONV), F32),
                        pltpu.VMEM((7, tc + CONV_PAD - 8, D_CONV), F32),
                        pltpu.VMEM((tc, D_CONV), F32),
                        pltpu.VMEM((CONV_PAD, 8, D_CONV), F32)],
        compiler_params=_cparams("parallel", "arbitrary"),
        name="conv_prompt",
    )(z, z, state_pad, cw, cb, g, b)


def _conv_sample_kernel(ua_ref, ub_ref, st_ref, cw_ref, cb_ref, g_ref, b_ref, o_ref, a_ref):
    ts = ua_ref.shape[0]
    ns = CONV_W - 1
    a = ua_ref[...].astype(F32) * _sigmoid(ub_ref[...].astype(F32))
    a_ref[...] = a
    st = st_ref[...]
    row = lax.broadcasted_iota(jnp.int32, (ns, D_CONV), 0)
    for t in range(ts):
        wt = jnp.zeros((ns, D_CONV), F32)
        for j in range(t, ns):
            wt = jnp.where(row == j, cw_ref[j - t:j - t + 1, :], wt)
        yc = jnp.sum(st * wt[None], axis=1) + cb_ref[...]
        for t2 in range(t + 1):
            wi = CONV_W - 1 - (t - t2)
            yc = yc + a[t2] * cw_ref[wi:wi + 1, :]
        o_ref[t] = _conv_tail(yc, g_ref, b_ref)


def _conv_sample(zs3, state, cw, cb, g, b, l, bs_blk):
    ts, bs, _ = zs3.shape
    ns = CONV_W - 1
    return pl.pallas_call(
        _conv_sample_kernel,
        grid=(bs // bs_blk,),
        in_specs=[pl.BlockSpec((ts, bs_blk, D_CONV), lambda i: (0, i, Z_UA // D_CONV)),
                  pl.BlockSpec((ts, bs_blk, D_CONV), lambda i: (0, i, Z_UB // D_CONV)),
                  pl.BlockSpec((None, bs_blk, ns, D_CONV), lambda i: (l, i, 0, 0)),
                  pl.BlockSpec((None, CONV_PAD, D_CONV), lambda i: (l, 0, 0))] + _conv_vec_specs(l, 3),
        out_specs=[pl.BlockSpec((ts, bs_blk, D_CONV), lambda i: (0, i, 0)),
                   pl.BlockSpec((ts, bs_blk, D_CONV), lambda i: (0, i, 0))],
        out_shape=[jax.ShapeDtypeStruct((ts, bs, D_CONV), BF16),
                   jax.ShapeDtypeStruct((ts, bs, D_CONV), F32)],
        compiler_params=_cparams("parallel"),
        name="conv_sample",
    )(zs3, zs3, state, cw, cb, g, b)


def _t5_bucket(dist):
    max_exact = N_BUCKETS // 2
    d = jnp.maximum(dist, 0)
    large = max_exact + (jnp.log(jnp.maximum(d, 1).astype(F32) / max_exact)
                         / math.log(MAX_DIST / max_exact) * (N_BUCKETS - max_exact)).astype(jnp.int32)
    return jnp.where(d < max_exact, d, jnp.minimum(large, N_BUCKETS - 1))


def _bias_heads(rel_bias, dist, valid):
    onehot = (_t5_bucket(dist)[..., None] == jnp.arange(N_BUCKETS)).astype(F32)
    bias = jnp.einsum("qkb,bh->qkh", onehot, rel_bias.astype(F32), precision=lax.Precision.HIGHEST)
    bias = jnp.where(valid[..., None], bias, NEG_INF)
    return jnp.transpose(bias, (2, 0, 1))


def _attn_prompt_kernel(sink_ref, q_ref, kc_ref, kp_ref, vc_ref, vp_ref, bias_ref, o_ref, *, l):
    first = pl.program_id(1) == 0
    w = WINDOW
    nq = q_ref.shape[0] // w
    kall = jnp.concatenate([kp_ref[...], kc_ref[...]], axis=0).astype(F32)
    vall = jnp.concatenate([vp_ref[...], vc_ref[...]], axis=0).astype(F32)
    lane = lax.broadcasted_iota(jnp.int32, kall.shape, 1)
    lo = lane < HEAD_DIM
    kroll = pltpu.roll(kall, HEAD_DIM, 1)
    vroll = pltpu.roll(vall, HEAD_DIM, 1)

    def halves(a, aroll, g):
        if g == 0:
            return jnp.where(lo, a, 0.0).astype(BF16), jnp.where(lo, 0.0, aroll).astype(BF16)
        return jnp.where(lo, aroll, 0.0).astype(BF16), jnp.where(lo, 0.0, a).astype(BF16)

    kh = [halves(kall, kroll, g) for g in range(N_KV)]
    vh = [halves(vall, vroll, g) for g in range(N_KV)]
    col = lax.broadcasted_iota(jnp.int32, (w, 4 * w), 1)
    prev_col = (col % (2 * w)) < w
    tiles_per_g = Q_PER_KV // 2
    units = [(qi, tile) for qi in range(nq) for tile in range(N_HEADS // 2)]
    for u0 in range(0, len(units), ATT_GROUP):
        group = units[u0:u0 + ATT_GROUP]
        scores = []
        for qi, tile in group:
            r0, g = qi * w, tile // tiles_per_g
            q = q_ref[r0:r0 + w, tile * LANES:(tile + 1) * LANES]
            kk = jnp.concatenate([kh[g][0][r0:r0 + 2 * w], kh[g][1][r0:r0 + 2 * w]], axis=0)
            s = _dot_nt(q, kk) * (HEAD_DIM ** -0.5) + bias_ref[tile]
            if qi == 0:
                s = jnp.where(jnp.logical_and(first, prev_col), NEG_INF, s)
            scores.append(s)
        probs = []
        for (qi, tile), s in zip(group, scores):
            ps = []
            for half in range(2):
                sh = s[:, half * 2 * w:(half + 1) * 2 * w]
                sink = sink_ref[l, 2 * tile + half]
                mx = jnp.maximum(jnp.max(sh, axis=-1, keepdims=True), sink)
                p = jnp.exp(sh - mx)
                den = jnp.sum(p, axis=-1, keepdims=True) + jnp.exp(sink - mx)
                ps.append((p * (1.0 / den)).astype(BF16))
            probs.append(jnp.concatenate(ps, axis=1))
        for (qi, tile), p in zip(group, probs):
            r0, g = qi * w, tile // tiles_per_g
            vv = jnp.concatenate([vh[g][0][r0:r0 + 2 * w], vh[g][1][r0:r0 + 2 * w]], axis=0)
            o_ref[r0:r0 + w, tile * LANES:(tile + 1) * LANES] = _dot(p, vv).astype(BF16)


def _attn_prompt(z, bias, sinks, l, bp, tp):
    w = WINDOW
    qb = ATT_QB if tp % (ATT_QB * w) == 0 else 1
    ns = tp // (qb * w)
    nb = tp // w
    kvw = N_KV * HEAD_DIM
    qw = N_HEADS * HEAD_DIM

    def cur(col):
        return lambda bb, i: (bb * ns + i, col)

    def prev(col):
        return lambda bb, i: (bb * nb + jnp.maximum(i * qb - 1, 0), col)

    return pl.pallas_call(
        functools.partial(_attn_prompt_kernel, l=l),
        grid=(bp, ns),
        in_specs=[pl.BlockSpec(memory_space=pltpu.SMEM),
                  pl.BlockSpec((qb * w, qw), cur(Z_Q // qw)),
                  pl.BlockSpec((qb * w, kvw), cur(Z_K // kvw)),
                  pl.BlockSpec((w, kvw), prev(Z_K // kvw)),
                  pl.BlockSpec((qb * w, kvw), cur(Z_V // kvw)),
                  pl.BlockSpec((w, kvw), prev(Z_V // kvw)),
                  pl.BlockSpec((N_HEADS // 2, w, 4 * w), lambda bb, i: (0, 0, 0))],
        out_specs=pl.BlockSpec((qb * w, qw), lambda bb, i: (bb * ns + i, 0)),
        out_shape=jax.ShapeDtypeStruct((bp * tp, qw), BF16),
        compiler_params=_cparams("parallel", "parallel"),
        name="attn_prompt",
    )(sinks, z, z, z, z, z, bias)


def _attn_sample_kernel(q_ref, kc_ref, vc_ref, kn_ref, vn_ref, bias_ref, sink_ref, o_ref):
    wb = kc_ref.shape[1]
    ts = kn_ref.shape[2]
    for g in range(N_KV):
        lo = g * HEAD_DIM
        qb = (q_ref[:, g].astype(F32) * (HEAD_DIM ** -0.5)).astype(BF16)
        kc = kc_ref[:, :, lo:lo + HEAD_DIM].astype(BF16)
        vc = vc_ref[:, :, lo:lo + HEAD_DIM].astype(BF16)
        kn = kn_ref[:, g].astype(F32)
        vn = vn_ref[:, g].astype(F32)
        bias = bias_ref[g]
        s_c = jnp.einsum("bqd,bkd->bqk", qb, kc, preferred_element_type=F32) + bias[None, :, :wb]
        qf = qb.astype(F32)
        s_n = [jnp.sum(qf * kn[:, j:j + 1, :], axis=-1, keepdims=True) + bias[None, :, wb + j:wb + j + 1]
               for j in range(ts)]
        sink = sink_ref[g][None]
        mx = jnp.maximum(jnp.max(s_c, axis=-1, keepdims=True), sink)
        for sj in s_n:
            mx = jnp.maximum(mx, sj)
        p_c = jnp.exp(s_c - mx)
        p_n = [jnp.exp(sj - mx) for sj in s_n]
        den = jnp.sum(p_c, axis=-1, keepdims=True) + jnp.exp(sink - mx)
        for pj in p_n:
            den = den + pj
        o = jnp.einsum("bqk,bkd->bqd", (p_c / den).astype(BF16), vc, preferred_element_type=F32)
        for j in range(ts):
            o = o + (p_n[j] / den).astype(BF16).astype(F32) * vn[:, j:j + 1, :]
        o_ref[:, g] = o


def _attn_sample(q4, kc, vc, kn, vn, bias, sinks, l, bs_blk):
    bs, _, rt, _ = q4.shape
    wb = kc.shape[2]
    ts = kn.shape[2]
    kvw = N_KV * HEAD_DIM
    return pl.pallas_call(
        _attn_sample_kernel,
        grid=(bs // bs_blk,),
        in_specs=[pl.BlockSpec((bs_blk, N_KV, rt, HEAD_DIM), lambda i: (i, 0, 0, 0)),
                  pl.BlockSpec((None, bs_blk, wb, kvw), lambda i: (l, i, 0, 0)),
                  pl.BlockSpec((None, bs_blk, wb, kvw), lambda i: (l, i, 0, 0)),
                  pl.BlockSpec((bs_blk, N_KV, ts, HEAD_DIM), lambda i: (i, 0, 0, 0)),
                  pl.BlockSpec((bs_blk, N_KV, ts, HEAD_DIM), lambda i: (i, 0, 0, 0)),
                  pl.BlockSpec((N_KV, rt, wb + ts), lambda i: (0, 0, 0)),
                  pl.BlockSpec((N_KV, rt, 1), lambda i: (0, 0, 0))],
        out_specs=pl.BlockSpec((bs_blk, N_KV, rt, HEAD_DIM), lambda i: (i, 0, 0, 0)),
        out_shape=jax.ShapeDtypeStruct((bs, N_KV, rt, HEAD_DIM), F32),
        compiler_params=_cparams("parallel"),
        name="attn_sample",
    )(q4, kc, vc, kn, vn, bias, sinks)


def _mlstm_kernel(*refs, t_valid, nseq, aliased, carried, first_layer=None):
    if aliased:
        refs = refs[:10] + refs[11:]
    (if_ref, ifr_ref, q_ref, k_ref, v_ref, o_ref, g_ref, c0_ref, n0_ref, m0_ref,
     h_ref, c1_ref, n1_ref, m1_ref) = refs[:14]
    c = pl.program_id(1)
    L = ifr_ref.shape[-1]
    if carried:
        c_in, n_in, m_in = c_out, n_out, m_out = refs[14:]

        @pl.when(c == 0)
        def _():
            c_in[...] = c0_ref[...]
            n_in[...] = n0_ref[...]
            m_in[...] = m0_ref[...]
    else:
        (c_in, n_in, m_in), (c_out, n_out, m_out) = (c0_ref, n0_ref, m0_ref), (c1_ref, n1_ref, m1_ref)
        if first_layer is not None:
            for dd in range(c1_ref.shape[0]):
                if dd != first_layer:
                    c1_ref[dd] = jnp.zeros(c1_ref.shape[1:], F32)
            c_out = c1_ref.at[first_layer]

    def seq(ref, s):
        return ref.at[s] if len(ref.shape) == 3 else ref

    tt = lax.broadcasted_iota(jnp.int32, (L, L), 0)
    ss = lax.broadcasted_iota(jnp.int32, (L, L), 1)
    causal = ss <= tt
    tril = causal.astype(F32)
    triu = (tt <= ss).astype(F32)
    seqs = range(nseq)
    heads = [(s_i, h) for s_i in seqs for h in range(M_HEADS)]
    gates = []
    for s_i in seqs:
        ifc = seq(if_ref, s_i)[...]
        ifr = ifr_ref[s_i]
        lf_c = _log_sigmoid(ifc)
        lf_r = _log_sigmoid(ifr)
        i_c, i_r = ifc, ifr
        if t_valid < L:
            rc = lax.broadcasted_iota(jnp.int32, (L, LANES), 0) < t_valid
            rr = lax.broadcasted_iota(jnp.int32, (2 * M_HEADS, L), 1) < t_valid
            lf_c = jnp.where(rc, lf_c, 0.0)
            lf_r = jnp.where(rr, lf_r, 0.0)
            i_c = jnp.where(rc, i_c, NEG_INF)
            i_r = jnp.where(rr, i_r, NEG_INF)
        gates.append((lf_c, lf_r, i_c, i_r))
    f_cs = [_dot_hi(tril, g[0]) for g in gates]
    f_rs = [_dot_hi(g[1], triu) for g in gates]
    state = {(s_i, h): (c_in[s_i, h], n_in[s_i, h], m_in[s_i, h:h + 1, 0:1]) for s_i, h in heads}
    st1 = {}
    for s_i, h in heads:
        _, nrow, m0 = state[s_i, h]
        fc = f_cs[s_i][:, M_HEADS + h:M_HEADS + h + 1]
        fr = f_rs[s_i][M_HEADS + h:M_HEADS + h + 1, :]
        ir = gates[s_i][3][h:h + 1, :]
        dm = jnp.where(causal, fc - fr + ir, NEG_INF)
        m_t = jnp.maximum(m0 + fc, jnp.max(dm, axis=-1, keepdims=True))
        st1[s_i, h] = (fc, dm, m_t, jnp.exp(m0 + fc - m_t))
    st2 = {}
    for s_i, h in heads:
        lo = h * M_DK
        fc, dm, m_t, inter = st1[s_i, h]
        qb = seq(q_ref, s_i)[:, lo:lo + M_DK]
        kf = seq(k_ref, s_i)[:, lo:lo + M_DK].astype(F32) * (M_DK ** -0.5)
        vf = seq(v_ref, s_i)[:, lo:lo + M_DV].astype(F32)
        kb, vb = kf.astype(BF16), vf.astype(BF16)
        sc = _dot_nt(qb, kb) * jnp.exp(dm - m_t)
        st2[s_i, h] = (qb, kf, vf, kb, vb, sc)
    st3 = {}
    for s_i, h in heads:
        cm, nrow, m0 = state[s_i, h]
        fc, dm, m_t, inter = st1[s_i, h]
        qb, kf, vf, kb, vb, sc = st2[s_i, h]
        num = inter * _dot(qb, cm.astype(BF16)) + _dot(sc.astype(BF16), vb)
        qn = (inter * jnp.sum(qb.astype(F32) * nrow, axis=-1, keepdims=True)
              + jnp.sum(sc, axis=-1, keepdims=True))
        st3[s_i, h] = num / jnp.maximum(jnp.abs(qn), jnp.exp(-m_t))
    new_state = {}
    for s_i, h in heads:
        cm, nrow, m0 = state[s_i, h]
        fc, dm, m_t, inter = st1[s_i, h]
        qb, kf, vf, kb, vb, sc = st2[s_i, h]
        ic = gates[s_i][2][:, h:h + 1]
        m_end = m_t[L - 1:L, :]
        f_end = fc[L - 1:L, :]
        decay = jnp.exp(m0 + f_end - m_end)
        w_s = jnp.exp(f_end - fc + ic - m_end)
        new_state[s_i, h] = (decay * cm + _dot_tn(kb, (w_s * vf).astype(BF16)),
                             decay * nrow + jnp.sum(w_s * kf, axis=0, keepdims=True),
                             jnp.broadcast_to(m_end, (1, LANES)))
    for s_i, h in heads:
        lo = h * M_DK
        hn = _norm(st3[s_i, h]) * g_ref[:, lo:lo + M_DV]
        gate = _sigmoid(seq(o_ref, s_i)[:, lo:lo + M_DV].astype(F32))
        seq(h_ref, s_i)[:, lo:lo + M_DV] = (gate * hn).astype(BF16)
    for s_i, h in heads:
        c_out[s_i, h], n_out[s_i, h], m_out[s_i, h:h + 1, :] = new_state[s_i, h]
    if not carried:
        for s_i in seqs:
            m_out[s_i, M_HEADS:, :] = jnp.zeros((M_HEADS, LANES), F32)

    if carried:
        @pl.when(c == pl.num_programs(1) - 1)
        def _():
            c1_ref[...] = c_out[...]
            n1_ref[...] = n_out[...]
            m1_ref[...] = m_out[...]


def _mlstm_state_specs(nseq, l_state):
    if l_state is None:
        c_spec = pl.BlockSpec((nseq, M_HEADS, M_DK, M_DV), lambda b, c: (b, 0, 0, 0))
    else:
        c_spec = pl.BlockSpec((None, nseq, M_HEADS, M_DK, M_DV), lambda b, c: (l_state, b, 0, 0, 0))
    return (c_spec,
            pl.BlockSpec((nseq, M_HEADS, 1, M_DK), lambda b, c: (b, 0, 0, 0)),
            pl.BlockSpec((nseq, 2 * M_HEADS, LANES), lambda b, c: (b, 0, 0)))


def _mlstm_scratch(nseq):
    return [pltpu.VMEM((nseq, M_HEADS, M_DK, M_DV), F32),
            pltpu.VMEM((nseq, M_HEADS, 1, M_DK), F32),
            pltpu.VMEM((nseq, 2 * M_HEADS, LANES), F32)]


def _mlstm_prompt(z, zif, ifr, gamma, l, bp, nc, L):
    hw = M_HEADS * M_DK
    zero = lambda *s: jnp.zeros(s, F32)

    def rows(col):
        return lambda b, c: (b * nc + c, col)

    kern = functools.partial(_mlstm_kernel, t_valid=L, nseq=1, aliased=False, carried=True)
    return pl.pallas_call(
        kern,
        grid=(bp, nc),
        in_specs=[pl.BlockSpec((L, LANES), rows(0)),
                  pl.BlockSpec((1, 2 * M_HEADS, L), lambda b, c: (b * nc + c, 0, 0)),
                  pl.BlockSpec((L, hw), rows(Z_MQ // hw)),
                  pl.BlockSpec((L, hw), rows(Z_MK // hw)),
                  pl.BlockSpec((L, hw), rows(Z_MV // hw)),
                  pl.BlockSpec((L, hw), rows(Z_MO // hw)),
                  pl.BlockSpec((None, 1, hw), lambda b, c: (l, 0, 0)),
                  *_mlstm_state_specs(1, None)],
        out_specs=[pl.BlockSpec((L, hw), lambda b, c: (b * nc + c, 0)),
                   *_mlstm_state_specs(1, None)],
        out_shape=[jax.ShapeDtypeStruct((bp * nc * L, hw), BF16),
                   jax.ShapeDtypeStruct((bp, M_HEADS, M_DK, M_DV), F32),
                   jax.ShapeDtypeStruct((bp, M_HEADS, 1, M_DK), F32),
                   jax.ShapeDtypeStruct((bp, 2 * M_HEADS, LANES), F32)],
        scratch_shapes=_mlstm_scratch(1),
        compiler_params=_cparams("parallel", "arbitrary"),
        name="mlstm_prompt",
    )(zif, ifr, z, z, z, z, gamma, zero(bp, M_HEADS, M_DK, M_DV), zero(bp, M_HEADS, 1, M_DK),
      zero(bp, 2 * M_HEADS, LANES))


def _mlstm_sample(zm3, if3, ifr, gamma, c_all, n0, m0x, c_out_prev, l, t_valid, nseq):
    bs, L, _ = zm3.shape
    depth = c_all.shape[0]
    hw = M_HEADS * M_DK
    aliased = c_out_prev is not None

    def blk(col):
        return pl.BlockSpec((nseq, L, hw), lambda b, c: (b, 0, col))

    c_in, n_spec, m_spec = _mlstm_state_specs(nseq, l)
    in_specs = [pl.BlockSpec((nseq, L, LANES), lambda b, c: (b, 0, 0)),
                pl.BlockSpec((nseq, 2 * M_HEADS, L), lambda b, c: (b, 0, 0)),
                blk(0), blk(1), blk(2), blk(3),
                pl.BlockSpec((None, 1, hw), lambda b, c: (l, 0, 0)),
                c_in, n_spec, m_spec]
    args = [if3, ifr, zm3, zm3, zm3, zm3, gamma, c_all, n0, m0x]
    aliases = {}
    if aliased:
        in_specs.append(pl.BlockSpec(memory_space=pl.ANY))
        args.append(c_out_prev)
        aliases = {len(args) - 1: 1}
        c_out = c_in
    else:
        c_out = pl.BlockSpec((depth, nseq, M_HEADS, M_DK, M_DV), lambda b, c: (0, b, 0, 0, 0))
    kern = functools.partial(_mlstm_kernel, t_valid=t_valid, nseq=nseq, aliased=aliased, carried=False,
                             first_layer=None if aliased else l)
    return pl.pallas_call(
        kern,
        grid=(bs // nseq, 1),
        in_specs=in_specs,
        out_specs=[pl.BlockSpec((nseq, L, hw), lambda b, c: (b, 0, 0)), c_out, n_spec, m_spec],
        out_shape=[jax.ShapeDtypeStruct((bs, L, hw), BF16),
                   jax.ShapeDtypeStruct((depth, bs, M_HEADS, M_DK, M_DV), F32),
                   jax.ShapeDtypeStruct((bs, M_HEADS, 1, M_DK), F32),
                   jax.ShapeDtypeStruct((bs, 2 * M_HEADS, LANES), F32)],
        input_output_aliases=aliases,
        compiler_params=_cparams("parallel", "arbitrary"),
        name="mlstm_sample",
    )(*args)


def _m_state_in(m):
    lead = m.shape[:-1]
    mx = jnp.zeros(lead + (2 * M_HEADS, LANES), F32)
    return mx.at[..., :M_HEADS, :].set(jnp.broadcast_to(m[..., None], lead + (M_HEADS, LANES)))


def _merge_kernel(*refs, np_tiles, alpha, route):
    (cp_ref, cs_ref, ap_ref, as_ref, mp_ref, ms_ref, g0_ref, g1_ref, g2_ref, x_ref,
     gtp_ref, gts_ref, shp_ref, shs_ref, scp_ref, scs_ref,
     wc_ref, wa_ref, wm_ref, wo_ref, pg_ref, pb_ref) = refs[:22]
    if route:
        rw_ref, rb_ref, x1_ref, h_ref, route_ref = refs[22:]
    else:
        x1_ref, h_ref = refs[22:]
    is_s = pl.program_id(0) >= np_tiles

    def gate(ref):
        return _sigmoid(ref[...].astype(F32))

    y = (gate(g0_ref) * _dot(_pick(is_s, cp_ref, cs_ref), wc_ref[...])
         + gate(g1_ref) * _dot(_pick(is_s, ap_ref, as_ref), wa_ref[...])
         + gate(g2_ref) * _dot(_pick(is_s, mp_ref, ms_ref), wm_ref[...]))
    mix = _dot(y.astype(BF16), wo_ref[...])
    x1 = _norm(alpha * x_ref[...] + _pick(is_s, gtp_ref, gts_ref) * mix) * pg_ref[...] + pb_ref[...]
    x1_ref[...] = x1
    h = _norm(x1) * (1.0 + _pick(is_s, scp_ref, scs_ref)) + _pick(is_s, shp_ref, shs_ref)
    h_ref[...] = h.astype(h_ref.dtype)
    if route:
        h_hi, h_lo = _split_bf16(h)
        w_hi, w_lo = _split_bf16(rw_ref[...])
        logits = _dot(h_hi, w_hi) + (_dot(h_hi, w_lo) + _dot(h_lo, w_hi)) + rb_ref[...]
        lane = lax.broadcasted_iota(jnp.int32, logits.shape, 1)
        m1 = jnp.max(logits, axis=-1, keepdims=True)
        e1 = jnp.min(jnp.where(logits == m1, lane, LANES), axis=-1, keepdims=True)
        l2 = jnp.where(lane == e1, NEG_INF, logits)
        m2 = jnp.max(l2, axis=-1, keepdims=True)
        e2 = jnp.min(jnp.where(l2 == m2, lane, LANES), axis=-1, keepdims=True)
        ex = jnp.exp(m2 - m1)
        w1 = 1.0 / (1.0 + ex)
        w2 = ex / (1.0 + ex)
        out = jnp.where(lane == 0, e1.astype(F32),
                        jnp.where(lane == 1, e2.astype(F32),
                                  jnp.where(lane == 2, w1, jnp.where(lane == 3, w2, 0.0))))
        route_ref[...] = out


def _merge(cp, cs, ap, as_, mp, ms, z, x, mod_p, mod_s, wc, wa, wm, wo, pg, pb, l, rm, alpha, router):
    n, d = x.shape
    hw = D_CONV
    route = router is not None
    tm = rm.t

    def pblk():
        return pl.BlockSpec((tm, hw), lambda i: (rm.prompt(i), 0))

    def sblk():
        return pl.BlockSpec((tm, hw), lambda i: (rm.sample(i), 0))

    def zg(k):
        return pl.BlockSpec((tm, d), lambda i: (i, Z_G // d + k))

    def lw(a):
        return pl.BlockSpec((None,) + a.shape[1:], lambda i: (l,) + (0,) * (a.ndim - 1))

    post = pl.BlockSpec((None, None, 1, d), lambda i: (l, 0, 0, 0))
    in_specs = [pblk(), sblk(), pblk(), sblk(), pblk(), sblk(), zg(0), zg(1), zg(2),
                pl.BlockSpec((tm, d), lambda i: (i, 0)),
                *rm.mod_specs(l, 2, d), *rm.mod_specs(l, 3, d), *rm.mod_specs(l, 4, d),
                lw(wc), lw(wa), lw(wm), lw(wo), post, post]
    args = [cp, cs, ap, as_, mp, ms, z, z, z, x, mod_p, mod_s, mod_p, mod_s, mod_p, mod_s,
            wc, wa, wm, wo, pg, pb]
    out_specs = [pl.BlockSpec((tm, d), lambda i: (i, 0)), pl.BlockSpec((tm, d), lambda i: (i, 0))]
    out_shape = [jax.ShapeDtypeStruct((n, d), F32), jax.ShapeDtypeStruct((n, d), F32 if route else BF16)]
    if route:
        rw, rb, lj = router
        in_specs += [pl.BlockSpec((None, d, LANES), lambda i: (lj, 0, 0)),
                     pl.BlockSpec((None, 1, LANES), lambda i: (lj, 0, 0))]
        args += [rw, rb]
        out_specs.append(pl.BlockSpec((tm, LANES), lambda i: (i, 0)))
        out_shape.append(jax.ShapeDtypeStruct((n, LANES), F32))
    kern = functools.partial(_merge_kernel, np_tiles=rm.np_tiles, alpha=alpha, route=route)
    return pl.pallas_call(
        kern, grid=(n // tm,), in_specs=in_specs, out_specs=out_specs, out_shape=out_shape,
        compiler_params=_cparams("parallel"), name="merge_route" if route else "merge",
    )(*args)


def _post_residual(x_ref, gp_ref, gs_ref, f, pg_ref, pb_ref, is_s, alpha):
    return _norm(alpha * x_ref[...] + _pick(is_s, gp_ref, gs_ref) * f) * pg_ref[...] + pb_ref[...]


def _ffn_kernel(h_ref, wg_ref, wu_ref, wd_ref, x_ref, gp_ref, gs_ref, pg_ref, pb_ref, o_ref, acc,
                *, alpha, np_tiles):
    j = pl.program_id(1)
    h = h_ref[...]
    a = (_silu(_dot(h, wg_ref[...])) * _dot(h, wu_ref[...])).astype(BF16)
    part = _dot(a, wd_ref[...])

    @pl.when(j == 0)
    def _():
        acc[...] = part

    @pl.when(j > 0)
    def _():
        acc[...] = acc[...] + part

    @pl.when(j == pl.num_programs(1) - 1)
    def _():
        is_s = pl.program_id(0) >= np_tiles
        o_ref[...] = _post_residual(x_ref, gp_ref, gs_ref, acc[...], pg_ref, pb_ref, is_s, alpha)


def _ffn_dense(h, wg, wu, wd, x, mod_p, mod_s, pg, pb, l, lj, rm, tf, alpha):
    n, d = x.shape
    f = wg.shape[2]
    tm = rm.t
    post = pl.BlockSpec((None, None, 1, d), lambda i, j: (l, 1, 0, 0))
    return pl.pallas_call(
        functools.partial(_ffn_kernel, alpha=alpha, np_tiles=rm.np_tiles),
        grid=(n // tm, f // tf),
        in_specs=[pl.BlockSpec((tm, d), lambda i, j: (i, 0)),
                  pl.BlockSpec((None, d, tf), lambda i, j: (lj, 0, j)),
                  pl.BlockSpec((None, d, tf), lambda i, j: (lj, 0, j)),
                  pl.BlockSpec((None, tf, d), lambda i, j: (lj, j, 0)),
                  pl.BlockSpec((tm, d), lambda i, j: (i, 0)),
                  *rm.mod_specs(l, 5, d), post, post],
        out_specs=pl.BlockSpec((tm, d), lambda i, j: (i, 0)),
        out_shape=jax.ShapeDtypeStruct((n, d), F32),
        scratch_shapes=[pltpu.VMEM((tm, d), F32)],
        compiler_params=_cparams("parallel", "arbitrary"),
        name="ffn_dense",
    )(h, wg, wu, wd, x, mod_p, mod_s, pg, pb)


def _rank_kernel(route_ref, rank_ref, tot_ref, carry):
    i = pl.program_id(0)
    tm = route_ref.shape[0]

    @pl.when(i == 0)
    def _():
        carry[...] = jnp.zeros_like(carry)

    r = route_ref[...]
    lane = lax.broadcasted_iota(jnp.int32, (tm, LANES), 1)
    e1 = r[:, 0:1].astype(jnp.int32)
    e2 = r[:, 1:2].astype(jnp.int32)
    hit1 = lane == e1
    hit2 = lane == e2
    onehot = jnp.where(jnp.logical_or(hit1, hit2), 1.0, 0.0)
    tt = lax.broadcasted_iota(jnp.int32, (tm, tm), 0)
    ss = lax.broadcasted_iota(jnp.int32, (tm, tm), 1)
    before = jnp.where(ss < tt, 1.0, 0.0).astype(BF16)
    cnt = _dot(before, onehot.astype(BF16)) + carry[0:1, :]
    r1 = jnp.sum(jnp.where(hit1, cnt, 0.0), axis=-1, keepdims=True)
    r2 = jnp.sum(jnp.where(hit2, cnt, 0.0), axis=-1, keepdims=True)
    rank_ref[...] = jnp.where(lane == 0, r1, jnp.where(lane == 1, r2, 0.0))
    carry[...] = carry[...] + jnp.sum(onehot, axis=0, keepdims=True)
    tot_ref[...] = carry[...]


def _moe_rank(route, tm):
    n = route.shape[0]
    return pl.pallas_call(
        _rank_kernel,
        grid=(n // tm,),
        in_specs=[pl.BlockSpec((tm, LANES), lambda i: (i, 0))],
        out_specs=[pl.BlockSpec((tm, LANES), lambda i: (i, 0)),
                   pl.BlockSpec((8, LANES), lambda i: (0, 0))],
        out_shape=[jax.ShapeDtypeStruct((n, LANES), F32), jax.ShapeDtypeStruct((8, LANES), F32)],
        scratch_shapes=[pltpu.VMEM((8, LANES), F32)],
        compiler_params=_cparams("arbitrary"),
        name="moe_rank",
    )(route)


def _row_copy(src, s, dst, t, sem):
    return pltpu.make_async_copy(src.at[pl.ds(s, 1)], dst.at[pl.ds(t, 1)], sem)


def _dispatch_kernel(d1_ref, d2_ref, zl_ref, h_ref, xs_hbm, zbuf, sem, zsem):
    i = pl.program_id(0)
    tm = h_ref.shape[0]

    @pl.when(i == 0)
    def _():
        zbuf[...] = jnp.zeros_like(zbuf)

        def zero_copy(k):
            row = pl.multiple_of(jnp.maximum(zl_ref[k], 0), MOE_SUB)
            return pltpu.make_async_copy(zbuf, xs_hbm.at[pl.ds(row, MOE_SUB)], zsem)

        def start(k, carry):
            @pl.when(zl_ref[k] >= 0)
            def _():
                zero_copy(k).start()
            return carry

        def wait(k, carry):
            @pl.when(zl_ref[k] >= 0)
            def _():
                zero_copy(k).wait()
            return carry

        lax.fori_loop(0, zl_ref.shape[0], start, 0)
        lax.fori_loop(0, zl_ref.shape[0], wait, 0)

    def issue(r, carry):
        _row_copy(h_ref, r, xs_hbm, d1_ref[i * tm + r], sem).start(priority=0)
        _row_copy(h_ref, r, xs_hbm, d2_ref[i * tm + r], sem).start(priority=1)
        return carry

    lax.fori_loop(0, tm, issue, 0, unroll=8)
    for _ in range(2):
        pltpu.make_async_copy(h_ref, xs_hbm.at[pl.ds(0, tm)], sem).wait()


def _moe_dispatch(dest1, dest2, zlist, h, n_rows, tm):
    n, d = h.shape
    grid_spec = pltpu.PrefetchScalarGridSpec(
        num_scalar_prefetch=3,
        grid=(n // tm,),
        in_specs=[pl.BlockSpec((tm, d), lambda i, *_: (i, 0))],
        out_specs=pl.BlockSpec(memory_space=pl.ANY),
        scratch_shapes=[pltpu.VMEM((MOE_SUB, d), F32), pltpu.SemaphoreType.DMA(()), pltpu.SemaphoreType.DMA(())],
    )
    return pl.pallas_call(
        _dispatch_kernel, grid_spec=grid_spec,
        out_shape=jax.ShapeDtypeStruct((n_rows, d), F32),
        compiler_params=_cparams("arbitrary"),
        name="moe_dispatch",
    )(dest1, dest2, zlist, h)


def _expert_kernel(blk_e_ref, nvalid_ref, xs_ref, wg_ref, wu_ref, wd_ref, y_ref, xb, acc):
    b = pl.program_id(0)
    j = pl.program_id(1)
    tb = xb.shape[0]
    nsub = tb // MOE_SUB
    nv = nvalid_ref[b]
    wg = wg_ref[...].astype(BF16)
    wu = wu_ref[...].astype(BF16)
    wd = wd_ref[...].astype(BF16)

    def run(rows):
        @pl.when(j == 0)
        def _():
            xb[rows, :] = xs_ref[rows, :].astype(BF16)

        x = xb[rows, :]
        a = (_silu(_dot(x, wg)) * _dot(x, wu)).astype(BF16)
        part = _dot(a, wd)

        @pl.when(j == 0)
        def _():
            acc[rows, :] = part

        @pl.when(j > 0)
        def _():
            acc[rows, :] = acc[rows, :] + part

        @pl.when(j == pl.num_programs(1) - 1)
        def _():
            y_ref[rows, :] = acc[rows, :]

    @pl.when(nv == tb)
    def _():
        run(pl.ds(0, tb))

    for s in range(nsub):
        rows = pl.ds(s * MOE_SUB, MOE_SUB)

        @pl.when(jnp.logical_and(s * MOE_SUB < nv, nv < tb))
        def _():
            run(rows)

        @pl.when(jnp.logical_and(s * MOE_SUB >= nv, j == pl.num_programs(1) - 1))
        def _():
            y_ref[rows, :] = jnp.zeros((MOE_SUB, y_ref.shape[1]), F32)


def _moe_experts(blk_e, nvalid, xs, wg, wu, wd, lj, tb, tf):
    n_rows, d = xs.shape
    f = wg.shape[3]
    nj = f // tf

    def jm(b, j, nv):
        return jnp.where(nv[b] > 0, j, nj - 1)

    grid_spec = pltpu.PrefetchScalarGridSpec(
        num_scalar_prefetch=2,
        grid=(n_rows // tb, nj),
        in_specs=[pl.BlockSpec((tb, d), lambda b, j, be, nv: (b, 0)),
                  pl.BlockSpec((None, None, d, tf), lambda b, j, be, nv: (lj, be[b], 0, jm(b, j, nv))),
                  pl.BlockSpec((None, None, d, tf), lambda b, j, be, nv: (lj, be[b], 0, jm(b, j, nv))),
                  pl.BlockSpec((None, None, tf, d), lambda b, j, be, nv: (lj, be[b], jm(b, j, nv), 0))],
        out_specs=pl.BlockSpec((tb, d), lambda b, j, be, nv: (b, 0)),
        scratch_shapes=[pltpu.VMEM((tb, d), BF16), pltpu.VMEM((tb, d), F32)],
    )
    return pl.pallas_call(
        _expert_kernel, grid_spec=grid_spec,
        out_shape=jax.ShapeDtypeStruct((n_rows, d), F32),
        compiler_params=_cparams("parallel", "arbitrary"),
        name="moe_experts",
    )(blk_e, nvalid, xs, wg, wu, wd)


def _combine_kernel(d1_ref, d2_ref, ys_hbm, route_ref, x_ref, gp_ref, gs_ref, pg_ref, pb_ref, o_ref,
                    y1, y2, sem1, sem2, *, alpha, np_tiles):
    i = pl.program_id(0)
    tm = x_ref.shape[0]

    def issue(r, carry):
        _row_copy(ys_hbm, d1_ref[i * tm + r], y1, r, sem1).start(priority=0)
        _row_copy(ys_hbm, d2_ref[i * tm + r], y2, r, sem2).start(priority=1)
        return carry

    lax.fori_loop(0, tm, issue, 0, unroll=8)
    pltpu.make_async_copy(ys_hbm.at[pl.ds(0, tm)], y1, sem1).wait()
    pltpu.make_async_copy(ys_hbm.at[pl.ds(0, tm)], y2, sem2).wait()
    r = route_ref[...]
    f = y1[...] * r[:, 2:3] + y2[...] * r[:, 3:4]
    o_ref[...] = _post_residual(x_ref, gp_ref, gs_ref, f, pg_ref, pb_ref, i >= np_tiles, alpha)


def _moe_combine(dest1, dest2, ys, route, x, mod_p, mod_s, pg, pb, l, rm, alpha):
    n, d = x.shape
    tm = rm.t
    post = pl.BlockSpec((None, None, 1, d), lambda i, *_: (l, 1, 0, 0))
    grid_spec = pltpu.PrefetchScalarGridSpec(
        num_scalar_prefetch=2,
        grid=(n // tm,),
        in_specs=[pl.BlockSpec(memory_space=pl.ANY),
                  pl.BlockSpec((tm, LANES), lambda i, *_: (i, 0)),
                  pl.BlockSpec((tm, d), lambda i, *_: (i, 0)),
                  *rm.mod_specs(l, 5, d), post, post],
        out_specs=pl.BlockSpec((tm, d), lambda i, *_: (i, 0)),
        scratch_shapes=[pltpu.VMEM((tm, d), F32), pltpu.VMEM((tm, d), F32),
                        pltpu.SemaphoreType.DMA(()), pltpu.SemaphoreType.DMA(())],
    )
    return pl.pallas_call(
        functools.partial(_combine_kernel, alpha=alpha, np_tiles=rm.np_tiles), grid_spec=grid_spec,
        out_shape=jax.ShapeDtypeStruct((n, d), F32),
        compiler_params=_cparams("arbitrary"),
        name="moe_combine",
    )(dest1, dest2, ys, route, x, mod_p, mod_s, pg, pb)


def _moe_ffn(h, route, wg, wu, wd, x, mod_p, mod_s, pg, pb, l, lj, rm, tb, tf, alpha):
    n = h.shape[0]
    rank, tot = _moe_rank(route, rm.t)
    counts = tot[0, :N_EXPERTS].astype(jnp.int32)
    padded = (counts + tb - 1) // tb * tb
    pad_end = jnp.cumsum(padded)
    pad_start = pad_end - padded
    experts = jnp.arange(N_EXPERTS, dtype=jnp.int32)

    def slot(col):
        e = route[:, col].astype(jnp.int32)
        start = jnp.sum(jnp.where(e[:, None] == experts[None, :], pad_start[None, :], 0), axis=1)
        return start + rank[:, col].astype(jnp.int32)

    dest1, dest2 = slot(0), slot(1)
    n_blocks = -(-(2 * n + N_EXPERTS * (tb - 1)) // tb)
    blk_start = jnp.arange(n_blocks, dtype=jnp.int32) * tb
    blk_e = jnp.minimum(jnp.sum(pad_end[None, :] <= blk_start[:, None], axis=1), N_EXPERTS - 1).astype(jnp.int32)
    own_end = jnp.sum(jnp.where(blk_e[:, None] == experts[None, :], (pad_start + counts)[None, :], 0), axis=1)
    nvalid = jnp.clip(own_end - blk_start, 0, tb).astype(jnp.int32)
    nvalid = jnp.where(blk_start < pad_end[-1], nvalid, 0)
    per_blk = tb // MOE_SUB
    sub_start = jnp.arange(n_blocks * per_blk, dtype=jnp.int32) * MOE_SUB
    sub_room = jnp.repeat(blk_start + nvalid, per_blk) - sub_start
    zlist = jnp.where(sub_room < MOE_SUB, sub_start, -1).astype(jnp.int32)
    xs = _moe_dispatch(dest1, dest2, zlist, h, n_blocks * tb, rm.t)
    ys = _moe_experts(blk_e, nvalid, xs, wg, wu, wd, lj, tb, tf)
    return _moe_combine(dest1, dest2, ys, route, x, mod_p, mod_s, pg, pb, l, rm, alpha)


def _pack_w_in(w_in, b_in):
    depth, d, _ = w_in.shape
    q_end = 2 * D_CONV + N_HEADS * HEAD_DIM
    k_end = q_end + N_KV * HEAD_DIM
    a_end = k_end + N_KV * HEAD_DIM
    m_end = a_end + 4 * M_HEADS * M_DK
    if_end = m_end + 2 * M_HEADS

    def pack(a):
        return jnp.concatenate([a[..., if_end:], a[..., :q_end], a[..., a_end:m_end], a[..., q_end:a_end]], axis=-1)

    def gates(a):
        return jnp.pad(a[..., m_end:if_end], [(0, 0)] * (a.ndim - 1) + [(0, LANES - 2 * M_HEADS)])

    return (pack(w_in).astype(BF16), pack(b_in).reshape(depth, 1, Z_W),
            gates(w_in).astype(BF16), gates(b_in).reshape(depth, 1, LANES))


def kernel(x_prompt, x_sample, cache_swa_k, cache_swa_v, state_conv, state_mlstm_C, state_mlstm_n, state_mlstm_m, c_prompt, c_sample, w_ada, b_ada, w_in, b_in, conv_w, conv_b, conv_ln_g, conv_ln_b, w_conv_out, attn_sinks, rel_bias, w_attn_out, m_norm_g, w_m_out, w_out, post_ln_g, post_ln_b, ffn_w_gate, ffn_w_up, ffn_w_down, router_w, router_b, moe_w_gate, moe_w_up, moe_w_down):
    bp, tp, d = x_prompt.shape
    bs, ts, _ = x_sample.shape
    depth = w_ada.shape[0]
    alpha = (2 * depth) ** 0.25
    n_p, n_s = bp * tp, bs * ts
    tm = n_s
    assert d == D_MODEL and tp % tm == 0 and tp % WINDOW == 0 and tm % 32 == 0
    rm = _RowMap(tm, bp, tp, n_p)
    rm_half = _RowMap(tm // 2, bp, tp, n_p)
    wb = cache_swa_k.shape[2]
    big = n_p >= 4096
    tc = 512 if big else tm
    lm = 256 if big else min(tp, 128)
    tb = 1024 if big else 2 * MOE_SUB
    bs_blk = 32 if bs % 32 == 0 else bs
    bs_att = 16 if bs % 16 == 0 else bs
    bs_m = 4 if bs % 4 == 0 else 1
    lts = 16

    x = jnp.concatenate([x_prompt.reshape(n_p, d), jnp.transpose(x_sample, (1, 0, 2)).reshape(n_s, d)], axis=0)

    nc_rows = -(-(bp + bs) // 8) * 8
    c_all = jnp.zeros((nc_rows, d), F32).at[:bp].set(c_prompt).at[bp:bp + bs].set(c_sample)
    mod = _ada_mod(c_all, w_ada, b_ada)
    mod_p = mod[:, :bp].reshape(depth, bp, 1, 6 * d)
    mod_s = jnp.tile(mod[:, bp:bp + bs], (1, ts, 1))

    w_in_p, b_in_p, w_if, b_if = _pack_w_in(w_in, b_in)
    wc_b, wa_b, wm_b, wo_b = (w.astype(BF16) for w in (w_conv_out, w_attn_out, w_m_out, w_out))
    fg_b, fu_b, fd_b = (w.astype(BF16) for w in (ffn_w_gate, ffn_w_up, ffn_w_down))
    cw_pad = jnp.pad(conv_w, ((0, 0), (0, CONV_PAD - CONV_W), (0, 0)))
    cvecs = [v.reshape(depth, 1, D_CONV) for v in (conv_b, conv_ln_g, conv_ln_b)]
    rw_pad = jnp.pad(router_w, ((0, 0), (0, 0), (0, LANES - N_EXPERTS)))
    rb_pad = jnp.pad(router_b, ((0, 0), (0, LANES - N_EXPERTS)), constant_values=NEG_INF)[:, None, :]
    pg = post_ln_g.reshape(depth, 2, 1, d)
    pb = post_ln_b.reshape(depth, 2, 1, d)
    gamma = m_norm_g.reshape(depth, 1, M_HEADS * M_DV)
    sinks = attn_sinks.astype(F32)

    qi = jnp.arange(WINDOW)[:, None]
    kj = jnp.arange(2 * WINDOW)[None, :]
    dist_p = qi + WINDOW - kj
    bh = _bias_heads(rel_bias, dist_p, (dist_p >= 0) & (dist_p < WINDOW))
    bias_p = bh.reshape(N_HEADS // 2, 2, WINDOW, 2 * WINDOW).transpose(0, 2, 1, 3).reshape(
        N_HEADS // 2, WINDOW, 4 * WINDOW)
    dist_s = jnp.arange(ts)[:, None] + wb - jnp.arange(wb + ts)[None, :]
    bias_s = _bias_heads(rel_bias, dist_s, (dist_s >= 0) & (dist_s < WINDOW)).reshape(
        N_KV, Q_PER_KV * ts, wb + ts)
    kc_all = cache_swa_k.reshape(depth, bs, wb, N_KV * HEAD_DIM)
    vc_all = cache_swa_v.reshape(depth, bs, wb, N_KV * HEAD_DIM)

    f_dense = ffn_w_gate.shape[2]
    tf_dense = f_dense // 2 if (f_dense // 2) % LANES == 0 else f_dense
    f_moe = moe_w_gate.shape[3]
    tf_moe = 512 if f_moe % 512 == 0 else f_moe

    n0_all = state_mlstm_n[:, :, :, None, :]
    m0_all = _m_state_in(state_mlstm_m)
    s_c = None
    new_p = [[] for _ in range(6)]
    new_s = [[] for _ in range(6)]
    for l in range(depth):
        j = l // 2
        z, zif = _ln_proj(x, mod_p, mod_s, w_in_p, b_in_p, w_if, b_if, l, rm)
        zs3 = z[n_p:].reshape(ts, bs, Z_W)

        cp, ns_p = _conv_prompt(z, jnp.zeros((bp, CONV_PAD, D_CONV), F32), cw_pad, *cvecs, l, bp, tp, tc)
        cs3, a_s3 = _conv_sample(zs3, state_conv, cw_pad, *cvecs, l, bs_blk)
        new_p[2].append(ns_p[:, CONV_PAD - CONV_W + 1:])
        new_s[2].append(jnp.concatenate([state_conv[l][:, ts:], jnp.transpose(a_s3, (1, 0, 2))], axis=1))

        sink_h = sinks[l].reshape(N_KV, Q_PER_KV, 1)
        sink_s = jnp.broadcast_to(sink_h, (N_KV, Q_PER_KV, ts)).reshape(N_KV, Q_PER_KV * ts, 1)
        ap = _attn_prompt(z, bias_p, sinks, l, bp, tp)
        nk = min(WINDOW, tp)
        kv_tail = jnp.stack([z[(b + 1) * tp - nk:(b + 1) * tp, Z_K:Z_K + 2 * N_KV * HEAD_DIM]
                             for b in range(bp)]).astype(F32)
        new_p[0].append(kv_tail[..., :N_KV * HEAD_DIM].reshape(bp, nk, N_KV, HEAD_DIM))
        new_p[1].append(kv_tail[..., N_KV * HEAD_DIM:].reshape(bp, nk, N_KV, HEAD_DIM))
        q_s = zs3[:, :, Z_Q:Z_Q + N_HEADS * HEAD_DIM].reshape(ts, bs, N_KV, Q_PER_KV, HEAD_DIM)
        q4 = jnp.transpose(q_s, (1, 2, 3, 0, 4)).reshape(bs, N_KV, Q_PER_KV * ts, HEAD_DIM)
        k_s = jnp.transpose(zs3[:, :, Z_K:Z_K + N_KV * HEAD_DIM].reshape(ts, bs, N_KV, HEAD_DIM), (1, 0, 2, 3))
        v_s = jnp.transpose(zs3[:, :, Z_V:Z_V + N_KV * HEAD_DIM].reshape(ts, bs, N_KV, HEAD_DIM), (1, 0, 2, 3))
        o4 = _attn_sample(q4, kc_all, vc_all, jnp.transpose(k_s, (0, 2, 1, 3)), jnp.transpose(v_s, (0, 2, 1, 3)),
                          bias_s, sink_s, l, bs_att)
        as_ = jnp.transpose(o4.reshape(bs, N_KV, Q_PER_KV, ts, HEAD_DIM), (3, 0, 1, 2, 4)).reshape(n_s, -1).astype(BF16)
        new_s[0].append(jnp.concatenate([cache_swa_k[l][:, ts:], k_s.astype(F32)], axis=1))
        new_s[1].append(jnp.concatenate([cache_swa_v[l][:, ts:], v_s.astype(F32)], axis=1))

        ncp = tp // lm
        if_p = zif[:n_p, :2 * M_HEADS].reshape(bp * ncp, lm, 2 * M_HEADS)
        mp, c1p, n1p, m1p = _mlstm_prompt(z, zif, jnp.transpose(if_p, (0, 2, 1)), gamma, l, bp, ncp, lm)
        new_p[3].append(c1p)
        new_p[4].append(n1p[:, :, 0])
        new_p[5].append(m1p[:, :M_HEADS, 0])
        tpad = ((0, 0), (0, lts - ts), (0, 0))
        zm3 = jnp.pad(jnp.transpose(zs3[:, :, Z_MQ:Z_K], (1, 0, 2)), tpad)
        if3 = jnp.pad(jnp.transpose(zif[n_p:].reshape(ts, bs, LANES), (1, 0, 2)), tpad)
        ms, s_c, n1s, m1s = _mlstm_sample(zm3, if3, jnp.transpose(if3[:, :, :2 * M_HEADS], (0, 2, 1)), gamma,
                                          state_mlstm_C, n0_all[l], m0_all[l], s_c, l, ts, bs_m)
        ms = jnp.transpose(ms[:, :ts], (1, 0, 2)).reshape(n_s, -1)
        new_s[4].append(n1s[:, :, 0])
        new_s[5].append(m1s[:, :M_HEADS, 0])

        moe = l % 2 == 1
        router = (rw_pad, rb_pad, j) if moe else None
        outs = _merge(cp, cs3.reshape(n_s, D_CONV), ap, as_, mp, ms, z, x, mod_p, mod_s,
                      wc_b, wa_b, wm_b, wo_b, pg, pb, l, rm_half, alpha, router)
        if moe:
            x1, h2, route = outs
            x = _moe_ffn(h2, route, moe_w_gate, moe_w_up, moe_w_down, x1, mod_p, mod_s, pg, pb,
                         l, j, rm, tb, tf_moe, alpha)
        else:
            x1, h2 = outs
            x = _ffn_dense(h2, fg_b, fu_b, fd_b, x1, mod_p, mod_s, pg, pb, l, j, rm, tf_dense, alpha)

    y_p = x[:n_p].reshape(bp, tp, d)
    y_s = jnp.transpose(x[n_p:].reshape(ts, bs, d), (1, 0, 2))
    p_k, p_v, p_conv, p_c, p_n, p_m = [jnp.stack(a) for a in new_p]
    s_k, s_v, s_conv = [jnp.stack(a) for a in new_s[:3]]
    s_n, s_m = jnp.stack(new_s[4]), jnp.stack(new_s[5])
    return (y_p, y_s, p_k, p_v, p_conv, p_c, p_n, p_m, s_k, s_v, s_conv, s_c, s_n, s_m)
```

```python
import functools
import math

import jax
import jax.numpy as jnp
from jax import lax
from jax.experimental import pallas as pl
from jax.experimental.pallas import tpu as pltpu

F32 = jnp.float32
BF16 = jnp.bfloat16

D_MODEL = 1024
D_CONV = 512
CONV_W = 31
CONV_PAD = 32
N_HEADS = 8
N_KV = 2
HEAD_DIM = 64
Q_PER_KV = N_HEADS // N_KV
WINDOW = 128
N_BUCKETS = 32
MAX_DIST = 128
M_HEADS = 4
M_DK = 128
M_DV = 128
N_EXPERTS = 8
LN_EPS = 1e-5
LANES = 128
NEG_INF = float("-inf")
VMEM_LIMIT = 56 * 1024 * 1024

Z_G, Z_UA, Z_UB, Z_Q = 0, 3072, 3584, 4096
Z_MQ, Z_MK, Z_MV, Z_MO = 4608, 5120, 5632, 6144
Z_K, Z_V, Z_W = 6656, 6784, 6912
TN_IN = 2304
ATT_QB = 4
ATT_GROUP = 8
MOE_SUB = 256
CONV_CHUNK = 32


def _cparams(*sem):
    return pltpu.CompilerParams(dimension_semantics=sem, vmem_limit_bytes=VMEM_LIMIT)


def _sigmoid(x):
    return 1.0 / (1.0 + jnp.exp(-x))


def _silu(x):
    return x * _sigmoid(x)


def _log_sigmoid(x):
    return jnp.minimum(x, 0.0) - jnp.log(1.0 + jnp.exp(-jnp.abs(x)))


def _norm(x):
    mu = jnp.mean(x, axis=-1, keepdims=True)
    xc = x - mu
    var = jnp.mean(xc * xc, axis=-1, keepdims=True)
    return xc * lax.rsqrt(var + LN_EPS)


def _dot(a, b):
    return jnp.dot(a, b, preferred_element_type=F32)


def _dot_nt(a, b):
    return lax.dot_general(a, b, (((1,), (1,)), ((), ())), preferred_element_type=F32)


def _dot_tn(a, b):
    return lax.dot_general(a, b, (((0,), (0,)), ((), ())), preferred_element_type=F32)


def _dot_hi(a, b):
    return jnp.dot(a, b, preferred_element_type=F32, precision=lax.Precision.HIGHEST)


def _split_bf16(a):
    hi = a.astype(BF16)
    return hi, (a - hi.astype(F32)).astype(BF16)


class _RowMap:
    def __init__(self, t, bp, tp, n_p):
        self.t = t
        self.bp = bp
        self.per_seq = tp // t
        self.np_tiles = n_p // t

    def seq(self, i):
        return jnp.minimum(i // self.per_seq, self.bp - 1)

    def prompt(self, i):
        return jnp.minimum(i, self.np_tiles - 1)

    def sample(self, i):
        return jnp.maximum(i - self.np_tiles, 0)

    def mod_specs(self, l, k, d):
        return (pl.BlockSpec((None, None, 1, d), lambda i, *_: (l, self.seq(i), 0, k)),
                pl.BlockSpec((None, self.t, d), lambda i, *_: (l, self.sample(i), k)))


def _pick(is_s, p_ref, s_ref):
    return jnp.where(is_s, s_ref[...], p_ref[...])


def _ada_kernel(c_ref, w_ref, b_ref, o_ref):
    s = _silu(c_ref[...]).astype(BF16)
    o_ref[0] = _dot(s, w_ref[0].astype(BF16)) + b_ref[0]


def _ada_mod(c_all, w_ada, b_ada):
    depth, d, n6 = w_ada.shape
    rows = c_all.shape[0]
    return pl.pallas_call(
        _ada_kernel,
        grid=(depth, n6 // d),
        in_specs=[pl.BlockSpec((rows, d), lambda l, j: (0, 0)),
                  pl.BlockSpec((1, d, d), lambda l, j: (l, 0, j)),
                  pl.BlockSpec((1, 1, d), lambda l, j: (l, 0, j))],
        out_specs=pl.BlockSpec((1, rows, d), lambda l, j: (l, 0, j)),
        out_shape=jax.ShapeDtypeStruct((depth, rows, n6), F32),
        compiler_params=_cparams("parallel", "parallel"),
        name="ada_mod",
    )(c_all, w_ada, b_ada.reshape(depth, 1, n6))


def _ln_proj_kernel(x_ref, shp_ref, shs_ref, scp_ref, scs_ref, w_ref, b_ref, wif_ref, bif_ref,
                    z_ref, zif_ref, h_scr, *, np_tiles):
    is_s = pl.program_id(0) >= np_tiles

    @pl.when(pl.program_id(1) == 0)
    def _():
        h = _norm(x_ref[...]) * (1.0 + _pick(is_s, scp_ref, scs_ref)) + _pick(is_s, shp_ref, shs_ref)
        h = h.astype(BF16)
        h_scr[...] = h
        zif_ref[...] = _dot(h, wif_ref[...]) + bif_ref[...]

    z_ref[...] = (_dot(h_scr[...], w_ref[...]) + b_ref[...]).astype(BF16)


def _ln_proj(x, mod_p, mod_s, w, b, wif, bif, l, rm):
    n, d = x.shape
    zw = w.shape[2]
    tm = rm.t
    shp, shs = rm.mod_specs(l, 0, d)
    scp, scs = rm.mod_specs(l, 1, d)
    return pl.pallas_call(
        functools.partial(_ln_proj_kernel, np_tiles=rm.np_tiles),
        grid=(n // tm, zw // TN_IN),
        in_specs=[pl.BlockSpec((tm, d), lambda i, j: (i, 0)),
                  shp, shs, scp, scs,
                  pl.BlockSpec((None, d, TN_IN), lambda i, j: (l, 0, j)),
                  pl.BlockSpec((None, 1, TN_IN), lambda i, j: (l, 0, j)),
                  pl.BlockSpec((None, d, LANES), lambda i, j: (l, 0, 0)),
                  pl.BlockSpec((None, 1, LANES), lambda i, j: (l, 0, 0))],
        out_specs=[pl.BlockSpec((tm, TN_IN), lambda i, j: (i, j)),
                   pl.BlockSpec((tm, LANES), lambda i, j: (i, 0))],
        out_shape=[jax.ShapeDtypeStruct((n, zw), BF16), jax.ShapeDtypeStruct((n, LANES), F32)],
        scratch_shapes=[pltpu.VMEM((tm, d), BF16)],
        compiler_params=_cparams("parallel", "arbitrary"),
        name="ln_proj",
    )(x, mod_p, mod_s, mod_p, mod_s, w, b, wif, bif)


def _conv_tail(yc, g_ref, b_ref):
    y = _norm(yc) * g_ref[...] + b_ref[...]
    return _silu(y).astype(BF16)


def _conv_prompt_kernel(ua_ref, ub_ref, st_ref, cw_ref, cb_ref, g_ref, b_ref, o_ref, ns_ref,
                        ext, shifted, yc, wrep):
    t = pl.program_id(1)
    tc = ua_ref.shape[0]
    sub = 8

    @pl.when(t == 0)
    def _():
        ext[0:CONV_PAD, :] = st_ref[0]

    @pl.when(t > 0)
    def _():
        ext[0:CONV_PAD, :] = ext[tc:tc + CONV_PAD, :]

    ext[CONV_PAD:, :] = ua_ref[...].astype(F32) * _sigmoid(ub_ref[...].astype(F32))
    for s in range(1, sub):
        shifted[s - 1] = ext[s:s + tc + CONV_PAD - sub, :]
    off = CONV_PAD - (CONV_W - 1)
    for w in range(CONV_W):
        wrep[w] = jnp.broadcast_to(cw_ref[w:w + 1, :], (sub, D_CONV))
    groups = CONV_CHUNK // sub

    def chunk(c, carry):
        r0 = c * CONV_CHUNK
        acc = jnp.broadcast_to(cb_ref[...].reshape(1, 1, D_CONV), (groups, sub, D_CONV))
        for w in range(CONV_W):
            base, s = (off + w) // sub * sub, (off + w) % sub
            src = ext if s == 0 else shifted.at[s - 1]
            win = src[pl.ds(pl.multiple_of(r0 + base, sub), CONV_CHUNK), :]
            acc = acc + win.reshape(groups, sub, D_CONV) * wrep[w][None]
        yc[pl.ds(pl.multiple_of(r0, CONV_CHUNK), CONV_CHUNK), :] = acc.reshape(CONV_CHUNK, D_CONV)
        return carry

    lax.fori_loop(0, tc // CONV_CHUNK, chunk, 0)
    o_ref[...] = _conv_tail(yc[...], g_ref, b_ref)

    @pl.when(t == pl.num_programs(1) - 1)
    def _():
        ns_ref[0] = ext[tc:tc + CONV_PAD, :]


def _conv_vec_specs(l, nargs):
    return [pl.BlockSpec((None, 1, D_CONV), lambda *_: (l, 0, 0)) for _ in range(nargs)]


def _conv_prompt(z, state_pad, cw, cb, g, b, l, bp, tp, tc):
    nt = tp // tc
    return pl.pallas_call(
        _conv_prompt_kernel,
        grid=(bp, nt),
        in_specs=[pl.BlockSpec((tc, D_CONV), lambda bb, t: (bb * nt + t, Z_UA // D_CONV)),
                  pl.BlockSpec((tc, D_CONV), lambda bb, t: (bb * nt + t, Z_UB // D_CONV)),
                  pl.BlockSpec((1, CONV_PAD, D_CONV), lambda bb, t: (bb, 0, 0)),
                  pl.BlockSpec((None, CONV_PAD, D_CONV), lambda bb, t: (l, 0, 0))] + _conv_vec_specs(l, 3),
        out_specs=[pl.BlockSpec((tc, D_CONV), lambda bb, t: (bb * nt + t, 0)),
                   pl.BlockSpec((1, CONV_PAD, D_CONV), lambda bb, t: (bb, 0, 0))],
        out_shape=[jax.ShapeDtypeStruct((bp * tp, D_CONV), BF16),
                   jax.ShapeDtypeStruct((bp, CONV_PAD, D_CONV), F32)],
        scratch_shapes=[pltpu.VMEM((tc + CONV_PAD, D_CONV), F32),
                        pltpu.VMEM((7, tc + CONV_PAD - 8, D_CONV), F32),
                        pltpu.VMEM((tc, D_CONV), F32),
                        pltpu.VMEM((CONV_PAD, 8, D_CONV), F32)],
        compiler_params=_cparams("parallel", "arbitrary"),
        name="conv_prompt",
    )(z, z, state_pad, cw, cb, g, b)


def _conv_sample_kernel(ua_ref, ub_ref, st_ref, cw_ref, cb_ref, g_ref, b_ref, o_ref, a_ref):
    ts = ua_ref.shape[0]
    ns = CONV_W - 1
    a = ua_ref[...].astype(F32) * _sigmoid(ub_ref[...].astype(F32))
    a_ref[...] = a
    st = st_ref[...]
    row = lax.broadcasted_iota(jnp.int32, (ns, D_CONV), 0)
    for t in range(ts):
        wt = jnp.zeros((ns, D_CONV), F32)
        for j in range(t, ns):
            wt = jnp.where(row == j, cw_ref[j - t:j - t + 1, :], wt)
        yc = jnp.sum(st * wt[None], axis=1) + cb_ref[...]
        for t2 in range(t + 1):
            wi = CONV_W - 1 - (t - t2)
            yc = yc + a[t2] * cw_ref[wi:wi + 1, :]
        o_ref[t] = _conv_tail(yc, g_ref, b_ref)


def _conv_sample(zs3, state, cw, cb, g, b, l, bs_blk):
    ts, bs, _ = zs3.shape
    ns = CONV_W - 1
    return pl.pallas_call(
        _conv_sample_kernel,
        grid=(bs // bs_blk,),
        in_specs=[pl.BlockSpec((ts, bs_blk, D_CONV), lambda i: (0, i, Z_UA // D_CONV)),
                  pl.BlockSpec((ts, bs_blk, D_CONV), lambda i: (0, i, Z_UB // D_CONV)),
                  pl.BlockSpec((None, bs_blk, ns, D_CONV), lambda i: (l, i, 0, 0)),
                  pl.BlockSpec((None, CONV_PAD, D_CONV), lambda i: (l, 0, 0))] + _conv_vec_specs(l, 3),
        out_specs=[pl.BlockSpec((ts, bs_blk, D_CONV), lambda i: (0, i, 0)),
                   pl.BlockSpec((ts, bs_blk, D_CONV), lambda i: (0, i, 0))],
        out_shape=[jax.ShapeDtypeStruct((ts, bs, D_CONV), BF16),
                   jax.ShapeDtypeStruct((ts, bs, D_CONV), F32)],
        compiler_params=_cparams("parallel"),
        name="conv_sample",
    )(zs3, zs3, state, cw, cb, g, b)


def _t5_bucket(dist):
    max_exact = N_BUCKETS // 2
    d = jnp.maximum(dist, 0)
    large = max_exact + (jnp.log(jnp.maximum(d, 1).astype(F32) / max_exact)
                         / math.log(MAX_DIST / max_exact) * (N_BUCKETS - max_exact)).astype(jnp.int32)
    return jnp.where(d < max_exact, d, jnp.minimum(large, N_BUCKETS - 1))


def _bias_heads(rel_bias, dist, valid):
    onehot = (_t5_bucket(dist)[..., None] == jnp.arange(N_BUCKETS)).astype(F32)
    bias = jnp.einsum("qkb,bh->qkh", onehot, rel_bias.astype(F32), precision=lax.Precision.HIGHEST)
    bias = jnp.where(valid[..., None], bias, NEG_INF)
    return jnp.transpose(bias, (2, 0, 1))


def _attn_prompt_kernel(sink_ref, q_ref, kc_ref, kp_ref, vc_ref, vp_ref, bias_ref, o_ref, *, l):
    first = pl.program_id(1) == 0
    w = WINDOW
    nq = q_ref.shape[0] // w
    kall = jnp.concatenate([kp_ref[...], kc_ref[...]], axis=0).astype(F32)
    vall = jnp.concatenate([vp_ref[...], vc_ref[...]], axis=0).astype(F32)
    lane = lax.broadcasted_iota(jnp.int32, kall.shape, 1)
    lo = lane < HEAD_DIM
    kroll = pltpu.roll(kall, HEAD_DIM, 1)
    vroll = pltpu.roll(vall, HEAD_DIM, 1)

    def halves(a, aroll, g):
        if g == 0:
            return jnp.where(lo, a, 0.0).astype(BF16), jnp.where(lo, 0.0, aroll).astype(BF16)
        return jnp.where(lo, aroll, 0.0).astype(BF16), jnp.where(lo, 0.0, a).astype(BF16)

    kh = [halves(kall, kroll, g) for g in range(N_KV)]
    vh = [halves(vall, vroll, g) for g in range(N_KV)]
    col = lax.broadcasted_iota(jnp.int32, (w, 4 * w), 1)
    prev_col = (col % (2 * w)) < w
    tiles_per_g = Q_PER_KV // 2
    units = [(qi, tile) for qi in range(nq) for tile in range(N_HEADS // 2)]
    for u0 in range(0, len(units), ATT_GROUP):
        group = units[u0:u0 + ATT_GROUP]
        scores = []
        for qi, tile in group:
            r0, g = qi * w, tile // tiles_per_g
            q = q_ref[r0:r0 + w, tile * LANES:(tile + 1) * LANES]
            kk = jnp.concatenate([kh[g][0][r0:r0 + 2 * w], kh[g][1][r0:r0 + 2 * w]], axis=0)
            s = _dot_nt(q, kk) * (HEAD_DIM ** -0.5) + bias_ref[tile]
            if qi == 0:
                s = jnp.where(jnp.logical_and(first, prev_col), NEG_INF, s)
            scores.append(s)
        probs = []
        for (qi, tile), s in zip(group, scores):
            ps = []
            for half in range(2):
                sh = s[:, half * 2 * w:(half + 1) * 2 * w]
                sink = sink_ref[l, 2 * tile + half]
                mx = jnp.maximum(jnp.max(sh, axis=-1, keepdims=True), sink)
                p = jnp.exp(sh - mx)
                den = jnp.sum(p, axis=-1, keepdims=True) + jnp.exp(sink - mx)
                ps.append((p * (1.0 / den)).astype(BF16))
            probs.append(jnp.concatenate(ps, axis=1))
        for (qi, tile), p in zip(group, probs):
            r0, g = qi * w, tile // tiles_per_g
            vv = jnp.concatenate([vh[g][0][r0:r0 + 2 * w], vh[g][1][r0:r0 + 2 * w]], axis=0)
            o_ref[r0:r0 + w, tile * LANES:(tile + 1) * LANES] = _dot(p, vv).astype(BF16)


def _attn_prompt(z, bias, sinks, l, bp, tp):
    w = WINDOW
    qb = ATT_QB if tp % (ATT_QB * w) == 0 else 1
    ns = tp // (qb * w)
    nb = tp // w
    kvw = N_KV * HEAD_DIM
    qw = N_HEADS * HEAD_DIM

    def cur(col):
        return lambda bb, i: (bb * ns + i, col)

    def prev(col):
        return lambda bb, i: (bb * nb + jnp.maximum(i * qb - 1, 0), col)

    return pl.pallas_call(
        functools.partial(_attn_prompt_kernel, l=l),
        grid=(bp, ns),
        in_specs=[pl.BlockSpec(memory_space=pltpu.SMEM),
                  pl.BlockSpec((qb * w, qw), cur(Z_Q // qw)),
                  pl.BlockSpec((qb * w, kvw), cur(Z_K // kvw)),
                  pl.BlockSpec((w, kvw), prev(Z_K // kvw)),
                  pl.BlockSpec((qb * w, kvw), cur(Z_V // kvw)),
                  pl.BlockSpec((w, kvw), prev(Z_V // kvw)),
                  pl.BlockSpec((N_HEADS // 2, w, 4 * w), lambda bb, i: (0, 0, 0))],
        out_specs=pl.BlockSpec((qb * w, qw), lambda bb, i: (bb * ns + i, 0)),
        out_shape=jax.ShapeDtypeStruct((bp * tp, qw), BF16),
        compiler_params=_cparams("parallel", "parallel"),
        name="attn_prompt",
    )(sinks, z, z, z, z, z, bias)


def _attn_sample_kernel(q_ref, kc_ref, vc_ref, kn_ref, vn_ref, bias_ref, sink_ref, o_ref):
    wb = kc_ref.shape[1]
    ts = kn_ref.shape[2]
    for g in range(N_KV):
        lo = g * HEAD_DIM
        qb = (q_ref[:, g].astype(F32) * (HEAD_DIM ** -0.5)).astype(BF16)
        kc = kc_ref[:, :, lo:lo + HEAD_DIM].astype(BF16)
        vc = vc_ref[:, :, lo:lo + HEAD_DIM].astype(BF16)
        kn = kn_ref[:, g].astype(F32)
        vn = vn_ref[:, g].astype(F32)
        bias = bias_ref[g]
        s_c = jnp.einsum("bqd,bkd->bqk", qb, kc, preferred_element_type=F32) + bias[None, :, :wb]
        qf = qb.astype(F32)
        s_n = [jnp.sum(qf * kn[:, j:j + 1, :], axis=-1, keepdims=True) + bias[None, :, wb + j:wb + j + 1]
               for j in range(ts)]
        sink = sink_ref[g][None]
        mx = jnp.maximum(jnp.max(s_c, axis=-1, keepdims=True), sink)
        for sj in s_n:
            mx = jnp.maximum(mx, sj)
        p_c = jnp.exp(s_c - mx)
        p_n = [jnp.exp(sj - mx) for sj in s_n]
        den = jnp.sum(p_c, axis=-1, keepdims=True) + jnp.exp(sink - mx)
        for pj in p_n:
            den = den + pj
        o = jnp.einsum("bqk,bkd->bqd", (p_c / den).astype(BF16), vc, preferred_element_type=F32)
        for j in range(ts):
            o = o + (p_n[j] / den).astype(BF16).astype(F32) * vn[:, j:j + 1, :]
        o_ref[:, g] = o


def _attn_sample(q4, kc, vc, kn, vn, bias, sinks, l, bs_blk):
    bs, _, rt, _ = q4.shape
    wb = kc.shape[2]
    ts = kn.shape[2]
    kvw = N_KV * HEAD_DIM
    return pl.pallas_call(
        _attn_sample_kernel,
        grid=(bs // bs_blk,),
        in_specs=[pl.BlockSpec((bs_blk, N_KV, rt, HEAD_DIM), lambda i: (i, 0, 0, 0)),
                  pl.BlockSpec((None, bs_blk, wb, kvw), lambda i: (l, i, 0, 0)),
                  pl.BlockSpec((None, bs_blk, wb, kvw), lambda i: (l, i, 0, 0)),
                  pl.BlockSpec((bs_blk, N_KV, ts, HEAD_DIM), lambda i: (i, 0, 0, 0)),
                  pl.BlockSpec((bs_blk, N_KV, ts, HEAD_DIM), lambda i: (i, 0, 0, 0)),
                  pl.BlockSpec((N_KV, rt, wb + ts), lambda i: (0, 0, 0)),
                  pl.BlockSpec((N_KV, rt, 1), lambda i: (0, 0, 0))],
        out_specs=pl.BlockSpec((bs_blk, N_KV, rt, HEAD_DIM), lambda i: (i, 0, 0, 0)),
        out_shape=jax.ShapeDtypeStruct((bs, N_KV, rt, HEAD_DIM), F32),
        compiler_params=_cparams("parallel"),
        name="attn_sample",
    )(q4, kc, vc, kn, vn, bias, sinks)


def _mlstm_kernel(*refs, t_valid, nseq, aliased, carried, chunk_axis, per_seq_inputs=False, first_layer=None):
    if aliased:
        refs = refs[:10] + refs[11:]
    if per_seq_inputs:
        groups = [refs[6 * s:6 * s + 6] for s in range(nseq)]
        if_ref, ifr_ref, q_ref, k_ref, v_ref, o_ref = (tuple(g[k] for g in groups) for k in range(6))
        refs = (None,) * 6 + refs[6 * nseq:]
    else:
        if_ref, ifr_ref, q_ref, k_ref, v_ref, o_ref = refs[:6]
    g_ref, c0_ref, n0_ref, m0_ref, h_ref, c1_ref, n1_ref, m1_ref = refs[6:14]
    c = pl.program_id(chunk_axis)
    L = (ifr_ref[0] if per_seq_inputs else ifr_ref).shape[-1]
    if carried:
        c_in, n_in, m_in = c_out, n_out, m_out = refs[14:]

        @pl.when(c == 0)
        def _():
            c_in[...] = c0_ref[...]
            n_in[...] = n0_ref[...]
            m_in[...] = m0_ref[...]
    else:
        (c_in, n_in, m_in), (c_out, n_out, m_out) = (c0_ref, n0_ref, m0_ref), (c1_ref, n1_ref, m1_ref)
        if first_layer is not None:
            for dd in range(c1_ref.shape[0]):
                if dd != first_layer:
                    c1_ref[dd] = jnp.zeros(c1_ref.shape[1:], F32)
            c_out = c1_ref.at[first_layer]

    def seq(ref, s):
        return ref[s] if isinstance(ref, tuple) else ref.at[s]

    tt = lax.broadcasted_iota(jnp.int32, (L, L), 0)
    ss = lax.broadcasted_iota(jnp.int32, (L, L), 1)
    causal = ss <= tt
    tril = causal.astype(F32)
    triu = (tt <= ss).astype(F32)
    seqs = range(nseq)
    heads = [(s_i, h) for s_i in seqs for h in range(M_HEADS)]
    mxu_sums = L % LANES == 0
    gates = []
    for s_i in seqs:
        ifc = seq(if_ref, s_i)[...]
        ifr = ifr_ref[s_i][0] if per_seq_inputs else ifr_ref[s_i]
        lf_c = _log_sigmoid(ifc)
        lf_r = _log_sigmoid(ifr)
        i_c, i_r = ifc, ifr
        if t_valid < L:
            rc = lax.broadcasted_iota(jnp.int32, (L, LANES), 0) < t_valid
            rr = lax.broadcasted_iota(jnp.int32, (2 * M_HEADS, L), 1) < t_valid
            lf_c = jnp.where(rc, lf_c, 0.0)
            lf_r = jnp.where(rr, lf_r, 0.0)
            i_c = jnp.where(rc, i_c, NEG_INF)
            i_r = jnp.where(rr, i_r, NEG_INF)
        gates.append((lf_c, lf_r, i_c, i_r))
    f_cs = [_dot_hi(tril, g[0]) for g in gates]
    f_rs = [_dot_hi(g[1], triu) for g in gates]
    state = {(s_i, h): (c_in[s_i, h], n_in[s_i, h], m_in[s_i, h:h + 1, :]) for s_i, h in heads}

    def wide(col):
        return jnp.concatenate([col] * (L // LANES), axis=1) if L >= LANES else col[:, :L]

    st1 = {}
    for s_i, h in heads:
        _, nrow, m0 = state[s_i, h]
        fc = jnp.broadcast_to(f_cs[s_i][:, M_HEADS + h:M_HEADS + h + 1], (L, LANES))
        fr = f_rs[s_i][M_HEADS + h:M_HEADS + h + 1, :]
        ir = gates[s_i][3][h:h + 1, :]
        dm = jnp.where(causal, wide(fc) - fr + ir, NEG_INF)
        m_t = jnp.maximum(m0 + fc, jnp.max(dm, axis=-1, keepdims=True))
        st1[s_i, h] = (fc, dm, m_t, jnp.exp(m0 + fc - m_t))
    st2 = {}
    for s_i, h in heads:
        lo = h * M_DK
        fc, dm, m_t, inter = st1[s_i, h]
        qb = seq(q_ref, s_i)[:, lo:lo + M_DK]
        kf = seq(k_ref, s_i)[:, lo:lo + M_DK].astype(F32) * (M_DK ** -0.5)
        vf = seq(v_ref, s_i)[:, lo:lo + M_DV].astype(F32)
        kb, vb = kf.astype(BF16), vf.astype(BF16)
        if mxu_sums:
            n_rows = jnp.broadcast_to(state[s_i, h][1], (M_DK, M_DK)).astype(BF16)
            qk = _dot_nt(qb, jnp.concatenate([kb, n_rows], axis=0))
            sc, qn_rep = qk[:, :L] * jnp.exp(dm - wide(m_t)), qk[:, L:]
        else:
            sc, qn_rep = _dot_nt(qb, kb) * jnp.exp(dm - wide(m_t)), None
        st2[s_i, h] = (qb, kf, vf, kb, vb, sc, qn_rep)
    st3 = {}
    for s_i, h in heads:
        cm, nrow, m0 = state[s_i, h]
        fc, dm, m_t, inter = st1[s_i, h]
        qb, kf, vf, kb, vb, sc, qn_rep = st2[s_i, h]
        if mxu_sums:
            sv = _dot(sc.astype(BF16), jnp.concatenate([vb, jnp.ones((L, M_DV), BF16)], axis=1))
            num = inter * _dot(qb, cm.astype(BF16)) + sv[:, :M_DV]
            qn = inter * qn_rep + sv[:, M_DV:]
            floor = jnp.exp(-m_t)
        else:
            num = inter * _dot(qb, cm.astype(BF16)) + _dot(sc.astype(BF16), vb)
            qn = (inter * jnp.sum(qb.astype(F32) * nrow, axis=-1, keepdims=True)
                  + jnp.sum(sc, axis=-1, keepdims=True))
            floor = jnp.exp(-m_t)
        st3[s_i, h] = num / jnp.maximum(jnp.abs(qn), floor)
    new_state = {}
    for s_i, h in heads:
        cm, nrow, m0 = state[s_i, h]
        fc, dm, m_t, inter = st1[s_i, h]
        qb, kf, vf, kb, vb, sc, _ = st2[s_i, h]
        ic = jnp.broadcast_to(gates[s_i][2][:, h:h + 1], (L, LANES))
        m_end = m_t[L - 1:L, :]
        f_end = fc[L - 1:L, :]
        decay = jnp.exp(m0 + f_end - m_end)
        w_s = jnp.exp(f_end - fc + ic - m_end)
        new_state[s_i, h] = (decay * cm + _dot_tn(kb, (w_s * vf).astype(BF16)),
                             decay * nrow + jnp.sum(w_s * kf, axis=0, keepdims=True),
                             m_end)
    for s_i, h in heads:
        lo = h * M_DK
        hn = _norm(st3[s_i, h]) * g_ref[:, lo:lo + M_DV]
        gate = _sigmoid(seq(o_ref, s_i)[:, lo:lo + M_DV].astype(F32))
        seq(h_ref, s_i)[:, lo:lo + M_DV] = (gate * hn).astype(BF16)
    for s_i, h in heads:
        c_out[s_i, h], n_out[s_i, h], m_out[s_i, h:h + 1, :] = new_state[s_i, h]
    if not carried:
        for s_i in seqs:
            m_out[s_i, M_HEADS:, :] = jnp.zeros((M_HEADS, LANES), F32)

    if carried:
        @pl.when(c == pl.num_programs(chunk_axis) - 1)
        def _():
            c1_ref[...] = c_out[...]
            n1_ref[...] = n_out[...]
            m1_ref[...] = m_out[...]


def _mlstm_state_specs(nseq, l_state):
    if l_state is None:
        c_spec = pl.BlockSpec((nseq, M_HEADS, M_DK, M_DV), lambda b, c: (b, 0, 0, 0))
    else:
        c_spec = pl.BlockSpec((None, nseq, M_HEADS, M_DK, M_DV), lambda b, c: (l_state, b, 0, 0, 0))
    return (c_spec,
            pl.BlockSpec((nseq, M_HEADS, 1, M_DK), lambda b, c: (b, 0, 0, 0)),
            pl.BlockSpec((nseq, 2 * M_HEADS, LANES), lambda b, c: (b, 0, 0)))


def _mlstm_scratch(nseq):
    return [pltpu.VMEM((nseq, M_HEADS, M_DK, M_DV), F32),
            pltpu.VMEM((nseq, M_HEADS, 1, M_DK), F32),
            pltpu.VMEM((nseq, 2 * M_HEADS, LANES), F32)]


def _mlstm_prompt(z, zif, ifr, gamma, l, bp, nc, L):
    hw = M_HEADS * M_DK
    zero = lambda *s: jnp.zeros(s, F32)
    in_specs, args = [], []
    for b in range(bp):
        def rows(col, b=b):
            return lambda c: (b * nc + c, col)

        in_specs += [pl.BlockSpec((L, LANES), rows(0)),
                     pl.BlockSpec((1, 2 * M_HEADS, L), lambda c, b=b: (b * nc + c, 0, 0)),
                     pl.BlockSpec((L, hw), rows(Z_MQ // hw)),
                     pl.BlockSpec((L, hw), rows(Z_MK // hw)),
                     pl.BlockSpec((L, hw), rows(Z_MV // hw)),
                     pl.BlockSpec((L, hw), rows(Z_MO // hw))]
        args += [zif, ifr, z, z, z, z]
    state_specs = [pl.BlockSpec((bp, M_HEADS, M_DK, M_DV), lambda c: (0, 0, 0, 0)),
                   pl.BlockSpec((bp, M_HEADS, 1, M_DK), lambda c: (0, 0, 0, 0)),
                   pl.BlockSpec((bp, 2 * M_HEADS, LANES), lambda c: (0, 0, 0))]
    kern = functools.partial(_mlstm_kernel, t_valid=L, nseq=bp, aliased=False, carried=True, chunk_axis=0,
                             per_seq_inputs=True)
    outs = pl.pallas_call(
        kern,
        grid=(nc,),
        in_specs=in_specs + [pl.BlockSpec((None, 1, hw), lambda c: (l, 0, 0))] + state_specs,
        out_specs=[pl.BlockSpec((bp, L, hw), lambda c: (0, c, 0))] + state_specs,
        out_shape=[jax.ShapeDtypeStruct((bp, nc * L, hw), BF16),
                   jax.ShapeDtypeStruct((bp, M_HEADS, M_DK, M_DV), F32),
                   jax.ShapeDtypeStruct((bp, M_HEADS, 1, M_DK), F32),
                   jax.ShapeDtypeStruct((bp, 2 * M_HEADS, LANES), F32)],
        scratch_shapes=_mlstm_scratch(bp),
        compiler_params=_cparams("arbitrary"),
        name="mlstm_prompt",
    )(*args, gamma, zero(bp, M_HEADS, M_DK, M_DV), zero(bp, M_HEADS, 1, M_DK), zero(bp, 2 * M_HEADS, LANES))
    return (outs[0].reshape(bp * nc * L, hw),) + tuple(outs[1:])


def _mlstm_sample(zm3, if3, ifr, gamma, c_all, n0, m0x, c_out_prev, l, t_valid, nseq):
    bs, L, _ = zm3.shape
    depth = c_all.shape[0]
    hw = M_HEADS * M_DK
    aliased = c_out_prev is not None

    def blk(col):
        return pl.BlockSpec((nseq, L, hw), lambda b, c: (b, 0, col))

    c_in, n_spec, m_spec = _mlstm_state_specs(nseq, l)
    in_specs = [pl.BlockSpec((nseq, L, LANES), lambda b, c: (b, 0, 0)),
                pl.BlockSpec((nseq, 2 * M_HEADS, L), lambda b, c: (b, 0, 0)),
                blk(0), blk(1), blk(2), blk(3),
                pl.BlockSpec((None, 1, hw), lambda b, c: (l, 0, 0)),
                c_in, n_spec, m_spec]
    args = [if3, ifr, zm3, zm3, zm3, zm3, gamma, c_all, n0, m0x]
    aliases = {}
    if aliased:
        in_specs.append(pl.BlockSpec(memory_space=pl.ANY))
        args.append(c_out_prev)
        aliases = {len(args) - 1: 1}
        c_out = c_in
    else:
        c_out = pl.BlockSpec((depth, nseq, M_HEADS, M_DK, M_DV), lambda b, c: (0, b, 0, 0, 0))
    kern = functools.partial(_mlstm_kernel, t_valid=t_valid, nseq=nseq, aliased=aliased, carried=False,
                             chunk_axis=1, first_layer=None if aliased else l)
    return pl.pallas_call(
        kern,
        grid=(bs // nseq, 1),
        in_specs=in_specs,
        out_specs=[pl.BlockSpec((nseq, L, hw), lambda b, c: (b, 0, 0)), c_out, n_spec, m_spec],
        out_shape=[jax.ShapeDtypeStruct((bs, L, hw), BF16),
                   jax.ShapeDtypeStruct((depth, bs, M_HEADS, M_DK, M_DV), F32),
                   jax.ShapeDtypeStruct((bs, M_HEADS, 1, M_DK), F32),
                   jax.ShapeDtypeStruct((bs, 2 * M_HEADS, LANES), F32)],
        input_output_aliases=aliases,
        compiler_params=_cparams("parallel", "arbitrary"),
        name="mlstm_sample",
    )(*args)


def _m_state_in(m):
    lead = m.shape[:-1]
    mx = jnp.zeros(lead + (2 * M_HEADS, LANES), F32)
    return mx.at[..., :M_HEADS, :].set(jnp.broadcast_to(m[..., None], lead + (M_HEADS, LANES)))


def _merge_kernel(*refs, np_tiles, alpha, route):
    (cp_ref, cs_ref, ap_ref, as_ref, mp_ref, ms_ref, g0_ref, g1_ref, g2_ref, x_ref,
     gtp_ref, gts_ref, shp_ref, shs_ref, scp_ref, scs_ref,
     wc_ref, wa_ref, wm_ref, wo_ref, pg_ref, pb_ref) = refs[:22]
    if route:
        rw_ref, rb_ref, x1_ref, h_ref, route_ref = refs[22:]
    else:
        x1_ref, h_ref = refs[22:]
    is_s = pl.program_id(0) >= np_tiles

    def gate(ref):
        return _sigmoid(ref[...].astype(F32))

    y = (gate(g0_ref) * _dot(_pick(is_s, cp_ref, cs_ref), wc_ref[...])
         + gate(g1_ref) * _dot(_pick(is_s, ap_ref, as_ref), wa_ref[...])
         + gate(g2_ref) * _dot(_pick(is_s, mp_ref, ms_ref), wm_ref[...]))
    mix = _dot(y.astype(BF16), wo_ref[...])
    x1 = _norm(alpha * x_ref[...] + _pick(is_s, gtp_ref, gts_ref) * mix) * pg_ref[...] + pb_ref[...]
    x1_ref[...] = x1
    h = _norm(x1) * (1.0 + _pick(is_s, scp_ref, scs_ref)) + _pick(is_s, shp_ref, shs_ref)
    h_ref[...] = h.astype(h_ref.dtype)
    if route:
        h_hi, h_lo = _split_bf16(h)
        w_hi, w_lo = _split_bf16(rw_ref[...])
        logits = _dot(h_hi, w_hi) + (_dot(h_hi, w_lo) + _dot(h_lo, w_hi)) + rb_ref[...]
        lane = lax.broadcasted_iota(jnp.int32, logits.shape, 1)
        m1 = jnp.max(logits, axis=-1, keepdims=True)
        e1 = jnp.min(jnp.where(logits == m1, lane, LANES), axis=-1, keepdims=True)
        l2 = jnp.where(lane == e1, NEG_INF, logits)
        m2 = jnp.max(l2, axis=-1, keepdims=True)
        e2 = jnp.min(jnp.where(l2 == m2, lane, LANES), axis=-1, keepdims=True)
        ex = jnp.exp(m2 - m1)
        w1 = 1.0 / (1.0 + ex)
        w2 = ex / (1.0 + ex)
        out = jnp.where(lane == 0, e1.astype(F32),
                        jnp.where(lane == 1, e2.astype(F32),
                                  jnp.where(lane == 2, w1, jnp.where(lane == 3, w2, 0.0))))
        route_ref[...] = out


def _merge(cp, cs, ap, as_, mp, ms, z, x, mod_p, mod_s, wc, wa, wm, wo, pg, pb, l, rm, alpha, router):
    n, d = x.shape
    hw = D_CONV
    route = router is not None
    tm = rm.t

    def pblk():
        return pl.BlockSpec((tm, hw), lambda i: (rm.prompt(i), 0))

    def sblk():
        return pl.BlockSpec((tm, hw), lambda i: (rm.sample(i), 0))

    def zg(k):
        return pl.BlockSpec((tm, d), lambda i: (i, Z_G // d + k))

    def lw(a):
        return pl.BlockSpec((None,) + a.shape[1:], lambda i: (l,) + (0,) * (a.ndim - 1))

    post = pl.BlockSpec((None, None, 1, d), lambda i: (l, 0, 0, 0))
    in_specs = [pblk(), sblk(), pblk(), sblk(), pblk(), sblk(), zg(0), zg(1), zg(2),
                pl.BlockSpec((tm, d), lambda i: (i, 0)),
                *rm.mod_specs(l, 2, d), *rm.mod_specs(l, 3, d), *rm.mod_specs(l, 4, d),
                lw(wc), lw(wa), lw(wm), lw(wo), post, post]
    args = [cp, cs, ap, as_, mp, ms, z, z, z, x, mod_p, mod_s, mod_p, mod_s, mod_p, mod_s,
            wc, wa, wm, wo, pg, pb]
    out_specs = [pl.BlockSpec((tm, d), lambda i: (i, 0)), pl.BlockSpec((tm, d), lambda i: (i, 0))]
    out_shape = [jax.ShapeDtypeStruct((n, d), F32), jax.ShapeDtypeStruct((n, d), F32 if route else BF16)]
    if route:
        rw, rb, lj = router
        in_specs += [pl.BlockSpec((None, d, LANES), lambda i: (lj, 0, 0)),
                     pl.BlockSpec((None, 1, LANES), lambda i: (lj, 0, 0))]
        args += [rw, rb]
        out_specs.append(pl.BlockSpec((tm, LANES), lambda i: (i, 0)))
        out_shape.append(jax.ShapeDtypeStruct((n, LANES), F32))
    kern = functools.partial(_merge_kernel, np_tiles=rm.np_tiles, alpha=alpha, route=route)
    return pl.pallas_call(
        kern, grid=(n // tm,), in_specs=in_specs, out_specs=out_specs, out_shape=out_shape,
        compiler_params=_cparams("parallel"), name="merge_route" if route else "merge",
    )(*args)


def _post_residual(x_ref, gp_ref, gs_ref, f, pg_ref, pb_ref, is_s, alpha):
    return _norm(alpha * x_ref[...] + _pick(is_s, gp_ref, gs_ref) * f) * pg_ref[...] + pb_ref[...]


def _ffn_kernel(h_ref, wg_ref, wu_ref, wd_ref, x_ref, gp_ref, gs_ref, pg_ref, pb_ref, o_ref, acc,
                *, alpha, np_tiles):
    j = pl.program_id(1)
    h = h_ref[...]
    a = (_silu(_dot(h, wg_ref[...])) * _dot(h, wu_ref[...])).astype(BF16)
    part = _dot(a, wd_ref[...])

    @pl.when(j == 0)
    def _():
        acc[...] = part

    @pl.when(j > 0)
    def _():
        acc[...] = acc[...] + part

    @pl.when(j == pl.num_programs(1) - 1)
    def _():
        is_s = pl.program_id(0) >= np_tiles
        o_ref[...] = _post_residual(x_ref, gp_ref, gs_ref, acc[...], pg_ref, pb_ref, is_s, alpha)


def _ffn_dense(h, wg, wu, wd, x, mod_p, mod_s, pg, pb, l, lj, rm, tf, alpha):
    n, d = x.shape
    f = wg.shape[2]
    tm = rm.t
    post = pl.BlockSpec((None, None, 1, d), lambda i, j: (l, 1, 0, 0))
    return pl.pallas_call(
        functools.partial(_ffn_kernel, alpha=alpha, np_tiles=rm.np_tiles),
        grid=(n // tm, f // tf),
        in_specs=[pl.BlockSpec((tm, d), lambda i, j: (i, 0)),
                  pl.BlockSpec((None, d, tf), lambda i, j: (lj, 0, j)),
                  pl.BlockSpec((None, d, tf), lambda i, j: (lj, 0, j)),
                  pl.BlockSpec((None, tf, d), lambda i, j: (lj, j, 0)),
                  pl.BlockSpec((tm, d), lambda i, j: (i, 0)),
                  *rm.mod_specs(l, 5, d), post, post],
        out_specs=pl.BlockSpec((tm, d), lambda i, j: (i, 0)),
        out_shape=jax.ShapeDtypeStruct((n, d), F32),
        scratch_shapes=[pltpu.VMEM((tm, d), F32)],
        compiler_params=_cparams("parallel", "arbitrary"),
        name="ffn_dense",
    )(h, wg, wu, wd, x, mod_p, mod_s, pg, pb)


def _rank_kernel(route_ref, rank_ref, tot_ref, carry):
    i = pl.program_id(0)
    tm = route_ref.shape[0]

    @pl.when(i == 0)
    def _():
        carry[...] = jnp.zeros_like(carry)

    r = route_ref[...]
    lane = lax.broadcasted_iota(jnp.int32, (tm, LANES), 1)
    e1 = r[:, 0:1].astype(jnp.int32)
    e2 = r[:, 1:2].astype(jnp.int32)
    hit1 = lane == e1
    hit2 = lane == e2
    onehot = jnp.where(jnp.logical_or(hit1, hit2), 1.0, 0.0)
    tt = lax.broadcasted_iota(jnp.int32, (tm, tm), 0)
    ss = lax.broadcasted_iota(jnp.int32, (tm, tm), 1)
    before = jnp.where(ss < tt, 1.0, 0.0).astype(BF16)
    cnt = _dot(before, onehot.astype(BF16)) + carry[0:1, :]
    r1 = jnp.sum(jnp.where(hit1, cnt, 0.0), axis=-1, keepdims=True)
    r2 = jnp.sum(jnp.where(hit2, cnt, 0.0), axis=-1, keepdims=True)
    rank_ref[...] = jnp.where(lane == 0, r1, jnp.where(lane == 1, r2, 0.0))
    carry[...] = carry[...] + jnp.sum(onehot, axis=0, keepdims=True)
    tot_ref[...] = carry[...]


def _moe_rank(route, tm):
    n = route.shape[0]
    return pl.pallas_call(
        _rank_kernel,
        grid=(n // tm,),
        in_specs=[pl.BlockSpec((tm, LANES), lambda i: (i, 0))],
        out_specs=[pl.BlockSpec((tm, LANES), lambda i: (i, 0)),
                   pl.BlockSpec((8, LANES), lambda i: (0, 0))],
        out_shape=[jax.ShapeDtypeStruct((n, LANES), F32), jax.ShapeDtypeStruct((8, LANES), F32)],
        scratch_shapes=[pltpu.VMEM((8, LANES), F32)],
        compiler_params=_cparams("arbitrary"),
        name="moe_rank",
    )(route)


def _row_copy(src, s, dst, t, sem):
    return pltpu.make_async_copy(src.at[pl.ds(s, 1)], dst.at[pl.ds(t, 1)], sem)


def _dispatch_kernel(d1_ref, d2_ref, zl_ref, h_ref, xs_hbm, zbuf, sem, zsem):
    i = pl.program_id(0)
    tm = h_ref.shape[0]

    @pl.when(i == 0)
    def _():
        zbuf[...] = jnp.zeros_like(zbuf)

        def zero_copy(k):
            row = pl.multiple_of(jnp.maximum(zl_ref[k], 0), MOE_SUB)
            return pltpu.make_async_copy(zbuf, xs_hbm.at[pl.ds(row, MOE_SUB)], zsem)

        def start(k, carry):
            @pl.when(zl_ref[k] >= 0)
            def _():
                zero_copy(k).start()
            return carry

        def wait(k, carry):
            @pl.when(zl_ref[k] >= 0)
            def _():
                zero_copy(k).wait()
            return carry

        lax.fori_loop(0, zl_ref.shape[0], start, 0)
        lax.fori_loop(0, zl_ref.shape[0], wait, 0)

    def issue(r, carry):
        _row_copy(h_ref, r, xs_hbm, d1_ref[i * tm + r], sem).start(priority=0)
        _row_copy(h_ref, r, xs_hbm, d2_ref[i * tm + r], sem).start(priority=1)
        return carry

    lax.fori_loop(0, tm, issue, 0, unroll=8)
    for _ in range(2):
        pltpu.make_async_copy(h_ref, xs_hbm.at[pl.ds(0, tm)], sem).wait()


def _moe_dispatch(dest1, dest2, zlist, h, n_rows, tm):
    n, d = h.shape
    grid_spec = pltpu.PrefetchScalarGridSpec(
        num_scalar_prefetch=3,
        grid=(n // tm,),
        in_specs=[pl.BlockSpec((tm, d), lambda i, *_: (i, 0))],
        out_specs=pl.BlockSpec(memory_space=pl.ANY),
        scratch_shapes=[pltpu.VMEM((MOE_SUB, d), F32), pltpu.SemaphoreType.DMA(()), pltpu.SemaphoreType.DMA(())],
    )
    return pl.pallas_call(
        _dispatch_kernel, grid_spec=grid_spec,
        out_shape=jax.ShapeDtypeStruct((n_rows, d), F32),
        compiler_params=_cparams("arbitrary"),
        name="moe_dispatch",
    )(dest1, dest2, zlist, h)


def _expert_kernel(blk_e_ref, nvalid_ref, xs_ref, wg_ref, wu_ref, wd_ref, y_ref, xb, acc):
    b = pl.program_id(0)
    j = pl.program_id(1)
    tb = xb.shape[0]
    nsub = tb // MOE_SUB
    nv = nvalid_ref[b]
    wg = wg_ref[...].astype(BF16)
    wu = wu_ref[...].astype(BF16)
    wd = wd_ref[...].astype(BF16)

    def run(rows):
        @pl.when(j == 0)
        def _():
            xb[rows, :] = xs_ref[rows, :].astype(BF16)

        x = xb[rows, :]
        a = (_silu(_dot(x, wg)) * _dot(x, wu)).astype(BF16)
        part = _dot(a, wd)

        @pl.when(j == 0)
        def _():
            acc[rows, :] = part

        @pl.when(j > 0)
        def _():
            acc[rows, :] = acc[rows, :] + part

        @pl.when(j == pl.num_programs(1) - 1)
        def _():
            y_ref[rows, :] = acc[rows, :]

    @pl.when(nv == tb)
    def _():
        run(pl.ds(0, tb))

    for s in range(nsub):
        rows = pl.ds(s * MOE_SUB, MOE_SUB)

        @pl.when(jnp.logical_and(s * MOE_SUB < nv, nv < tb))
        def _():
            run(rows)

        @pl.when(jnp.logical_and(s * MOE_SUB >= nv, j == pl.num_programs(1) - 1))
        def _():
            y_ref[rows, :] = jnp.zeros((MOE_SUB, y_ref.shape[1]), F32)


def _moe_experts(blk_e, nvalid, xs, wg, wu, wd, lj, tb, tf):
    n_rows, d = xs.shape
    f = wg.shape[3]
    nj = f // tf

    def jm(b, j, nv):
        return jnp.where(nv[b] > 0, j, nj - 1)

    grid_spec = pltpu.PrefetchScalarGridSpec(
        num_scalar_prefetch=2,
        grid=(n_rows // tb, nj),
        in_specs=[pl.BlockSpec((tb, d), lambda b, j, be, nv: (b, 0)),
                  pl.BlockSpec((None, None, d, tf), lambda b, j, be, nv: (lj, be[b], 0, jm(b, j, nv))),
                  pl.BlockSpec((None, None, d, tf), lambda b, j, be, nv: (lj, be[b], 0, jm(b, j, nv))),
                  pl.BlockSpec((None, None, tf, d), lambda b, j, be, nv: (lj, be[b], jm(b, j, nv), 0))],
        out_specs=pl.BlockSpec((tb, d), lambda b, j, be, nv: (b, 0)),
        scratch_shapes=[pltpu.VMEM((tb, d), BF16), pltpu.VMEM((tb, d), F32)],
    )
    return pl.pallas_call(
        _expert_kernel, grid_spec=grid_spec,
        out_shape=jax.ShapeDtypeStruct((n_rows, d), F32),
        compiler_params=_cparams("parallel", "arbitrary"),
        name="moe_experts",
    )(blk_e, nvalid, xs, wg, wu, wd)


def _combine_kernel(d1_ref, d2_ref, ys_hbm, route_ref, x_ref, gp_ref, gs_ref, pg_ref, pb_ref, *rest,
                    alpha, np_tiles, split):
    (*outs, y1, y2, sem1, sem2) = rest
    i = pl.program_id(0)
    tm = x_ref.shape[0]

    def issue(r, carry):
        _row_copy(ys_hbm, d1_ref[i * tm + r], y1, r, sem1).start(priority=0)
        _row_copy(ys_hbm, d2_ref[i * tm + r], y2, r, sem2).start(priority=1)
        return carry

    lax.fori_loop(0, tm, issue, 0, unroll=8)
    pltpu.make_async_copy(ys_hbm.at[pl.ds(0, tm)], y1, sem1).wait()
    pltpu.make_async_copy(ys_hbm.at[pl.ds(0, tm)], y2, sem2).wait()
    r = route_ref[...]
    f = y1[...] * r[:, 2:3] + y2[...] * r[:, 3:4]
    out = _post_residual(x_ref, gp_ref, gs_ref, f, pg_ref, pb_ref, i >= np_tiles, alpha)
    if split:
        op_ref, os_ref = outs

        @pl.when(i < np_tiles)
        def _():
            op_ref[...] = out

        @pl.when(i >= np_tiles)
        def _():
            os_ref[...] = out
    else:
        outs[0][...] = out


def _moe_combine(dest1, dest2, ys, route, x, mod_p, mod_s, pg, pb, l, rm, alpha, split):
    n, d = x.shape
    tm = rm.t
    if split:
        n_p = rm.np_tiles * tm
        out_specs = [pl.BlockSpec((tm, d), lambda i, *_: (rm.prompt(i), 0)),
                     pl.BlockSpec((tm, d), lambda i, *_: (rm.sample(i), 0))]
        out_shape = [jax.ShapeDtypeStruct((n_p, d), F32), jax.ShapeDtypeStruct((n - n_p, d), F32)]
    else:
        out_specs = pl.BlockSpec((tm, d), lambda i, *_: (i, 0))
        out_shape = jax.ShapeDtypeStruct((n, d), F32)
    post = pl.BlockSpec((None, None, 1, d), lambda i, *_: (l, 1, 0, 0))
    grid_spec = pltpu.PrefetchScalarGridSpec(
        num_scalar_prefetch=2,
        grid=(n // tm,),
        in_specs=[pl.BlockSpec(memory_space=pl.ANY),
                  pl.BlockSpec((tm, LANES), lambda i, *_: (i, 0)),
                  pl.BlockSpec((tm, d), lambda i, *_: (i, 0)),
                  *rm.mod_specs(l, 5, d), post, post],
        out_specs=out_specs,
        scratch_shapes=[pltpu.VMEM((tm, d), F32), pltpu.VMEM((tm, d), F32),
                        pltpu.SemaphoreType.DMA(()), pltpu.SemaphoreType.DMA(())],
    )
    return pl.pallas_call(
        functools.partial(_combine_kernel, alpha=alpha, np_tiles=rm.np_tiles, split=split), grid_spec=grid_spec,
        out_shape=out_shape,
        compiler_params=_cparams("arbitrary"),
        name="moe_combine",
    )(dest1, dest2, ys, route, x, mod_p, mod_s, pg, pb)


def _moe_ffn(h, route, wg, wu, wd, x, mod_p, mod_s, pg, pb, l, lj, rm, tb, tf, alpha, split):
    n = h.shape[0]
    rank, tot = _moe_rank(route, rm.t)
    counts = tot[0, :N_EXPERTS].astype(jnp.int32)
    padded = (counts + tb - 1) // tb * tb
    pad_end = jnp.cumsum(padded)
    pad_start = pad_end - padded
    experts = jnp.arange(N_EXPERTS, dtype=jnp.int32)

    def slot(col):
        e = route[:, col].astype(jnp.int32)
        start = jnp.sum(jnp.where(e[:, None] == experts[None, :], pad_start[None, :], 0), axis=1)
        return start + rank[:, col].astype(jnp.int32)

    dest1, dest2 = slot(0), slot(1)
    n_blocks = -(-(2 * n + N_EXPERTS * (tb - 1)) // tb)
    blk_start = jnp.arange(n_blocks, dtype=jnp.int32) * tb
    blk_e = jnp.minimum(jnp.sum(pad_end[None, :] <= blk_start[:, None], axis=1), N_EXPERTS - 1).astype(jnp.int32)
    own_end = jnp.sum(jnp.where(blk_e[:, None] == experts[None, :], (pad_start + counts)[None, :], 0), axis=1)
    nvalid = jnp.clip(own_end - blk_start, 0, tb).astype(jnp.int32)
    nvalid = jnp.where(blk_start < pad_end[-1], nvalid, 0)
    per_blk = tb // MOE_SUB
    sub_start = jnp.arange(n_blocks * per_blk, dtype=jnp.int32) * MOE_SUB
    sub_room = jnp.repeat(blk_start + nvalid, per_blk) - sub_start
    zlist = jnp.where(sub_room < MOE_SUB, sub_start, -1).astype(jnp.int32)
    xs = _moe_dispatch(dest1, dest2, zlist, h, n_blocks * tb, rm.t)
    ys = _moe_experts(blk_e, nvalid, xs, wg, wu, wd, lj, tb, tf)
    return _moe_combine(dest1, dest2, ys, route, x, mod_p, mod_s, pg, pb, l, rm, alpha, split)


def _pack_w_in(w_in, b_in):
    depth, d, _ = w_in.shape
    q_end = 2 * D_CONV + N_HEADS * HEAD_DIM
    k_end = q_end + N_KV * HEAD_DIM
    a_end = k_end + N_KV * HEAD_DIM
    m_end = a_end + 4 * M_HEADS * M_DK
    if_end = m_end + 2 * M_HEADS

    def pack(a):
        return jnp.concatenate([a[..., if_end:], a[..., :q_end], a[..., a_end:m_end], a[..., q_end:a_end]], axis=-1)

    def gates(a):
        return jnp.pad(a[..., m_end:if_end], [(0, 0)] * (a.ndim - 1) + [(0, LANES - 2 * M_HEADS)])

    return (pack(w_in).astype(BF16), pack(b_in).reshape(depth, 1, Z_W),
            gates(w_in).astype(BF16), gates(b_in).reshape(depth, 1, LANES))


def kernel(x_prompt, x_sample, cache_swa_k, cache_swa_v, state_conv, state_mlstm_C, state_mlstm_n, state_mlstm_m, c_prompt, c_sample, w_ada, b_ada, w_in, b_in, conv_w, conv_b, conv_ln_g, conv_ln_b, w_conv_out, attn_sinks, rel_bias, w_attn_out, m_norm_g, w_m_out, w_out, post_ln_g, post_ln_b, ffn_w_gate, ffn_w_up, ffn_w_down, router_w, router_b, moe_w_gate, moe_w_up, moe_w_down):
    bp, tp, d = x_prompt.shape
    bs, ts, _ = x_sample.shape
    depth = w_ada.shape[0]
    alpha = (2 * depth) ** 0.25
    n_p, n_s = bp * tp, bs * ts
    tm = n_s
    assert d == D_MODEL and tp % tm == 0 and tp % WINDOW == 0 and tm % 32 == 0
    rm = _RowMap(tm, bp, tp, n_p)
    rm_half = _RowMap(tm // 2, bp, tp, n_p)
    wb = cache_swa_k.shape[2]
    big = n_p >= 4096
    tc = 512 if big else tm
    lm = 256 if big else min(tp, 128)
    tb = 1024 if big else 2 * MOE_SUB
    bs_blk = 32 if bs % 32 == 0 else bs
    bs_att = 16 if bs % 16 == 0 else bs
    bs_m = 4 if bs % 4 == 0 else 1
    lts = 16

    x = jnp.concatenate([x_prompt.reshape(n_p, d), jnp.transpose(x_sample, (1, 0, 2)).reshape(n_s, d)], axis=0)

    nc_rows = -(-(bp + bs) // 8) * 8
    c_all = jnp.zeros((nc_rows, d), F32).at[:bp].set(c_prompt).at[bp:bp + bs].set(c_sample)
    mod = _ada_mod(c_all, w_ada, b_ada)
    mod_p = mod[:, :bp].reshape(depth, bp, 1, 6 * d)
    mod_s = jnp.tile(mod[:, bp:bp + bs], (1, ts, 1))

    w_in_p, b_in_p, w_if, b_if = _pack_w_in(w_in, b_in)
    wc_b, wa_b, wm_b, wo_b = (w.astype(BF16) for w in (w_conv_out, w_attn_out, w_m_out, w_out))
    fg_b, fu_b, fd_b = (w.astype(BF16) for w in (ffn_w_gate, ffn_w_up, ffn_w_down))
    cw_pad = jnp.pad(conv_w, ((0, 0), (0, CONV_PAD - CONV_W), (0, 0)))
    cvecs = [v.reshape(depth, 1, D_CONV) for v in (conv_b, conv_ln_g, conv_ln_b)]
    rw_pad = jnp.pad(router_w, ((0, 0), (0, 0), (0, LANES - N_EXPERTS)))
    rb_pad = jnp.pad(router_b, ((0, 0), (0, LANES - N_EXPERTS)), constant_values=NEG_INF)[:, None, :]
    pg = post_ln_g.reshape(depth, 2, 1, d)
    pb = post_ln_b.reshape(depth, 2, 1, d)
    gamma = m_norm_g.reshape(depth, 1, M_HEADS * M_DV)
    sinks = attn_sinks.astype(F32)

    qi = jnp.arange(WINDOW)[:, None]
    kj = jnp.arange(2 * WINDOW)[None, :]
    dist_p = qi + WINDOW - kj
    bh = _bias_heads(rel_bias, dist_p, (dist_p >= 0) & (dist_p < WINDOW))
    bias_p = bh.reshape(N_HEADS // 2, 2, WINDOW, 2 * WINDOW).transpose(0, 2, 1, 3).reshape(
        N_HEADS // 2, WINDOW, 4 * WINDOW)
    dist_s = jnp.arange(ts)[:, None] + wb - jnp.arange(wb + ts)[None, :]
    bias_s = _bias_heads(rel_bias, dist_s, (dist_s >= 0) & (dist_s < WINDOW)).reshape(
        N_KV, Q_PER_KV * ts, wb + ts)
    kc_all = cache_swa_k.reshape(depth, bs, wb, N_KV * HEAD_DIM)
    vc_all = cache_swa_v.reshape(depth, bs, wb, N_KV * HEAD_DIM)

    f_dense = ffn_w_gate.shape[2]
    tf_dense = f_dense // 2 if (f_dense // 2) % LANES == 0 else f_dense
    f_moe = moe_w_gate.shape[3]
    tf_moe = 512 if f_moe % 512 == 0 else f_moe

    n0_all = state_mlstm_n[:, :, :, None, :]
    m0_all = _m_state_in(state_mlstm_m)
    s_c = None
    new_p = [[] for _ in range(6)]
    new_s = [[] for _ in range(6)]
    for l in range(depth):
        j = l // 2
        z, zif = _ln_proj(x, mod_p, mod_s, w_in_p, b_in_p, w_if, b_if, l, rm)
        zs3 = z[n_p:].reshape(ts, bs, Z_W)

        cp, ns_p = _conv_prompt(z, jnp.zeros((bp, CONV_PAD, D_CONV), F32), cw_pad, *cvecs, l, bp, tp, tc)
        cs3, a_s3 = _conv_sample(zs3, state_conv, cw_pad, *cvecs, l, bs_blk)
        new_p[2].append(ns_p[:, CONV_PAD - CONV_W + 1:])
        new_s[2].append(jnp.transpose(a_s3, (1, 0, 2)))

        sink_h = sinks[l].reshape(N_KV, Q_PER_KV, 1)
        sink_s = jnp.broadcast_to(sink_h, (N_KV, Q_PER_KV, ts)).reshape(N_KV, Q_PER_KV * ts, 1)
        ap = _attn_prompt(z, bias_p, sinks, l, bp, tp)
        nk = min(WINDOW, tp)
        kv_tail = jnp.stack([z[(b + 1) * tp - nk:(b + 1) * tp, Z_K:Z_K + 2 * N_KV * HEAD_DIM]
                             for b in range(bp)]).astype(F32)
        new_p[0].append(kv_tail[..., :N_KV * HEAD_DIM].reshape(bp, nk, N_KV, HEAD_DIM))
        new_p[1].append(kv_tail[..., N_KV * HEAD_DIM:].reshape(bp, nk, N_KV, HEAD_DIM))
        q_s = zs3[:, :, Z_Q:Z_Q + N_HEADS * HEAD_DIM].reshape(ts, bs, N_KV, Q_PER_KV, HEAD_DIM)
        q4 = jnp.transpose(q_s, (1, 2, 3, 0, 4)).reshape(bs, N_KV, Q_PER_KV * ts, HEAD_DIM)
        k_s = jnp.transpose(zs3[:, :, Z_K:Z_K + N_KV * HEAD_DIM].reshape(ts, bs, N_KV, HEAD_DIM), (1, 0, 2, 3))
        v_s = jnp.transpose(zs3[:, :, Z_V:Z_V + N_KV * HEAD_DIM].reshape(ts, bs, N_KV, HEAD_DIM), (1, 0, 2, 3))
        o4 = _attn_sample(q4, kc_all, vc_all, jnp.transpose(k_s, (0, 2, 1, 3)), jnp.transpose(v_s, (0, 2, 1, 3)),
                          bias_s, sink_s, l, bs_att)
        as_ = jnp.transpose(o4.reshape(bs, N_KV, Q_PER_KV, ts, HEAD_DIM), (3, 0, 1, 2, 4)).reshape(n_s, -1).astype(BF16)
        new_s[0].append(k_s.astype(F32))
        new_s[1].append(v_s.astype(F32))

        ncp = tp // lm
        if_p = zif[:n_p, :2 * M_HEADS].reshape(bp * ncp, lm, 2 * M_HEADS)
        mp, c1p, n1p, m1p = _mlstm_prompt(z, zif, jnp.transpose(if_p, (0, 2, 1)), gamma, l, bp, ncp, lm)
        new_p[3].append(c1p)
        new_p[4].append(n1p[:, :, 0])
        new_p[5].append(m1p[:, :M_HEADS, 0])
        tpad = ((0, 0), (0, lts - ts), (0, 0))
        zm3 = jnp.pad(jnp.transpose(zs3[:, :, Z_MQ:Z_K], (1, 0, 2)), tpad)
        if3 = jnp.pad(jnp.transpose(zif[n_p:].reshape(ts, bs, LANES), (1, 0, 2)), tpad)
        ms, s_c, n1s, m1s = _mlstm_sample(zm3, if3, jnp.transpose(if3[:, :, :2 * M_HEADS], (0, 2, 1)), gamma,
                                          state_mlstm_C, n0_all[l], m0_all[l], s_c, l, ts, bs_m)
        ms = jnp.transpose(ms[:, :ts], (1, 0, 2)).reshape(n_s, -1)
        new_s[4].append(n1s[:, :, 0])
        new_s[5].append(m1s[:, :M_HEADS, 0])

        moe = l % 2 == 1
        router = (rw_pad, rb_pad, j) if moe else None
        outs = _merge(cp, cs3.reshape(n_s, D_CONV), ap, as_, mp, ms, z, x, mod_p, mod_s,
                      wc_b, wa_b, wm_b, wo_b, pg, pb, l, rm_half, alpha, router)
        if moe:
            x1, h2, route = outs
            x = _moe_ffn(h2, route, moe_w_gate, moe_w_up, moe_w_down, x1, mod_p, mod_s, pg, pb,
                         l, j, rm, tb, tf_moe, alpha, split=l == depth - 1)
        else:
            x1, h2 = outs
            x = _ffn_dense(h2, fg_b, fu_b, fd_b, x1, mod_p, mod_s, pg, pb, l, j, rm, tf_dense, alpha)

    x_p, x_s = x if isinstance(x, (list, tuple)) else (x[:n_p], x[n_p:])
    y_p = x_p.reshape(bp, tp, d)
    y_s = jnp.transpose(x_s.reshape(ts, bs, d), (1, 0, 2))
    p_k, p_v, p_conv, p_c, p_n, p_m = [jnp.stack(a) for a in new_p]
    s_k, s_v, s_conv = [jnp.concatenate([old[:, :, ts:], jnp.stack(new)], axis=2)
                        for old, new in zip((cache_swa_k, cache_swa_v, state_conv), new_s[:3])]
    s_n, s_m = jnp.stack(new_s[4]), jnp.stack(new_s[5])
    return (y_p, y_s, p_k, p_v, p_conv, p_c, p_n, p_m, s_k, s_v, s_conv, s_c, s_n, s_m)
```

```python
import functools
import math

import jax
import jax.numpy as jnp
from jax import lax
from jax.experimental import pallas as pl
from jax.experimental.pallas import tpu as pltpu

F32 = jnp.float32
BF16 = jnp.bfloat16

D_MODEL = 1024
D_CONV = 512
CONV_W = 31
CONV_PAD = 32
N_HEADS = 8
N_KV = 2
HEAD_DIM = 64
Q_PER_KV = N_HEADS // N_KV
WINDOW = 128
N_BUCKETS = 32
MAX_DIST = 128
M_HEADS = 4
M_DK = 128
M_DV = 128
N_EXPERTS = 8
LN_EPS = 1e-5
LANES = 128
NEG_INF = float("-inf")
VMEM_LIMIT = 56 * 1024 * 1024

Z_G, Z_UA, Z_UB, Z_Q = 0, 3072, 3584, 4096
Z_MQ, Z_MK, Z_MV, Z_MO = 4608, 5120, 5632, 6144
Z_K, Z_V, Z_W = 6656, 6784, 6912
TN_IN = 3456
ATT_QB = 4
ATT_GROUP = 8
MOE_SUB = 256
CONV_CHUNK = 32


def _cparams(*sem):
    return pltpu.CompilerParams(dimension_semantics=sem, vmem_limit_bytes=VMEM_LIMIT)


def _sigmoid(x):
    return 1.0 / (1.0 + jnp.exp(-x))


def _silu(x):
    return x * _sigmoid(x)


def _log_sigmoid(x):
    return jnp.minimum(x, 0.0) - jnp.log(1.0 + jnp.exp(-jnp.abs(x)))


def _norm(x):
    mu = jnp.mean(x, axis=-1, keepdims=True)
    xc = x - mu
    var = jnp.mean(xc * xc, axis=-1, keepdims=True)
    return xc * lax.rsqrt(var + LN_EPS)


def _dot(a, b):
    return jnp.dot(a, b, preferred_element_type=F32)


def _dot_nt(a, b):
    return lax.dot_general(a, b, (((1,), (1,)), ((), ())), preferred_element_type=F32)


def _dot_tn(a, b):
    return lax.dot_general(a, b, (((0,), (0,)), ((), ())), preferred_element_type=F32)


def _dot_hi(a, b):
    return jnp.dot(a, b, preferred_element_type=F32, precision=lax.Precision.HIGHEST)


def _split_bf16(a):
    hi = a.astype(BF16)
    return hi, (a - hi.astype(F32)).astype(BF16)


class _RowMap:
    def __init__(self, t, bp, tp, n_p):
        self.t = t
        self.bp = bp
        self.per_seq = tp // t
        self.np_tiles = n_p // t

    def seq(self, i):
        return jnp.minimum(i // self.per_seq, self.bp - 1)

    def prompt(self, i):
        return jnp.minimum(i, self.np_tiles - 1)

    def sample(self, i):
        return jnp.maximum(i - self.np_tiles, 0)

    def mod_specs(self, l, k, d):
        return (pl.BlockSpec((None, None, 1, d), lambda i, *_: (l, self.seq(i), 0, k)),
                pl.BlockSpec((None, self.t, d), lambda i, *_: (l, self.sample(i), k)))


def _pick(is_s, p_ref, s_ref):
    return jnp.where(is_s, s_ref[...], p_ref[...])


def _ada_kernel(c_ref, w_ref, b_ref, o_ref):
    s = _silu(c_ref[...]).astype(BF16)
    o_ref[0] = _dot(s, w_ref[0].astype(BF16)) + b_ref[0]


def _ada_mod(c_all, w_ada, b_ada):
    depth, d, n6 = w_ada.shape
    rows = c_all.shape[0]
    return pl.pallas_call(
        _ada_kernel,
        grid=(depth, n6 // d),
        in_specs=[pl.BlockSpec((rows, d), lambda l, j: (0, 0)),
                  pl.BlockSpec((1, d, d), lambda l, j: (l, 0, j)),
                  pl.BlockSpec((1, 1, d), lambda l, j: (l, 0, j))],
        out_specs=pl.BlockSpec((1, rows, d), lambda l, j: (l, 0, j)),
        out_shape=jax.ShapeDtypeStruct((depth, rows, n6), F32),
        compiler_params=_cparams("parallel", "parallel"),
        name="ada_mod",
    )(c_all, w_ada, b_ada.reshape(depth, 1, n6))


def _ln_proj_kernel(x_ref, shp_ref, shs_ref, scp_ref, scs_ref, w_ref, b_ref, wif_ref, bif_ref,
                    z_ref, zif_ref, h_scr, *, np_tiles):
    is_s = pl.program_id(0) >= np_tiles

    @pl.when(pl.program_id(1) == 0)
    def _():
        h = _norm(x_ref[...]) * (1.0 + _pick(is_s, scp_ref, scs_ref)) + _pick(is_s, shp_ref, shs_ref)
        h = h.astype(BF16)
        h_scr[...] = h
        zif_ref[...] = _dot(h, wif_ref[...]) + bif_ref[...]

    z_ref[...] = (_dot(h_scr[...], w_ref[...]) + b_ref[...]).astype(BF16)


def _ln_proj(x, mod_p, mod_s, w, b, wif, bif, l, rm):
    n, d = x.shape
    zw = w.shape[2]
    tm = rm.t
    shp, shs = rm.mod_specs(l, 0, d)
    scp, scs = rm.mod_specs(l, 1, d)
    return pl.pallas_call(
        functools.partial(_ln_proj_kernel, np_tiles=rm.np_tiles),
        grid=(n // tm, zw // TN_IN),
        in_specs=[pl.BlockSpec((tm, d), lambda i, j: (i, 0)),
                  shp, shs, scp, scs,
                  pl.BlockSpec((None, d, TN_IN), lambda i, j: (l, 0, j)),
                  pl.BlockSpec((None, 1, TN_IN), lambda i, j: (l, 0, j)),
                  pl.BlockSpec((None, d, LANES), lambda i, j: (l, 0, 0)),
                  pl.BlockSpec((None, 1, LANES), lambda i, j: (l, 0, 0))],
        out_specs=[pl.BlockSpec((tm, TN_IN), lambda i, j: (i, j)),
                   pl.BlockSpec((tm, LANES), lambda i, j: (i, 0))],
        out_shape=[jax.ShapeDtypeStruct((n, zw), BF16), jax.ShapeDtypeStruct((n, LANES), F32)],
        scratch_shapes=[pltpu.VMEM((tm, d), BF16)],
        compiler_params=_cparams("parallel", "arbitrary"),
        name="ln_proj",
    )(x, mod_p, mod_s, mod_p, mod_s, w, b, wif, bif)


def _conv_tail(yc, g_ref, b_ref):
    y = _norm(yc) * g_ref[...] + b_ref[...]
    return _silu(y).astype(BF16)


def _conv_prompt_kernel(ua_ref, ub_ref, st_ref, cw_ref, cb_ref, g_ref, b_ref, o_ref, ns_ref,
                        ext, shifted, yc, wrep):
    t = pl.program_id(1)
    tc = ua_ref.shape[0]
    sub = 8

    @pl.when(t == 0)
    def _():
        ext[0:CONV_PAD, :] = st_ref[0]

    @pl.when(t > 0)
    def _():
        ext[0:CONV_PAD, :] = ext[tc:tc + CONV_PAD, :]

    ext[CONV_PAD:, :] = ua_ref[...].astype(F32) * _sigmoid(ub_ref[...].astype(F32))
    for s in range(1, sub):
        shifted[s - 1] = ext[s:s + tc + CONV_PAD - sub, :]
    off = CONV_PAD - (CONV_W - 1)
    for w in range(CONV_W):
        wrep[w] = jnp.broadcast_to(cw_ref[w:w + 1, :], (sub, D_CONV))
    groups = CONV_CHUNK // sub

    for r0 in range(0, tc, CONV_CHUNK):
        acc = jnp.broadcast_to(cb_ref[...].reshape(1, 1, D_CONV), (groups, sub, D_CONV))
        for w in range(CONV_W):
            base, s = (off + w) // sub * sub, (off + w) % sub
            src = ext if s == 0 else shifted.at[s - 1]
            win = src[r0 + base:r0 + base + CONV_CHUNK, :]
            acc = acc + win.reshape(groups, sub, D_CONV) * wrep[w][None]
        yc[r0:r0 + CONV_CHUNK, :] = acc.reshape(CONV_CHUNK, D_CONV)
    o_ref[...] = _conv_tail(yc[...], g_ref, b_ref)

    @pl.when(t == pl.num_programs(1) - 1)
    def _():
        ns_ref[0] = ext[tc:tc + CONV_PAD, :]


def _conv_vec_specs(l, nargs):
    return [pl.BlockSpec((None, 1, D_CONV), lambda *_: (l, 0, 0)) for _ in range(nargs)]


def _conv_prompt(z, state_pad, cw, cb, g, b, l, bp, tp, tc):
    nt = tp // tc
    return pl.pallas_call(
        _conv_prompt_kernel,
        grid=(bp, nt),
        in_specs=[pl.BlockSpec((tc, D_CONV), lambda bb, t: (bb * nt + t, Z_UA // D_CONV)),
                  pl.BlockSpec((tc, D_CONV), lambda bb, t: (bb * nt + t, Z_UB // D_CONV)),
                  pl.BlockSpec((1, CONV_PAD, D_CONV), lambda bb, t: (bb, 0, 0)),
                  pl.BlockSpec((None, CONV_PAD, D_CONV), lambda bb, t: (l, 0, 0))] + _conv_vec_specs(l, 3),
        out_specs=[pl.BlockSpec((tc, D_CONV), lambda bb, t: (bb * nt + t, 0)),
                   pl.BlockSpec((1, CONV_PAD, D_CONV), lambda bb, t: (bb, 0, 0))],
        out_shape=[jax.ShapeDtypeStruct((bp * tp, D_CONV), BF16),
                   jax.ShapeDtypeStruct((bp, CONV_PAD, D_CONV), F32)],
        scratch_shapes=[pltpu.VMEM((tc + CONV_PAD, D_CONV), F32),
                        pltpu.VMEM((7, tc + CONV_PAD - 8, D_CONV), F32),
                        pltpu.VMEM((tc, D_CONV), F32),
                        pltpu.VMEM((CONV_PAD, 8, D_CONV), F32)],
        compiler_params=_cparams("parallel", "arbitrary"),
        name="conv_prompt",
    )(z, z, state_pad, cw, cb, g, b)


def _conv_sample_kernel(ua_ref, ub_ref, st_ref, cw_ref, cb_ref, g_ref, b_ref, o_ref, a_ref):
    ts = ua_ref.shape[0]
    ns = CONV_W - 1
    a = ua_ref[...].astype(F32) * _sigmoid(ub_ref[...].astype(F32))
    a_ref[...] = a
    st = st_ref[...]
    row = lax.broadcasted_iota(jnp.int32, (ns, D_CONV), 0)
    for t in range(ts):
        wt = jnp.zeros((ns, D_CONV), F32)
        for j in range(t, ns):
            wt = jnp.where(row == j, cw_ref[j - t:j - t + 1, :], wt)
        yc = jnp.sum(st * wt[None], axis=1) + cb_ref[...]
        for t2 in range(t + 1):
            wi = CONV_W - 1 - (t - t2)
            yc = yc + a[t2] * cw_ref[wi:wi + 1, :]
        o_ref[t] = _conv_tail(yc, g_ref, b_ref)


def _conv_sample(zs3, state, cw, cb, g, b, l, bs_blk):
    ts, bs, _ = zs3.shape
    ns = CONV_W - 1
    return pl.pallas_call(
        _conv_sample_kernel,
        grid=(bs // bs_blk,),
        in_specs=[pl.BlockSpec((ts, bs_blk, D_CONV), lambda i: (0, i, Z_UA // D_CONV)),
                  pl.BlockSpec((ts, bs_blk, D_CONV), lambda i: (0, i, Z_UB // D_CONV)),
                  pl.BlockSpec((None, bs_blk, ns, D_CONV), lambda i: (l, i, 0, 0)),
                  pl.BlockSpec((None, CONV_PAD, D_CONV), lambda i: (l, 0, 0))] + _conv_vec_specs(l, 3),
        out_specs=[pl.BlockSpec((ts, bs_blk, D_CONV), lambda i: (0, i, 0)),
                   pl.BlockSpec((ts, bs_blk, D_CONV), lambda i: (0, i, 0))],
        out_shape=[jax.ShapeDtypeStruct((ts, bs, D_CONV), BF16),
                   jax.ShapeDtypeStruct((ts, bs, D_CONV), F32)],
        compiler_params=_cparams("parallel"),
        name="conv_sample",
    )(zs3, zs3, state, cw, cb, g, b)


def _t5_bucket(dist):
    max_exact = N_BUCKETS // 2
    d = jnp.maximum(dist, 0)
    large = max_exact + (jnp.log(jnp.maximum(d, 1).astype(F32) / max_exact)
                         / math.log(MAX_DIST / max_exact) * (N_BUCKETS - max_exact)).astype(jnp.int32)
    return jnp.where(d < max_exact, d, jnp.minimum(large, N_BUCKETS - 1))


def _bias_heads(rel_bias, dist, valid):
    onehot = (_t5_bucket(dist)[..., None] == jnp.arange(N_BUCKETS)).astype(F32)
    bias = jnp.einsum("qkb,bh->qkh", onehot, rel_bias.astype(F32), precision=lax.Precision.HIGHEST)
    bias = jnp.where(valid[..., None], bias, NEG_INF)
    return jnp.transpose(bias, (2, 0, 1))


def _attn_prompt_kernel(sink_ref, q_ref, kc_ref, kp_ref, vc_ref, vp_ref, bias_ref, o_ref, *, l):
    first = pl.program_id(1) == 0
    w = WINDOW
    nq = q_ref.shape[0] // w
    kall = jnp.concatenate([kp_ref[...], kc_ref[...]], axis=0).astype(F32)
    vall = jnp.concatenate([vp_ref[...], vc_ref[...]], axis=0).astype(F32)
    lane = lax.broadcasted_iota(jnp.int32, kall.shape, 1)
    lo = lane < HEAD_DIM
    kroll = pltpu.roll(kall, HEAD_DIM, 1)
    vroll = pltpu.roll(vall, HEAD_DIM, 1)

    def halves(a, aroll, g):
        if g == 0:
            return jnp.where(lo, a, 0.0).astype(BF16), jnp.where(lo, 0.0, aroll).astype(BF16)
        return jnp.where(lo, aroll, 0.0).astype(BF16), jnp.where(lo, 0.0, a).astype(BF16)

    kh = [halves(kall, kroll, g) for g in range(N_KV)]
    vh = [halves(vall, vroll, g) for g in range(N_KV)]
    col = lax.broadcasted_iota(jnp.int32, (w, 4 * w), 1)
    prev_col = (col % (2 * w)) < w
    tiles_per_g = Q_PER_KV // 2
    units = [(qi, tile) for qi in range(nq) for tile in range(N_HEADS // 2)]
    for u0 in range(0, len(units), ATT_GROUP):
        group = units[u0:u0 + ATT_GROUP]
        scores = []
        for qi, tile in group:
            r0, g = qi * w, tile // tiles_per_g
            q = q_ref[r0:r0 + w, tile * LANES:(tile + 1) * LANES]
            kk = jnp.concatenate([kh[g][0][r0:r0 + 2 * w], kh[g][1][r0:r0 + 2 * w]], axis=0)
            s = _dot_nt(q, kk) * (HEAD_DIM ** -0.5) + bias_ref[tile]
            if qi == 0:
                s = jnp.where(jnp.logical_and(first, prev_col), NEG_INF, s)
            scores.append(s)
        probs = []
        for (qi, tile), s in zip(group, scores):
            ps = []
            for half in range(2):
                sh = s[:, half * 2 * w:(half + 1) * 2 * w]
                sink = sink_ref[l, 2 * tile + half]
                mx = jnp.maximum(jnp.max(sh, axis=-1, keepdims=True), sink)
                p = jnp.exp(sh - mx)
                den = jnp.sum(p, axis=-1, keepdims=True) + jnp.exp(sink - mx)
                ps.append((p * (1.0 / den)).astype(BF16))
            probs.append(jnp.concatenate(ps, axis=1))
        for (qi, tile), p in zip(group, probs):
            r0, g = qi * w, tile // tiles_per_g
            vv = jnp.concatenate([vh[g][0][r0:r0 + 2 * w], vh[g][1][r0:r0 + 2 * w]], axis=0)
            o_ref[r0:r0 + w, tile * LANES:(tile + 1) * LANES] = _dot(p, vv).astype(BF16)


def _attn_prompt(z, bias, sinks, l, bp, tp):
    w = WINDOW
    qb = ATT_QB if tp % (ATT_QB * w) == 0 else 1
    ns = tp // (qb * w)
    nb = tp // w
    kvw = N_KV * HEAD_DIM
    qw = N_HEADS * HEAD_DIM

    def cur(col):
        return lambda bb, i: (bb * ns + i, col)

    def prev(col):
        return lambda bb, i: (bb * nb + jnp.maximum(i * qb - 1, 0), col)

    return pl.pallas_call(
        functools.partial(_attn_prompt_kernel, l=l),
        grid=(bp, ns),
        in_specs=[pl.BlockSpec(memory_space=pltpu.SMEM),
                  pl.BlockSpec((qb * w, qw), cur(Z_Q // qw)),
                  pl.BlockSpec((qb * w, kvw), cur(Z_K // kvw)),
                  pl.BlockSpec((w, kvw), prev(Z_K // kvw)),
                  pl.BlockSpec((qb * w, kvw), cur(Z_V // kvw)),
                  pl.BlockSpec((w, kvw), prev(Z_V // kvw)),
                  pl.BlockSpec((N_HEADS // 2, w, 4 * w), lambda bb, i: (0, 0, 0))],
        out_specs=pl.BlockSpec((qb * w, qw), lambda bb, i: (bb * ns + i, 0)),
        out_shape=jax.ShapeDtypeStruct((bp * tp, qw), BF16),
        compiler_params=_cparams("parallel", "parallel"),
        name="attn_prompt",
    )(sinks, z, z, z, z, z, bias)


def _attn_sample_kernel(q_ref, kc_ref, vc_ref, kn_ref, vn_ref, bias_ref, sink_ref, o_ref):
    wb = kc_ref.shape[1]
    ts = kn_ref.shape[2]
    for g in range(N_KV):
        lo = g * HEAD_DIM
        qb = (q_ref[:, g].astype(F32) * (HEAD_DIM ** -0.5)).astype(BF16)
        kc = kc_ref[:, :, lo:lo + HEAD_DIM].astype(BF16)
        vc = vc_ref[:, :, lo:lo + HEAD_DIM].astype(BF16)
        kn = kn_ref[:, g].astype(F32)
        vn = vn_ref[:, g].astype(F32)
        bias = bias_ref[g]
        s_c = jnp.einsum("bqd,bkd->bqk", qb, kc, preferred_element_type=F32) + bias[None, :, :wb]
        qf = qb.astype(F32)
        s_n = [jnp.sum(qf * kn[:, j:j + 1, :], axis=-1, keepdims=True) + bias[None, :, wb + j:wb + j + 1]
               for j in range(ts)]
        sink = sink_ref[g][None]
        mx = jnp.maximum(jnp.max(s_c, axis=-1, keepdims=True), sink)
        for sj in s_n:
            mx = jnp.maximum(mx, sj)
        p_c = jnp.exp(s_c - mx)
        p_n = [jnp.exp(sj - mx) for sj in s_n]
        den = jnp.sum(p_c, axis=-1, keepdims=True) + jnp.exp(sink - mx)
        for pj in p_n:
            den = den + pj
        o = jnp.einsum("bqk,bkd->bqd", (p_c / den).astype(BF16), vc, preferred_element_type=F32)
        for j in range(ts):
            o = o + (p_n[j] / den).astype(BF16).astype(F32) * vn[:, j:j + 1, :]
        o_ref[:, g] = o


def _attn_sample(q4, kc, vc, kn, vn, bias, sinks, l, bs_blk):
    bs, _, rt, _ = q4.shape
    wb = kc.shape[2]
    ts = kn.shape[2]
    kvw = N_KV * HEAD_DIM
    return pl.pallas_call(
        _attn_sample_kernel,
        grid=(bs // bs_blk,),
        in_specs=[pl.BlockSpec((bs_blk, N_KV, rt, HEAD_DIM), lambda i: (i, 0, 0, 0)),
                  pl.BlockSpec((None, bs_blk, wb, kvw), lambda i: (l, i, 0, 0)),
                  pl.BlockSpec((None, bs_blk, wb, kvw), lambda i: (l, i, 0, 0)),
                  pl.BlockSpec((bs_blk, N_KV, ts, HEAD_DIM), lambda i: (i, 0, 0, 0)),
                  pl.BlockSpec((bs_blk, N_KV, ts, HEAD_DIM), lambda i: (i, 0, 0, 0)),
                  pl.BlockSpec((N_KV, rt, wb + ts), lambda i: (0, 0, 0)),
                  pl.BlockSpec((N_KV, rt, 1), lambda i: (0, 0, 0))],
        out_specs=pl.BlockSpec((bs_blk, N_KV, rt, HEAD_DIM), lambda i: (i, 0, 0, 0)),
        out_shape=jax.ShapeDtypeStruct((bs, N_KV, rt, HEAD_DIM), F32),
        compiler_params=_cparams("parallel"),
        name="attn_sample",
    )(q4, kc, vc, kn, vn, bias, sinks)


def _mlstm_kernel(*refs, t_valid, nseq, aliased, carried, chunk_axis, per_seq_inputs=False, first_layer=None):
    if aliased:
        refs = refs[:10] + refs[11:]
    if per_seq_inputs:
        groups = [refs[6 * s:6 * s + 6] for s in range(nseq)]
        if_ref, ifr_ref, q_ref, k_ref, v_ref, o_ref = (tuple(g[k] for g in groups) for k in range(6))
        refs = (None,) * 6 + refs[6 * nseq:]
    else:
        if_ref, ifr_ref, q_ref, k_ref, v_ref, o_ref = refs[:6]
    g_ref, c0_ref, n0_ref, m0_ref, h_ref, c1_ref, n1_ref, m1_ref = refs[6:14]
    c = pl.program_id(chunk_axis)
    L = (ifr_ref[0] if per_seq_inputs else ifr_ref).shape[-1]
    if carried:
        c_in, n_in, m_in = c_out, n_out, m_out = refs[14:]

        @pl.when(c == 0)
        def _():
            c_in[...] = c0_ref[...]
            n_in[...] = n0_ref[...]
            m_in[...] = m0_ref[...]
    else:
        (c_in, n_in, m_in), (c_out, n_out, m_out) = (c0_ref, n0_ref, m0_ref), (c1_ref, n1_ref, m1_ref)
        if first_layer is not None:
            for dd in range(c1_ref.shape[0]):
                if dd != first_layer:
                    c1_ref[dd] = jnp.zeros(c1_ref.shape[1:], F32)
            c_out = c1_ref.at[first_layer]

    def seq(ref, s):
        return ref[s] if isinstance(ref, tuple) else ref.at[s]

    tt = lax.broadcasted_iota(jnp.int32, (L, L), 0)
    ss = lax.broadcasted_iota(jnp.int32, (L, L), 1)
    causal = ss <= tt
    tril = causal.astype(F32)
    triu = (tt <= ss).astype(F32)
    seqs = range(nseq)
    heads = [(s_i, h) for s_i in seqs for h in range(M_HEADS)]
    mxu_sums = L % LANES == 0
    gates = []
    for s_i in seqs:
        ifc = seq(if_ref, s_i)[...]
        ifr = ifr_ref[s_i][0] if per_seq_inputs else ifr_ref[s_i]
        lf_c = _log_sigmoid(ifc)
        lf_r = _log_sigmoid(ifr)
        i_c, i_r = ifc, ifr
        if t_valid < L:
            rc = lax.broadcasted_iota(jnp.int32, (L, LANES), 0) < t_valid
            rr = lax.broadcasted_iota(jnp.int32, (2 * M_HEADS, L), 1) < t_valid
            lf_c = jnp.where(rc, lf_c, 0.0)
            lf_r = jnp.where(rr, lf_r, 0.0)
            i_c = jnp.where(rc, i_c, NEG_INF)
            i_r = jnp.where(rr, i_r, NEG_INF)
        gates.append((lf_c, lf_r, i_c, i_r))
    f_cs = [_dot_hi(tril, g[0]) for g in gates]
    f_rs = [_dot_hi(g[1], triu) for g in gates]
    state = {(s_i, h): (c_in[s_i, h], n_in[s_i, h], m_in[s_i, h:h + 1, :]) for s_i, h in heads}

    def wide(col):
        return jnp.concatenate([col] * (L // LANES), axis=1) if L >= LANES else col[:, :L]

    st1 = {}
    for s_i, h in heads:
        _, nrow, m0 = state[s_i, h]
        fc = jnp.broadcast_to(f_cs[s_i][:, M_HEADS + h:M_HEADS + h + 1], (L, LANES))
        fr = f_rs[s_i][M_HEADS + h:M_HEADS + h + 1, :]
        ir = gates[s_i][3][h:h + 1, :]
        dm = jnp.where(causal, wide(fc) - fr + ir, NEG_INF)
        m_t = jnp.maximum(m0 + fc, jnp.max(dm, axis=-1, keepdims=True))
        st1[s_i, h] = (fc, dm, m_t, jnp.exp(m0 + fc - m_t))
    st2 = {}
    for s_i, h in heads:
        lo = h * M_DK
        fc, dm, m_t, inter = st1[s_i, h]
        qb = seq(q_ref, s_i)[:, lo:lo + M_DK]
        kf = seq(k_ref, s_i)[:, lo:lo + M_DK].astype(F32) * (M_DK ** -0.5)
        vf = seq(v_ref, s_i)[:, lo:lo + M_DV].astype(F32)
        kb, vb = kf.astype(BF16), vf.astype(BF16)
        if mxu_sums:
            n_rows = jnp.broadcast_to(state[s_i, h][1], (M_DK, M_DK)).astype(BF16)
            qk = _dot_nt(qb, jnp.concatenate([kb, n_rows], axis=0))
            sc, qn_rep = qk[:, :L] * jnp.exp(dm - wide(m_t)), qk[:, L:]
        else:
            sc, qn_rep = _dot_nt(qb, kb) * jnp.exp(dm - wide(m_t)), None
        st2[s_i, h] = (qb, kf, vf, kb, vb, sc, qn_rep)
    st3 = {}
    for s_i, h in heads:
        cm, nrow, m0 = state[s_i, h]
        fc, dm, m_t, inter = st1[s_i, h]
        qb, kf, vf, kb, vb, sc, qn_rep = st2[s_i, h]
        if mxu_sums:
            sv = _dot(sc.astype(BF16), jnp.concatenate([vb, jnp.ones((L, M_DV), BF16)], axis=1))
            num = inter * _dot(qb, cm.astype(BF16)) + sv[:, :M_DV]
            qn = inter * qn_rep + sv[:, M_DV:]
            floor = jnp.exp(-m_t)
        else:
            num = inter * _dot(qb, cm.astype(BF16)) + _dot(sc.astype(BF16), vb)
            qn = (inter * jnp.sum(qb.astype(F32) * nrow, axis=-1, keepdims=True)
                  + jnp.sum(sc, axis=-1, keepdims=True))
            floor = jnp.exp(-m_t)
        st3[s_i, h] = num / jnp.maximum(jnp.abs(qn), floor)
    new_state = {}
    for s_i, h in heads:
        cm, nrow, m0 = state[s_i, h]
        fc, dm, m_t, inter = st1[s_i, h]
        qb, kf, vf, kb, vb, sc, _ = st2[s_i, h]
        ic = jnp.broadcast_to(gates[s_i][2][:, h:h + 1], (L, LANES))
        m_end = m_t[L - 1:L, :]
        f_end = fc[L - 1:L, :]
        decay = jnp.exp(m0 + f_end - m_end)
        w_s = jnp.exp(f_end - fc + ic - m_end)
        new_state[s_i, h] = (decay * cm + _dot_tn(kb, (w_s * vf).astype(BF16)),
                             decay * nrow + jnp.sum(w_s * kf, axis=0, keepdims=True),
                             m_end)
    for s_i, h in heads:
        lo = h * M_DK
        hn = _norm(st3[s_i, h]) * g_ref[:, lo:lo + M_DV]
        gate = _sigmoid(seq(o_ref, s_i)[:, lo:lo + M_DV].astype(F32))
        seq(h_ref, s_i)[:, lo:lo + M_DV] = (gate * hn).astype(BF16)
    for s_i, h in heads:
        c_out[s_i, h], n_out[s_i, h], m_out[s_i, h:h + 1, :] = new_state[s_i, h]
    if not carried:
        for s_i in seqs:
            m_out[s_i, M_HEADS:, :] = jnp.zeros((M_HEADS, LANES), F32)

    if carried:
        @pl.when(c == pl.num_programs(chunk_axis) - 1)
        def _():
            c1_ref[...] = c_out[...]
            n1_ref[...] = n_out[...]
            m1_ref[...] = m_out[...]


def _mlstm_state_specs(nseq, l_state):
    if l_state is None:
        c_spec = pl.BlockSpec((nseq, M_HEADS, M_DK, M_DV), lambda b, c: (b, 0, 0, 0))
    else:
        c_spec = pl.BlockSpec((None, nseq, M_HEADS, M_DK, M_DV), lambda b, c: (l_state, b, 0, 0, 0))
    return (c_spec,
            pl.BlockSpec((nseq, M_HEADS, 1, M_DK), lambda b, c: (b, 0, 0, 0)),
            pl.BlockSpec((nseq, 2 * M_HEADS, LANES), lambda b, c: (b, 0, 0)))


def _mlstm_scratch(nseq):
    return [pltpu.VMEM((nseq, M_HEADS, M_DK, M_DV), F32),
            pltpu.VMEM((nseq, M_HEADS, 1, M_DK), F32),
            pltpu.VMEM((nseq, 2 * M_HEADS, LANES), F32)]


def _mlstm_prompt(z, zif, ifr, gamma, l, bp, nc, L):
    hw = M_HEADS * M_DK
    zero = lambda *s: jnp.zeros(s, F32)
    in_specs, args = [], []
    for b in range(bp):
        def rows(col, b=b):
            return lambda c: (b * nc + c, col)

        in_specs += [pl.BlockSpec((L, LANES), rows(0)),
                     pl.BlockSpec((1, 2 * M_HEADS, L), lambda c, b=b: (b * nc + c, 0, 0)),
                     pl.BlockSpec((L, hw), rows(Z_MQ // hw)),
                     pl.BlockSpec((L, hw), rows(Z_MK // hw)),
                     pl.BlockSpec((L, hw), rows(Z_MV // hw)),
                     pl.BlockSpec((L, hw), rows(Z_MO // hw))]
        args += [zif, ifr, z, z, z, z]
    state_specs = [pl.BlockSpec((bp, M_HEADS, M_DK, M_DV), lambda c: (0, 0, 0, 0)),
                   pl.BlockSpec((bp, M_HEADS, 1, M_DK), lambda c: (0, 0, 0, 0)),
                   pl.BlockSpec((bp, 2 * M_HEADS, LANES), lambda c: (0, 0, 0))]
    kern = functools.partial(_mlstm_kernel, t_valid=L, nseq=bp, aliased=False, carried=True, chunk_axis=0,
                             per_seq_inputs=True)
    outs = pl.pallas_call(
        kern,
        grid=(nc,),
        in_specs=in_specs + [pl.BlockSpec((None, 1, hw), lambda c: (l, 0, 0))] + state_specs,
        out_specs=[pl.BlockSpec((bp, L, hw), lambda c: (0, c, 0))] + state_specs,
        out_shape=[jax.ShapeDtypeStruct((bp, nc * L, hw), BF16),
                   jax.ShapeDtypeStruct((bp, M_HEADS, M_DK, M_DV), F32),
                   jax.ShapeDtypeStruct((bp, M_HEADS, 1, M_DK), F32),
                   jax.ShapeDtypeStruct((bp, 2 * M_HEADS, LANES), F32)],
        scratch_shapes=_mlstm_scratch(bp),
        compiler_params=_cparams("arbitrary"),
        name="mlstm_prompt",
    )(*args, gamma, zero(bp, M_HEADS, M_DK, M_DV), zero(bp, M_HEADS, 1, M_DK), zero(bp, 2 * M_HEADS, LANES))
    return (outs[0].reshape(bp * nc * L, hw),) + tuple(outs[1:])


def _mlstm_sample(zm3, if3, ifr, gamma, c_all, n0, m0x, c_out_prev, l, t_valid, nseq):
    bs, L, _ = zm3.shape
    depth = c_all.shape[0]
    hw = M_HEADS * M_DK
    aliased = c_out_prev is not None

    def blk(col):
        return pl.BlockSpec((nseq, L, hw), lambda b, c: (b, 0, col))

    c_in, n_spec, m_spec = _mlstm_state_specs(nseq, l)
    in_specs = [pl.BlockSpec((nseq, L, LANES), lambda b, c: (b, 0, 0)),
                pl.BlockSpec((nseq, 2 * M_HEADS, L), lambda b, c: (b, 0, 0)),
                blk(0), blk(1), blk(2), blk(3),
                pl.BlockSpec((None, 1, hw), lambda b, c: (l, 0, 0)),
                c_in, n_spec, m_spec]
    args = [if3, ifr, zm3, zm3, zm3, zm3, gamma, c_all, n0, m0x]
    aliases = {}
    if aliased:
        in_specs.append(pl.BlockSpec(memory_space=pl.ANY))
        args.append(c_out_prev)
        aliases = {len(args) - 1: 1}
        c_out = c_in
    else:
        c_out = pl.BlockSpec((depth, nseq, M_HEADS, M_DK, M_DV), lambda b, c: (0, b, 0, 0, 0))
    kern = functools.partial(_mlstm_kernel, t_valid=t_valid, nseq=nseq, aliased=aliased, carried=False,
                             chunk_axis=1, first_layer=None if aliased else l)
    return pl.pallas_call(
        kern,
        grid=(bs // nseq, 1),
        in_specs=in_specs,
        out_specs=[pl.BlockSpec((nseq, L, hw), lambda b, c: (b, 0, 0)), c_out, n_spec, m_spec],
        out_shape=[jax.ShapeDtypeStruct((bs, L, hw), BF16),
                   jax.ShapeDtypeStruct((depth, bs, M_HEADS, M_DK, M_DV), F32),
                   jax.ShapeDtypeStruct((bs, M_HEADS, 1, M_DK), F32),
                   jax.ShapeDtypeStruct((bs, 2 * M_HEADS, LANES), F32)],
        input_output_aliases=aliases,
        compiler_params=_cparams("parallel", "arbitrary"),
        name="mlstm_sample",
    )(*args)


def _m_state_in(m):
    lead = m.shape[:-1]
    mx = jnp.zeros(lead + (2 * M_HEADS, LANES), F32)
    return mx.at[..., :M_HEADS, :].set(jnp.broadcast_to(m[..., None], lead + (M_HEADS, LANES)))


def _merge_kernel(*refs, np_tiles, alpha, route):
    (cp_ref, cs_ref, ap_ref, as_ref, mp_ref, ms_ref, g0_ref, g1_ref, g2_ref, x_ref,
     gtp_ref, gts_ref, shp_ref, shs_ref, scp_ref, scs_ref,
     wc_ref, wa_ref, wm_ref, wo_ref, pg_ref, pb_ref) = refs[:22]
    if route:
        rw_ref, rb_ref, x1_ref, h_ref, route_ref = refs[22:]
    else:
        x1_ref, h_ref = refs[22:]
    is_s = pl.program_id(0) >= np_tiles

    def gate(ref):
        return _sigmoid(ref[...].astype(F32))

    def run(c_ref, a_ref, m_ref, gt_ref, sh_ref, sc_ref):
        y = (gate(g0_ref) * _dot(c_ref[...], wc_ref[...])
             + gate(g1_ref) * _dot(a_ref[...], wa_ref[...])
             + gate(g2_ref) * _dot(m_ref[...], wm_ref[...]))
        mix = _dot(y.astype(BF16), wo_ref[...])
        x1 = _norm(alpha * x_ref[...] + gt_ref[...] * mix) * pg_ref[...] + pb_ref[...]
        x1_ref[...] = x1
        h = _norm(x1) * (1.0 + sc_ref[...]) + sh_ref[...]
        h_ref[...] = h.astype(h_ref.dtype)
        if route:
            h_hi, h_lo = _split_bf16(h)
            w_hi, w_lo = _split_bf16(rw_ref[...])
            logits = _dot(h_hi, w_hi) + (_dot(h_hi, w_lo) + _dot(h_lo, w_hi)) + rb_ref[...]
            lane = lax.broadcasted_iota(jnp.int32, logits.shape, 1)
            m1 = jnp.max(logits, axis=-1, keepdims=True)
            e1 = jnp.min(jnp.where(logits == m1, lane, LANES), axis=-1, keepdims=True)
            l2 = jnp.where(lane == e1, NEG_INF, logits)
            m2 = jnp.max(l2, axis=-1, keepdims=True)
            e2 = jnp.min(jnp.where(l2 == m2, lane, LANES), axis=-1, keepdims=True)
            ex = jnp.exp(m2 - m1)
            w1 = 1.0 / (1.0 + ex)
            w2 = ex / (1.0 + ex)
            out = jnp.where(lane == 0, e1.astype(F32),
                            jnp.where(lane == 1, e2.astype(F32),
                                      jnp.where(lane == 2, w1, jnp.where(lane == 3, w2, 0.0))))
            route_ref[...] = out

    @pl.when(is_s)
    def _():
        run(cs_ref, as_ref, ms_ref, gts_ref, shs_ref, scs_ref)

    @pl.when(jnp.logical_not(is_s))
    def _():
        run(cp_ref, ap_ref, mp_ref, gtp_ref, shp_ref, scp_ref)


def _merge(cp, cs, ap, as_, mp, ms, z, x, mod_p, mod_s, wc, wa, wm, wo, pg, pb, l, rm, alpha, router):
    n, d = x.shape
    hw = D_CONV
    route = router is not None
    tm = rm.t

    def pblk():
        return pl.BlockSpec((tm, hw), lambda i: (rm.prompt(i), 0))

    def sblk():
        return pl.BlockSpec((tm, hw), lambda i: (rm.sample(i), 0))

    def zg(k):
        return pl.BlockSpec((tm, d), lambda i: (i, Z_G // d + k))

    def lw(a):
        return pl.BlockSpec((None,) + a.shape[1:], lambda i: (l,) + (0,) * (a.ndim - 1))

    post = pl.BlockSpec((None, None, 1, d), lambda i: (l, 0, 0, 0))
    in_specs = [pblk(), sblk(), pblk(), sblk(), pblk(), sblk(), zg(0), zg(1), zg(2),
                pl.BlockSpec((tm, d), lambda i: (i, 0)),
                *rm.mod_specs(l, 2, d), *rm.mod_specs(l, 3, d), *rm.mod_specs(l, 4, d),
                lw(wc), lw(wa), lw(wm), lw(wo), post, post]
    args = [cp, cs, ap, as_, mp, ms, z, z, z, x, mod_p, mod_s, mod_p, mod_s, mod_p, mod_s,
            wc, wa, wm, wo, pg, pb]
    out_specs = [pl.BlockSpec((tm, d), lambda i: (i, 0)), pl.BlockSpec((tm, d), lambda i: (i, 0))]
    out_shape = [jax.ShapeDtypeStruct((n, d), F32), jax.ShapeDtypeStruct((n, d), F32 if route else BF16)]
    if route:
        rw, rb, lj = router
        in_specs += [pl.BlockSpec((None, d, LANES), lambda i: (lj, 0, 0)),
                     pl.BlockSpec((None, 1, LANES), lambda i: (lj, 0, 0))]
        args += [rw, rb]
        out_specs.append(pl.BlockSpec((tm, LANES), lambda i: (i, 0)))
        out_shape.append(jax.ShapeDtypeStruct((n, LANES), F32))
    kern = functools.partial(_merge_kernel, np_tiles=rm.np_tiles, alpha=alpha, route=route)
    return pl.pallas_call(
        kern, grid=(n // tm,), in_specs=in_specs, out_specs=out_specs, out_shape=out_shape,
        compiler_params=_cparams("parallel"), name="merge_route" if route else "merge",
    )(*args)


def _post_residual(x_ref, gp_ref, gs_ref, f, pg_ref, pb_ref, is_s, alpha):
    return _norm(alpha * x_ref[...] + _pick(is_s, gp_ref, gs_ref) * f) * pg_ref[...] + pb_ref[...]


def _ffn_kernel(h_ref, wg_ref, wu_ref, wd_ref, x_ref, gp_ref, gs_ref, pg_ref, pb_ref, o_ref, acc,
                *, alpha, np_tiles):
    j = pl.program_id(1)
    h = h_ref[...]
    a = (_silu(_dot(h, wg_ref[...])) * _dot(h, wu_ref[...])).astype(BF16)
    part = _dot(a, wd_ref[...])

    @pl.when(j == 0)
    def _():
        acc[...] = part

    @pl.when(j > 0)
    def _():
        acc[...] = acc[...] + part

    @pl.when(j == pl.num_programs(1) - 1)
    def _():
        is_s = pl.program_id(0) >= np_tiles
        o_ref[...] = _post_residual(x_ref, gp_ref, gs_ref, acc[...], pg_ref, pb_ref, is_s, alpha)


def _ffn_dense(h, wg, wu, wd, x, mod_p, mod_s, pg, pb, l, lj, rm, tf, alpha):
    n, d = x.shape
    f = wg.shape[2]
    tm = rm.t
    post = pl.BlockSpec((None, None, 1, d), lambda i, j: (l, 1, 0, 0))
    return pl.pallas_call(
        functools.partial(_ffn_kernel, alpha=alpha, np_tiles=rm.np_tiles),
        grid=(n // tm, f // tf),
        in_specs=[pl.BlockSpec((tm, d), lambda i, j: (i, 0)),
                  pl.BlockSpec((None, d, tf), lambda i, j: (lj, 0, j)),
                  pl.BlockSpec((None, d, tf), lambda i, j: (lj, 0, j)),
                  pl.BlockSpec((None, tf, d), lambda i, j: (lj, j, 0)),
                  pl.BlockSpec((tm, d), lambda i, j: (i, 0)),
                  *rm.mod_specs(l, 5, d), post, post],
        out_specs=pl.BlockSpec((tm, d), lambda i, j: (i, 0)),
        out_shape=jax.ShapeDtypeStruct((n, d), F32),
        scratch_shapes=[pltpu.VMEM((tm, d), F32)],
        compiler_params=_cparams("parallel", "arbitrary"),
        name="ffn_dense",
    )(h, wg, wu, wd, x, mod_p, mod_s, pg, pb)


def _rank_kernel(route_ref, rank_ref, tot_ref, carry):
    i = pl.program_id(0)
    tm = route_ref.shape[0]

    @pl.when(i == 0)
    def _():
        carry[...] = jnp.zeros_like(carry)

    r = route_ref[...]
    lane = lax.broadcasted_iota(jnp.int32, (tm, LANES), 1)
    e1 = r[:, 0:1].astype(jnp.int32)
    e2 = r[:, 1:2].astype(jnp.int32)
    hit1 = lane == e1
    hit2 = lane == e2
    onehot = jnp.where(jnp.logical_or(hit1, hit2), 1.0, 0.0)
    tt = lax.broadcasted_iota(jnp.int32, (tm, tm), 0)
    ss = lax.broadcasted_iota(jnp.int32, (tm, tm), 1)
    before = jnp.where(ss < tt, 1.0, 0.0).astype(BF16)
    cnt = _dot(before, onehot.astype(BF16)) + carry[0:1, :]
    r1 = jnp.sum(jnp.where(hit1, cnt, 0.0), axis=-1, keepdims=True)
    r2 = jnp.sum(jnp.where(hit2, cnt, 0.0), axis=-1, keepdims=True)
    rank_ref[...] = jnp.where(lane == 0, r1, jnp.where(lane == 1, r2, 0.0))
    carry[...] = carry[...] + jnp.sum(onehot, axis=0, keepdims=True)
    tot_ref[...] = carry[...]


def _moe_rank(route, tm):
    n = route.shape[0]
    return pl.pallas_call(
        _rank_kernel,
        grid=(n // tm,),
        in_specs=[pl.BlockSpec((tm, LANES), lambda i: (i, 0))],
        out_specs=[pl.BlockSpec((tm, LANES), lambda i: (i, 0)),
                   pl.BlockSpec((8, LANES), lambda i: (0, 0))],
        out_shape=[jax.ShapeDtypeStruct((n, LANES), F32), jax.ShapeDtypeStruct((8, LANES), F32)],
        scratch_shapes=[pltpu.VMEM((8, LANES), F32)],
        compiler_params=_cparams("arbitrary"),
        name="moe_rank",
    )(route)


def _row_copy(src, s, dst, t, sem):
    return pltpu.make_async_copy(src.at[pl.ds(s, 1)], dst.at[pl.ds(t, 1)], sem)


def _dispatch_kernel(d1_ref, d2_ref, zl_ref, h_ref, xs_hbm, zbuf, sem, zsem):
    i = pl.program_id(0)
    tm = h_ref.shape[0]

    @pl.when(i == 0)
    def _():
        zbuf[...] = jnp.zeros_like(zbuf)

        def zero_copy(k):
            row = pl.multiple_of(jnp.maximum(zl_ref[k], 0), MOE_SUB)
            return pltpu.make_async_copy(zbuf, xs_hbm.at[pl.ds(row, MOE_SUB)], zsem)

        def start(k, carry):
            @pl.when(zl_ref[k] >= 0)
            def _():
                zero_copy(k).start()
            return carry

        def wait(k, carry):
            @pl.when(zl_ref[k] >= 0)
            def _():
                zero_copy(k).wait()
            return carry

        lax.fori_loop(0, zl_ref.shape[0], start, 0)
        lax.fori_loop(0, zl_ref.shape[0], wait, 0)

    def issue(r, carry):
        _row_copy(h_ref, r, xs_hbm, d1_ref[i * tm + r], sem).start(priority=0)
        _row_copy(h_ref, r, xs_hbm, d2_ref[i * tm + r], sem).start(priority=1)
        return carry

    lax.fori_loop(0, tm, issue, 0, unroll=8)
    for _ in range(2):
        pltpu.make_async_copy(h_ref, xs_hbm.at[pl.ds(0, tm)], sem).wait()


def _moe_dispatch(dest1, dest2, zlist, h, n_rows, tm):
    n, d = h.shape
    grid_spec = pltpu.PrefetchScalarGridSpec(
        num_scalar_prefetch=3,
        grid=(n // tm,),
        in_specs=[pl.BlockSpec((tm, d), lambda i, *_: (i, 0))],
        out_specs=pl.BlockSpec(memory_space=pl.ANY),
        scratch_shapes=[pltpu.VMEM((MOE_SUB, d), F32), pltpu.SemaphoreType.DMA(()), pltpu.SemaphoreType.DMA(())],
    )
    return pl.pallas_call(
        _dispatch_kernel, grid_spec=grid_spec,
        out_shape=jax.ShapeDtypeStruct((n_rows, d), F32),
        compiler_params=_cparams("arbitrary"),
        name="moe_dispatch",
    )(dest1, dest2, zlist, h)


def _expert_kernel(blk_e_ref, nvalid_ref, xs_ref, wg_ref, wu_ref, wd_ref, y_ref, xb, acc):
    b = pl.program_id(0)
    j = pl.program_id(1)
    tb = xb.shape[0]
    nsub = tb // MOE_SUB
    nv = nvalid_ref[b]
    wg = wg_ref[...].astype(BF16)
    wu = wu_ref[...].astype(BF16)
    wd = wd_ref[...].astype(BF16)

    def run(rows):
        @pl.when(j == 0)
        def _():
            xb[rows, :] = xs_ref[rows, :].astype(BF16)

        x = xb[rows, :]
        a = (_silu(_dot(x, wg)) * _dot(x, wu)).astype(BF16)
        part = _dot(a, wd)

        @pl.when(j == 0)
        def _():
            acc[rows, :] = part

        @pl.when(j > 0)
        def _():
            acc[rows, :] = acc[rows, :] + part

        @pl.when(j == pl.num_programs(1) - 1)
        def _():
            y_ref[rows, :] = acc[rows, :]

    @pl.when(nv == tb)
    def _():
        run(pl.ds(0, tb))

    for s in range(nsub):
        rows = pl.ds(s * MOE_SUB, MOE_SUB)

        @pl.when(jnp.logical_and(s * MOE_SUB < nv, nv < tb))
        def _():
            run(rows)

        @pl.when(jnp.logical_and(s * MOE_SUB >= nv, j == pl.num_programs(1) - 1))
        def _():
            y_ref[rows, :] = jnp.zeros((MOE_SUB, y_ref.shape[1]), F32)


def _moe_experts(blk_e, nvalid, xs, wg, wu, wd, lj, tb, tf):
    n_rows, d = xs.shape
    f = wg.shape[3]
    nj = f // tf

    def jm(b, j, nv):
        return jnp.where(nv[b] > 0, j, nj - 1)

    grid_spec = pltpu.PrefetchScalarGridSpec(
        num_scalar_prefetch=2,
        grid=(n_rows // tb, nj),
        in_specs=[pl.BlockSpec((tb, d), lambda b, j, be, nv: (b, 0)),
                  pl.BlockSpec((None, None, d, tf), lambda b, j, be, nv: (lj, be[b], 0, jm(b, j, nv))),
                  pl.BlockSpec((None, None, d, tf), lambda b, j, be, nv: (lj, be[b], 0, jm(b, j, nv))),
                  pl.BlockSpec((None, None, tf, d), lambda b, j, be, nv: (lj, be[b], jm(b, j, nv), 0))],
        out_specs=pl.BlockSpec((tb, d), lambda b, j, be, nv: (b, 0)),
        scratch_shapes=[pltpu.VMEM((tb, d), BF16), pltpu.VMEM((tb, d), F32)],
    )
    return pl.pallas_call(
        _expert_kernel, grid_spec=grid_spec,
        out_shape=jax.ShapeDtypeStruct((n_rows, d), F32),
        compiler_params=_cparams("parallel", "arbitrary"),
        name="moe_experts",
    )(blk_e, nvalid, xs, wg, wu, wd)


def _combine_kernel(d1_ref, d2_ref, ys_hbm, route_ref, x_ref, gp_ref, gs_ref, pg_ref, pb_ref, *rest,
                    alpha, np_tiles, split):
    (*outs, y1, y2, sem1, sem2) = rest
    i = pl.program_id(0)
    tm = x_ref.shape[0]

    def issue(r, carry):
        _row_copy(ys_hbm, d1_ref[i * tm + r], y1, r, sem1).start(priority=0)
        _row_copy(ys_hbm, d2_ref[i * tm + r], y2, r, sem2).start(priority=1)
        return carry

    lax.fori_loop(0, tm, issue, 0, unroll=8)
    pltpu.make_async_copy(ys_hbm.at[pl.ds(0, tm)], y1, sem1).wait()
    pltpu.make_async_copy(ys_hbm.at[pl.ds(0, tm)], y2, sem2).wait()
    r = route_ref[...]
    f = y1[...] * r[:, 2:3] + y2[...] * r[:, 3:4]
    out = _post_residual(x_ref, gp_ref, gs_ref, f, pg_ref, pb_ref, i >= np_tiles, alpha)
    if split:
        op_ref, os_ref = outs

        @pl.when(i < np_tiles)
        def _():
            op_ref[...] = out

        @pl.when(i >= np_tiles)
        def _():
            os_ref[...] = out
    else:
        outs[0][...] = out


def _moe_combine(dest1, dest2, ys, route, x, mod_p, mod_s, pg, pb, l, rm, alpha, split):
    n, d = x.shape
    tm = rm.t
    if split:
        n_p = rm.np_tiles * tm
        out_specs = [pl.BlockSpec((tm, d), lambda i, *_: (rm.prompt(i), 0)),
                     pl.BlockSpec((tm, d), lambda i, *_: (rm.sample(i), 0))]
        out_shape = [jax.ShapeDtypeStruct((n_p, d), F32), jax.ShapeDtypeStruct((n - n_p, d), F32)]
    else:
        out_specs = pl.BlockSpec((tm, d), lambda i, *_: (i, 0))
        out_shape = jax.ShapeDtypeStruct((n, d), F32)
    post = pl.BlockSpec((None, None, 1, d), lambda i, *_: (l, 1, 0, 0))
    grid_spec = pltpu.PrefetchScalarGridSpec(
        num_scalar_prefetch=2,
        grid=(n // tm,),
        in_specs=[pl.BlockSpec(memory_space=pl.ANY),
                  pl.BlockSpec((tm, LANES), lambda i, *_: (i, 0)),
                  pl.BlockSpec((tm, d), lambda i, *_: (i, 0)),
                  *rm.mod_specs(l, 5, d), post, post],
        out_specs=out_specs,
        scratch_shapes=[pltpu.VMEM((tm, d), F32), pltpu.VMEM((tm, d), F32),
                        pltpu.SemaphoreType.DMA(()), pltpu.SemaphoreType.DMA(())],
    )
    return pl.pallas_call(
        functools.partial(_combine_kernel, alpha=alpha, np_tiles=rm.np_tiles, split=split), grid_spec=grid_spec,
        out_shape=out_shape,
        compiler_params=_cparams("arbitrary"),
        name="moe_combine",
    )(dest1, dest2, ys, route, x, mod_p, mod_s, pg, pb)


def _moe_ffn(h, route, wg, wu, wd, x, mod_p, mod_s, pg, pb, l, lj, rm, tb, tf, alpha, split):
    n = h.shape[0]
    rank, tot = _moe_rank(route, rm.t)
    counts = tot[0, :N_EXPERTS].astype(jnp.int32)
    padded = (counts + tb - 1) // tb * tb
    pad_end = jnp.cumsum(padded)
    pad_start = pad_end - padded
    experts = jnp.arange(N_EXPERTS, dtype=jnp.int32)

    def slot(col):
        e = route[:, col].astype(jnp.int32)
        start = jnp.sum(jnp.where(e[:, None] == experts[None, :], pad_start[None, :], 0), axis=1)
        return start + rank[:, col].astype(jnp.int32)

    dest1, dest2 = slot(0), slot(1)
    n_blocks = -(-(2 * n + N_EXPERTS * (tb - 1)) // tb)
    blk_start = jnp.arange(n_blocks, dtype=jnp.int32) * tb
    blk_e = jnp.minimum(jnp.sum(pad_end[None, :] <= blk_start[:, None], axis=1), N_EXPERTS - 1).astype(jnp.int32)
    own_end = jnp.sum(jnp.where(blk_e[:, None] == experts[None, :], (pad_start + counts)[None, :], 0), axis=1)
    nvalid = jnp.clip(own_end - blk_start, 0, tb).astype(jnp.int32)
    nvalid = jnp.where(blk_start < pad_end[-1], nvalid, 0)
    per_blk = tb // MOE_SUB
    sub_start = jnp.arange(n_blocks * per_blk, dtype=jnp.int32) * MOE_SUB
    sub_room = jnp.repeat(blk_start + nvalid, per_blk) - sub_start
    zlist = jnp.where(sub_room < MOE_SUB, sub_start, -1).astype(jnp.int32)
    xs = _moe_dispatch(dest1, dest2, zlist, h, n_blocks * tb, rm.t)
    ys = _moe_experts(blk_e, nvalid, xs, wg, wu, wd, lj, tb, tf)
    return _moe_combine(dest1, dest2, ys, route, x, mod_p, mod_s, pg, pb, l, rm, alpha, split)


def _pack_w_in(w_in, b_in):
    depth, d, _ = w_in.shape
    q_end = 2 * D_CONV + N_HEADS * HEAD_DIM
    k_end = q_end + N_KV * HEAD_DIM
    a_end = k_end + N_KV * HEAD_DIM
    m_end = a_end + 4 * M_HEADS * M_DK
    if_end = m_end + 2 * M_HEADS

    def pack(a):
        return jnp.concatenate([a[..., if_end:], a[..., :q_end], a[..., a_end:m_end], a[..., q_end:a_end]], axis=-1)

    def gates(a):
        return jnp.pad(a[..., m_end:if_end], [(0, 0)] * (a.ndim - 1) + [(0, LANES - 2 * M_HEADS)])

    return (pack(w_in).astype(BF16), pack(b_in).reshape(depth, 1, Z_W),
            gates(w_in).astype(BF16), gates(b_in).reshape(depth, 1, LANES))


def kernel(x_prompt, x_sample, cache_swa_k, cache_swa_v, state_conv, state_mlstm_C, state_mlstm_n, state_mlstm_m, c_prompt, c_sample, w_ada, b_ada, w_in, b_in, conv_w, conv_b, conv_ln_g, conv_ln_b, w_conv_out, attn_sinks, rel_bias, w_attn_out, m_norm_g, w_m_out, w_out, post_ln_g, post_ln_b, ffn_w_gate, ffn_w_up, ffn_w_down, router_w, router_b, moe_w_gate, moe_w_up, moe_w_down):
    bp, tp, d = x_prompt.shape
    bs, ts, _ = x_sample.shape
    depth = w_ada.shape[0]
    alpha = (2 * depth) ** 0.25
    n_p, n_s = bp * tp, bs * ts
    tm = n_s
    assert d == D_MODEL and tp % tm == 0 and tp % WINDOW == 0 and tm % 32 == 0
    rm = _RowMap(tm, bp, tp, n_p)
    rm_half = _RowMap(tm // 2, bp, tp, n_p)
    wb = cache_swa_k.shape[2]
    big = n_p >= 4096
    tc = 512 if big else tm
    lm = 256 if big else min(tp, 128)
    tb = 1024 if big else 2 * MOE_SUB
    bs_blk = 32 if bs % 32 == 0 else bs
    bs_att = 16 if bs % 16 == 0 else bs
    bs_m = 4 if bs % 4 == 0 else 1
    lts = 16

    x = jnp.concatenate([x_prompt.reshape(n_p, d), jnp.transpose(x_sample, (1, 0, 2)).reshape(n_s, d)], axis=0)

    nc_rows = -(-(bp + bs) // 8) * 8
    c_all = jnp.zeros((nc_rows, d), F32).at[:bp].set(c_prompt).at[bp:bp + bs].set(c_sample)
    mod = _ada_mod(c_all, w_ada, b_ada)
    mod_p = mod[:, :bp].reshape(depth, bp, 1, 6 * d)
    mod_s = jnp.tile(mod[:, bp:bp + bs], (1, ts, 1))

    w_in_p, b_in_p, w_if, b_if = _pack_w_in(w_in, b_in)
    wc_b, wa_b, wm_b, wo_b = (w.astype(BF16) for w in (w_conv_out, w_attn_out, w_m_out, w_out))
    fg_b, fu_b, fd_b = (w.astype(BF16) for w in (ffn_w_gate, ffn_w_up, ffn_w_down))
    cw_pad = jnp.pad(conv_w, ((0, 0), (0, CONV_PAD - CONV_W), (0, 0)))
    cvecs = [v.reshape(depth, 1, D_CONV) for v in (conv_b, conv_ln_g, conv_ln_b)]
    rw_pad = jnp.pad(router_w, ((0, 0), (0, 0), (0, LANES - N_EXPERTS)))
    rb_pad = jnp.pad(router_b, ((0, 0), (0, LANES - N_EXPERTS)), constant_values=NEG_INF)[:, None, :]
    pg = post_ln_g.reshape(depth, 2, 1, d)
    pb = post_ln_b.reshape(depth, 2, 1, d)
    gamma = m_norm_g.reshape(depth, 1, M_HEADS * M_DV)
    sinks = attn_sinks.astype(F32)

    qi = jnp.arange(WINDOW)[:, None]
    kj = jnp.arange(2 * WINDOW)[None, :]
    dist_p = qi + WINDOW - kj
    bh = _bias_heads(rel_bias, dist_p, (dist_p >= 0) & (dist_p < WINDOW))
    bias_p = bh.reshape(N_HEADS // 2, 2, WINDOW, 2 * WINDOW).transpose(0, 2, 1, 3).reshape(
        N_HEADS // 2, WINDOW, 4 * WINDOW)
    dist_s = jnp.arange(ts)[:, None] + wb - jnp.arange(wb + ts)[None, :]
    bias_s = _bias_heads(rel_bias, dist_s, (dist_s >= 0) & (dist_s < WINDOW)).reshape(
        N_KV, Q_PER_KV * ts, wb + ts)
    kc_all = cache_swa_k.reshape(depth, bs, wb, N_KV * HEAD_DIM)
    vc_all = cache_swa_v.reshape(depth, bs, wb, N_KV * HEAD_DIM)

    f_dense = ffn_w_gate.shape[2]
    tf_dense = f_dense // 2 if (f_dense // 2) % LANES == 0 else f_dense
    f_moe = moe_w_gate.shape[3]
    tf_moe = 512 if f_moe % 512 == 0 else f_moe

    n0_all = state_mlstm_n[:, :, :, None, :]
    m0_all = _m_state_in(state_mlstm_m)
    s_c = None
    new_p = [[] for _ in range(6)]
    new_s = [[] for _ in range(6)]
    for l in range(depth):
        j = l // 2
        z, zif = _ln_proj(x, mod_p, mod_s, w_in_p, b_in_p, w_if, b_if, l, rm)
        zs3 = z[n_p:].reshape(ts, bs, Z_W)

        cp, ns_p = _conv_prompt(z, jnp.zeros((bp, CONV_PAD, D_CONV), F32), cw_pad, *cvecs, l, bp, tp, tc)
        cs3, a_s3 = _conv_sample(zs3, state_conv, cw_pad, *cvecs, l, bs_blk)
        new_p[2].append(ns_p[:, CONV_PAD - CONV_W + 1:])
        new_s[2].append(jnp.transpose(a_s3, (1, 0, 2)))

        sink_h = sinks[l].reshape(N_KV, Q_PER_KV, 1)
        sink_s = jnp.broadcast_to(sink_h, (N_KV, Q_PER_KV, ts)).reshape(N_KV, Q_PER_KV * ts, 1)
        ap = _attn_prompt(z, bias_p, sinks, l, bp, tp)
        nk = min(WINDOW, tp)
        kv_tail = jnp.stack([z[(b + 1) * tp - nk:(b + 1) * tp, Z_K:Z_K + 2 * N_KV * HEAD_DIM]
                             for b in range(bp)]).astype(F32)
        new_p[0].append(kv_tail[..., :N_KV * HEAD_DIM].reshape(bp, nk, N_KV, HEAD_DIM))
        new_p[1].append(kv_tail[..., N_KV * HEAD_DIM:].reshape(bp, nk, N_KV, HEAD_DIM))
        q_s = zs3[:, :, Z_Q:Z_Q + N_HEADS * HEAD_DIM].reshape(ts, bs, N_KV, Q_PER_KV, HEAD_DIM)
        q4 = jnp.transpose(q_s, (1, 2, 3, 0, 4)).reshape(bs, N_KV, Q_PER_KV * ts, HEAD_DIM)
        k_s = jnp.transpose(zs3[:, :, Z_K:Z_K + N_KV * HEAD_DIM].reshape(ts, bs, N_KV, HEAD_DIM), (1, 0, 2, 3))
        v_s = jnp.transpose(zs3[:, :, Z_V:Z_V + N_KV * HEAD_DIM].reshape(ts, bs, N_KV, HEAD_DIM), (1, 0, 2, 3))
        o4 = _attn_sample(q4, kc_all, vc_all, jnp.transpose(k_s, (0, 2, 1, 3)), jnp.transpose(v_s, (0, 2, 1, 3)),
                          bias_s, sink_s, l, bs_att)
        as_ = jnp.transpose(o4.reshape(bs, N_KV, Q_PER_KV, ts, HEAD_DIM), (3, 0, 1, 2, 4)).reshape(n_s, -1).astype(BF16)
        new_s[0].append(k_s.astype(F32))
        new_s[1].append(v_s.astype(F32))

        ncp = tp // lm
        if_p = zif[:n_p, :2 * M_HEADS].reshape(bp * ncp, lm, 2 * M_HEADS)
        mp, c1p, n1p, m1p = _mlstm_prompt(z, zif, jnp.transpose(if_p, (0, 2, 1)), gamma, l, bp, ncp, lm)
        new_p[3].append(c1p)
        new_p[4].append(n1p[:, :, 0])
        new_p[5].append(m1p[:, :M_HEADS, 0])
        tpad = ((0, 0), (0, lts - ts), (0, 0))
        zm3 = jnp.pad(jnp.transpose(zs3[:, :, Z_MQ:Z_K], (1, 0, 2)), tpad)
        if3 = jnp.pad(jnp.transpose(zif[n_p:].reshape(ts, bs, LANES), (1, 0, 2)), tpad)
        ms, s_c, n1s, m1s = _mlstm_sample(zm3, if3, jnp.transpose(if3[:, :, :2 * M_HEADS], (0, 2, 1)), gamma,
                                          state_mlstm_C, n0_all[l], m0_all[l], s_c, l, ts, bs_m)
        ms = jnp.transpose(ms[:, :ts], (1, 0, 2)).reshape(n_s, -1)
        new_s[4].append(n1s[:, :, 0])
        new_s[5].append(m1s[:, :M_HEADS, 0])

        moe = l % 2 == 1
        router = (rw_pad, rb_pad, j) if moe else None
        outs = _merge(cp, cs3.reshape(n_s, D_CONV), ap, as_, mp, ms, z, x, mod_p, mod_s,
                      wc_b, wa_b, wm_b, wo_b, pg, pb, l, rm_half, alpha, router)
        if moe:
            x1, h2, route = outs
            x = _moe_ffn(h2, route, moe_w_gate, moe_w_up, moe_w_down, x1, mod_p, mod_s, pg, pb,
                         l, j, rm, tb, tf_moe, alpha, split=l == depth - 1)
        else:
            x1, h2 = outs
            x = _ffn_dense(h2, fg_b, fu_b, fd_b, x1, mod_p, mod_s, pg, pb, l, j, rm, tf_dense, alpha)

    x_p, x_s = x if isinstance(x, (list, tuple)) else (x[:n_p], x[n_p:])
    y_p = x_p.reshape(bp, tp, d)
    y_s = jnp.transpose(x_s.reshape(ts, bs, d), (1, 0, 2))
    p_k, p_v, p_conv, p_c, p_n, p_m = [jnp.stack(a) for a in new_p]
    s_k, s_v, s_conv = [jnp.concatenate([old[:, :, ts:], jnp.stack(new)], axis=2)
                        for old, new in zip((cache_swa_k, cache_swa_v, state_conv), new_s[:3])]
    s_n, s_m = jnp.stack(new_s[4]), jnp.stack(new_s[5])
    return (y_p, y_s, p_k, p_v, p_conv, p_c, p_n, p_m, s_k, s_v, s_conv, s_c, s_n, s_m)
```

```python
import functools
import math

import jax
import jax.numpy as jnp
from jax import lax
from jax.experimental import pallas as pl
from jax.experimental.pallas import tpu as pltpu

F32 = jnp.float32
BF16 = jnp.bfloat16

D_MODEL = 1024
D_CONV = 512
CONV_W = 31
CONV_PAD = 32
N_HEADS = 8
N_KV = 2
HEAD_DIM = 64
Q_PER_KV = N_HEADS // N_KV
WINDOW = 128
N_BUCKETS = 32
MAX_DIST = 128
M_HEADS = 4
M_DK = 128
M_DV = 128
N_EXPERTS = 8
LN_EPS = 1e-5
LANES = 128
NEG_INF = float("-inf")
VMEM_LIMIT = 56 * 1024 * 1024

Z_G, Z_UA, Z_UB, Z_Q = 0, 3072, 3584, 4096
Z_MQ, Z_MK, Z_MV, Z_MO = 4608, 5120, 5632, 6144
Z_K, Z_V, Z_W = 6656, 6784, 6912
TN_IN = 3456
ATT_QB = 4
ATT_GROUP = 8
MOE_SUB = 256
CONV_CHUNK = 32

def _cparams(*sem):
    return pltpu.CompilerParams(dimension_semantics=sem, vmem_limit_bytes=VMEM_LIMIT)


def _sigmoid(x):
    return 1.0 / (1.0 + jnp.exp(-x))


def _silu(x):
    return x * _sigmoid(x)


def _log_sigmoid(x):
    return jnp.minimum(x, 0.0) - jnp.log(1.0 + jnp.exp(-jnp.abs(x)))


def _norm(x):
    mu = jnp.mean(x, axis=-1, keepdims=True)
    xc = x - mu
    var = jnp.mean(xc * xc, axis=-1, keepdims=True)
    return xc * lax.rsqrt(var + LN_EPS)


def _dot(a, b):
    return jnp.dot(a, b, preferred_element_type=F32)


def _dot_nt(a, b):
    return lax.dot_general(a, b, (((1,), (1,)), ((), ())), preferred_element_type=F32)


def _dot_tn(a, b):
    return lax.dot_general(a, b, (((0,), (0,)), ((), ())), preferred_element_type=F32)


def _dot_hi(a, b):
    return jnp.dot(a, b, preferred_element_type=F32, precision=lax.Precision.HIGHEST)


def _split_bf16(a):
    hi = a.astype(BF16)
    return hi, (a - hi.astype(F32)).astype(BF16)


class _RowMap:
    def __init__(self, t, bp, tp, n_p, n_s):
        self.t = t
        self.bp = bp
        self.per_seq = tp // t
        self.np_tiles = n_p // t
        self.ns_tiles = n_s // t

    def seq(self, i):
        return jnp.minimum(i // self.per_seq, self.bp - 1)

    def prompt(self, i):
        return jnp.minimum(i, self.np_tiles - 1)

    def sample(self, i):
        return jnp.clip(i - self.np_tiles, 0, self.ns_tiles - 1)

    def mod_specs(self, l, k, d):
        return (pl.BlockSpec((None, None, 1, d), lambda i, *_: (l, self.seq(i), 0, k)),
                pl.BlockSpec((None, self.t, d), lambda i, *_: (l, self.sample(i), k)))


    def x_specs(self, xs_arr, d):
        off = self.np_tiles if xs_arr.shape[0] > self.ns_tiles * self.t else 0
        return (pl.BlockSpec((self.t, d), lambda i, *_: (self.prompt(i), 0)),
                pl.BlockSpec((self.t, d), lambda i, *_: (off + self.sample(i), 0)))


def _pick(is_s, p_ref, s_ref):
    return jnp.where(is_s, s_ref[...], p_ref[...])


def _ada_kernel(c_ref, w_ref, b_ref, o_ref):
    s = _silu(c_ref[...]).astype(BF16)
    o_ref[0] = _dot(s, w_ref[0].astype(BF16)) + b_ref[0]


def _ada_mod(c_all, w_ada, b_ada):
    depth, d, n6 = w_ada.shape
    rows = c_all.shape[0]
    return pl.pallas_call(
        _ada_kernel,
        grid=(depth, n6 // d),
        in_specs=[pl.BlockSpec((rows, d), lambda l, j: (0, 0)),
                  pl.BlockSpec((1, d, d), lambda l, j: (l, 0, j)),
                  pl.BlockSpec((1, 1, d), lambda l, j: (l, 0, j))],
        out_specs=pl.BlockSpec((1, rows, d), lambda l, j: (l, 0, j)),
        out_shape=jax.ShapeDtypeStruct((depth, rows, n6), F32),
        compiler_params=_cparams("parallel", "parallel"),
        name="ada_mod",
    )(c_all, w_ada, b_ada.reshape(depth, 1, n6))


def _ln_proj_kernel(xp_ref, xs_ref, shp_ref, shs_ref, scp_ref, scs_ref, w_ref, b_ref, wif_ref, bif_ref,
                    z_ref, zif_ref, h_scr, *, np_tiles):
    is_s = pl.program_id(0) >= np_tiles
    first = pl.program_id(1) == 0

    def prologue(x_ref, sh_ref, sc_ref):
        h = (_norm(x_ref[...]) * (1.0 + sc_ref[...]) + sh_ref[...]).astype(BF16)
        h_scr[...] = h
        zif_ref[...] = _dot(h, wif_ref[...]) + bif_ref[...]

    @pl.when(jnp.logical_and(first, is_s))
    def _():
        prologue(xs_ref, shs_ref, scs_ref)

    @pl.when(jnp.logical_and(first, jnp.logical_not(is_s)))
    def _():
        prologue(xp_ref, shp_ref, scp_ref)

    z_ref[...] = (_dot(h_scr[...], w_ref[...]) + b_ref[...]).astype(BF16)


def _ln_proj(xp, xs, mod_p, mod_s, w, b, wif, bif, l, rm, n):
    d = xp.shape[1]
    zw = w.shape[2]
    tm = rm.t
    shp, shs = rm.mod_specs(l, 0, d)
    scp, scs = rm.mod_specs(l, 1, d)
    return pl.pallas_call(
        functools.partial(_ln_proj_kernel, np_tiles=rm.np_tiles),
        grid=(n // tm, zw // TN_IN),
        in_specs=[*rm.x_specs(xs, d),
                  shp, shs, scp, scs,
                  pl.BlockSpec((None, d, TN_IN), lambda i, j: (l, 0, j)),
                  pl.BlockSpec((None, 1, TN_IN), lambda i, j: (l, 0, j)),
                  pl.BlockSpec((None, d, LANES), lambda i, j: (l, 0, 0)),
                  pl.BlockSpec((None, 1, LANES), lambda i, j: (l, 0, 0))],
        out_specs=[pl.BlockSpec((tm, TN_IN), lambda i, j: (i, j)),
                   pl.BlockSpec((tm, LANES), lambda i, j: (i, 0))],
        out_shape=[jax.ShapeDtypeStruct((n, zw), BF16), jax.ShapeDtypeStruct((n, LANES), F32)],
        scratch_shapes=[pltpu.VMEM((tm, d), BF16)],
        compiler_params=_cparams("parallel", "arbitrary"),
        name="ln_proj",
    )(xp, xs, mod_p, mod_s, mod_p, mod_s, w, b, wif, bif)


def _conv_tail(yc, g_ref, b_ref):
    y = _norm(yc) * g_ref[...] + b_ref[...]
    return _silu(y).astype(BF16)


def _conv_prompt_kernel(ua_ref, ub_ref, st_ref, cw_ref, cb_ref, g_ref, b_ref, o_ref, ns_ref,
                        ext, shifted, yc, wrep):
    t = pl.program_id(1)
    tc = ua_ref.shape[0]
    sub = 8

    @pl.when(t == 0)
    def _():
        ext[0:CONV_PAD, :] = st_ref[0]

    @pl.when(t > 0)
    def _():
        ext[0:CONV_PAD, :] = ext[tc:tc + CONV_PAD, :]

    ext[CONV_PAD:, :] = ua_ref[...].astype(F32) * _sigmoid(ub_ref[...].astype(F32))
    for s in range(1, sub):
        shifted[s - 1] = ext[s:s + tc + CONV_PAD - sub, :]
    off = CONV_PAD - (CONV_W - 1)
    for w in range(CONV_W):
        wrep[w] = jnp.broadcast_to(cw_ref[w:w + 1, :], (sub, D_CONV))
    groups = CONV_CHUNK // sub

    for r0 in range(0, tc, CONV_CHUNK):
        acc = jnp.broadcast_to(cb_ref[...].reshape(1, 1, D_CONV), (groups, sub, D_CONV))
        for w in range(CONV_W):
            base, s = (off + w) // sub * sub, (off + w) % sub
            src = ext if s == 0 else shifted.at[s - 1]
            win = src[r0 + base:r0 + base + CONV_CHUNK, :]
            acc = acc + win.reshape(groups, sub, D_CONV) * wrep[w][None]
        yc[r0:r0 + CONV_CHUNK, :] = acc.reshape(CONV_CHUNK, D_CONV)
    o_ref[...] = _conv_tail(yc[...], g_ref, b_ref)

    @pl.when(t == pl.num_programs(1) - 1)
    def _():
        ns_ref[0] = ext[tc:tc + CONV_PAD, :]


def _conv_vec_specs(l, nargs):
    return [pl.BlockSpec((None, 1, D_CONV), lambda *_: (l, 0, 0)) for _ in range(nargs)]


def _conv_prompt(z, state_pad, cw, cb, g, b, l, bp, tp, tc):
    nt = tp // tc
    return pl.pallas_call(
        _conv_prompt_kernel,
        grid=(bp, nt),
        in_specs=[pl.BlockSpec((tc, D_CONV), lambda bb, t: (bb * nt + t, Z_UA // D_CONV)),
                  pl.BlockSpec((tc, D_CONV), lambda bb, t: (bb * nt + t, Z_UB // D_CONV)),
                  pl.BlockSpec((1, CONV_PAD, D_CONV), lambda bb, t: (bb, 0, 0)),
                  pl.BlockSpec((None, CONV_PAD, D_CONV), lambda bb, t: (l, 0, 0))] + _conv_vec_specs(l, 3),
        out_specs=[pl.BlockSpec((tc, D_CONV), lambda bb, t: (bb * nt + t, 0)),
                   pl.BlockSpec((1, CONV_PAD, D_CONV), lambda bb, t: (bb, 0, 0))],
        out_shape=[jax.ShapeDtypeStruct((bp * tp, D_CONV), BF16),
                   jax.ShapeDtypeStruct((bp, CONV_PAD, D_CONV), F32)],
        scratch_shapes=[pltpu.VMEM((tc + CONV_PAD, D_CONV), F32),
                        pltpu.VMEM((7, tc + CONV_PAD - 8, D_CONV), F32),
                        pltpu.VMEM((tc, D_CONV), F32),
                        pltpu.VMEM((CONV_PAD, 8, D_CONV), F32)],
        compiler_params=_cparams("parallel", "arbitrary"),
        name="conv_prompt",
    )(z, z, state_pad, cw, cb, g, b)


def _conv_sample_kernel(ua_ref, ub_ref, st_ref, cw_ref, cb_ref, g_ref, b_ref, o_ref, a_ref):
    ts = ua_ref.shape[0]
    ns = CONV_W - 1
    a = ua_ref[...].astype(F32) * _sigmoid(ub_ref[...].astype(F32))
    a_ref[...] = a
    st = st_ref[...]
    row = lax.broadcasted_iota(jnp.int32, (ns, D_CONV), 0)
    for t in range(ts):
        wt = jnp.zeros((ns, D_CONV), F32)
        for j in range(t, ns):
            wt = jnp.where(row == j, cw_ref[j - t:j - t + 1, :], wt)
        yc = jnp.sum(st * wt[None], axis=1) + cb_ref[...]
        for t2 in range(t + 1):
            wi = CONV_W - 1 - (t - t2)
            yc = yc + a[t2] * cw_ref[wi:wi + 1, :]
        o_ref[t] = _conv_tail(yc, g_ref, b_ref)


def _conv_sample(zs3, state, cw, cb, g, b, l, bs_blk):
    ts, bs, _ = zs3.shape
    ns = CONV_W - 1
    return pl.pallas_call(
        _conv_sample_kernel,
        grid=(bs // bs_blk,),
        in_specs=[pl.BlockSpec((ts, bs_blk, D_CONV), lambda i: (0, i, Z_UA // D_CONV)),
                  pl.BlockSpec((ts, bs_blk, D_CONV), lambda i: (0, i, Z_UB // D_CONV)),
                  pl.BlockSpec((None, bs_blk, ns, D_CONV), lambda i: (l, i, 0, 0)),
                  pl.BlockSpec((None, CONV_PAD, D_CONV), lambda i: (l, 0, 0))] + _conv_vec_specs(l, 3),
        out_specs=[pl.BlockSpec((ts, bs_blk, D_CONV), lambda i: (0, i, 0)),
                   pl.BlockSpec((ts, bs_blk, D_CONV), lambda i: (0, i, 0))],
        out_shape=[jax.ShapeDtypeStruct((ts, bs, D_CONV), BF16),
                   jax.ShapeDtypeStruct((ts, bs, D_CONV), F32)],
        compiler_params=_cparams("parallel"),
        name="conv_sample",
    )(zs3, zs3, state, cw, cb, g, b)


def _t5_bucket(dist):
    max_exact = N_BUCKETS // 2
    d = jnp.maximum(dist, 0)
    large = max_exact + (jnp.log(jnp.maximum(d, 1).astype(F32) / max_exact)
                         / math.log(MAX_DIST / max_exact) * (N_BUCKETS - max_exact)).astype(jnp.int32)
    return jnp.where(d < max_exact, d, jnp.minimum(large, N_BUCKETS - 1))


def _bias_heads(rel_bias, dist, valid):
    onehot = (_t5_bucket(dist)[..., None] == jnp.arange(N_BUCKETS)).astype(F32)
    bias = jnp.einsum("qkb,bh->qkh", onehot, rel_bias.astype(F32), precision=lax.Precision.HIGHEST)
    bias = jnp.where(valid[..., None], bias, NEG_INF)
    return jnp.transpose(bias, (2, 0, 1))


def _attn_prompt_kernel(sink_ref, q_ref, kc_ref, kp_ref, vc_ref, vp_ref, bias_ref, o_ref, *, l):
    first = pl.program_id(1) == 0
    w = WINDOW
    nq = q_ref.shape[0] // w
    kall = jnp.concatenate([kp_ref[...], kc_ref[...]], axis=0).astype(F32)
    vall = jnp.concatenate([vp_ref[...], vc_ref[...]], axis=0).astype(F32)
    lane = lax.broadcasted_iota(jnp.int32, kall.shape, 1)
    lo = lane < HEAD_DIM
    kroll = pltpu.roll(kall, HEAD_DIM, 1)
    vroll = pltpu.roll(vall, HEAD_DIM, 1)

    def halves(a, aroll, g):
        if g == 0:
            return jnp.where(lo, a, 0.0).astype(BF16), jnp.where(lo, 0.0, aroll).astype(BF16)
        return jnp.where(lo, aroll, 0.0).astype(BF16), jnp.where(lo, 0.0, a).astype(BF16)

    kh = [halves(kall, kroll, g) for g in range(N_KV)]
    vh = [halves(vall, vroll, g) for g in range(N_KV)]
    col = lax.broadcasted_iota(jnp.int32, (w, 4 * w), 1)
    prev_col = (col % (2 * w)) < w
    tiles_per_g = Q_PER_KV // 2
    units = [(qi, tile) for qi in range(nq) for tile in range(N_HEADS // 2)]
    for u0 in range(0, len(units), ATT_GROUP):
        group = units[u0:u0 + ATT_GROUP]
        scores = []
        for qi, tile in group:
            r0, g = qi * w, tile // tiles_per_g
            q = q_ref[r0:r0 + w, tile * LANES:(tile + 1) * LANES]
            kk = jnp.concatenate([kh[g][0][r0:r0 + 2 * w], kh[g][1][r0:r0 + 2 * w]], axis=0)
            s = _dot_nt(q, kk) * (HEAD_DIM ** -0.5) + bias_ref[tile]
            if qi == 0:
                s = jnp.where(jnp.logical_and(first, prev_col), NEG_INF, s)
            scores.append(s)
        probs = []
        for (qi, tile), s in zip(group, scores):
            ps = []
            for half in range(2):
                sh = s[:, half * 2 * w:(half + 1) * 2 * w]
                sink = sink_ref[l, 2 * tile + half]
                mx = jnp.maximum(jnp.max(sh, axis=-1, keepdims=True), sink)
                p = jnp.exp(sh - mx)
                den = jnp.sum(p, axis=-1, keepdims=True) + jnp.exp(sink - mx)
                ps.append((p * (1.0 / den)).astype(BF16))
            probs.append(jnp.concatenate(ps, axis=1))
        for (qi, tile), p in zip(group, probs):
            r0, g = qi * w, tile // tiles_per_g
            vv = jnp.concatenate([vh[g][0][r0:r0 + 2 * w], vh[g][1][r0:r0 + 2 * w]], axis=0)
            o_ref[r0:r0 + w, tile * LANES:(tile + 1) * LANES] = _dot(p, vv).astype(BF16)


def _attn_prompt(z, bias, sinks, l, bp, tp):
    w = WINDOW
    qb = ATT_QB if tp % (ATT_QB * w) == 0 else 1
    ns = tp // (qb * w)
    nb = tp // w
    kvw = N_KV * HEAD_DIM
    qw = N_HEADS * HEAD_DIM

    def cur(col):
        return lambda bb, i: (bb * ns + i, col)

    def prev(col):
        return lambda bb, i: (bb * nb + jnp.maximum(i * qb - 1, 0), col)

    return pl.pallas_call(
        functools.partial(_attn_prompt_kernel, l=l),
        grid=(bp, ns),
        in_specs=[pl.BlockSpec(memory_space=pltpu.SMEM),
                  pl.BlockSpec((qb * w, qw), cur(Z_Q // qw)),
                  pl.BlockSpec((qb * w, kvw), cur(Z_K // kvw)),
                  pl.BlockSpec((w, kvw), prev(Z_K // kvw)),
                  pl.BlockSpec((qb * w, kvw), cur(Z_V // kvw)),
                  pl.BlockSpec((w, kvw), prev(Z_V // kvw)),
                  pl.BlockSpec((N_HEADS // 2, w, 4 * w), lambda bb, i: (0, 0, 0))],
        out_specs=pl.BlockSpec((qb * w, qw), lambda bb, i: (bb * ns + i, 0)),
        out_shape=jax.ShapeDtypeStruct((bp * tp, qw), BF16),
        compiler_params=_cparams("parallel", "parallel"),
        name="attn_prompt",
    )(sinks, z, z, z, z, z, bias)


def _attn_sample_kernel(q_ref, kc_ref, vc_ref, kn_ref, vn_ref, bias_ref, sink_ref, o_ref):
    wb = kc_ref.shape[1]
    ts = kn_ref.shape[2]
    for g in range(N_KV):
        lo = g * HEAD_DIM
        qb = (q_ref[:, g].astype(F32) * (HEAD_DIM ** -0.5)).astype(BF16)
        kc = kc_ref[:, :, lo:lo + HEAD_DIM].astype(BF16)
        vc = vc_ref[:, :, lo:lo + HEAD_DIM].astype(BF16)
        kn = kn_ref[:, g].astype(F32)
        vn = vn_ref[:, g].astype(F32)
        bias = bias_ref[g]
        s_c = jnp.einsum("bqd,bkd->bqk", qb, kc, preferred_element_type=F32) + bias[None, :, :wb]
        qf = qb.astype(F32)
        s_n = [jnp.sum(qf * kn[:, j:j + 1, :], axis=-1, keepdims=True) + bias[None, :, wb + j:wb + j + 1]
               for j in range(ts)]
        sink = sink_ref[g][None]
        mx = jnp.maximum(jnp.max(s_c, axis=-1, keepdims=True), sink)
        for sj in s_n:
            mx = jnp.maximum(mx, sj)
        p_c = jnp.exp(s_c - mx)
        p_n = [jnp.exp(sj - mx) for sj in s_n]
        den = jnp.sum(p_c, axis=-1, keepdims=True) + jnp.exp(sink - mx)
        for pj in p_n:
            den = den + pj
        o = jnp.einsum("bqk,bkd->bqd", (p_c / den).astype(BF16), vc, preferred_element_type=F32)
        for j in range(ts):
            o = o + (p_n[j] / den).astype(BF16).astype(F32) * vn[:, j:j + 1, :]
        o_ref[:, g] = o


def _attn_sample(q4, kc, vc, kn, vn, bias, sinks, l, bs_blk):
    bs, _, rt, _ = q4.shape
    wb = kc.shape[2]
    ts = kn.shape[2]
    kvw = N_KV * HEAD_DIM
    return pl.pallas_call(
        _attn_sample_kernel,
        grid=(bs // bs_blk,),
        in_specs=[pl.BlockSpec((bs_blk, N_KV, rt, HEAD_DIM), lambda i: (i, 0, 0, 0)),
                  pl.BlockSpec((None, bs_blk, wb, kvw), lambda i: (l, i, 0, 0)),
                  pl.BlockSpec((None, bs_blk, wb, kvw), lambda i: (l, i, 0, 0)),
                  pl.BlockSpec((bs_blk, N_KV, ts, HEAD_DIM), lambda i: (i, 0, 0, 0)),
                  pl.BlockSpec((bs_blk, N_KV, ts, HEAD_DIM), lambda i: (i, 0, 0, 0)),
                  pl.BlockSpec((N_KV, rt, wb + ts), lambda i: (0, 0, 0)),
                  pl.BlockSpec((N_KV, rt, 1), lambda i: (0, 0, 0))],
        out_specs=pl.BlockSpec((bs_blk, N_KV, rt, HEAD_DIM), lambda i: (i, 0, 0, 0)),
        out_shape=jax.ShapeDtypeStruct((bs, N_KV, rt, HEAD_DIM), F32),
        compiler_params=_cparams("parallel"),
        name="attn_sample",
    )(q4, kc, vc, kn, vn, bias, sinks)


def _mlstm_kernel(*refs, t_valid, nseq, aliased, carried, chunk_axis, per_seq_inputs=False, first_layer=None):
    if aliased:
        refs = refs[:10] + refs[11:]
    if per_seq_inputs:
        groups = [refs[6 * s:6 * s + 6] for s in range(nseq)]
        if_ref, ifr_ref, q_ref, k_ref, v_ref, o_ref = (tuple(g[k] for g in groups) for k in range(6))
        refs = (None,) * 6 + refs[6 * nseq:]
    else:
        if_ref, ifr_ref, q_ref, k_ref, v_ref, o_ref = refs[:6]
    g_ref, c0_ref, n0_ref, m0_ref, h_ref, c1_ref, n1_ref, m1_ref = refs[6:14]
    c = pl.program_id(chunk_axis)
    L = (ifr_ref[0] if per_seq_inputs else ifr_ref).shape[-1]
    if carried:
        c_in, n_in, m_in = c_out, n_out, m_out = refs[14:]

        @pl.when(c == 0)
        def _():
            c_in[...] = c0_ref[...]
            n_in[...] = n0_ref[...]
            m_in[...] = m0_ref[...]
    else:
        (c_in, n_in, m_in), (c_out, n_out, m_out) = (c0_ref, n0_ref, m0_ref), (c1_ref, n1_ref, m1_ref)
        if first_layer is not None:
            for dd in range(c1_ref.shape[0]):
                if dd != first_layer:
                    c1_ref[dd] = jnp.zeros(c1_ref.shape[1:], F32)
            c_out = c1_ref.at[first_layer]

    def seq(ref, s):
        return ref[s] if isinstance(ref, tuple) else ref.at[s]

    tt = lax.broadcasted_iota(jnp.int32, (L, L), 0)
    ss = lax.broadcasted_iota(jnp.int32, (L, L), 1)
    causal = ss <= tt
    tril = causal.astype(F32)
    triu = (tt <= ss).astype(F32)
    seqs = range(nseq)
    heads = [(s_i, h) for s_i in seqs for h in range(M_HEADS)]
    mxu_sums = L % LANES == 0
    gates = []
    for s_i in seqs:
        ifc = seq(if_ref, s_i)[...]
        ifr = ifr_ref[s_i][0] if per_seq_inputs else ifr_ref[s_i]
        lf_c = _log_sigmoid(ifc)
        lf_r = _log_sigmoid(ifr)
        i_c, i_r = ifc, ifr
        if t_valid < L:
            rc = lax.broadcasted_iota(jnp.int32, (L, LANES), 0) < t_valid
            rr = lax.broadcasted_iota(jnp.int32, (2 * M_HEADS, L), 1) < t_valid
            lf_c = jnp.where(rc, lf_c, 0.0)
            lf_r = jnp.where(rr, lf_r, 0.0)
            i_c = jnp.where(rc, i_c, NEG_INF)
            i_r = jnp.where(rr, i_r, NEG_INF)
        gates.append((lf_c, lf_r, i_c, i_r))
    f_cs = [_dot_hi(tril, g[0]) for g in gates]
    f_rs = [_dot_hi(g[1], triu) for g in gates]
    state = {(s_i, h): (c_in[s_i, h], n_in[s_i, h], m_in[s_i, h:h + 1, :]) for s_i, h in heads}

    def wide(col):
        return jnp.concatenate([col] * (L // LANES), axis=1) if L >= LANES else col[:, :L]

    st1 = {}
    for s_i, h in heads:
        _, nrow, m0 = state[s_i, h]
        fc = jnp.broadcast_to(f_cs[s_i][:, M_HEADS + h:M_HEADS + h + 1], (L, LANES))
        fr = f_rs[s_i][M_HEADS + h:M_HEADS + h + 1, :]
        ir = gates[s_i][3][h:h + 1, :]
        dm = jnp.where(causal, wide(fc) - fr + ir, NEG_INF)
        m_t = jnp.maximum(m0 + fc, jnp.max(dm, axis=-1, keepdims=True))
        st1[s_i, h] = (fc, dm, m_t, jnp.exp(m0 + fc - m_t))
    st2 = {}
    for s_i, h in heads:
        lo = h * M_DK
        fc, dm, m_t, inter = st1[s_i, h]
        qb = seq(q_ref, s_i)[:, lo:lo + M_DK]
        kf = seq(k_ref, s_i)[:, lo:lo + M_DK].astype(F32) * (M_DK ** -0.5)
        vf = seq(v_ref, s_i)[:, lo:lo + M_DV].astype(F32)
        kb, vb = kf.astype(BF16), vf.astype(BF16)
        if mxu_sums:
            n_rows = jnp.broadcast_to(state[s_i, h][1], (M_DK, M_DK)).astype(BF16)
            qk = _dot_nt(qb, jnp.concatenate([kb, n_rows], axis=0))
            sc, qn_rep = qk[:, :L] * jnp.exp(dm - wide(m_t)), qk[:, L:]
        else:
            sc, qn_rep = _dot_nt(qb, kb) * jnp.exp(dm - wide(m_t)), None
        st2[s_i, h] = (qb, kf, vf, kb, vb, sc, qn_rep)
    st3 = {}
    for s_i, h in heads:
        cm, nrow, m0 = state[s_i, h]
        fc, dm, m_t, inter = st1[s_i, h]
        qb, kf, vf, kb, vb, sc, qn_rep = st2[s_i, h]
        if mxu_sums:
            sv = _dot(sc.astype(BF16), jnp.concatenate([vb, jnp.ones((L, M_DV), BF16)], axis=1))
            num = inter * _dot(qb, cm.astype(BF16)) + sv[:, :M_DV]
            qn = inter * qn_rep + sv[:, M_DV:]
            floor = jnp.exp(-m_t)
        else:
            num = inter * _dot(qb, cm.astype(BF16)) + _dot(sc.astype(BF16), vb)
            qn = (inter * jnp.sum(qb.astype(F32) * nrow, axis=-1, keepdims=True)
                  + jnp.sum(sc, axis=-1, keepdims=True))
            floor = jnp.exp(-m_t)
        st3[s_i, h] = num / jnp.maximum(jnp.abs(qn), floor)
    new_state = {}
    for s_i, h in heads:
        cm, nrow, m0 = state[s_i, h]
        fc, dm, m_t, inter = st1[s_i, h]
        qb, kf, vf, kb, vb, sc, _ = st2[s_i, h]
        ic = jnp.broadcast_to(gates[s_i][2][:, h:h + 1], (L, LANES))
        m_end = m_t[L - 1:L, :]
        f_end = fc[L - 1:L, :]
        decay = jnp.exp(m0 + f_end - m_end)
        w_s = jnp.exp(f_end - fc + ic - m_end)
        new_state[s_i, h] = (decay * cm + _dot_tn(kb, (w_s * vf).astype(BF16)),
                             decay * nrow + jnp.sum(w_s * kf, axis=0, keepdims=True),
                             m_end)
    for s_i, h in heads:
        lo = h * M_DK
        hn = _norm(st3[s_i, h]) * g_ref[:, lo:lo + M_DV]
        gate = _sigmoid(seq(o_ref, s_i)[:, lo:lo + M_DV].astype(F32))
        seq(h_ref, s_i)[:, lo:lo + M_DV] = (gate * hn).astype(BF16)
    for s_i, h in heads:
        c_out[s_i, h], n_out[s_i, h], m_out[s_i, h:h + 1, :] = new_state[s_i, h]
    if not carried:
        for s_i in seqs:
            m_out[s_i, M_HEADS:, :] = jnp.zeros((M_HEADS, LANES), F32)

    if carried:
        @pl.when(c == pl.num_programs(chunk_axis) - 1)
        def _():
            c1_ref[...] = c_out[...]
            n1_ref[...] = n_out[...]
            m1_ref[...] = m_out[...]


def _mlstm_state_specs(nseq, l_state):
    if l_state is None:
        c_spec = pl.BlockSpec((nseq, M_HEADS, M_DK, M_DV), lambda b, c: (b, 0, 0, 0))
    else:
        c_spec = pl.BlockSpec((None, nseq, M_HEADS, M_DK, M_DV), lambda b, c: (l_state, b, 0, 0, 0))
    return (c_spec,
            pl.BlockSpec((nseq, M_HEADS, 1, M_DK), lambda b, c: (b, 0, 0, 0)),
            pl.BlockSpec((nseq, 2 * M_HEADS, LANES), lambda b, c: (b, 0, 0)))


def _mlstm_scratch(nseq):
    return [pltpu.VMEM((nseq, M_HEADS, M_DK, M_DV), F32),
            pltpu.VMEM((nseq, M_HEADS, 1, M_DK), F32),
            pltpu.VMEM((nseq, 2 * M_HEADS, LANES), F32)]


def _mlstm_prompt(z, zif, ifr, gamma, l, bp, nc, L):
    hw = M_HEADS * M_DK
    zero = lambda *s: jnp.zeros(s, F32)
    in_specs, args = [], []
    for b in range(bp):
        def rows(col, b=b):
            return lambda c: (b * nc + c, col)

        in_specs += [pl.BlockSpec((L, LANES), rows(0)),
                     pl.BlockSpec((1, 2 * M_HEADS, L), lambda c, b=b: (b * nc + c, 0, 0)),
                     pl.BlockSpec((L, hw), rows(Z_MQ // hw)),
                     pl.BlockSpec((L, hw), rows(Z_MK // hw)),
                     pl.BlockSpec((L, hw), rows(Z_MV // hw)),
                     pl.BlockSpec((L, hw), rows(Z_MO // hw))]
        args += [zif, ifr, z, z, z, z]
    state_specs = [pl.BlockSpec((bp, M_HEADS, M_DK, M_DV), lambda c: (0, 0, 0, 0)),
                   pl.BlockSpec((bp, M_HEADS, 1, M_DK), lambda c: (0, 0, 0, 0)),
                   pl.BlockSpec((bp, 2 * M_HEADS, LANES), lambda c: (0, 0, 0))]
    kern = functools.partial(_mlstm_kernel, t_valid=L, nseq=bp, aliased=False, carried=True, chunk_axis=0,
                             per_seq_inputs=True)
    outs = pl.pallas_call(
        kern,
        grid=(nc,),
        in_specs=in_specs + [pl.BlockSpec((None, 1, hw), lambda c: (l, 0, 0))] + state_specs,
        out_specs=[pl.BlockSpec((bp, L, hw), lambda c: (0, c, 0))] + state_specs,
        out_shape=[jax.ShapeDtypeStruct((bp, nc * L, hw), BF16),
                   jax.ShapeDtypeStruct((bp, M_HEADS, M_DK, M_DV), F32),
                   jax.ShapeDtypeStruct((bp, M_HEADS, 1, M_DK), F32),
                   jax.ShapeDtypeStruct((bp, 2 * M_HEADS, LANES), F32)],
        scratch_shapes=_mlstm_scratch(bp),
        compiler_params=_cparams("arbitrary"),
        name="mlstm_prompt",
    )(*args, gamma, zero(bp, M_HEADS, M_DK, M_DV), zero(bp, M_HEADS, 1, M_DK), zero(bp, 2 * M_HEADS, LANES))
    return (outs[0].reshape(bp * nc * L, hw),) + tuple(outs[1:])


def _mlstm_sample(zm3, if3, ifr, gamma, c_all, n0, m0x, c_out_prev, l, t_valid, nseq):
    bs, L, _ = zm3.shape
    depth = c_all.shape[0]
    hw = M_HEADS * M_DK
    aliased = c_out_prev is not None

    def blk(col):
        return pl.BlockSpec((nseq, L, hw), lambda b, c: (b, 0, col))

    c_in, n_spec, m_spec = _mlstm_state_specs(nseq, l)
    in_specs = [pl.BlockSpec((nseq, L, LANES), lambda b, c: (b, 0, 0)),
                pl.BlockSpec((nseq, 2 * M_HEADS, L), lambda b, c: (b, 0, 0)),
                blk(0), blk(1), blk(2), blk(3),
                pl.BlockSpec((None, 1, hw), lambda b, c: (l, 0, 0)),
                c_in, n_spec, m_spec]
    args = [if3, ifr, zm3, zm3, zm3, zm3, gamma, c_all, n0, m0x]
    aliases = {}
    if aliased:
        in_specs.append(pl.BlockSpec(memory_space=pl.ANY))
        args.append(c_out_prev)
        aliases = {len(args) - 1: 1}
        c_out = c_in
    else:
        c_out = pl.BlockSpec((depth, nseq, M_HEADS, M_DK, M_DV), lambda b, c: (0, b, 0, 0, 0))
    kern = functools.partial(_mlstm_kernel, t_valid=t_valid, nseq=nseq, aliased=aliased, carried=False,
                             chunk_axis=1, first_layer=None if aliased else l)
    return pl.pallas_call(
        kern,
        grid=(bs // nseq, 1),
        in_specs=in_specs,
        out_specs=[pl.BlockSpec((nseq, L, hw), lambda b, c: (b, 0, 0)), c_out, n_spec, m_spec],
        out_shape=[jax.ShapeDtypeStruct((bs, L, hw), BF16),
                   jax.ShapeDtypeStruct((depth, bs, M_HEADS, M_DK, M_DV), F32),
                   jax.ShapeDtypeStruct((bs, M_HEADS, 1, M_DK), F32),
                   jax.ShapeDtypeStruct((bs, 2 * M_HEADS, LANES), F32)],
        input_output_aliases=aliases,
        compiler_params=_cparams("parallel", "arbitrary"),
        name="mlstm_sample",
    )(*args)


def _m_state_in(m):
    lead = m.shape[:-1]
    mx = jnp.zeros(lead + (2 * M_HEADS, LANES), F32)
    return mx.at[..., :M_HEADS, :].set(jnp.broadcast_to(m[..., None], lead + (M_HEADS, LANES)))


def _merge_kernel(*refs, np_tiles, alpha, route):
    (cp_ref, cs_ref, ap_ref, as_ref, mp_ref, ms_ref, g0_ref, g1_ref, g2_ref, xp_ref, xs_ref,
     gtp_ref, gts_ref, shp_ref, shs_ref, scp_ref, scs_ref,
     wc_ref, wa_ref, wm_ref, wo_ref, pg_ref, pb_ref) = refs[:23]
    if route:
        rw_ref, rb_ref, x1_ref, h_ref, route_ref = refs[23:]
    else:
        x1_ref, h_ref = refs[23:]
    is_s = pl.program_id(0) >= np_tiles

    def gate(ref):
        return _sigmoid(ref[...].astype(F32))

    def run(c_ref, a_ref, m_ref, x_ref, gt_ref, sh_ref, sc_ref):
        y = (gate(g0_ref) * _dot(c_ref[...], wc_ref[...])
             + gate(g1_ref) * _dot(a_ref[...], wa_ref[...])
             + gate(g2_ref) * _dot(m_ref[...], wm_ref[...]))
        mix = _dot(y.astype(BF16), wo_ref[...])
        x1 = _norm(alpha * x_ref[...] + gt_ref[...] * mix) * pg_ref[...] + pb_ref[...]
        x1_ref[...] = x1
        h = _norm(x1) * (1.0 + sc_ref[...]) + sh_ref[...]
        h_ref[...] = h.astype(h_ref.dtype)
        if route:
            h_hi, h_lo = _split_bf16(h)
            w_hi, w_lo = _split_bf16(rw_ref[...])
            logits = _dot(h_hi, w_hi) + (_dot(h_hi, w_lo) + _dot(h_lo, w_hi)) + rb_ref[...]
            lane = lax.broadcasted_iota(jnp.int32, logits.shape, 1)
            m1 = jnp.max(logits, axis=-1, keepdims=True)
            e1 = jnp.min(jnp.where(logits == m1, lane, LANES), axis=-1, keepdims=True)
            l2 = jnp.where(lane == e1, NEG_INF, logits)
            m2 = jnp.max(l2, axis=-1, keepdims=True)
            e2 = jnp.min(jnp.where(l2 == m2, lane, LANES), axis=-1, keepdims=True)
            ex = jnp.exp(m2 - m1)
            w1 = 1.0 / (1.0 + ex)
            w2 = ex / (1.0 + ex)
            out = jnp.where(lane == 0, e1.astype(F32),
                            jnp.where(lane == 1, e2.astype(F32),
                                      jnp.where(lane == 2, w1, jnp.where(lane == 3, w2, 0.0))))
            route_ref[...] = out

    @pl.when(is_s)
    def _():
        run(cs_ref, as_ref, ms_ref, xs_ref, gts_ref, shs_ref, scs_ref)

    @pl.when(jnp.logical_not(is_s))
    def _():
        run(cp_ref, ap_ref, mp_ref, xp_ref, gtp_ref, shp_ref, scp_ref)


def _merge(cp, cs, ap, as_, mp, ms, z, xp, xs, mod_p, mod_s, wc, wa, wm, wo, pg, pb, l, rm, alpha, router):
    n, d = z.shape[0], xp.shape[1]
    hw = D_CONV
    route = router is not None
    tm = rm.t

    def pblk():
        return pl.BlockSpec((tm, hw), lambda i: (rm.prompt(i), 0))

    def sblk():
        return pl.BlockSpec((tm, hw), lambda i: (rm.sample(i), 0))

    def zg(k):
        return pl.BlockSpec((tm, d), lambda i: (i, Z_G // d + k))

    def lw(a):
        return pl.BlockSpec((None,) + a.shape[1:], lambda i: (l,) + (0,) * (a.ndim - 1))

    post = pl.BlockSpec((None, None, 1, d), lambda i: (l, 0, 0, 0))
    in_specs = [pblk(), sblk(), pblk(), sblk(), pblk(), sblk(), zg(0), zg(1), zg(2),
                *rm.x_specs(xs, d),
                *rm.mod_specs(l, 2, d), *rm.mod_specs(l, 3, d), *rm.mod_specs(l, 4, d),
                lw(wc), lw(wa), lw(wm), lw(wo), post, post]
    args = [cp, cs, ap, as_, mp, ms, z, z, z, xp, xs, mod_p, mod_s, mod_p, mod_s, mod_p, mod_s,
            wc, wa, wm, wo, pg, pb]
    out_specs = [pl.BlockSpec((tm, d), lambda i: (i, 0)), pl.BlockSpec((tm, d), lambda i: (i, 0))]
    out_shape = [jax.ShapeDtypeStruct((n, d), F32), jax.ShapeDtypeStruct((n, d), F32 if route else BF16)]
    if route:
        rw, rb, lj = router
        in_specs += [pl.BlockSpec((None, d, LANES), lambda i: (lj, 0, 0)),
                     pl.BlockSpec((None, 1, LANES), lambda i: (lj, 0, 0))]
        args += [rw, rb]
        out_specs.append(pl.BlockSpec((tm, LANES), lambda i: (i, 0)))
        out_shape.append(jax.ShapeDtypeStruct((n, LANES), F32))
    kern = functools.partial(_merge_kernel, np_tiles=rm.np_tiles, alpha=alpha, route=route)
    return pl.pallas_call(
        kern, grid=(n // tm,), in_specs=in_specs, out_specs=out_specs, out_shape=out_shape,
        compiler_params=_cparams("parallel"), name="merge_route" if route else "merge",
    )(*args)


def _post_residual(x_ref, gp_ref, gs_ref, f, pg_ref, pb_ref, is_s, alpha):
    return _norm(alpha * x_ref[...] + _pick(is_s, gp_ref, gs_ref) * f) * pg_ref[...] + pb_ref[...]


def _ffn_kernel(h_ref, wg_ref, wu_ref, wd_ref, x_ref, gp_ref, gs_ref, pg_ref, pb_ref, o_ref, acc,
                *, alpha, np_tiles):
    j = pl.program_id(1)
    h = h_ref[...]
    a = (_silu(_dot(h, wg_ref[...])) * _dot(h, wu_ref[...])).astype(BF16)
    part = _dot(a, wd_ref[...])

    @pl.when(j == 0)
    def _():
        acc[...] = part

    @pl.when(j > 0)
    def _():
        acc[...] = acc[...] + part

    @pl.when(j == pl.num_programs(1) - 1)
    def _():
        is_s = pl.program_id(0) >= np_tiles
        o_ref[...] = _post_residual(x_ref, gp_ref, gs_ref, acc[...], pg_ref, pb_ref, is_s, alpha)


def _ffn_dense(h, wg, wu, wd, x, mod_p, mod_s, pg, pb, l, lj, rm, tf, alpha):
    n, d = x.shape
    f = wg.shape[2]
    tm = rm.t
    post = pl.BlockSpec((None, None, 1, d), lambda i, j: (l, 1, 0, 0))
    return pl.pallas_call(
        functools.partial(_ffn_kernel, alpha=alpha, np_tiles=rm.np_tiles),
        grid=(n // tm, f // tf),
        in_specs=[pl.BlockSpec((tm, d), lambda i, j: (i, 0)),
                  pl.BlockSpec((None, d, tf), lambda i, j: (lj, 0, j)),
                  pl.BlockSpec((None, d, tf), lambda i, j: (lj, 0, j)),
                  pl.BlockSpec((None, tf, d), lambda i, j: (lj, j, 0)),
                  pl.BlockSpec((tm, d), lambda i, j: (i, 0)),
                  *rm.mod_specs(l, 5, d), post, post],
        out_specs=pl.BlockSpec((tm, d), lambda i, j: (i, 0)),
        out_shape=jax.ShapeDtypeStruct((n, d), F32),
        scratch_shapes=[pltpu.VMEM((tm, d), F32)],
        compiler_params=_cparams("parallel", "arbitrary"),
        name="ffn_dense",
    )(h, wg, wu, wd, x, mod_p, mod_s, pg, pb)


def _rank_kernel(route_ref, rank_ref, tot_ref, carry):
    i = pl.program_id(0)
    tm = route_ref.shape[0]

    @pl.when(i == 0)
    def _():
        carry[...] = jnp.zeros_like(carry)

    r = route_ref[...]
    lane = lax.broadcasted_iota(jnp.int32, (tm, LANES), 1)
    e1 = r[:, 0:1].astype(jnp.int32)
    e2 = r[:, 1:2].astype(jnp.int32)
    hit1 = lane == e1
    hit2 = lane == e2
    onehot = jnp.where(jnp.logical_or(hit1, hit2), 1.0, 0.0)
    tt = lax.broadcasted_iota(jnp.int32, (tm, tm), 0)
    ss = lax.broadcasted_iota(jnp.int32, (tm, tm), 1)
    before = jnp.where(ss < tt, 1.0, 0.0).astype(BF16)
    cnt = _dot(before, onehot.astype(BF16)) + carry[0:1, :]
    r1 = jnp.sum(jnp.where(hit1, cnt, 0.0), axis=-1, keepdims=True)
    r2 = jnp.sum(jnp.where(hit2, cnt, 0.0), axis=-1, keepdims=True)
    rank_ref[...] = jnp.where(lane == 0, r1, jnp.where(lane == 1, r2, 0.0))
    carry[...] = carry[...] + jnp.sum(onehot, axis=0, keepdims=True)
    tot_ref[...] = carry[...]


def _moe_rank(route, tm):
    n = route.shape[0]
    return pl.pallas_call(
        _rank_kernel,
        grid=(n // tm,),
        in_specs=[pl.BlockSpec((tm, LANES), lambda i: (i, 0))],
        out_specs=[pl.BlockSpec((tm, LANES), lambda i: (i, 0)),
                   pl.BlockSpec((8, LANES), lambda i: (0, 0))],
        out_shape=[jax.ShapeDtypeStruct((n, LANES), F32), jax.ShapeDtypeStruct((8, LANES), F32)],
        scratch_shapes=[pltpu.VMEM((8, LANES), F32)],
        compiler_params=_cparams("arbitrary"),
        name="moe_rank",
    )(route)


def _row_copy(src, s, dst, t, sem):
    return pltpu.make_async_copy(src.at[pl.ds(s, 1)], dst.at[pl.ds(t, 1)], sem)


def _dispatch_kernel(d1_ref, d2_ref, zl_ref, h_ref, xs_hbm, zbuf, sem, zsem):
    i = pl.program_id(0)
    tm = h_ref.shape[0]

    @pl.when(i == 0)
    def _():
        zbuf[...] = jnp.zeros_like(zbuf)

        def zero_copy(k):
            row = pl.multiple_of(jnp.maximum(zl_ref[k], 0), MOE_SUB)
            return pltpu.make_async_copy(zbuf, xs_hbm.at[pl.ds(row, MOE_SUB)], zsem)

        def start(k, carry):
            @pl.when(zl_ref[k] >= 0)
            def _():
                zero_copy(k).start()
            return carry

        def wait(k, carry):
            @pl.when(zl_ref[k] >= 0)
            def _():
                zero_copy(k).wait()
            return carry

        lax.fori_loop(0, zl_ref.shape[0], start, 0)
        lax.fori_loop(0, zl_ref.shape[0], wait, 0)

    def issue(r, carry):
        _row_copy(h_ref, r, xs_hbm, d1_ref[i * tm + r], sem).start(priority=0)
        _row_copy(h_ref, r, xs_hbm, d2_ref[i * tm + r], sem).start(priority=1)
        return carry

    lax.fori_loop(0, tm, issue, 0, unroll=8)
    for _ in range(2):
        pltpu.make_async_copy(h_ref, xs_hbm.at[pl.ds(0, tm)], sem).wait()


def _moe_dispatch(dest1, dest2, zlist, h, n_rows, tm):
    n, d = h.shape
    grid_spec = pltpu.PrefetchScalarGridSpec(
        num_scalar_prefetch=3,
        grid=(n // tm,),
        in_specs=[pl.BlockSpec((tm, d), lambda i, *_: (i, 0))],
        out_specs=pl.BlockSpec(memory_space=pl.ANY),
        scratch_shapes=[pltpu.VMEM((MOE_SUB, d), F32), pltpu.SemaphoreType.DMA(()), pltpu.SemaphoreType.DMA(())],
    )
    return pl.pallas_call(
        _dispatch_kernel, grid_spec=grid_spec,
        out_shape=jax.ShapeDtypeStruct((n_rows, d), F32),
        compiler_params=_cparams("arbitrary"),
        name="moe_dispatch",
    )(dest1, dest2, zlist, h)


def _expert_kernel(blk_e_ref, nvalid_ref, xs_ref, wg_ref, wu_ref, wd_ref, y_ref, xb, acc):
    b = pl.program_id(0)
    j = pl.program_id(1)
    tb = xb.shape[0]
    nsub = tb // MOE_SUB
    nv = nvalid_ref[b]
    used = (nv + MOE_SUB - 1) // MOE_SUB

    def run(rows):
        wg = wg_ref[...].astype(BF16)
        wu = wu_ref[...].astype(BF16)
        wd = wd_ref[...].astype(BF16)

        @pl.when(j == 0)
        def _():
            xb[rows, :] = xs_ref[rows, :].astype(BF16)

        x = xb[rows, :]
        a = (_silu(_dot(x, wg)) * _dot(x, wu)).astype(BF16)
        part = _dot(a, wd)

        @pl.when(j == 0)
        def _():
            acc[rows, :] = part

        @pl.when(j > 0)
        def _():
            acc[rows, :] = acc[rows, :] + part

        @pl.when(j == pl.num_programs(1) - 1)
        def _():
            y_ref[rows, :] = acc[rows, :]

    for m in range(1, nsub + 1):
        @pl.when(used == m)
        def _():
            run(pl.ds(0, m * MOE_SUB))

    for s in range(nsub):
        @pl.when(jnp.logical_and(s >= used, j == pl.num_programs(1) - 1))
        def _():
            y_ref[pl.ds(s * MOE_SUB, MOE_SUB), :] = jnp.zeros((MOE_SUB, y_ref.shape[1]), F32)


def _moe_experts(blk_e, nvalid, xs, wg, wu, wd, lj, tb, tf):
    n_rows, d = xs.shape
    f = wg.shape[3]
    nj = f // tf

    def jm(b, j, nv):
        return jnp.where(nv[b] > 0, j, nj - 1)

    grid_spec = pltpu.PrefetchScalarGridSpec(
        num_scalar_prefetch=2,
        grid=(n_rows // tb, nj),
        in_specs=[pl.BlockSpec((tb, d), lambda b, j, be, nv: (b, 0)),
                  pl.BlockSpec((None, None, d, tf), lambda b, j, be, nv: (lj, be[b], 0, jm(b, j, nv))),
                  pl.BlockSpec((None, None, d, tf), lambda b, j, be, nv: (lj, be[b], 0, jm(b, j, nv))),
                  pl.BlockSpec((None, None, tf, d), lambda b, j, be, nv: (lj, be[b], jm(b, j, nv), 0))],
        out_specs=pl.BlockSpec((tb, d), lambda b, j, be, nv: (b, 0)),
        scratch_shapes=[pltpu.VMEM((tb, d), BF16), pltpu.VMEM((tb, d), F32)],
    )
    return pl.pallas_call(
        _expert_kernel, grid_spec=grid_spec,
        out_shape=jax.ShapeDtypeStruct((n_rows, d), F32),
        compiler_params=_cparams("parallel", "arbitrary"),
        name="moe_experts",
    )(blk_e, nvalid, xs, wg, wu, wd)


def _combine_kernel(d1_ref, d2_ref, ys_hbm, route_ref, x_ref, gp_ref, gs_ref, pg_ref, pb_ref, *rest,
                    alpha, np_tiles, split):
    (*outs, y1, y2, sem1, sem2) = rest
    i = pl.program_id(0)
    tm = x_ref.shape[0]

    def issue(r, carry):
        _row_copy(ys_hbm, d1_ref[i * tm + r], y1, r, sem1).start(priority=0)
        _row_copy(ys_hbm, d2_ref[i * tm + r], y2, r, sem2).start(priority=1)
        return carry

    lax.fori_loop(0, tm, issue, 0, unroll=8)
    pltpu.make_async_copy(ys_hbm.at[pl.ds(0, tm)], y1, sem1).wait()
    pltpu.make_async_copy(ys_hbm.at[pl.ds(0, tm)], y2, sem2).wait()
    r = route_ref[...]
    f = y1[...] * r[:, 2:3] + y2[...] * r[:, 3:4]
    out = _post_residual(x_ref, gp_ref, gs_ref, f, pg_ref, pb_ref, i >= np_tiles, alpha)
    if split:
        op_ref, os_ref = outs

        @pl.when(i < np_tiles)
        def _():
            op_ref[...] = out

        @pl.when(i >= np_tiles)
        def _():
            os_ref[...] = out
    else:
        outs[0][...] = out


def _moe_combine(dest1, dest2, ys, route, x, mod_p, mod_s, pg, pb, l, rm, alpha, split):
    n, d = x.shape
    tm = rm.t
    if split:
        n_p = rm.np_tiles * tm
        out_specs = [pl.BlockSpec((tm, d), lambda i, *_: (rm.prompt(i), 0)),
                     pl.BlockSpec((tm, d), lambda i, *_: (rm.sample(i), 0))]
        out_shape = [jax.ShapeDtypeStruct((n_p, d), F32), jax.ShapeDtypeStruct((n - n_p, d), F32)]
    else:
        out_specs = pl.BlockSpec((tm, d), lambda i, *_: (i, 0))
        out_shape = jax.ShapeDtypeStruct((n, d), F32)
    post = pl.BlockSpec((None, None, 1, d), lambda i, *_: (l, 1, 0, 0))
    grid_spec = pltpu.PrefetchScalarGridSpec(
        num_scalar_prefetch=2,
        grid=(n // tm,),
        in_specs=[pl.BlockSpec(memory_space=pl.ANY),
                  pl.BlockSpec((tm, LANES), lambda i, *_: (i, 0)),
                  pl.BlockSpec((tm, d), lambda i, *_: (i, 0)),
                  *rm.mod_specs(l, 5, d), post, post],
        out_specs=out_specs,
        scratch_shapes=[pltpu.VMEM((tm, d), F32), pltpu.VMEM((tm, d), F32),
                        pltpu.SemaphoreType.DMA(()), pltpu.SemaphoreType.DMA(())],
    )
    return pl.pallas_call(
        functools.partial(_combine_kernel, alpha=alpha, np_tiles=rm.np_tiles, split=split), grid_spec=grid_spec,
        out_shape=out_shape,
        compiler_params=_cparams("arbitrary"),
        name="moe_combine",
    )(dest1, dest2, ys, route, x, mod_p, mod_s, pg, pb)


def _moe_ffn(h, route, wg, wu, wd, x, mod_p, mod_s, pg, pb, l, lj, rm, tb, tf, alpha, split):
    n = h.shape[0]
    rank, tot = _moe_rank(route, rm.t)
    counts = tot[0, :N_EXPERTS].astype(jnp.int32)
    padded = (counts + tb - 1) // tb * tb
    pad_end = jnp.cumsum(padded)
    pad_start = pad_end - padded
    experts = jnp.arange(N_EXPERTS, dtype=jnp.int32)

    def slot(col):
        e = route[:, col].astype(jnp.int32)
        start = jnp.sum(jnp.where(e[:, None] == experts[None, :], pad_start[None, :], 0), axis=1)
        return start + rank[:, col].astype(jnp.int32)

    dest1, dest2 = slot(0), slot(1)
    n_blocks = -(-(2 * n + N_EXPERTS * (tb - 1)) // tb)
    blk_start = jnp.arange(n_blocks, dtype=jnp.int32) * tb
    blk_e = jnp.minimum(jnp.sum(pad_end[None, :] <= blk_start[:, None], axis=1), N_EXPERTS - 1).astype(jnp.int32)
    own_end = jnp.sum(jnp.where(blk_e[:, None] == experts[None, :], (pad_start + counts)[None, :], 0), axis=1)
    nvalid = jnp.clip(own_end - blk_start, 0, tb).astype(jnp.int32)
    nvalid = jnp.where(blk_start < pad_end[-1], nvalid, 0)
    per_blk = tb // MOE_SUB
    sub_start = jnp.arange(n_blocks * per_blk, dtype=jnp.int32) * MOE_SUB
    sub_room = jnp.repeat(blk_start + nvalid, per_blk) - sub_start
    zlist = jnp.where(sub_room < MOE_SUB, sub_start, -1).astype(jnp.int32)
    xs = _moe_dispatch(dest1, dest2, zlist, h, n_blocks * tb, rm.t)
    ys = _moe_experts(blk_e, nvalid, xs, wg, wu, wd, lj, tb, tf)
    return _moe_combine(dest1, dest2, ys, route, x, mod_p, mod_s, pg, pb, l, rm, alpha, split)


def _pack_w_in(w_in, b_in):
    depth, d, _ = w_in.shape
    q_end = 2 * D_CONV + N_HEADS * HEAD_DIM
    k_end = q_end + N_KV * HEAD_DIM
    a_end = k_end + N_KV * HEAD_DIM
    m_end = a_end + 4 * M_HEADS * M_DK
    if_end = m_end + 2 * M_HEADS

    def pack(a):
        return jnp.concatenate([a[..., if_end:], a[..., :q_end], a[..., a_end:m_end], a[..., q_end:a_end]], axis=-1)

    def gates(a):
        return jnp.pad(a[..., m_end:if_end], [(0, 0)] * (a.ndim - 1) + [(0, LANES - 2 * M_HEADS)])

    return (pack(w_in).astype(BF16), pack(b_in).reshape(depth, 1, Z_W),
            gates(w_in).astype(BF16), gates(b_in).reshape(depth, 1, LANES))


def kernel(x_prompt, x_sample, cache_swa_k, cache_swa_v, state_conv, state_mlstm_C, state_mlstm_n, state_mlstm_m, c_prompt, c_sample, w_ada, b_ada, w_in, b_in, conv_w, conv_b, conv_ln_g, conv_ln_b, w_conv_out, attn_sinks, rel_bias, w_attn_out, m_norm_g, w_m_out, w_out, post_ln_g, post_ln_b, ffn_w_gate, ffn_w_up, ffn_w_down, router_w, router_b, moe_w_gate, moe_w_up, moe_w_down):
    bp, tp, d = x_prompt.shape
    bs, ts, _ = x_sample.shape
    depth = w_ada.shape[0]
    alpha = (2 * depth) ** 0.25
    n_p, n_s = bp * tp, bs * ts
    tm = n_s
    assert d == D_MODEL and tp % tm == 0 and tp % WINDOW == 0 and tm % 32 == 0
    rm = _RowMap(tm, bp, tp, n_p, n_s)
    rm_half = _RowMap(tm // 2, bp, tp, n_p, n_s)
    wb = cache_swa_k.shape[2]
    big = n_p >= 4096
    tc = 512 if big else tm
    lm = 256 if big else min(tp, 128)
    tb = 1024 if big else 2 * MOE_SUB
    bs_blk = 32 if bs % 32 == 0 else bs
    bs_att = 16 if bs % 16 == 0 else bs
    bs_m = 4 if bs % 4 == 0 else 1
    lts = 16

    n = n_p + n_s
    x = (x_prompt.reshape(n_p, d), jnp.transpose(x_sample, (1, 0, 2)).reshape(n_s, d))

    nc_rows = -(-(bp + bs) // 8) * 8
    c_all = jnp.zeros((nc_rows, d), F32).at[:bp].set(c_prompt).at[bp:bp + bs].set(c_sample)
    mod = _ada_mod(c_all, w_ada, b_ada)
    mod_p = mod[:, :bp].reshape(depth, bp, 1, 6 * d)
    mod_s = jnp.tile(mod[:, bp:bp + bs], (1, ts, 1))

    w_in_p, b_in_p, w_if, b_if = _pack_w_in(w_in, b_in)
    wc_b, wa_b, wm_b, wo_b = (w.astype(BF16) for w in (w_conv_out, w_attn_out, w_m_out, w_out))
    fg_b, fu_b, fd_b = (w.astype(BF16) for w in (ffn_w_gate, ffn_w_up, ffn_w_down))
    cw_pad = jnp.pad(conv_w, ((0, 0), (0, CONV_PAD - CONV_W), (0, 0)))
    cvecs = [v.reshape(depth, 1, D_CONV) for v in (conv_b, conv_ln_g, conv_ln_b)]
    rw_pad = jnp.pad(router_w, ((0, 0), (0, 0), (0, LANES - N_EXPERTS)))
    rb_pad = jnp.pad(router_b, ((0, 0), (0, LANES - N_EXPERTS)), constant_values=NEG_INF)[:, None, :]
    pg = post_ln_g.reshape(depth, 2, 1, d)
    pb = post_ln_b.reshape(depth, 2, 1, d)
    gamma = m_norm_g.reshape(depth, 1, M_HEADS * M_DV)
    sinks = attn_sinks.astype(F32)

    qi = jnp.arange(WINDOW)[:, None]
    kj = jnp.arange(2 * WINDOW)[None, :]
    dist_p = qi + WINDOW - kj
    bh = _bias_heads(rel_bias, dist_p, (dist_p >= 0) & (dist_p < WINDOW))
    bias_p = bh.reshape(N_HEADS // 2, 2, WINDOW, 2 * WINDOW).transpose(0, 2, 1, 3).reshape(
        N_HEADS // 2, WINDOW, 4 * WINDOW)
    dist_s = jnp.arange(ts)[:, None] + wb - jnp.arange(wb + ts)[None, :]
    bias_s = _bias_heads(rel_bias, dist_s, (dist_s >= 0) & (dist_s < WINDOW)).reshape(
        N_KV, Q_PER_KV * ts, wb + ts)
    kc_all = cache_swa_k.reshape(depth, bs, wb, N_KV * HEAD_DIM)
    vc_all = cache_swa_v.reshape(depth, bs, wb, N_KV * HEAD_DIM)

    f_dense = ffn_w_gate.shape[2]
    tf_dense = f_dense // 2 if (f_dense // 2) % LANES == 0 else f_dense
    f_moe = moe_w_gate.shape[3]
    tf_moe = 512 if f_moe % 512 == 0 else f_moe

    n0_all = state_mlstm_n[:, :, :, None, :]
    m0_all = _m_state_in(state_mlstm_m)
    s_c = None
    new_p = [[] for _ in range(6)]
    new_s = [[] for _ in range(6)]
    for l in range(depth):
        j = l // 2
        xp, xs = x if isinstance(x, tuple) else (x, x)
        z, zif = _ln_proj(xp, xs, mod_p, mod_s, w_in_p, b_in_p, w_if, b_if, l, rm, n)
        zs3 = z[n_p:].reshape(ts, bs, Z_W)

        cp, ns_p = _conv_prompt(z, jnp.zeros((bp, CONV_PAD, D_CONV), F32), cw_pad, *cvecs, l, bp, tp, tc)
        cs3, a_s3 = _conv_sample(zs3, state_conv, cw_pad, *cvecs, l, bs_blk)
        new_p[2].append(ns_p[:, CONV_PAD - CONV_W + 1:])
        new_s[2].append(jnp.transpose(a_s3, (1, 0, 2)))

        sink_h = sinks[l].reshape(N_KV, Q_PER_KV, 1)
        sink_s = jnp.broadcast_to(sink_h, (N_KV, Q_PER_KV, ts)).reshape(N_KV, Q_PER_KV * ts, 1)
        ap = _attn_prompt(z, bias_p, sinks, l, bp, tp)
        nk = min(WINDOW, tp)
        kv_tail = jnp.stack([z[(b + 1) * tp - nk:(b + 1) * tp, Z_K:Z_K + 2 * N_KV * HEAD_DIM]
                             for b in range(bp)]).astype(F32)
        new_p[0].append(kv_tail[..., :N_KV * HEAD_DIM].reshape(bp, nk, N_KV, HEAD_DIM))
        new_p[1].append(kv_tail[..., N_KV * HEAD_DIM:].reshape(bp, nk, N_KV, HEAD_DIM))
        q_s = zs3[:, :, Z_Q:Z_Q + N_HEADS * HEAD_DIM].reshape(ts, bs, N_KV, Q_PER_KV, HEAD_DIM)
        q4 = jnp.transpose(q_s, (1, 2, 3, 0, 4)).reshape(bs, N_KV, Q_PER_KV * ts, HEAD_DIM)
        k_s = jnp.transpose(zs3[:, :, Z_K:Z_K + N_KV * HEAD_DIM].reshape(ts, bs, N_KV, HEAD_DIM), (1, 0, 2, 3))
        v_s = jnp.transpose(zs3[:, :, Z_V:Z_V + N_KV * HEAD_DIM].reshape(ts, bs, N_KV, HEAD_DIM), (1, 0, 2, 3))
        o4 = _attn_sample(q4, kc_all, vc_all, jnp.transpose(k_s, (0, 2, 1, 3)), jnp.transpose(v_s, (0, 2, 1, 3)),
                          bias_s, sink_s, l, bs_att)
        as_ = jnp.transpose(o4.reshape(bs, N_KV, Q_PER_KV, ts, HEAD_DIM), (3, 0, 1, 2, 4)).reshape(n_s, -1).astype(BF16)
        new_s[0].append(k_s.astype(F32))
        new_s[1].append(v_s.astype(F32))

        ncp = tp // lm
        if_p = zif[:n_p, :2 * M_HEADS].reshape(bp * ncp, lm, 2 * M_HEADS)
        mp, c1p, n1p, m1p = _mlstm_prompt(z, zif, jnp.transpose(if_p, (0, 2, 1)), gamma, l, bp, ncp, lm)
        new_p[3].append(c1p)
        new_p[4].append(n1p[:, :, 0])
        new_p[5].append(m1p[:, :M_HEADS, 0])
        tpad = ((0, 0), (0, lts - ts), (0, 0))
        zm3 = jnp.pad(jnp.transpose(zs3[:, :, Z_MQ:Z_K], (1, 0, 2)), tpad)
        if3 = jnp.pad(jnp.transpose(zif[n_p:].reshape(ts, bs, LANES), (1, 0, 2)), tpad)
        ms, s_c, n1s, m1s = _mlstm_sample(zm3, if3, jnp.transpose(if3[:, :, :2 * M_HEADS], (0, 2, 1)), gamma,
                                          state_mlstm_C, n0_all[l], m0_all[l], s_c, l, ts, bs_m)
        ms = jnp.transpose(ms[:, :ts], (1, 0, 2)).reshape(n_s, -1)
        new_s[4].append(n1s[:, :, 0])
        new_s[5].append(m1s[:, :M_HEADS, 0])

        moe = l % 2 == 1
        router = (rw_pad, rb_pad, j) if moe else None
        outs = _merge(cp, cs3.reshape(n_s, D_CONV), ap, as_, mp, ms, z, xp, xs, mod_p, mod_s,
                      wc_b, wa_b, wm_b, wo_b, pg, pb, l, rm_half, alpha, router)
        if moe:
            x1, h2, route = outs
            x = _moe_ffn(h2, route, moe_w_gate, moe_w_up, moe_w_down, x1, mod_p, mod_s, pg, pb,
                         l, j, rm, tb, tf_moe, alpha, split=l == depth - 1)
        else:
            x1, h2 = outs
            x = _ffn_dense(h2, fg_b, fu_b, fd_b, x1, mod_p, mod_s, pg, pb, l, j, rm, tf_dense, alpha)

    x_p, x_s = x if isinstance(x, (list, tuple)) else (x[:n_p], x[n_p:])
    y_p = x_p.reshape(bp, tp, d)
    y_s = jnp.transpose(x_s.reshape(ts, bs, d), (1, 0, 2))
    p_k, p_v, p_conv, p_c, p_n, p_m = [jnp.stack(a) for a in new_p]
    s_k, s_v, s_conv = [jnp.concatenate([old[:, :, ts:], jnp.stack(new)], axis=2)
                        for old, new in zip((cache_swa_k, cache_swa_v, state_conv), new_s[:3])]
    s_n, s_m = jnp.stack(new_s[4]), jnp.stack(new_s[5])
    return (y_p, y_s, p_k, p_v, p_conv, p_c, p_n, p_m, s_k, s_v, s_conv, s_c, s_n, s_m)
```

```python
import functools
import math

import jax
import jax.numpy as jnp
from jax import lax
from jax.experimental import pallas as pl
from jax.experimental.pallas import tpu as pltpu

F32 = jnp.float32
BF16 = jnp.bfloat16

D_MODEL = 1024
D_CONV = 512
CONV_W = 31
CONV_PAD = 32
N_HEADS = 8
N_KV = 2
HEAD_DIM = 64
Q_PER_KV = N_HEADS // N_KV
WINDOW = 128
N_BUCKETS = 32
MAX_DIST = 128
M_HEADS = 4
M_DK = 128
M_DV = 128
N_EXPERTS = 8
LN_EPS = 1e-5
LANES = 128
NEG_INF = float("-inf")
VMEM_LIMIT = 56 * 1024 * 1024

Z_G, Z_UA, Z_UB, Z_Q = 0, 3072, 3584, 4096
Z_MQ, Z_MK, Z_MV, Z_MO = 4608, 5120, 5632, 6144
Z_K, Z_V, Z_W = 6656, 6784, 6912
TN_IN = 3456
ATT_QB = 4
ATT_GROUP = 8
MOE_SUB = 256
CONV_CHUNK = 32

def _cparams(*sem):
    return pltpu.CompilerParams(dimension_semantics=sem, vmem_limit_bytes=VMEM_LIMIT)


def _sigmoid(x):
    return 1.0 / (1.0 + jnp.exp(-x))


def _silu(x):
    return x * _sigmoid(x)


def _log_sigmoid(x):
    return jnp.minimum(x, 0.0) - jnp.log(1.0 + jnp.exp(-jnp.abs(x)))


def _norm(x):
    mu = jnp.mean(x, axis=-1, keepdims=True)
    xc = x - mu
    var = jnp.mean(xc * xc, axis=-1, keepdims=True)
    return xc * lax.rsqrt(var + LN_EPS)


def _dot(a, b):
    return jnp.dot(a, b, preferred_element_type=F32)


def _dot_nt(a, b):
    return lax.dot_general(a, b, (((1,), (1,)), ((), ())), preferred_element_type=F32)


def _dot_tn(a, b):
    return lax.dot_general(a, b, (((0,), (0,)), ((), ())), preferred_element_type=F32)


def _dot_hi(a, b):
    return jnp.dot(a, b, preferred_element_type=F32, precision=lax.Precision.HIGHEST)


def _split_bf16(a):
    hi = a.astype(BF16)
    return hi, (a - hi.astype(F32)).astype(BF16)


class _RowMap:
    def __init__(self, t, bp, tp, n_p, n_s):
        self.t = t
        self.bp = bp
        self.per_seq = tp // t
        self.np_tiles = n_p // t
        self.ns_tiles = n_s // t

    def seq(self, i):
        return jnp.minimum(i // self.per_seq, self.bp - 1)

    def prompt(self, i):
        return jnp.minimum(i, self.np_tiles - 1)

    def sample(self, i):
        return jnp.clip(i - self.np_tiles, 0, self.ns_tiles - 1)

    def mod_specs(self, l, k, d):
        return (pl.BlockSpec((None, None, 1, d), lambda i, *_: (l, self.seq(i), 0, k)),
                pl.BlockSpec((None, self.t, d), lambda i, *_: (l, self.sample(i), k)))


    def x_specs(self, xs_arr, d):
        off = self.np_tiles if xs_arr.shape[0] > self.ns_tiles * self.t else 0
        return (pl.BlockSpec((self.t, d), lambda i, *_: (self.prompt(i), 0)),
                pl.BlockSpec((self.t, d), lambda i, *_: (off + self.sample(i), 0)))


def _pick(is_s, p_ref, s_ref):
    return jnp.where(is_s, s_ref[...], p_ref[...])


def _ada_kernel(c_ref, w_ref, b_ref, o_ref):
    s = _silu(c_ref[...]).astype(BF16)
    o_ref[0] = _dot(s, w_ref[0].astype(BF16)) + b_ref[0]


def _ada_mod(c_all, w_ada, b_ada):
    depth, d, n6 = w_ada.shape
    rows = c_all.shape[0]
    return pl.pallas_call(
        _ada_kernel,
        grid=(depth, n6 // d),
        in_specs=[pl.BlockSpec((rows, d), lambda l, j: (0, 0)),
                  pl.BlockSpec((1, d, d), lambda l, j: (l, 0, j)),
                  pl.BlockSpec((1, 1, d), lambda l, j: (l, 0, j))],
        out_specs=pl.BlockSpec((1, rows, d), lambda l, j: (l, 0, j)),
        out_shape=jax.ShapeDtypeStruct((depth, rows, n6), F32),
        compiler_params=_cparams("parallel", "parallel"),
        name="ada_mod",
    )(c_all, w_ada, b_ada.reshape(depth, 1, n6))


def _ln_proj_kernel(xp_ref, xs_ref, shp_ref, shs_ref, scp_ref, scs_ref, w_ref, b_ref, wif_ref, bif_ref,
                    z_ref, zif_ref, h_scr, *, np_tiles):
    is_s = pl.program_id(0) >= np_tiles
    first = pl.program_id(1) == 0

    def prologue(x_ref, sh_ref, sc_ref):
        h = (_norm(x_ref[...]) * (1.0 + sc_ref[...]) + sh_ref[...]).astype(BF16)
        h_scr[...] = h
        zif_ref[...] = _dot(h, wif_ref[...]) + bif_ref[...]

    @pl.when(jnp.logical_and(first, is_s))
    def _():
        prologue(xs_ref, shs_ref, scs_ref)

    @pl.when(jnp.logical_and(first, jnp.logical_not(is_s)))
    def _():
        prologue(xp_ref, shp_ref, scp_ref)

    z_ref[...] = (_dot(h_scr[...], w_ref[...]) + b_ref[...]).astype(BF16)


def _ln_proj(xp, xs, mod_p, mod_s, w, b, wif, bif, l, rm, n):
    d = xp.shape[1]
    zw = w.shape[2]
    tm = rm.t
    shp, shs = rm.mod_specs(l, 0, d)
    scp, scs = rm.mod_specs(l, 1, d)
    return pl.pallas_call(
        functools.partial(_ln_proj_kernel, np_tiles=rm.np_tiles),
        grid=(n // tm, zw // TN_IN),
        in_specs=[*rm.x_specs(xs, d),
                  shp, shs, scp, scs,
                  pl.BlockSpec((None, d, TN_IN), lambda i, j: (l, 0, j)),
                  pl.BlockSpec((None, 1, TN_IN), lambda i, j: (l, 0, j)),
                  pl.BlockSpec((None, d, LANES), lambda i, j: (l, 0, 0)),
                  pl.BlockSpec((None, 1, LANES), lambda i, j: (l, 0, 0))],
        out_specs=[pl.BlockSpec((tm, TN_IN), lambda i, j: (i, j)),
                   pl.BlockSpec((tm, LANES), lambda i, j: (i, 0))],
        out_shape=[jax.ShapeDtypeStruct((n, zw), BF16), jax.ShapeDtypeStruct((n, LANES), F32)],
        scratch_shapes=[pltpu.VMEM((tm, d), BF16)],
        compiler_params=_cparams("parallel", "arbitrary"),
        name="ln_proj",
    )(xp, xs, mod_p, mod_s, mod_p, mod_s, w, b, wif, bif)


def _conv_tail(yc, g_ref, b_ref):
    y = _norm(yc) * g_ref[...] + b_ref[...]
    return _silu(y).astype(BF16)


def _conv_prompt_kernel(ua_ref, ub_ref, st_ref, cw_ref, cb_ref, g_ref, b_ref, o_ref, ns_ref,
                        ext, shifted, yc, wrep):
    t = pl.program_id(1)
    tc = ua_ref.shape[0]
    sub = 8

    @pl.when(t == 0)
    def _():
        ext[0:CONV_PAD, :] = st_ref[0]

    @pl.when(t > 0)
    def _():
        ext[0:CONV_PAD, :] = ext[tc:tc + CONV_PAD, :]

    ext[CONV_PAD:, :] = ua_ref[...].astype(F32) * _sigmoid(ub_ref[...].astype(F32))
    for s in range(1, sub):
        shifted[s - 1] = ext[s:s + tc + CONV_PAD - sub, :]
    off = CONV_PAD - (CONV_W - 1)
    for w in range(CONV_W):
        wrep[w] = jnp.broadcast_to(cw_ref[w:w + 1, :], (sub, D_CONV))
    groups = CONV_CHUNK // sub

    for r0 in range(0, tc, CONV_CHUNK):
        acc = jnp.broadcast_to(cb_ref[...].reshape(1, 1, D_CONV), (groups, sub, D_CONV))
        for w in range(CONV_W):
            base, s = (off + w) // sub * sub, (off + w) % sub
            src = ext if s == 0 else shifted.at[s - 1]
            win = src[r0 + base:r0 + base + CONV_CHUNK, :]
            acc = acc + win.reshape(groups, sub, D_CONV) * wrep[w][None]
        yc[r0:r0 + CONV_CHUNK, :] = acc.reshape(CONV_CHUNK, D_CONV)
    o_ref[...] = _conv_tail(yc[...], g_ref, b_ref)

    @pl.when(t == pl.num_programs(1) - 1)
    def _():
        ns_ref[0] = ext[tc:tc + CONV_PAD, :]


def _conv_vec_specs(l, nargs):
    return [pl.BlockSpec((None, 1, D_CONV), lambda *_: (l, 0, 0)) for _ in range(nargs)]


def _conv_prompt(z, state_pad, cw, cb, g, b, l, bp, tp, tc):
    nt = tp // tc
    return pl.pallas_call(
        _conv_prompt_kernel,
        grid=(bp, nt),
        in_specs=[pl.BlockSpec((tc, D_CONV), lambda bb, t: (bb * nt + t, Z_UA // D_CONV)),
                  pl.BlockSpec((tc, D_CONV), lambda bb, t: (bb * nt + t, Z_UB // D_CONV)),
                  pl.BlockSpec((1, CONV_PAD, D_CONV), lambda bb, t: (bb, 0, 0)),
                  pl.BlockSpec((None, CONV_PAD, D_CONV), lambda bb, t: (l, 0, 0))] + _conv_vec_specs(l, 3),
        out_specs=[pl.BlockSpec((tc, D_CONV), lambda bb, t: (bb * nt + t, 0)),
                   pl.BlockSpec((1, CONV_PAD, D_CONV), lambda bb, t: (bb, 0, 0))],
        out_shape=[jax.ShapeDtypeStruct((bp * tp, D_CONV), BF16),
                   jax.ShapeDtypeStruct((bp, CONV_PAD, D_CONV), F32)],
        scratch_shapes=[pltpu.VMEM((tc + CONV_PAD, D_CONV), F32),
                        pltpu.VMEM((7, tc + CONV_PAD - 8, D_CONV), F32),
                        pltpu.VMEM((tc, D_CONV), F32),
                        pltpu.VMEM((CONV_PAD, 8, D_CONV), F32)],
        compiler_params=_cparams("parallel", "arbitrary"),
        name="conv_prompt",
    )(z, z, state_pad, cw, cb, g, b)


def _conv_sample_kernel(ua_ref, ub_ref, st_ref, cw_ref, cb_ref, g_ref, b_ref, o_ref, a_ref):
    ts = ua_ref.shape[0]
    ns = CONV_W - 1
    a = ua_ref[...].astype(F32) * _sigmoid(ub_ref[...].astype(F32))
    a_ref[...] = a
    st = st_ref[...]
    row = lax.broadcasted_iota(jnp.int32, (ns, D_CONV), 0)
    for t in range(ts):
        wt = jnp.zeros((ns, D_CONV), F32)
        for j in range(t, ns):
            wt = jnp.where(row == j, cw_ref[j - t:j - t + 1, :], wt)
        yc = jnp.sum(st * wt[None], axis=1) + cb_ref[...]
        for t2 in range(t + 1):
            wi = CONV_W - 1 - (t - t2)
            yc = yc + a[t2] * cw_ref[wi:wi + 1, :]
        o_ref[t] = _conv_tail(yc, g_ref, b_ref)


def _conv_sample(zs3, state, cw, cb, g, b, l, bs_blk):
    ts, bs, _ = zs3.shape
    ns = CONV_W - 1
    return pl.pallas_call(
        _conv_sample_kernel,
        grid=(bs // bs_blk,),
        in_specs=[pl.BlockSpec((ts, bs_blk, D_CONV), lambda i: (0, i, Z_UA // D_CONV)),
                  pl.BlockSpec((ts, bs_blk, D_CONV), lambda i: (0, i, Z_UB // D_CONV)),
                  pl.BlockSpec((None, bs_blk, ns, D_CONV), lambda i: (l, i, 0, 0)),
                  pl.BlockSpec((None, CONV_PAD, D_CONV), lambda i: (l, 0, 0))] + _conv_vec_specs(l, 3),
        out_specs=[pl.BlockSpec((ts, bs_blk, D_CONV), lambda i: (0, i, 0)),
                   pl.BlockSpec((ts, bs_blk, D_CONV), lambda i: (0, i, 0))],
        out_shape=[jax.ShapeDtypeStruct((ts, bs, D_CONV), BF16),
                   jax.ShapeDtypeStruct((ts, bs, D_CONV), F32)],
        compiler_params=_cparams("parallel"),
        name="conv_sample",
    )(zs3, zs3, state, cw, cb, g, b)


def _t5_bucket(dist):
    max_exact = N_BUCKETS // 2
    d = jnp.maximum(dist, 0)
    large = max_exact + (jnp.log(jnp.maximum(d, 1).astype(F32) / max_exact)
                         / math.log(MAX_DIST / max_exact) * (N_BUCKETS - max_exact)).astype(jnp.int32)
    return jnp.where(d < max_exact, d, jnp.minimum(large, N_BUCKETS - 1))


def _bias_heads(rel_bias, dist, valid):
    onehot = (_t5_bucket(dist)[..., None] == jnp.arange(N_BUCKETS)).astype(F32)
    bias = jnp.einsum("qkb,bh->qkh", onehot, rel_bias.astype(F32), precision=lax.Precision.HIGHEST)
    bias = jnp.where(valid[..., None], bias, NEG_INF)
    return jnp.transpose(bias, (2, 0, 1))


def _attn_prompt_kernel(sink_ref, q_ref, kc_ref, kp_ref, vc_ref, vp_ref, bias_ref, o_ref, *, l):
    first = pl.program_id(1) == 0
    w = WINDOW
    nq = q_ref.shape[0] // w
    kall = jnp.concatenate([kp_ref[...], kc_ref[...]], axis=0).astype(F32)
    vall = jnp.concatenate([vp_ref[...], vc_ref[...]], axis=0).astype(F32)
    lane = lax.broadcasted_iota(jnp.int32, kall.shape, 1)
    lo = lane < HEAD_DIM
    kroll = pltpu.roll(kall, HEAD_DIM, 1)
    vroll = pltpu.roll(vall, HEAD_DIM, 1)

    def halves(a, aroll, g):
        if g == 0:
            return jnp.where(lo, a, 0.0).astype(BF16), jnp.where(lo, 0.0, aroll).astype(BF16)
        return jnp.where(lo, aroll, 0.0).astype(BF16), jnp.where(lo, 0.0, a).astype(BF16)

    kh = [halves(kall, kroll, g) for g in range(N_KV)]
    vh = [halves(vall, vroll, g) for g in range(N_KV)]
    col = lax.broadcasted_iota(jnp.int32, (w, 4 * w), 1)
    prev_col = (col % (2 * w)) < w
    tiles_per_g = Q_PER_KV // 2
    units = [(qi, tile) for qi in range(nq) for tile in range(N_HEADS // 2)]
    for u0 in range(0, len(units), ATT_GROUP):
        group = units[u0:u0 + ATT_GROUP]
        scores = []
        for qi, tile in group:
            r0, g = qi * w, tile // tiles_per_g
            q = q_ref[r0:r0 + w, tile * LANES:(tile + 1) * LANES]
            kk = jnp.concatenate([kh[g][0][r0:r0 + 2 * w], kh[g][1][r0:r0 + 2 * w]], axis=0)
            s = _dot_nt(q, kk) * (HEAD_DIM ** -0.5) + bias_ref[tile]
            if qi == 0:
                s = jnp.where(jnp.logical_and(first, prev_col), NEG_INF, s)
            scores.append(s)
        probs = []
        for (qi, tile), s in zip(group, scores):
            ps = []
            for half in range(2):
                sh = s[:, half * 2 * w:(half + 1) * 2 * w]
                sink = sink_ref[l, 2 * tile + half]
                mx = jnp.maximum(jnp.max(sh, axis=-1, keepdims=True), sink)
                p = jnp.exp(sh - mx)
                den = jnp.sum(p, axis=-1, keepdims=True) + jnp.exp(sink - mx)
                ps.append((p * (1.0 / den)).astype(BF16))
            probs.append(jnp.concatenate(ps, axis=1))
        for (qi, tile), p in zip(group, probs):
            r0, g = qi * w, tile // tiles_per_g
            vv = jnp.concatenate([vh[g][0][r0:r0 + 2 * w], vh[g][1][r0:r0 + 2 * w]], axis=0)
            o_ref[r0:r0 + w, tile * LANES:(tile + 1) * LANES] = _dot(p, vv).astype(BF16)


def _attn_prompt(z, bias, sinks, l, bp, tp):
    w = WINDOW
    qb = ATT_QB if tp % (ATT_QB * w) == 0 else 1
    ns = tp // (qb * w)
    nb = tp // w
    kvw = N_KV * HEAD_DIM
    qw = N_HEADS * HEAD_DIM

    def cur(col):
        return lambda bb, i: (bb * ns + i, col)

    def prev(col):
        return lambda bb, i: (bb * nb + jnp.maximum(i * qb - 1, 0), col)

    return pl.pallas_call(
        functools.partial(_attn_prompt_kernel, l=l),
        grid=(bp, ns),
        in_specs=[pl.BlockSpec(memory_space=pltpu.SMEM),
                  pl.BlockSpec((qb * w, qw), cur(Z_Q // qw)),
                  pl.BlockSpec((qb * w, kvw), cur(Z_K // kvw)),
                  pl.BlockSpec((w, kvw), prev(Z_K // kvw)),
                  pl.BlockSpec((qb * w, kvw), cur(Z_V // kvw)),
                  pl.BlockSpec((w, kvw), prev(Z_V // kvw)),
                  pl.BlockSpec((N_HEADS // 2, w, 4 * w), lambda bb, i: (0, 0, 0))],
        out_specs=pl.BlockSpec((qb * w, qw), lambda bb, i: (bb * ns + i, 0)),
        out_shape=jax.ShapeDtypeStruct((bp * tp, qw), BF16),
        compiler_params=_cparams("parallel", "parallel"),
        name="attn_prompt",
    )(sinks, z, z, z, z, z, bias)


def _attn_sample_kernel(q_ref, kc_ref, vc_ref, kn_ref, vn_ref, bias_ref, sink_ref, o_ref):
    wb = kc_ref.shape[1]
    ts = kn_ref.shape[2]
    for g in range(N_KV):
        lo = g * HEAD_DIM
        qb = (q_ref[:, g].astype(F32) * (HEAD_DIM ** -0.5)).astype(BF16)
        kc = kc_ref[:, :, lo:lo + HEAD_DIM].astype(BF16)
        vc = vc_ref[:, :, lo:lo + HEAD_DIM].astype(BF16)
        kn = kn_ref[:, g].astype(F32)
        vn = vn_ref[:, g].astype(F32)
        bias = bias_ref[g]
        s_c = jnp.einsum("bqd,bkd->bqk", qb, kc, preferred_element_type=F32) + bias[None, :, :wb]
        qf = qb.astype(F32)
        s_n = [jnp.sum(qf * kn[:, j:j + 1, :], axis=-1, keepdims=True) + bias[None, :, wb + j:wb + j + 1]
               for j in range(ts)]
        sink = sink_ref[g][None]
        mx = jnp.maximum(jnp.max(s_c, axis=-1, keepdims=True), sink)
        for sj in s_n:
            mx = jnp.maximum(mx, sj)
        p_c = jnp.exp(s_c - mx)
        p_n = [jnp.exp(sj - mx) for sj in s_n]
        den = jnp.sum(p_c, axis=-1, keepdims=True) + jnp.exp(sink - mx)
        for pj in p_n:
            den = den + pj
        o = jnp.einsum("bqk,bkd->bqd", (p_c / den).astype(BF16), vc, preferred_element_type=F32)
        for j in range(ts):
            o = o + (p_n[j] / den).astype(BF16).astype(F32) * vn[:, j:j + 1, :]
        o_ref[:, g] = o


def _attn_sample(q4, kc, vc, kn, vn, bias, sinks, l, bs_blk):
    bs, _, rt, _ = q4.shape
    wb = kc.shape[2]
    ts = kn.shape[2]
    kvw = N_KV * HEAD_DIM
    return pl.pallas_call(
        _attn_sample_kernel,
        grid=(bs // bs_blk,),
        in_specs=[pl.BlockSpec((bs_blk, N_KV, rt, HEAD_DIM), lambda i: (i, 0, 0, 0)),
                  pl.BlockSpec((None, bs_blk, wb, kvw), lambda i: (l, i, 0, 0)),
                  pl.BlockSpec((None, bs_blk, wb, kvw), lambda i: (l, i, 0, 0)),
                  pl.BlockSpec((bs_blk, N_KV, ts, HEAD_DIM), lambda i: (i, 0, 0, 0)),
                  pl.BlockSpec((bs_blk, N_KV, ts, HEAD_DIM), lambda i: (i, 0, 0, 0)),
                  pl.BlockSpec((N_KV, rt, wb + ts), lambda i: (0, 0, 0)),
                  pl.BlockSpec((N_KV, rt, 1), lambda i: (0, 0, 0))],
        out_specs=pl.BlockSpec((bs_blk, N_KV, rt, HEAD_DIM), lambda i: (i, 0, 0, 0)),
        out_shape=jax.ShapeDtypeStruct((bs, N_KV, rt, HEAD_DIM), F32),
        compiler_params=_cparams("parallel"),
        name="attn_sample",
    )(q4, kc, vc, kn, vn, bias, sinks)


def _mlstm_kernel(*refs, t_valid, nseq, aliased, carried, chunk_axis, per_seq_inputs=False, first_layer=None):
    if aliased:
        refs = refs[:10] + refs[11:]
    if per_seq_inputs:
        groups = [refs[6 * s:6 * s + 6] for s in range(nseq)]
        if_ref, ifr_ref, q_ref, k_ref, v_ref, o_ref = (tuple(g[k] for g in groups) for k in range(6))
        refs = (None,) * 6 + refs[6 * nseq:]
    else:
        if_ref, ifr_ref, q_ref, k_ref, v_ref, o_ref = refs[:6]
    g_ref, c0_ref, n0_ref, m0_ref, h_ref, c1_ref, n1_ref, m1_ref = refs[6:14]
    c = pl.program_id(chunk_axis)
    L = (ifr_ref[0] if per_seq_inputs else ifr_ref).shape[-1]
    if carried:
        c_in, n_in, m_in = c_out, n_out, m_out = refs[14:]

        @pl.when(c == 0)
        def _():
            c_in[...] = c0_ref[...]
            n_in[...] = n0_ref[...]
            m_in[...] = m0_ref[...]
    else:
        (c_in, n_in, m_in), (c_out, n_out, m_out) = (c0_ref, n0_ref, m0_ref), (c1_ref, n1_ref, m1_ref)
        if first_layer is not None:
            for dd in range(c1_ref.shape[0]):
                if dd != first_layer:
                    c1_ref[dd] = jnp.zeros(c1_ref.shape[1:], F32)
            c_out = c1_ref.at[first_layer]

    def seq(ref, s):
        return ref[s] if isinstance(ref, tuple) else ref.at[s]

    tt = lax.broadcasted_iota(jnp.int32, (L, L), 0)
    ss = lax.broadcasted_iota(jnp.int32, (L, L), 1)
    causal = ss <= tt
    tril = causal.astype(F32)
    triu = (tt <= ss).astype(F32)
    seqs = range(nseq)
    heads = [(s_i, h) for s_i in seqs for h in range(M_HEADS)]
    mxu_sums = L % LANES == 0
    gates = []
    for s_i in seqs:
        ifc = seq(if_ref, s_i)[...]
        ifr = ifr_ref[s_i][0] if per_seq_inputs else ifr_ref[s_i]
        lf_c = _log_sigmoid(ifc)
        lf_r = _log_sigmoid(ifr)
        i_c, i_r = ifc, ifr
        if t_valid < L:
            rc = lax.broadcasted_iota(jnp.int32, (L, LANES), 0) < t_valid
            rr = lax.broadcasted_iota(jnp.int32, (2 * M_HEADS, L), 1) < t_valid
            lf_c = jnp.where(rc, lf_c, 0.0)
            lf_r = jnp.where(rr, lf_r, 0.0)
            i_c = jnp.where(rc, i_c, NEG_INF)
            i_r = jnp.where(rr, i_r, NEG_INF)
        gates.append((lf_c, lf_r, i_c, i_r))
    f_cs = [_dot_hi(tril, g[0]) for g in gates]
    f_rs = [_dot_hi(g[1], triu) for g in gates]
    state = {(s_i, h): (c_in[s_i, h], n_in[s_i, h], m_in[s_i, h:h + 1, :]) for s_i, h in heads}

    def wide(col):
        return jnp.concatenate([col] * (L // LANES), axis=1) if L >= LANES else col[:, :L]

    st1 = {}
    for s_i, h in heads:
        _, nrow, m0 = state[s_i, h]
        fc = jnp.broadcast_to(f_cs[s_i][:, M_HEADS + h:M_HEADS + h + 1], (L, LANES))
        fr = f_rs[s_i][M_HEADS + h:M_HEADS + h + 1, :]
        ir = gates[s_i][3][h:h + 1, :]
        dm = jnp.where(causal, wide(fc) - fr + ir, NEG_INF)
        m_t = jnp.maximum(m0 + fc, jnp.max(dm, axis=-1, keepdims=True))
        st1[s_i, h] = (fc, dm, m_t, jnp.exp(m0 + fc - m_t))
    st2 = {}
    for s_i, h in heads:
        lo = h * M_DK
        fc, dm, m_t, inter = st1[s_i, h]
        qb = seq(q_ref, s_i)[:, lo:lo + M_DK]
        kf = seq(k_ref, s_i)[:, lo:lo + M_DK].astype(F32) * (M_DK ** -0.5)
        vf = seq(v_ref, s_i)[:, lo:lo + M_DV].astype(F32)
        kb, vb = kf.astype(BF16), vf.astype(BF16)
        if mxu_sums:
            n_rows = jnp.broadcast_to(state[s_i, h][1], (M_DK, M_DK)).astype(BF16)
            qk = _dot_nt(qb, jnp.concatenate([kb, n_rows], axis=0))
            sc, qn_rep = qk[:, :L] * jnp.exp(dm - wide(m_t)), qk[:, L:]
        else:
            sc, qn_rep = _dot_nt(qb, kb) * jnp.exp(dm - wide(m_t)), None
        st2[s_i, h] = (qb, kf, vf, kb, vb, sc, qn_rep)
    st3 = {}
    for s_i, h in heads:
        cm, nrow, m0 = state[s_i, h]
        fc, dm, m_t, inter = st1[s_i, h]
        qb, kf, vf, kb, vb, sc, qn_rep = st2[s_i, h]
        if mxu_sums:
            sv = _dot(sc.astype(BF16), jnp.concatenate([vb, jnp.ones((L, M_DV), BF16)], axis=1))
            num = inter * _dot(qb, cm.astype(BF16)) + sv[:, :M_DV]
            qn = inter * qn_rep + sv[:, M_DV:]
            floor = jnp.exp(-m_t)
        else:
            num = inter * _dot(qb, cm.astype(BF16)) + _dot(sc.astype(BF16), vb)
            qn = (inter * jnp.sum(qb.astype(F32) * nrow, axis=-1, keepdims=True)
                  + jnp.sum(sc, axis=-1, keepdims=True))
            floor = jnp.exp(-m_t)
        st3[s_i, h] = num / jnp.maximum(jnp.abs(qn), floor)
    new_state = {}
    for s_i, h in heads:
        cm, nrow, m0 = state[s_i, h]
        fc, dm, m_t, inter = st1[s_i, h]
        qb, kf, vf, kb, vb, sc, _ = st2[s_i, h]
        ic = jnp.broadcast_to(gates[s_i][2][:, h:h + 1], (L, LANES))
        m_end = m_t[L - 1:L, :]
        f_end = fc[L - 1:L, :]
        decay = jnp.exp(m0 + f_end - m_end)
        w_s = jnp.exp(f_end - fc + ic - m_end)
        new_state[s_i, h] = (decay * cm + _dot_tn(kb, (w_s * vf).astype(BF16)),
                             decay * nrow + jnp.sum(w_s * kf, axis=0, keepdims=True),
                             m_end)
    for s_i, h in heads:
        lo = h * M_DK
        hn = _norm(st3[s_i, h]) * g_ref[:, lo:lo + M_DV]
        gate = _sigmoid(seq(o_ref, s_i)[:, lo:lo + M_DV].astype(F32))
        seq(h_ref, s_i)[:, lo:lo + M_DV] = (gate * hn).astype(BF16)
    for s_i, h in heads:
        c_out[s_i, h], n_out[s_i, h], m_out[s_i, h:h + 1, :] = new_state[s_i, h]
    if not carried:
        for s_i in seqs:
            m_out[s_i, M_HEADS:, :] = jnp.zeros((M_HEADS, LANES), F32)

    if carried:
        @pl.when(c == pl.num_programs(chunk_axis) - 1)
        def _():
            c1_ref[...] = c_out[...]
            n1_ref[...] = n_out[...]
            m1_ref[...] = m_out[...]


def _mlstm_state_specs(nseq, l_state):
    if l_state is None:
        c_spec = pl.BlockSpec((nseq, M_HEADS, M_DK, M_DV), lambda b, c: (b, 0, 0, 0))
    else:
        c_spec = pl.BlockSpec((None, nseq, M_HEADS, M_DK, M_DV), lambda b, c: (l_state, b, 0, 0, 0))
    return (c_spec,
            pl.BlockSpec((nseq, M_HEADS, 1, M_DK), lambda b, c: (b, 0, 0, 0)),
            pl.BlockSpec((nseq, 2 * M_HEADS, LANES), lambda b, c: (b, 0, 0)))


def _mlstm_scratch(nseq):
    return [pltpu.VMEM((nseq, M_HEADS, M_DK, M_DV), F32),
            pltpu.VMEM((nseq, M_HEADS, 1, M_DK), F32),
            pltpu.VMEM((nseq, 2 * M_HEADS, LANES), F32)]


def _mlstm_prompt(z, zif, ifr, gamma, l, bp, nc, L):
    hw = M_HEADS * M_DK
    zero = lambda *s: jnp.zeros(s, F32)
    in_specs, args = [], []
    for b in range(bp):
        def rows(col, b=b):
            return lambda c: (b * nc + c, col)

        in_specs += [pl.BlockSpec((L, LANES), rows(0)),
                     pl.BlockSpec((1, 2 * M_HEADS, L), lambda c, b=b: (b * nc + c, 0, 0)),
                     pl.BlockSpec((L, hw), rows(Z_MQ // hw)),
                     pl.BlockSpec((L, hw), rows(Z_MK // hw)),
                     pl.BlockSpec((L, hw), rows(Z_MV // hw)),
                     pl.BlockSpec((L, hw), rows(Z_MO // hw))]
        args += [zif, ifr, z, z, z, z]
    state_specs = [pl.BlockSpec((bp, M_HEADS, M_DK, M_DV), lambda c: (0, 0, 0, 0)),
                   pl.BlockSpec((bp, M_HEADS, 1, M_DK), lambda c: (0, 0, 0, 0)),
                   pl.BlockSpec((bp, 2 * M_HEADS, LANES), lambda c: (0, 0, 0))]
    kern = functools.partial(_mlstm_kernel, t_valid=L, nseq=bp, aliased=False, carried=True, chunk_axis=0,
                             per_seq_inputs=True)
    outs = pl.pallas_call(
        kern,
        grid=(nc,),
        in_specs=in_specs + [pl.BlockSpec((None, 1, hw), lambda c: (l, 0, 0))] + state_specs,
        out_specs=[pl.BlockSpec((bp, L, hw), lambda c: (0, c, 0))] + state_specs,
        out_shape=[jax.ShapeDtypeStruct((bp, nc * L, hw), BF16),
                   jax.ShapeDtypeStruct((bp, M_HEADS, M_DK, M_DV), F32),
                   jax.ShapeDtypeStruct((bp, M_HEADS, 1, M_DK), F32),
                   jax.ShapeDtypeStruct((bp, 2 * M_HEADS, LANES), F32)],
        scratch_shapes=_mlstm_scratch(bp),
        compiler_params=_cparams("arbitrary"),
        name="mlstm_prompt",
    )(*args, gamma, zero(bp, M_HEADS, M_DK, M_DV), zero(bp, M_HEADS, 1, M_DK), zero(bp, 2 * M_HEADS, LANES))
    return (outs[0].reshape(bp * nc * L, hw),) + tuple(outs[1:])


def _mlstm_sample(zm3, if3, ifr, gamma, c_all, n0, m0x, c_out_prev, l, t_valid, nseq):
    bs, L, _ = zm3.shape
    depth = c_all.shape[0]
    hw = M_HEADS * M_DK
    aliased = c_out_prev is not None

    def blk(col):
        return pl.BlockSpec((nseq, L, hw), lambda b, c: (b, 0, col))

    c_in, n_spec, m_spec = _mlstm_state_specs(nseq, l)
    in_specs = [pl.BlockSpec((nseq, L, LANES), lambda b, c: (b, 0, 0)),
                pl.BlockSpec((nseq, 2 * M_HEADS, L), lambda b, c: (b, 0, 0)),
                blk(0), blk(1), blk(2), blk(3),
                pl.BlockSpec((None, 1, hw), lambda b, c: (l, 0, 0)),
                c_in, n_spec, m_spec]
    args = [if3, ifr, zm3, zm3, zm3, zm3, gamma, c_all, n0, m0x]
    aliases = {}
    if aliased:
        in_specs.append(pl.BlockSpec(memory_space=pl.ANY))
        args.append(c_out_prev)
        aliases = {len(args) - 1: 1}
        c_out = c_in
    else:
        c_out = pl.BlockSpec((depth, nseq, M_HEADS, M_DK, M_DV), lambda b, c: (0, b, 0, 0, 0))
    kern = functools.partial(_mlstm_kernel, t_valid=t_valid, nseq=nseq, aliased=aliased, carried=False,
                             chunk_axis=1, first_layer=None if aliased else l)
    return pl.pallas_call(
        kern,
        grid=(bs // nseq, 1),
        in_specs=in_specs,
        out_specs=[pl.BlockSpec((nseq, L, hw), lambda b, c: (b, 0, 0)), c_out, n_spec, m_spec],
        out_shape=[jax.ShapeDtypeStruct((bs, L, hw), BF16),
                   jax.ShapeDtypeStruct((depth, bs, M_HEADS, M_DK, M_DV), F32),
                   jax.ShapeDtypeStruct((bs, M_HEADS, 1, M_DK), F32),
                   jax.ShapeDtypeStruct((bs, 2 * M_HEADS, LANES), F32)],
        input_output_aliases=aliases,
        compiler_params=_cparams("parallel", "arbitrary"),
        name="mlstm_sample",
    )(*args)


def _m_state_in(m):
    lead = m.shape[:-1]
    mx = jnp.zeros(lead + (2 * M_HEADS, LANES), F32)
    return mx.at[..., :M_HEADS, :].set(jnp.broadcast_to(m[..., None], lead + (M_HEADS, LANES)))


def _merge_kernel(*refs, np_tiles, alpha, route):
    (cp_ref, cs_ref, ap_ref, as_ref, mp_ref, ms_ref, g0_ref, g1_ref, g2_ref, xp_ref, xs_ref,
     gtp_ref, gts_ref, shp_ref, shs_ref, scp_ref, scs_ref,
     wc_ref, wa_ref, wm_ref, wo_ref, pg_ref, pb_ref) = refs[:23]
    if route:
        rw_ref, rb_ref, x1_ref, h_ref, route_ref = refs[23:]
    else:
        x1_ref, h_ref = refs[23:]
    is_s = pl.program_id(0) >= np_tiles

    def gate(ref):
        return _sigmoid(ref[...].astype(F32))

    def run(c_ref, a_ref, m_ref, x_ref, gt_ref, sh_ref, sc_ref):
        y = (gate(g0_ref) * _dot(c_ref[...], wc_ref[...])
             + gate(g1_ref) * _dot(a_ref[...], wa_ref[...])
             + gate(g2_ref) * _dot(m_ref[...], wm_ref[...]))
        mix = _dot(y.astype(BF16), wo_ref[...])
        x1 = _norm(alpha * x_ref[...] + gt_ref[...] * mix) * pg_ref[...] + pb_ref[...]
        x1_ref[...] = x1
        h = _norm(x1) * (1.0 + sc_ref[...]) + sh_ref[...]
        h_ref[...] = h.astype(h_ref.dtype)
        if route:
            h_hi, h_lo = _split_bf16(h)
            w_hi, w_lo = _split_bf16(rw_ref[...])
            logits = _dot(h_hi, w_hi) + (_dot(h_hi, w_lo) + _dot(h_lo, w_hi)) + rb_ref[...]
            lane = lax.broadcasted_iota(jnp.int32, logits.shape, 1)
            m1 = jnp.max(logits, axis=-1, keepdims=True)
            e1 = jnp.min(jnp.where(logits == m1, lane, LANES), axis=-1, keepdims=True)
            l2 = jnp.where(lane == e1, NEG_INF, logits)
            m2 = jnp.max(l2, axis=-1, keepdims=True)
            e2 = jnp.min(jnp.where(l2 == m2, lane, LANES), axis=-1, keepdims=True)
            ex = jnp.exp(m2 - m1)
            w1 = 1.0 / (1.0 + ex)
            w2 = ex / (1.0 + ex)
            out = jnp.where(lane == 0, e1.astype(F32),
                            jnp.where(lane == 1, e2.astype(F32),
                                      jnp.where(lane == 2, w1, jnp.where(lane == 3, w2, 0.0))))
            route_ref[...] = out

    @pl.when(is_s)
    def _():
        run(cs_ref, as_ref, ms_ref, xs_ref, gts_ref, shs_ref, scs_ref)

    @pl.when(jnp.logical_not(is_s))
    def _():
        run(cp_ref, ap_ref, mp_ref, xp_ref, gtp_ref, shp_ref, scp_ref)


def _merge(cp, cs, ap, as_, mp, ms, z, xp, xs, mod_p, mod_s, wc, wa, wm, wo, pg, pb, l, rm, alpha, router):
    n, d = z.shape[0], xp.shape[1]
    hw = D_CONV
    route = router is not None
    tm = rm.t

    def pblk():
        return pl.BlockSpec((tm, hw), lambda i: (rm.prompt(i), 0))

    def sblk():
        return pl.BlockSpec((tm, hw), lambda i: (rm.sample(i), 0))

    def zg(k):
        return pl.BlockSpec((tm, d), lambda i: (i, Z_G // d + k))

    def lw(a):
        return pl.BlockSpec((None,) + a.shape[1:], lambda i: (l,) + (0,) * (a.ndim - 1))

    post = pl.BlockSpec((None, None, 1, d), lambda i: (l, 0, 0, 0))
    in_specs = [pblk(), sblk(), pblk(), sblk(), pblk(), sblk(), zg(0), zg(1), zg(2),
                *rm.x_specs(xs, d),
                *rm.mod_specs(l, 2, d), *rm.mod_specs(l, 3, d), *rm.mod_specs(l, 4, d),
                lw(wc), lw(wa), lw(wm), lw(wo), post, post]
    args = [cp, cs, ap, as_, mp, ms, z, z, z, xp, xs, mod_p, mod_s, mod_p, mod_s, mod_p, mod_s,
            wc, wa, wm, wo, pg, pb]
    out_specs = [pl.BlockSpec((tm, d), lambda i: (i, 0)), pl.BlockSpec((tm, d), lambda i: (i, 0))]
    out_shape = [jax.ShapeDtypeStruct((n, d), F32), jax.ShapeDtypeStruct((n, d), F32 if route else BF16)]
    if route:
        rw, rb, lj = router
        in_specs += [pl.BlockSpec((None, d, LANES), lambda i: (lj, 0, 0)),
                     pl.BlockSpec((None, 1, LANES), lambda i: (lj, 0, 0))]
        args += [rw, rb]
        out_specs.append(pl.BlockSpec((tm, LANES), lambda i: (i, 0)))
        out_shape.append(jax.ShapeDtypeStruct((n, LANES), F32))
    kern = functools.partial(_merge_kernel, np_tiles=rm.np_tiles, alpha=alpha, route=route)
    return pl.pallas_call(
        kern, grid=(n // tm,), in_specs=in_specs, out_specs=out_specs, out_shape=out_shape,
        compiler_params=_cparams("parallel"), name="merge_route" if route else "merge",
    )(*args)


def _post_residual(x_ref, gp_ref, gs_ref, f, pg_ref, pb_ref, is_s, alpha):
    return _norm(alpha * x_ref[...] + _pick(is_s, gp_ref, gs_ref) * f) * pg_ref[...] + pb_ref[...]


def _ffn_kernel(h_ref, wg_ref, wu_ref, wd_ref, x_ref, gp_ref, gs_ref, pg_ref, pb_ref, o_ref, acc,
                *, alpha, np_tiles):
    j = pl.program_id(1)
    h = h_ref[...]
    a = (_silu(_dot(h, wg_ref[...])) * _dot(h, wu_ref[...])).astype(BF16)
    part = _dot(a, wd_ref[...])

    @pl.when(j == 0)
    def _():
        acc[...] = part

    @pl.when(j > 0)
    def _():
        acc[...] = acc[...] + part

    @pl.when(j == pl.num_programs(1) - 1)
    def _():
        is_s = pl.program_id(0) >= np_tiles
        o_ref[...] = _post_residual(x_ref, gp_ref, gs_ref, acc[...], pg_ref, pb_ref, is_s, alpha)


def _ffn_dense(h, wg, wu, wd, x, mod_p, mod_s, pg, pb, l, lj, rm, tf, alpha):
    n, d = x.shape
    f = wg.shape[2]
    tm = rm.t
    post = pl.BlockSpec((None, None, 1, d), lambda i, j: (l, 1, 0, 0))
    return pl.pallas_call(
        functools.partial(_ffn_kernel, alpha=alpha, np_tiles=rm.np_tiles),
        grid=(n // tm, f // tf),
        in_specs=[pl.BlockSpec((tm, d), lambda i, j: (i, 0)),
                  pl.BlockSpec((None, d, tf), lambda i, j: (lj, 0, j)),
                  pl.BlockSpec((None, d, tf), lambda i, j: (lj, 0, j)),
                  pl.BlockSpec((None, tf, d), lambda i, j: (lj, j, 0)),
                  pl.BlockSpec((tm, d), lambda i, j: (i, 0)),
                  *rm.mod_specs(l, 5, d), post, post],
        out_specs=pl.BlockSpec((tm, d), lambda i, j: (i, 0)),
        out_shape=jax.ShapeDtypeStruct((n, d), F32),
        scratch_shapes=[pltpu.VMEM((tm, d), F32)],
        compiler_params=_cparams("parallel", "arbitrary"),
        name="ffn_dense",
    )(h, wg, wu, wd, x, mod_p, mod_s, pg, pb)


def _rank_kernel(route_ref, rank_ref, tot_ref, carry):
    i = pl.program_id(0)
    tm = route_ref.shape[0]

    @pl.when(i == 0)
    def _():
        carry[...] = jnp.zeros_like(carry)

    r = route_ref[...]
    lane = lax.broadcasted_iota(jnp.int32, (tm, LANES), 1)
    e1 = r[:, 0:1].astype(jnp.int32)
    e2 = r[:, 1:2].astype(jnp.int32)
    hit1 = lane == e1
    hit2 = lane == e2
    onehot = jnp.where(jnp.logical_or(hit1, hit2), 1.0, 0.0)
    tt = lax.broadcasted_iota(jnp.int32, (tm, tm), 0)
    ss = lax.broadcasted_iota(jnp.int32, (tm, tm), 1)
    before = jnp.where(ss < tt, 1.0, 0.0).astype(BF16)
    cnt = _dot(before, onehot.astype(BF16)) + carry[0:1, :]
    r1 = jnp.sum(jnp.where(hit1, cnt, 0.0), axis=-1, keepdims=True)
    r2 = jnp.sum(jnp.where(hit2, cnt, 0.0), axis=-1, keepdims=True)
    rank_ref[...] = jnp.where(lane == 0, r1, jnp.where(lane == 1, r2, 0.0))
    carry[...] = carry[...] + jnp.sum(onehot, axis=0, keepdims=True)
    tot_ref[...] = carry[...]


def _moe_rank(route, tm):
    n = route.shape[0]
    return pl.pallas_call(
        _rank_kernel,
        grid=(n // tm,),
        in_specs=[pl.BlockSpec((tm, LANES), lambda i: (i, 0))],
        out_specs=[pl.BlockSpec((tm, LANES), lambda i: (i, 0)),
                   pl.BlockSpec((8, LANES), lambda i: (0, 0))],
        out_shape=[jax.ShapeDtypeStruct((n, LANES), F32), jax.ShapeDtypeStruct((8, LANES), F32)],
        scratch_shapes=[pltpu.VMEM((8, LANES), F32)],
        compiler_params=_cparams("arbitrary"),
        name="moe_rank",
    )(route)


def _row_copy(src, s, dst, t, sem):
    return pltpu.make_async_copy(src.at[pl.ds(s, 1)], dst.at[pl.ds(t, 1)], sem)


def _dispatch_kernel(d1_ref, d2_ref, zl_ref, h_ref, xs_hbm, zbuf, stage, sem, zsem):
    i = pl.program_id(0)
    tm = h_ref.shape[0]

    @pl.when(i == 0)
    def _():
        zbuf[...] = jnp.zeros_like(zbuf)

        def zero_copy(k):
            row = pl.multiple_of(jnp.maximum(zl_ref[k], 0), MOE_SUB)
            return pltpu.make_async_copy(zbuf, xs_hbm.at[pl.ds(row, MOE_SUB)], zsem)

        def start(k, carry):
            @pl.when(zl_ref[k] >= 0)
            def _():
                zero_copy(k).start()
            return carry

        def wait(k, carry):
            @pl.when(zl_ref[k] >= 0)
            def _():
                zero_copy(k).wait()
            return carry

        lax.fori_loop(0, zl_ref.shape[0], start, 0)
        lax.fori_loop(0, zl_ref.shape[0], wait, 0)

    slot = i % 2
    stage[slot] = h_ref[...]

    def issue(r, carry):
        _row_copy(stage.at[slot], r, xs_hbm, d1_ref[i * tm + r], sem.at[slot]).start(priority=0)
        _row_copy(stage.at[slot], r, xs_hbm, d2_ref[i * tm + r], sem.at[slot]).start(priority=1)
        return carry

    lax.fori_loop(0, tm, issue, 0, unroll=8)

    def drain(s):
        for _ in range(2):
            pltpu.make_async_copy(stage.at[s], xs_hbm.at[pl.ds(0, tm)], sem.at[s]).wait()

    @pl.when(i > 0)
    def _():
        drain(1 - slot)

    @pl.when(i == pl.num_programs(0) - 1)
    def _():
        drain(slot)


def _moe_dispatch(dest1, dest2, zlist, h, n_rows, tm):
    n, d = h.shape
    grid_spec = pltpu.PrefetchScalarGridSpec(
        num_scalar_prefetch=3,
        grid=(n // tm,),
        in_specs=[pl.BlockSpec((tm, d), lambda i, *_: (i, 0))],
        out_specs=pl.BlockSpec(memory_space=pl.ANY),
        scratch_shapes=[pltpu.VMEM((MOE_SUB, d), F32), pltpu.VMEM((2, tm, d), F32),
                        pltpu.SemaphoreType.DMA((2,)), pltpu.SemaphoreType.DMA(())],
    )
    return pl.pallas_call(
        _dispatch_kernel, grid_spec=grid_spec,
        out_shape=jax.ShapeDtypeStruct((n_rows, d), F32),
        compiler_params=_cparams("arbitrary"),
        name="moe_dispatch",
    )(dest1, dest2, zlist, h)


def _expert_kernel(blk_e_ref, nvalid_ref, xs_ref, wg_ref, wu_ref, wd_ref, y_ref, xb, acc):
    b = pl.program_id(0)
    j = pl.program_id(1)
    tb = xb.shape[0]
    nsub = tb // MOE_SUB
    nv = nvalid_ref[b]
    used = (nv + MOE_SUB - 1) // MOE_SUB

    def run(rows):
        wg = wg_ref[...].astype(BF16)
        wu = wu_ref[...].astype(BF16)
        wd = wd_ref[...].astype(BF16)

        @pl.when(j == 0)
        def _():
            xb[rows, :] = xs_ref[rows, :].astype(BF16)

        x = xb[rows, :]
        a = (_silu(_dot(x, wg)) * _dot(x, wu)).astype(BF16)
        part = _dot(a, wd)

        @pl.when(j == 0)
        def _():
            acc[rows, :] = part

        @pl.when(j > 0)
        def _():
            acc[rows, :] = acc[rows, :] + part

        @pl.when(j == pl.num_programs(1) - 1)
        def _():
            y_ref[rows, :] = acc[rows, :]

    for m in range(1, nsub + 1):
        @pl.when(used == m)
        def _():
            run(pl.ds(0, m * MOE_SUB))

    for s in range(nsub):
        @pl.when(jnp.logical_and(s >= used, j == pl.num_programs(1) - 1))
        def _():
            y_ref[pl.ds(s * MOE_SUB, MOE_SUB), :] = jnp.zeros((MOE_SUB, y_ref.shape[1]), F32)


def _moe_experts(blk_e, nvalid, xs, wg, wu, wd, lj, tb, tf):
    n_rows, d = xs.shape
    f = wg.shape[3]
    nj = f // tf

    def jm(b, j, nv):
        return jnp.where(nv[b] > 0, j, nj - 1)

    grid_spec = pltpu.PrefetchScalarGridSpec(
        num_scalar_prefetch=2,
        grid=(n_rows // tb, nj),
        in_specs=[pl.BlockSpec((tb, d), lambda b, j, be, nv: (b, 0)),
                  pl.BlockSpec((None, None, d, tf), lambda b, j, be, nv: (lj, be[b], 0, jm(b, j, nv))),
                  pl.BlockSpec((None, None, d, tf), lambda b, j, be, nv: (lj, be[b], 0, jm(b, j, nv))),
                  pl.BlockSpec((None, None, tf, d), lambda b, j, be, nv: (lj, be[b], jm(b, j, nv), 0))],
        out_specs=pl.BlockSpec((tb, d), lambda b, j, be, nv: (b, 0)),
        scratch_shapes=[pltpu.VMEM((tb, d), BF16), pltpu.VMEM((tb, d), F32)],
    )
    return pl.pallas_call(
        _expert_kernel, grid_spec=grid_spec,
        out_shape=jax.ShapeDtypeStruct((n_rows, d), F32),
        compiler_params=_cparams("parallel", "arbitrary"),
        name="moe_experts",
    )(blk_e, nvalid, xs, wg, wu, wd)


def _combine_kernel(d1_ref, d2_ref, ys_hbm, route_ref, x_ref, gp_ref, gs_ref, pg_ref, pb_ref, *rest,
                    alpha, np_tiles, split):
    (*outs, y1, y2, sem1, sem2) = rest
    i = pl.program_id(0)
    tm = x_ref.shape[0]
    slot = i % 2

    def gather(tile, s):
        def issue(r, carry):
            _row_copy(ys_hbm, d1_ref[tile * tm + r], y1.at[s], r, sem1.at[s]).start(priority=0)
            _row_copy(ys_hbm, d2_ref[tile * tm + r], y2.at[s], r, sem2.at[s]).start(priority=1)
            return carry

        lax.fori_loop(0, tm, issue, 0, unroll=8)

    @pl.when(i == 0)
    def _():
        gather(0, 0)

    @pl.when(i + 1 < pl.num_programs(0))
    def _():
        gather(i + 1, 1 - slot)

    pltpu.make_async_copy(ys_hbm.at[pl.ds(0, tm)], y1.at[slot], sem1.at[slot]).wait()
    pltpu.make_async_copy(ys_hbm.at[pl.ds(0, tm)], y2.at[slot], sem2.at[slot]).wait()
    r = route_ref[...]
    f = y1[slot] * r[:, 2:3] + y2[slot] * r[:, 3:4]
    out = _post_residual(x_ref, gp_ref, gs_ref, f, pg_ref, pb_ref, i >= np_tiles, alpha)
    if split:
        op_ref, os_ref = outs

        @pl.when(i < np_tiles)
        def _():
            op_ref[...] = out

        @pl.when(i >= np_tiles)
        def _():
            os_ref[...] = out
    else:
        outs[0][...] = out


def _moe_combine(dest1, dest2, ys, route, x, mod_p, mod_s, pg, pb, l, rm, alpha, split):
    n, d = x.shape
    tm = rm.t
    if split:
        n_p = rm.np_tiles * tm
        out_specs = [pl.BlockSpec((tm, d), lambda i, *_: (rm.prompt(i), 0)),
                     pl.BlockSpec((tm, d), lambda i, *_: (rm.sample(i), 0))]
        out_shape = [jax.ShapeDtypeStruct((n_p, d), F32), jax.ShapeDtypeStruct((n - n_p, d), F32)]
    else:
        out_specs = pl.BlockSpec((tm, d), lambda i, *_: (i, 0))
        out_shape = jax.ShapeDtypeStruct((n, d), F32)
    post = pl.BlockSpec((None, None, 1, d), lambda i, *_: (l, 1, 0, 0))
    grid_spec = pltpu.PrefetchScalarGridSpec(
        num_scalar_prefetch=2,
        grid=(n // tm,),
        in_specs=[pl.BlockSpec(memory_space=pl.ANY),
                  pl.BlockSpec((tm, LANES), lambda i, *_: (i, 0)),
                  pl.BlockSpec((tm, d), lambda i, *_: (i, 0)),
                  *rm.mod_specs(l, 5, d), post, post],
        out_specs=out_specs,
        scratch_shapes=[pltpu.VMEM((2, tm, d), F32), pltpu.VMEM((2, tm, d), F32),
                        pltpu.SemaphoreType.DMA((2,)), pltpu.SemaphoreType.DMA((2,))],
    )
    return pl.pallas_call(
        functools.partial(_combine_kernel, alpha=alpha, np_tiles=rm.np_tiles, split=split), grid_spec=grid_spec,
        out_shape=out_shape,
        compiler_params=_cparams("arbitrary"),
        name="moe_combine",
    )(dest1, dest2, ys, route, x, mod_p, mod_s, pg, pb)


def _moe_ffn(h, route, wg, wu, wd, x, mod_p, mod_s, pg, pb, l, lj, rm, tb, tf, alpha, split):
    n = h.shape[0]
    rank, tot = _moe_rank(route, rm.t)
    counts = tot[0, :N_EXPERTS].astype(jnp.int32)
    padded = (counts + tb - 1) // tb * tb
    pad_end = jnp.cumsum(padded)
    pad_start = pad_end - padded
    experts = jnp.arange(N_EXPERTS, dtype=jnp.int32)

    def slot(col):
        e = route[:, col].astype(jnp.int32)
        start = jnp.sum(jnp.where(e[:, None] == experts[None, :], pad_start[None, :], 0), axis=1)
        return start + rank[:, col].astype(jnp.int32)

    dest1, dest2 = slot(0), slot(1)
    n_blocks = -(-(2 * n + N_EXPERTS * (tb - 1)) // tb)
    blk_start = jnp.arange(n_blocks, dtype=jnp.int32) * tb
    blk_e = jnp.minimum(jnp.sum(pad_end[None, :] <= blk_start[:, None], axis=1), N_EXPERTS - 1).astype(jnp.int32)
    own_end = jnp.sum(jnp.where(blk_e[:, None] == experts[None, :], (pad_start + counts)[None, :], 0), axis=1)
    nvalid = jnp.clip(own_end - blk_start, 0, tb).astype(jnp.int32)
    nvalid = jnp.where(blk_start < pad_end[-1], nvalid, 0)
    per_blk = tb // MOE_SUB
    sub_start = jnp.arange(n_blocks * per_blk, dtype=jnp.int32) * MOE_SUB
    sub_room = jnp.repeat(blk_start + nvalid, per_blk) - sub_start
    zlist = jnp.where(sub_room < MOE_SUB, sub_start, -1).astype(jnp.int32)
    xs = _moe_dispatch(dest1, dest2, zlist, h, n_blocks * tb, rm.t)
    ys = _moe_experts(blk_e, nvalid, xs, wg, wu, wd, lj, tb, tf)
    return _moe_combine(dest1, dest2, ys, route, x, mod_p, mod_s, pg, pb, l, rm, alpha, split)


def _pack_w_in(w_in, b_in):
    depth, d, _ = w_in.shape
    q_end = 2 * D_CONV + N_HEADS * HEAD_DIM
    k_end = q_end + N_KV * HEAD_DIM
    a_end = k_end + N_KV * HEAD_DIM
    m_end = a_end + 4 * M_HEADS * M_DK
    if_end = m_end + 2 * M_HEADS

    def pack(a):
        return jnp.concatenate([a[..., if_end:], a[..., :q_end], a[..., a_end:m_end], a[..., q_end:a_end]], axis=-1)

    def gates(a):
        return jnp.pad(a[..., m_end:if_end], [(0, 0)] * (a.ndim - 1) + [(0, LANES - 2 * M_HEADS)])

    return (pack(w_in).astype(BF16), pack(b_in).reshape(depth, 1, Z_W),
            gates(w_in).astype(BF16), gates(b_in).reshape(depth, 1, LANES))


def kernel(x_prompt, x_sample, cache_swa_k, cache_swa_v, state_conv, state_mlstm_C, state_mlstm_n, state_mlstm_m, c_prompt, c_sample, w_ada, b_ada, w_in, b_in, conv_w, conv_b, conv_ln_g, conv_ln_b, w_conv_out, attn_sinks, rel_bias, w_attn_out, m_norm_g, w_m_out, w_out, post_ln_g, post_ln_b, ffn_w_gate, ffn_w_up, ffn_w_down, router_w, router_b, moe_w_gate, moe_w_up, moe_w_down):
    bp, tp, d = x_prompt.shape
    bs, ts, _ = x_sample.shape
    depth = w_ada.shape[0]
    alpha = (2 * depth) ** 0.25
    n_p, n_s = bp * tp, bs * ts
    tm = n_s
    assert d == D_MODEL and tp % tm == 0 and tp % WINDOW == 0 and tm % 32 == 0
    rm = _RowMap(tm, bp, tp, n_p, n_s)
    rm_half = _RowMap(tm // 2, bp, tp, n_p, n_s)
    wb = cache_swa_k.shape[2]
    big = n_p >= 4096
    tc = 512 if big else tm
    lm = 256 if big else min(tp, 128)
    tb = 1024 if big else 2 * MOE_SUB
    bs_blk = 32 if bs % 32 == 0 else bs
    bs_att = 16 if bs % 16 == 0 else bs
    bs_m = 4 if bs % 4 == 0 else 1
    lts = 16

    n = n_p + n_s
    x = (x_prompt.reshape(n_p, d), jnp.transpose(x_sample, (1, 0, 2)).reshape(n_s, d))

    nc_rows = -(-(bp + bs) // 8) * 8
    c_all = jnp.zeros((nc_rows, d), F32).at[:bp].set(c_prompt).at[bp:bp + bs].set(c_sample)
    mod = _ada_mod(c_all, w_ada, b_ada)
    mod_p = mod[:, :bp].reshape(depth, bp, 1, 6 * d)
    mod_s = jnp.tile(mod[:, bp:bp + bs], (1, ts, 1))

    w_in_p, b_in_p, w_if, b_if = _pack_w_in(w_in, b_in)
    wc_b, wa_b, wm_b, wo_b = (w.astype(BF16) for w in (w_conv_out, w_attn_out, w_m_out, w_out))
    fg_b, fu_b, fd_b = (w.astype(BF16) for w in (ffn_w_gate, ffn_w_up, ffn_w_down))
    cw_pad = jnp.pad(conv_w, ((0, 0), (0, CONV_PAD - CONV_W), (0, 0)))
    cvecs = [v.reshape(depth, 1, D_CONV) for v in (conv_b, conv_ln_g, conv_ln_b)]
    rw_pad = jnp.pad(router_w, ((0, 0), (0, 0), (0, LANES - N_EXPERTS)))
    rb_pad = jnp.pad(router_b, ((0, 0), (0, LANES - N_EXPERTS)), constant_values=NEG_INF)[:, None, :]
    pg = post_ln_g.reshape(depth, 2, 1, d)
    pb = post_ln_b.reshape(depth, 2, 1, d)
    gamma = m_norm_g.reshape(depth, 1, M_HEADS * M_DV)
    sinks = attn_sinks.astype(F32)

    qi = jnp.arange(WINDOW)[:, None]
    kj = jnp.arange(2 * WINDOW)[None, :]
    dist_p = qi + WINDOW - kj
    bh = _bias_heads(rel_bias, dist_p, (dist_p >= 0) & (dist_p < WINDOW))
    bias_p = bh.reshape(N_HEADS // 2, 2, WINDOW, 2 * WINDOW).transpose(0, 2, 1, 3).reshape(
        N_HEADS // 2, WINDOW, 4 * WINDOW)
    dist_s = jnp.arange(ts)[:, None] + wb - jnp.arange(wb + ts)[None, :]
    bias_s = _bias_heads(rel_bias, dist_s, (dist_s >= 0) & (dist_s < WINDOW)).reshape(
        N_KV, Q_PER_KV * ts, wb + ts)
    kc_all = cache_swa_k.reshape(depth, bs, wb, N_KV * HEAD_DIM)
    vc_all = cache_swa_v.reshape(depth, bs, wb, N_KV * HEAD_DIM)

    f_dense = ffn_w_gate.shape[2]
    tf_dense = f_dense // 2 if (f_dense // 2) % LANES == 0 else f_dense
    f_moe = moe_w_gate.shape[3]
    tf_moe = 512 if f_moe % 512 == 0 else f_moe

    n0_all = state_mlstm_n[:, :, :, None, :]
    m0_all = _m_state_in(state_mlstm_m)
    s_c = None
    new_p = [[] for _ in range(6)]
    new_s = [[] for _ in range(6)]
    for l in range(depth):
        j = l // 2
        xp, xs = x if isinstance(x, tuple) else (x, x)
        z, zif = _ln_proj(xp, xs, mod_p, mod_s, w_in_p, b_in_p, w_if, b_if, l, rm, n)
        zs3 = z[n_p:].reshape(ts, bs, Z_W)

        cp, ns_p = _conv_prompt(z, jnp.zeros((bp, CONV_PAD, D_CONV), F32), cw_pad, *cvecs, l, bp, tp, tc)
        cs3, a_s3 = _conv_sample(zs3, state_conv, cw_pad, *cvecs, l, bs_blk)
        new_p[2].append(ns_p[:, CONV_PAD - CONV_W + 1:])
        new_s[2].append(jnp.transpose(a_s3, (1, 0, 2)))

        sink_h = sinks[l].reshape(N_KV, Q_PER_KV, 1)
        sink_s = jnp.broadcast_to(sink_h, (N_KV, Q_PER_KV, ts)).reshape(N_KV, Q_PER_KV * ts, 1)
        ap = _attn_prompt(z, bias_p, sinks, l, bp, tp)
        nk = min(WINDOW, tp)
        kv_tail = jnp.stack([z[(b + 1) * tp - nk:(b + 1) * tp, Z_K:Z_K + 2 * N_KV * HEAD_DIM]
                             for b in range(bp)]).astype(F32)
        new_p[0].append(kv_tail[..., :N_KV * HEAD_DIM].reshape(bp, nk, N_KV, HEAD_DIM))
        new_p[1].append(kv_tail[..., N_KV * HEAD_DIM:].reshape(bp, nk, N_KV, HEAD_DIM))
        q_s = zs3[:, :, Z_Q:Z_Q + N_HEADS * HEAD_DIM].reshape(ts, bs, N_KV, Q_PER_KV, HEAD_DIM)
        q4 = jnp.transpose(q_s, (1, 2, 3, 0, 4)).reshape(bs, N_KV, Q_PER_KV * ts, HEAD_DIM)
        k_s = jnp.transpose(zs3[:, :, Z_K:Z_K + N_KV * HEAD_DIM].reshape(ts, bs, N_KV, HEAD_DIM), (1, 0, 2, 3))
        v_s = jnp.transpose(zs3[:, :, Z_V:Z_V + N_KV * HEAD_DIM].reshape(ts, bs, N_KV, HEAD_DIM), (1, 0, 2, 3))
        o4 = _attn_sample(q4, kc_all, vc_all, jnp.transpose(k_s, (0, 2, 1, 3)), jnp.transpose(v_s, (0, 2, 1, 3)),
                          bias_s, sink_s, l, bs_att)
        as_ = jnp.transpose(o4.reshape(bs, N_KV, Q_PER_KV, ts, HEAD_DIM), (3, 0, 1, 2, 4)).reshape(n_s, -1).astype(BF16)
        new_s[0].append(k_s.astype(F32))
        new_s[1].append(v_s.astype(F32))

        ncp = tp // lm
        if_p = zif[:n_p, :2 * M_HEADS].reshape(bp * ncp, lm, 2 * M_HEADS)
        mp, c1p, n1p, m1p = _mlstm_prompt(z, zif, jnp.transpose(if_p, (0, 2, 1)), gamma, l, bp, ncp, lm)
        new_p[3].append(c1p)
        new_p[4].append(n1p[:, :, 0])
        new_p[5].append(m1p[:, :M_HEADS, 0])
        tpad = ((0, 0), (0, lts - ts), (0, 0))
        zm3 = jnp.pad(jnp.transpose(zs3[:, :, Z_MQ:Z_K], (1, 0, 2)), tpad)
        if3 = jnp.pad(jnp.transpose(zif[n_p:].reshape(ts, bs, LANES), (1, 0, 2)), tpad)
        ms, s_c, n1s, m1s = _mlstm_sample(zm3, if3, jnp.transpose(if3[:, :, :2 * M_HEADS], (0, 2, 1)), gamma,
                                          state_mlstm_C, n0_all[l], m0_all[l], s_c, l, ts, bs_m)
        ms = jnp.transpose(ms[:, :ts], (1, 0, 2)).reshape(n_s, -1)
        new_s[4].append(n1s[:, :, 0])
        new_s[5].append(m1s[:, :M_HEADS, 0])

        moe = l % 2 == 1
        router = (rw_pad, rb_pad, j) if moe else None
        outs = _merge(cp, cs3.reshape(n_s, D_CONV), ap, as_, mp, ms, z, xp, xs, mod_p, mod_s,
                      wc_b, wa_b, wm_b, wo_b, pg, pb, l, rm_half, alpha, router)
        if moe:
            x1, h2, route = outs
            x = _moe_ffn(h2, route, moe_w_gate, moe_w_up, moe_w_down, x1, mod_p, mod_s, pg, pb,
                         l, j, rm, tb, tf_moe, alpha, split=l == depth - 1)
        else:
            x1, h2 = outs
            x = _ffn_dense(h2, fg_b, fu_b, fd_b, x1, mod_p, mod_s, pg, pb, l, j, rm, tf_dense, alpha)

    x_p, x_s = x if isinstance(x, (list, tuple)) else (x[:n_p], x[n_p:])
    y_p = x_p.reshape(bp, tp, d)
    y_s = jnp.transpose(x_s.reshape(ts, bs, d), (1, 0, 2))
    p_k, p_v, p_conv, p_c, p_n, p_m = [jnp.stack(a) for a in new_p]
    s_k, s_v, s_conv = [jnp.concatenate([old[:, :, ts:], jnp.stack(new)], axis=2)
                        for old, new in zip((cache_swa_k, cache_swa_v, state_conv), new_s[:3])]
    s_n, s_m = jnp.stack(new_s[4]), jnp.stack(new_s[5])
    return (y_p, y_s, p_k, p_v, p_conv, p_c, p_n, p_m, s_k, s_v, s_conv, s_c, s_n, s_m)
```

```python
import functools
import math

import jax
import jax.numpy as jnp
from jax import lax
from jax.experimental import pallas as pl
from jax.experimental.pallas import tpu as pltpu

F32 = jnp.float32
BF16 = jnp.bfloat16

D_MODEL = 1024
D_CONV = 512
CONV_W = 31
CONV_PAD = 32
N_HEADS = 8
N_KV = 2
HEAD_DIM = 64
Q_PER_KV = N_HEADS // N_KV
WINDOW = 128
N_BUCKETS = 32
MAX_DIST = 128
M_HEADS = 4
M_DK = 128
M_DV = 128
N_EXPERTS = 8
LN_EPS = 1e-5
LANES = 128
NEG_INF = float("-inf")
VMEM_LIMIT = 56 * 1024 * 1024

Z_G, Z_UA, Z_UB, Z_Q = 0, 3072, 3584, 4096
Z_MQ, Z_MK, Z_MV, Z_MO = 4608, 5120, 5632, 6144
Z_K, Z_V, Z_W = 6656, 6784, 6912
TN_IN = 3456
ATT_QB = 4
ATT_GROUP = 8
MOE_SUB = 256
CONV_CHUNK = 32

def _cparams(*sem):
    return pltpu.CompilerParams(dimension_semantics=sem, vmem_limit_bytes=VMEM_LIMIT)


def _sigmoid(x):
    return 1.0 / (1.0 + jnp.exp(-x))


def _silu(x):
    return x * _sigmoid(x)


def _log_sigmoid(x):
    return jnp.minimum(x, 0.0) - jnp.log(1.0 + jnp.exp(-jnp.abs(x)))


def _norm(x):
    mu = jnp.mean(x, axis=-1, keepdims=True)
    xc = x - mu
    var = jnp.mean(xc * xc, axis=-1, keepdims=True)
    return xc * lax.rsqrt(var + LN_EPS)


def _dot(a, b):
    return jnp.dot(a, b, preferred_element_type=F32)


def _dot_nt(a, b):
    return lax.dot_general(a, b, (((1,), (1,)), ((), ())), preferred_element_type=F32)


def _dot_tn(a, b):
    return lax.dot_general(a, b, (((0,), (0,)), ((), ())), preferred_element_type=F32)


def _dot_hi(a, b):
    return jnp.dot(a, b, preferred_element_type=F32, precision=lax.Precision.HIGHEST)


def _split_bf16(a):
    hi = a.astype(BF16)
    return hi, (a - hi.astype(F32)).astype(BF16)


class _RowMap:
    def __init__(self, t, bp, tp, n_p, n_s):
        self.t = t
        self.bp = bp
        self.per_seq = tp // t
        self.np_tiles = n_p // t
        self.ns_tiles = n_s // t

    def seq(self, i):
        return jnp.minimum(i // self.per_seq, self.bp - 1)

    def prompt(self, i):
        return jnp.minimum(i, self.np_tiles - 1)

    def sample(self, i):
        return jnp.clip(i - self.np_tiles, 0, self.ns_tiles - 1)

    def mod_specs(self, l, k, d):
        return (pl.BlockSpec((None, None, 1, d), lambda i, *_: (l, self.seq(i), 0, k)),
                pl.BlockSpec((None, self.t, d), lambda i, *_: (l, self.sample(i), k)))


    def x_specs(self, xs_arr, d):
        off = self.np_tiles if xs_arr.shape[0] > self.ns_tiles * self.t else 0
        return (pl.BlockSpec((self.t, d), lambda i, *_: (self.prompt(i), 0)),
                pl.BlockSpec((self.t, d), lambda i, *_: (off + self.sample(i), 0)))


def _pick(is_s, p_ref, s_ref):
    return jnp.where(is_s, s_ref[...], p_ref[...])


def _ada_kernel(c_ref, w_ref, b_ref, o_ref):
    s = _silu(c_ref[...]).astype(BF16)
    o_ref[0] = _dot(s, w_ref[0].astype(BF16)) + b_ref[0]


def _ada_mod(c_all, w_ada, b_ada):
    depth, d, n6 = w_ada.shape
    rows = c_all.shape[0]
    return pl.pallas_call(
        _ada_kernel,
        grid=(depth, n6 // d),
        in_specs=[pl.BlockSpec((rows, d), lambda l, j: (0, 0)),
                  pl.BlockSpec((1, d, d), lambda l, j: (l, 0, j)),
                  pl.BlockSpec((1, 1, d), lambda l, j: (l, 0, j))],
        out_specs=pl.BlockSpec((1, rows, d), lambda l, j: (l, 0, j)),
        out_shape=jax.ShapeDtypeStruct((depth, rows, n6), F32),
        compiler_params=_cparams("parallel", "parallel"),
        name="ada_mod",
    )(c_all, w_ada, b_ada.reshape(depth, 1, n6))


def _ln_proj_kernel(xp_ref, xs_ref, shp_ref, shs_ref, scp_ref, scs_ref, w_ref, b_ref, wif_ref, bif_ref,
                    z_ref, zif_ref, h_scr, *, np_tiles):
    is_s = pl.program_id(0) >= np_tiles
    first = pl.program_id(1) == 0

    def prologue(x_ref, sh_ref, sc_ref):
        h = (_norm(x_ref[...]) * (1.0 + sc_ref[...]) + sh_ref[...]).astype(BF16)
        h_scr[...] = h
        zif_ref[...] = _dot(h, wif_ref[...]) + bif_ref[...]

    @pl.when(jnp.logical_and(first, is_s))
    def _():
        prologue(xs_ref, shs_ref, scs_ref)

    @pl.when(jnp.logical_and(first, jnp.logical_not(is_s)))
    def _():
        prologue(xp_ref, shp_ref, scp_ref)

    z_ref[...] = (_dot(h_scr[...], w_ref[...]) + b_ref[...]).astype(BF16)


def _ln_proj(xp, xs, mod_p, mod_s, w, b, wif, bif, l, rm, n):
    d = xp.shape[1]
    zw = w.shape[2]
    tm = rm.t
    shp, shs = rm.mod_specs(l, 0, d)
    scp, scs = rm.mod_specs(l, 1, d)
    return pl.pallas_call(
        functools.partial(_ln_proj_kernel, np_tiles=rm.np_tiles),
        grid=(n // tm, zw // TN_IN),
        in_specs=[*rm.x_specs(xs, d),
                  shp, shs, scp, scs,
                  pl.BlockSpec((None, d, TN_IN), lambda i, j: (l, 0, j)),
                  pl.BlockSpec((None, 1, TN_IN), lambda i, j: (l, 0, j)),
                  pl.BlockSpec((None, d, LANES), lambda i, j: (l, 0, 0)),
                  pl.BlockSpec((None, 1, LANES), lambda i, j: (l, 0, 0))],
        out_specs=[pl.BlockSpec((tm, TN_IN), lambda i, j: (i, j)),
                   pl.BlockSpec((tm, LANES), lambda i, j: (i, 0))],
        out_shape=[jax.ShapeDtypeStruct((n, zw), BF16), jax.ShapeDtypeStruct((n, LANES), F32)],
        scratch_shapes=[pltpu.VMEM((tm, d), BF16)],
        compiler_params=_cparams("parallel", "arbitrary"),
        name="ln_proj",
    )(xp, xs, mod_p, mod_s, mod_p, mod_s, w, b, wif, bif)


def _conv_tail(yc, g_ref, b_ref):
    y = _norm(yc) * g_ref[...] + b_ref[...]
    return _silu(y).astype(BF16)


def _conv_prompt_kernel(ua_ref, ub_ref, st_ref, cw_ref, cb_ref, g_ref, b_ref, o_ref, ns_ref,
                        ext, shifted, yc, wrep):
    t = pl.program_id(1)
    tc = ua_ref.shape[0]
    sub = 8

    @pl.when(t == 0)
    def _():
        ext[0:CONV_PAD, :] = st_ref[0]

    @pl.when(t > 0)
    def _():
        ext[0:CONV_PAD, :] = ext[tc:tc + CONV_PAD, :]

    ext[CONV_PAD:, :] = ua_ref[...].astype(F32) * _sigmoid(ub_ref[...].astype(F32))
    for s in range(1, sub):
        shifted[s - 1] = ext[s:s + tc + CONV_PAD - sub, :]
    off = CONV_PAD - (CONV_W - 1)
    for w in range(CONV_W):
        wrep[w] = jnp.broadcast_to(cw_ref[w:w + 1, :], (sub, D_CONV))
    groups = CONV_CHUNK // sub

    for r0 in range(0, tc, CONV_CHUNK):
        acc = jnp.broadcast_to(cb_ref[...].reshape(1, 1, D_CONV), (groups, sub, D_CONV))
        for w in range(CONV_W):
            base, s = (off + w) // sub * sub, (off + w) % sub
            src = ext if s == 0 else shifted.at[s - 1]
            win = src[r0 + base:r0 + base + CONV_CHUNK, :]
            acc = acc + win.reshape(groups, sub, D_CONV) * wrep[w][None]
        yc[r0:r0 + CONV_CHUNK, :] = acc.reshape(CONV_CHUNK, D_CONV)
    o_ref[...] = _conv_tail(yc[...], g_ref, b_ref)

    @pl.when(t == pl.num_programs(1) - 1)
    def _():
        ns_ref[0] = ext[tc:tc + CONV_PAD, :]


def _conv_vec_specs(l, nargs):
    return [pl.BlockSpec((None, 1, D_CONV), lambda *_: (l, 0, 0)) for _ in range(nargs)]


def _conv_prompt(z, state_pad, cw, cb, g, b, l, bp, tp, tc):
    nt = tp // tc
    return pl.pallas_call(
        _conv_prompt_kernel,
        grid=(bp, nt),
        in_specs=[pl.BlockSpec((tc, D_CONV), lambda bb, t: (bb * nt + t, Z_UA // D_CONV)),
                  pl.BlockSpec((tc, D_CONV), lambda bb, t: (bb * nt + t, Z_UB // D_CONV)),
                  pl.BlockSpec((1, CONV_PAD, D_CONV), lambda bb, t: (bb, 0, 0)),
                  pl.BlockSpec((None, CONV_PAD, D_CONV), lambda bb, t: (l, 0, 0))] + _conv_vec_specs(l, 3),
        out_specs=[pl.BlockSpec((tc, D_CONV), lambda bb, t: (bb * nt + t, 0)),
                   pl.BlockSpec((1, CONV_PAD, D_CONV), lambda bb, t: (bb, 0, 0))],
        out_shape=[jax.ShapeDtypeStruct((bp * tp, D_CONV), BF16),
                   jax.ShapeDtypeStruct((bp, CONV_PAD, D_CONV), F32)],
        scratch_shapes=[pltpu.VMEM((tc + CONV_PAD, D_CONV), F32),
                        pltpu.VMEM((7, tc + CONV_PAD - 8, D_CONV), F32),
                        pltpu.VMEM((tc, D_CONV), F32),
                        pltpu.VMEM((CONV_PAD, 8, D_CONV), F32)],
        compiler_params=_cparams("parallel", "arbitrary"),
        name="conv_prompt",
    )(z, z, state_pad, cw, cb, g, b)


def _conv_sample_kernel(ua_ref, ub_ref, st_ref, cw_ref, cb_ref, g_ref, b_ref, o_ref, a_ref):
    ts = ua_ref.shape[0]
    ns = CONV_W - 1
    a = ua_ref[...].astype(F32) * _sigmoid(ub_ref[...].astype(F32))
    a_ref[...] = a
    st = st_ref[...]
    row = lax.broadcasted_iota(jnp.int32, (ns, D_CONV), 0)
    for t in range(ts):
        wt = jnp.zeros((ns, D_CONV), F32)
        for j in range(t, ns):
            wt = jnp.where(row == j, cw_ref[j - t:j - t + 1, :], wt)
        yc = jnp.sum(st * wt[None], axis=1) + cb_ref[...]
        for t2 in range(t + 1):
            wi = CONV_W - 1 - (t - t2)
            yc = yc + a[t2] * cw_ref[wi:wi + 1, :]
        o_ref[t] = _conv_tail(yc, g_ref, b_ref)


def _conv_sample(zs3, state, cw, cb, g, b, l, bs_blk):
    ts, bs, _ = zs3.shape
    ns = CONV_W - 1
    return pl.pallas_call(
        _conv_sample_kernel,
        grid=(bs // bs_blk,),
        in_specs=[pl.BlockSpec((ts, bs_blk, D_CONV), lambda i: (0, i, Z_UA // D_CONV)),
                  pl.BlockSpec((ts, bs_blk, D_CONV), lambda i: (0, i, Z_UB // D_CONV)),
                  pl.BlockSpec((None, bs_blk, ns, D_CONV), lambda i: (l, i, 0, 0)),
                  pl.BlockSpec((None, CONV_PAD, D_CONV), lambda i: (l, 0, 0))] + _conv_vec_specs(l, 3),
        out_specs=[pl.BlockSpec((ts, bs_blk, D_CONV), lambda i: (0, i, 0)),
                   pl.BlockSpec((ts, bs_blk, D_CONV), lambda i: (0, i, 0))],
        out_shape=[jax.ShapeDtypeStruct((ts, bs, D_CONV), BF16),
                   jax.ShapeDtypeStruct((ts, bs, D_CONV), F32)],
        compiler_params=_cparams("parallel"),
        name="conv_sample",
    )(zs3, zs3, state, cw, cb, g, b)


def _t5_bucket(dist):
    max_exact = N_BUCKETS // 2
    d = jnp.maximum(dist, 0)
    large = max_exact + (jnp.log(jnp.maximum(d, 1).astype(F32) / max_exact)
                         / math.log(MAX_DIST / max_exact) * (N_BUCKETS - max_exact)).astype(jnp.int32)
    return jnp.where(d < max_exact, d, jnp.minimum(large, N_BUCKETS - 1))


def _bias_heads(rel_bias, dist, valid):
    onehot = (_t5_bucket(dist)[..., None] == jnp.arange(N_BUCKETS)).astype(F32)
    bias = jnp.einsum("qkb,bh->qkh", onehot, rel_bias.astype(F32), precision=lax.Precision.HIGHEST)
    bias = jnp.where(valid[..., None], bias, NEG_INF)
    return jnp.transpose(bias, (2, 0, 1))


def _attn_prompt_kernel(sink_ref, q_ref, kc_ref, kp_ref, vc_ref, vp_ref, bias_ref, o_ref, *, l):
    first = pl.program_id(1) == 0
    w = WINDOW
    nq = q_ref.shape[0] // w
    kall = jnp.concatenate([kp_ref[...], kc_ref[...]], axis=0).astype(F32)
    vall = jnp.concatenate([vp_ref[...], vc_ref[...]], axis=0).astype(F32)
    lane = lax.broadcasted_iota(jnp.int32, kall.shape, 1)
    lo = lane < HEAD_DIM
    kroll = pltpu.roll(kall, HEAD_DIM, 1)
    vroll = pltpu.roll(vall, HEAD_DIM, 1)

    def halves(a, aroll, g):
        if g == 0:
            return jnp.where(lo, a, 0.0).astype(BF16), jnp.where(lo, 0.0, aroll).astype(BF16)
        return jnp.where(lo, aroll, 0.0).astype(BF16), jnp.where(lo, 0.0, a).astype(BF16)

    kh = [halves(kall, kroll, g) for g in range(N_KV)]
    vh = [halves(vall, vroll, g) for g in range(N_KV)]
    col = lax.broadcasted_iota(jnp.int32, (w, 4 * w), 1)
    prev_col = (col % (2 * w)) < w
    tiles_per_g = Q_PER_KV // 2
    units = [(qi, tile) for qi in range(nq) for tile in range(N_HEADS // 2)]
    for u0 in range(0, len(units), ATT_GROUP):
        group = units[u0:u0 + ATT_GROUP]
        scores = []
        for qi, tile in group:
            r0, g = qi * w, tile // tiles_per_g
            q = q_ref[r0:r0 + w, tile * LANES:(tile + 1) * LANES]
            kk = jnp.concatenate([kh[g][0][r0:r0 + 2 * w], kh[g][1][r0:r0 + 2 * w]], axis=0)
            s = _dot_nt(q, kk) * (HEAD_DIM ** -0.5) + bias_ref[tile]
            if qi == 0:
                s = jnp.where(jnp.logical_and(first, prev_col), NEG_INF, s)
            scores.append(s)
        probs = []
        for (qi, tile), s in zip(group, scores):
            ps = []
            for half in range(2):
                sh = s[:, half * 2 * w:(half + 1) * 2 * w]
                sink = sink_ref[l, 2 * tile + half]
                mx = jnp.maximum(jnp.max(sh, axis=-1, keepdims=True), sink)
                p = jnp.exp(sh - mx)
                den = jnp.sum(p, axis=-1, keepdims=True) + jnp.exp(sink - mx)
                ps.append((p * (1.0 / den)).astype(BF16))
            probs.append(jnp.concatenate(ps, axis=1))
        for (qi, tile), p in zip(group, probs):
            r0, g = qi * w, tile // tiles_per_g
            vv = jnp.concatenate([vh[g][0][r0:r0 + 2 * w], vh[g][1][r0:r0 + 2 * w]], axis=0)
            o_ref[r0:r0 + w, tile * LANES:(tile + 1) * LANES] = _dot(p, vv).astype(BF16)


def _attn_prompt(z, bias, sinks, l, bp, tp):
    w = WINDOW
    qb = ATT_QB if tp % (ATT_QB * w) == 0 else 1
    ns = tp // (qb * w)
    nb = tp // w
    kvw = N_KV * HEAD_DIM
    qw = N_HEADS * HEAD_DIM

    def cur(col):
        return lambda bb, i: (bb * ns + i, col)

    def prev(col):
        return lambda bb, i: (bb * nb + jnp.maximum(i * qb - 1, 0), col)

    return pl.pallas_call(
        functools.partial(_attn_prompt_kernel, l=l),
        grid=(bp, ns),
        in_specs=[pl.BlockSpec(memory_space=pltpu.SMEM),
                  pl.BlockSpec((qb * w, qw), cur(Z_Q // qw)),
                  pl.BlockSpec((qb * w, kvw), cur(Z_K // kvw)),
                  pl.BlockSpec((w, kvw), prev(Z_K // kvw)),
                  pl.BlockSpec((qb * w, kvw), cur(Z_V // kvw)),
                  pl.BlockSpec((w, kvw), prev(Z_V // kvw)),
                  pl.BlockSpec((N_HEADS // 2, w, 4 * w), lambda bb, i: (0, 0, 0))],
        out_specs=pl.BlockSpec((qb * w, qw), lambda bb, i: (bb * ns + i, 0)),
        out_shape=jax.ShapeDtypeStruct((bp * tp, qw), BF16),
        compiler_params=_cparams("parallel", "parallel"),
        name="attn_prompt",
    )(sinks, z, z, z, z, z, bias)


def _attn_sample_kernel(q_ref, kc_ref, vc_ref, kn_ref, vn_ref, bias_ref, sink_ref, o_ref):
    wb = kc_ref.shape[1]
    ts = kn_ref.shape[2]
    for g in range(N_KV):
        lo = g * HEAD_DIM
        qb = (q_ref[:, g].astype(F32) * (HEAD_DIM ** -0.5)).astype(BF16)
        kc = kc_ref[:, :, lo:lo + HEAD_DIM].astype(BF16)
        vc = vc_ref[:, :, lo:lo + HEAD_DIM].astype(BF16)
        kn = kn_ref[:, g].astype(F32)
        vn = vn_ref[:, g].astype(F32)
        bias = bias_ref[g]
        s_c = jnp.einsum("bqd,bkd->bqk", qb, kc, preferred_element_type=F32) + bias[None, :, :wb]
        qf = qb.astype(F32)
        s_n = [jnp.sum(qf * kn[:, j:j + 1, :], axis=-1, keepdims=True) + bias[None, :, wb + j:wb + j + 1]
               for j in range(ts)]
        sink = sink_ref[g][None]
        mx = jnp.maximum(jnp.max(s_c, axis=-1, keepdims=True), sink)
        for sj in s_n:
            mx = jnp.maximum(mx, sj)
        p_c = jnp.exp(s_c - mx)
        p_n = [jnp.exp(sj - mx) for sj in s_n]
        den = jnp.sum(p_c, axis=-1, keepdims=True) + jnp.exp(sink - mx)
        for pj in p_n:
            den = den + pj
        o = jnp.einsum("bqk,bkd->bqd", (p_c / den).astype(BF16), vc, preferred_element_type=F32)
        for j in range(ts):
            o = o + (p_n[j] / den).astype(BF16).astype(F32) * vn[:, j:j + 1, :]
        o_ref[:, g] = o


def _attn_sample(q4, kc, vc, kn, vn, bias, sinks, l, bs_blk):
    bs, _, rt, _ = q4.shape
    wb = kc.shape[2]
    ts = kn.shape[2]
    kvw = N_KV * HEAD_DIM
    return pl.pallas_call(
        _attn_sample_kernel,
        grid=(bs // bs_blk,),
        in_specs=[pl.BlockSpec((bs_blk, N_KV, rt, HEAD_DIM), lambda i: (i, 0, 0, 0)),
                  pl.BlockSpec((None, bs_blk, wb, kvw), lambda i: (l, i, 0, 0)),
                  pl.BlockSpec((None, bs_blk, wb, kvw), lambda i: (l, i, 0, 0)),
                  pl.BlockSpec((bs_blk, N_KV, ts, HEAD_DIM), lambda i: (i, 0, 0, 0)),
                  pl.BlockSpec((bs_blk, N_KV, ts, HEAD_DIM), lambda i: (i, 0, 0, 0)),
                  pl.BlockSpec((N_KV, rt, wb + ts), lambda i: (0, 0, 0)),
                  pl.BlockSpec((N_KV, rt, 1), lambda i: (0, 0, 0))],
        out_specs=pl.BlockSpec((bs_blk, N_KV, rt, HEAD_DIM), lambda i: (i, 0, 0, 0)),
        out_shape=jax.ShapeDtypeStruct((bs, N_KV, rt, HEAD_DIM), F32),
        compiler_params=_cparams("parallel"),
        name="attn_sample",
    )(q4, kc, vc, kn, vn, bias, sinks)


def _mlstm_kernel(*refs, t_valid, nseq, aliased, carried, chunk_axis, per_seq_inputs=False, first_layer=None):
    if aliased:
        refs = refs[:10] + refs[11:]
    if per_seq_inputs:
        groups = [refs[6 * s:6 * s + 6] for s in range(nseq)]
        if_ref, ifr_ref, q_ref, k_ref, v_ref, o_ref = (tuple(g[k] for g in groups) for k in range(6))
        refs = (None,) * 6 + refs[6 * nseq:]
    else:
        if_ref, ifr_ref, q_ref, k_ref, v_ref, o_ref = refs[:6]
    g_ref, c0_ref, n0_ref, m0_ref, h_ref, c1_ref, n1_ref, m1_ref = refs[6:14]
    c = pl.program_id(chunk_axis)
    L = (ifr_ref[0] if per_seq_inputs else ifr_ref).shape[-1]
    if carried:
        c_in, n_in, m_in = c_out, n_out, m_out = refs[14:]

        @pl.when(c == 0)
        def _():
            c_in[...] = c0_ref[...]
            n_in[...] = n0_ref[...]
            m_in[...] = m0_ref[...]
    else:
        (c_in, n_in, m_in), (c_out, n_out, m_out) = (c0_ref, n0_ref, m0_ref), (c1_ref, n1_ref, m1_ref)
        if first_layer is not None:
            for dd in range(c1_ref.shape[0]):
                if dd != first_layer:
                    c1_ref[dd] = jnp.zeros(c1_ref.shape[1:], F32)
            c_out = c1_ref.at[first_layer]

    def seq(ref, s):
        return ref[s] if isinstance(ref, tuple) else ref.at[s]

    tt = lax.broadcasted_iota(jnp.int32, (L, L), 0)
    ss = lax.broadcasted_iota(jnp.int32, (L, L), 1)
    causal = ss <= tt
    tril = causal.astype(F32)
    triu = (tt <= ss).astype(F32)
    seqs = range(nseq)
    heads = [(s_i, h) for s_i in seqs for h in range(M_HEADS)]
    mxu_sums = L % LANES == 0
    gates = []
    for s_i in seqs:
        ifc = seq(if_ref, s_i)[...]
        ifr = ifr_ref[s_i][0] if per_seq_inputs else ifr_ref[s_i]
        lf_c = _log_sigmoid(ifc)
        lf_r = _log_sigmoid(ifr)
        i_c, i_r = ifc, ifr
        if t_valid < L:
            rc = lax.broadcasted_iota(jnp.int32, (L, LANES), 0) < t_valid
            rr = lax.broadcasted_iota(jnp.int32, (2 * M_HEADS, L), 1) < t_valid
            lf_c = jnp.where(rc, lf_c, 0.0)
            lf_r = jnp.where(rr, lf_r, 0.0)
            i_c = jnp.where(rc, i_c, NEG_INF)
            i_r = jnp.where(rr, i_r, NEG_INF)
        gates.append((lf_c, lf_r, i_c, i_r))
    f_cs = [_dot_hi(tril, g[0]) for g in gates]
    f_rs = [_dot_hi(g[1], triu) for g in gates]
    state = {(s_i, h): (c_in[s_i, h], n_in[s_i, h], m_in[s_i, h:h + 1, :]) for s_i, h in heads}

    def wide(col):
        return jnp.concatenate([col] * (L // LANES), axis=1) if L >= LANES else col[:, :L]

    st1 = {}
    for s_i, h in heads:
        _, nrow, m0 = state[s_i, h]
        fc = jnp.broadcast_to(f_cs[s_i][:, M_HEADS + h:M_HEADS + h + 1], (L, LANES))
        fr = f_rs[s_i][M_HEADS + h:M_HEADS + h + 1, :]
        ir = gates[s_i][3][h:h + 1, :]
        dm = jnp.where(causal, wide(fc) - fr + ir, NEG_INF)
        m_t = jnp.maximum(m0 + fc, jnp.max(dm, axis=-1, keepdims=True))
        st1[s_i, h] = (fc, dm, m_t, jnp.exp(m0 + fc - m_t))
    st2 = {}
    for s_i, h in heads:
        lo = h * M_DK
        fc, dm, m_t, inter = st1[s_i, h]
        qb = seq(q_ref, s_i)[:, lo:lo + M_DK]
        kf = seq(k_ref, s_i)[:, lo:lo + M_DK].astype(F32) * (M_DK ** -0.5)
        vf = seq(v_ref, s_i)[:, lo:lo + M_DV].astype(F32)
        kb, vb = kf.astype(BF16), vf.astype(BF16)
        if mxu_sums:
            n_rows = jnp.broadcast_to(state[s_i, h][1], (M_DK, M_DK)).astype(BF16)
            qk = _dot_nt(qb, jnp.concatenate([kb, n_rows], axis=0))
            sc, qn_rep = qk[:, :L] * jnp.exp(dm - wide(m_t)), qk[:, L:]
        else:
            sc, qn_rep = _dot_nt(qb, kb) * jnp.exp(dm - wide(m_t)), None
        st2[s_i, h] = (qb, kf, vf, kb, vb, sc, qn_rep)
    st3 = {}
    for s_i, h in heads:
        cm, nrow, m0 = state[s_i, h]
        fc, dm, m_t, inter = st1[s_i, h]
        qb, kf, vf, kb, vb, sc, qn_rep = st2[s_i, h]
        if mxu_sums:
            sv = _dot(sc.astype(BF16), jnp.concatenate([vb, jnp.ones((L, M_DV), BF16)], axis=1))
            num = inter * _dot(qb, cm.astype(BF16)) + sv[:, :M_DV]
            qn = inter * qn_rep + sv[:, M_DV:]
            floor = jnp.exp(-m_t)
        else:
            num = inter * _dot(qb, cm.astype(BF16)) + _dot(sc.astype(BF16), vb)
            qn = (inter * jnp.sum(qb.astype(F32) * nrow, axis=-1, keepdims=True)
                  + jnp.sum(sc, axis=-1, keepdims=True))
            floor = jnp.exp(-m_t)
        st3[s_i, h] = num / jnp.maximum(jnp.abs(qn), floor)
    new_state = {}
    for s_i, h in heads:
        cm, nrow, m0 = state[s_i, h]
        fc, dm, m_t, inter = st1[s_i, h]
        qb, kf, vf, kb, vb, sc, _ = st2[s_i, h]
        ic = jnp.broadcast_to(gates[s_i][2][:, h:h + 1], (L, LANES))
        m_end = m_t[L - 1:L, :]
        f_end = fc[L - 1:L, :]
        decay = jnp.exp(m0 + f_end - m_end)
        w_s = jnp.exp(f_end - fc + ic - m_end)
        new_state[s_i, h] = (decay * cm + _dot_tn(kb, (w_s * vf).astype(BF16)),
                             decay * nrow + jnp.sum(w_s * kf, axis=0, keepdims=True),
                             m_end)
    for s_i, h in heads:
        lo = h * M_DK
        hn = _norm(st3[s_i, h]) * g_ref[:, lo:lo + M_DV]
        gate = _sigmoid(seq(o_ref, s_i)[:, lo:lo + M_DV].astype(F32))
        seq(h_ref, s_i)[:, lo:lo + M_DV] = (gate * hn).astype(BF16)
    for s_i, h in heads:
        c_out[s_i, h], n_out[s_i, h], m_out[s_i, h:h + 1, :] = new_state[s_i, h]
    if not carried:
        for s_i in seqs:
            m_out[s_i, M_HEADS:, :] = jnp.zeros((M_HEADS, LANES), F32)

    if carried:
        @pl.when(c == pl.num_programs(chunk_axis) - 1)
        def _():
            c1_ref[...] = c_out[...]
            n1_ref[...] = n_out[...]
            m1_ref[...] = m_out[...]


def _mlstm_state_specs(nseq, l_state):
    if l_state is None:
        c_spec = pl.BlockSpec((nseq, M_HEADS, M_DK, M_DV), lambda b, c: (b, 0, 0, 0))
    else:
        c_spec = pl.BlockSpec((None, nseq, M_HEADS, M_DK, M_DV), lambda b, c: (l_state, b, 0, 0, 0))
    return (c_spec,
            pl.BlockSpec((nseq, M_HEADS, 1, M_DK), lambda b, c: (b, 0, 0, 0)),
            pl.BlockSpec((nseq, 2 * M_HEADS, LANES), lambda b, c: (b, 0, 0)))


def _mlstm_scratch(nseq):
    return [pltpu.VMEM((nseq, M_HEADS, M_DK, M_DV), F32),
            pltpu.VMEM((nseq, M_HEADS, 1, M_DK), F32),
            pltpu.VMEM((nseq, 2 * M_HEADS, LANES), F32)]


def _mlstm_prompt(z, zif, ifr, gamma, l, bp, nc, L):
    hw = M_HEADS * M_DK
    zero = lambda *s: jnp.zeros(s, F32)
    in_specs, args = [], []
    for b in range(bp):
        def rows(col, b=b):
            return lambda c: (b * nc + c, col)

        in_specs += [pl.BlockSpec((L, LANES), rows(0)),
                     pl.BlockSpec((1, 2 * M_HEADS, L), lambda c, b=b: (b * nc + c, 0, 0)),
                     pl.BlockSpec((L, hw), rows(Z_MQ // hw)),
                     pl.BlockSpec((L, hw), rows(Z_MK // hw)),
                     pl.BlockSpec((L, hw), rows(Z_MV // hw)),
                     pl.BlockSpec((L, hw), rows(Z_MO // hw))]
        args += [zif, ifr, z, z, z, z]
    state_specs = [pl.BlockSpec((bp, M_HEADS, M_DK, M_DV), lambda c: (0, 0, 0, 0)),
                   pl.BlockSpec((bp, M_HEADS, 1, M_DK), lambda c: (0, 0, 0, 0)),
                   pl.BlockSpec((bp, 2 * M_HEADS, LANES), lambda c: (0, 0, 0))]
    kern = functools.partial(_mlstm_kernel, t_valid=L, nseq=bp, aliased=False, carried=True, chunk_axis=0,
                             per_seq_inputs=True)
    outs = pl.pallas_call(
        kern,
        grid=(nc,),
        in_specs=in_specs + [pl.BlockSpec((None, 1, hw), lambda c: (l, 0, 0))] + state_specs,
        out_specs=[pl.BlockSpec((bp, L, hw), lambda c: (0, c, 0))] + state_specs,
        out_shape=[jax.ShapeDtypeStruct((bp, nc * L, hw), BF16),
                   jax.ShapeDtypeStruct((bp, M_HEADS, M_DK, M_DV), F32),
                   jax.ShapeDtypeStruct((bp, M_HEADS, 1, M_DK), F32),
                   jax.ShapeDtypeStruct((bp, 2 * M_HEADS, LANES), F32)],
        scratch_shapes=_mlstm_scratch(bp),
        compiler_params=_cparams("arbitrary"),
        name="mlstm_prompt",
    )(*args, gamma, zero(bp, M_HEADS, M_DK, M_DV), zero(bp, M_HEADS, 1, M_DK), zero(bp, 2 * M_HEADS, LANES))
    return (outs[0].reshape(bp * nc * L, hw),) + tuple(outs[1:])


def _mlstm_sample(zm3, if3, ifr, gamma, c_all, n0, m0x, c_out_prev, l, t_valid, nseq):
    bs, L, _ = zm3.shape
    depth = c_all.shape[0]
    hw = M_HEADS * M_DK
    aliased = c_out_prev is not None

    def blk(col):
        return pl.BlockSpec((nseq, L, hw), lambda b, c: (b, 0, col))

    c_in, n_spec, m_spec = _mlstm_state_specs(nseq, l)
    in_specs = [pl.BlockSpec((nseq, L, LANES), lambda b, c: (b, 0, 0)),
                pl.BlockSpec((nseq, 2 * M_HEADS, L), lambda b, c: (b, 0, 0)),
                blk(0), blk(1), blk(2), blk(3),
                pl.BlockSpec((None, 1, hw), lambda b, c: (l, 0, 0)),
                c_in, n_spec, m_spec]
    args = [if3, ifr, zm3, zm3, zm3, zm3, gamma, c_all, n0, m0x]
    aliases = {}
    if aliased:
        in_specs.append(pl.BlockSpec(memory_space=pl.ANY))
        args.append(c_out_prev)
        aliases = {len(args) - 1: 1}
        c_out = c_in
    else:
        c_out = pl.BlockSpec((depth, nseq, M_HEADS, M_DK, M_DV), lambda b, c: (0, b, 0, 0, 0))
    kern = functools.partial(_mlstm_kernel, t_valid=t_valid, nseq=nseq, aliased=aliased, carried=False,
                             chunk_axis=1, first_layer=None if aliased else l)
    return pl.pallas_call(
        kern,
        grid=(bs // nseq, 1),
        in_specs=in_specs,
        out_specs=[pl.BlockSpec((nseq, L, hw), lambda b, c: (b, 0, 0)), c_out, n_spec, m_spec],
        out_shape=[jax.ShapeDtypeStruct((bs, L, hw), BF16),
                   jax.ShapeDtypeStruct((depth, bs, M_HEADS, M_DK, M_DV), F32),
                   jax.ShapeDtypeStruct((bs, M_HEADS, 1, M_DK), F32),
                   jax.ShapeDtypeStruct((bs, 2 * M_HEADS, LANES), F32)],
        input_output_aliases=aliases,
        compiler_params=_cparams("parallel", "arbitrary"),
        name="mlstm_sample",
    )(*args)


def _m_state_in(m):
    lead = m.shape[:-1]
    mx = jnp.zeros(lead + (2 * M_HEADS, LANES), F32)
    return mx.at[..., :M_HEADS, :].set(jnp.broadcast_to(m[..., None], lead + (M_HEADS, LANES)))


def _merge_kernel(*refs, np_tiles, alpha, route):
    (cp_ref, cs_ref, ap_ref, as_ref, mp_ref, ms_ref, g0_ref, g1_ref, g2_ref, xp_ref, xs_ref,
     gtp_ref, gts_ref, shp_ref, shs_ref, scp_ref, scs_ref,
     wc_ref, wa_ref, wm_ref, wo_ref, pg_ref, pb_ref) = refs[:23]
    if route:
        rw_ref, rb_ref, x1_ref, h_ref, route_ref = refs[23:]
    else:
        x1_ref, h_ref = refs[23:]
    is_s = pl.program_id(0) >= np_tiles

    def gate(ref):
        return _sigmoid(ref[...].astype(F32))

    def run(c_ref, a_ref, m_ref, x_ref, gt_ref, sh_ref, sc_ref):
        y = (gate(g0_ref) * _dot(c_ref[...], wc_ref[...])
             + gate(g1_ref) * _dot(a_ref[...], wa_ref[...])
             + gate(g2_ref) * _dot(m_ref[...], wm_ref[...]))
        mix = _dot(y.astype(BF16), wo_ref[...])
        x1 = _norm(alpha * x_ref[...] + gt_ref[...] * mix) * pg_ref[...] + pb_ref[...]
        x1_ref[...] = x1
        h = _norm(x1) * (1.0 + sc_ref[...]) + sh_ref[...]
        h_ref[...] = h.astype(h_ref.dtype)
        if route:
            h_hi, h_lo = _split_bf16(h)
            w_hi, w_lo = _split_bf16(rw_ref[...])
            logits = _dot(h_hi, w_hi) + (_dot(h_hi, w_lo) + _dot(h_lo, w_hi)) + rb_ref[...]
            lane = lax.broadcasted_iota(jnp.int32, logits.shape, 1)
            m1 = jnp.max(logits, axis=-1, keepdims=True)
            e1 = jnp.min(jnp.where(logits == m1, lane, LANES), axis=-1, keepdims=True)
            l2 = jnp.where(lane == e1, NEG_INF, logits)
            m2 = jnp.max(l2, axis=-1, keepdims=True)
            e2 = jnp.min(jnp.where(l2 == m2, lane, LANES), axis=-1, keepdims=True)
            ex = jnp.exp(m2 - m1)
            w1 = 1.0 / (1.0 + ex)
            w2 = ex / (1.0 + ex)
            out = jnp.where(lane == 0, e1.astype(F32),
                            jnp.where(lane == 1, e2.astype(F32),
                                      jnp.where(lane == 2, w1, jnp.where(lane == 3, w2, 0.0))))
            route_ref[...] = out

    @pl.when(is_s)
    def _():
        run(cs_ref, as_ref, ms_ref, xs_ref, gts_ref, shs_ref, scs_ref)

    @pl.when(jnp.logical_not(is_s))
    def _():
        run(cp_ref, ap_ref, mp_ref, xp_ref, gtp_ref, shp_ref, scp_ref)


def _merge(cp, cs, ap, as_, mp, ms, z, xp, xs, mod_p, mod_s, wc, wa, wm, wo, pg, pb, l, rm, alpha, router):
    n, d = z.shape[0], xp.shape[1]
    hw = D_CONV
    route = router is not None
    tm = rm.t

    def pblk():
        return pl.BlockSpec((tm, hw), lambda i: (rm.prompt(i), 0))

    def sblk():
        return pl.BlockSpec((tm, hw), lambda i: (rm.sample(i), 0))

    def zg(k):
        return pl.BlockSpec((tm, d), lambda i: (i, Z_G // d + k))

    def lw(a):
        return pl.BlockSpec((None,) + a.shape[1:], lambda i: (l,) + (0,) * (a.ndim - 1))

    post = pl.BlockSpec((None, None, 1, d), lambda i: (l, 0, 0, 0))
    in_specs = [pblk(), sblk(), pblk(), sblk(), pblk(), sblk(), zg(0), zg(1), zg(2),
                *rm.x_specs(xs, d),
                *rm.mod_specs(l, 2, d), *rm.mod_specs(l, 3, d), *rm.mod_specs(l, 4, d),
                lw(wc), lw(wa), lw(wm), lw(wo), post, post]
    args = [cp, cs, ap, as_, mp, ms, z, z, z, xp, xs, mod_p, mod_s, mod_p, mod_s, mod_p, mod_s,
            wc, wa, wm, wo, pg, pb]
    out_specs = [pl.BlockSpec((tm, d), lambda i: (i, 0)), pl.BlockSpec((tm, d), lambda i: (i, 0))]
    out_shape = [jax.ShapeDtypeStruct((n, d), F32), jax.ShapeDtypeStruct((n, d), F32 if route else BF16)]
    if route:
        rw, rb, lj = router
        in_specs += [pl.BlockSpec((None, d, LANES), lambda i: (lj, 0, 0)),
                     pl.BlockSpec((None, 1, LANES), lambda i: (lj, 0, 0))]
        args += [rw, rb]
        out_specs.append(pl.BlockSpec((tm, LANES), lambda i: (i, 0)))
        out_shape.append(jax.ShapeDtypeStruct((n, LANES), F32))
    kern = functools.partial(_merge_kernel, np_tiles=rm.np_tiles, alpha=alpha, route=route)
    return pl.pallas_call(
        kern, grid=(n // tm,), in_specs=in_specs, out_specs=out_specs, out_shape=out_shape,
        compiler_params=_cparams("parallel"), name="merge_route" if route else "merge",
    )(*args)


def _post_residual(x_ref, gp_ref, gs_ref, f, pg_ref, pb_ref, is_s, alpha):
    return _norm(alpha * x_ref[...] + _pick(is_s, gp_ref, gs_ref) * f) * pg_ref[...] + pb_ref[...]


def _ffn_kernel(h_ref, wg_ref, wu_ref, wd_ref, x_ref, gp_ref, gs_ref, pg_ref, pb_ref, o_ref, acc,
                *, alpha, np_tiles):
    j = pl.program_id(1)
    h = h_ref[...]
    a = (_silu(_dot(h, wg_ref[...])) * _dot(h, wu_ref[...])).astype(BF16)
    part = _dot(a, wd_ref[...])

    @pl.when(j == 0)
    def _():
        acc[...] = part

    @pl.when(j > 0)
    def _():
        acc[...] = acc[...] + part

    @pl.when(j == pl.num_programs(1) - 1)
    def _():
        is_s = pl.program_id(0) >= np_tiles
        o_ref[...] = _post_residual(x_ref, gp_ref, gs_ref, acc[...], pg_ref, pb_ref, is_s, alpha)


def _ffn_dense(h, wg, wu, wd, x, mod_p, mod_s, pg, pb, l, lj, rm, tf, alpha):
    n, d = x.shape
    f = wg.shape[2]
    tm = rm.t
    post = pl.BlockSpec((None, None, 1, d), lambda i, j: (l, 1, 0, 0))
    return pl.pallas_call(
        functools.partial(_ffn_kernel, alpha=alpha, np_tiles=rm.np_tiles),
        grid=(n // tm, f // tf),
        in_specs=[pl.BlockSpec((tm, d), lambda i, j: (i, 0)),
                  pl.BlockSpec((None, d, tf), lambda i, j: (lj, 0, j)),
                  pl.BlockSpec((None, d, tf), lambda i, j: (lj, 0, j)),
                  pl.BlockSpec((None, tf, d), lambda i, j: (lj, j, 0)),
                  pl.BlockSpec((tm, d), lambda i, j: (i, 0)),
                  *rm.mod_specs(l, 5, d), post, post],
        out_specs=pl.BlockSpec((tm, d), lambda i, j: (i, 0)),
        out_shape=jax.ShapeDtypeStruct((n, d), F32),
        scratch_shapes=[pltpu.VMEM((tm, d), F32)],
        compiler_params=_cparams("parallel", "arbitrary"),
        name="ffn_dense",
    )(h, wg, wu, wd, x, mod_p, mod_s, pg, pb)


def _rank_kernel(route_ref, rank_ref, tot_ref, carry):
    i = pl.program_id(0)
    tm = route_ref.shape[0]

    @pl.when(i == 0)
    def _():
        carry[...] = jnp.zeros_like(carry)

    r = route_ref[...]
    lane = lax.broadcasted_iota(jnp.int32, (tm, LANES), 1)
    e1 = r[:, 0:1].astype(jnp.int32)
    e2 = r[:, 1:2].astype(jnp.int32)
    hit1 = lane == e1
    hit2 = lane == e2
    onehot = jnp.where(jnp.logical_or(hit1, hit2), 1.0, 0.0)
    tt = lax.broadcasted_iota(jnp.int32, (tm, tm), 0)
    ss = lax.broadcasted_iota(jnp.int32, (tm, tm), 1)
    before = jnp.where(ss < tt, 1.0, 0.0).astype(BF16)
    cnt = _dot(before, onehot.astype(BF16)) + carry[0:1, :]
    r1 = jnp.sum(jnp.where(hit1, cnt, 0.0), axis=-1, keepdims=True)
    r2 = jnp.sum(jnp.where(hit2, cnt, 0.0), axis=-1, keepdims=True)
    rank_ref[...] = jnp.where(lane == 0, r1, jnp.where(lane == 1, r2, 0.0))
    carry[...] = carry[...] + jnp.sum(onehot, axis=0, keepdims=True)
    tot_ref[...] = carry[...]


def _moe_rank(route, tm):
    n = route.shape[0]
    return pl.pallas_call(
        _rank_kernel,
        grid=(n // tm,),
        in_specs=[pl.BlockSpec((tm, LANES), lambda i: (i, 0))],
        out_specs=[pl.BlockSpec((tm, LANES), lambda i: (i, 0)),
                   pl.BlockSpec((8, LANES), lambda i: (0, 0))],
        out_shape=[jax.ShapeDtypeStruct((n, LANES), F32), jax.ShapeDtypeStruct((8, LANES), F32)],
        scratch_shapes=[pltpu.VMEM((8, LANES), F32)],
        compiler_params=_cparams("arbitrary"),
        name="moe_rank",
    )(route)


def _row_copy(src, s, dst, t, sem):
    return pltpu.make_async_copy(src.at[pl.ds(s, 1)], dst.at[pl.ds(t, 1)], sem)


def _dispatch_kernel(d1_ref, d2_ref, zl_ref, h_ref, xs_hbm, zbuf, stage, sem, zsem):
    i = pl.program_id(0)
    tm = h_ref.shape[0]

    @pl.when(i == 0)
    def _():
        zbuf[...] = jnp.zeros_like(zbuf)

        def zero_copy(k):
            row = pl.multiple_of(jnp.maximum(zl_ref[k], 0), MOE_SUB)
            return pltpu.make_async_copy(zbuf, xs_hbm.at[pl.ds(row, MOE_SUB)], zsem)

        def start(k, carry):
            @pl.when(zl_ref[k] >= 0)
            def _():
                zero_copy(k).start()
            return carry

        def wait(k, carry):
            @pl.when(zl_ref[k] >= 0)
            def _():
                zero_copy(k).wait()
            return carry

        lax.fori_loop(0, zl_ref.shape[0], start, 0)
        lax.fori_loop(0, zl_ref.shape[0], wait, 0)

    slot = i % 2
    stage[slot] = h_ref[...]

    def issue(r, carry):
        _row_copy(stage.at[slot], r, xs_hbm, d1_ref[i * tm + r], sem.at[slot]).start(priority=0)
        _row_copy(stage.at[slot], r, xs_hbm, d2_ref[i * tm + r], sem.at[slot]).start(priority=1)
        return carry

    lax.fori_loop(0, tm, issue, 0, unroll=8)

    def drain(s):
        for _ in range(2):
            pltpu.make_async_copy(stage.at[s], xs_hbm.at[pl.ds(0, tm)], sem.at[s]).wait()

    @pl.when(i > 0)
    def _():
        drain(1 - slot)

    @pl.when(i == pl.num_programs(0) - 1)
    def _():
        drain(slot)


def _moe_dispatch(dest1, dest2, zlist, h, n_rows, tm):
    n, d = h.shape
    grid_spec = pltpu.PrefetchScalarGridSpec(
        num_scalar_prefetch=3,
        grid=(n // tm,),
        in_specs=[pl.BlockSpec((tm, d), lambda i, *_: (i, 0))],
        out_specs=pl.BlockSpec(memory_space=pl.ANY),
        scratch_shapes=[pltpu.VMEM((MOE_SUB, d), F32), pltpu.VMEM((2, tm, d), F32),
                        pltpu.SemaphoreType.DMA((2,)), pltpu.SemaphoreType.DMA(())],
    )
    return pl.pallas_call(
        _dispatch_kernel, grid_spec=grid_spec,
        out_shape=jax.ShapeDtypeStruct((n_rows, d), F32),
        compiler_params=_cparams("arbitrary"),
        name="moe_dispatch",
    )(dest1, dest2, zlist, h)


def _expert_kernel(blk_e_ref, nvalid_ref, xs_ref, wg_ref, wu_ref, wd_ref, y_ref, xb, acc):
    b = pl.program_id(0)
    j = pl.program_id(1)
    tb = xb.shape[0]
    nsub = tb // MOE_SUB
    nv = nvalid_ref[b]
    used = (nv + MOE_SUB - 1) // MOE_SUB

    def run(rows):
        wg = wg_ref[...].astype(BF16)
        wu = wu_ref[...].astype(BF16)
        wd = wd_ref[...].astype(BF16)

        @pl.when(j == 0)
        def _():
            xb[rows, :] = xs_ref[rows, :].astype(BF16)

        x = xb[rows, :]
        a = (_silu(_dot(x, wg)) * _dot(x, wu)).astype(BF16)
        part = _dot(a, wd)

        @pl.when(j == 0)
        def _():
            acc[rows, :] = part

        @pl.when(j > 0)
        def _():
            acc[rows, :] = acc[rows, :] + part

        @pl.when(j == pl.num_programs(1) - 1)
        def _():
            y_ref[rows, :] = acc[rows, :]

    for m in range(1, nsub + 1):
        @pl.when(used == m)
        def _():
            run(pl.ds(0, m * MOE_SUB))

    for s in range(nsub):
        @pl.when(jnp.logical_and(s >= used, j == pl.num_programs(1) - 1))
        def _():
            y_ref[pl.ds(s * MOE_SUB, MOE_SUB), :] = jnp.zeros((MOE_SUB, y_ref.shape[1]), F32)


def _moe_experts(blk_e, nvalid, xs, wg, wu, wd, lj, tb, tf):
    n_rows, d = xs.shape
    f = wg.shape[3]
    nj = f // tf

    def jm(b, j, nv):
        return jnp.where(nv[b] > 0, j, nj - 1)

    grid_spec = pltpu.PrefetchScalarGridSpec(
        num_scalar_prefetch=2,
        grid=(n_rows // tb, nj),
        in_specs=[pl.BlockSpec((tb, d), lambda b, j, be, nv: (b, 0)),
                  pl.BlockSpec((None, None, d, tf), lambda b, j, be, nv: (lj, be[b], 0, jm(b, j, nv))),
                  pl.BlockSpec((None, None, d, tf), lambda b, j, be, nv: (lj, be[b], 0, jm(b, j, nv))),
                  pl.BlockSpec((None, None, tf, d), lambda b, j, be, nv: (lj, be[b], jm(b, j, nv), 0))],
        out_specs=pl.BlockSpec((tb, d), lambda b, j, be, nv: (b, 0)),
        scratch_shapes=[pltpu.VMEM((tb, d), BF16), pltpu.VMEM((tb, d), F32)],
    )
    return pl.pallas_call(
        _expert_kernel, grid_spec=grid_spec,
        out_shape=jax.ShapeDtypeStruct((n_rows, d), F32),
        compiler_params=_cparams("parallel", "arbitrary"),
        name="moe_experts",
    )(blk_e, nvalid, xs, wg, wu, wd)


def _combine_kernel(d1_ref, d2_ref, ys_hbm, route_ref, x_ref, gp_ref, gs_ref, pg_ref, pb_ref, *rest,
                    alpha, np_tiles, split):
    (*outs, y1, y2, sem1, sem2) = rest
    i = pl.program_id(0)
    tm = x_ref.shape[0]
    slot = i % 2

    last = pl.num_programs(0) - 1

    def issue(tile, s, r):
        _row_copy(ys_hbm, d1_ref[tile * tm + r], y1.at[s], r, sem1.at[s]).start(priority=0)
        _row_copy(ys_hbm, d2_ref[tile * tm + r], y2.at[s], r, sem2.at[s]).start(priority=1)

    def drain(s):
        pltpu.make_async_copy(ys_hbm.at[pl.ds(0, tm)], y1.at[s], sem1.at[s]).wait()
        pltpu.make_async_copy(ys_hbm.at[pl.ds(0, tm)], y2.at[s], sem2.at[s]).wait()

    @pl.when(i == 0)
    def _():
        lax.fori_loop(0, tm, lambda r, c: (issue(0, 0, r), c)[1], 0, unroll=8)

    drain(slot)
    nxt = jnp.minimum(i + 1, last)
    for r in range(tm):
        issue(nxt, 1 - slot, r)
    r = route_ref[...]
    f = y1[slot] * r[:, 2:3] + y2[slot] * r[:, 3:4]
    out = _post_residual(x_ref, gp_ref, gs_ref, f, pg_ref, pb_ref, i >= np_tiles, alpha)
    if split:
        op_ref, os_ref = outs

        @pl.when(i < np_tiles)
        def _():
            op_ref[...] = out

        @pl.when(i >= np_tiles)
        def _():
            os_ref[...] = out
    else:
        outs[0][...] = out

    @pl.when(i == last)
    def _():
        drain(1 - slot)


def _moe_combine(dest1, dest2, ys, route, x, mod_p, mod_s, pg, pb, l, rm, alpha, split):
    n, d = x.shape
    tm = rm.t
    if split:
        n_p = rm.np_tiles * tm
        out_specs = [pl.BlockSpec((tm, d), lambda i, *_: (rm.prompt(i), 0)),
                     pl.BlockSpec((tm, d), lambda i, *_: (rm.sample(i), 0))]
        out_shape = [jax.ShapeDtypeStruct((n_p, d), F32), jax.ShapeDtypeStruct((n - n_p, d), F32)]
    else:
        out_specs = pl.BlockSpec((tm, d), lambda i, *_: (i, 0))
        out_shape = jax.ShapeDtypeStruct((n, d), F32)
    post = pl.BlockSpec((None, None, 1, d), lambda i, *_: (l, 1, 0, 0))
    grid_spec = pltpu.PrefetchScalarGridSpec(
        num_scalar_prefetch=2,
        grid=(n // tm,),
        in_specs=[pl.BlockSpec(memory_space=pl.ANY),
                  pl.BlockSpec((tm, LANES), lambda i, *_: (i, 0)),
                  pl.BlockSpec((tm, d), lambda i, *_: (i, 0)),
                  *rm.mod_specs(l, 5, d), post, post],
        out_specs=out_specs,
        scratch_shapes=[pltpu.VMEM((2, tm, d), F32), pltpu.VMEM((2, tm, d), F32),
                        pltpu.SemaphoreType.DMA((2,)), pltpu.SemaphoreType.DMA((2,))],
    )
    return pl.pallas_call(
        functools.partial(_combine_kernel, alpha=alpha, np_tiles=rm.np_tiles, split=split), grid_spec=grid_spec,
        out_shape=out_shape,
        compiler_params=_cparams("arbitrary"),
        name="moe_combine",
    )(dest1, dest2, ys, route, x, mod_p, mod_s, pg, pb)


def _moe_ffn(h, route, wg, wu, wd, x, mod_p, mod_s, pg, pb, l, lj, rm, tb, tf, alpha, split):
    n = h.shape[0]
    rank, tot = _moe_rank(route, rm.t)
    counts = tot[0, :N_EXPERTS].astype(jnp.int32)
    padded = (counts + tb - 1) // tb * tb
    pad_end = jnp.cumsum(padded)
    pad_start = pad_end - padded
    experts = jnp.arange(N_EXPERTS, dtype=jnp.int32)

    def slot(col):
        e = route[:, col].astype(jnp.int32)
        start = jnp.sum(jnp.where(e[:, None] == experts[None, :], pad_start[None, :], 0), axis=1)
        return start + rank[:, col].astype(jnp.int32)

    dest1, dest2 = slot(0), slot(1)
    n_blocks = -(-(2 * n + N_EXPERTS * (tb - 1)) // tb)
    blk_start = jnp.arange(n_blocks, dtype=jnp.int32) * tb
    blk_e = jnp.minimum(jnp.sum(pad_end[None, :] <= blk_start[:, None], axis=1), N_EXPERTS - 1).astype(jnp.int32)
    own_end = jnp.sum(jnp.where(blk_e[:, None] == experts[None, :], (pad_start + counts)[None, :], 0), axis=1)
    nvalid = jnp.clip(own_end - blk_start, 0, tb).astype(jnp.int32)
    nvalid = jnp.where(blk_start < pad_end[-1], nvalid, 0)
    per_blk = tb // MOE_SUB
    sub_start = jnp.arange(n_blocks * per_blk, dtype=jnp.int32) * MOE_SUB
    sub_room = jnp.repeat(blk_start + nvalid, per_blk) - sub_start
    zlist = jnp.where(sub_room < MOE_SUB, sub_start, -1).astype(jnp.int32)
    xs = _moe_dispatch(dest1, dest2, zlist, h, n_blocks * tb, rm.t)
    ys = _moe_experts(blk_e, nvalid, xs, wg, wu, wd, lj, tb, tf)
    return _moe_combine(dest1, dest2, ys, route, x, mod_p, mod_s, pg, pb, l, rm, alpha, split)


def _pack_w_in(w_in, b_in):
    depth, d, _ = w_in.shape
    q_end = 2 * D_CONV + N_HEADS * HEAD_DIM
    k_end = q_end + N_KV * HEAD_DIM
    a_end = k_end + N_KV * HEAD_DIM
    m_end = a_end + 4 * M_HEADS * M_DK
    if_end = m_end + 2 * M_HEADS

    def pack(a):
        return jnp.concatenate([a[..., if_end:], a[..., :q_end], a[..., a_end:m_end], a[..., q_end:a_end]], axis=-1)

    def gates(a):
        return jnp.pad(a[..., m_end:if_end], [(0, 0)] * (a.ndim - 1) + [(0, LANES - 2 * M_HEADS)])

    return (pack(w_in).astype(BF16), pack(b_in).reshape(depth, 1, Z_W),
            gates(w_in).astype(BF16), gates(b_in).reshape(depth, 1, LANES))


def kernel(x_prompt, x_sample, cache_swa_k, cache_swa_v, state_conv, state_mlstm_C, state_mlstm_n, state_mlstm_m, c_prompt, c_sample, w_ada, b_ada, w_in, b_in, conv_w, conv_b, conv_ln_g, conv_ln_b, w_conv_out, attn_sinks, rel_bias, w_attn_out, m_norm_g, w_m_out, w_out, post_ln_g, post_ln_b, ffn_w_gate, ffn_w_up, ffn_w_down, router_w, router_b, moe_w_gate, moe_w_up, moe_w_down):
    bp, tp, d = x_prompt.shape
    bs, ts, _ = x_sample.shape
    depth = w_ada.shape[0]
    alpha = (2 * depth) ** 0.25
    n_p, n_s = bp * tp, bs * ts
    tm = n_s
    assert d == D_MODEL and tp % tm == 0 and tp % WINDOW == 0 and tm % 32 == 0
    rm = _RowMap(tm, bp, tp, n_p, n_s)
    rm_half = _RowMap(tm // 2, bp, tp, n_p, n_s)
    wb = cache_swa_k.shape[2]
    big = n_p >= 4096
    tc = 512 if big else tm
    lm = 256 if big else min(tp, 128)
    tb = 1024 if big else 2 * MOE_SUB
    bs_blk = 32 if bs % 32 == 0 else bs
    bs_att = 16 if bs % 16 == 0 else bs
    bs_m = 4 if bs % 4 == 0 else 1
    lts = 16

    n = n_p + n_s
    x = (x_prompt.reshape(n_p, d), jnp.transpose(x_sample, (1, 0, 2)).reshape(n_s, d))

    nc_rows = -(-(bp + bs) // 8) * 8
    c_all = jnp.zeros((nc_rows, d), F32).at[:bp].set(c_prompt).at[bp:bp + bs].set(c_sample)
    mod = _ada_mod(c_all, w_ada, b_ada)
    mod_p = mod[:, :bp].reshape(depth, bp, 1, 6 * d)
    mod_s = jnp.tile(mod[:, bp:bp + bs], (1, ts, 1))

    w_in_p, b_in_p, w_if, b_if = _pack_w_in(w_in, b_in)
    wc_b, wa_b, wm_b, wo_b = (w.astype(BF16) for w in (w_conv_out, w_attn_out, w_m_out, w_out))
    fg_b, fu_b, fd_b = (w.astype(BF16) for w in (ffn_w_gate, ffn_w_up, ffn_w_down))
    cw_pad = jnp.pad(conv_w, ((0, 0), (0, CONV_PAD - CONV_W), (0, 0)))
    cvecs = [v.reshape(depth, 1, D_CONV) for v in (conv_b, conv_ln_g, conv_ln_b)]
    rw_pad = jnp.pad(router_w, ((0, 0), (0, 0), (0, LANES - N_EXPERTS)))
    rb_pad = jnp.pad(router_b, ((0, 0), (0, LANES - N_EXPERTS)), constant_values=NEG_INF)[:, None, :]
    pg = post_ln_g.reshape(depth, 2, 1, d)
    pb = post_ln_b.reshape(depth, 2, 1, d)
    gamma = m_norm_g.reshape(depth, 1, M_HEADS * M_DV)
    sinks = attn_sinks.astype(F32)

    qi = jnp.arange(WINDOW)[:, None]
    kj = jnp.arange(2 * WINDOW)[None, :]
    dist_p = qi + WINDOW - kj
    bh = _bias_heads(rel_bias, dist_p, (dist_p >= 0) & (dist_p < WINDOW))
    bias_p = bh.reshape(N_HEADS // 2, 2, WINDOW, 2 * WINDOW).transpose(0, 2, 1, 3).reshape(
        N_HEADS // 2, WINDOW, 4 * WINDOW)
    dist_s = jnp.arange(ts)[:, None] + wb - jnp.arange(wb + ts)[None, :]
    bias_s = _bias_heads(rel_bias, dist_s, (dist_s >= 0) & (dist_s < WINDOW)).reshape(
        N_KV, Q_PER_KV * ts, wb + ts)
    kc_all = cache_swa_k.reshape(depth, bs, wb, N_KV * HEAD_DIM)
    vc_all = cache_swa_v.reshape(depth, bs, wb, N_KV * HEAD_DIM)

    f_dense = ffn_w_gate.shape[2]
    tf_dense = f_dense // 2 if (f_dense // 2) % LANES == 0 else f_dense
    f_moe = moe_w_gate.shape[3]
    tf_moe = 512 if f_moe % 512 == 0 else f_moe

    n0_all = state_mlstm_n[:, :, :, None, :]
    m0_all = _m_state_in(state_mlstm_m)
    s_c = None
    new_p = [[] for _ in range(6)]
    new_s = [[] for _ in range(6)]
    for l in range(depth):
        j = l // 2
        xp, xs = x if isinstance(x, tuple) else (x, x)
        z, zif = _ln_proj(xp, xs, mod_p, mod_s, w_in_p, b_in_p, w_if, b_if, l, rm, n)
        zs3 = z[n_p:].reshape(ts, bs, Z_W)

        cp, ns_p = _conv_prompt(z, jnp.zeros((bp, CONV_PAD, D_CONV), F32), cw_pad, *cvecs, l, bp, tp, tc)
        cs3, a_s3 = _conv_sample(zs3, state_conv, cw_pad, *cvecs, l, bs_blk)
        new_p[2].append(ns_p[:, CONV_PAD - CONV_W + 1:])
        new_s[2].append(jnp.transpose(a_s3, (1, 0, 2)))

        sink_h = sinks[l].reshape(N_KV, Q_PER_KV, 1)
        sink_s = jnp.broadcast_to(sink_h, (N_KV, Q_PER_KV, ts)).reshape(N_KV, Q_PER_KV * ts, 1)
        ap = _attn_prompt(z, bias_p, sinks, l, bp, tp)
        nk = min(WINDOW, tp)
        kv_tail = jnp.stack([z[(b + 1) * tp - nk:(b + 1) * tp, Z_K:Z_K + 2 * N_KV * HEAD_DIM]
                             for b in range(bp)]).astype(F32)
        new_p[0].append(kv_tail[..., :N_KV * HEAD_DIM].reshape(bp, nk, N_KV, HEAD_DIM))
        new_p[1].append(kv_tail[..., N_KV * HEAD_DIM:].reshape(bp, nk, N_KV, HEAD_DIM))
        q_s = zs3[:, :, Z_Q:Z_Q + N_HEADS * HEAD_DIM].reshape(ts, bs, N_KV, Q_PER_KV, HEAD_DIM)
        q4 = jnp.transpose(q_s, (1, 2, 3, 0, 4)).reshape(bs, N_KV, Q_PER_KV * ts, HEAD_DIM)
        k_s = jnp.transpose(zs3[:, :, Z_K:Z_K + N_KV * HEAD_DIM].reshape(ts, bs, N_KV, HEAD_DIM), (1, 0, 2, 3))
        v_s = jnp.transpose(zs3[:, :, Z_V:Z_V + N_KV * HEAD_DIM].reshape(ts, bs, N_KV, HEAD_DIM), (1, 0, 2, 3))
        o4 = _attn_sample(q4, kc_all, vc_all, jnp.transpose(k_s, (0, 2, 1, 3)), jnp.transpose(v_s, (0, 2, 1, 3)),
                          bias_s, sink_s, l, bs_att)
        as_ = jnp.transpose(o4.reshape(bs, N_KV, Q_PER_KV, ts, HEAD_DIM), (3, 0, 1, 2, 4)).reshape(n_s, -1).astype(BF16)
        new_s[0].append(k_s.astype(F32))
        new_s[1].append(v_s.astype(F32))

        ncp = tp // lm
        if_p = zif[:n_p, :2 * M_HEADS].reshape(bp * ncp, lm, 2 * M_HEADS)
        mp, c1p, n1p, m1p = _mlstm_prompt(z, zif, jnp.transpose(if_p, (0, 2, 1)), gamma, l, bp, ncp, lm)
        new_p[3].append(c1p)
        new_p[4].append(n1p[:, :, 0])
        new_p[5].append(m1p[:, :M_HEADS, 0])
        tpad = ((0, 0), (0, lts - ts), (0, 0))
        zm3 = jnp.pad(jnp.transpose(zs3[:, :, Z_MQ:Z_K], (1, 0, 2)), tpad)
        if3 = jnp.pad(jnp.transpose(zif[n_p:].reshape(ts, bs, LANES), (1, 0, 2)), tpad)
        ms, s_c, n1s, m1s = _mlstm_sample(zm3, if3, jnp.transpose(if3[:, :, :2 * M_HEADS], (0, 2, 1)), gamma,
                                          state_mlstm_C, n0_all[l], m0_all[l], s_c, l, ts, bs_m)
        ms = jnp.transpose(ms[:, :ts], (1, 0, 2)).reshape(n_s, -1)
        new_s[4].append(n1s[:, :, 0])
        new_s[5].append(m1s[:, :M_HEADS, 0])

        moe = l % 2 == 1
        router = (rw_pad, rb_pad, j) if moe else None
        outs = _merge(cp, cs3.reshape(n_s, D_CONV), ap, as_, mp, ms, z, xp, xs, mod_p, mod_s,
                      wc_b, wa_b, wm_b, wo_b, pg, pb, l, rm_half, alpha, router)
        if moe:
            x1, h2, route = outs
            x = _moe_ffn(h2, route, moe_w_gate, moe_w_up, moe_w_down, x1, mod_p, mod_s, pg, pb,
                         l, j, rm, tb, tf_moe, alpha, split=l == depth - 1)
        else:
            x1, h2 = outs
            x = _ffn_dense(h2, fg_b, fu_b, fd_b, x1, mod_p, mod_s, pg, pb, l, j, rm, tf_dense, alpha)

    x_p, x_s = x if isinstance(x, (list, tuple)) else (x[:n_p], x[n_p:])
    y_p = x_p.reshape(bp, tp, d)
    y_s = jnp.transpose(x_s.reshape(ts, bs, d), (1, 0, 2))
    p_k, p_v, p_conv, p_c, p_n, p_m = [jnp.stack(a) for a in new_p]
    s_k, s_v, s_conv = [jnp.concatenate([old[:, :, ts:], jnp.stack(new)], axis=2)
                        for old, new in zip((cache_swa_k, cache_swa_v, state_conv), new_s[:3])]
    s_n, s_m = jnp.stack(new_s[4]), jnp.stack(new_s[5])
    return (y_p, y_s, p_k, p_v, p_conv, p_c, p_n, p_m, s_k, s_v, s_conv, s_c, s_n, s_m)
```

```python
import functools
import math

import jax
import jax.numpy as jnp
from jax import lax
from jax.experimental import pallas as pl
from jax.experimental.pallas import tpu as pltpu

F32 = jnp.float32
BF16 = jnp.bfloat16

D_MODEL = 1024
D_CONV = 512
CONV_W = 31
CONV_PAD = 32
N_HEADS = 8
N_KV = 2
HEAD_DIM = 64
Q_PER_KV = N_HEADS // N_KV
WINDOW = 128
N_BUCKETS = 32
MAX_DIST = 128
M_HEADS = 4
M_DK = 128
M_DV = 128
N_EXPERTS = 8
LN_EPS = 1e-5
LANES = 128
NEG_INF = float("-inf")
LOG2E = math.log2(math.e)
VMEM_LIMIT = 56 * 1024 * 1024

Z_G, Z_UA, Z_UB, Z_Q = 0, 3072, 3584, 4096
Z_MQ, Z_MK, Z_MV, Z_MO = 4608, 5120, 5632, 6144
Z_K, Z_V, Z_W = 6656, 6784, 6912
TN_IN = 3456
ATT_QB = 4
ATT_GROUP = 8
MOE_SUB = 256
CONV_CHUNK = 32

def _cparams(*sem):
    return pltpu.CompilerParams(dimension_semantics=sem, vmem_limit_bytes=VMEM_LIMIT)


def _sigmoid(x):
    return 1.0 / (1.0 + jnp.exp(-x))


def _silu(x):
    return x * _sigmoid(x)


def _log_sigmoid(x):
    return jnp.minimum(x, 0.0) - jnp.log(1.0 + jnp.exp(-jnp.abs(x)))


def _norm(x):
    mu = jnp.mean(x, axis=-1, keepdims=True)
    xc = x - mu
    var = jnp.mean(xc * xc, axis=-1, keepdims=True)
    return xc * lax.rsqrt(var + LN_EPS)


def _dot(a, b):
    return jnp.dot(a, b, preferred_element_type=F32)


def _dot_nt(a, b):
    return lax.dot_general(a, b, (((1,), (1,)), ((), ())), preferred_element_type=F32)


def _dot_tn(a, b):
    return lax.dot_general(a, b, (((0,), (0,)), ((), ())), preferred_element_type=F32)


def _dot_hi(a, b):
    return jnp.dot(a, b, preferred_element_type=F32, precision=lax.Precision.HIGHEST)


def _split_bf16(a):
    hi = a.astype(BF16)
    return hi, (a - hi.astype(F32)).astype(BF16)


_RARELY = pl.Buffered(1)


class _RowMap:
    def __init__(self, t, bp, tp, n_p, n_s):
        self.t = t
        self.bp = bp
        self.per_seq = tp // t
        self.np_tiles = n_p // t
        self.ns_tiles = n_s // t

    def seq(self, i):
        return jnp.minimum(i // self.per_seq, self.bp - 1)

    def prompt(self, i):
        return jnp.minimum(i, self.np_tiles - 1)

    def sample(self, i):
        return jnp.clip(i - self.np_tiles, 0, self.ns_tiles - 1)

    def mod_specs(self, l, k, d):
        return (pl.BlockSpec((None, None, 1, d), lambda i, *_: (l, self.seq(i), 0, k)),
                pl.BlockSpec((None, self.t, d), lambda i, *_: (l, self.sample(i), k), pipeline_mode=_RARELY))

    def x_specs(self, xs_arr, d):
        off = self.np_tiles if xs_arr.shape[0] > self.ns_tiles * self.t else 0
        return (pl.BlockSpec((self.t, d), lambda i, *_: (self.prompt(i), 0)),
                pl.BlockSpec((self.t, d), lambda i, *_: (off + self.sample(i), 0), pipeline_mode=_RARELY))


def _pick(is_s, p_ref, s_ref):
    return jnp.where(is_s, s_ref[...], p_ref[...])


def _ada_kernel(c_ref, w_ref, b_ref, o_ref):
    s = _silu(c_ref[...]).astype(BF16)
    o_ref[0] = _dot(s, w_ref[0].astype(BF16)) + b_ref[0]


def _ada_mod(c_all, w_ada, b_ada):
    depth, d, n6 = w_ada.shape
    rows = c_all.shape[0]
    return pl.pallas_call(
        _ada_kernel,
        grid=(depth, n6 // d),
        in_specs=[pl.BlockSpec((rows, d), lambda l, j: (0, 0)),
                  pl.BlockSpec((1, d, d), lambda l, j: (l, 0, j)),
                  pl.BlockSpec((1, 1, d), lambda l, j: (l, 0, j))],
        out_specs=pl.BlockSpec((1, rows, d), lambda l, j: (l, 0, j)),
        out_shape=jax.ShapeDtypeStruct((depth, rows, n6), F32),
        compiler_params=_cparams("parallel", "parallel"),
        name="ada_mod",
    )(c_all, w_ada, b_ada.reshape(depth, 1, n6))


def _ln_proj_kernel(xp_ref, xs_ref, shp_ref, shs_ref, scp_ref, scs_ref, w_ref, b_ref, wif_ref, bif_ref,
                    z_ref, zif_ref, h_scr, *, np_tiles):
    is_s = pl.program_id(0) >= np_tiles
    first = pl.program_id(1) == 0

    def prologue(x_ref, sh_ref, sc_ref):
        h = (_norm(x_ref[...]) * (1.0 + sc_ref[...]) + sh_ref[...]).astype(BF16)
        h_scr[...] = h
        zif_ref[...] = _dot(h, wif_ref[...]) + bif_ref[...]

    @pl.when(jnp.logical_and(first, is_s))
    def _():
        prologue(xs_ref, shs_ref, scs_ref)

    @pl.when(jnp.logical_and(first, jnp.logical_not(is_s)))
    def _():
        prologue(xp_ref, shp_ref, scp_ref)

    z_ref[...] = (_dot(h_scr[...], w_ref[...]) + b_ref[...]).astype(BF16)


def _ln_proj(xp, xs, mod_p, mod_s, w, b, wif, bif, l, rm, n):
    d = xp.shape[1]
    zw = w.shape[2]
    tm = rm.t
    shp, shs = rm.mod_specs(l, 0, d)
    scp, scs = rm.mod_specs(l, 1, d)
    return pl.pallas_call(
        functools.partial(_ln_proj_kernel, np_tiles=rm.np_tiles),
        grid=(n // tm, zw // TN_IN),
        in_specs=[*rm.x_specs(xs, d),
                  shp, shs, scp, scs,
                  pl.BlockSpec((None, d, TN_IN), lambda i, j: (l, 0, j)),
                  pl.BlockSpec((None, 1, TN_IN), lambda i, j: (l, 0, j)),
                  pl.BlockSpec((None, d, LANES), lambda i, j: (l, 0, 0)),
                  pl.BlockSpec((None, 1, LANES), lambda i, j: (l, 0, 0))],
        out_specs=[pl.BlockSpec((tm, TN_IN), lambda i, j: (i, j)),
                   pl.BlockSpec((tm, LANES), lambda i, j: (i, 0))],
        out_shape=[jax.ShapeDtypeStruct((n, zw), BF16), jax.ShapeDtypeStruct((n, LANES), F32)],
        scratch_shapes=[pltpu.VMEM((tm, d), BF16)],
        compiler_params=_cparams("parallel", "arbitrary"),
        name="ln_proj",
    )(xp, xs, mod_p, mod_s, mod_p, mod_s, w, b, wif, bif)


def _conv_tail(yc, g_ref, b_ref):
    y = _norm(yc) * g_ref[...] + b_ref[...]
    return _silu(y).astype(BF16)


def _conv_prompt_kernel(ua_ref, ub_ref, st_ref, cw_ref, cb_ref, g_ref, b_ref, o_ref, ns_ref,
                        ext, shifted, yc, wrep):
    t = pl.program_id(1)
    tc = ua_ref.shape[0]
    sub = 8

    @pl.when(t == 0)
    def _():
        ext[0:CONV_PAD, :] = st_ref[0]

    @pl.when(t > 0)
    def _():
        ext[0:CONV_PAD, :] = ext[tc:tc + CONV_PAD, :]

    ext[CONV_PAD:, :] = ua_ref[...].astype(F32) * _sigmoid(ub_ref[...].astype(F32))
    for s in range(1, sub):
        shifted[s - 1] = ext[s:s + tc + CONV_PAD - sub, :]
    off = CONV_PAD - (CONV_W - 1)
    for w in range(CONV_W):
        wrep[w] = jnp.broadcast_to(cw_ref[w:w + 1, :], (sub, D_CONV))
    groups = CONV_CHUNK // sub

    for r0 in range(0, tc, CONV_CHUNK):
        acc = jnp.broadcast_to(cb_ref[...].reshape(1, 1, D_CONV), (groups, sub, D_CONV))
        for w in range(CONV_W):
            base, s = (off + w) // sub * sub, (off + w) % sub
            src = ext if s == 0 else shifted.at[s - 1]
            win = src[r0 + base:r0 + base + CONV_CHUNK, :]
            acc = acc + win.reshape(groups, sub, D_CONV) * wrep[w][None]
        yc[r0:r0 + CONV_CHUNK, :] = acc.reshape(CONV_CHUNK, D_CONV)
    o_ref[...] = _conv_tail(yc[...], g_ref, b_ref)

    @pl.when(t == pl.num_programs(1) - 1)
    def _():
        ns_ref[0] = ext[tc:tc + CONV_PAD, :]


def _conv_vec_specs(l, nargs):
    return [pl.BlockSpec((None, 1, D_CONV), lambda *_: (l, 0, 0)) for _ in range(nargs)]


def _conv_prompt(z, state_pad, cw, cb, g, b, l, bp, tp, tc):
    nt = tp // tc
    return pl.pallas_call(
        _conv_prompt_kernel,
        grid=(bp, nt),
        in_specs=[pl.BlockSpec((tc, D_CONV), lambda bb, t: (bb * nt + t, Z_UA // D_CONV)),
                  pl.BlockSpec((tc, D_CONV), lambda bb, t: (bb * nt + t, Z_UB // D_CONV)),
                  pl.BlockSpec((1, CONV_PAD, D_CONV), lambda bb, t: (bb, 0, 0)),
                  pl.BlockSpec((None, CONV_PAD, D_CONV), lambda bb, t: (l, 0, 0))] + _conv_vec_specs(l, 3),
        out_specs=[pl.BlockSpec((tc, D_CONV), lambda bb, t: (bb * nt + t, 0)),
                   pl.BlockSpec((1, CONV_PAD, D_CONV), lambda bb, t: (bb, 0, 0))],
        out_shape=[jax.ShapeDtypeStruct((bp * tp, D_CONV), BF16),
                   jax.ShapeDtypeStruct((bp, CONV_PAD, D_CONV), F32)],
        scratch_shapes=[pltpu.VMEM((tc + CONV_PAD, D_CONV), F32),
                        pltpu.VMEM((7, tc + CONV_PAD - 8, D_CONV), F32),
                        pltpu.VMEM((tc, D_CONV), F32),
                        pltpu.VMEM((CONV_PAD, 8, D_CONV), F32)],
        compiler_params=_cparams("parallel", "arbitrary"),
        name="conv_prompt",
    )(z, z, state_pad, cw, cb, g, b)


def _conv_sample_kernel(ua_ref, ub_ref, st_ref, cw_ref, cb_ref, g_ref, b_ref, o_ref, a_ref):
    ts = ua_ref.shape[0]
    ns = CONV_W - 1
    a = ua_ref[...].astype(F32) * _sigmoid(ub_ref[...].astype(F32))
    a_ref[...] = a
    st = st_ref[...]
    row = lax.broadcasted_iota(jnp.int32, (ns, D_CONV), 0)
    for t in range(ts):
        wt = jnp.zeros((ns, D_CONV), F32)
        for j in range(t, ns):
            wt = jnp.where(row == j, cw_ref[j - t:j - t + 1, :], wt)
        yc = jnp.sum(st * wt[None], axis=1) + cb_ref[...]
        for t2 in range(t + 1):
            wi = CONV_W - 1 - (t - t2)
            yc = yc + a[t2] * cw_ref[wi:wi + 1, :]
        o_ref[t] = _conv_tail(yc, g_ref, b_ref)


def _conv_sample(zs3, state, cw, cb, g, b, l, bs_blk):
    ts, bs, _ = zs3.shape
    ns = CONV_W - 1
    return pl.pallas_call(
        _conv_sample_kernel,
        grid=(bs // bs_blk,),
        in_specs=[pl.BlockSpec((ts, bs_blk, D_CONV), lambda i: (0, i, Z_UA // D_CONV)),
                  pl.BlockSpec((ts, bs_blk, D_CONV), lambda i: (0, i, Z_UB // D_CONV)),
                  pl.BlockSpec((None, bs_blk, ns, D_CONV), lambda i: (l, i, 0, 0)),
                  pl.BlockSpec((None, CONV_PAD, D_CONV), lambda i: (l, 0, 0))] + _conv_vec_specs(l, 3),
        out_specs=[pl.BlockSpec((ts, bs_blk, D_CONV), lambda i: (0, i, 0)),
                   pl.BlockSpec((ts, bs_blk, D_CONV), lambda i: (0, i, 0))],
        out_shape=[jax.ShapeDtypeStruct((ts, bs, D_CONV), BF16),
                   jax.ShapeDtypeStruct((ts, bs, D_CONV), F32)],
        compiler_params=_cparams("parallel"),
        name="conv_sample",
    )(zs3, zs3, state, cw, cb, g, b)


def _t5_bucket(dist):
    max_exact = N_BUCKETS // 2
    d = jnp.maximum(dist, 0)
    large = max_exact + (jnp.log(jnp.maximum(d, 1).astype(F32) / max_exact)
                         / math.log(MAX_DIST / max_exact) * (N_BUCKETS - max_exact)).astype(jnp.int32)
    return jnp.where(d < max_exact, d, jnp.minimum(large, N_BUCKETS - 1))


def _bias_heads(rel_bias, dist, valid):
    onehot = (_t5_bucket(dist)[..., None] == jnp.arange(N_BUCKETS)).astype(F32)
    bias = jnp.einsum("qkb,bh->qkh", onehot, rel_bias.astype(F32), precision=lax.Precision.HIGHEST)
    bias = jnp.where(valid[..., None], bias, NEG_INF)
    return jnp.transpose(bias, (2, 0, 1))


def _attn_prompt_kernel(sink_ref, q_ref, kc_ref, kp_ref, vc_ref, vp_ref, bias_ref, o_ref, *, l):
    first = pl.program_id(1) == 0
    w = WINDOW
    nq = q_ref.shape[0] // w
    kall = jnp.concatenate([kp_ref[...], kc_ref[...]], axis=0).astype(F32)
    vall = jnp.concatenate([vp_ref[...], vc_ref[...]], axis=0).astype(F32)
    lane = lax.broadcasted_iota(jnp.int32, kall.shape, 1)
    lo = lane < HEAD_DIM
    kroll = pltpu.roll(kall, HEAD_DIM, 1)
    vroll = pltpu.roll(vall, HEAD_DIM, 1)

    def halves(a, aroll, g):
        if g == 0:
            return jnp.where(lo, a, 0.0).astype(BF16), jnp.where(lo, 0.0, aroll).astype(BF16)
        return jnp.where(lo, aroll, 0.0).astype(BF16), jnp.where(lo, 0.0, a).astype(BF16)

    kh = [halves(kall, kroll, g) for g in range(N_KV)]
    vh = [halves(vall, vroll, g) for g in range(N_KV)]
    col = lax.broadcasted_iota(jnp.int32, (w, 4 * w), 1)
    prev_col = (col % (2 * w)) < w
    tiles_per_g = Q_PER_KV // 2
    units = [(qi, tile) for qi in range(nq) for tile in range(N_HEADS // 2)]
    for u0 in range(0, len(units), ATT_GROUP):
        group = units[u0:u0 + ATT_GROUP]
        scores = []
        for qi, tile in group:
            r0, g = qi * w, tile // tiles_per_g
            q = q_ref[r0:r0 + w, tile * LANES:(tile + 1) * LANES]
            kk = jnp.concatenate([kh[g][0][r0:r0 + 2 * w], kh[g][1][r0:r0 + 2 * w]], axis=0)
            s = _dot_nt(q, kk) * (HEAD_DIM ** -0.5 * LOG2E) + bias_ref[tile]
            if qi == 0:
                s = jnp.where(jnp.logical_and(first, prev_col), NEG_INF, s)
            scores.append(s)
        probs = []
        for (qi, tile), s in zip(group, scores):
            ps = []
            for half in range(2):
                sh = s[:, half * 2 * w:(half + 1) * 2 * w]
                sink = sink_ref[l, 2 * tile + half]
                mx = jnp.maximum(jnp.max(sh, axis=-1, keepdims=True), sink)
                p = jnp.exp2(sh - mx)
                den = jnp.sum(p, axis=-1, keepdims=True) + jnp.exp2(sink - mx)
                ps.append((p * (1.0 / den)).astype(BF16))
            probs.append(jnp.concatenate(ps, axis=1))
        for (qi, tile), p in zip(group, probs):
            r0, g = qi * w, tile // tiles_per_g
            vv = jnp.concatenate([vh[g][0][r0:r0 + 2 * w], vh[g][1][r0:r0 + 2 * w]], axis=0)
            o_ref[r0:r0 + w, tile * LANES:(tile + 1) * LANES] = _dot(p, vv).astype(BF16)


def _attn_prompt(z, bias, sinks, l, bp, tp):
    w = WINDOW
    qb = ATT_QB if tp % (ATT_QB * w) == 0 else 1
    ns = tp // (qb * w)
    nb = tp // w
    kvw = N_KV * HEAD_DIM
    qw = N_HEADS * HEAD_DIM

    def cur(col):
        return lambda bb, i: (bb * ns + i, col)

    def prev(col):
        return lambda bb, i: (bb * nb + jnp.maximum(i * qb - 1, 0), col)

    return pl.pallas_call(
        functools.partial(_attn_prompt_kernel, l=l),
        grid=(bp, ns),
        in_specs=[pl.BlockSpec(memory_space=pltpu.SMEM),
                  pl.BlockSpec((qb * w, qw), cur(Z_Q // qw)),
                  pl.BlockSpec((qb * w, kvw), cur(Z_K // kvw)),
                  pl.BlockSpec((w, kvw), prev(Z_K // kvw)),
                  pl.BlockSpec((qb * w, kvw), cur(Z_V // kvw)),
                  pl.BlockSpec((w, kvw), prev(Z_V // kvw)),
                  pl.BlockSpec((N_HEADS // 2, w, 4 * w), lambda bb, i: (0, 0, 0))],
        out_specs=pl.BlockSpec((qb * w, qw), lambda bb, i: (bb * ns + i, 0)),
        out_shape=jax.ShapeDtypeStruct((bp * tp, qw), BF16),
        compiler_params=_cparams("parallel", "parallel"),
        name="attn_prompt",
    )(sinks, z, z, z, z, z, bias)


def _attn_sample_kernel(q_ref, kc_ref, vc_ref, kn_ref, vn_ref, bias_ref, sink_ref, o_ref):
    wb = kc_ref.shape[1]
    ts = kn_ref.shape[2]
    for g in range(N_KV):
        lo = g * HEAD_DIM
        qb = (q_ref[:, g].astype(F32) * (HEAD_DIM ** -0.5)).astype(BF16)
        kc = kc_ref[:, :, lo:lo + HEAD_DIM].astype(BF16)
        vc = vc_ref[:, :, lo:lo + HEAD_DIM].astype(BF16)
        kn = kn_ref[:, g].astype(F32)
        vn = vn_ref[:, g].astype(F32)
        bias = bias_ref[g]
        s_c = jnp.einsum("bqd,bkd->bqk", qb, kc, preferred_element_type=F32) + bias[None, :, :wb]
        qf = qb.astype(F32)
        s_n = [jnp.sum(qf * kn[:, j:j + 1, :], axis=-1, keepdims=True) + bias[None, :, wb + j:wb + j + 1]
               for j in range(ts)]
        sink = sink_ref[g][None]
        mx = jnp.maximum(jnp.max(s_c, axis=-1, keepdims=True), sink)
        for sj in s_n:
            mx = jnp.maximum(mx, sj)
        p_c = jnp.exp(s_c - mx)
        p_n = [jnp.exp(sj - mx) for sj in s_n]
        den = jnp.sum(p_c, axis=-1, keepdims=True) + jnp.exp(sink - mx)
        for pj in p_n:
            den = den + pj
        o = jnp.einsum("bqk,bkd->bqd", (p_c / den).astype(BF16), vc, preferred_element_type=F32)
        for j in range(ts):
            o = o + (p_n[j] / den).astype(BF16).astype(F32) * vn[:, j:j + 1, :]
        o_ref[:, g] = o


def _attn_sample(q4, kc, vc, kn, vn, bias, sinks, l, bs_blk):
    bs, _, rt, _ = q4.shape
    wb = kc.shape[2]
    ts = kn.shape[2]
    kvw = N_KV * HEAD_DIM
    return pl.pallas_call(
        _attn_sample_kernel,
        grid=(bs // bs_blk,),
        in_specs=[pl.BlockSpec((bs_blk, N_KV, rt, HEAD_DIM), lambda i: (i, 0, 0, 0)),
                  pl.BlockSpec((None, bs_blk, wb, kvw), lambda i: (l, i, 0, 0)),
                  pl.BlockSpec((None, bs_blk, wb, kvw), lambda i: (l, i, 0, 0)),
                  pl.BlockSpec((bs_blk, N_KV, ts, HEAD_DIM), lambda i: (i, 0, 0, 0)),
                  pl.BlockSpec((bs_blk, N_KV, ts, HEAD_DIM), lambda i: (i, 0, 0, 0)),
                  pl.BlockSpec((N_KV, rt, wb + ts), lambda i: (0, 0, 0)),
                  pl.BlockSpec((N_KV, rt, 1), lambda i: (0, 0, 0))],
        out_specs=pl.BlockSpec((bs_blk, N_KV, rt, HEAD_DIM), lambda i: (i, 0, 0, 0)),
        out_shape=jax.ShapeDtypeStruct((bs, N_KV, rt, HEAD_DIM), F32),
        compiler_params=_cparams("parallel"),
        name="attn_sample",
    )(q4, kc, vc, kn, vn, bias, sinks)


def _mlstm_kernel(*refs, t_valid, nseq, aliased, carried, chunk_axis, per_seq_inputs=False, first_layer=None):
    if aliased:
        refs = refs[:10] + refs[11:]
    if per_seq_inputs:
        groups = [refs[6 * s:6 * s + 6] for s in range(nseq)]
        if_ref, ifr_ref, q_ref, k_ref, v_ref, o_ref = (tuple(g[k] for g in groups) for k in range(6))
        refs = (None,) * 6 + refs[6 * nseq:]
    else:
        if_ref, ifr_ref, q_ref, k_ref, v_ref, o_ref = refs[:6]
    g_ref, c0_ref, n0_ref, m0_ref, h_ref, c1_ref, n1_ref, m1_ref = refs[6:14]
    c = pl.program_id(chunk_axis)
    L = (ifr_ref[0] if per_seq_inputs else ifr_ref).shape[-1]
    if carried:
        c_in, n_in, m_in = c_out, n_out, m_out = refs[14:]

        @pl.when(c == 0)
        def _():
            c_in[...] = c0_ref[...]
            n_in[...] = n0_ref[...]
            m_in[...] = m0_ref[...]
    else:
        (c_in, n_in, m_in), (c_out, n_out, m_out) = (c0_ref, n0_ref, m0_ref), (c1_ref, n1_ref, m1_ref)
        if first_layer is not None:
            for dd in range(c1_ref.shape[0]):
                if dd != first_layer:
                    c1_ref[dd] = jnp.zeros(c1_ref.shape[1:], F32)
            c_out = c1_ref.at[first_layer]

    def seq(ref, s):
        return ref[s] if isinstance(ref, tuple) else ref.at[s]

    tt = lax.broadcasted_iota(jnp.int32, (L, L), 0)
    ss = lax.broadcasted_iota(jnp.int32, (L, L), 1)
    causal = ss <= tt
    tril = causal.astype(F32)
    triu = (tt <= ss).astype(F32)
    seqs = range(nseq)
    heads = [(s_i, h) for s_i in seqs for h in range(M_HEADS)]
    mxu_sums = L % LANES == 0
    gates = []
    for s_i in seqs:
        ifc = seq(if_ref, s_i)[...]
        ifr = ifr_ref[s_i][0] if per_seq_inputs else ifr_ref[s_i]
        lf_c = _log_sigmoid(ifc)
        lf_r = _log_sigmoid(ifr)
        i_c, i_r = ifc, ifr
        if t_valid < L:
            rc = lax.broadcasted_iota(jnp.int32, (L, LANES), 0) < t_valid
            rr = lax.broadcasted_iota(jnp.int32, (2 * M_HEADS, L), 1) < t_valid
            lf_c = jnp.where(rc, lf_c, 0.0)
            lf_r = jnp.where(rr, lf_r, 0.0)
            i_c = jnp.where(rc, i_c, NEG_INF)
            i_r = jnp.where(rr, i_r, NEG_INF)
        gates.append((lf_c, lf_r, i_c, i_r))
    f_cs = [_dot_hi(tril, g[0]) for g in gates]
    f_rs = [_dot_hi(g[1], triu) for g in gates]
    state = {(s_i, h): (c_in[s_i, h], n_in[s_i, h], m_in[s_i, h:h + 1, :]) for s_i, h in heads}

    def wide(col):
        return jnp.concatenate([col] * (L // LANES), axis=1) if L >= LANES else col[:, :L]

    st1 = {}
    for s_i, h in heads:
        _, nrow, m0 = state[s_i, h]
        fc = jnp.broadcast_to(f_cs[s_i][:, M_HEADS + h:M_HEADS + h + 1], (L, LANES))
        fr = f_rs[s_i][M_HEADS + h:M_HEADS + h + 1, :]
        ir = gates[s_i][3][h:h + 1, :]
        dm = jnp.where(causal, wide(fc) - fr + ir, NEG_INF)
        m_t = jnp.maximum(m0 + fc, jnp.max(dm, axis=-1, keepdims=True))
        st1[s_i, h] = (fc, dm, m_t, jnp.exp(m0 + fc - m_t))
    st2 = {}
    for s_i, h in heads:
        lo = h * M_DK
        fc, dm, m_t, inter = st1[s_i, h]
        qb = seq(q_ref, s_i)[:, lo:lo + M_DK]
        kf = seq(k_ref, s_i)[:, lo:lo + M_DK].astype(F32) * (M_DK ** -0.5)
        vf = seq(v_ref, s_i)[:, lo:lo + M_DV].astype(F32)
        kb, vb = kf.astype(BF16), vf.astype(BF16)
        if mxu_sums:
            n_rows = jnp.broadcast_to(state[s_i, h][1], (M_DK, M_DK)).astype(BF16)
            qk = _dot_nt(qb, jnp.concatenate([kb, n_rows], axis=0))
            sc, qn_rep = qk[:, :L] * jnp.exp(dm - wide(m_t)), qk[:, L:]
        else:
            sc, qn_rep = _dot_nt(qb, kb) * jnp.exp(dm - wide(m_t)), None
        st2[s_i, h] = (qb, kf, vf, kb, vb, sc, qn_rep)
    st3 = {}
    for s_i, h in heads:
        cm, nrow, m0 = state[s_i, h]
        fc, dm, m_t, inter = st1[s_i, h]
        qb, kf, vf, kb, vb, sc, qn_rep = st2[s_i, h]
        if mxu_sums:
            sv = _dot(sc.astype(BF16), jnp.concatenate([vb, jnp.ones((L, M_DV), BF16)], axis=1))
            num = inter * _dot(qb, cm.astype(BF16)) + sv[:, :M_DV]
            qn = inter * qn_rep + sv[:, M_DV:]
            floor = jnp.exp(-m_t)
        else:
            num = inter * _dot(qb, cm.astype(BF16)) + _dot(sc.astype(BF16), vb)
            qn = (inter * jnp.sum(qb.astype(F32) * nrow, axis=-1, keepdims=True)
                  + jnp.sum(sc, axis=-1, keepdims=True))
            floor = jnp.exp(-m_t)
        st3[s_i, h] = num / jnp.maximum(jnp.abs(qn), floor)
    new_state = {}
    for s_i, h in heads:
        cm, nrow, m0 = state[s_i, h]
        fc, dm, m_t, inter = st1[s_i, h]
        qb, kf, vf, kb, vb, sc, _ = st2[s_i, h]
        ic = jnp.broadcast_to(gates[s_i][2][:, h:h + 1], (L, LANES))
        m_end = m_t[L - 1:L, :]
        f_end = fc[L - 1:L, :]
        decay = jnp.exp(m0 + f_end - m_end)
        w_s = jnp.exp(f_end - fc + ic - m_end)
        new_state[s_i, h] = (decay * cm + _dot_tn(kb, (w_s * vf).astype(BF16)),
                             decay * nrow + jnp.sum(w_s * kf, axis=0, keepdims=True),
                             m_end)
    for s_i, h in heads:
        lo = h * M_DK
        hn = _norm(st3[s_i, h]) * g_ref[:, lo:lo + M_DV]
        gate = _sigmoid(seq(o_ref, s_i)[:, lo:lo + M_DV].astype(F32))
        seq(h_ref, s_i)[:, lo:lo + M_DV] = (gate * hn).astype(BF16)
    for s_i, h in heads:
        c_out[s_i, h], n_out[s_i, h], m_out[s_i, h:h + 1, :] = new_state[s_i, h]
    if not carried:
        for s_i in seqs:
            m_out[s_i, M_HEADS:, :] = jnp.zeros((M_HEADS, LANES), F32)

    if carried:
        @pl.when(c == pl.num_programs(chunk_axis) - 1)
        def _():
            c1_ref[...] = c_out[...]
            n1_ref[...] = n_out[...]
            m1_ref[...] = m_out[...]


def _mlstm_state_specs(nseq, l_state):
    if l_state is None:
        c_spec = pl.BlockSpec((nseq, M_HEADS, M_DK, M_DV), lambda b, c: (b, 0, 0, 0))
    else:
        c_spec = pl.BlockSpec((None, nseq, M_HEADS, M_DK, M_DV), lambda b, c: (l_state, b, 0, 0, 0))
    return (c_spec,
            pl.BlockSpec((nseq, M_HEADS, 1, M_DK), lambda b, c: (b, 0, 0, 0)),
            pl.BlockSpec((nseq, 2 * M_HEADS, LANES), lambda b, c: (b, 0, 0)))


def _mlstm_scratch(nseq):
    return [pltpu.VMEM((nseq, M_HEADS, M_DK, M_DV), F32),
            pltpu.VMEM((nseq, M_HEADS, 1, M_DK), F32),
            pltpu.VMEM((nseq, 2 * M_HEADS, LANES), F32)]


def _mlstm_prompt(z, zif, ifr, gamma, l, bp, nc, L):
    hw = M_HEADS * M_DK
    zero = lambda *s: jnp.zeros(s, F32)
    in_specs, args = [], []
    for b in range(bp):
        def rows(col, b=b):
            return lambda c: (b * nc + c, col)

        in_specs += [pl.BlockSpec((L, LANES), rows(0)),
                     pl.BlockSpec((1, 2 * M_HEADS, L), lambda c, b=b: (b * nc + c, 0, 0)),
                     pl.BlockSpec((L, hw), rows(Z_MQ // hw)),
                     pl.BlockSpec((L, hw), rows(Z_MK // hw)),
                     pl.BlockSpec((L, hw), rows(Z_MV // hw)),
                     pl.BlockSpec((L, hw), rows(Z_MO // hw))]
        args += [zif, ifr, z, z, z, z]
    state_specs = [pl.BlockSpec((bp, M_HEADS, M_DK, M_DV), lambda c: (0, 0, 0, 0)),
                   pl.BlockSpec((bp, M_HEADS, 1, M_DK), lambda c: (0, 0, 0, 0)),
                   pl.BlockSpec((bp, 2 * M_HEADS, LANES), lambda c: (0, 0, 0))]
    kern = functools.partial(_mlstm_kernel, t_valid=L, nseq=bp, aliased=False, carried=True, chunk_axis=0,
                             per_seq_inputs=True)
    outs = pl.pallas_call(
        kern,
        grid=(nc,),
        in_specs=in_specs + [pl.BlockSpec((None, 1, hw), lambda c: (l, 0, 0))] + state_specs,
        out_specs=[pl.BlockSpec((bp, L, hw), lambda c: (0, c, 0))] + state_specs,
        out_shape=[jax.ShapeDtypeStruct((bp, nc * L, hw), BF16),
                   jax.ShapeDtypeStruct((bp, M_HEADS, M_DK, M_DV), F32),
                   jax.ShapeDtypeStruct((bp, M_HEADS, 1, M_DK), F32),
                   jax.ShapeDtypeStruct((bp, 2 * M_HEADS, LANES), F32)],
        scratch_shapes=_mlstm_scratch(bp),
        compiler_params=_cparams("arbitrary"),
        name="mlstm_prompt",
    )(*args, gamma, zero(bp, M_HEADS, M_DK, M_DV), zero(bp, M_HEADS, 1, M_DK), zero(bp, 2 * M_HEADS, LANES))
    return (outs[0].reshape(bp * nc * L, hw),) + tuple(outs[1:])


def _mlstm_sample(zm3, if3, ifr, gamma, c_all, n0, m0x, c_out_prev, l, t_valid, nseq):
    bs, L, _ = zm3.shape
    depth = c_all.shape[0]
    hw = M_HEADS * M_DK
    aliased = c_out_prev is not None

    def blk(col):
        return pl.BlockSpec((nseq, L, hw), lambda b, c: (b, 0, col))

    c_in, n_spec, m_spec = _mlstm_state_specs(nseq, l)
    in_specs = [pl.BlockSpec((nseq, L, LANES), lambda b, c: (b, 0, 0)),
                pl.BlockSpec((nseq, 2 * M_HEADS, L), lambda b, c: (b, 0, 0)),
                blk(0), blk(1), blk(2), blk(3),
                pl.BlockSpec((None, 1, hw), lambda b, c: (l, 0, 0)),
                c_in, n_spec, m_spec]
    args = [if3, ifr, zm3, zm3, zm3, zm3, gamma, c_all, n0, m0x]
    aliases = {}
    if aliased:
        in_specs.append(pl.BlockSpec(memory_space=pl.ANY))
        args.append(c_out_prev)
        aliases = {len(args) - 1: 1}
        c_out = c_in
    else:
        c_out = pl.BlockSpec((depth, nseq, M_HEADS, M_DK, M_DV), lambda b, c: (0, b, 0, 0, 0))
    kern = functools.partial(_mlstm_kernel, t_valid=t_valid, nseq=nseq, aliased=aliased, carried=False,
                             chunk_axis=1, first_layer=None if aliased else l)
    return pl.pallas_call(
        kern,
        grid=(bs // nseq, 1),
        in_specs=in_specs,
        out_specs=[pl.BlockSpec((nseq, L, hw), lambda b, c: (b, 0, 0)), c_out, n_spec, m_spec],
        out_shape=[jax.ShapeDtypeStruct((bs, L, hw), BF16),
                   jax.ShapeDtypeStruct((depth, bs, M_HEADS, M_DK, M_DV), F32),
                   jax.ShapeDtypeStruct((bs, M_HEADS, 1, M_DK), F32),
                   jax.ShapeDtypeStruct((bs, 2 * M_HEADS, LANES), F32)],
        input_output_aliases=aliases,
        compiler_params=_cparams("parallel", "arbitrary"),
        name="mlstm_sample",
    )(*args)


def _m_state_in(m):
    lead = m.shape[:-1]
    mx = jnp.zeros(lead + (2 * M_HEADS, LANES), F32)
    return mx.at[..., :M_HEADS, :].set(jnp.broadcast_to(m[..., None], lead + (M_HEADS, LANES)))


def _merge_kernel(*refs, np_tiles, alpha, route):
    (cp_ref, cs_ref, ap_ref, as_ref, mp_ref, ms_ref, g0_ref, g1_ref, g2_ref, xp_ref, xs_ref,
     gtp_ref, gts_ref, shp_ref, shs_ref, scp_ref, scs_ref,
     wc_ref, wa_ref, wm_ref, wo_ref, pg_ref, pb_ref) = refs[:23]
    if route:
        rw_ref, rb_ref, x1_ref, h_ref, route_ref = refs[23:]
    else:
        x1_ref, h_ref = refs[23:]
    is_s = pl.program_id(0) >= np_tiles

    def gate(ref):
        return _sigmoid(ref[...].astype(F32))

    def run(c_ref, a_ref, m_ref, x_ref, gt_ref, sh_ref, sc_ref):
        y = (gate(g0_ref) * _dot(c_ref[...], wc_ref[...])
             + gate(g1_ref) * _dot(a_ref[...], wa_ref[...])
             + gate(g2_ref) * _dot(m_ref[...], wm_ref[...]))
        mix = _dot(y.astype(BF16), wo_ref[...])
        x1 = _norm(alpha * x_ref[...] + gt_ref[...] * mix) * pg_ref[...] + pb_ref[...]
        x1_ref[...] = x1
        h = _norm(x1) * (1.0 + sc_ref[...]) + sh_ref[...]
        h_ref[...] = h.astype(h_ref.dtype)
        if route:
            h_hi, h_lo = _split_bf16(h)
            w_hi, w_lo = _split_bf16(rw_ref[...])
            logits = _dot(h_hi, w_hi) + (_dot(h_hi, w_lo) + _dot(h_lo, w_hi)) + rb_ref[...]
            lane = lax.broadcasted_iota(jnp.int32, logits.shape, 1)
            m1 = jnp.max(logits, axis=-1, keepdims=True)
            e1 = jnp.min(jnp.where(logits == m1, lane, LANES), axis=-1, keepdims=True)
            l2 = jnp.where(lane == e1, NEG_INF, logits)
            m2 = jnp.max(l2, axis=-1, keepdims=True)
            e2 = jnp.min(jnp.where(l2 == m2, lane, LANES), axis=-1, keepdims=True)
            ex = jnp.exp(m2 - m1)
            w1 = 1.0 / (1.0 + ex)
            w2 = ex / (1.0 + ex)
            out = jnp.where(lane == 0, e1.astype(F32),
                            jnp.where(lane == 1, e2.astype(F32),
                                      jnp.where(lane == 2, w1, jnp.where(lane == 3, w2, 0.0))))
            route_ref[...] = out

    @pl.when(is_s)
    def _():
        run(cs_ref, as_ref, ms_ref, xs_ref, gts_ref, shs_ref, scs_ref)

    @pl.when(jnp.logical_not(is_s))
    def _():
        run(cp_ref, ap_ref, mp_ref, xp_ref, gtp_ref, shp_ref, scp_ref)


def _merge(cp, cs, ap, as_, mp, ms, z, xp, xs, mod_p, mod_s, wc, wa, wm, wo, pg, pb, l, rm, alpha, router):
    n, d = z.shape[0], xp.shape[1]
    hw = D_CONV
    route = router is not None
    tm = rm.t

    def pblk():
        return pl.BlockSpec((tm, hw), lambda i: (rm.prompt(i), 0))

    def sblk():
        return pl.BlockSpec((tm, hw), lambda i: (rm.sample(i), 0), pipeline_mode=_RARELY)

    def zg(k):
        return pl.BlockSpec((tm, d), lambda i: (i, Z_G // d + k))

    def lw(a):
        return pl.BlockSpec((None,) + a.shape[1:], lambda i: (l,) + (0,) * (a.ndim - 1), pipeline_mode=_RARELY)

    post = pl.BlockSpec((None, None, 1, d), lambda i: (l, 0, 0, 0))
    in_specs = [pblk(), sblk(), pblk(), sblk(), pblk(), sblk(), zg(0), zg(1), zg(2),
                *rm.x_specs(xs, d),
                *rm.mod_specs(l, 2, d), *rm.mod_specs(l, 3, d), *rm.mod_specs(l, 4, d),
                lw(wc), lw(wa), lw(wm), lw(wo), post, post]
    args = [cp, cs, ap, as_, mp, ms, z, z, z, xp, xs, mod_p, mod_s, mod_p, mod_s, mod_p, mod_s,
            wc, wa, wm, wo, pg, pb]
    out_specs = [pl.BlockSpec((tm, d), lambda i: (i, 0)), pl.BlockSpec((tm, d), lambda i: (i, 0))]
    out_shape = [jax.ShapeDtypeStruct((n, d), F32), jax.ShapeDtypeStruct((n, d), F32 if route else BF16)]
    if route:
        rw, rb, lj = router
        in_specs += [pl.BlockSpec((None, d, LANES), lambda i: (lj, 0, 0)),
                     pl.BlockSpec((None, 1, LANES), lambda i: (lj, 0, 0))]
        args += [rw, rb]
        out_specs.append(pl.BlockSpec((tm, LANES), lambda i: (i, 0)))
        out_shape.append(jax.ShapeDtypeStruct((n, LANES), F32))
    kern = functools.partial(_merge_kernel, np_tiles=rm.np_tiles, alpha=alpha, route=route)
    return pl.pallas_call(
        kern, grid=(n // tm,), in_specs=in_specs, out_specs=out_specs, out_shape=out_shape,
        compiler_params=_cparams("parallel"), name="merge_route" if route else "merge",
    )(*args)


def _post_residual(x_ref, gp_ref, gs_ref, f, pg_ref, pb_ref, is_s, alpha):
    return _norm(alpha * x_ref[...] + _pick(is_s, gp_ref, gs_ref) * f) * pg_ref[...] + pb_ref[...]


def _ffn_kernel(h_ref, wg_ref, wu_ref, wd_ref, x_ref, gp_ref, gs_ref, pg_ref, pb_ref, o_ref, *, alpha, np_tiles):
    h = h_ref[...]
    a = (_silu(_dot(h, wg_ref[...])) * _dot(h, wu_ref[...])).astype(BF16)
    is_s = pl.program_id(0) >= np_tiles
    o_ref[...] = _post_residual(x_ref, gp_ref, gs_ref, _dot(a, wd_ref[...]), pg_ref, pb_ref, is_s, alpha)


def _ffn_dense(h, wg, wu, wd, x, mod_p, mod_s, pg, pb, l, lj, rm, alpha):
    n, d = x.shape
    f = wg.shape[2]
    tm = rm.t
    post = pl.BlockSpec((None, None, 1, d), lambda i: (l, 1, 0, 0))
    return pl.pallas_call(
        functools.partial(_ffn_kernel, alpha=alpha, np_tiles=rm.np_tiles),
        grid=(n // tm,),
        in_specs=[pl.BlockSpec((tm, d), lambda i: (i, 0)),
                  pl.BlockSpec((None, d, f), lambda i: (lj, 0, 0), pipeline_mode=_RARELY),
                  pl.BlockSpec((None, d, f), lambda i: (lj, 0, 0), pipeline_mode=_RARELY),
                  pl.BlockSpec((None, f, d), lambda i: (lj, 0, 0), pipeline_mode=_RARELY),
                  pl.BlockSpec((tm, d), lambda i: (i, 0)),
                  *rm.mod_specs(l, 5, d), post, post],
        out_specs=pl.BlockSpec((tm, d), lambda i: (i, 0)),
        out_shape=jax.ShapeDtypeStruct((n, d), F32),
        compiler_params=_cparams("parallel"),
        name="ffn_dense",
    )(h, wg, wu, wd, x, mod_p, mod_s, pg, pb)


def _rank_kernel(route_ref, rank_ref, tot_ref, carry):
    i = pl.program_id(0)
    tm = route_ref.shape[0]

    @pl.when(i == 0)
    def _():
        carry[...] = jnp.zeros_like(carry)

    r = route_ref[...]
    lane = lax.broadcasted_iota(jnp.int32, (tm, LANES), 1)
    e1 = r[:, 0:1].astype(jnp.int32)
    e2 = r[:, 1:2].astype(jnp.int32)
    hit1 = lane == e1
    hit2 = lane == e2
    onehot = jnp.where(jnp.logical_or(hit1, hit2), 1.0, 0.0)
    tt = lax.broadcasted_iota(jnp.int32, (tm, tm), 0)
    ss = lax.broadcasted_iota(jnp.int32, (tm, tm), 1)
    before = jnp.where(ss < tt, 1.0, 0.0).astype(BF16)
    cnt = _dot(before, onehot.astype(BF16)) + carry[0:1, :]
    r1 = jnp.sum(jnp.where(hit1, cnt, 0.0), axis=-1, keepdims=True)
    r2 = jnp.sum(jnp.where(hit2, cnt, 0.0), axis=-1, keepdims=True)
    rank_ref[...] = jnp.where(lane == 0, r1, jnp.where(lane == 1, r2, 0.0))
    carry[...] = carry[...] + jnp.sum(onehot, axis=0, keepdims=True)
    tot_ref[...] = carry[...]


def _moe_rank(route, tm):
    n = route.shape[0]
    return pl.pallas_call(
        _rank_kernel,
        grid=(n // tm,),
        in_specs=[pl.BlockSpec((tm, LANES), lambda i: (i, 0))],
        out_specs=[pl.BlockSpec((tm, LANES), lambda i: (i, 0)),
                   pl.BlockSpec((8, LANES), lambda i: (0, 0))],
        out_shape=[jax.ShapeDtypeStruct((n, LANES), F32), jax.ShapeDtypeStruct((8, LANES), F32)],
        scratch_shapes=[pltpu.VMEM((8, LANES), F32)],
        compiler_params=_cparams("arbitrary"),
        name="moe_rank",
    )(route)


def _row_copy(src, s, dst, t, sem):
    return pltpu.make_async_copy(src.at[pl.ds(s, 1)], dst.at[pl.ds(t, 1)], sem)


def _dispatch_kernel(d1_ref, d2_ref, zl_ref, h_ref, xs_hbm, zbuf, stage, sem, zsem):
    i = pl.program_id(0)
    tm = h_ref.shape[0]

    @pl.when(i == 0)
    def _():
        zbuf[...] = jnp.zeros_like(zbuf)

        def zero_copy(k):
            row = pl.multiple_of(jnp.maximum(zl_ref[k], 0), MOE_SUB)
            return pltpu.make_async_copy(zbuf, xs_hbm.at[pl.ds(row, MOE_SUB)], zsem)

        def start(k, carry):
            @pl.when(zl_ref[k] >= 0)
            def _():
                zero_copy(k).start()
            return carry

        def wait(k, carry):
            @pl.when(zl_ref[k] >= 0)
            def _():
                zero_copy(k).wait()
            return carry

        lax.fori_loop(0, zl_ref.shape[0], start, 0)
        lax.fori_loop(0, zl_ref.shape[0], wait, 0)

    slot = i % 2
    stage[slot] = h_ref[...]

    def issue(r, carry):
        _row_copy(stage.at[slot], r, xs_hbm, d1_ref[i * tm + r], sem.at[slot]).start(priority=0)
        _row_copy(stage.at[slot], r, xs_hbm, d2_ref[i * tm + r], sem.at[slot]).start(priority=1)
        return carry

    lax.fori_loop(0, tm, issue, 0, unroll=8)

    def drain(s):
        for _ in range(2):
            pltpu.make_async_copy(stage.at[s], xs_hbm.at[pl.ds(0, tm)], sem.at[s]).wait()

    @pl.when(i > 0)
    def _():
        drain(1 - slot)

    @pl.when(i == pl.num_programs(0) - 1)
    def _():
        drain(slot)


def _moe_dispatch(dest1, dest2, zlist, h, n_rows, tm):
    n, d = h.shape
    grid_spec = pltpu.PrefetchScalarGridSpec(
        num_scalar_prefetch=3,
        grid=(n // tm,),
        in_specs=[pl.BlockSpec((tm, d), lambda i, *_: (i, 0))],
        out_specs=pl.BlockSpec(memory_space=pl.ANY),
        scratch_shapes=[pltpu.VMEM((MOE_SUB, d), F32), pltpu.VMEM((2, tm, d), F32),
                        pltpu.SemaphoreType.DMA((2,)), pltpu.SemaphoreType.DMA(())],
    )
    return pl.pallas_call(
        _dispatch_kernel, grid_spec=grid_spec,
        out_shape=jax.ShapeDtypeStruct((n_rows, d), F32),
        compiler_params=_cparams("arbitrary"),
        name="moe_dispatch",
    )(dest1, dest2, zlist, h)


def _expert_kernel(blk_e_ref, nvalid_ref, xs_ref, wg_ref, wu_ref, wd_ref, y_ref, xb, acc):
    b = pl.program_id(0)
    j = pl.program_id(1)
    tb = xb.shape[0]
    nsub = tb // MOE_SUB
    nv = nvalid_ref[b]
    used = (nv + MOE_SUB - 1) // MOE_SUB

    def run(rows):
        wg = wg_ref[...].astype(BF16)
        wu = wu_ref[...].astype(BF16)
        wd = wd_ref[...].astype(BF16)

        @pl.when(j == 0)
        def _():
            xb[rows, :] = xs_ref[rows, :].astype(BF16)

        x = xb[rows, :]
        a = (_silu(_dot(x, wg)) * _dot(x, wu)).astype(BF16)
        part = _dot(a, wd)

        @pl.when(j == 0)
        def _():
            acc[rows, :] = part

        @pl.when(j > 0)
        def _():
            acc[rows, :] = acc[rows, :] + part

        @pl.when(j == pl.num_programs(1) - 1)
        def _():
            y_ref[rows, :] = acc[rows, :]

    for m in range(1, nsub + 1):
        @pl.when(used == m)
        def _():
            run(pl.ds(0, m * MOE_SUB))

    for s in range(nsub):
        @pl.when(jnp.logical_and(s >= used, j == pl.num_programs(1) - 1))
        def _():
            y_ref[pl.ds(s * MOE_SUB, MOE_SUB), :] = jnp.zeros((MOE_SUB, y_ref.shape[1]), F32)


def _moe_experts(blk_e, nvalid, xs, wg, wu, wd, lj, tb, tf):
    n_rows, d = xs.shape
    f = wg.shape[3]
    nj = f // tf

    def jm(b, j, nv):
        return jnp.where(nv[b] > 0, j, nj - 1)

    grid_spec = pltpu.PrefetchScalarGridSpec(
        num_scalar_prefetch=2,
        grid=(n_rows // tb, nj),
        in_specs=[pl.BlockSpec((tb, d), lambda b, j, be, nv: (b, 0)),
                  pl.BlockSpec((None, None, d, tf), lambda b, j, be, nv: (lj, be[b], 0, jm(b, j, nv))),
                  pl.BlockSpec((None, None, d, tf), lambda b, j, be, nv: (lj, be[b], 0, jm(b, j, nv))),
                  pl.BlockSpec((None, None, tf, d), lambda b, j, be, nv: (lj, be[b], jm(b, j, nv), 0))],
        out_specs=pl.BlockSpec((tb, d), lambda b, j, be, nv: (b, 0)),
        scratch_shapes=[pltpu.VMEM((tb, d), BF16), pltpu.VMEM((tb, d), F32)],
    )
    return pl.pallas_call(
        _expert_kernel, grid_spec=grid_spec,
        out_shape=jax.ShapeDtypeStruct((n_rows, d), F32),
        compiler_params=_cparams("parallel", "arbitrary"),
        name="moe_experts",
    )(blk_e, nvalid, xs, wg, wu, wd)


def _combine_kernel(d1_ref, d2_ref, ys_hbm, route_ref, x_ref, gp_ref, gs_ref, pg_ref, pb_ref, *rest,
                    alpha, np_tiles, split):
    (*outs, y1, y2, sem1, sem2) = rest
    i = pl.program_id(0)
    tm = x_ref.shape[0]
    slot = i % 2

    last = pl.num_programs(0) - 1

    def issue(tile, s, r):
        _row_copy(ys_hbm, d1_ref[tile * tm + r], y1.at[s], r, sem1.at[s]).start(priority=0)
        _row_copy(ys_hbm, d2_ref[tile * tm + r], y2.at[s], r, sem2.at[s]).start(priority=1)

    def drain(s):
        pltpu.make_async_copy(ys_hbm.at[pl.ds(0, tm)], y1.at[s], sem1.at[s]).wait()
        pltpu.make_async_copy(ys_hbm.at[pl.ds(0, tm)], y2.at[s], sem2.at[s]).wait()

    @pl.when(i == 0)
    def _():
        lax.fori_loop(0, tm, lambda r, c: (issue(0, 0, r), c)[1], 0, unroll=8)

    drain(slot)
    nxt = jnp.minimum(i + 1, last)
    for r in range(tm):
        issue(nxt, 1 - slot, r)
    r = route_ref[...]
    f = y1[slot] * r[:, 2:3] + y2[slot] * r[:, 3:4]
    out = _post_residual(x_ref, gp_ref, gs_ref, f, pg_ref, pb_ref, i >= np_tiles, alpha)
    if split:
        op_ref, os_ref = outs

        @pl.when(i < np_tiles)
        def _():
            op_ref[...] = out

        @pl.when(i >= np_tiles)
        def _():
            os_ref[...] = out
    else:
        outs[0][...] = out

    @pl.when(i == last)
    def _():
        drain(1 - slot)


def _moe_combine(dest1, dest2, ys, route, x, mod_p, mod_s, pg, pb, l, rm, alpha, split):
    n, d = x.shape
    tm = rm.t
    if split:
        n_p = rm.np_tiles * tm
        out_specs = [pl.BlockSpec((tm, d), lambda i, *_: (rm.prompt(i), 0)),
                     pl.BlockSpec((tm, d), lambda i, *_: (rm.sample(i), 0))]
        out_shape = [jax.ShapeDtypeStruct((n_p, d), F32), jax.ShapeDtypeStruct((n - n_p, d), F32)]
    else:
        out_specs = pl.BlockSpec((tm, d), lambda i, *_: (i, 0))
        out_shape = jax.ShapeDtypeStruct((n, d), F32)
    post = pl.BlockSpec((None, None, 1, d), lambda i, *_: (l, 1, 0, 0))
    grid_spec = pltpu.PrefetchScalarGridSpec(
        num_scalar_prefetch=2,
        grid=(n // tm,),
        in_specs=[pl.BlockSpec(memory_space=pl.ANY),
                  pl.BlockSpec((tm, LANES), lambda i, *_: (i, 0)),
                  pl.BlockSpec((tm, d), lambda i, *_: (i, 0)),
                  *rm.mod_specs(l, 5, d), post, post],
        out_specs=out_specs,
        scratch_shapes=[pltpu.VMEM((2, tm, d), F32), pltpu.VMEM((2, tm, d), F32),
                        pltpu.SemaphoreType.DMA((2,)), pltpu.SemaphoreType.DMA((2,))],
    )
    return pl.pallas_call(
        functools.partial(_combine_kernel, alpha=alpha, np_tiles=rm.np_tiles, split=split), grid_spec=grid_spec,
        out_shape=out_shape,
        compiler_params=_cparams("arbitrary"),
        name="moe_combine",
    )(dest1, dest2, ys, route, x, mod_p, mod_s, pg, pb)


def _moe_ffn(h, route, wg, wu, wd, x, mod_p, mod_s, pg, pb, l, lj, rm, tb, tf, alpha, split):
    n = h.shape[0]
    rank, tot = _moe_rank(route, rm.t)
    counts = tot[0, :N_EXPERTS].astype(jnp.int32)
    padded = (counts + tb - 1) // tb * tb
    pad_end = jnp.cumsum(padded)
    pad_start = pad_end - padded
    experts = jnp.arange(N_EXPERTS, dtype=jnp.int32)

    def slot(col):
        e = route[:, col].astype(jnp.int32)
        start = jnp.sum(jnp.where(e[:, None] == experts[None, :], pad_start[None, :], 0), axis=1)
        return start + rank[:, col].astype(jnp.int32)

    dest1, dest2 = slot(0), slot(1)
    n_blocks = -(-(2 * n + N_EXPERTS * (tb - 1)) // tb)
    blk_start = jnp.arange(n_blocks, dtype=jnp.int32) * tb
    blk_e = jnp.minimum(jnp.sum(pad_end[None, :] <= blk_start[:, None], axis=1), N_EXPERTS - 1).astype(jnp.int32)
    own_end = jnp.sum(jnp.where(blk_e[:, None] == experts[None, :], (pad_start + counts)[None, :], 0), axis=1)
    nvalid = jnp.clip(own_end - blk_start, 0, tb).astype(jnp.int32)
    nvalid = jnp.where(blk_start < pad_end[-1], nvalid, 0)
    per_blk = tb // MOE_SUB
    sub_start = jnp.arange(n_blocks * per_blk, dtype=jnp.int32) * MOE_SUB
    sub_room = jnp.repeat(blk_start + nvalid, per_blk) - sub_start
    zlist = jnp.where(sub_room < MOE_SUB, sub_start, -1).astype(jnp.int32)
    xs = _moe_dispatch(dest1, dest2, zlist, h, n_blocks * tb, rm.t)
    ys = _moe_experts(blk_e, nvalid, xs, wg, wu, wd, lj, tb, tf)
    return _moe_combine(dest1, dest2, ys, route, x, mod_p, mod_s, pg, pb, l, rm, alpha, split)


def _pack_w_in(w_in, b_in):
    depth, d, _ = w_in.shape
    q_end = 2 * D_CONV + N_HEADS * HEAD_DIM
    k_end = q_end + N_KV * HEAD_DIM
    a_end = k_end + N_KV * HEAD_DIM
    m_end = a_end + 4 * M_HEADS * M_DK
    if_end = m_end + 2 * M_HEADS

    def pack(a):
        return jnp.concatenate([a[..., if_end:], a[..., :q_end], a[..., a_end:m_end], a[..., q_end:a_end]], axis=-1)

    def gates(a):
        return jnp.pad(a[..., m_end:if_end], [(0, 0)] * (a.ndim - 1) + [(0, LANES - 2 * M_HEADS)])

    return (pack(w_in).astype(BF16), pack(b_in).reshape(depth, 1, Z_W),
            gates(w_in).astype(BF16), gates(b_in).reshape(depth, 1, LANES))


def kernel(x_prompt, x_sample, cache_swa_k, cache_swa_v, state_conv, state_mlstm_C, state_mlstm_n, state_mlstm_m, c_prompt, c_sample, w_ada, b_ada, w_in, b_in, conv_w, conv_b, conv_ln_g, conv_ln_b, w_conv_out, attn_sinks, rel_bias, w_attn_out, m_norm_g, w_m_out, w_out, post_ln_g, post_ln_b, ffn_w_gate, ffn_w_up, ffn_w_down, router_w, router_b, moe_w_gate, moe_w_up, moe_w_down):
    bp, tp, d = x_prompt.shape
    bs, ts, _ = x_sample.shape
    depth = w_ada.shape[0]
    alpha = (2 * depth) ** 0.25
    n_p, n_s = bp * tp, bs * ts
    tm = n_s
    assert d == D_MODEL and tp % tm == 0 and tp % WINDOW == 0 and tm % 32 == 0
    rm = _RowMap(tm, bp, tp, n_p, n_s)
    rm_half = _RowMap(tm // 2, bp, tp, n_p, n_s)
    wb = cache_swa_k.shape[2]
    big = n_p >= 4096
    tc = 512 if big else tm
    lm = 256 if big else min(tp, 128)
    tb = 1024 if big else 2 * MOE_SUB
    bs_blk = 32 if bs % 32 == 0 else bs
    bs_att = 16 if bs % 16 == 0 else bs
    bs_m = 4 if bs % 4 == 0 else 1
    lts = 16

    n = n_p + n_s
    x = (x_prompt.reshape(n_p, d), jnp.transpose(x_sample, (1, 0, 2)).reshape(n_s, d))

    nc_rows = -(-(bp + bs) // 8) * 8
    c_all = jnp.zeros((nc_rows, d), F32).at[:bp].set(c_prompt).at[bp:bp + bs].set(c_sample)
    mod = _ada_mod(c_all, w_ada, b_ada)
    mod_p = mod[:, :bp].reshape(depth, bp, 1, 6 * d)
    mod_s = jnp.tile(mod[:, bp:bp + bs], (1, ts, 1))

    w_in_p, b_in_p, w_if, b_if = _pack_w_in(w_in, b_in)
    wc_b, wa_b, wm_b, wo_b = (w.astype(BF16) for w in (w_conv_out, w_attn_out, w_m_out, w_out))
    fg_b, fu_b, fd_b = (w.astype(BF16) for w in (ffn_w_gate, ffn_w_up, ffn_w_down))
    cw_pad = jnp.pad(conv_w, ((0, 0), (0, CONV_PAD - CONV_W), (0, 0)))
    cvecs = [v.reshape(depth, 1, D_CONV) for v in (conv_b, conv_ln_g, conv_ln_b)]
    rw_pad = jnp.pad(router_w, ((0, 0), (0, 0), (0, LANES - N_EXPERTS)))
    rb_pad = jnp.pad(router_b, ((0, 0), (0, LANES - N_EXPERTS)), constant_values=NEG_INF)[:, None, :]
    pg = post_ln_g.reshape(depth, 2, 1, d)
    pb = post_ln_b.reshape(depth, 2, 1, d)
    gamma = m_norm_g.reshape(depth, 1, M_HEADS * M_DV)
    sinks = attn_sinks.astype(F32)

    qi = jnp.arange(WINDOW)[:, None]
    kj = jnp.arange(2 * WINDOW)[None, :]
    dist_p = qi + WINDOW - kj
    bh = _bias_heads(rel_bias, dist_p, (dist_p >= 0) & (dist_p < WINDOW))
    bias_p = bh.reshape(N_HEADS // 2, 2, WINDOW, 2 * WINDOW).transpose(0, 2, 1, 3).reshape(
        N_HEADS // 2, WINDOW, 4 * WINDOW)
    dist_s = jnp.arange(ts)[:, None] + wb - jnp.arange(wb + ts)[None, :]
    bias_s = _bias_heads(rel_bias, dist_s, (dist_s >= 0) & (dist_s < WINDOW)).reshape(
        N_KV, Q_PER_KV * ts, wb + ts)
    kc_all = cache_swa_k.reshape(depth, bs, wb, N_KV * HEAD_DIM)
    vc_all = cache_swa_v.reshape(depth, bs, wb, N_KV * HEAD_DIM)

    f_moe = moe_w_gate.shape[3]
    tf_moe = 512 if f_moe % 512 == 0 else f_moe

    n0_all = state_mlstm_n[:, :, :, None, :]
    m0_all = _m_state_in(state_mlstm_m)
    s_c = None
    new_p = [[] for _ in range(6)]
    new_s = [[] for _ in range(6)]
    for l in range(depth):
        j = l // 2
        xp, xs = x if isinstance(x, tuple) else (x, x)
        z, zif = _ln_proj(xp, xs, mod_p, mod_s, w_in_p, b_in_p, w_if, b_if, l, rm, n)
        zs3 = z[n_p:].reshape(ts, bs, Z_W)

        cp, ns_p = _conv_prompt(z, jnp.zeros((bp, CONV_PAD, D_CONV), F32), cw_pad, *cvecs, l, bp, tp, tc)
        cs3, a_s3 = _conv_sample(zs3, state_conv, cw_pad, *cvecs, l, bs_blk)
        new_p[2].append(ns_p[:, CONV_PAD - CONV_W + 1:])
        new_s[2].append(jnp.transpose(a_s3, (1, 0, 2)))

        sink_h = sinks[l].reshape(N_KV, Q_PER_KV, 1)
        sink_s = jnp.broadcast_to(sink_h, (N_KV, Q_PER_KV, ts)).reshape(N_KV, Q_PER_KV * ts, 1)
        ap = _attn_prompt(z, bias_p * LOG2E, sinks * LOG2E, l, bp, tp)
        nk = min(WINDOW, tp)
        kv_tail = jnp.stack([z[(b + 1) * tp - nk:(b + 1) * tp, Z_K:Z_K + 2 * N_KV * HEAD_DIM]
                             for b in range(bp)]).astype(F32)
        new_p[0].append(kv_tail[..., :N_KV * HEAD_DIM].reshape(bp, nk, N_KV, HEAD_DIM))
        new_p[1].append(kv_tail[..., N_KV * HEAD_DIM:].reshape(bp, nk, N_KV, HEAD_DIM))
        q_s = zs3[:, :, Z_Q:Z_Q + N_HEADS * HEAD_DIM].reshape(ts, bs, N_KV, Q_PER_KV, HEAD_DIM)
        q4 = jnp.transpose(q_s, (1, 2, 3, 0, 4)).reshape(bs, N_KV, Q_PER_KV * ts, HEAD_DIM)
        k_s = jnp.transpose(zs3[:, :, Z_K:Z_K + N_KV * HEAD_DIM].reshape(ts, bs, N_KV, HEAD_DIM), (1, 0, 2, 3))
        v_s = jnp.transpose(zs3[:, :, Z_V:Z_V + N_KV * HEAD_DIM].reshape(ts, bs, N_KV, HEAD_DIM), (1, 0, 2, 3))
        o4 = _attn_sample(q4, kc_all, vc_all, jnp.transpose(k_s, (0, 2, 1, 3)), jnp.transpose(v_s, (0, 2, 1, 3)),
                          bias_s, sink_s, l, bs_att)
        as_ = jnp.transpose(o4.reshape(bs, N_KV, Q_PER_KV, ts, HEAD_DIM), (3, 0, 1, 2, 4)).reshape(n_s, -1).astype(BF16)
        new_s[0].append(k_s.astype(F32))
        new_s[1].append(v_s.astype(F32))

        ncp = tp // lm
        if_p = zif[:n_p, :2 * M_HEADS].reshape(bp * ncp, lm, 2 * M_HEADS)
        mp, c1p, n1p, m1p = _mlstm_prompt(z, zif, jnp.transpose(if_p, (0, 2, 1)), gamma, l, bp, ncp, lm)
        new_p[3].append(c1p)
        new_p[4].append(n1p[:, :, 0])
        new_p[5].append(m1p[:, :M_HEADS, 0])
        tpad = ((0, 0), (0, lts - ts), (0, 0))
        zm3 = jnp.pad(jnp.transpose(zs3[:, :, Z_MQ:Z_K], (1, 0, 2)), tpad)
        if3 = jnp.pad(jnp.transpose(zif[n_p:].reshape(ts, bs, LANES), (1, 0, 2)), tpad)
        ms, s_c, n1s, m1s = _mlstm_sample(zm3, if3, jnp.transpose(if3[:, :, :2 * M_HEADS], (0, 2, 1)), gamma,
                                          state_mlstm_C, n0_all[l], m0_all[l], s_c, l, ts, bs_m)
        ms = jnp.transpose(ms[:, :ts], (1, 0, 2)).reshape(n_s, -1)
        new_s[4].append(n1s[:, :, 0])
        new_s[5].append(m1s[:, :M_HEADS, 0])

        moe = l % 2 == 1
        router = (rw_pad, rb_pad, j) if moe else None
        outs = _merge(cp, cs3.reshape(n_s, D_CONV), ap, as_, mp, ms, z, xp, xs, mod_p, mod_s,
                      wc_b, wa_b, wm_b, wo_b, pg, pb, l, rm, alpha, router)
        if moe:
            x1, h2, route = outs
            x = _moe_ffn(h2, route, moe_w_gate, moe_w_up, moe_w_down, x1, mod_p, mod_s, pg, pb,
                         l, j, rm, tb, tf_moe, alpha, split=l == depth - 1)
        else:
            x1, h2 = outs
            x = _ffn_dense(h2, fg_b, fu_b, fd_b, x1, mod_p, mod_s, pg, pb, l, j, rm, alpha)

    x_p, x_s = x if isinstance(x, (list, tuple)) else (x[:n_p], x[n_p:])
    y_p = x_p.reshape(bp, tp, d)
    y_s = jnp.transpose(x_s.reshape(ts, bs, d), (1, 0, 2))
    p_k, p_v, p_conv, p_c, p_n, p_m = [jnp.stack(a) for a in new_p]
    s_k, s_v, s_conv = [jnp.concatenate([old[:, :, ts:], jnp.stack(new)], axis=2)
                        for old, new in zip((cache_swa_k, cache_swa_v, state_conv), new_s[:3])]
    s_n, s_m = jnp.stack(new_s[4]), jnp.stack(new_s[5])
    return (y_p, y_s, p_k, p_v, p_conv, p_c, p_n, p_m, s_k, s_v, s_conv, s_c, s_n, s_m)
```

```python
import functools
import math

import jax
import jax.numpy as jnp
from jax import lax
from jax.experimental import pallas as pl
from jax.experimental.pallas import tpu as pltpu

F32 = jnp.float32
BF16 = jnp.bfloat16

D_MODEL = 1024
D_CONV = 512
CONV_W = 31
CONV_PAD = 32
N_HEADS = 8
N_KV = 2
HEAD_DIM = 64
Q_PER_KV = N_HEADS // N_KV
WINDOW = 128
N_BUCKETS = 32
MAX_DIST = 128
M_HEADS = 4
M_DK = 128
M_DV = 128
N_EXPERTS = 8
LN_EPS = 1e-5
LANES = 128
NEG_INF = float("-inf")
LOG2E = math.log2(math.e)
VMEM_LIMIT = 56 * 1024 * 1024

Z_G, Z_UA, Z_UB, Z_Q = 0, 3072, 3584, 4096
Z_MQ, Z_MK, Z_MV, Z_MO = 4608, 5120, 5632, 6144
Z_K, Z_V, Z_W = 6656, 6784, 6912
TN_IN = 6912
ATT_QB = 4
ATT_GROUP = 8
MOE_SUB = 256
CONV_CHUNK = 32

def _cparams(*sem):
    return pltpu.CompilerParams(dimension_semantics=sem, vmem_limit_bytes=VMEM_LIMIT)


def _sigmoid(x):
    return 1.0 / (1.0 + jnp.exp(-x))


def _silu(x):
    return x * _sigmoid(x)


def _log_sigmoid(x):
    return jnp.minimum(x, 0.0) - jnp.log(1.0 + jnp.exp(-jnp.abs(x)))


def _norm(x):
    mu = jnp.mean(x, axis=-1, keepdims=True)
    xc = x - mu
    var = jnp.mean(xc * xc, axis=-1, keepdims=True)
    return xc * lax.rsqrt(var + LN_EPS)


def _dot(a, b):
    return jnp.dot(a, b, preferred_element_type=F32)


def _dot_nt(a, b):
    return lax.dot_general(a, b, (((1,), (1,)), ((), ())), preferred_element_type=F32)


def _dot_tn(a, b):
    return lax.dot_general(a, b, (((0,), (0,)), ((), ())), preferred_element_type=F32)


def _dot_hi(a, b):
    return jnp.dot(a, b, preferred_element_type=F32, precision=lax.Precision.HIGHEST)


def _split_bf16(a):
    hi = a.astype(BF16)
    return hi, (a - hi.astype(F32)).astype(BF16)


_RARELY = pl.Buffered(1)


class _RowMap:
    def __init__(self, t, bp, tp, n_p, n_s):
        self.t = t
        self.bp = bp
        self.per_seq = tp // t
        self.np_tiles = n_p // t
        self.ns_tiles = n_s // t

    def seq(self, i):
        return jnp.minimum(i // self.per_seq, self.bp - 1)

    def prompt(self, i):
        return jnp.minimum(i, self.np_tiles - 1)

    def sample(self, i):
        return jnp.clip(i - self.np_tiles, 0, self.ns_tiles - 1)

    def mod_specs(self, l, k, d):
        return (pl.BlockSpec((None, None, 1, d), lambda i, *_: (l, self.seq(i), 0, k)),
                pl.BlockSpec((None, self.t, d), lambda i, *_: (l, self.sample(i), k), pipeline_mode=_RARELY))

    def x_specs(self, xs_arr, d):
        off = self.np_tiles if xs_arr.shape[0] > self.ns_tiles * self.t else 0
        return (pl.BlockSpec((self.t, d), lambda i, *_: (self.prompt(i), 0)),
                pl.BlockSpec((self.t, d), lambda i, *_: (off + self.sample(i), 0), pipeline_mode=_RARELY))


def _pick(is_s, p_ref, s_ref):
    return jnp.where(is_s, s_ref[...], p_ref[...])


def _ada_kernel(c_ref, w_ref, b_ref, o_ref):
    s = _silu(c_ref[...]).astype(BF16)
    o_ref[0] = _dot(s, w_ref[0].astype(BF16)) + b_ref[0]


def _ada_mod(c_all, w_ada, b_ada):
    depth, d, n6 = w_ada.shape
    rows = c_all.shape[0]
    return pl.pallas_call(
        _ada_kernel,
        grid=(depth, n6 // d),
        in_specs=[pl.BlockSpec((rows, d), lambda l, j: (0, 0)),
                  pl.BlockSpec((1, d, d), lambda l, j: (l, 0, j)),
                  pl.BlockSpec((1, 1, d), lambda l, j: (l, 0, j))],
        out_specs=pl.BlockSpec((1, rows, d), lambda l, j: (l, 0, j)),
        out_shape=jax.ShapeDtypeStruct((depth, rows, n6), F32),
        compiler_params=_cparams("parallel", "parallel"),
        name="ada_mod",
    )(c_all, w_ada, b_ada.reshape(depth, 1, n6))


def _ln_proj_kernel(xp_ref, xs_ref, shp_ref, shs_ref, scp_ref, scs_ref, w_ref, b_ref, wif_ref, bif_ref,
                    z_ref, zif_ref, h_scr, *, np_tiles):
    is_s = pl.program_id(0) >= np_tiles
    first = pl.program_id(1) == 0

    def prologue(x_ref, sh_ref, sc_ref):
        h = (_norm(x_ref[...]) * (1.0 + sc_ref[...]) + sh_ref[...]).astype(BF16)
        h_scr[...] = h
        zif_ref[...] = _dot(h, wif_ref[...]) + bif_ref[...]

    @pl.when(jnp.logical_and(first, is_s))
    def _():
        prologue(xs_ref, shs_ref, scs_ref)

    @pl.when(jnp.logical_and(first, jnp.logical_not(is_s)))
    def _():
        prologue(xp_ref, shp_ref, scp_ref)

    z_ref[...] = (_dot(h_scr[...], w_ref[...]) + b_ref[...]).astype(BF16)


def _ln_proj(xp, xs, mod_p, mod_s, w, b, wif, bif, l, rm, n):
    d = xp.shape[1]
    zw = w.shape[2]
    tm = rm.t
    shp, shs = rm.mod_specs(l, 0, d)
    scp, scs = rm.mod_specs(l, 1, d)
    wmode = _RARELY if zw == TN_IN else None
    return pl.pallas_call(
        functools.partial(_ln_proj_kernel, np_tiles=rm.np_tiles),
        grid=(n // tm, zw // TN_IN),
        in_specs=[*rm.x_specs(xs, d),
                  shp, shs, scp, scs,
                  pl.BlockSpec((None, d, TN_IN), lambda i, j: (l, 0, j), pipeline_mode=wmode),
                  pl.BlockSpec((None, 1, TN_IN), lambda i, j: (l, 0, j), pipeline_mode=wmode),
                  pl.BlockSpec((None, d, LANES), lambda i, j: (l, 0, 0), pipeline_mode=_RARELY),
                  pl.BlockSpec((None, 1, LANES), lambda i, j: (l, 0, 0), pipeline_mode=_RARELY)],
        out_specs=[pl.BlockSpec((tm, TN_IN), lambda i, j: (i, j)),
                   pl.BlockSpec((tm, LANES), lambda i, j: (i, 0))],
        out_shape=[jax.ShapeDtypeStruct((n, zw), BF16), jax.ShapeDtypeStruct((n, LANES), F32)],
        scratch_shapes=[pltpu.VMEM((tm, d), BF16)],
        compiler_params=_cparams("parallel", "arbitrary"),
        name="ln_proj",
    )(xp, xs, mod_p, mod_s, mod_p, mod_s, w, b, wif, bif)


def _conv_tail(yc, g_ref, b_ref):
    y = _norm(yc) * g_ref[...] + b_ref[...]
    return _silu(y).astype(BF16)


def _conv_prompt_kernel(ua_ref, ub_ref, st_ref, cw_ref, cb_ref, g_ref, b_ref, o_ref, ns_ref,
                        ext, shifted, yc, wrep):
    t = pl.program_id(1)
    tc = ua_ref.shape[0]
    sub = 8

    @pl.when(t == 0)
    def _():
        ext[0:CONV_PAD, :] = st_ref[0]

    @pl.when(t > 0)
    def _():
        ext[0:CONV_PAD, :] = ext[tc:tc + CONV_PAD, :]

    ext[CONV_PAD:, :] = ua_ref[...].astype(F32) * _sigmoid(ub_ref[...].astype(F32))
    for s in range(1, sub):
        shifted[s - 1] = ext[s:s + tc + CONV_PAD - sub, :]
    off = CONV_PAD - (CONV_W - 1)
    for w in range(CONV_W):
        wrep[w] = jnp.broadcast_to(cw_ref[w:w + 1, :], (sub, D_CONV))
    groups = CONV_CHUNK // sub

    for r0 in range(0, tc, CONV_CHUNK):
        acc = jnp.broadcast_to(cb_ref[...].reshape(1, 1, D_CONV), (groups, sub, D_CONV))
        for w in range(CONV_W):
            base, s = (off + w) // sub * sub, (off + w) % sub
            src = ext if s == 0 else shifted.at[s - 1]
            win = src[r0 + base:r0 + base + CONV_CHUNK, :]
            acc = acc + win.reshape(groups, sub, D_CONV) * wrep[w][None]
        yc[r0:r0 + CONV_CHUNK, :] = acc.reshape(CONV_CHUNK, D_CONV)
    o_ref[...] = _conv_tail(yc[...], g_ref, b_ref)

    @pl.when(t == pl.num_programs(1) - 1)
    def _():
        ns_ref[0] = ext[tc:tc + CONV_PAD, :]


def _conv_vec_specs(l, nargs):
    return [pl.BlockSpec((None, 1, D_CONV), lambda *_: (l, 0, 0)) for _ in range(nargs)]


def _conv_prompt(z, state_pad, cw, cb, g, b, l, bp, tp, tc):
    nt = tp // tc
    return pl.pallas_call(
        _conv_prompt_kernel,
        grid=(bp, nt),
        in_specs=[pl.BlockSpec((tc, D_CONV), lambda bb, t: (bb * nt + t, Z_UA // D_CONV)),
                  pl.BlockSpec((tc, D_CONV), lambda bb, t: (bb * nt + t, Z_UB // D_CONV)),
                  pl.BlockSpec((1, CONV_PAD, D_CONV), lambda bb, t: (bb, 0, 0)),
                  pl.BlockSpec((None, CONV_PAD, D_CONV), lambda bb, t: (l, 0, 0))] + _conv_vec_specs(l, 3),
        out_specs=[pl.BlockSpec((tc, D_CONV), lambda bb, t: (bb * nt + t, 0)),
                   pl.BlockSpec((1, CONV_PAD, D_CONV), lambda bb, t: (bb, 0, 0))],
        out_shape=[jax.ShapeDtypeStruct((bp * tp, D_CONV), BF16),
                   jax.ShapeDtypeStruct((bp, CONV_PAD, D_CONV), F32)],
        scratch_shapes=[pltpu.VMEM((tc + CONV_PAD, D_CONV), F32),
                        pltpu.VMEM((7, tc + CONV_PAD - 8, D_CONV), F32),
                        pltpu.VMEM((tc, D_CONV), F32),
                        pltpu.VMEM((CONV_PAD, 8, D_CONV), F32)],
        compiler_params=_cparams("parallel", "arbitrary"),
        name="conv_prompt",
    )(z, z, state_pad, cw, cb, g, b)


def _conv_sample_kernel(ua_ref, ub_ref, st_ref, cw_ref, cb_ref, g_ref, b_ref, o_ref, a_ref):
    ts = ua_ref.shape[0]
    ns = CONV_W - 1
    a = ua_ref[...].astype(F32) * _sigmoid(ub_ref[...].astype(F32))
    a_ref[...] = a
    st = st_ref[...]
    row = lax.broadcasted_iota(jnp.int32, (ns, D_CONV), 0)
    for t in range(ts):
        wt = jnp.zeros((ns, D_CONV), F32)
        for j in range(t, ns):
            wt = jnp.where(row == j, cw_ref[j - t:j - t + 1, :], wt)
        yc = jnp.sum(st * wt[None], axis=1) + cb_ref[...]
        for t2 in range(t + 1):
            wi = CONV_W - 1 - (t - t2)
            yc = yc + a[t2] * cw_ref[wi:wi + 1, :]
        o_ref[t] = _conv_tail(yc, g_ref, b_ref)


def _conv_sample(zs3, state, cw, cb, g, b, l, bs_blk):
    ts, bs, _ = zs3.shape
    ns = CONV_W - 1
    return pl.pallas_call(
        _conv_sample_kernel,
        grid=(bs // bs_blk,),
        in_specs=[pl.BlockSpec((ts, bs_blk, D_CONV), lambda i: (0, i, Z_UA // D_CONV)),
                  pl.BlockSpec((ts, bs_blk, D_CONV), lambda i: (0, i, Z_UB // D_CONV)),
                  pl.BlockSpec((None, bs_blk, ns, D_CONV), lambda i: (l, i, 0, 0)),
                  pl.BlockSpec((None, CONV_PAD, D_CONV), lambda i: (l, 0, 0))] + _conv_vec_specs(l, 3),
        out_specs=[pl.BlockSpec((ts, bs_blk, D_CONV), lambda i: (0, i, 0)),
                   pl.BlockSpec((ts, bs_blk, D_CONV), lambda i: (0, i, 0))],
        out_shape=[jax.ShapeDtypeStruct((ts, bs, D_CONV), BF16),
                   jax.ShapeDtypeStruct((ts, bs, D_CONV), F32)],
        compiler_params=_cparams("parallel"),
        name="conv_sample",
    )(zs3, zs3, state, cw, cb, g, b)


def _t5_bucket(dist):
    max_exact = N_BUCKETS // 2
    d = jnp.maximum(dist, 0)
    large = max_exact + (jnp.log(jnp.maximum(d, 1).astype(F32) / max_exact)
                         / math.log(MAX_DIST / max_exact) * (N_BUCKETS - max_exact)).astype(jnp.int32)
    return jnp.where(d < max_exact, d, jnp.minimum(large, N_BUCKETS - 1))


def _bias_heads(rel_bias, dist, valid):
    onehot = (_t5_bucket(dist)[..., None] == jnp.arange(N_BUCKETS)).astype(F32)
    bias = jnp.einsum("qkb,bh->qkh", onehot, rel_bias.astype(F32), precision=lax.Precision.HIGHEST)
    bias = jnp.where(valid[..., None], bias, NEG_INF)
    return jnp.transpose(bias, (2, 0, 1))


def _attn_prompt_kernel(sink_ref, q_ref, kc_ref, kp_ref, vc_ref, vp_ref, bias_ref, o_ref, *, l):
    first = pl.program_id(1) == 0
    w = WINDOW
    nq = q_ref.shape[0] // w
    kall = jnp.concatenate([kp_ref[...], kc_ref[...]], axis=0).astype(F32)
    vall = jnp.concatenate([vp_ref[...], vc_ref[...]], axis=0).astype(F32)
    lane = lax.broadcasted_iota(jnp.int32, kall.shape, 1)
    lo = lane < HEAD_DIM
    kroll = pltpu.roll(kall, HEAD_DIM, 1)
    vroll = pltpu.roll(vall, HEAD_DIM, 1)

    def halves(a, aroll, g):
        if g == 0:
            return jnp.where(lo, a, 0.0).astype(BF16), jnp.where(lo, 0.0, aroll).astype(BF16)
        return jnp.where(lo, aroll, 0.0).astype(BF16), jnp.where(lo, 0.0, a).astype(BF16)

    kh = [halves(kall, kroll, g) for g in range(N_KV)]
    vh = [halves(vall, vroll, g) for g in range(N_KV)]
    col = lax.broadcasted_iota(jnp.int32, (w, 4 * w), 1)
    prev_col = (col % (2 * w)) < w
    tiles_per_g = Q_PER_KV // 2
    units = [(qi, tile) for qi in range(nq) for tile in range(N_HEADS // 2)]
    for u0 in range(0, len(units), ATT_GROUP):
        group = units[u0:u0 + ATT_GROUP]
        scores = []
        for qi, tile in group:
            r0, g = qi * w, tile // tiles_per_g
            q = q_ref[r0:r0 + w, tile * LANES:(tile + 1) * LANES]
            kk = jnp.concatenate([kh[g][0][r0:r0 + 2 * w], kh[g][1][r0:r0 + 2 * w]], axis=0)
            s = _dot_nt(q, kk) * (HEAD_DIM ** -0.5 * LOG2E) + bias_ref[tile]
            if qi == 0:
                s = jnp.where(jnp.logical_and(first, prev_col), NEG_INF, s)
            scores.append(s)
        probs = []
        for (qi, tile), s in zip(group, scores):
            ps = []
            for half in range(2):
                sh = s[:, half * 2 * w:(half + 1) * 2 * w]
                sink = sink_ref[l, 2 * tile + half]
                mx = jnp.maximum(jnp.max(sh, axis=-1, keepdims=True), sink)
                p = jnp.exp2(sh - mx)
                den = jnp.sum(p, axis=-1, keepdims=True) + jnp.exp2(sink - mx)
                ps.append((p * (1.0 / den)).astype(BF16))
            probs.append(jnp.concatenate(ps, axis=1))
        for (qi, tile), p in zip(group, probs):
            r0, g = qi * w, tile // tiles_per_g
            vv = jnp.concatenate([vh[g][0][r0:r0 + 2 * w], vh[g][1][r0:r0 + 2 * w]], axis=0)
            o_ref[r0:r0 + w, tile * LANES:(tile + 1) * LANES] = _dot(p, vv).astype(BF16)


def _attn_prompt(z, bias, sinks, l, bp, tp):
    w = WINDOW
    qb = ATT_QB if tp % (ATT_QB * w) == 0 else 1
    ns = tp // (qb * w)
    nb = tp // w
    kvw = N_KV * HEAD_DIM
    qw = N_HEADS * HEAD_DIM

    def cur(col):
        return lambda bb, i: (bb * ns + i, col)

    def prev(col):
        return lambda bb, i: (bb * nb + jnp.maximum(i * qb - 1, 0), col)

    return pl.pallas_call(
        functools.partial(_attn_prompt_kernel, l=l),
        grid=(bp, ns),
        in_specs=[pl.BlockSpec(memory_space=pltpu.SMEM),
                  pl.BlockSpec((qb * w, qw), cur(Z_Q // qw)),
                  pl.BlockSpec((qb * w, kvw), cur(Z_K // kvw)),
                  pl.BlockSpec((w, kvw), prev(Z_K // kvw)),
                  pl.BlockSpec((qb * w, kvw), cur(Z_V // kvw)),
                  pl.BlockSpec((w, kvw), prev(Z_V // kvw)),
                  pl.BlockSpec((N_HEADS // 2, w, 4 * w), lambda bb, i: (0, 0, 0))],
        out_specs=pl.BlockSpec((qb * w, qw), lambda bb, i: (bb * ns + i, 0)),
        out_shape=jax.ShapeDtypeStruct((bp * tp, qw), BF16),
        compiler_params=_cparams("parallel", "parallel"),
        name="attn_prompt",
    )(sinks, z, z, z, z, z, bias)


def _attn_sample_kernel(q_ref, kc_ref, vc_ref, kn_ref, vn_ref, bias_ref, sink_ref, o_ref):
    wb = kc_ref.shape[1]
    ts = kn_ref.shape[2]
    for g in range(N_KV):
        lo = g * HEAD_DIM
        qb = (q_ref[:, g].astype(F32) * (HEAD_DIM ** -0.5)).astype(BF16)
        kc = kc_ref[:, :, lo:lo + HEAD_DIM].astype(BF16)
        vc = vc_ref[:, :, lo:lo + HEAD_DIM].astype(BF16)
        kn = kn_ref[:, g].astype(F32)
        vn = vn_ref[:, g].astype(F32)
        bias = bias_ref[g]
        s_c = jnp.einsum("bqd,bkd->bqk", qb, kc, preferred_element_type=F32) + bias[None, :, :wb]
        qf = qb.astype(F32)
        s_n = [jnp.sum(qf * kn[:, j:j + 1, :], axis=-1, keepdims=True) + bias[None, :, wb + j:wb + j + 1]
               for j in range(ts)]
        sink = sink_ref[g][None]
        mx = jnp.maximum(jnp.max(s_c, axis=-1, keepdims=True), sink)
        for sj in s_n:
            mx = jnp.maximum(mx, sj)
        p_c = jnp.exp(s_c - mx)
        p_n = [jnp.exp(sj - mx) for sj in s_n]
        den = jnp.sum(p_c, axis=-1, keepdims=True) + jnp.exp(sink - mx)
        for pj in p_n:
            den = den + pj
        o = jnp.einsum("bqk,bkd->bqd", (p_c / den).astype(BF16), vc, preferred_element_type=F32)
        for j in range(ts):
            o = o + (p_n[j] / den).astype(BF16).astype(F32) * vn[:, j:j + 1, :]
        o_ref[:, g] = o


def _attn_sample(q4, kc, vc, kn, vn, bias, sinks, l, bs_blk):
    bs, _, rt, _ = q4.shape
    wb = kc.shape[2]
    ts = kn.shape[2]
    kvw = N_KV * HEAD_DIM
    return pl.pallas_call(
        _attn_sample_kernel,
        grid=(bs // bs_blk,),
        in_specs=[pl.BlockSpec((bs_blk, N_KV, rt, HEAD_DIM), lambda i: (i, 0, 0, 0)),
                  pl.BlockSpec((None, bs_blk, wb, kvw), lambda i: (l, i, 0, 0)),
                  pl.BlockSpec((None, bs_blk, wb, kvw), lambda i: (l, i, 0, 0)),
                  pl.BlockSpec((bs_blk, N_KV, ts, HEAD_DIM), lambda i: (i, 0, 0, 0)),
                  pl.BlockSpec((bs_blk, N_KV, ts, HEAD_DIM), lambda i: (i, 0, 0, 0)),
                  pl.BlockSpec((N_KV, rt, wb + ts), lambda i: (0, 0, 0)),
                  pl.BlockSpec((N_KV, rt, 1), lambda i: (0, 0, 0))],
        out_specs=pl.BlockSpec((bs_blk, N_KV, rt, HEAD_DIM), lambda i: (i, 0, 0, 0)),
        out_shape=jax.ShapeDtypeStruct((bs, N_KV, rt, HEAD_DIM), F32),
        compiler_params=_cparams("parallel"),
        name="attn_sample",
    )(q4, kc, vc, kn, vn, bias, sinks)


def _mlstm_kernel(*refs, t_valid, nseq, aliased, carried, chunk_axis, per_seq_inputs=False, first_layer=None):
    if aliased:
        refs = refs[:10] + refs[11:]
    if per_seq_inputs:
        groups = [refs[6 * s:6 * s + 6] for s in range(nseq)]
        if_ref, ifr_ref, q_ref, k_ref, v_ref, o_ref = (tuple(g[k] for g in groups) for k in range(6))
        refs = (None,) * 6 + refs[6 * nseq:]
    else:
        if_ref, ifr_ref, q_ref, k_ref, v_ref, o_ref = refs[:6]
    g_ref, c0_ref, n0_ref, m0_ref, h_ref, c1_ref, n1_ref, m1_ref = refs[6:14]
    c = pl.program_id(chunk_axis)
    L = (ifr_ref[0] if per_seq_inputs else ifr_ref).shape[-1]
    if carried:
        c_in, n_in, m_in = c_out, n_out, m_out = refs[14:]

        @pl.when(c == 0)
        def _():
            c_in[...] = c0_ref[...]
            n_in[...] = n0_ref[...]
            m_in[...] = m0_ref[...]
    else:
        (c_in, n_in, m_in), (c_out, n_out, m_out) = (c0_ref, n0_ref, m0_ref), (c1_ref, n1_ref, m1_ref)
        if first_layer is not None:
            for dd in range(c1_ref.shape[0]):
                if dd != first_layer:
                    c1_ref[dd] = jnp.zeros(c1_ref.shape[1:], F32)
            c_out = c1_ref.at[first_layer]

    def seq(ref, s):
        return ref[s] if isinstance(ref, tuple) else ref.at[s]

    tt = lax.broadcasted_iota(jnp.int32, (L, L), 0)
    ss = lax.broadcasted_iota(jnp.int32, (L, L), 1)
    causal = ss <= tt
    tril = causal.astype(F32)
    triu = (tt <= ss).astype(F32)
    seqs = range(nseq)
    heads = [(s_i, h) for s_i in seqs for h in range(M_HEADS)]
    mxu_sums = L % LANES == 0
    gates = []
    for s_i in seqs:
        ifc = seq(if_ref, s_i)[...]
        ifr = ifr_ref[s_i][0] if per_seq_inputs else ifr_ref[s_i]
        lf_c = _log_sigmoid(ifc)
        lf_r = _log_sigmoid(ifr)
        i_c, i_r = ifc, ifr
        if t_valid < L:
            rc = lax.broadcasted_iota(jnp.int32, (L, LANES), 0) < t_valid
            rr = lax.broadcasted_iota(jnp.int32, (2 * M_HEADS, L), 1) < t_valid
            lf_c = jnp.where(rc, lf_c, 0.0)
            lf_r = jnp.where(rr, lf_r, 0.0)
            i_c = jnp.where(rc, i_c, NEG_INF)
            i_r = jnp.where(rr, i_r, NEG_INF)
        gates.append((lf_c, lf_r, i_c, i_r))
    f_cs = [_dot_hi(tril, g[0]) for g in gates]
    f_rs = [_dot_hi(g[1], triu) for g in gates]
    state = {(s_i, h): (c_in[s_i, h], n_in[s_i, h], m_in[s_i, h:h + 1, :]) for s_i, h in heads}

    def wide(col):
        return jnp.concatenate([col] * (L // LANES), axis=1) if L >= LANES else col[:, :L]

    st1 = {}
    for s_i, h in heads:
        _, nrow, m0 = state[s_i, h]
        fc = jnp.broadcast_to(f_cs[s_i][:, M_HEADS + h:M_HEADS + h + 1], (L, LANES))
        fr = f_rs[s_i][M_HEADS + h:M_HEADS + h + 1, :]
        ir = gates[s_i][3][h:h + 1, :]
        dm = jnp.where(causal, wide(fc) - fr + ir, NEG_INF)
        m_t = jnp.maximum(m0 + fc, jnp.max(dm, axis=-1, keepdims=True))
        st1[s_i, h] = (fc, dm, m_t, jnp.exp(m0 + fc - m_t))
    st2 = {}
    for s_i, h in heads:
        lo = h * M_DK
        fc, dm, m_t, inter = st1[s_i, h]
        qb = seq(q_ref, s_i)[:, lo:lo + M_DK]
        kf = seq(k_ref, s_i)[:, lo:lo + M_DK].astype(F32) * (M_DK ** -0.5)
        vf = seq(v_ref, s_i)[:, lo:lo + M_DV].astype(F32)
        kb, vb = kf.astype(BF16), vf.astype(BF16)
        if mxu_sums:
            n_rows = jnp.broadcast_to(state[s_i, h][1], (M_DK, M_DK)).astype(BF16)
            qk = _dot_nt(qb, jnp.concatenate([kb, n_rows], axis=0))
            sc, qn_rep = qk[:, :L] * jnp.exp(dm - wide(m_t)), qk[:, L:]
        else:
            sc, qn_rep = _dot_nt(qb, kb) * jnp.exp(dm - wide(m_t)), None
        st2[s_i, h] = (qb, kf, vf, kb, vb, sc, qn_rep)
    st3 = {}
    for s_i, h in heads:
        cm, nrow, m0 = state[s_i, h]
        fc, dm, m_t, inter = st1[s_i, h]
        qb, kf, vf, kb, vb, sc, qn_rep = st2[s_i, h]
        if mxu_sums:
            sv = _dot(sc.astype(BF16), jnp.concatenate([vb, jnp.ones((L, M_DV), BF16)], axis=1))
            num = inter * _dot(qb, cm.astype(BF16)) + sv[:, :M_DV]
            qn = inter * qn_rep + sv[:, M_DV:]
            floor = jnp.exp(-m_t)
        else:
            num = inter * _dot(qb, cm.astype(BF16)) + _dot(sc.astype(BF16), vb)
            qn = (inter * jnp.sum(qb.astype(F32) * nrow, axis=-1, keepdims=True)
                  + jnp.sum(sc, axis=-1, keepdims=True))
            floor = jnp.exp(-m_t)
        st3[s_i, h] = num / jnp.maximum(jnp.abs(qn), floor)
    new_state = {}
    for s_i, h in heads:
        cm, nrow, m0 = state[s_i, h]
        fc, dm, m_t, inter = st1[s_i, h]
        qb, kf, vf, kb, vb, sc, _ = st2[s_i, h]
        ic = jnp.broadcast_to(gates[s_i][2][:, h:h + 1], (L, LANES))
        m_end = m_t[L - 1:L, :]
        f_end = fc[L - 1:L, :]
        decay = jnp.exp(m0 + f_end - m_end)
        w_s = jnp.exp(f_end - fc + ic - m_end)
        new_state[s_i, h] = (decay * cm + _dot_tn(kb, (w_s * vf).astype(BF16)),
                             decay * nrow + jnp.sum(w_s * kf, axis=0, keepdims=True),
                             m_end)
    for s_i, h in heads:
        lo = h * M_DK
        hn = _norm(st3[s_i, h]) * g_ref[:, lo:lo + M_DV]
        gate = _sigmoid(seq(o_ref, s_i)[:, lo:lo + M_DV].astype(F32))
        seq(h_ref, s_i)[:, lo:lo + M_DV] = (gate * hn).astype(BF16)
    for s_i, h in heads:
        c_out[s_i, h], n_out[s_i, h], m_out[s_i, h:h + 1, :] = new_state[s_i, h]
    if not carried:
        for s_i in seqs:
            m_out[s_i, M_HEADS:, :] = jnp.zeros((M_HEADS, LANES), F32)

    if carried:
        @pl.when(c == pl.num_programs(chunk_axis) - 1)
        def _():
            c1_ref[...] = c_out[...]
            n1_ref[...] = n_out[...]
            m1_ref[...] = m_out[...]


def _mlstm_state_specs(nseq, l_state):
    if l_state is None:
        c_spec = pl.BlockSpec((nseq, M_HEADS, M_DK, M_DV), lambda b, c: (b, 0, 0, 0))
    else:
        c_spec = pl.BlockSpec((None, nseq, M_HEADS, M_DK, M_DV), lambda b, c: (l_state, b, 0, 0, 0))
    return (c_spec,
            pl.BlockSpec((nseq, M_HEADS, 1, M_DK), lambda b, c: (b, 0, 0, 0)),
            pl.BlockSpec((nseq, 2 * M_HEADS, LANES), lambda b, c: (b, 0, 0)))


def _mlstm_scratch(nseq):
    return [pltpu.VMEM((nseq, M_HEADS, M_DK, M_DV), F32),
            pltpu.VMEM((nseq, M_HEADS, 1, M_DK), F32),
            pltpu.VMEM((nseq, 2 * M_HEADS, LANES), F32)]


def _mlstm_prompt(z, zif, ifr, gamma, l, bp, nc, L):
    hw = M_HEADS * M_DK
    zero = lambda *s: jnp.zeros(s, F32)
    in_specs, args = [], []
    for b in range(bp):
        def rows(col, b=b):
            return lambda c: (b * nc + c, col)

        in_specs += [pl.BlockSpec((L, LANES), rows(0)),
                     pl.BlockSpec((1, 2 * M_HEADS, L), lambda c, b=b: (b * nc + c, 0, 0)),
                     pl.BlockSpec((L, hw), rows(Z_MQ // hw)),
                     pl.BlockSpec((L, hw), rows(Z_MK // hw)),
                     pl.BlockSpec((L, hw), rows(Z_MV // hw)),
                     pl.BlockSpec((L, hw), rows(Z_MO // hw))]
        args += [zif, ifr, z, z, z, z]
    state_specs = [pl.BlockSpec((bp, M_HEADS, M_DK, M_DV), lambda c: (0, 0, 0, 0)),
                   pl.BlockSpec((bp, M_HEADS, 1, M_DK), lambda c: (0, 0, 0, 0)),
                   pl.BlockSpec((bp, 2 * M_HEADS, LANES), lambda c: (0, 0, 0))]
    kern = functools.partial(_mlstm_kernel, t_valid=L, nseq=bp, aliased=False, carried=True, chunk_axis=0,
                             per_seq_inputs=True)
    outs = pl.pallas_call(
        kern,
        grid=(nc,),
        in_specs=in_specs + [pl.BlockSpec((None, 1, hw), lambda c: (l, 0, 0))] + state_specs,
        out_specs=[pl.BlockSpec((bp, L, hw), lambda c: (0, c, 0))] + state_specs,
        out_shape=[jax.ShapeDtypeStruct((bp, nc * L, hw), BF16),
                   jax.ShapeDtypeStruct((bp, M_HEADS, M_DK, M_DV), F32),
                   jax.ShapeDtypeStruct((bp, M_HEADS, 1, M_DK), F32),
                   jax.ShapeDtypeStruct((bp, 2 * M_HEADS, LANES), F32)],
        scratch_shapes=_mlstm_scratch(bp),
        compiler_params=_cparams("arbitrary"),
        name="mlstm_prompt",
    )(*args, gamma, zero(bp, M_HEADS, M_DK, M_DV), zero(bp, M_HEADS, 1, M_DK), zero(bp, 2 * M_HEADS, LANES))
    return (outs[0].reshape(bp * nc * L, hw),) + tuple(outs[1:])


def _mlstm_sample(zm3, if3, ifr, gamma, c_all, n0, m0x, c_out_prev, l, t_valid, nseq):
    bs, L, _ = zm3.shape
    depth = c_all.shape[0]
    hw = M_HEADS * M_DK
    aliased = c_out_prev is not None

    def blk(col):
        return pl.BlockSpec((nseq, L, hw), lambda b, c: (b, 0, col))

    c_in, n_spec, m_spec = _mlstm_state_specs(nseq, l)
    in_specs = [pl.BlockSpec((nseq, L, LANES), lambda b, c: (b, 0, 0)),
                pl.BlockSpec((nseq, 2 * M_HEADS, L), lambda b, c: (b, 0, 0)),
                blk(0), blk(1), blk(2), blk(3),
                pl.BlockSpec((None, 1, hw), lambda b, c: (l, 0, 0)),
                c_in, n_spec, m_spec]
    args = [if3, ifr, zm3, zm3, zm3, zm3, gamma, c_all, n0, m0x]
    aliases = {}
    if aliased:
        in_specs.append(pl.BlockSpec(memory_space=pl.ANY))
        args.append(c_out_prev)
        aliases = {len(args) - 1: 1}
        c_out = c_in
    else:
        c_out = pl.BlockSpec((depth, nseq, M_HEADS, M_DK, M_DV), lambda b, c: (0, b, 0, 0, 0))
    kern = functools.partial(_mlstm_kernel, t_valid=t_valid, nseq=nseq, aliased=aliased, carried=False,
                             chunk_axis=1, first_layer=None if aliased else l)
    return pl.pallas_call(
        kern,
        grid=(bs // nseq, 1),
        in_specs=in_specs,
        out_specs=[pl.BlockSpec((nseq, L, hw), lambda b, c: (b, 0, 0)), c_out, n_spec, m_spec],
        out_shape=[jax.ShapeDtypeStruct((bs, L, hw), BF16),
                   jax.ShapeDtypeStruct((depth, bs, M_HEADS, M_DK, M_DV), F32),
                   jax.ShapeDtypeStruct((bs, M_HEADS, 1, M_DK), F32),
                   jax.ShapeDtypeStruct((bs, 2 * M_HEADS, LANES), F32)],
        input_output_aliases=aliases,
        compiler_params=_cparams("parallel", "arbitrary"),
        name="mlstm_sample",
    )(*args)


def _m_state_in(m):
    lead = m.shape[:-1]
    mx = jnp.zeros(lead + (2 * M_HEADS, LANES), F32)
    return mx.at[..., :M_HEADS, :].set(jnp.broadcast_to(m[..., None], lead + (M_HEADS, LANES)))


def _merge_kernel(*refs, np_tiles, alpha, route):
    (cp_ref, cs_ref, ap_ref, as_ref, mp_ref, ms_ref, g0_ref, g1_ref, g2_ref, xp_ref, xs_ref,
     gtp_ref, gts_ref, shp_ref, shs_ref, scp_ref, scs_ref,
     wc_ref, wa_ref, wm_ref, wo_ref, pg_ref, pb_ref) = refs[:23]
    if route:
        rw_ref, rb_ref, x1_ref, h_ref, route_ref = refs[23:]
    else:
        x1_ref, h_ref = refs[23:]
    is_s = pl.program_id(0) >= np_tiles

    def gate(ref):
        return _sigmoid(ref[...].astype(F32))

    def run(c_ref, a_ref, m_ref, x_ref, gt_ref, sh_ref, sc_ref):
        y = (gate(g0_ref) * _dot(c_ref[...], wc_ref[...])
             + gate(g1_ref) * _dot(a_ref[...], wa_ref[...])
             + gate(g2_ref) * _dot(m_ref[...], wm_ref[...]))
        mix = _dot(y.astype(BF16), wo_ref[...])
        x1 = _norm(alpha * x_ref[...] + gt_ref[...] * mix) * pg_ref[...] + pb_ref[...]
        x1_ref[...] = x1
        h = _norm(x1) * (1.0 + sc_ref[...]) + sh_ref[...]
        h_ref[...] = h.astype(h_ref.dtype)
        if route:
            h_hi, h_lo = _split_bf16(h)
            w_hi, w_lo = _split_bf16(rw_ref[...])
            logits = _dot(h_hi, w_hi) + (_dot(h_hi, w_lo) + _dot(h_lo, w_hi)) + rb_ref[...]
            lane = lax.broadcasted_iota(jnp.int32, logits.shape, 1)
            m1 = jnp.max(logits, axis=-1, keepdims=True)
            e1 = jnp.min(jnp.where(logits == m1, lane, LANES), axis=-1, keepdims=True)
            l2 = jnp.where(lane == e1, NEG_INF, logits)
            m2 = jnp.max(l2, axis=-1, keepdims=True)
            e2 = jnp.min(jnp.where(l2 == m2, lane, LANES), axis=-1, keepdims=True)
            ex = jnp.exp(m2 - m1)
            w1 = 1.0 / (1.0 + ex)
            w2 = ex / (1.0 + ex)
            out = jnp.where(lane == 0, e1.astype(F32),
                            jnp.where(lane == 1, e2.astype(F32),
                                      jnp.where(lane == 2, w1, jnp.where(lane == 3, w2, 0.0))))
            route_ref[...] = out

    @pl.when(is_s)
    def _():
        run(cs_ref, as_ref, ms_ref, xs_ref, gts_ref, shs_ref, scs_ref)

    @pl.when(jnp.logical_not(is_s))
    def _():
        run(cp_ref, ap_ref, mp_ref, xp_ref, gtp_ref, shp_ref, scp_ref)


def _merge(cp, cs, ap, as_, mp, ms, z, xp, xs, mod_p, mod_s, wc, wa, wm, wo, pg, pb, l, rm, alpha, router):
    n, d = z.shape[0], xp.shape[1]
    hw = D_CONV
    route = router is not None
    tm = rm.t

    def pblk():
        return pl.BlockSpec((tm, hw), lambda i: (rm.prompt(i), 0))

    def sblk():
        return pl.BlockSpec((tm, hw), lambda i: (rm.sample(i), 0), pipeline_mode=_RARELY)

    def zg(k):
        return pl.BlockSpec((tm, d), lambda i: (i, Z_G // d + k))

    def lw(a):
        return pl.BlockSpec((None,) + a.shape[1:], lambda i: (l,) + (0,) * (a.ndim - 1), pipeline_mode=_RARELY)

    post = pl.BlockSpec((None, None, 1, d), lambda i: (l, 0, 0, 0))
    in_specs = [pblk(), sblk(), pblk(), sblk(), pblk(), sblk(), zg(0), zg(1), zg(2),
                *rm.x_specs(xs, d),
                *rm.mod_specs(l, 2, d), *rm.mod_specs(l, 3, d), *rm.mod_specs(l, 4, d),
                lw(wc), lw(wa), lw(wm), lw(wo), post, post]
    args = [cp, cs, ap, as_, mp, ms, z, z, z, xp, xs, mod_p, mod_s, mod_p, mod_s, mod_p, mod_s,
            wc, wa, wm, wo, pg, pb]
    out_specs = [pl.BlockSpec((tm, d), lambda i: (i, 0)), pl.BlockSpec((tm, d), lambda i: (i, 0))]
    out_shape = [jax.ShapeDtypeStruct((n, d), F32), jax.ShapeDtypeStruct((n, d), F32 if route else BF16)]
    if route:
        rw, rb, lj = router
        in_specs += [pl.BlockSpec((None, d, LANES), lambda i: (lj, 0, 0)),
                     pl.BlockSpec((None, 1, LANES), lambda i: (lj, 0, 0))]
        args += [rw, rb]
        out_specs.append(pl.BlockSpec((tm, LANES), lambda i: (i, 0)))
        out_shape.append(jax.ShapeDtypeStruct((n, LANES), F32))
    kern = functools.partial(_merge_kernel, np_tiles=rm.np_tiles, alpha=alpha, route=route)
    return pl.pallas_call(
        kern, grid=(n // tm,), in_specs=in_specs, out_specs=out_specs, out_shape=out_shape,
        compiler_params=_cparams("parallel"), name="merge_route" if route else "merge",
    )(*args)


def _post_residual(x_ref, gp_ref, gs_ref, f, pg_ref, pb_ref, is_s, alpha):
    return _norm(alpha * x_ref[...] + _pick(is_s, gp_ref, gs_ref) * f) * pg_ref[...] + pb_ref[...]


def _ffn_kernel(h_ref, wg_ref, wu_ref, wd_ref, x_ref, gp_ref, gs_ref, pg_ref, pb_ref, o_ref, *, alpha, np_tiles):
    h = h_ref[...]
    a = (_silu(_dot(h, wg_ref[...])) * _dot(h, wu_ref[...])).astype(BF16)
    is_s = pl.program_id(0) >= np_tiles
    o_ref[...] = _post_residual(x_ref, gp_ref, gs_ref, _dot(a, wd_ref[...]), pg_ref, pb_ref, is_s, alpha)


def _ffn_dense(h, wg, wu, wd, x, mod_p, mod_s, pg, pb, l, lj, rm, alpha):
    n, d = x.shape
    f = wg.shape[2]
    tm = rm.t
    post = pl.BlockSpec((None, None, 1, d), lambda i: (l, 1, 0, 0))
    return pl.pallas_call(
        functools.partial(_ffn_kernel, alpha=alpha, np_tiles=rm.np_tiles),
        grid=(n // tm,),
        in_specs=[pl.BlockSpec((tm, d), lambda i: (i, 0)),
                  pl.BlockSpec((None, d, f), lambda i: (lj, 0, 0), pipeline_mode=_RARELY),
                  pl.BlockSpec((None, d, f), lambda i: (lj, 0, 0), pipeline_mode=_RARELY),
                  pl.BlockSpec((None, f, d), lambda i: (lj, 0, 0), pipeline_mode=_RARELY),
                  pl.BlockSpec((tm, d), lambda i: (i, 0)),
                  *rm.mod_specs(l, 5, d), post, post],
        out_specs=pl.BlockSpec((tm, d), lambda i: (i, 0)),
        out_shape=jax.ShapeDtypeStruct((n, d), F32),
        compiler_params=_cparams("parallel"),
        name="ffn_dense",
    )(h, wg, wu, wd, x, mod_p, mod_s, pg, pb)


def _rank_kernel(route_ref, rank_ref, tot_ref, carry):
    i = pl.program_id(0)
    tm = route_ref.shape[0]

    @pl.when(i == 0)
    def _():
        carry[...] = jnp.zeros_like(carry)

    r = route_ref[...]
    lane = lax.broadcasted_iota(jnp.int32, (tm, LANES), 1)
    e1 = r[:, 0:1].astype(jnp.int32)
    e2 = r[:, 1:2].astype(jnp.int32)
    hit1 = lane == e1
    hit2 = lane == e2
    onehot = jnp.where(jnp.logical_or(hit1, hit2), 1.0, 0.0)
    tt = lax.broadcasted_iota(jnp.int32, (tm, tm), 0)
    ss = lax.broadcasted_iota(jnp.int32, (tm, tm), 1)
    before = jnp.where(ss < tt, 1.0, 0.0).astype(BF16)
    cnt = _dot(before, onehot.astype(BF16)) + carry[0:1, :]
    r1 = jnp.sum(jnp.where(hit1, cnt, 0.0), axis=-1, keepdims=True)
    r2 = jnp.sum(jnp.where(hit2, cnt, 0.0), axis=-1, keepdims=True)
    rank_ref[...] = jnp.where(lane == 0, r1, jnp.where(lane == 1, r2, 0.0))
    carry[...] = carry[...] + jnp.sum(onehot, axis=0, keepdims=True)
    tot_ref[...] = carry[...]


def _moe_rank(route, tm):
    n = route.shape[0]
    return pl.pallas_call(
        _rank_kernel,
        grid=(n // tm,),
        in_specs=[pl.BlockSpec((tm, LANES), lambda i: (i, 0))],
        out_specs=[pl.BlockSpec((tm, LANES), lambda i: (i, 0)),
                   pl.BlockSpec((8, LANES), lambda i: (0, 0))],
        out_shape=[jax.ShapeDtypeStruct((n, LANES), F32), jax.ShapeDtypeStruct((8, LANES), F32)],
        scratch_shapes=[pltpu.VMEM((8, LANES), F32)],
        compiler_params=_cparams("arbitrary"),
        name="moe_rank",
    )(route)


def _row_copy(src, s, dst, t, sem):
    return pltpu.make_async_copy(src.at[pl.ds(s, 1)], dst.at[pl.ds(t, 1)], sem)


def _dispatch_kernel(d1_ref, d2_ref, zl_ref, h_ref, xs_hbm, zbuf, stage, sem, zsem):
    i = pl.program_id(0)
    tm = h_ref.shape[0]

    @pl.when(i == 0)
    def _():
        zbuf[...] = jnp.zeros_like(zbuf)

        def zero_copy(k):
            row = pl.multiple_of(jnp.maximum(zl_ref[k], 0), MOE_SUB)
            return pltpu.make_async_copy(zbuf, xs_hbm.at[pl.ds(row, MOE_SUB)], zsem)

        def start(k, carry):
            @pl.when(zl_ref[k] >= 0)
            def _():
                zero_copy(k).start()
            return carry

        def wait(k, carry):
            @pl.when(zl_ref[k] >= 0)
            def _():
                zero_copy(k).wait()
            return carry

        lax.fori_loop(0, zl_ref.shape[0], start, 0)
        lax.fori_loop(0, zl_ref.shape[0], wait, 0)

    slot = i % 2
    stage[slot] = h_ref[...]

    def issue(r, carry):
        _row_copy(stage.at[slot], r, xs_hbm, d1_ref[i * tm + r], sem.at[slot]).start(priority=0)
        _row_copy(stage.at[slot], r, xs_hbm, d2_ref[i * tm + r], sem.at[slot]).start(priority=1)
        return carry

    lax.fori_loop(0, tm, issue, 0, unroll=8)

    def drain(s):
        for _ in range(2):
            pltpu.make_async_copy(stage.at[s], xs_hbm.at[pl.ds(0, tm)], sem.at[s]).wait()

    @pl.when(i > 0)
    def _():
        drain(1 - slot)

    @pl.when(i == pl.num_programs(0) - 1)
    def _():
        drain(slot)


def _moe_dispatch(dest1, dest2, zlist, h, n_rows, tm):
    n, d = h.shape
    grid_spec = pltpu.PrefetchScalarGridSpec(
        num_scalar_prefetch=3,
        grid=(n // tm,),
        in_specs=[pl.BlockSpec((tm, d), lambda i, *_: (i, 0))],
        out_specs=pl.BlockSpec(memory_space=pl.ANY),
        scratch_shapes=[pltpu.VMEM((MOE_SUB, d), F32), pltpu.VMEM((2, tm, d), F32),
                        pltpu.SemaphoreType.DMA((2,)), pltpu.SemaphoreType.DMA(())],
    )
    return pl.pallas_call(
        _dispatch_kernel, grid_spec=grid_spec,
        out_shape=jax.ShapeDtypeStruct((n_rows, d), F32),
        compiler_params=_cparams("arbitrary"),
        name="moe_dispatch",
    )(dest1, dest2, zlist, h)


def _expert_kernel(blk_e_ref, nvalid_ref, xs_ref, wg_ref, wu_ref, wd_ref, y_ref, xb, acc):
    b = pl.program_id(0)
    j = pl.program_id(1)
    tb = xb.shape[0]
    nsub = tb // MOE_SUB
    nv = nvalid_ref[b]
    used = (nv + MOE_SUB - 1) // MOE_SUB

    def run(rows):
        wg = wg_ref[...].astype(BF16)
        wu = wu_ref[...].astype(BF16)
        wd = wd_ref[...].astype(BF16)

        @pl.when(j == 0)
        def _():
            xb[rows, :] = xs_ref[rows, :].astype(BF16)

        x = xb[rows, :]
        a = (_silu(_dot(x, wg)) * _dot(x, wu)).astype(BF16)
        part = _dot(a, wd)

        @pl.when(j == 0)
        def _():
            acc[rows, :] = part

        @pl.when(j > 0)
        def _():
            acc[rows, :] = acc[rows, :] + part

        @pl.when(j == pl.num_programs(1) - 1)
        def _():
            y_ref[rows, :] = acc[rows, :]

    for m in range(1, nsub + 1):
        @pl.when(used == m)
        def _():
            run(pl.ds(0, m * MOE_SUB))

    for s in range(nsub):
        @pl.when(jnp.logical_and(s >= used, j == pl.num_programs(1) - 1))
        def _():
            y_ref[pl.ds(s * MOE_SUB, MOE_SUB), :] = jnp.zeros((MOE_SUB, y_ref.shape[1]), F32)


def _moe_experts(blk_e, nvalid, xs, wg, wu, wd, lj, tb, tf):
    n_rows, d = xs.shape
    f = wg.shape[3]
    nj = f // tf

    def jm(b, j, nv):
        return jnp.where(nv[b] > 0, j, nj - 1)

    grid_spec = pltpu.PrefetchScalarGridSpec(
        num_scalar_prefetch=2,
        grid=(n_rows // tb, nj),
        in_specs=[pl.BlockSpec((tb, d), lambda b, j, be, nv: (b, 0)),
                  pl.BlockSpec((None, None, d, tf), lambda b, j, be, nv: (lj, be[b], 0, jm(b, j, nv))),
                  pl.BlockSpec((None, None, d, tf), lambda b, j, be, nv: (lj, be[b], 0, jm(b, j, nv))),
                  pl.BlockSpec((None, None, tf, d), lambda b, j, be, nv: (lj, be[b], jm(b, j, nv), 0))],
        out_specs=pl.BlockSpec((tb, d), lambda b, j, be, nv: (b, 0)),
        scratch_shapes=[pltpu.VMEM((tb, d), BF16), pltpu.VMEM((tb, d), F32)],
    )
    return pl.pallas_call(
        _expert_kernel, grid_spec=grid_spec,
        out_shape=jax.ShapeDtypeStruct((n_rows, d), F32),
        compiler_params=_cparams("parallel", "arbitrary"),
        name="moe_experts",
    )(blk_e, nvalid, xs, wg, wu, wd)


def _combine_kernel(d1_ref, d2_ref, ys_hbm, route_ref, x_ref, gp_ref, gs_ref, pg_ref, pb_ref, *rest,
                    alpha, np_tiles, split):
    (*outs, y1, y2, sem1, sem2) = rest
    i = pl.program_id(0)
    tm = x_ref.shape[0]
    slot = i % 2

    last = pl.num_programs(0) - 1

    def issue(tile, s, r):
        _row_copy(ys_hbm, d1_ref[tile * tm + r], y1.at[s], r, sem1.at[s]).start(priority=0)
        _row_copy(ys_hbm, d2_ref[tile * tm + r], y2.at[s], r, sem2.at[s]).start(priority=1)

    def drain(s):
        pltpu.make_async_copy(ys_hbm.at[pl.ds(0, tm)], y1.at[s], sem1.at[s]).wait()
        pltpu.make_async_copy(ys_hbm.at[pl.ds(0, tm)], y2.at[s], sem2.at[s]).wait()

    @pl.when(i == 0)
    def _():
        lax.fori_loop(0, tm, lambda r, c: (issue(0, 0, r), c)[1], 0, unroll=8)

    drain(slot)
    nxt = jnp.minimum(i + 1, last)
    for r in range(tm):
        issue(nxt, 1 - slot, r)
    r = route_ref[...]
    f = y1[slot] * r[:, 2:3] + y2[slot] * r[:, 3:4]
    out = _post_residual(x_ref, gp_ref, gs_ref, f, pg_ref, pb_ref, i >= np_tiles, alpha)
    if split:
        op_ref, os_ref = outs

        @pl.when(i < np_tiles)
        def _():
            op_ref[...] = out

        @pl.when(i >= np_tiles)
        def _():
            os_ref[...] = out
    else:
        outs[0][...] = out

    @pl.when(i == last)
    def _():
        drain(1 - slot)


def _moe_combine(dest1, dest2, ys, route, x, mod_p, mod_s, pg, pb, l, rm, alpha, split):
    n, d = x.shape
    tm = rm.t
    if split:
        n_p = rm.np_tiles * tm
        out_specs = [pl.BlockSpec((tm, d), lambda i, *_: (rm.prompt(i), 0)),
                     pl.BlockSpec((tm, d), lambda i, *_: (rm.sample(i), 0))]
        out_shape = [jax.ShapeDtypeStruct((n_p, d), F32), jax.ShapeDtypeStruct((n - n_p, d), F32)]
    else:
        out_specs = pl.BlockSpec((tm, d), lambda i, *_: (i, 0))
        out_shape = jax.ShapeDtypeStruct((n, d), F32)
    post = pl.BlockSpec((None, None, 1, d), lambda i, *_: (l, 1, 0, 0))
    grid_spec = pltpu.PrefetchScalarGridSpec(
        num_scalar_prefetch=2,
        grid=(n // tm,),
        in_specs=[pl.BlockSpec(memory_space=pl.ANY),
                  pl.BlockSpec((tm, LANES), lambda i, *_: (i, 0)),
                  pl.BlockSpec((tm, d), lambda i, *_: (i, 0)),
                  *rm.mod_specs(l, 5, d), post, post],
        out_specs=out_specs,
        scratch_shapes=[pltpu.VMEM((2, tm, d), F32), pltpu.VMEM((2, tm, d), F32),
                        pltpu.SemaphoreType.DMA((2,)), pltpu.SemaphoreType.DMA((2,))],
    )
    return pl.pallas_call(
        functools.partial(_combine_kernel, alpha=alpha, np_tiles=rm.np_tiles, split=split), grid_spec=grid_spec,
        out_shape=out_shape,
        compiler_params=_cparams("arbitrary"),
        name="moe_combine",
    )(dest1, dest2, ys, route, x, mod_p, mod_s, pg, pb)


def _moe_ffn(h, route, wg, wu, wd, x, mod_p, mod_s, pg, pb, l, lj, rm, tb, tf, alpha, split):
    n = h.shape[0]
    rank, tot = _moe_rank(route, rm.t)
    counts = tot[0, :N_EXPERTS].astype(jnp.int32)
    padded = (counts + tb - 1) // tb * tb
    pad_end = jnp.cumsum(padded)
    pad_start = pad_end - padded
    experts = jnp.arange(N_EXPERTS, dtype=jnp.int32)

    def slot(col):
        e = route[:, col].astype(jnp.int32)
        start = jnp.sum(jnp.where(e[:, None] == experts[None, :], pad_start[None, :], 0), axis=1)
        return start + rank[:, col].astype(jnp.int32)

    dest1, dest2 = slot(0), slot(1)
    n_blocks = -(-(2 * n + N_EXPERTS * (tb - 1)) // tb)
    blk_start = jnp.arange(n_blocks, dtype=jnp.int32) * tb
    blk_e = jnp.minimum(jnp.sum(pad_end[None, :] <= blk_start[:, None], axis=1), N_EXPERTS - 1).astype(jnp.int32)
    own_end = jnp.sum(jnp.where(blk_e[:, None] == experts[None, :], (pad_start + counts)[None, :], 0), axis=1)
    nvalid = jnp.clip(own_end - blk_start, 0, tb).astype(jnp.int32)
    nvalid = jnp.where(blk_start < pad_end[-1], nvalid, 0)
    per_blk = tb // MOE_SUB
    sub_start = jnp.arange(n_blocks * per_blk, dtype=jnp.int32) * MOE_SUB
    sub_room = jnp.repeat(blk_start + nvalid, per_blk) - sub_start
    zlist = jnp.where(sub_room < MOE_SUB, sub_start, -1).astype(jnp.int32)
    xs = _moe_dispatch(dest1, dest2, zlist, h, n_blocks * tb, rm.t)
    ys = _moe_experts(blk_e, nvalid, xs, wg, wu, wd, lj, tb, tf)
    return _moe_combine(dest1, dest2, ys, route, x, mod_p, mod_s, pg, pb, l, rm, alpha, split)


def _pack_w_in(w_in, b_in):
    depth, d, _ = w_in.shape
    q_end = 2 * D_CONV + N_HEADS * HEAD_DIM
    k_end = q_end + N_KV * HEAD_DIM
    a_end = k_end + N_KV * HEAD_DIM
    m_end = a_end + 4 * M_HEADS * M_DK
    if_end = m_end + 2 * M_HEADS

    def pack(a):
        return jnp.concatenate([a[..., if_end:], a[..., :q_end], a[..., a_end:m_end], a[..., q_end:a_end]], axis=-1)

    def gates(a):
        return jnp.pad(a[..., m_end:if_end], [(0, 0)] * (a.ndim - 1) + [(0, LANES - 2 * M_HEADS)])

    return (pack(w_in).astype(BF16), pack(b_in).reshape(depth, 1, Z_W),
            gates(w_in).astype(BF16), gates(b_in).reshape(depth, 1, LANES))


def kernel(x_prompt, x_sample, cache_swa_k, cache_swa_v, state_conv, state_mlstm_C, state_mlstm_n, state_mlstm_m, c_prompt, c_sample, w_ada, b_ada, w_in, b_in, conv_w, conv_b, conv_ln_g, conv_ln_b, w_conv_out, attn_sinks, rel_bias, w_attn_out, m_norm_g, w_m_out, w_out, post_ln_g, post_ln_b, ffn_w_gate, ffn_w_up, ffn_w_down, router_w, router_b, moe_w_gate, moe_w_up, moe_w_down):
    bp, tp, d = x_prompt.shape
    bs, ts, _ = x_sample.shape
    depth = w_ada.shape[0]
    alpha = (2 * depth) ** 0.25
    n_p, n_s = bp * tp, bs * ts
    tm = n_s
    assert d == D_MODEL and tp % tm == 0 and tp % WINDOW == 0 and tm % 32 == 0
    rm = _RowMap(tm, bp, tp, n_p, n_s)
    rm_half = _RowMap(tm // 2, bp, tp, n_p, n_s)
    wb = cache_swa_k.shape[2]
    big = n_p >= 4096
    tc = 512 if big else tm
    lm = 256 if big else min(tp, 128)
    tb = 1024 if big else 2 * MOE_SUB
    bs_blk = 32 if bs % 32 == 0 else bs
    bs_att = 16 if bs % 16 == 0 else bs
    bs_m = 4 if bs % 4 == 0 else 1
    lts = 16

    n = n_p + n_s
    x = (x_prompt.reshape(n_p, d), jnp.transpose(x_sample, (1, 0, 2)).reshape(n_s, d))

    nc_rows = -(-(bp + bs) // 8) * 8
    c_all = jnp.zeros((nc_rows, d), F32).at[:bp].set(c_prompt).at[bp:bp + bs].set(c_sample)
    mod = _ada_mod(c_all, w_ada, b_ada)
    mod_p = mod[:, :bp].reshape(depth, bp, 1, 6 * d)
    mod_s = jnp.tile(mod[:, bp:bp + bs], (1, ts, 1))

    w_in_p, b_in_p, w_if, b_if = _pack_w_in(w_in, b_in)
    wc_b, wa_b, wm_b, wo_b = (w.astype(BF16) for w in (w_conv_out, w_attn_out, w_m_out, w_out))
    fg_b, fu_b, fd_b = (w.astype(BF16) for w in (ffn_w_gate, ffn_w_up, ffn_w_down))
    cw_pad = jnp.pad(conv_w, ((0, 0), (0, CONV_PAD - CONV_W), (0, 0)))
    cvecs = [v.reshape(depth, 1, D_CONV) for v in (conv_b, conv_ln_g, conv_ln_b)]
    rw_pad = jnp.pad(router_w, ((0, 0), (0, 0), (0, LANES - N_EXPERTS)))
    rb_pad = jnp.pad(router_b, ((0, 0), (0, LANES - N_EXPERTS)), constant_values=NEG_INF)[:, None, :]
    pg = post_ln_g.reshape(depth, 2, 1, d)
    pb = post_ln_b.reshape(depth, 2, 1, d)
    gamma = m_norm_g.reshape(depth, 1, M_HEADS * M_DV)
    sinks = attn_sinks.astype(F32)

    qi = jnp.arange(WINDOW)[:, None]
    kj = jnp.arange(2 * WINDOW)[None, :]
    dist_p = qi + WINDOW - kj
    bh = _bias_heads(rel_bias, dist_p, (dist_p >= 0) & (dist_p < WINDOW))
    bias_p = bh.reshape(N_HEADS // 2, 2, WINDOW, 2 * WINDOW).transpose(0, 2, 1, 3).reshape(
        N_HEADS // 2, WINDOW, 4 * WINDOW)
    dist_s = jnp.arange(ts)[:, None] + wb - jnp.arange(wb + ts)[None, :]
    bias_s = _bias_heads(rel_bias, dist_s, (dist_s >= 0) & (dist_s < WINDOW)).reshape(
        N_KV, Q_PER_KV * ts, wb + ts)
    kc_all = cache_swa_k.reshape(depth, bs, wb, N_KV * HEAD_DIM)
    vc_all = cache_swa_v.reshape(depth, bs, wb, N_KV * HEAD_DIM)

    f_moe = moe_w_gate.shape[3]
    tf_moe = 512 if f_moe % 512 == 0 else f_moe

    n0_all = state_mlstm_n[:, :, :, None, :]
    m0_all = _m_state_in(state_mlstm_m)
    s_c = None
    new_p = [[] for _ in range(6)]
    new_s = [[] for _ in range(6)]
    for l in range(depth):
        j = l // 2
        xp, xs = x if isinstance(x, tuple) else (x, x)
        z, zif = _ln_proj(xp, xs, mod_p, mod_s, w_in_p, b_in_p, w_if, b_if, l, rm, n)
        zs3 = z[n_p:].reshape(ts, bs, Z_W)

        cp, ns_p = _conv_prompt(z, jnp.zeros((bp, CONV_PAD, D_CONV), F32), cw_pad, *cvecs, l, bp, tp, tc)
        cs3, a_s3 = _conv_sample(zs3, state_conv, cw_pad, *cvecs, l, bs_blk)
        new_p[2].append(ns_p[:, CONV_PAD - CONV_W + 1:])
        new_s[2].append(jnp.transpose(a_s3, (1, 0, 2)))

        sink_h = sinks[l].reshape(N_KV, Q_PER_KV, 1)
        sink_s = jnp.broadcast_to(sink_h, (N_KV, Q_PER_KV, ts)).reshape(N_KV, Q_PER_KV * ts, 1)
        ap = _attn_prompt(z, bias_p * LOG2E, sinks * LOG2E, l, bp, tp)
        nk = min(WINDOW, tp)
        kv_tail = jnp.stack([z[(b + 1) * tp - nk:(b + 1) * tp, Z_K:Z_K + 2 * N_KV * HEAD_DIM]
                             for b in range(bp)]).astype(F32)
        new_p[0].append(kv_tail[..., :N_KV * HEAD_DIM].reshape(bp, nk, N_KV, HEAD_DIM))
        new_p[1].append(kv_tail[..., N_KV * HEAD_DIM:].reshape(bp, nk, N_KV, HEAD_DIM))
        q_s = zs3[:, :, Z_Q:Z_Q + N_HEADS * HEAD_DIM].reshape(ts, bs, N_KV, Q_PER_KV, HEAD_DIM)
        q4 = jnp.transpose(q_s, (1, 2, 3, 0, 4)).reshape(bs, N_KV, Q_PER_KV * ts, HEAD_DIM)
        k_s = jnp.transpose(zs3[:, :, Z_K:Z_K + N_KV * HEAD_DIM].reshape(ts, bs, N_KV, HEAD_DIM), (1, 0, 2, 3))
        v_s = jnp.transpose(zs3[:, :, Z_V:Z_V + N_KV * HEAD_DIM].reshape(ts, bs, N_KV, HEAD_DIM), (1, 0, 2, 3))
        o4 = _attn_sample(q4, kc_all, vc_all, jnp.transpose(k_s, (0, 2, 1, 3)), jnp.transpose(v_s, (0, 2, 1, 3)),
                          bias_s, sink_s, l, bs_att)
        as_ = jnp.transpose(o4.reshape(bs, N_KV, Q_PER_KV, ts, HEAD_DIM), (3, 0, 1, 2, 4)).reshape(n_s, -1).astype(BF16)
        new_s[0].append(k_s.astype(F32))
        new_s[1].append(v_s.astype(F32))

        ncp = tp // lm
        if_p = zif[:n_p, :2 * M_HEADS].reshape(bp * ncp, lm, 2 * M_HEADS)
        mp, c1p, n1p, m1p = _mlstm_prompt(z, zif, jnp.transpose(if_p, (0, 2, 1)), gamma, l, bp, ncp, lm)
        new_p[3].append(c1p)
        new_p[4].append(n1p[:, :, 0])
        new_p[5].append(m1p[:, :M_HEADS, 0])
        tpad = ((0, 0), (0, lts - ts), (0, 0))
        zm3 = jnp.pad(jnp.transpose(zs3[:, :, Z_MQ:Z_K], (1, 0, 2)), tpad)
        if3 = jnp.pad(jnp.transpose(zif[n_p:].reshape(ts, bs, LANES), (1, 0, 2)), tpad)
        ms, s_c, n1s, m1s = _mlstm_sample(zm3, if3, jnp.transpose(if3[:, :, :2 * M_HEADS], (0, 2, 1)), gamma,
                                          state_mlstm_C, n0_all[l], m0_all[l], s_c, l, ts, bs_m)
        ms = jnp.transpose(ms[:, :ts], (1, 0, 2)).reshape(n_s, -1)
        new_s[4].append(n1s[:, :, 0])
        new_s[5].append(m1s[:, :M_HEADS, 0])

        moe = l % 2 == 1
        router = (rw_pad, rb_pad, j) if moe else None
        outs = _merge(cp, cs3.reshape(n_s, D_CONV), ap, as_, mp, ms, z, xp, xs, mod_p, mod_s,
                      wc_b, wa_b, wm_b, wo_b, pg, pb, l, rm, alpha, router)
        if moe:
            x1, h2, route = outs
            x = _moe_ffn(h2, route, moe_w_gate, moe_w_up, moe_w_down, x1, mod_p, mod_s, pg, pb,
                         l, j, rm, tb, tf_moe, alpha, split=l == depth - 1)
        else:
            x1, h2 = outs
            x = _ffn_dense(h2, fg_b, fu_b, fd_b, x1, mod_p, mod_s, pg, pb, l, j, rm, alpha)

    x_p, x_s = x if isinstance(x, (list, tuple)) else (x[:n_p], x[n_p:])
    y_p = x_p.reshape(bp, tp, d)
    y_s = jnp.transpose(x_s.reshape(ts, bs, d), (1, 0, 2))
    p_k, p_v, p_conv, p_c, p_n, p_m = [jnp.stack(a) for a in new_p]
    s_k, s_v, s_conv = [jnp.concatenate([old[:, :, ts:], jnp.stack(new)], axis=2)
                        for old, new in zip((cache_swa_k, cache_swa_v, state_conv), new_s[:3])]
    s_n, s_m = jnp.stack(new_s[4]), jnp.stack(new_s[5])
    return (y_p, y_s, p_k, p_v, p_conv, p_c, p_n, p_m, s_k, s_v, s_conv, s_c, s_n, s_m)
```

```python
import functools
import math

import jax
import jax.numpy as jnp
from jax import lax
from jax.experimental import pallas as pl
from jax.experimental.pallas import tpu as pltpu

F32 = jnp.float32
BF16 = jnp.bfloat16

D_MODEL = 1024
D_CONV = 512
CONV_W = 31
CONV_PAD = 32
N_HEADS = 8
N_KV = 2
HEAD_DIM = 64
Q_PER_KV = N_HEADS // N_KV
WINDOW = 128
N_BUCKETS = 32
MAX_DIST = 128
M_HEADS = 4
M_DK = 128
M_DV = 128
N_EXPERTS = 8
LN_EPS = 1e-5
LANES = 128
NEG_INF = float("-inf")
LOG2E = math.log2(math.e)
VMEM_LIMIT = 56 * 1024 * 1024

Z_G, Z_UA, Z_UB, Z_Q = 0, 3072, 3584, 4096
Z_MQ, Z_MK, Z_MV, Z_MO = 4608, 5120, 5632, 6144
Z_K, Z_V, Z_W = 6656, 6784, 6912
TN_IN = 6912
ATT_QB = 4
ATT_GROUP = 8
MOE_SUB = 256
CONV_CHUNK = 32

def _cparams(*sem):
    return pltpu.CompilerParams(dimension_semantics=sem, vmem_limit_bytes=VMEM_LIMIT)


def _sigmoid(x):
    return 1.0 / (1.0 + jnp.exp(-x))


def _silu(x):
    return x * _sigmoid(x)


def _log_sigmoid(x):
    return jnp.minimum(x, 0.0) - jnp.log(1.0 + jnp.exp(-jnp.abs(x)))


def _norm(x):
    mu = jnp.mean(x, axis=-1, keepdims=True)
    xc = x - mu
    var = jnp.mean(xc * xc, axis=-1, keepdims=True)
    return xc * lax.rsqrt(var + LN_EPS)


def _dot(a, b):
    return jnp.dot(a, b, preferred_element_type=F32)


def _dot_nt(a, b):
    return lax.dot_general(a, b, (((1,), (1,)), ((), ())), preferred_element_type=F32)


def _dot_tn(a, b):
    return lax.dot_general(a, b, (((0,), (0,)), ((), ())), preferred_element_type=F32)


def _dot_hi(a, b):
    return jnp.dot(a, b, preferred_element_type=F32, precision=lax.Precision.HIGHEST)


def _split_bf16(a):
    hi = a.astype(BF16)
    return hi, (a - hi.astype(F32)).astype(BF16)


_RARELY = pl.Buffered(1)


class _RowMap:
    def __init__(self, t, bp, tp, n_p, n_s):
        self.t = t
        self.bp = bp
        self.per_seq = tp // t
        self.np_tiles = n_p // t
        self.ns_tiles = n_s // t

    def seq(self, i):
        return jnp.minimum(i // self.per_seq, self.bp - 1)

    def prompt(self, i):
        return jnp.minimum(i, self.np_tiles - 1)

    def sample(self, i):
        return jnp.clip(i - self.np_tiles, 0, self.ns_tiles - 1)

    def mod_specs(self, l, k, d):
        return (pl.BlockSpec((None, None, 1, d), lambda i, *_: (l, self.seq(i), 0, k)),
                pl.BlockSpec((None, self.t, d), lambda i, *_: (l, self.sample(i), k), pipeline_mode=_RARELY))

    def x_specs(self, xs_arr, d):
        off = self.np_tiles if xs_arr.shape[0] > self.ns_tiles * self.t else 0
        return (pl.BlockSpec((self.t, d), lambda i, *_: (self.prompt(i), 0)),
                pl.BlockSpec((self.t, d), lambda i, *_: (off + self.sample(i), 0), pipeline_mode=_RARELY))


def _pick(is_s, p_ref, s_ref):
    return jnp.where(is_s, s_ref[...], p_ref[...])


def _ada_kernel(c_ref, w_ref, b_ref, o_ref):
    s = _silu(c_ref[...]).astype(BF16)
    o_ref[0] = _dot(s, w_ref[0].astype(BF16)) + b_ref[0]


def _ada_mod(c_all, w_ada, b_ada):
    depth, d, n6 = w_ada.shape
    rows = c_all.shape[0]
    return pl.pallas_call(
        _ada_kernel,
        grid=(depth, n6 // d),
        in_specs=[pl.BlockSpec((rows, d), lambda l, j: (0, 0)),
                  pl.BlockSpec((1, d, d), lambda l, j: (l, 0, j)),
                  pl.BlockSpec((1, 1, d), lambda l, j: (l, 0, j))],
        out_specs=pl.BlockSpec((1, rows, d), lambda l, j: (l, 0, j)),
        out_shape=jax.ShapeDtypeStruct((depth, rows, n6), F32),
        compiler_params=_cparams("parallel", "parallel"),
        name="ada_mod",
    )(c_all, w_ada, b_ada.reshape(depth, 1, n6))


def _ln_proj_kernel(xp_ref, xs_ref, shp_ref, shs_ref, scp_ref, scs_ref, w_ref, b_ref, wif_ref, bif_ref,
                    z_ref, zif_ref, h_scr, *, np_tiles):
    is_s = pl.program_id(0) >= np_tiles
    first = pl.program_id(1) == 0

    def prologue(x_ref, sh_ref, sc_ref):
        h = (_norm(x_ref[...]) * (1.0 + sc_ref[...]) + sh_ref[...]).astype(BF16)
        h_scr[...] = h
        zif_ref[...] = _dot(h, wif_ref[...]) + bif_ref[...]

    @pl.when(jnp.logical_and(first, is_s))
    def _():
        prologue(xs_ref, shs_ref, scs_ref)

    @pl.when(jnp.logical_and(first, jnp.logical_not(is_s)))
    def _():
        prologue(xp_ref, shp_ref, scp_ref)

    z_ref[...] = (_dot(h_scr[...], w_ref[...]) + b_ref[...]).astype(BF16)


def _ln_proj(xp, xs, mod_p, mod_s, w, b, wif, bif, l, rm, n):
    d = xp.shape[1]
    zw = w.shape[2]
    tm = rm.t
    shp, shs = rm.mod_specs(l, 0, d)
    scp, scs = rm.mod_specs(l, 1, d)
    wmode = _RARELY if zw == TN_IN else None
    return pl.pallas_call(
        functools.partial(_ln_proj_kernel, np_tiles=rm.np_tiles),
        grid=(n // tm, zw // TN_IN),
        in_specs=[*rm.x_specs(xs, d),
                  shp, shs, scp, scs,
                  pl.BlockSpec((None, d, TN_IN), lambda i, j: (l, 0, j), pipeline_mode=wmode),
                  pl.BlockSpec((None, 1, TN_IN), lambda i, j: (l, 0, j), pipeline_mode=wmode),
                  pl.BlockSpec((None, d, LANES), lambda i, j: (l, 0, 0), pipeline_mode=_RARELY),
                  pl.BlockSpec((None, 1, LANES), lambda i, j: (l, 0, 0), pipeline_mode=_RARELY)],
        out_specs=[pl.BlockSpec((tm, TN_IN), lambda i, j: (i, j)),
                   pl.BlockSpec((tm, LANES), lambda i, j: (i, 0))],
        out_shape=[jax.ShapeDtypeStruct((n, zw), BF16), jax.ShapeDtypeStruct((n, LANES), F32)],
        scratch_shapes=[pltpu.VMEM((tm, d), BF16)],
        compiler_params=_cparams("parallel", "arbitrary"),
        name="ln_proj",
    )(xp, xs, mod_p, mod_s, mod_p, mod_s, w, b, wif, bif)


def _conv_tail(yc, g_ref, b_ref):
    y = _norm(yc) * g_ref[...] + b_ref[...]
    return _silu(y).astype(BF16)


def _conv_prompt_kernel(ua_ref, ub_ref, st_ref, cw_ref, cb_ref, g_ref, b_ref, o_ref, ns_ref,
                        ext, shifted, yc, wrep):
    t = pl.program_id(1)
    tc = ua_ref.shape[0]
    sub = 8

    @pl.when(t == 0)
    def _():
        ext[0:CONV_PAD, :] = st_ref[0]

    @pl.when(t > 0)
    def _():
        ext[0:CONV_PAD, :] = ext[tc:tc + CONV_PAD, :]

    ext[CONV_PAD:, :] = ua_ref[...].astype(F32) * _sigmoid(ub_ref[...].astype(F32))
    for s in range(1, sub):
        shifted[s - 1] = ext[s:s + tc + CONV_PAD - sub, :]
    off = CONV_PAD - (CONV_W - 1)
    for w in range(CONV_W):
        wrep[w] = jnp.broadcast_to(cw_ref[w:w + 1, :], (sub, D_CONV))
    groups = CONV_CHUNK // sub

    for r0 in range(0, tc, CONV_CHUNK):
        acc = jnp.broadcast_to(cb_ref[...].reshape(1, 1, D_CONV), (groups, sub, D_CONV))
        for w in range(CONV_W):
            base, s = (off + w) // sub * sub, (off + w) % sub
            src = ext if s == 0 else shifted.at[s - 1]
            win = src[r0 + base:r0 + base + CONV_CHUNK, :]
            acc = acc + win.reshape(groups, sub, D_CONV) * wrep[w][None]
        yc[r0:r0 + CONV_CHUNK, :] = acc.reshape(CONV_CHUNK, D_CONV)
    o_ref[...] = _conv_tail(yc[...], g_ref, b_ref)

    @pl.when(t == pl.num_programs(1) - 1)
    def _():
        ns_ref[0] = ext[tc:tc + CONV_PAD, :]


def _conv_vec_specs(l, nargs):
    return [pl.BlockSpec((None, 1, D_CONV), lambda *_: (l, 0, 0)) for _ in range(nargs)]


def _conv_prompt(z, state_pad, cw, cb, g, b, l, bp, tp, tc):
    nt = tp // tc
    return pl.pallas_call(
        _conv_prompt_kernel,
        grid=(bp, nt),
        in_specs=[pl.BlockSpec((tc, D_CONV), lambda bb, t: (bb * nt + t, Z_UA // D_CONV)),
                  pl.BlockSpec((tc, D_CONV), lambda bb, t: (bb * nt + t, Z_UB // D_CONV)),
                  pl.BlockSpec((1, CONV_PAD, D_CONV), lambda bb, t: (bb, 0, 0)),
                  pl.BlockSpec((None, CONV_PAD, D_CONV), lambda bb, t: (l, 0, 0))] + _conv_vec_specs(l, 3),
        out_specs=[pl.BlockSpec((tc, D_CONV), lambda bb, t: (bb * nt + t, 0)),
                   pl.BlockSpec((1, CONV_PAD, D_CONV), lambda bb, t: (bb, 0, 0))],
        out_shape=[jax.ShapeDtypeStruct((bp * tp, D_CONV), BF16),
                   jax.ShapeDtypeStruct((bp, CONV_PAD, D_CONV), F32)],
        scratch_shapes=[pltpu.VMEM((tc + CONV_PAD, D_CONV), F32),
                        pltpu.VMEM((7, tc + CONV_PAD - 8, D_CONV), F32),
                        pltpu.VMEM((tc, D_CONV), F32),
                        pltpu.VMEM((CONV_PAD, 8, D_CONV), F32)],
        compiler_params=_cparams("parallel", "arbitrary"),
        name="conv_prompt",
    )(z, z, state_pad, cw, cb, g, b)


def _conv_sample_kernel(ua_ref, ub_ref, st_ref, cw_ref, cb_ref, g_ref, b_ref, *rest, first_layer):
    o_ref, ns_ref = rest[-2:]
    ts = ua_ref.shape[0]
    ns = CONV_W - 1
    a = ua_ref[...].astype(F32) * _sigmoid(ub_ref[...].astype(F32))
    st = st_ref[...]
    if first_layer is not None:
        for dd in range(ns_ref.shape[0]):
            if dd != first_layer:
                ns_ref[dd] = jnp.zeros(ns_ref.shape[1:], F32)
        ns_ref = ns_ref.at[first_layer]
    ns_ref[:, 0:ns - ts, :] = st[:, ts:, :]
    for t in range(ts):
        ns_ref[:, ns - ts + t, :] = a[t]
    row = lax.broadcasted_iota(jnp.int32, (ns, D_CONV), 0)
    for t in range(ts):
        wt = jnp.zeros((ns, D_CONV), F32)
        for j in range(t, ns):
            wt = jnp.where(row == j, cw_ref[j - t:j - t + 1, :], wt)
        yc = jnp.sum(st * wt[None], axis=1) + cb_ref[...]
        for t2 in range(t + 1):
            wi = CONV_W - 1 - (t - t2)
            yc = yc + a[t2] * cw_ref[wi:wi + 1, :]
        o_ref[t] = _conv_tail(yc, g_ref, b_ref)


def _conv_sample(zs3, state, cw, cb, g, b, ns_prev, l, bs_blk):
    ts, bs, _ = zs3.shape
    depth = state.shape[0]
    ns = CONV_W - 1
    in_specs = [pl.BlockSpec((ts, bs_blk, D_CONV), lambda i: (0, i, Z_UA // D_CONV)),
                pl.BlockSpec((ts, bs_blk, D_CONV), lambda i: (0, i, Z_UB // D_CONV)),
                pl.BlockSpec((None, bs_blk, ns, D_CONV), lambda i: (l, i, 0, 0)),
                pl.BlockSpec((None, CONV_PAD, D_CONV), lambda i: (l, 0, 0))] + _conv_vec_specs(l, 3)
    args = [zs3, zs3, state, cw, cb, g, b]
    aliases = {}
    if ns_prev is None:
        ns_spec = pl.BlockSpec((depth, bs_blk, ns, D_CONV), lambda i: (0, i, 0, 0))
    else:
        in_specs.append(pl.BlockSpec(memory_space=pl.ANY))
        args.append(ns_prev)
        aliases = {len(args) - 1: 1}
        ns_spec = pl.BlockSpec((None, bs_blk, ns, D_CONV), lambda i: (l, i, 0, 0))
    return pl.pallas_call(
        functools.partial(_conv_sample_kernel, first_layer=l if ns_prev is None else None),
        grid=(bs // bs_blk,),
        in_specs=in_specs,
        out_specs=[pl.BlockSpec((ts, bs_blk, D_CONV), lambda i: (0, i, 0)), ns_spec],
        out_shape=[jax.ShapeDtypeStruct((ts, bs, D_CONV), BF16),
                   jax.ShapeDtypeStruct((depth, bs, ns, D_CONV), F32)],
        input_output_aliases=aliases,
        compiler_params=_cparams("parallel"),
        name="conv_sample",
    )(*args)


def _t5_bucket(dist):
    max_exact = N_BUCKETS // 2
    d = jnp.maximum(dist, 0)
    large = max_exact + (jnp.log(jnp.maximum(d, 1).astype(F32) / max_exact)
                         / math.log(MAX_DIST / max_exact) * (N_BUCKETS - max_exact)).astype(jnp.int32)
    return jnp.where(d < max_exact, d, jnp.minimum(large, N_BUCKETS - 1))


def _bias_heads(rel_bias, dist, valid):
    onehot = (_t5_bucket(dist)[..., None] == jnp.arange(N_BUCKETS)).astype(F32)
    bias = jnp.einsum("qkb,bh->qkh", onehot, rel_bias.astype(F32), precision=lax.Precision.HIGHEST)
    bias = jnp.where(valid[..., None], bias, NEG_INF)
    return jnp.transpose(bias, (2, 0, 1))


def _attn_prompt_kernel(sink_ref, q_ref, kc_ref, kp_ref, vc_ref, vp_ref, bias_ref, o_ref, *, l):
    first = pl.program_id(1) == 0
    w = WINDOW
    nq = q_ref.shape[0] // w
    kall = jnp.concatenate([kp_ref[...], kc_ref[...]], axis=0).astype(F32)
    vall = jnp.concatenate([vp_ref[...], vc_ref[...]], axis=0).astype(F32)
    lane = lax.broadcasted_iota(jnp.int32, kall.shape, 1)
    lo = lane < HEAD_DIM
    kroll = pltpu.roll(kall, HEAD_DIM, 1)
    vroll = pltpu.roll(vall, HEAD_DIM, 1)

    def halves(a, aroll, g):
        if g == 0:
            return jnp.where(lo, a, 0.0).astype(BF16), jnp.where(lo, 0.0, aroll).astype(BF16)
        return jnp.where(lo, aroll, 0.0).astype(BF16), jnp.where(lo, 0.0, a).astype(BF16)

    kh = [halves(kall, kroll, g) for g in range(N_KV)]
    vh = [halves(vall, vroll, g) for g in range(N_KV)]
    col = lax.broadcasted_iota(jnp.int32, (w, 4 * w), 1)
    prev_col = (col % (2 * w)) < w
    tiles_per_g = Q_PER_KV // 2
    units = [(qi, tile) for qi in range(nq) for tile in range(N_HEADS // 2)]
    for u0 in range(0, len(units), ATT_GROUP):
        group = units[u0:u0 + ATT_GROUP]
        scores = []
        for qi, tile in group:
            r0, g = qi * w, tile // tiles_per_g
            q = q_ref[r0:r0 + w, tile * LANES:(tile + 1) * LANES]
            kk = jnp.concatenate([kh[g][0][r0:r0 + 2 * w], kh[g][1][r0:r0 + 2 * w]], axis=0)
            s = _dot_nt(q, kk) * (HEAD_DIM ** -0.5 * LOG2E) + bias_ref[tile]
            if qi == 0:
                s = jnp.where(jnp.logical_and(first, prev_col), NEG_INF, s)
            scores.append(s)
        probs = []
        for (qi, tile), s in zip(group, scores):
            ps = []
            for half in range(2):
                sh = s[:, half * 2 * w:(half + 1) * 2 * w]
                sink = sink_ref[l, 2 * tile + half]
                mx = jnp.maximum(jnp.max(sh, axis=-1, keepdims=True), sink)
                p = jnp.exp2(sh - mx)
                den = jnp.sum(p, axis=-1, keepdims=True) + jnp.exp2(sink - mx)
                ps.append((p * (1.0 / den)).astype(BF16))
            probs.append(jnp.concatenate(ps, axis=1))
        for (qi, tile), p in zip(group, probs):
            r0, g = qi * w, tile // tiles_per_g
            vv = jnp.concatenate([vh[g][0][r0:r0 + 2 * w], vh[g][1][r0:r0 + 2 * w]], axis=0)
            o_ref[r0:r0 + w, tile * LANES:(tile + 1) * LANES] = _dot(p, vv).astype(BF16)


def _attn_prompt(z, bias, sinks, l, bp, tp):
    w = WINDOW
    qb = ATT_QB if tp % (ATT_QB * w) == 0 else 1
    ns = tp // (qb * w)
    nb = tp // w
    kvw = N_KV * HEAD_DIM
    qw = N_HEADS * HEAD_DIM

    def cur(col):
        return lambda bb, i: (bb * ns + i, col)

    def prev(col):
        return lambda bb, i: (bb * nb + jnp.maximum(i * qb - 1, 0), col)

    return pl.pallas_call(
        functools.partial(_attn_prompt_kernel, l=l),
        grid=(bp, ns),
        in_specs=[pl.BlockSpec(memory_space=pltpu.SMEM),
                  pl.BlockSpec((qb * w, qw), cur(Z_Q // qw)),
                  pl.BlockSpec((qb * w, kvw), cur(Z_K // kvw)),
                  pl.BlockSpec((w, kvw), prev(Z_K // kvw)),
                  pl.BlockSpec((qb * w, kvw), cur(Z_V // kvw)),
                  pl.BlockSpec((w, kvw), prev(Z_V // kvw)),
                  pl.BlockSpec((N_HEADS // 2, w, 4 * w), lambda bb, i: (0, 0, 0))],
        out_specs=pl.BlockSpec((qb * w, qw), lambda bb, i: (bb * ns + i, 0)),
        out_shape=jax.ShapeDtypeStruct((bp * tp, qw), BF16),
        compiler_params=_cparams("parallel", "parallel"),
        name="attn_prompt",
    )(sinks, z, z, z, z, z, bias)


def _attn_sample_kernel(q_ref, kc_ref, vc_ref, kn_ref, vn_ref, bias_ref, sink_ref, o_ref):
    wb = kc_ref.shape[1]
    ts = kn_ref.shape[2]
    for g in range(N_KV):
        lo = g * HEAD_DIM
        qb = (q_ref[:, g].astype(F32) * (HEAD_DIM ** -0.5)).astype(BF16)
        kc = kc_ref[:, :, lo:lo + HEAD_DIM].astype(BF16)
        vc = vc_ref[:, :, lo:lo + HEAD_DIM].astype(BF16)
        kn = kn_ref[:, g].astype(F32)
        vn = vn_ref[:, g].astype(F32)
        bias = bias_ref[g]
        s_c = jnp.einsum("bqd,bkd->bqk", qb, kc, preferred_element_type=F32) + bias[None, :, :wb]
        qf = qb.astype(F32)
        s_n = [jnp.sum(qf * kn[:, j:j + 1, :], axis=-1, keepdims=True) + bias[None, :, wb + j:wb + j + 1]
               for j in range(ts)]
        sink = sink_ref[g][None]
        mx = jnp.maximum(jnp.max(s_c, axis=-1, keepdims=True), sink)
        for sj in s_n:
            mx = jnp.maximum(mx, sj)
        p_c = jnp.exp(s_c - mx)
        p_n = [jnp.exp(sj - mx) for sj in s_n]
        den = jnp.sum(p_c, axis=-1, keepdims=True) + jnp.exp(sink - mx)
        for pj in p_n:
            den = den + pj
        o = jnp.einsum("bqk,bkd->bqd", (p_c / den).astype(BF16), vc, preferred_element_type=F32)
        for j in range(ts):
            o = o + (p_n[j] / den).astype(BF16).astype(F32) * vn[:, j:j + 1, :]
        o_ref[:, g] = o


def _attn_sample(q4, kc, vc, kn, vn, bias, sinks, l, bs_blk):
    bs, _, rt, _ = q4.shape
    wb = kc.shape[2]
    ts = kn.shape[2]
    kvw = N_KV * HEAD_DIM
    return pl.pallas_call(
        _attn_sample_kernel,
        grid=(bs // bs_blk,),
        in_specs=[pl.BlockSpec((bs_blk, N_KV, rt, HEAD_DIM), lambda i: (i, 0, 0, 0)),
                  pl.BlockSpec((None, bs_blk, wb, kvw), lambda i: (l, i, 0, 0)),
                  pl.BlockSpec((None, bs_blk, wb, kvw), lambda i: (l, i, 0, 0)),
                  pl.BlockSpec((bs_blk, N_KV, ts, HEAD_DIM), lambda i: (i, 0, 0, 0)),
                  pl.BlockSpec((bs_blk, N_KV, ts, HEAD_DIM), lambda i: (i, 0, 0, 0)),
                  pl.BlockSpec((N_KV, rt, wb + ts), lambda i: (0, 0, 0)),
                  pl.BlockSpec((N_KV, rt, 1), lambda i: (0, 0, 0))],
        out_specs=pl.BlockSpec((bs_blk, N_KV, rt, HEAD_DIM), lambda i: (i, 0, 0, 0)),
        out_shape=jax.ShapeDtypeStruct((bs, N_KV, rt, HEAD_DIM), F32),
        compiler_params=_cparams("parallel"),
        name="attn_sample",
    )(q4, kc, vc, kn, vn, bias, sinks)


def _mlstm_kernel(*refs, t_valid, nseq, aliased, carried, chunk_axis, per_seq_inputs=False, first_layer=None):
    if aliased:
        refs = refs[:10] + refs[11:]
    if per_seq_inputs:
        groups = [refs[6 * s:6 * s + 6] for s in range(nseq)]
        if_ref, ifr_ref, q_ref, k_ref, v_ref, o_ref = (tuple(g[k] for g in groups) for k in range(6))
        refs = (None,) * 6 + refs[6 * nseq:]
    else:
        if_ref, ifr_ref, q_ref, k_ref, v_ref, o_ref = refs[:6]
    g_ref, c0_ref, n0_ref, m0_ref, h_ref, c1_ref, n1_ref, m1_ref = refs[6:14]
    c = pl.program_id(chunk_axis)
    L = (ifr_ref[0] if per_seq_inputs else ifr_ref).shape[-1]
    if carried:
        c_in, n_in, m_in = c_out, n_out, m_out = refs[14:]

        @pl.when(c == 0)
        def _():
            c_in[...] = c0_ref[...]
            n_in[...] = n0_ref[...]
            m_in[...] = m0_ref[...]
    else:
        (c_in, n_in, m_in), (c_out, n_out, m_out) = (c0_ref, n0_ref, m0_ref), (c1_ref, n1_ref, m1_ref)
        if first_layer is not None:
            for dd in range(c1_ref.shape[0]):
                if dd != first_layer:
                    c1_ref[dd] = jnp.zeros(c1_ref.shape[1:], F32)
            c_out = c1_ref.at[first_layer]

    def seq(ref, s):
        return ref[s] if isinstance(ref, tuple) else ref.at[s]

    tt = lax.broadcasted_iota(jnp.int32, (L, L), 0)
    ss = lax.broadcasted_iota(jnp.int32, (L, L), 1)
    causal = ss <= tt
    tril = causal.astype(F32)
    triu = (tt <= ss).astype(F32)
    seqs = range(nseq)
    heads = [(s_i, h) for s_i in seqs for h in range(M_HEADS)]
    mxu_sums = L % LANES == 0
    gates = []
    for s_i in seqs:
        ifc = seq(if_ref, s_i)[...]
        ifr = ifr_ref[s_i][0] if per_seq_inputs else ifr_ref[s_i]
        lf_c = _log_sigmoid(ifc)
        lf_r = _log_sigmoid(ifr)
        i_c, i_r = ifc, ifr
        if t_valid < L:
            rc = lax.broadcasted_iota(jnp.int32, (L, LANES), 0) < t_valid
            rr = lax.broadcasted_iota(jnp.int32, (2 * M_HEADS, L), 1) < t_valid
            lf_c = jnp.where(rc, lf_c, 0.0)
            lf_r = jnp.where(rr, lf_r, 0.0)
            i_c = jnp.where(rc, i_c, NEG_INF)
            i_r = jnp.where(rr, i_r, NEG_INF)
        gates.append((lf_c, lf_r, i_c, i_r))
    f_cs = [_dot_hi(tril, g[0]) for g in gates]
    f_rs = [_dot_hi(g[1], triu) for g in gates]
    state = {(s_i, h): (c_in[s_i, h], n_in[s_i, h], m_in[s_i, h:h + 1, :]) for s_i, h in heads}

    def wide(col):
        return jnp.concatenate([col] * (L // LANES), axis=1) if L >= LANES else col[:, :L]

    st1 = {}
    for s_i, h in heads:
        _, nrow, m0 = state[s_i, h]
        fc = jnp.broadcast_to(f_cs[s_i][:, M_HEADS + h:M_HEADS + h + 1], (L, LANES))
        fr = f_rs[s_i][M_HEADS + h:M_HEADS + h + 1, :]
        ir = gates[s_i][3][h:h + 1, :]
        dm = jnp.where(causal, wide(fc) - fr + ir, NEG_INF)
        m_t = jnp.maximum(m0 + fc, jnp.max(dm, axis=-1, keepdims=True))
        st1[s_i, h] = (fc, dm, m_t, jnp.exp(m0 + fc - m_t))
    st2 = {}
    for s_i, h in heads:
        lo = h * M_DK
        fc, dm, m_t, inter = st1[s_i, h]
        qb = seq(q_ref, s_i)[:, lo:lo + M_DK]
        kf = seq(k_ref, s_i)[:, lo:lo + M_DK].astype(F32) * (M_DK ** -0.5)
        vf = seq(v_ref, s_i)[:, lo:lo + M_DV].astype(F32)
        kb, vb = kf.astype(BF16), vf.astype(BF16)
        if mxu_sums:
            n_rows = jnp.broadcast_to(state[s_i, h][1], (M_DK, M_DK)).astype(BF16)
            qk = _dot_nt(qb, jnp.concatenate([kb, n_rows], axis=0))
            sc, qn_rep = qk[:, :L] * jnp.exp(dm - wide(m_t)), qk[:, L:]
        else:
            sc, qn_rep = _dot_nt(qb, kb) * jnp.exp(dm - wide(m_t)), None
        st2[s_i, h] = (qb, kf, vf, kb, vb, sc, qn_rep)
    st3 = {}
    for s_i, h in heads:
        cm, nrow, m0 = state[s_i, h]
        fc, dm, m_t, inter = st1[s_i, h]
        qb, kf, vf, kb, vb, sc, qn_rep = st2[s_i, h]
        if mxu_sums:
            sv = _dot(sc.astype(BF16), jnp.concatenate([vb, jnp.ones((L, M_DV), BF16)], axis=1))
            num = inter * _dot(qb, cm.astype(BF16)) + sv[:, :M_DV]
            qn = inter * qn_rep + sv[:, M_DV:]
            floor = jnp.exp(-m_t)
        else:
            num = inter * _dot(qb, cm.astype(BF16)) + _dot(sc.astype(BF16), vb)
            qn = (inter * jnp.sum(qb.astype(F32) * nrow, axis=-1, keepdims=True)
                  + jnp.sum(sc, axis=-1, keepdims=True))
            floor = jnp.exp(-m_t)
        st3[s_i, h] = num / jnp.maximum(jnp.abs(qn), floor)
    new_state = {}
    for s_i, h in heads:
        cm, nrow, m0 = state[s_i, h]
        fc, dm, m_t, inter = st1[s_i, h]
        qb, kf, vf, kb, vb, sc, _ = st2[s_i, h]
        ic = jnp.broadcast_to(gates[s_i][2][:, h:h + 1], (L, LANES))
        m_end = m_t[L - 1:L, :]
        f_end = fc[L - 1:L, :]
        decay = jnp.exp(m0 + f_end - m_end)
        w_s = jnp.exp(f_end - fc + ic - m_end)
        new_state[s_i, h] = (decay * cm + _dot_tn(kb, (w_s * vf).astype(BF16)),
                             decay * nrow + jnp.sum(w_s * kf, axis=0, keepdims=True),
                             m_end)
    for s_i, h in heads:
        lo = h * M_DK
        hn = _norm(st3[s_i, h]) * g_ref[:, lo:lo + M_DV]
        gate = _sigmoid(seq(o_ref, s_i)[:, lo:lo + M_DV].astype(F32))
        seq(h_ref, s_i)[:, lo:lo + M_DV] = (gate * hn).astype(BF16)
    for s_i, h in heads:
        c_out[s_i, h], n_out[s_i, h], m_out[s_i, h:h + 1, :] = new_state[s_i, h]
    if not carried:
        for s_i in seqs:
            m_out[s_i, M_HEADS:, :] = jnp.zeros((M_HEADS, LANES), F32)

    if carried:
        @pl.when(c == pl.num_programs(chunk_axis) - 1)
        def _():
            c1_ref[...] = c_out[...]
            n1_ref[...] = n_out[...]
            m1_ref[...] = m_out[...]


def _mlstm_state_specs(nseq, l_state):
    if l_state is None:
        c_spec = pl.BlockSpec((nseq, M_HEADS, M_DK, M_DV), lambda b, c: (b, 0, 0, 0))
    else:
        c_spec = pl.BlockSpec((None, nseq, M_HEADS, M_DK, M_DV), lambda b, c: (l_state, b, 0, 0, 0))
    return (c_spec,
            pl.BlockSpec((nseq, M_HEADS, 1, M_DK), lambda b, c: (b, 0, 0, 0)),
            pl.BlockSpec((nseq, 2 * M_HEADS, LANES), lambda b, c: (b, 0, 0)))


def _mlstm_scratch(nseq):
    return [pltpu.VMEM((nseq, M_HEADS, M_DK, M_DV), F32),
            pltpu.VMEM((nseq, M_HEADS, 1, M_DK), F32),
            pltpu.VMEM((nseq, 2 * M_HEADS, LANES), F32)]


def _mlstm_prompt(z, zif, ifr, gamma, l, bp, nc, L):
    hw = M_HEADS * M_DK
    zero = lambda *s: jnp.zeros(s, F32)
    in_specs, args = [], []
    for b in range(bp):
        def rows(col, b=b):
            return lambda c: (b * nc + c, col)

        in_specs += [pl.BlockSpec((L, LANES), rows(0)),
                     pl.BlockSpec((1, 2 * M_HEADS, L), lambda c, b=b: (b * nc + c, 0, 0)),
                     pl.BlockSpec((L, hw), rows(Z_MQ // hw)),
                     pl.BlockSpec((L, hw), rows(Z_MK // hw)),
                     pl.BlockSpec((L, hw), rows(Z_MV // hw)),
                     pl.BlockSpec((L, hw), rows(Z_MO // hw))]
        args += [zif, ifr, z, z, z, z]
    state_specs = [pl.BlockSpec((bp, M_HEADS, M_DK, M_DV), lambda c: (0, 0, 0, 0)),
                   pl.BlockSpec((bp, M_HEADS, 1, M_DK), lambda c: (0, 0, 0, 0)),
                   pl.BlockSpec((bp, 2 * M_HEADS, LANES), lambda c: (0, 0, 0))]
    kern = functools.partial(_mlstm_kernel, t_valid=L, nseq=bp, aliased=False, carried=True, chunk_axis=0,
                             per_seq_inputs=True)
    outs = pl.pallas_call(
        kern,
        grid=(nc,),
        in_specs=in_specs + [pl.BlockSpec((None, 1, hw), lambda c: (l, 0, 0))] + state_specs,
        out_specs=[pl.BlockSpec((bp, L, hw), lambda c: (0, c, 0))] + state_specs,
        out_shape=[jax.ShapeDtypeStruct((bp, nc * L, hw), BF16),
                   jax.ShapeDtypeStruct((bp, M_HEADS, M_DK, M_DV), F32),
                   jax.ShapeDtypeStruct((bp, M_HEADS, 1, M_DK), F32),
                   jax.ShapeDtypeStruct((bp, 2 * M_HEADS, LANES), F32)],
        scratch_shapes=_mlstm_scratch(bp),
        compiler_params=_cparams("arbitrary"),
        name="mlstm_prompt",
    )(*args, gamma, zero(bp, M_HEADS, M_DK, M_DV), zero(bp, M_HEADS, 1, M_DK), zero(bp, 2 * M_HEADS, LANES))
    return (outs[0].reshape(bp * nc * L, hw),) + tuple(outs[1:])


def _mlstm_sample(zm3, if3, ifr, gamma, c_all, n0, m0x, c_out_prev, l, t_valid, nseq):
    bs, L, _ = zm3.shape
    depth = c_all.shape[0]
    hw = M_HEADS * M_DK
    aliased = c_out_prev is not None

    def blk(col):
        return pl.BlockSpec((nseq, L, hw), lambda b, c: (b, 0, col))

    c_in, n_spec, m_spec = _mlstm_state_specs(nseq, l)
    in_specs = [pl.BlockSpec((nseq, L, LANES), lambda b, c: (b, 0, 0)),
                pl.BlockSpec((nseq, 2 * M_HEADS, L), lambda b, c: (b, 0, 0)),
                blk(0), blk(1), blk(2), blk(3),
                pl.BlockSpec((None, 1, hw), lambda b, c: (l, 0, 0)),
                c_in, n_spec, m_spec]
    args = [if3, ifr, zm3, zm3, zm3, zm3, gamma, c_all, n0, m0x]
    aliases = {}
    if aliased:
        in_specs.append(pl.BlockSpec(memory_space=pl.ANY))
        args.append(c_out_prev)
        aliases = {len(args) - 1: 1}
        c_out = c_in
    else:
        c_out = pl.BlockSpec((depth, nseq, M_HEADS, M_DK, M_DV), lambda b, c: (0, b, 0, 0, 0))
    kern = functools.partial(_mlstm_kernel, t_valid=t_valid, nseq=nseq, aliased=aliased, carried=False,
                             chunk_axis=1, first_layer=None if aliased else l)
    return pl.pallas_call(
        kern,
        grid=(bs // nseq, 1),
        in_specs=in_specs,
        out_specs=[pl.BlockSpec((nseq, L, hw), lambda b, c: (b, 0, 0)), c_out, n_spec, m_spec],
        out_shape=[jax.ShapeDtypeStruct((bs, L, hw), BF16),
                   jax.ShapeDtypeStruct((depth, bs, M_HEADS, M_DK, M_DV), F32),
                   jax.ShapeDtypeStruct((bs, M_HEADS, 1, M_DK), F32),
                   jax.ShapeDtypeStruct((bs, 2 * M_HEADS, LANES), F32)],
        input_output_aliases=aliases,
        compiler_params=_cparams("parallel", "arbitrary"),
        name="mlstm_sample",
    )(*args)


def _m_state_in(m):
    lead = m.shape[:-1]
    mx = jnp.zeros(lead + (2 * M_HEADS, LANES), F32)
    return mx.at[..., :M_HEADS, :].set(jnp.broadcast_to(m[..., None], lead + (M_HEADS, LANES)))


def _merge_kernel(*refs, np_tiles, alpha, route):
    (cp_ref, cs_ref, ap_ref, as_ref, mp_ref, ms_ref, g0_ref, g1_ref, g2_ref, xp_ref, xs_ref,
     gtp_ref, gts_ref, shp_ref, shs_ref, scp_ref, scs_ref,
     wc_ref, wa_ref, wm_ref, wo_ref, pg_ref, pb_ref) = refs[:23]
    if route:
        rw_ref, rb_ref, x1_ref, h_ref, route_ref = refs[23:]
    else:
        x1_ref, h_ref = refs[23:]
    is_s = pl.program_id(0) >= np_tiles

    def gate(ref):
        return _sigmoid(ref[...].astype(F32))

    def run(c_ref, a_ref, m_ref, x_ref, gt_ref, sh_ref, sc_ref):
        y = (gate(g0_ref) * _dot(c_ref[...], wc_ref[...])
             + gate(g1_ref) * _dot(a_ref[...], wa_ref[...])
             + gate(g2_ref) * _dot(m_ref[...], wm_ref[...]))
        mix = _dot(y.astype(BF16), wo_ref[...])
        x1 = _norm(alpha * x_ref[...] + gt_ref[...] * mix) * pg_ref[...] + pb_ref[...]
        x1_ref[...] = x1
        h = _norm(x1) * (1.0 + sc_ref[...]) + sh_ref[...]
        h_ref[...] = h.astype(h_ref.dtype)
        if route:
            h_hi, h_lo = _split_bf16(h)
            w_hi, w_lo = _split_bf16(rw_ref[...])
            logits = _dot(h_hi, w_hi) + (_dot(h_hi, w_lo) + _dot(h_lo, w_hi)) + rb_ref[...]
            lane = lax.broadcasted_iota(jnp.int32, logits.shape, 1)
            m1 = jnp.max(logits, axis=-1, keepdims=True)
            e1 = jnp.min(jnp.where(logits == m1, lane, LANES), axis=-1, keepdims=True)
            l2 = jnp.where(lane == e1, NEG_INF, logits)
            m2 = jnp.max(l2, axis=-1, keepdims=True)
            e2 = jnp.min(jnp.where(l2 == m2, lane, LANES), axis=-1, keepdims=True)
            ex = jnp.exp(m2 - m1)
            w1 = 1.0 / (1.0 + ex)
            w2 = ex / (1.0 + ex)
            out = jnp.where(lane == 0, e1.astype(F32),
                            jnp.where(lane == 1, e2.astype(F32),
                                      jnp.where(lane == 2, w1, jnp.where(lane == 3, w2, 0.0))))
            route_ref[...] = out

    @pl.when(is_s)
    def _():
        run(cs_ref, as_ref, ms_ref, xs_ref, gts_ref, shs_ref, scs_ref)

    @pl.when(jnp.logical_not(is_s))
    def _():
        run(cp_ref, ap_ref, mp_ref, xp_ref, gtp_ref, shp_ref, scp_ref)


def _merge(cp, cs, ap, as_, mp, ms, z, xp, xs, mod_p, mod_s, wc, wa, wm, wo, pg, pb, l, rm, alpha, router):
    n, d = z.shape[0], xp.shape[1]
    hw = D_CONV
    route = router is not None
    tm = rm.t

    def pblk():
        return pl.BlockSpec((tm, hw), lambda i: (rm.prompt(i), 0))

    def sblk():
        return pl.BlockSpec((tm, hw), lambda i: (rm.sample(i), 0), pipeline_mode=_RARELY)

    def zg(k):
        return pl.BlockSpec((tm, d), lambda i: (i, Z_G // d + k))

    def lw(a):
        return pl.BlockSpec((None,) + a.shape[1:], lambda i: (l,) + (0,) * (a.ndim - 1), pipeline_mode=_RARELY)

    post = pl.BlockSpec((None, None, 1, d), lambda i: (l, 0, 0, 0))
    in_specs = [pblk(), sblk(), pblk(), sblk(), pblk(), sblk(), zg(0), zg(1), zg(2),
                *rm.x_specs(xs, d),
                *rm.mod_specs(l, 2, d), *rm.mod_specs(l, 3, d), *rm.mod_specs(l, 4, d),
                lw(wc), lw(wa), lw(wm), lw(wo), post, post]
    args = [cp, cs, ap, as_, mp, ms, z, z, z, xp, xs, mod_p, mod_s, mod_p, mod_s, mod_p, mod_s,
            wc, wa, wm, wo, pg, pb]
    out_specs = [pl.BlockSpec((tm, d), lambda i: (i, 0)), pl.BlockSpec((tm, d), lambda i: (i, 0))]
    out_shape = [jax.ShapeDtypeStruct((n, d), F32), jax.ShapeDtypeStruct((n, d), F32 if route else BF16)]
    if route:
        rw, rb, lj = router
        in_specs += [pl.BlockSpec((None, d, LANES), lambda i: (lj, 0, 0)),
                     pl.BlockSpec((None, 1, LANES), lambda i: (lj, 0, 0))]
        args += [rw, rb]
        out_specs.append(pl.BlockSpec((tm, LANES), lambda i: (i, 0)))
        out_shape.append(jax.ShapeDtypeStruct((n, LANES), F32))
    kern = functools.partial(_merge_kernel, np_tiles=rm.np_tiles, alpha=alpha, route=route)
    return pl.pallas_call(
        kern, grid=(n // tm,), in_specs=in_specs, out_specs=out_specs, out_shape=out_shape,
        compiler_params=_cparams("parallel"), name="merge_route" if route else "merge",
    )(*args)


def _post_residual(x_ref, gp_ref, gs_ref, f, pg_ref, pb_ref, is_s, alpha):
    return _norm(alpha * x_ref[...] + _pick(is_s, gp_ref, gs_ref) * f) * pg_ref[...] + pb_ref[...]


def _ffn_kernel(h_ref, wg_ref, wu_ref, wd_ref, x_ref, gp_ref, gs_ref, pg_ref, pb_ref, o_ref, *, alpha, np_tiles):
    h = h_ref[...]
    a = (_silu(_dot(h, wg_ref[...])) * _dot(h, wu_ref[...])).astype(BF16)
    is_s = pl.program_id(0) >= np_tiles
    o_ref[...] = _post_residual(x_ref, gp_ref, gs_ref, _dot(a, wd_ref[...]), pg_ref, pb_ref, is_s, alpha)


def _ffn_dense(h, wg, wu, wd, x, mod_p, mod_s, pg, pb, l, lj, rm, alpha):
    n, d = x.shape
    f = wg.shape[2]
    tm = rm.t
    post = pl.BlockSpec((None, None, 1, d), lambda i: (l, 1, 0, 0))
    return pl.pallas_call(
        functools.partial(_ffn_kernel, alpha=alpha, np_tiles=rm.np_tiles),
        grid=(n // tm,),
        in_specs=[pl.BlockSpec((tm, d), lambda i: (i, 0)),
                  pl.BlockSpec((None, d, f), lambda i: (lj, 0, 0), pipeline_mode=_RARELY),
                  pl.BlockSpec((None, d, f), lambda i: (lj, 0, 0), pipeline_mode=_RARELY),
                  pl.BlockSpec((None, f, d), lambda i: (lj, 0, 0), pipeline_mode=_RARELY),
                  pl.BlockSpec((tm, d), lambda i: (i, 0)),
                  *rm.mod_specs(l, 5, d), post, post],
        out_specs=pl.BlockSpec((tm, d), lambda i: (i, 0)),
        out_shape=jax.ShapeDtypeStruct((n, d), F32),
        compiler_params=_cparams("parallel"),
        name="ffn_dense",
    )(h, wg, wu, wd, x, mod_p, mod_s, pg, pb)


def _rank_kernel(route_ref, rank_ref, tot_ref, carry):
    i = pl.program_id(0)
    tm = route_ref.shape[0]

    @pl.when(i == 0)
    def _():
        carry[...] = jnp.zeros_like(carry)

    r = route_ref[...]
    lane = lax.broadcasted_iota(jnp.int32, (tm, LANES), 1)
    e1 = r[:, 0:1].astype(jnp.int32)
    e2 = r[:, 1:2].astype(jnp.int32)
    hit1 = lane == e1
    hit2 = lane == e2
    onehot = jnp.where(jnp.logical_or(hit1, hit2), 1.0, 0.0)
    tt = lax.broadcasted_iota(jnp.int32, (tm, tm), 0)
    ss = lax.broadcasted_iota(jnp.int32, (tm, tm), 1)
    before = jnp.where(ss < tt, 1.0, 0.0).astype(BF16)
    cnt = _dot(before, onehot.astype(BF16)) + carry[0:1, :]
    r1 = jnp.sum(jnp.where(hit1, cnt, 0.0), axis=-1, keepdims=True)
    r2 = jnp.sum(jnp.where(hit2, cnt, 0.0), axis=-1, keepdims=True)
    rank_ref[...] = jnp.where(lane == 0, r1, jnp.where(lane == 1, r2, 0.0))
    carry[...] = carry[...] + jnp.sum(onehot, axis=0, keepdims=True)
    tot_ref[...] = carry[...]


def _moe_rank(route, tm):
    n = route.shape[0]
    return pl.pallas_call(
        _rank_kernel,
        grid=(n // tm,),
        in_specs=[pl.BlockSpec((tm, LANES), lambda i: (i, 0))],
        out_specs=[pl.BlockSpec((tm, LANES), lambda i: (i, 0)),
                   pl.BlockSpec((8, LANES), lambda i: (0, 0))],
        out_shape=[jax.ShapeDtypeStruct((n, LANES), F32), jax.ShapeDtypeStruct((8, LANES), F32)],
        scratch_shapes=[pltpu.VMEM((8, LANES), F32)],
        compiler_params=_cparams("arbitrary"),
        name="moe_rank",
    )(route)


def _row_copy(src, s, dst, t, sem):
    return pltpu.make_async_copy(src.at[pl.ds(s, 1)], dst.at[pl.ds(t, 1)], sem)


def _dispatch_kernel(d1_ref, d2_ref, zl_ref, h_ref, xs_hbm, zbuf, stage, sem, zsem):
    i = pl.program_id(0)
    tm = h_ref.shape[0]

    @pl.when(i == 0)
    def _():
        zbuf[...] = jnp.zeros_like(zbuf)

        def zero_copy(k):
            row = pl.multiple_of(jnp.maximum(zl_ref[k], 0), MOE_SUB)
            return pltpu.make_async_copy(zbuf, xs_hbm.at[pl.ds(row, MOE_SUB)], zsem)

        def start(k, carry):
            @pl.when(zl_ref[k] >= 0)
            def _():
                zero_copy(k).start()
            return carry

        def wait(k, carry):
            @pl.when(zl_ref[k] >= 0)
            def _():
                zero_copy(k).wait()
            return carry

        lax.fori_loop(0, zl_ref.shape[0], start, 0)
        lax.fori_loop(0, zl_ref.shape[0], wait, 0)

    slot = i % 2
    stage[slot] = h_ref[...]

    def issue(r, carry):
        _row_copy(stage.at[slot], r, xs_hbm, d1_ref[i * tm + r], sem.at[slot]).start(priority=0)
        _row_copy(stage.at[slot], r, xs_hbm, d2_ref[i * tm + r], sem.at[slot]).start(priority=1)
        return carry

    lax.fori_loop(0, tm, issue, 0, unroll=8)

    def drain(s):
        for _ in range(2):
            pltpu.make_async_copy(stage.at[s], xs_hbm.at[pl.ds(0, tm)], sem.at[s]).wait()

    @pl.when(i > 0)
    def _():
        drain(1 - slot)

    @pl.when(i == pl.num_programs(0) - 1)
    def _():
        drain(slot)


def _moe_dispatch(dest1, dest2, zlist, h, n_rows, tm):
    n, d = h.shape
    grid_spec = pltpu.PrefetchScalarGridSpec(
        num_scalar_prefetch=3,
        grid=(n // tm,),
        in_specs=[pl.BlockSpec((tm, d), lambda i, *_: (i, 0))],
        out_specs=pl.BlockSpec(memory_space=pl.ANY),
        scratch_shapes=[pltpu.VMEM((MOE_SUB, d), F32), pltpu.VMEM((2, tm, d), F32),
                        pltpu.SemaphoreType.DMA((2,)), pltpu.SemaphoreType.DMA(())],
    )
    return pl.pallas_call(
        _dispatch_kernel, grid_spec=grid_spec,
        out_shape=jax.ShapeDtypeStruct((n_rows, d), F32),
        compiler_params=_cparams("arbitrary"),
        name="moe_dispatch",
    )(dest1, dest2, zlist, h)


def _expert_kernel(blk_e_ref, nvalid_ref, xs_ref, wg_ref, wu_ref, wd_ref, y_ref, xb):
    b = pl.program_id(0)
    j = pl.program_id(1)
    tb = xb.shape[0]
    nsub = tb // MOE_SUB
    nv = nvalid_ref[b]
    used = (nv + MOE_SUB - 1) // MOE_SUB

    def run(rows):
        wg = wg_ref[...].astype(BF16)
        wu = wu_ref[...].astype(BF16)
        wd = wd_ref[...].astype(BF16)

        @pl.when(j == 0)
        def _():
            xb[rows, :] = xs_ref[rows, :].astype(BF16)

        x = xb[rows, :]
        a = (_silu(_dot(x, wg)) * _dot(x, wu)).astype(BF16)
        part = _dot(a, wd)

        @pl.when(j == 0)
        def _():
            y_ref[rows, :] = part

        @pl.when(j > 0)
        def _():
            y_ref[rows, :] = y_ref[rows, :] + part

    for m in range(1, nsub + 1):
        @pl.when(used == m)
        def _():
            run(pl.ds(0, m * MOE_SUB))

    for s in range(nsub):
        @pl.when(jnp.logical_and(s >= used, j == pl.num_programs(1) - 1))
        def _():
            y_ref[pl.ds(s * MOE_SUB, MOE_SUB), :] = jnp.zeros((MOE_SUB, y_ref.shape[1]), F32)


def _moe_experts(blk_e, nvalid, xs, wg, wu, wd, lj, tb, tf):
    n_rows, d = xs.shape
    f = wg.shape[3]
    nj = f // tf

    def jm(b, j, nv):
        return jnp.where(nv[b] > 0, j, nj - 1)

    grid_spec = pltpu.PrefetchScalarGridSpec(
        num_scalar_prefetch=2,
        grid=(n_rows // tb, nj),
        in_specs=[pl.BlockSpec((tb, d), lambda b, j, be, nv: (b, 0)),
                  pl.BlockSpec((None, None, d, tf), lambda b, j, be, nv: (lj, be[b], 0, jm(b, j, nv))),
                  pl.BlockSpec((None, None, d, tf), lambda b, j, be, nv: (lj, be[b], 0, jm(b, j, nv))),
                  pl.BlockSpec((None, None, tf, d), lambda b, j, be, nv: (lj, be[b], jm(b, j, nv), 0))],
        out_specs=pl.BlockSpec((tb, d), lambda b, j, be, nv: (b, 0)),
        scratch_shapes=[pltpu.VMEM((tb, d), BF16)],
    )
    return pl.pallas_call(
        _expert_kernel, grid_spec=grid_spec,
        out_shape=jax.ShapeDtypeStruct((n_rows, d), F32),
        compiler_params=_cparams("parallel", "arbitrary"),
        name="moe_experts",
    )(blk_e, nvalid, xs, wg, wu, wd)


def _combine_kernel(d1_ref, d2_ref, ys_hbm, route_ref, x_ref, gp_ref, gs_ref, pg_ref, pb_ref, *rest,
                    alpha, np_tiles, split):
    (*outs, y1, y2, sem1, sem2) = rest
    i = pl.program_id(0)
    tm = x_ref.shape[0]
    slot = i % 2

    last = pl.num_programs(0) - 1

    def issue(tile, s, r):
        _row_copy(ys_hbm, d1_ref[tile * tm + r], y1.at[s], r, sem1.at[s]).start(priority=0)
        _row_copy(ys_hbm, d2_ref[tile * tm + r], y2.at[s], r, sem2.at[s]).start(priority=1)

    def drain(s):
        pltpu.make_async_copy(ys_hbm.at[pl.ds(0, tm)], y1.at[s], sem1.at[s]).wait()
        pltpu.make_async_copy(ys_hbm.at[pl.ds(0, tm)], y2.at[s], sem2.at[s]).wait()

    @pl.when(i == 0)
    def _():
        lax.fori_loop(0, tm, lambda r, c: (issue(0, 0, r), c)[1], 0, unroll=8)

    drain(slot)
    nxt = jnp.minimum(i + 1, last)
    for r in range(tm):
        issue(nxt, 1 - slot, r)
    r = route_ref[...]
    f = y1[slot] * r[:, 2:3] + y2[slot] * r[:, 3:4]
    out = _post_residual(x_ref, gp_ref, gs_ref, f, pg_ref, pb_ref, i >= np_tiles, alpha)
    if split:
        op_ref, os_ref = outs

        @pl.when(i < np_tiles)
        def _():
            op_ref[...] = out

        @pl.when(i >= np_tiles)
        def _():
            os_ref[...] = out
    else:
        outs[0][...] = out

    @pl.when(i == last)
    def _():
        drain(1 - slot)


def _moe_combine(dest1, dest2, ys, route, x, mod_p, mod_s, pg, pb, l, rm, alpha, split):
    n, d = x.shape
    tm = rm.t
    if split:
        n_p = rm.np_tiles * tm
        out_specs = [pl.BlockSpec((tm, d), lambda i, *_: (rm.prompt(i), 0)),
                     pl.BlockSpec((tm, d), lambda i, *_: (rm.sample(i), 0))]
        out_shape = [jax.ShapeDtypeStruct((n_p, d), F32), jax.ShapeDtypeStruct((n - n_p, d), F32)]
    else:
        out_specs = pl.BlockSpec((tm, d), lambda i, *_: (i, 0))
        out_shape = jax.ShapeDtypeStruct((n, d), F32)
    post = pl.BlockSpec((None, None, 1, d), lambda i, *_: (l, 1, 0, 0))
    grid_spec = pltpu.PrefetchScalarGridSpec(
        num_scalar_prefetch=2,
        grid=(n // tm,),
        in_specs=[pl.BlockSpec(memory_space=pl.ANY),
                  pl.BlockSpec((tm, LANES), lambda i, *_: (i, 0)),
                  pl.BlockSpec((tm, d), lambda i, *_: (i, 0)),
                  *rm.mod_specs(l, 5, d), post, post],
        out_specs=out_specs,
        scratch_shapes=[pltpu.VMEM((2, tm, d), F32), pltpu.VMEM((2, tm, d), F32),
                        pltpu.SemaphoreType.DMA((2,)), pltpu.SemaphoreType.DMA((2,))],
    )
    return pl.pallas_call(
        functools.partial(_combine_kernel, alpha=alpha, np_tiles=rm.np_tiles, split=split), grid_spec=grid_spec,
        out_shape=out_shape,
        compiler_params=_cparams("arbitrary"),
        name="moe_combine",
    )(dest1, dest2, ys, route, x, mod_p, mod_s, pg, pb)


def _moe_ffn(h, route, wg, wu, wd, x, mod_p, mod_s, pg, pb, l, lj, rm, tb, tf, alpha, split):
    n = h.shape[0]
    rank, tot = _moe_rank(route, rm.t)
    counts = tot[0, :N_EXPERTS].astype(jnp.int32)
    padded = (counts + tb - 1) // tb * tb
    pad_end = jnp.cumsum(padded)
    pad_start = pad_end - padded
    experts = jnp.arange(N_EXPERTS, dtype=jnp.int32)

    def slot(col):
        e = route[:, col].astype(jnp.int32)
        start = jnp.sum(jnp.where(e[:, None] == experts[None, :], pad_start[None, :], 0), axis=1)
        return start + rank[:, col].astype(jnp.int32)

    dest1, dest2 = slot(0), slot(1)
    n_blocks = -(-(2 * n + N_EXPERTS * (tb - 1)) // tb)
    blk_start = jnp.arange(n_blocks, dtype=jnp.int32) * tb
    blk_e = jnp.minimum(jnp.sum(pad_end[None, :] <= blk_start[:, None], axis=1), N_EXPERTS - 1).astype(jnp.int32)
    own_end = jnp.sum(jnp.where(blk_e[:, None] == experts[None, :], (pad_start + counts)[None, :], 0), axis=1)
    nvalid = jnp.clip(own_end - blk_start, 0, tb).astype(jnp.int32)
    nvalid = jnp.where(blk_start < pad_end[-1], nvalid, 0)
    per_blk = tb // MOE_SUB
    sub_start = jnp.arange(n_blocks * per_blk, dtype=jnp.int32) * MOE_SUB
    sub_room = jnp.repeat(blk_start + nvalid, per_blk) - sub_start
    zlist = jnp.where(sub_room < MOE_SUB, sub_start, -1).astype(jnp.int32)
    xs = _moe_dispatch(dest1, dest2, zlist, h, n_blocks * tb, rm.t)
    ys = _moe_experts(blk_e, nvalid, xs, wg, wu, wd, lj, tb, tf)
    return _moe_combine(dest1, dest2, ys, route, x, mod_p, mod_s, pg, pb, l, rm, alpha, split)


def _pack_w_in(w_in, b_in):
    depth, d, _ = w_in.shape
    q_end = 2 * D_CONV + N_HEADS * HEAD_DIM
    k_end = q_end + N_KV * HEAD_DIM
    a_end = k_end + N_KV * HEAD_DIM
    m_end = a_end + 4 * M_HEADS * M_DK
    if_end = m_end + 2 * M_HEADS

    def pack(a):
        return jnp.concatenate([a[..., if_end:], a[..., :q_end], a[..., a_end:m_end], a[..., q_end:a_end]], axis=-1)

    def gates(a):
        return jnp.pad(a[..., m_end:if_end], [(0, 0)] * (a.ndim - 1) + [(0, LANES - 2 * M_HEADS)])

    return (pack(w_in).astype(BF16), pack(b_in).reshape(depth, 1, Z_W),
            gates(w_in).astype(BF16), gates(b_in).reshape(depth, 1, LANES))


def kernel(x_prompt, x_sample, cache_swa_k, cache_swa_v, state_conv, state_mlstm_C, state_mlstm_n, state_mlstm_m, c_prompt, c_sample, w_ada, b_ada, w_in, b_in, conv_w, conv_b, conv_ln_g, conv_ln_b, w_conv_out, attn_sinks, rel_bias, w_attn_out, m_norm_g, w_m_out, w_out, post_ln_g, post_ln_b, ffn_w_gate, ffn_w_up, ffn_w_down, router_w, router_b, moe_w_gate, moe_w_up, moe_w_down):
    bp, tp, d = x_prompt.shape
    bs, ts, _ = x_sample.shape
    depth = w_ada.shape[0]
    alpha = (2 * depth) ** 0.25
    n_p, n_s = bp * tp, bs * ts
    tm = n_s
    assert d == D_MODEL and tp % tm == 0 and tp % WINDOW == 0 and tm % 32 == 0
    rm = _RowMap(tm, bp, tp, n_p, n_s)
    rm_half = _RowMap(tm // 2, bp, tp, n_p, n_s)
    wb = cache_swa_k.shape[2]
    big = n_p >= 4096
    tc = 512 if big else tm
    lm = 256 if big else min(tp, 128)
    tb = 1024 if big else 2 * MOE_SUB
    bs_blk = 32 if bs % 32 == 0 else bs
    bs_att = 16 if bs % 16 == 0 else bs
    bs_m = 4 if bs % 4 == 0 else 1
    lts = 16

    n = n_p + n_s
    x = (x_prompt.reshape(n_p, d), jnp.transpose(x_sample, (1, 0, 2)).reshape(n_s, d))

    nc_rows = -(-(bp + bs) // 8) * 8
    c_all = jnp.zeros((nc_rows, d), F32).at[:bp].set(c_prompt).at[bp:bp + bs].set(c_sample)
    mod = _ada_mod(c_all, w_ada, b_ada)
    mod_p = mod[:, :bp].reshape(depth, bp, 1, 6 * d)
    mod_s = jnp.tile(mod[:, bp:bp + bs], (1, ts, 1))

    w_in_p, b_in_p, w_if, b_if = _pack_w_in(w_in, b_in)
    wc_b, wa_b, wm_b, wo_b = (w.astype(BF16) for w in (w_conv_out, w_attn_out, w_m_out, w_out))
    fg_b, fu_b, fd_b = (w.astype(BF16) for w in (ffn_w_gate, ffn_w_up, ffn_w_down))
    cw_pad = jnp.pad(conv_w, ((0, 0), (0, CONV_PAD - CONV_W), (0, 0)))
    cvecs = [v.reshape(depth, 1, D_CONV) for v in (conv_b, conv_ln_g, conv_ln_b)]
    rw_pad = jnp.pad(router_w, ((0, 0), (0, 0), (0, LANES - N_EXPERTS)))
    rb_pad = jnp.pad(router_b, ((0, 0), (0, LANES - N_EXPERTS)), constant_values=NEG_INF)[:, None, :]
    pg = post_ln_g.reshape(depth, 2, 1, d)
    pb = post_ln_b.reshape(depth, 2, 1, d)
    gamma = m_norm_g.reshape(depth, 1, M_HEADS * M_DV)
    sinks = attn_sinks.astype(F32)

    qi = jnp.arange(WINDOW)[:, None]
    kj = jnp.arange(2 * WINDOW)[None, :]
    dist_p = qi + WINDOW - kj
    bh = _bias_heads(rel_bias, dist_p, (dist_p >= 0) & (dist_p < WINDOW))
    bias_p = bh.reshape(N_HEADS // 2, 2, WINDOW, 2 * WINDOW).transpose(0, 2, 1, 3).reshape(
        N_HEADS // 2, WINDOW, 4 * WINDOW)
    dist_s = jnp.arange(ts)[:, None] + wb - jnp.arange(wb + ts)[None, :]
    bias_s = _bias_heads(rel_bias, dist_s, (dist_s >= 0) & (dist_s < WINDOW)).reshape(
        N_KV, Q_PER_KV * ts, wb + ts)
    kc_all = cache_swa_k.reshape(depth, bs, wb, N_KV * HEAD_DIM)
    vc_all = cache_swa_v.reshape(depth, bs, wb, N_KV * HEAD_DIM)

    f_moe = moe_w_gate.shape[3]
    tf_moe = 512 if f_moe % 512 == 0 else f_moe

    n0_all = state_mlstm_n[:, :, :, None, :]
    m0_all = _m_state_in(state_mlstm_m)
    s_c = s_conv = None
    new_p = [[] for _ in range(6)]
    new_s = [[] for _ in range(6)]
    for l in range(depth):
        j = l // 2
        xp, xs = x if isinstance(x, tuple) else (x, x)
        z, zif = _ln_proj(xp, xs, mod_p, mod_s, w_in_p, b_in_p, w_if, b_if, l, rm, n)
        zs3 = z[n_p:].reshape(ts, bs, Z_W)

        cp, ns_p = _conv_prompt(z, jnp.zeros((bp, CONV_PAD, D_CONV), F32), cw_pad, *cvecs, l, bp, tp, tc)
        cs3, s_conv = _conv_sample(zs3, state_conv, cw_pad, *cvecs, s_conv, l, bs_blk)
        new_p[2].append(ns_p[:, CONV_PAD - CONV_W + 1:])

        sink_h = sinks[l].reshape(N_KV, Q_PER_KV, 1)
        sink_s = jnp.broadcast_to(sink_h, (N_KV, Q_PER_KV, ts)).reshape(N_KV, Q_PER_KV * ts, 1)
        ap = _attn_prompt(z, bias_p * LOG2E, sinks * LOG2E, l, bp, tp)
        nk = min(WINDOW, tp)
        kv_tail = jnp.stack([z[(b + 1) * tp - nk:(b + 1) * tp, Z_K:Z_K + 2 * N_KV * HEAD_DIM]
                             for b in range(bp)]).astype(F32)
        new_p[0].append(kv_tail[..., :N_KV * HEAD_DIM].reshape(bp, nk, N_KV, HEAD_DIM))
        new_p[1].append(kv_tail[..., N_KV * HEAD_DIM:].reshape(bp, nk, N_KV, HEAD_DIM))
        q_s = zs3[:, :, Z_Q:Z_Q + N_HEADS * HEAD_DIM].reshape(ts, bs, N_KV, Q_PER_KV, HEAD_DIM)
        q4 = jnp.transpose(q_s, (1, 2, 3, 0, 4)).reshape(bs, N_KV, Q_PER_KV * ts, HEAD_DIM)
        k_s = jnp.transpose(zs3[:, :, Z_K:Z_K + N_KV * HEAD_DIM].reshape(ts, bs, N_KV, HEAD_DIM), (1, 0, 2, 3))
        v_s = jnp.transpose(zs3[:, :, Z_V:Z_V + N_KV * HEAD_DIM].reshape(ts, bs, N_KV, HEAD_DIM), (1, 0, 2, 3))
        o4 = _attn_sample(q4, kc_all, vc_all, jnp.transpose(k_s, (0, 2, 1, 3)), jnp.transpose(v_s, (0, 2, 1, 3)),
                          bias_s, sink_s, l, bs_att)
        as_ = jnp.transpose(o4.reshape(bs, N_KV, Q_PER_KV, ts, HEAD_DIM), (3, 0, 1, 2, 4)).reshape(n_s, -1).astype(BF16)
        new_s[0].append(k_s.astype(F32))
        new_s[1].append(v_s.astype(F32))

        ncp = tp // lm
        if_p = zif[:n_p, :2 * M_HEADS].reshape(bp * ncp, lm, 2 * M_HEADS)
        mp, c1p, n1p, m1p = _mlstm_prompt(z, zif, jnp.transpose(if_p, (0, 2, 1)), gamma, l, bp, ncp, lm)
        new_p[3].append(c1p)
        new_p[4].append(n1p[:, :, 0])
        new_p[5].append(m1p[:, :M_HEADS, 0])
        tpad = ((0, 0), (0, lts - ts), (0, 0))
        zm3 = jnp.pad(jnp.transpose(zs3[:, :, Z_MQ:Z_K], (1, 0, 2)), tpad)
        if3 = jnp.pad(jnp.transpose(zif[n_p:].reshape(ts, bs, LANES), (1, 0, 2)), tpad)
        ms, s_c, n1s, m1s = _mlstm_sample(zm3, if3, jnp.transpose(if3[:, :, :2 * M_HEADS], (0, 2, 1)), gamma,
                                          state_mlstm_C, n0_all[l], m0_all[l], s_c, l, ts, bs_m)
        ms = jnp.transpose(ms[:, :ts], (1, 0, 2)).reshape(n_s, -1)
        new_s[4].append(n1s[:, :, 0])
        new_s[5].append(m1s[:, :M_HEADS, 0])

        moe = l % 2 == 1
        router = (rw_pad, rb_pad, j) if moe else None
        outs = _merge(cp, cs3.reshape(n_s, D_CONV), ap, as_, mp, ms, z, xp, xs, mod_p, mod_s,
                      wc_b, wa_b, wm_b, wo_b, pg, pb, l, rm, alpha, router)
        if moe:
            x1, h2, route = outs
            x = _moe_ffn(h2, route, moe_w_gate, moe_w_up, moe_w_down, x1, mod_p, mod_s, pg, pb,
                         l, j, rm, tb, tf_moe, alpha, split=l == depth - 1)
        else:
            x1, h2 = outs
            x = _ffn_dense(h2, fg_b, fu_b, fd_b, x1, mod_p, mod_s, pg, pb, l, j, rm, alpha)

    x_p, x_s = x if isinstance(x, (list, tuple)) else (x[:n_p], x[n_p:])
    y_p = x_p.reshape(bp, tp, d)
    y_s = jnp.transpose(x_s.reshape(ts, bs, d), (1, 0, 2))
    p_k, p_v, p_conv, p_c, p_n, p_m = [jnp.stack(a) for a in new_p]
    s_k, s_v = [jnp.concatenate([old[:, :, ts:], jnp.stack(new)], axis=2)
                for old, new in zip((cache_swa_k, cache_swa_v), new_s[:2])]
    s_n, s_m = jnp.stack(new_s[4]), jnp.stack(new_s[5])
    return (y_p, y_s, p_k, p_v, p_conv, p_c, p_n, p_m, s_k, s_v, s_conv, s_c, s_n, s_m)
```

```python
import functools
import math

import jax
import jax.numpy as jnp
from jax import lax
from jax.experimental import pallas as pl
from jax.experimental.pallas import tpu as pltpu

F32 = jnp.float32
BF16 = jnp.bfloat16

D_MODEL = 1024
D_CONV = 512
CONV_W = 31
CONV_PAD = 32
N_HEADS = 8
N_KV = 2
HEAD_DIM = 64
Q_PER_KV = N_HEADS // N_KV
WINDOW = 128
N_BUCKETS = 32
MAX_DIST = 128
M_HEADS = 4
M_DK = 128
M_DV = 128
N_EXPERTS = 8
LN_EPS = 1e-5
LANES = 128
NEG_INF = float("-inf")
LOG2E = math.log2(math.e)
VMEM_LIMIT = 56 * 1024 * 1024

Z_G, Z_UA, Z_UB, Z_Q = 0, 3072, 3584, 4096
Z_MQ, Z_MK, Z_MV, Z_MO = 4608, 5120, 5632, 6144
Z_K, Z_V, Z_W = 6656, 6784, 6912
TN_IN = 6912
ATT_QB = 4
ATT_GROUP = 8
MOE_SUB = 256
CONV_CHUNK = 32

def _cparams(*sem):
    return pltpu.CompilerParams(dimension_semantics=sem, vmem_limit_bytes=VMEM_LIMIT)


def _sigmoid(x):
    return 1.0 / (1.0 + jnp.exp(-x))


def _silu(x):
    return x * _sigmoid(x)


def _log_sigmoid(x):
    return jnp.minimum(x, 0.0) - jnp.log(1.0 + jnp.exp(-jnp.abs(x)))


def _norm(x):
    mu = jnp.mean(x, axis=-1, keepdims=True)
    xc = x - mu
    var = jnp.mean(xc * xc, axis=-1, keepdims=True)
    return xc * lax.rsqrt(var + LN_EPS)


def _dot(a, b):
    return jnp.dot(a, b, preferred_element_type=F32)


def _dot_nt(a, b):
    return lax.dot_general(a, b, (((1,), (1,)), ((), ())), preferred_element_type=F32)


def _dot_tn(a, b):
    return lax.dot_general(a, b, (((0,), (0,)), ((), ())), preferred_element_type=F32)


def _dot_hi(a, b):
    return jnp.dot(a, b, preferred_element_type=F32, precision=lax.Precision.HIGHEST)


def _split_bf16(a):
    hi = a.astype(BF16)
    return hi, (a - hi.astype(F32)).astype(BF16)


_RARELY = pl.Buffered(1)


class _RowMap:
    def __init__(self, t, bp, tp, n_p, n_s):
        self.t = t
        self.bp = bp
        self.per_seq = tp // t
        self.np_tiles = n_p // t
        self.ns_tiles = n_s // t

    def seq(self, i):
        return jnp.minimum(i // self.per_seq, self.bp - 1)

    def prompt(self, i):
        return jnp.minimum(i, self.np_tiles - 1)

    def sample(self, i):
        return jnp.clip(i - self.np_tiles, 0, self.ns_tiles - 1)

    def mod_specs(self, l, k, d):
        return (pl.BlockSpec((None, None, 1, d), lambda i, *_: (l, self.seq(i), 0, k)),
                pl.BlockSpec((None, self.t, d), lambda i, *_: (l, self.sample(i), k), pipeline_mode=_RARELY))

    def x_specs(self, xs_arr, d):
        off = self.np_tiles if xs_arr.shape[0] > self.ns_tiles * self.t else 0
        return (pl.BlockSpec((self.t, d), lambda i, *_: (self.prompt(i), 0)),
                pl.BlockSpec((self.t, d), lambda i, *_: (off + self.sample(i), 0), pipeline_mode=_RARELY))


def _pick(is_s, p_ref, s_ref):
    return jnp.where(is_s, s_ref[...], p_ref[...])


def _ada_kernel(c_ref, w_ref, b_ref, o_ref):
    s = _silu(c_ref[...]).astype(BF16)
    o_ref[0] = _dot(s, w_ref[0].astype(BF16)) + b_ref[0]


def _ada_mod(c_all, w_ada, b_ada):
    depth, d, n6 = w_ada.shape
    rows = c_all.shape[0]
    return pl.pallas_call(
        _ada_kernel,
        grid=(depth, n6 // d),
        in_specs=[pl.BlockSpec((rows, d), lambda l, j: (0, 0)),
                  pl.BlockSpec((1, d, d), lambda l, j: (l, 0, j)),
                  pl.BlockSpec((1, 1, d), lambda l, j: (l, 0, j))],
        out_specs=pl.BlockSpec((1, rows, d), lambda l, j: (l, 0, j)),
        out_shape=jax.ShapeDtypeStruct((depth, rows, n6), F32),
        compiler_params=_cparams("parallel", "parallel"),
        name="ada_mod",
    )(c_all, w_ada, b_ada.reshape(depth, 1, n6))


def _ln_proj_kernel(xp_ref, xs_ref, shp_ref, shs_ref, scp_ref, scs_ref, w_ref, b_ref, wif_ref, bif_ref,
                    z_ref, zif_ref, h_scr, *, np_tiles):
    is_s = pl.program_id(0) >= np_tiles
    first = pl.program_id(1) == 0

    def prologue(x_ref, sh_ref, sc_ref):
        h = (_norm(x_ref[...]) * (1.0 + sc_ref[...]) + sh_ref[...]).astype(BF16)
        h_scr[...] = h
        zif_ref[...] = _dot(h, wif_ref[...]) + bif_ref[...]

    @pl.when(jnp.logical_and(first, is_s))
    def _():
        prologue(xs_ref, shs_ref, scs_ref)

    @pl.when(jnp.logical_and(first, jnp.logical_not(is_s)))
    def _():
        prologue(xp_ref, shp_ref, scp_ref)

    z_ref[...] = (_dot(h_scr[...], w_ref[...]) + b_ref[...]).astype(BF16)


def _ln_proj(xp, xs, mod_p, mod_s, w, b, wif, bif, l, rm, n):
    d = xp.shape[1]
    zw = w.shape[2]
    tm = rm.t
    shp, shs = rm.mod_specs(l, 0, d)
    scp, scs = rm.mod_specs(l, 1, d)
    wmode = _RARELY if zw == TN_IN else None
    return pl.pallas_call(
        functools.partial(_ln_proj_kernel, np_tiles=rm.np_tiles),
        grid=(n // tm, zw // TN_IN),
        in_specs=[*rm.x_specs(xs, d),
                  shp, shs, scp, scs,
                  pl.BlockSpec((None, d, TN_IN), lambda i, j: (l, 0, j), pipeline_mode=wmode),
                  pl.BlockSpec((None, 1, TN_IN), lambda i, j: (l, 0, j), pipeline_mode=wmode),
                  pl.BlockSpec((None, d, LANES), lambda i, j: (l, 0, 0), pipeline_mode=_RARELY),
                  pl.BlockSpec((None, 1, LANES), lambda i, j: (l, 0, 0), pipeline_mode=_RARELY)],
        out_specs=[pl.BlockSpec((tm, TN_IN), lambda i, j: (i, j)),
                   pl.BlockSpec((tm, LANES), lambda i, j: (i, 0))],
        out_shape=[jax.ShapeDtypeStruct((n, zw), BF16), jax.ShapeDtypeStruct((n, LANES), F32)],
        scratch_shapes=[pltpu.VMEM((tm, d), BF16)],
        compiler_params=_cparams("parallel", "arbitrary"),
        name="ln_proj",
    )(xp, xs, mod_p, mod_s, mod_p, mod_s, w, b, wif, bif)


def _conv_tail(yc, g_ref, b_ref):
    y = _norm(yc) * g_ref[...] + b_ref[...]
    return _silu(y).astype(BF16)


def _conv_prompt_kernel(ua_ref, ub_ref, st_ref, cw_ref, cb_ref, g_ref, b_ref, o_ref, ns_ref,
                        ext, shifted, yc, wrep):
    t = pl.program_id(1)
    tc = ua_ref.shape[0]
    sub = 8

    @pl.when(t == 0)
    def _():
        ext[0:CONV_PAD, :] = st_ref[0]

    @pl.when(t > 0)
    def _():
        ext[0:CONV_PAD, :] = ext[tc:tc + CONV_PAD, :]

    ext[CONV_PAD:, :] = ua_ref[...].astype(F32) * _sigmoid(ub_ref[...].astype(F32))
    for s in range(1, sub):
        shifted[s - 1] = ext[s:s + tc + CONV_PAD - sub, :]
    off = CONV_PAD - (CONV_W - 1)
    for w in range(CONV_W):
        wrep[w] = jnp.broadcast_to(cw_ref[w:w + 1, :], (sub, D_CONV))
    groups = CONV_CHUNK // sub

    for r0 in range(0, tc, CONV_CHUNK):
        acc = jnp.broadcast_to(cb_ref[...].reshape(1, 1, D_CONV), (groups, sub, D_CONV))
        for w in range(CONV_W):
            base, s = (off + w) // sub * sub, (off + w) % sub
            src = ext if s == 0 else shifted.at[s - 1]
            win = src[r0 + base:r0 + base + CONV_CHUNK, :]
            acc = acc + win.reshape(groups, sub, D_CONV) * wrep[w][None]
        yc[r0:r0 + CONV_CHUNK, :] = acc.reshape(CONV_CHUNK, D_CONV)
    o_ref[...] = _conv_tail(yc[...], g_ref, b_ref)

    @pl.when(t == pl.num_programs(1) - 1)
    def _():
        ns_ref[0] = ext[tc:tc + CONV_PAD, :]


def _conv_vec_specs(l, nargs):
    return [pl.BlockSpec((None, 1, D_CONV), lambda *_: (l, 0, 0)) for _ in range(nargs)]


def _conv_prompt(z, state_pad, cw, cb, g, b, l, bp, tp, tc):
    nt = tp // tc
    return pl.pallas_call(
        _conv_prompt_kernel,
        grid=(bp, nt),
        in_specs=[pl.BlockSpec((tc, D_CONV), lambda bb, t: (bb * nt + t, Z_UA // D_CONV)),
                  pl.BlockSpec((tc, D_CONV), lambda bb, t: (bb * nt + t, Z_UB // D_CONV)),
                  pl.BlockSpec((1, CONV_PAD, D_CONV), lambda bb, t: (bb, 0, 0)),
                  pl.BlockSpec((None, CONV_PAD, D_CONV), lambda bb, t: (l, 0, 0))] + _conv_vec_specs(l, 3),
        out_specs=[pl.BlockSpec((tc, D_CONV), lambda bb, t: (bb * nt + t, 0)),
                   pl.BlockSpec((1, CONV_PAD, D_CONV), lambda bb, t: (bb, 0, 0))],
        out_shape=[jax.ShapeDtypeStruct((bp * tp, D_CONV), BF16),
                   jax.ShapeDtypeStruct((bp, CONV_PAD, D_CONV), F32)],
        scratch_shapes=[pltpu.VMEM((tc + CONV_PAD, D_CONV), F32),
                        pltpu.VMEM((7, tc + CONV_PAD - 8, D_CONV), F32),
                        pltpu.VMEM((tc, D_CONV), F32),
                        pltpu.VMEM((CONV_PAD, 8, D_CONV), F32)],
        compiler_params=_cparams("parallel", "arbitrary"),
        name="conv_prompt",
    )(z, z, state_pad, cw, cb, g, b)


def _conv_sample_kernel(ua_ref, ub_ref, st_ref, cw_ref, cb_ref, g_ref, b_ref, *rest, first_layer):
    o_ref, ns_ref = rest[-2:]
    ts = ua_ref.shape[0]
    ns = CONV_W - 1
    a = ua_ref[...].astype(F32) * _sigmoid(ub_ref[...].astype(F32))
    st = st_ref[...]
    if first_layer is not None:
        for dd in range(ns_ref.shape[0]):
            if dd != first_layer:
                ns_ref[dd] = jnp.zeros(ns_ref.shape[1:], F32)
        ns_ref = ns_ref.at[first_layer]
    ns_ref[:, 0:ns - ts, :] = st[:, ts:, :]
    for t in range(ts):
        ns_ref[:, ns - ts + t, :] = a[t]
    row = lax.broadcasted_iota(jnp.int32, (ns, D_CONV), 0)
    for t in range(ts):
        wt = jnp.zeros((ns, D_CONV), F32)
        for j in range(t, ns):
            wt = jnp.where(row == j, cw_ref[j - t:j - t + 1, :], wt)
        yc = jnp.sum(st * wt[None], axis=1) + cb_ref[...]
        for t2 in range(t + 1):
            wi = CONV_W - 1 - (t - t2)
            yc = yc + a[t2] * cw_ref[wi:wi + 1, :]
        o_ref[t] = _conv_tail(yc, g_ref, b_ref)


def _conv_sample(zs3, state, cw, cb, g, b, ns_prev, l, bs_blk):
    ts, bs, _ = zs3.shape
    depth = state.shape[0]
    ns = CONV_W - 1
    in_specs = [pl.BlockSpec((ts, bs_blk, D_CONV), lambda i: (0, i, Z_UA // D_CONV)),
                pl.BlockSpec((ts, bs_blk, D_CONV), lambda i: (0, i, Z_UB // D_CONV)),
                pl.BlockSpec((None, bs_blk, ns, D_CONV), lambda i: (l, i, 0, 0)),
                pl.BlockSpec((None, CONV_PAD, D_CONV), lambda i: (l, 0, 0))] + _conv_vec_specs(l, 3)
    args = [zs3, zs3, state, cw, cb, g, b]
    aliases = {}
    if ns_prev is None:
        ns_spec = pl.BlockSpec((depth, bs_blk, ns, D_CONV), lambda i: (0, i, 0, 0))
    else:
        in_specs.append(pl.BlockSpec(memory_space=pl.ANY))
        args.append(ns_prev)
        aliases = {len(args) - 1: 1}
        ns_spec = pl.BlockSpec((None, bs_blk, ns, D_CONV), lambda i: (l, i, 0, 0))
    return pl.pallas_call(
        functools.partial(_conv_sample_kernel, first_layer=l if ns_prev is None else None),
        grid=(bs // bs_blk,),
        in_specs=in_specs,
        out_specs=[pl.BlockSpec((ts, bs_blk, D_CONV), lambda i: (0, i, 0)), ns_spec],
        out_shape=[jax.ShapeDtypeStruct((ts, bs, D_CONV), BF16),
                   jax.ShapeDtypeStruct((depth, bs, ns, D_CONV), F32)],
        input_output_aliases=aliases,
        compiler_params=_cparams("parallel"),
        name="conv_sample",
    )(*args)


def _t5_bucket(dist):
    max_exact = N_BUCKETS // 2
    d = jnp.maximum(dist, 0)
    large = max_exact + (jnp.log(jnp.maximum(d, 1).astype(F32) / max_exact)
                         / math.log(MAX_DIST / max_exact) * (N_BUCKETS - max_exact)).astype(jnp.int32)
    return jnp.where(d < max_exact, d, jnp.minimum(large, N_BUCKETS - 1))


def _bias_heads(rel_bias, dist, valid):
    onehot = (_t5_bucket(dist)[..., None] == jnp.arange(N_BUCKETS)).astype(F32)
    bias = jnp.einsum("qkb,bh->qkh", onehot, rel_bias.astype(F32), precision=lax.Precision.HIGHEST)
    bias = jnp.where(valid[..., None], bias, NEG_INF)
    return jnp.transpose(bias, (2, 0, 1))


def _attn_prompt_kernel(sink_ref, q_ref, kc_ref, kp_ref, vc_ref, vp_ref, bias_ref, o_ref, *, l):
    first = pl.program_id(1) == 0
    w = WINDOW
    nq = q_ref.shape[0] // w
    kall = jnp.concatenate([kp_ref[...], kc_ref[...]], axis=0).astype(F32)
    vall = jnp.concatenate([vp_ref[...], vc_ref[...]], axis=0).astype(F32)
    lane = lax.broadcasted_iota(jnp.int32, kall.shape, 1)
    lo = lane < HEAD_DIM
    kroll = pltpu.roll(kall, HEAD_DIM, 1)
    vroll = pltpu.roll(vall, HEAD_DIM, 1)

    def halves(a, aroll, g):
        if g == 0:
            return jnp.where(lo, a, 0.0).astype(BF16), jnp.where(lo, 0.0, aroll).astype(BF16)
        return jnp.where(lo, aroll, 0.0).astype(BF16), jnp.where(lo, 0.0, a).astype(BF16)

    kh = [halves(kall, kroll, g) for g in range(N_KV)]
    vh = [halves(vall, vroll, g) for g in range(N_KV)]
    col = lax.broadcasted_iota(jnp.int32, (w, 4 * w), 1)
    prev_col = (col % (2 * w)) < w
    tiles_per_g = Q_PER_KV // 2
    units = [(qi, tile) for qi in range(nq) for tile in range(N_HEADS // 2)]
    for u0 in range(0, len(units), ATT_GROUP):
        group = units[u0:u0 + ATT_GROUP]
        scores = []
        for qi, tile in group:
            r0, g = qi * w, tile // tiles_per_g
            q = q_ref[r0:r0 + w, tile * LANES:(tile + 1) * LANES]
            kk = jnp.concatenate([kh[g][0][r0:r0 + 2 * w], kh[g][1][r0:r0 + 2 * w]], axis=0)
            s = _dot_nt(q, kk) * (HEAD_DIM ** -0.5 * LOG2E) + bias_ref[tile]
            if qi == 0:
                s = jnp.where(jnp.logical_and(first, prev_col), NEG_INF, s)
            scores.append(s)
        probs = []
        for (qi, tile), s in zip(group, scores):
            ps = []
            for half in range(2):
                sh = s[:, half * 2 * w:(half + 1) * 2 * w]
                sink = sink_ref[l, 2 * tile + half]
                mx = jnp.maximum(jnp.max(sh, axis=-1, keepdims=True), sink)
                p = jnp.exp2(sh - mx)
                den = jnp.sum(p, axis=-1, keepdims=True) + jnp.exp2(sink - mx)
                ps.append((p * (1.0 / den)).astype(BF16))
            probs.append(jnp.concatenate(ps, axis=1))
        for (qi, tile), p in zip(group, probs):
            r0, g = qi * w, tile // tiles_per_g
            vv = jnp.concatenate([vh[g][0][r0:r0 + 2 * w], vh[g][1][r0:r0 + 2 * w]], axis=0)
            o_ref[r0:r0 + w, tile * LANES:(tile + 1) * LANES] = _dot(p, vv).astype(BF16)


def _attn_prompt(z, bias, sinks, l, bp, tp):
    w = WINDOW
    qb = ATT_QB if tp % (ATT_QB * w) == 0 else 1
    ns = tp // (qb * w)
    nb = tp // w
    kvw = N_KV * HEAD_DIM
    qw = N_HEADS * HEAD_DIM

    def cur(col):
        return lambda bb, i: (bb * ns + i, col)

    def prev(col):
        return lambda bb, i: (bb * nb + jnp.maximum(i * qb - 1, 0), col)

    return pl.pallas_call(
        functools.partial(_attn_prompt_kernel, l=l),
        grid=(bp, ns),
        in_specs=[pl.BlockSpec(memory_space=pltpu.SMEM),
                  pl.BlockSpec((qb * w, qw), cur(Z_Q // qw)),
                  pl.BlockSpec((qb * w, kvw), cur(Z_K // kvw)),
                  pl.BlockSpec((w, kvw), prev(Z_K // kvw)),
                  pl.BlockSpec((qb * w, kvw), cur(Z_V // kvw)),
                  pl.BlockSpec((w, kvw), prev(Z_V // kvw)),
                  pl.BlockSpec((N_HEADS // 2, w, 4 * w), lambda bb, i: (0, 0, 0))],
        out_specs=pl.BlockSpec((qb * w, qw), lambda bb, i: (bb * ns + i, 0)),
        out_shape=jax.ShapeDtypeStruct((bp * tp, qw), BF16),
        compiler_params=_cparams("parallel", "parallel"),
        name="attn_prompt",
    )(sinks, z, z, z, z, z, bias)


def _attn_sample_kernel(q_ref, kc_ref, vc_ref, kn_ref, vn_ref, bias_ref, sink_ref, kt_ref, vt_ref, *rest,
                        first_layer):
    o_ref, nk_ref, nv_ref = rest[-3:]
    wb = kc_ref.shape[1]
    ts = kn_ref.shape[2]
    for new_ref, old_ref, t_ref in ((nk_ref, kc_ref, kt_ref), (nv_ref, vc_ref, vt_ref)):
        if first_layer is not None:
            for dd in range(new_ref.shape[0]):
                if dd != first_layer:
                    new_ref[dd] = jnp.zeros(new_ref.shape[1:], F32)
            new_ref = new_ref.at[first_layer]
        new_ref[:, 0:wb - ts, :] = old_ref[:, ts:, :]
        for t in range(ts):
            new_ref[:, wb - ts + t, :] = t_ref[t].astype(F32)
    for g in range(N_KV):
        lo = g * HEAD_DIM
        qb = (q_ref[:, g].astype(F32) * (HEAD_DIM ** -0.5)).astype(BF16)
        kc = kc_ref[:, :, lo:lo + HEAD_DIM].astype(BF16)
        vc = vc_ref[:, :, lo:lo + HEAD_DIM].astype(BF16)
        kn = kn_ref[:, g].astype(F32)
        vn = vn_ref[:, g].astype(F32)
        bias = bias_ref[g]
        s_c = jnp.einsum("bqd,bkd->bqk", qb, kc, preferred_element_type=F32) + bias[None, :, :wb]
        qf = qb.astype(F32)
        s_n = [jnp.sum(qf * kn[:, j:j + 1, :], axis=-1, keepdims=True) + bias[None, :, wb + j:wb + j + 1]
               for j in range(ts)]
        sink = sink_ref[g][None]
        mx = jnp.maximum(jnp.max(s_c, axis=-1, keepdims=True), sink)
        for sj in s_n:
            mx = jnp.maximum(mx, sj)
        p_c = jnp.exp(s_c - mx)
        p_n = [jnp.exp(sj - mx) for sj in s_n]
        den = jnp.sum(p_c, axis=-1, keepdims=True) + jnp.exp(sink - mx)
        for pj in p_n:
            den = den + pj
        o = jnp.einsum("bqk,bkd->bqd", (p_c / den).astype(BF16), vc, preferred_element_type=F32)
        for j in range(ts):
            o = o + (p_n[j] / den).astype(BF16).astype(F32) * vn[:, j:j + 1, :]
        o_ref[:, g] = o


def _attn_sample(q4, kc, vc, kn, vn, bias, sinks, kt, vt, new_prev, l, bs_blk):
    bs, _, rt, _ = q4.shape
    depth, _, wb, kvw = kc.shape
    ts = kn.shape[2]
    tmaj = pl.BlockSpec((ts, bs_blk, kvw), lambda i: (0, i, 0))
    in_specs = [pl.BlockSpec((bs_blk, N_KV, rt, HEAD_DIM), lambda i: (i, 0, 0, 0)),
                pl.BlockSpec((None, bs_blk, wb, kvw), lambda i: (l, i, 0, 0)),
                pl.BlockSpec((None, bs_blk, wb, kvw), lambda i: (l, i, 0, 0)),
                pl.BlockSpec((bs_blk, N_KV, ts, HEAD_DIM), lambda i: (i, 0, 0, 0)),
                pl.BlockSpec((bs_blk, N_KV, ts, HEAD_DIM), lambda i: (i, 0, 0, 0)),
                pl.BlockSpec((N_KV, rt, wb + ts), lambda i: (0, 0, 0)),
                pl.BlockSpec((N_KV, rt, 1), lambda i: (0, 0, 0)),
                tmaj, tmaj]
    args = [q4, kc, vc, kn, vn, bias, sinks, kt, vt]
    aliases = {}
    if new_prev is None:
        new_spec = pl.BlockSpec((depth, bs_blk, wb, kvw), lambda i: (0, i, 0, 0))
    else:
        in_specs += [pl.BlockSpec(memory_space=pl.ANY)] * 2
        args += list(new_prev)
        aliases = {len(args) - 2: 1, len(args) - 1: 2}
        new_spec = pl.BlockSpec((None, bs_blk, wb, kvw), lambda i: (l, i, 0, 0))
    new_shape = jax.ShapeDtypeStruct((depth, bs, wb, kvw), F32)
    outs = pl.pallas_call(
        functools.partial(_attn_sample_kernel, first_layer=l if new_prev is None else None),
        grid=(bs // bs_blk,),
        in_specs=in_specs,
        out_specs=[pl.BlockSpec((bs_blk, N_KV, rt, HEAD_DIM), lambda i: (i, 0, 0, 0)), new_spec, new_spec],
        out_shape=[jax.ShapeDtypeStruct((bs, N_KV, rt, HEAD_DIM), F32), new_shape, new_shape],
        input_output_aliases=aliases,
        compiler_params=_cparams("parallel"),
        name="attn_sample",
    )(*args)
    return outs[0], (outs[1], outs[2])


def _mlstm_kernel(*refs, t_valid, nseq, aliased, carried, chunk_axis, per_seq_inputs=False, first_layer=None):
    if aliased:
        refs = refs[:10] + refs[11:]
    if per_seq_inputs:
        groups = [refs[6 * s:6 * s + 6] for s in range(nseq)]
        if_ref, ifr_ref, q_ref, k_ref, v_ref, o_ref = (tuple(g[k] for g in groups) for k in range(6))
        refs = (None,) * 6 + refs[6 * nseq:]
    else:
        if_ref, ifr_ref, q_ref, k_ref, v_ref, o_ref = refs[:6]
    g_ref, c0_ref, n0_ref, m0_ref, h_ref, c1_ref, n1_ref, m1_ref = refs[6:14]
    c = pl.program_id(chunk_axis)
    L = (ifr_ref[0] if per_seq_inputs else ifr_ref).shape[-1]
    if carried:
        c_in, n_in, m_in = c_out, n_out, m_out = refs[14:]

        @pl.when(c == 0)
        def _():
            c_in[...] = c0_ref[...]
            n_in[...] = n0_ref[...]
            m_in[...] = m0_ref[...]
    else:
        (c_in, n_in, m_in), (c_out, n_out, m_out) = (c0_ref, n0_ref, m0_ref), (c1_ref, n1_ref, m1_ref)
        if first_layer is not None:
            for dd in range(c1_ref.shape[0]):
                if dd != first_layer:
                    c1_ref[dd] = jnp.zeros(c1_ref.shape[1:], F32)
            c_out = c1_ref.at[first_layer]

    def seq(ref, s):
        return ref[s] if isinstance(ref, tuple) else ref.at[s]

    tt = lax.broadcasted_iota(jnp.int32, (L, L), 0)
    ss = lax.broadcasted_iota(jnp.int32, (L, L), 1)
    causal = ss <= tt
    tril = causal.astype(F32)
    triu = (tt <= ss).astype(F32)
    seqs = range(nseq)
    heads = [(s_i, h) for s_i in seqs for h in range(M_HEADS)]
    mxu_sums = L % LANES == 0
    gates = []
    for s_i in seqs:
        ifc = seq(if_ref, s_i)[...]
        ifr = ifr_ref[s_i][0] if per_seq_inputs else ifr_ref[s_i]
        lf_c = _log_sigmoid(ifc)
        lf_r = _log_sigmoid(ifr)
        i_c, i_r = ifc, ifr
        if t_valid < L:
            rc = lax.broadcasted_iota(jnp.int32, (L, LANES), 0) < t_valid
            rr = lax.broadcasted_iota(jnp.int32, (2 * M_HEADS, L), 1) < t_valid
            lf_c = jnp.where(rc, lf_c, 0.0)
            lf_r = jnp.where(rr, lf_r, 0.0)
            i_c = jnp.where(rc, i_c, NEG_INF)
            i_r = jnp.where(rr, i_r, NEG_INF)
        gates.append((lf_c, lf_r, i_c, i_r))
    f_cs = [_dot_hi(tril, g[0]) for g in gates]
    f_rs = [_dot_hi(g[1], triu) for g in gates]
    state = {(s_i, h): (c_in[s_i, h], n_in[s_i, h], m_in[s_i, h:h + 1, :]) for s_i, h in heads}

    def wide(col):
        return jnp.concatenate([col] * (L // LANES), axis=1) if L >= LANES else col[:, :L]

    st1 = {}
    for s_i, h in heads:
        _, nrow, m0 = state[s_i, h]
        fc = jnp.broadcast_to(f_cs[s_i][:, M_HEADS + h:M_HEADS + h + 1], (L, LANES))
        fr = f_rs[s_i][M_HEADS + h:M_HEADS + h + 1, :]
        ir = gates[s_i][3][h:h + 1, :]
        dm = jnp.where(causal, wide(fc) - fr + ir, NEG_INF)
        m_t = jnp.maximum(m0 + fc, jnp.max(dm, axis=-1, keepdims=True))
        st1[s_i, h] = (fc, dm, m_t, jnp.exp(m0 + fc - m_t))
    st2 = {}
    for s_i, h in heads:
        lo = h * M_DK
        fc, dm, m_t, inter = st1[s_i, h]
        qb = seq(q_ref, s_i)[:, lo:lo + M_DK]
        kf = seq(k_ref, s_i)[:, lo:lo + M_DK].astype(F32) * (M_DK ** -0.5)
        vf = seq(v_ref, s_i)[:, lo:lo + M_DV].astype(F32)
        kb, vb = kf.astype(BF16), vf.astype(BF16)
        if mxu_sums:
            n_rows = jnp.broadcast_to(state[s_i, h][1], (M_DK, M_DK)).astype(BF16)
            qk = _dot_nt(qb, jnp.concatenate([kb, n_rows], axis=0))
            sc, qn_rep = qk[:, :L] * jnp.exp(dm - wide(m_t)), qk[:, L:]
        else:
            sc, qn_rep = _dot_nt(qb, kb) * jnp.exp(dm - wide(m_t)), None
        st2[s_i, h] = (qb, kf, vf, kb, vb, sc, qn_rep)
    st3 = {}
    for s_i, h in heads:
        cm, nrow, m0 = state[s_i, h]
        fc, dm, m_t, inter = st1[s_i, h]
        qb, kf, vf, kb, vb, sc, qn_rep = st2[s_i, h]
        if mxu_sums:
            sv = _dot(sc.astype(BF16), jnp.concatenate([vb, jnp.ones((L, M_DV), BF16)], axis=1))
            num = inter * _dot(qb, cm.astype(BF16)) + sv[:, :M_DV]
            qn = inter * qn_rep + sv[:, M_DV:]
            floor = jnp.exp(-m_t)
        else:
            num = inter * _dot(qb, cm.astype(BF16)) + _dot(sc.astype(BF16), vb)
            qn = (inter * jnp.sum(qb.astype(F32) * nrow, axis=-1, keepdims=True)
                  + jnp.sum(sc, axis=-1, keepdims=True))
            floor = jnp.exp(-m_t)
        st3[s_i, h] = num / jnp.maximum(jnp.abs(qn), floor)
    new_state = {}
    for s_i, h in heads:
        cm, nrow, m0 = state[s_i, h]
        fc, dm, m_t, inter = st1[s_i, h]
        qb, kf, vf, kb, vb, sc, _ = st2[s_i, h]
        ic = jnp.broadcast_to(gates[s_i][2][:, h:h + 1], (L, LANES))
        m_end = m_t[L - 1:L, :]
        f_end = fc[L - 1:L, :]
        decay = jnp.exp(m0 + f_end - m_end)
        w_s = jnp.exp(f_end - fc + ic - m_end)
        new_state[s_i, h] = (decay * cm + _dot_tn(kb, (w_s * vf).astype(BF16)),
                             decay * nrow + jnp.sum(w_s * kf, axis=0, keepdims=True),
                             m_end)
    for s_i, h in heads:
        lo = h * M_DK
        hn = _norm(st3[s_i, h]) * g_ref[:, lo:lo + M_DV]
        gate = _sigmoid(seq(o_ref, s_i)[:, lo:lo + M_DV].astype(F32))
        seq(h_ref, s_i)[:, lo:lo + M_DV] = (gate * hn).astype(BF16)
    for s_i, h in heads:
        c_out[s_i, h], n_out[s_i, h], m_out[s_i, h:h + 1, :] = new_state[s_i, h]
    if not carried:
        for s_i in seqs:
            m_out[s_i, M_HEADS:, :] = jnp.zeros((M_HEADS, LANES), F32)

    if carried:
        @pl.when(c == pl.num_programs(chunk_axis) - 1)
        def _():
            c1_ref[...] = c_out[...]
            n1_ref[...] = n_out[...]
            m1_ref[...] = m_out[...]


def _mlstm_state_specs(nseq, l_state):
    if l_state is None:
        c_spec = pl.BlockSpec((nseq, M_HEADS, M_DK, M_DV), lambda b, c: (b, 0, 0, 0))
    else:
        c_spec = pl.BlockSpec((None, nseq, M_HEADS, M_DK, M_DV), lambda b, c: (l_state, b, 0, 0, 0))
    return (c_spec,
            pl.BlockSpec((nseq, M_HEADS, 1, M_DK), lambda b, c: (b, 0, 0, 0)),
            pl.BlockSpec((nseq, 2 * M_HEADS, LANES), lambda b, c: (b, 0, 0)))


def _mlstm_scratch(nseq):
    return [pltpu.VMEM((nseq, M_HEADS, M_DK, M_DV), F32),
            pltpu.VMEM((nseq, M_HEADS, 1, M_DK), F32),
            pltpu.VMEM((nseq, 2 * M_HEADS, LANES), F32)]


def _mlstm_prompt(z, zif, ifr, gamma, l, bp, nc, L):
    hw = M_HEADS * M_DK
    zero = lambda *s: jnp.zeros(s, F32)
    in_specs, args = [], []
    for b in range(bp):
        def rows(col, b=b):
            return lambda c: (b * nc + c, col)

        in_specs += [pl.BlockSpec((L, LANES), rows(0)),
                     pl.BlockSpec((1, 2 * M_HEADS, L), lambda c, b=b: (b * nc + c, 0, 0)),
                     pl.BlockSpec((L, hw), rows(Z_MQ // hw)),
                     pl.BlockSpec((L, hw), rows(Z_MK // hw)),
                     pl.BlockSpec((L, hw), rows(Z_MV // hw)),
                     pl.BlockSpec((L, hw), rows(Z_MO // hw))]
        args += [zif, ifr, z, z, z, z]
    state_specs = [pl.BlockSpec((bp, M_HEADS, M_DK, M_DV), lambda c: (0, 0, 0, 0)),
                   pl.BlockSpec((bp, M_HEADS, 1, M_DK), lambda c: (0, 0, 0, 0)),
                   pl.BlockSpec((bp, 2 * M_HEADS, LANES), lambda c: (0, 0, 0))]
    kern = functools.partial(_mlstm_kernel, t_valid=L, nseq=bp, aliased=False, carried=True, chunk_axis=0,
                             per_seq_inputs=True)
    outs = pl.pallas_call(
        kern,
        grid=(nc,),
        in_specs=in_specs + [pl.BlockSpec((None, 1, hw), lambda c: (l, 0, 0))] + state_specs,
        out_specs=[pl.BlockSpec((bp, L, hw), lambda c: (0, c, 0))] + state_specs,
        out_shape=[jax.ShapeDtypeStruct((bp, nc * L, hw), BF16),
                   jax.ShapeDtypeStruct((bp, M_HEADS, M_DK, M_DV), F32),
                   jax.ShapeDtypeStruct((bp, M_HEADS, 1, M_DK), F32),
                   jax.ShapeDtypeStruct((bp, 2 * M_HEADS, LANES), F32)],
        scratch_shapes=_mlstm_scratch(bp),
        compiler_params=_cparams("arbitrary"),
        name="mlstm_prompt",
    )(*args, gamma, zero(bp, M_HEADS, M_DK, M_DV), zero(bp, M_HEADS, 1, M_DK), zero(bp, 2 * M_HEADS, LANES))
    return (outs[0].reshape(bp * nc * L, hw),) + tuple(outs[1:])


def _mlstm_sample(zm3, if3, ifr, gamma, c_all, n0, m0x, c_out_prev, l, t_valid, nseq):
    bs, L, _ = zm3.shape
    depth = c_all.shape[0]
    hw = M_HEADS * M_DK
    aliased = c_out_prev is not None

    def blk(col):
        return pl.BlockSpec((nseq, L, hw), lambda b, c: (b, 0, col))

    c_in, n_spec, m_spec = _mlstm_state_specs(nseq, l)
    in_specs = [pl.BlockSpec((nseq, L, LANES), lambda b, c: (b, 0, 0)),
                pl.BlockSpec((nseq, 2 * M_HEADS, L), lambda b, c: (b, 0, 0)),
                blk(0), blk(1), blk(2), blk(3),
                pl.BlockSpec((None, 1, hw), lambda b, c: (l, 0, 0)),
                c_in, n_spec, m_spec]
    args = [if3, ifr, zm3, zm3, zm3, zm3, gamma, c_all, n0, m0x]
    aliases = {}
    if aliased:
        in_specs.append(pl.BlockSpec(memory_space=pl.ANY))
        args.append(c_out_prev)
        aliases = {len(args) - 1: 1}
        c_out = c_in
    else:
        c_out = pl.BlockSpec((depth, nseq, M_HEADS, M_DK, M_DV), lambda b, c: (0, b, 0, 0, 0))
    kern = functools.partial(_mlstm_kernel, t_valid=t_valid, nseq=nseq, aliased=aliased, carried=False,
                             chunk_axis=1, first_layer=None if aliased else l)
    return pl.pallas_call(
        kern,
        grid=(bs // nseq, 1),
        in_specs=in_specs,
        out_specs=[pl.BlockSpec((nseq, L, hw), lambda b, c: (b, 0, 0)), c_out, n_spec, m_spec],
        out_shape=[jax.ShapeDtypeStruct((bs, L, hw), BF16),
                   jax.ShapeDtypeStruct((depth, bs, M_HEADS, M_DK, M_DV), F32),
                   jax.ShapeDtypeStruct((bs, M_HEADS, 1, M_DK), F32),
                   jax.ShapeDtypeStruct((bs, 2 * M_HEADS, LANES), F32)],
        input_output_aliases=aliases,
        compiler_params=_cparams("parallel", "arbitrary"),
        name="mlstm_sample",
    )(*args)


def _m_state_in(m):
    lead = m.shape[:-1]
    mx = jnp.zeros(lead + (2 * M_HEADS, LANES), F32)
    return mx.at[..., :M_HEADS, :].set(jnp.broadcast_to(m[..., None], lead + (M_HEADS, LANES)))


def _merge_kernel(*refs, np_tiles, alpha, route):
    (cp_ref, cs_ref, ap_ref, as_ref, mp_ref, ms_ref, g0_ref, g1_ref, g2_ref, xp_ref, xs_ref,
     gtp_ref, gts_ref, shp_ref, shs_ref, scp_ref, scs_ref,
     wc_ref, wa_ref, wm_ref, wo_ref, pg_ref, pb_ref) = refs[:23]
    if route:
        rw_ref, rb_ref, x1_ref, h_ref, route_ref = refs[23:]
    else:
        x1_ref, h_ref = refs[23:]
    is_s = pl.program_id(0) >= np_tiles

    def gate(ref):
        return _sigmoid(ref[...].astype(F32))

    def run(c_ref, a_ref, m_ref, x_ref, gt_ref, sh_ref, sc_ref):
        y = (gate(g0_ref) * _dot(c_ref[...], wc_ref[...])
             + gate(g1_ref) * _dot(a_ref[...], wa_ref[...])
             + gate(g2_ref) * _dot(m_ref[...], wm_ref[...]))
        mix = _dot(y.astype(BF16), wo_ref[...])
        x1 = _norm(alpha * x_ref[...] + gt_ref[...] * mix) * pg_ref[...] + pb_ref[...]
        x1_ref[...] = x1
        h = _norm(x1) * (1.0 + sc_ref[...]) + sh_ref[...]
        h_ref[...] = h.astype(h_ref.dtype)
        if route:
            h_hi, h_lo = _split_bf16(h)
            w_hi, w_lo = _split_bf16(rw_ref[...])
            logits = _dot(h_hi, w_hi) + (_dot(h_hi, w_lo) + _dot(h_lo, w_hi)) + rb_ref[...]
            lane = lax.broadcasted_iota(jnp.int32, logits.shape, 1)
            m1 = jnp.max(logits, axis=-1, keepdims=True)
            e1 = jnp.min(jnp.where(logits == m1, lane, LANES), axis=-1, keepdims=True)
            l2 = jnp.where(lane == e1, NEG_INF, logits)
            m2 = jnp.max(l2, axis=-1, keepdims=True)
            e2 = jnp.min(jnp.where(l2 == m2, lane, LANES), axis=-1, keepdims=True)
            ex = jnp.exp(m2 - m1)
            w1 = 1.0 / (1.0 + ex)
            w2 = ex / (1.0 + ex)
            out = jnp.where(lane == 0, e1.astype(F32),
                            jnp.where(lane == 1, e2.astype(F32),
                                      jnp.where(lane == 2, w1, jnp.where(lane == 3, w2, 0.0))))
            route_ref[...] = out

    @pl.when(is_s)
    def _():
        run(cs_ref, as_ref, ms_ref, xs_ref, gts_ref, shs_ref, scs_ref)

    @pl.when(jnp.logical_not(is_s))
    def _():
        run(cp_ref, ap_ref, mp_ref, xp_ref, gtp_ref, shp_ref, scp_ref)


def _merge(cp, cs, ap, as_, mp, ms, z, xp, xs, mod_p, mod_s, wc, wa, wm, wo, pg, pb, l, rm, alpha, router):
    n, d = z.shape[0], xp.shape[1]
    hw = D_CONV
    route = router is not None
    tm = rm.t

    def pblk():
        return pl.BlockSpec((tm, hw), lambda i: (rm.prompt(i), 0))

    def sblk():
        return pl.BlockSpec((tm, hw), lambda i: (rm.sample(i), 0), pipeline_mode=_RARELY)

    def zg(k):
        return pl.BlockSpec((tm, d), lambda i: (i, Z_G // d + k))

    def lw(a):
        return pl.BlockSpec((None,) + a.shape[1:], lambda i: (l,) + (0,) * (a.ndim - 1), pipeline_mode=_RARELY)

    post = pl.BlockSpec((None, None, 1, d), lambda i: (l, 0, 0, 0))
    in_specs = [pblk(), sblk(), pblk(), sblk(), pblk(), sblk(), zg(0), zg(1), zg(2),
                *rm.x_specs(xs, d),
                *rm.mod_specs(l, 2, d), *rm.mod_specs(l, 3, d), *rm.mod_specs(l, 4, d),
                lw(wc), lw(wa), lw(wm), lw(wo), post, post]
    args = [cp, cs, ap, as_, mp, ms, z, z, z, xp, xs, mod_p, mod_s, mod_p, mod_s, mod_p, mod_s,
            wc, wa, wm, wo, pg, pb]
    out_specs = [pl.BlockSpec((tm, d), lambda i: (i, 0)), pl.BlockSpec((tm, d), lambda i: (i, 0))]
    out_shape = [jax.ShapeDtypeStruct((n, d), F32), jax.ShapeDtypeStruct((n, d), F32 if route else BF16)]
    if route:
        rw, rb, lj = router
        in_specs += [pl.BlockSpec((None, d, LANES), lambda i: (lj, 0, 0)),
                     pl.BlockSpec((None, 1, LANES), lambda i: (lj, 0, 0))]
        args += [rw, rb]
        out_specs.append(pl.BlockSpec((tm, LANES), lambda i: (i, 0)))
        out_shape.append(jax.ShapeDtypeStruct((n, LANES), F32))
    kern = functools.partial(_merge_kernel, np_tiles=rm.np_tiles, alpha=alpha, route=route)
    return pl.pallas_call(
        kern, grid=(n // tm,), in_specs=in_specs, out_specs=out_specs, out_shape=out_shape,
        compiler_params=_cparams("parallel"), name="merge_route" if route else "merge",
    )(*args)


def _post_residual(x_ref, gp_ref, gs_ref, f, pg_ref, pb_ref, is_s, alpha):
    return _norm(alpha * x_ref[...] + _pick(is_s, gp_ref, gs_ref) * f) * pg_ref[...] + pb_ref[...]


def _ffn_kernel(h_ref, wg_ref, wu_ref, wd_ref, x_ref, gp_ref, gs_ref, pg_ref, pb_ref, o_ref, *, alpha, np_tiles):
    h = h_ref[...]
    a = (_silu(_dot(h, wg_ref[...])) * _dot(h, wu_ref[...])).astype(BF16)
    is_s = pl.program_id(0) >= np_tiles
    o_ref[...] = _post_residual(x_ref, gp_ref, gs_ref, _dot(a, wd_ref[...]), pg_ref, pb_ref, is_s, alpha)


def _ffn_dense(h, wg, wu, wd, x, mod_p, mod_s, pg, pb, l, lj, rm, alpha):
    n, d = x.shape
    f = wg.shape[2]
    tm = rm.t
    post = pl.BlockSpec((None, None, 1, d), lambda i: (l, 1, 0, 0))
    return pl.pallas_call(
        functools.partial(_ffn_kernel, alpha=alpha, np_tiles=rm.np_tiles),
        grid=(n // tm,),
        in_specs=[pl.BlockSpec((tm, d), lambda i: (i, 0)),
                  pl.BlockSpec((None, d, f), lambda i: (lj, 0, 0), pipeline_mode=_RARELY),
                  pl.BlockSpec((None, d, f), lambda i: (lj, 0, 0), pipeline_mode=_RARELY),
                  pl.BlockSpec((None, f, d), lambda i: (lj, 0, 0), pipeline_mode=_RARELY),
                  pl.BlockSpec((tm, d), lambda i: (i, 0)),
                  *rm.mod_specs(l, 5, d), post, post],
        out_specs=pl.BlockSpec((tm, d), lambda i: (i, 0)),
        out_shape=jax.ShapeDtypeStruct((n, d), F32),
        compiler_params=_cparams("parallel"),
        name="ffn_dense",
    )(h, wg, wu, wd, x, mod_p, mod_s, pg, pb)


def _rank_kernel(route_ref, rank_ref, tot_ref, carry):
    i = pl.program_id(0)
    tm = route_ref.shape[0]

    @pl.when(i == 0)
    def _():
        carry[...] = jnp.zeros_like(carry)

    r = route_ref[...]
    lane = lax.broadcasted_iota(jnp.int32, (tm, LANES), 1)
    e1 = r[:, 0:1].astype(jnp.int32)
    e2 = r[:, 1:2].astype(jnp.int32)
    hit1 = lane == e1
    hit2 = lane == e2
    onehot = jnp.where(jnp.logical_or(hit1, hit2), 1.0, 0.0)
    tt = lax.broadcasted_iota(jnp.int32, (tm, tm), 0)
    ss = lax.broadcasted_iota(jnp.int32, (tm, tm), 1)
    before = jnp.where(ss < tt, 1.0, 0.0).astype(BF16)
    cnt = _dot(before, onehot.astype(BF16)) + carry[0:1, :]
    r1 = jnp.sum(jnp.where(hit1, cnt, 0.0), axis=-1, keepdims=True)
    r2 = jnp.sum(jnp.where(hit2, cnt, 0.0), axis=-1, keepdims=True)
    rank_ref[...] = jnp.where(lane == 0, r1, jnp.where(lane == 1, r2, 0.0))
    carry[...] = carry[...] + jnp.sum(onehot, axis=0, keepdims=True)
    tot_ref[...] = carry[...]


def _moe_rank(route, tm):
    n = route.shape[0]
    return pl.pallas_call(
        _rank_kernel,
        grid=(n // tm,),
        in_specs=[pl.BlockSpec((tm, LANES), lambda i: (i, 0))],
        out_specs=[pl.BlockSpec((tm, LANES), lambda i: (i, 0)),
                   pl.BlockSpec((8, LANES), lambda i: (0, 0))],
        out_shape=[jax.ShapeDtypeStruct((n, LANES), F32), jax.ShapeDtypeStruct((8, LANES), F32)],
        scratch_shapes=[pltpu.VMEM((8, LANES), F32)],
        compiler_params=_cparams("arbitrary"),
        name="moe_rank",
    )(route)


def _row_copy(src, s, dst, t, sem):
    return pltpu.make_async_copy(src.at[pl.ds(s, 1)], dst.at[pl.ds(t, 1)], sem)


def _dispatch_kernel(d1_ref, d2_ref, zl_ref, h_ref, xs_hbm, zbuf, stage, sem, zsem):
    i = pl.program_id(0)
    tm = h_ref.shape[0]

    @pl.when(i == 0)
    def _():
        zbuf[...] = jnp.zeros_like(zbuf)

        def zero_copy(k):
            row = pl.multiple_of(jnp.maximum(zl_ref[k], 0), MOE_SUB)
            return pltpu.make_async_copy(zbuf, xs_hbm.at[pl.ds(row, MOE_SUB)], zsem)

        def start(k, carry):
            @pl.when(zl_ref[k] >= 0)
            def _():
                zero_copy(k).start()
            return carry

        def wait(k, carry):
            @pl.when(zl_ref[k] >= 0)
            def _():
                zero_copy(k).wait()
            return carry

        lax.fori_loop(0, zl_ref.shape[0], start, 0)
        lax.fori_loop(0, zl_ref.shape[0], wait, 0)

    slot = i % 2
    stage[slot] = h_ref[...]

    def issue(r, carry):
        _row_copy(stage.at[slot], r, xs_hbm, d1_ref[i * tm + r], sem.at[slot]).start(priority=0)
        _row_copy(stage.at[slot], r, xs_hbm, d2_ref[i * tm + r], sem.at[slot]).start(priority=1)
        return carry

    lax.fori_loop(0, tm, issue, 0, unroll=8)

    def drain(s):
        for _ in range(2):
            pltpu.make_async_copy(stage.at[s], xs_hbm.at[pl.ds(0, tm)], sem.at[s]).wait()

    @pl.when(i > 0)
    def _():
        drain(1 - slot)

    @pl.when(i == pl.num_programs(0) - 1)
    def _():
        drain(slot)


def _moe_dispatch(dest1, dest2, zlist, h, n_rows, tm):
    n, d = h.shape
    grid_spec = pltpu.PrefetchScalarGridSpec(
        num_scalar_prefetch=3,
        grid=(n // tm,),
        in_specs=[pl.BlockSpec((tm, d), lambda i, *_: (i, 0))],
        out_specs=pl.BlockSpec(memory_space=pl.ANY),
        scratch_shapes=[pltpu.VMEM((MOE_SUB, d), F32), pltpu.VMEM((2, tm, d), F32),
                        pltpu.SemaphoreType.DMA((2,)), pltpu.SemaphoreType.DMA(())],
    )
    return pl.pallas_call(
        _dispatch_kernel, grid_spec=grid_spec,
        out_shape=jax.ShapeDtypeStruct((n_rows, d), F32),
        compiler_params=_cparams("arbitrary"),
        name="moe_dispatch",
    )(dest1, dest2, zlist, h)


def _expert_kernel(blk_e_ref, nvalid_ref, xs_ref, wg_ref, wu_ref, wd_ref, y_ref, xb):
    b = pl.program_id(0)
    j = pl.program_id(1)
    tb = xb.shape[0]
    nsub = tb // MOE_SUB
    nv = nvalid_ref[b]
    used = (nv + MOE_SUB - 1) // MOE_SUB

    def run(rows):
        wg = wg_ref[...].astype(BF16)
        wu = wu_ref[...].astype(BF16)
        wd = wd_ref[...].astype(BF16)

        @pl.when(j == 0)
        def _():
            xb[rows, :] = xs_ref[rows, :].astype(BF16)

        x = xb[rows, :]
        a = (_silu(_dot(x, wg)) * _dot(x, wu)).astype(BF16)
        part = _dot(a, wd)

        @pl.when(j == 0)
        def _():
            y_ref[rows, :] = part

        @pl.when(j > 0)
        def _():
            y_ref[rows, :] = y_ref[rows, :] + part

    for m in range(1, nsub + 1):
        @pl.when(used == m)
        def _():
            run(pl.ds(0, m * MOE_SUB))

    for s in range(nsub):
        @pl.when(jnp.logical_and(s >= used, j == pl.num_programs(1) - 1))
        def _():
            y_ref[pl.ds(s * MOE_SUB, MOE_SUB), :] = jnp.zeros((MOE_SUB, y_ref.shape[1]), F32)


def _moe_experts(blk_e, nvalid, xs, wg, wu, wd, lj, tb, tf):
    n_rows, d = xs.shape
    f = wg.shape[3]
    nj = f // tf

    def jm(b, j, nv):
        return jnp.where(nv[b] > 0, j, nj - 1)

    grid_spec = pltpu.PrefetchScalarGridSpec(
        num_scalar_prefetch=2,
        grid=(n_rows // tb, nj),
        in_specs=[pl.BlockSpec((tb, d), lambda b, j, be, nv: (b, 0)),
                  pl.BlockSpec((None, None, d, tf), lambda b, j, be, nv: (lj, be[b], 0, jm(b, j, nv))),
                  pl.BlockSpec((None, None, d, tf), lambda b, j, be, nv: (lj, be[b], 0, jm(b, j, nv))),
                  pl.BlockSpec((None, None, tf, d), lambda b, j, be, nv: (lj, be[b], jm(b, j, nv), 0))],
        out_specs=pl.BlockSpec((tb, d), lambda b, j, be, nv: (b, 0)),
        scratch_shapes=[pltpu.VMEM((tb, d), BF16)],
    )
    return pl.pallas_call(
        _expert_kernel, grid_spec=grid_spec,
        out_shape=jax.ShapeDtypeStruct((n_rows, d), F32),
        compiler_params=_cparams("parallel", "arbitrary"),
        name="moe_experts",
    )(blk_e, nvalid, xs, wg, wu, wd)


def _combine_kernel(d1_ref, d2_ref, ys_hbm, route_ref, x_ref, gp_ref, gs_ref, pg_ref, pb_ref, *rest,
                    alpha, np_tiles, split):
    (*outs, y1, y2, sem1, sem2) = rest
    i = pl.program_id(0)
    tm = x_ref.shape[0]
    slot = i % 2

    last = pl.num_programs(0) - 1

    def issue(tile, s, r):
        _row_copy(ys_hbm, d1_ref[tile * tm + r], y1.at[s], r, sem1.at[s]).start(priority=0)
        _row_copy(ys_hbm, d2_ref[tile * tm + r], y2.at[s], r, sem2.at[s]).start(priority=1)

    def drain(s):
        pltpu.make_async_copy(ys_hbm.at[pl.ds(0, tm)], y1.at[s], sem1.at[s]).wait()
        pltpu.make_async_copy(ys_hbm.at[pl.ds(0, tm)], y2.at[s], sem2.at[s]).wait()

    @pl.when(i == 0)
    def _():
        lax.fori_loop(0, tm, lambda r, c: (issue(0, 0, r), c)[1], 0, unroll=8)

    drain(slot)
    nxt = jnp.minimum(i + 1, last)
    for r in range(tm):
        issue(nxt, 1 - slot, r)
    r = route_ref[...]
    f = y1[slot] * r[:, 2:3] + y2[slot] * r[:, 3:4]
    out = _post_residual(x_ref, gp_ref, gs_ref, f, pg_ref, pb_ref, i >= np_tiles, alpha)
    if split:
        op_ref, os_ref = outs

        @pl.when(i < np_tiles)
        def _():
            op_ref[...] = out

        @pl.when(i >= np_tiles)
        def _():
            os_ref[...] = out
    else:
        outs[0][...] = out

    @pl.when(i == last)
    def _():
        drain(1 - slot)


def _moe_combine(dest1, dest2, ys, route, x, mod_p, mod_s, pg, pb, l, rm, alpha, split):
    n, d = x.shape
    tm = rm.t
    if split:
        n_p = rm.np_tiles * tm
        out_specs = [pl.BlockSpec((tm, d), lambda i, *_: (rm.prompt(i), 0)),
                     pl.BlockSpec((tm, d), lambda i, *_: (rm.sample(i), 0))]
        out_shape = [jax.ShapeDtypeStruct((n_p, d), F32), jax.ShapeDtypeStruct((n - n_p, d), F32)]
    else:
        out_specs = pl.BlockSpec((tm, d), lambda i, *_: (i, 0))
        out_shape = jax.ShapeDtypeStruct((n, d), F32)
    post = pl.BlockSpec((None, None, 1, d), lambda i, *_: (l, 1, 0, 0))
    grid_spec = pltpu.PrefetchScalarGridSpec(
        num_scalar_prefetch=2,
        grid=(n // tm,),
        in_specs=[pl.BlockSpec(memory_space=pl.ANY),
                  pl.BlockSpec((tm, LANES), lambda i, *_: (i, 0)),
                  pl.BlockSpec((tm, d), lambda i, *_: (i, 0)),
                  *rm.mod_specs(l, 5, d), post, post],
        out_specs=out_specs,
        scratch_shapes=[pltpu.VMEM((2, tm, d), F32), pltpu.VMEM((2, tm, d), F32),
                        pltpu.SemaphoreType.DMA((2,)), pltpu.SemaphoreType.DMA((2,))],
    )
    return pl.pallas_call(
        functools.partial(_combine_kernel, alpha=alpha, np_tiles=rm.np_tiles, split=split), grid_spec=grid_spec,
        out_shape=out_shape,
        compiler_params=_cparams("arbitrary"),
        name="moe_combine",
    )(dest1, dest2, ys, route, x, mod_p, mod_s, pg, pb)


def _moe_ffn(h, route, wg, wu, wd, x, mod_p, mod_s, pg, pb, l, lj, rm, tb, tf, alpha, split):
    n = h.shape[0]
    rank, tot = _moe_rank(route, rm.t)
    counts = tot[0, :N_EXPERTS].astype(jnp.int32)
    padded = (counts + tb - 1) // tb * tb
    pad_end = jnp.cumsum(padded)
    pad_start = pad_end - padded
    experts = jnp.arange(N_EXPERTS, dtype=jnp.int32)

    def slot(col):
        e = route[:, col].astype(jnp.int32)
        start = jnp.sum(jnp.where(e[:, None] == experts[None, :], pad_start[None, :], 0), axis=1)
        return start + rank[:, col].astype(jnp.int32)

    dest1, dest2 = slot(0), slot(1)
    n_blocks = -(-(2 * n + N_EXPERTS * (tb - 1)) // tb)
    blk_start = jnp.arange(n_blocks, dtype=jnp.int32) * tb
    blk_e = jnp.minimum(jnp.sum(pad_end[None, :] <= blk_start[:, None], axis=1), N_EXPERTS - 1).astype(jnp.int32)
    own_end = jnp.sum(jnp.where(blk_e[:, None] == experts[None, :], (pad_start + counts)[None, :], 0), axis=1)
    nvalid = jnp.clip(own_end - blk_start, 0, tb).astype(jnp.int32)
    nvalid = jnp.where(blk_start < pad_end[-1], nvalid, 0)
    per_blk = tb // MOE_SUB
    sub_start = jnp.arange(n_blocks * per_blk, dtype=jnp.int32) * MOE_SUB
    sub_room = jnp.repeat(blk_start + nvalid, per_blk) - sub_start
    zlist = jnp.where(sub_room < MOE_SUB, sub_start, -1).astype(jnp.int32)
    xs = _moe_dispatch(dest1, dest2, zlist, h, n_blocks * tb, rm.t)
    ys = _moe_experts(blk_e, nvalid, xs, wg, wu, wd, lj, tb, tf)
    return _moe_combine(dest1, dest2, ys, route, x, mod_p, mod_s, pg, pb, l, rm, alpha, split)


def _pack_w_in(w_in, b_in):
    depth, d, _ = w_in.shape
    q_end = 2 * D_CONV + N_HEADS * HEAD_DIM
    k_end = q_end + N_KV * HEAD_DIM
    a_end = k_end + N_KV * HEAD_DIM
    m_end = a_end + 4 * M_HEADS * M_DK
    if_end = m_end + 2 * M_HEADS

    def pack(a):
        return jnp.concatenate([a[..., if_end:], a[..., :q_end], a[..., a_end:m_end], a[..., q_end:a_end]], axis=-1)

    def gates(a):
        return jnp.pad(a[..., m_end:if_end], [(0, 0)] * (a.ndim - 1) + [(0, LANES - 2 * M_HEADS)])

    return (pack(w_in).astype(BF16), pack(b_in).reshape(depth, 1, Z_W),
            gates(w_in).astype(BF16), gates(b_in).reshape(depth, 1, LANES))


def kernel(x_prompt, x_sample, cache_swa_k, cache_swa_v, state_conv, state_mlstm_C, state_mlstm_n, state_mlstm_m, c_prompt, c_sample, w_ada, b_ada, w_in, b_in, conv_w, conv_b, conv_ln_g, conv_ln_b, w_conv_out, attn_sinks, rel_bias, w_attn_out, m_norm_g, w_m_out, w_out, post_ln_g, post_ln_b, ffn_w_gate, ffn_w_up, ffn_w_down, router_w, router_b, moe_w_gate, moe_w_up, moe_w_down):
    bp, tp, d = x_prompt.shape
    bs, ts, _ = x_sample.shape
    depth = w_ada.shape[0]
    alpha = (2 * depth) ** 0.25
    n_p, n_s = bp * tp, bs * ts
    tm = n_s
    assert d == D_MODEL and tp % tm == 0 and tp % WINDOW == 0 and tm % 32 == 0
    rm = _RowMap(tm, bp, tp, n_p, n_s)
    rm_half = _RowMap(tm // 2, bp, tp, n_p, n_s)
    wb = cache_swa_k.shape[2]
    big = n_p >= 4096
    tc = 512 if big else tm
    lm = 256 if big else min(tp, 128)
    tb = 1024 if big else 2 * MOE_SUB
    bs_blk = 32 if bs % 32 == 0 else bs
    bs_att = 16 if bs % 16 == 0 else bs
    bs_m = 4 if bs % 4 == 0 else 1
    lts = 16

    n = n_p + n_s
    x = (x_prompt.reshape(n_p, d), jnp.transpose(x_sample, (1, 0, 2)).reshape(n_s, d))

    nc_rows = -(-(bp + bs) // 8) * 8
    c_all = jnp.zeros((nc_rows, d), F32).at[:bp].set(c_prompt).at[bp:bp + bs].set(c_sample)
    mod = _ada_mod(c_all, w_ada, b_ada)
    mod_p = mod[:, :bp].reshape(depth, bp, 1, 6 * d)
    mod_s = jnp.tile(mod[:, bp:bp + bs], (1, ts, 1))

    w_in_p, b_in_p, w_if, b_if = _pack_w_in(w_in, b_in)
    wc_b, wa_b, wm_b, wo_b = (w.astype(BF16) for w in (w_conv_out, w_attn_out, w_m_out, w_out))
    fg_b, fu_b, fd_b = (w.astype(BF16) for w in (ffn_w_gate, ffn_w_up, ffn_w_down))
    cw_pad = jnp.pad(conv_w, ((0, 0), (0, CONV_PAD - CONV_W), (0, 0)))
    cvecs = [v.reshape(depth, 1, D_CONV) for v in (conv_b, conv_ln_g, conv_ln_b)]
    rw_pad = jnp.pad(router_w, ((0, 0), (0, 0), (0, LANES - N_EXPERTS)))
    rb_pad = jnp.pad(router_b, ((0, 0), (0, LANES - N_EXPERTS)), constant_values=NEG_INF)[:, None, :]
    pg = post_ln_g.reshape(depth, 2, 1, d)
    pb = post_ln_b.reshape(depth, 2, 1, d)
    gamma = m_norm_g.reshape(depth, 1, M_HEADS * M_DV)
    sinks = attn_sinks.astype(F32)

    qi = jnp.arange(WINDOW)[:, None]
    kj = jnp.arange(2 * WINDOW)[None, :]
    dist_p = qi + WINDOW - kj
    bh = _bias_heads(rel_bias, dist_p, (dist_p >= 0) & (dist_p < WINDOW))
    bias_p = bh.reshape(N_HEADS // 2, 2, WINDOW, 2 * WINDOW).transpose(0, 2, 1, 3).reshape(
        N_HEADS // 2, WINDOW, 4 * WINDOW)
    dist_s = jnp.arange(ts)[:, None] + wb - jnp.arange(wb + ts)[None, :]
    bias_s = _bias_heads(rel_bias, dist_s, (dist_s >= 0) & (dist_s < WINDOW)).reshape(
        N_KV, Q_PER_KV * ts, wb + ts)
    kc_all = cache_swa_k.reshape(depth, bs, wb, N_KV * HEAD_DIM)
    vc_all = cache_swa_v.reshape(depth, bs, wb, N_KV * HEAD_DIM)

    f_moe = moe_w_gate.shape[3]
    tf_moe = 512 if f_moe % 512 == 0 else f_moe

    n0_all = state_mlstm_n[:, :, :, None, :]
    m0_all = _m_state_in(state_mlstm_m)
    s_c = s_conv = s_kv = None
    new_p = [[] for _ in range(6)]
    new_s = [[] for _ in range(6)]
    for l in range(depth):
        j = l // 2
        xp, xs = x if isinstance(x, tuple) else (x, x)
        z, zif = _ln_proj(xp, xs, mod_p, mod_s, w_in_p, b_in_p, w_if, b_if, l, rm, n)
        zs3 = z[n_p:].reshape(ts, bs, Z_W)

        cp, ns_p = _conv_prompt(z, jnp.zeros((bp, CONV_PAD, D_CONV), F32), cw_pad, *cvecs, l, bp, tp, tc)
        cs3, s_conv = _conv_sample(zs3, state_conv, cw_pad, *cvecs, s_conv, l, bs_blk)
        new_p[2].append(ns_p[:, CONV_PAD - CONV_W + 1:])

        sink_h = sinks[l].reshape(N_KV, Q_PER_KV, 1)
        sink_s = jnp.broadcast_to(sink_h, (N_KV, Q_PER_KV, ts)).reshape(N_KV, Q_PER_KV * ts, 1)
        ap = _attn_prompt(z, bias_p * LOG2E, sinks * LOG2E, l, bp, tp)
        nk = min(WINDOW, tp)
        kv_tail = jnp.stack([z[(b + 1) * tp - nk:(b + 1) * tp, Z_K:Z_K + 2 * N_KV * HEAD_DIM]
                             for b in range(bp)]).astype(F32)
        new_p[0].append(kv_tail[..., :N_KV * HEAD_DIM].reshape(bp, nk, N_KV, HEAD_DIM))
        new_p[1].append(kv_tail[..., N_KV * HEAD_DIM:].reshape(bp, nk, N_KV, HEAD_DIM))
        q_s = zs3[:, :, Z_Q:Z_Q + N_HEADS * HEAD_DIM].reshape(ts, bs, N_KV, Q_PER_KV, HEAD_DIM)
        q4 = jnp.transpose(q_s, (1, 2, 3, 0, 4)).reshape(bs, N_KV, Q_PER_KV * ts, HEAD_DIM)
        k_s = jnp.transpose(zs3[:, :, Z_K:Z_K + N_KV * HEAD_DIM].reshape(ts, bs, N_KV, HEAD_DIM), (1, 0, 2, 3))
        v_s = jnp.transpose(zs3[:, :, Z_V:Z_V + N_KV * HEAD_DIM].reshape(ts, bs, N_KV, HEAD_DIM), (1, 0, 2, 3))
        o4, s_kv = _attn_sample(q4, kc_all, vc_all, jnp.transpose(k_s, (0, 2, 1, 3)), jnp.transpose(v_s, (0, 2, 1, 3)),
                                bias_s, sink_s, zs3[:, :, Z_K:Z_K + N_KV * HEAD_DIM],
                                zs3[:, :, Z_V:Z_V + N_KV * HEAD_DIM], s_kv, l, bs_att)
        as_ = jnp.transpose(o4.reshape(bs, N_KV, Q_PER_KV, ts, HEAD_DIM), (3, 0, 1, 2, 4)).reshape(n_s, -1).astype(BF16)

        ncp = tp // lm
        if_p = zif[:n_p, :2 * M_HEADS].reshape(bp * ncp, lm, 2 * M_HEADS)
        mp, c1p, n1p, m1p = _mlstm_prompt(z, zif, jnp.transpose(if_p, (0, 2, 1)), gamma, l, bp, ncp, lm)
        new_p[3].append(c1p)
        new_p[4].append(n1p[:, :, 0])
        new_p[5].append(m1p[:, :M_HEADS, 0])
        tpad = ((0, 0), (0, lts - ts), (0, 0))
        zm3 = jnp.pad(jnp.transpose(zs3[:, :, Z_MQ:Z_K], (1, 0, 2)), tpad)
        if3 = jnp.pad(jnp.transpose(zif[n_p:].reshape(ts, bs, LANES), (1, 0, 2)), tpad)
        ms, s_c, n1s, m1s = _mlstm_sample(zm3, if3, jnp.transpose(if3[:, :, :2 * M_HEADS], (0, 2, 1)), gamma,
                                          state_mlstm_C, n0_all[l], m0_all[l], s_c, l, ts, bs_m)
        ms = jnp.transpose(ms[:, :ts], (1, 0, 2)).reshape(n_s, -1)
        new_s[4].append(n1s[:, :, 0])
        new_s[5].append(m1s[:, :M_HEADS, 0])

        moe = l % 2 == 1
        router = (rw_pad, rb_pad, j) if moe else None
        outs = _merge(cp, cs3.reshape(n_s, D_CONV), ap, as_, mp, ms, z, xp, xs, mod_p, mod_s,
                      wc_b, wa_b, wm_b, wo_b, pg, pb, l, rm, alpha, router)
        if moe:
            x1, h2, route = outs
            x = _moe_ffn(h2, route, moe_w_gate, moe_w_up, moe_w_down, x1, mod_p, mod_s, pg, pb,
                         l, j, rm, tb, tf_moe, alpha, split=l == depth - 1)
        else:
            x1, h2 = outs
            x = _ffn_dense(h2, fg_b, fu_b, fd_b, x1, mod_p, mod_s, pg, pb, l, j, rm, alpha)

    x_p, x_s = x if isinstance(x, (list, tuple)) else (x[:n_p], x[n_p:])
    y_p = x_p.reshape(bp, tp, d)
    y_s = jnp.transpose(x_s.reshape(ts, bs, d), (1, 0, 2))
    p_k, p_v, p_conv, p_c, p_n, p_m = [jnp.stack(a) for a in new_p]
    s_k, s_v = [a.reshape(cache_swa_k.shape) for a in s_kv]
    s_n, s_m = jnp.stack(new_s[4]), jnp.stack(new_s[5])
    return (y_p, y_s, p_k, p_v, p_conv, p_c, p_n, p_m, s_k, s_v, s_conv, s_c, s_n, s_m)
```

```python
import functools
import math

import jax
import jax.numpy as jnp
from jax import lax
from jax.experimental import pallas as pl
from jax.experimental.pallas import tpu as pltpu

F32 = jnp.float32
BF16 = jnp.bfloat16

D_MODEL = 1024
D_CONV = 512
CONV_W = 31
CONV_PAD = 32
N_HEADS = 8
N_KV = 2
HEAD_DIM = 64
Q_PER_KV = N_HEADS // N_KV
WINDOW = 128
N_BUCKETS = 32
MAX_DIST = 128
M_HEADS = 4
M_DK = 128
M_DV = 128
N_EXPERTS = 8
LN_EPS = 1e-5
LANES = 128
NEG_INF = float("-inf")
LOG2E = math.log2(math.e)
VMEM_LIMIT = 56 * 1024 * 1024

Z_G, Z_UA, Z_UB, Z_Q = 0, 3072, 3584, 4096
Z_MQ, Z_MK, Z_MV, Z_MO = 4608, 5120, 5632, 6144
Z_K, Z_V, Z_W = 6656, 6784, 6912
TN_IN = 6912
ATT_QB = 8
ATT_GROUP = 8
MOE_SUB = 256
CONV_CHUNK = 32

def _cparams(*sem):
    return pltpu.CompilerParams(dimension_semantics=sem, vmem_limit_bytes=VMEM_LIMIT)


def _sigmoid(x):
    return 1.0 / (1.0 + jnp.exp(-x))


def _silu(x):
    return x * _sigmoid(x)


def _log_sigmoid(x):
    return jnp.minimum(x, 0.0) - jnp.log(1.0 + jnp.exp(-jnp.abs(x)))


def _norm(x):
    mu = jnp.mean(x, axis=-1, keepdims=True)
    xc = x - mu
    var = jnp.mean(xc * xc, axis=-1, keepdims=True)
    return xc * lax.rsqrt(var + LN_EPS)


def _dot(a, b):
    return jnp.dot(a, b, preferred_element_type=F32)


def _dot_nt(a, b):
    return lax.dot_general(a, b, (((1,), (1,)), ((), ())), preferred_element_type=F32)


def _dot_tn(a, b):
    return lax.dot_general(a, b, (((0,), (0,)), ((), ())), preferred_element_type=F32)


def _dot_hi(a, b):
    return jnp.dot(a, b, preferred_element_type=F32, precision=lax.Precision.HIGHEST)


def _split_bf16(a):
    hi = a.astype(BF16)
    return hi, (a - hi.astype(F32)).astype(BF16)


_RARELY = pl.Buffered(1)


class _RowMap:
    def __init__(self, t, bp, tp, n_p, n_s):
        self.t = t
        self.bp = bp
        self.per_seq = tp // t
        self.np_tiles = n_p // t
        self.ns_tiles = n_s // t

    def seq(self, i):
        return jnp.minimum(i // self.per_seq, self.bp - 1)

    def prompt(self, i):
        return jnp.minimum(i, self.np_tiles - 1)

    def sample(self, i):
        return jnp.clip(i - self.np_tiles, 0, self.ns_tiles - 1)

    def mod_specs(self, l, k, d):
        return (pl.BlockSpec((None, None, 1, d), lambda i, *_: (l, self.seq(i), 0, k)),
                pl.BlockSpec((None, self.t, d), lambda i, *_: (l, self.sample(i), k), pipeline_mode=_RARELY))

    def x_specs(self, xs_arr, d):
        off = self.np_tiles if xs_arr.shape[0] > self.ns_tiles * self.t else 0
        return (pl.BlockSpec((self.t, d), lambda i, *_: (self.prompt(i), 0)),
                pl.BlockSpec((self.t, d), lambda i, *_: (off + self.sample(i), 0), pipeline_mode=_RARELY))


def _pick(is_s, p_ref, s_ref):
    return jnp.where(is_s, s_ref[...], p_ref[...])


def _ada_kernel(c_ref, w_ref, b_ref, o_ref):
    s = _silu(c_ref[...]).astype(BF16)
    o_ref[0] = _dot(s, w_ref[0].astype(BF16)) + b_ref[0]


def _ada_mod(c_all, w_ada, b_ada):
    depth, d, n6 = w_ada.shape
    rows = c_all.shape[0]
    return pl.pallas_call(
        _ada_kernel,
        grid=(depth, n6 // d),
        in_specs=[pl.BlockSpec((rows, d), lambda l, j: (0, 0)),
                  pl.BlockSpec((1, d, d), lambda l, j: (l, 0, j)),
                  pl.BlockSpec((1, 1, d), lambda l, j: (l, 0, j))],
        out_specs=pl.BlockSpec((1, rows, d), lambda l, j: (l, 0, j)),
        out_shape=jax.ShapeDtypeStruct((depth, rows, n6), F32),
        compiler_params=_cparams("parallel", "parallel"),
        name="ada_mod",
    )(c_all, w_ada, b_ada.reshape(depth, 1, n6))


def _ln_proj_kernel(xp_ref, xs_ref, shp_ref, shs_ref, scp_ref, scs_ref, w_ref, b_ref, wif_ref, bif_ref,
                    z_ref, zif_ref, h_scr, *, np_tiles):
    is_s = pl.program_id(0) >= np_tiles
    first = pl.program_id(1) == 0

    def prologue(x_ref, sh_ref, sc_ref):
        h = (_norm(x_ref[...]) * (1.0 + sc_ref[...]) + sh_ref[...]).astype(BF16)
        h_scr[...] = h
        zif_ref[...] = _dot(h, wif_ref[...]) + bif_ref[...]

    @pl.when(jnp.logical_and(first, is_s))
    def _():
        prologue(xs_ref, shs_ref, scs_ref)

    @pl.when(jnp.logical_and(first, jnp.logical_not(is_s)))
    def _():
        prologue(xp_ref, shp_ref, scp_ref)

    z_ref[...] = (_dot(h_scr[...], w_ref[...]) + b_ref[...]).astype(BF16)


def _ln_proj(xp, xs, mod_p, mod_s, w, b, wif, bif, l, rm, n):
    d = xp.shape[1]
    zw = w.shape[2]
    tm = rm.t
    shp, shs = rm.mod_specs(l, 0, d)
    scp, scs = rm.mod_specs(l, 1, d)
    wmode = _RARELY if zw == TN_IN else None
    return pl.pallas_call(
        functools.partial(_ln_proj_kernel, np_tiles=rm.np_tiles),
        grid=(n // tm, zw // TN_IN),
        in_specs=[*rm.x_specs(xs, d),
                  shp, shs, scp, scs,
                  pl.BlockSpec((None, d, TN_IN), lambda i, j: (l, 0, j), pipeline_mode=wmode),
                  pl.BlockSpec((None, 1, TN_IN), lambda i, j: (l, 0, j), pipeline_mode=wmode),
                  pl.BlockSpec((None, d, LANES), lambda i, j: (l, 0, 0), pipeline_mode=_RARELY),
                  pl.BlockSpec((None, 1, LANES), lambda i, j: (l, 0, 0), pipeline_mode=_RARELY)],
        out_specs=[pl.BlockSpec((tm, TN_IN), lambda i, j: (i, j)),
                   pl.BlockSpec((tm, LANES), lambda i, j: (i, 0))],
        out_shape=[jax.ShapeDtypeStruct((n, zw), BF16), jax.ShapeDtypeStruct((n, LANES), F32)],
        scratch_shapes=[pltpu.VMEM((tm, d), BF16)],
        compiler_params=_cparams("parallel", "arbitrary"),
        name="ln_proj",
    )(xp, xs, mod_p, mod_s, mod_p, mod_s, w, b, wif, bif)


def _conv_tail(yc, g_ref, b_ref):
    y = _norm(yc) * g_ref[...] + b_ref[...]
    return _silu(y).astype(BF16)


def _conv_prompt_kernel(ua_ref, ub_ref, st_ref, cw_ref, cb_ref, g_ref, b_ref, o_ref, ns_ref,
                        ext, shifted, yc, wrep):
    t = pl.program_id(1)
    tc = ua_ref.shape[0]
    sub = 8

    @pl.when(t == 0)
    def _():
        ext[0:CONV_PAD, :] = st_ref[0]

    @pl.when(t > 0)
    def _():
        ext[0:CONV_PAD, :] = ext[tc:tc + CONV_PAD, :]

    ext[CONV_PAD:, :] = ua_ref[...].astype(F32) * _sigmoid(ub_ref[...].astype(F32))
    for s in range(1, sub):
        shifted[s - 1] = ext[s:s + tc + CONV_PAD - sub, :]
    off = CONV_PAD - (CONV_W - 1)
    for w in range(CONV_W):
        wrep[w] = jnp.broadcast_to(cw_ref[w:w + 1, :], (sub, D_CONV))
    groups = CONV_CHUNK // sub

    for r0 in range(0, tc, CONV_CHUNK):
        acc = jnp.broadcast_to(cb_ref[...].reshape(1, 1, D_CONV), (groups, sub, D_CONV))
        for w in range(CONV_W):
            base, s = (off + w) // sub * sub, (off + w) % sub
            src = ext if s == 0 else shifted.at[s - 1]
            win = src[r0 + base:r0 + base + CONV_CHUNK, :]
            acc = acc + win.reshape(groups, sub, D_CONV) * wrep[w][None]
        yc[r0:r0 + CONV_CHUNK, :] = acc.reshape(CONV_CHUNK, D_CONV)
    o_ref[...] = _conv_tail(yc[...], g_ref, b_ref)

    @pl.when(t == pl.num_programs(1) - 1)
    def _():
        ns_ref[0] = ext[tc:tc + CONV_PAD, :]


def _conv_vec_specs(l, nargs):
    return [pl.BlockSpec((None, 1, D_CONV), lambda *_: (l, 0, 0)) for _ in range(nargs)]


def _conv_prompt(z, state_pad, cw, cb, g, b, l, bp, tp, tc):
    nt = tp // tc
    return pl.pallas_call(
        _conv_prompt_kernel,
        grid=(bp, nt),
        in_specs=[pl.BlockSpec((tc, D_CONV), lambda bb, t: (bb * nt + t, Z_UA // D_CONV)),
                  pl.BlockSpec((tc, D_CONV), lambda bb, t: (bb * nt + t, Z_UB // D_CONV)),
                  pl.BlockSpec((1, CONV_PAD, D_CONV), lambda bb, t: (bb, 0, 0)),
                  pl.BlockSpec((None, CONV_PAD, D_CONV), lambda bb, t: (l, 0, 0))] + _conv_vec_specs(l, 3),
        out_specs=[pl.BlockSpec((tc, D_CONV), lambda bb, t: (bb * nt + t, 0)),
                   pl.BlockSpec((1, CONV_PAD, D_CONV), lambda bb, t: (bb, 0, 0))],
        out_shape=[jax.ShapeDtypeStruct((bp * tp, D_CONV), BF16),
                   jax.ShapeDtypeStruct((bp, CONV_PAD, D_CONV), F32)],
        scratch_shapes=[pltpu.VMEM((tc + CONV_PAD, D_CONV), F32),
                        pltpu.VMEM((7, tc + CONV_PAD - 8, D_CONV), F32),
                        pltpu.VMEM((tc, D_CONV), F32),
                        pltpu.VMEM((CONV_PAD, 8, D_CONV), F32)],
        compiler_params=_cparams("parallel", "arbitrary"),
        name="conv_prompt",
    )(z, z, state_pad, cw, cb, g, b)


def _conv_sample_kernel(ua_ref, ub_ref, st_ref, cw_ref, cb_ref, g_ref, b_ref, *rest, first_layer):
    o_ref, ns_ref = rest[-2:]
    ts = ua_ref.shape[0]
    ns = CONV_W - 1
    a = ua_ref[...].astype(F32) * _sigmoid(ub_ref[...].astype(F32))
    st = st_ref[...]
    if first_layer is not None:
        for dd in range(ns_ref.shape[0]):
            if dd != first_layer:
                ns_ref[dd] = jnp.zeros(ns_ref.shape[1:], F32)
        ns_ref = ns_ref.at[first_layer]
    ns_ref[:, 0:ns - ts, :] = st[:, ts:, :]
    for t in range(ts):
        ns_ref[:, ns - ts + t, :] = a[t]
    row = lax.broadcasted_iota(jnp.int32, (ns, D_CONV), 0)
    for t in range(ts):
        wt = jnp.zeros((ns, D_CONV), F32)
        for j in range(t, ns):
            wt = jnp.where(row == j, cw_ref[j - t:j - t + 1, :], wt)
        yc = jnp.sum(st * wt[None], axis=1) + cb_ref[...]
        for t2 in range(t + 1):
            wi = CONV_W - 1 - (t - t2)
            yc = yc + a[t2] * cw_ref[wi:wi + 1, :]
        o_ref[t] = _conv_tail(yc, g_ref, b_ref)


def _conv_sample(zs3, state, cw, cb, g, b, ns_prev, l, bs_blk):
    ts, bs, _ = zs3.shape
    depth = state.shape[0]
    ns = CONV_W - 1
    in_specs = [pl.BlockSpec((ts, bs_blk, D_CONV), lambda i: (0, i, Z_UA // D_CONV)),
                pl.BlockSpec((ts, bs_blk, D_CONV), lambda i: (0, i, Z_UB // D_CONV)),
                pl.BlockSpec((None, bs_blk, ns, D_CONV), lambda i: (l, i, 0, 0)),
                pl.BlockSpec((None, CONV_PAD, D_CONV), lambda i: (l, 0, 0))] + _conv_vec_specs(l, 3)
    args = [zs3, zs3, state, cw, cb, g, b]
    aliases = {}
    if ns_prev is None:
        ns_spec = pl.BlockSpec((depth, bs_blk, ns, D_CONV), lambda i: (0, i, 0, 0))
    else:
        in_specs.append(pl.BlockSpec(memory_space=pl.ANY))
        args.append(ns_prev)
        aliases = {len(args) - 1: 1}
        ns_spec = pl.BlockSpec((None, bs_blk, ns, D_CONV), lambda i: (l, i, 0, 0))
    return pl.pallas_call(
        functools.partial(_conv_sample_kernel, first_layer=l if ns_prev is None else None),
        grid=(bs // bs_blk,),
        in_specs=in_specs,
        out_specs=[pl.BlockSpec((ts, bs_blk, D_CONV), lambda i: (0, i, 0)), ns_spec],
        out_shape=[jax.ShapeDtypeStruct((ts, bs, D_CONV), BF16),
                   jax.ShapeDtypeStruct((depth, bs, ns, D_CONV), F32)],
        input_output_aliases=aliases,
        compiler_params=_cparams("parallel"),
        name="conv_sample",
    )(*args)


def _t5_bucket(dist):
    max_exact = N_BUCKETS // 2
    d = jnp.maximum(dist, 0)
    large = max_exact + (jnp.log(jnp.maximum(d, 1).astype(F32) / max_exact)
                         / math.log(MAX_DIST / max_exact) * (N_BUCKETS - max_exact)).astype(jnp.int32)
    return jnp.where(d < max_exact, d, jnp.minimum(large, N_BUCKETS - 1))


def _bias_heads(rel_bias, dist, valid):
    onehot = (_t5_bucket(dist)[..., None] == jnp.arange(N_BUCKETS)).astype(F32)
    bias = jnp.einsum("qkb,bh->qkh", onehot, rel_bias.astype(F32), precision=lax.Precision.HIGHEST)
    bias = jnp.where(valid[..., None], bias, NEG_INF)
    return jnp.transpose(bias, (2, 0, 1))


def _attn_prompt_kernel(sink_ref, q_ref, kc_ref, kp_ref, vc_ref, vp_ref, bias_ref, o_ref, *, l):
    first = pl.program_id(1) == 0
    w = WINDOW
    nq = q_ref.shape[0] // w
    kall = jnp.concatenate([kp_ref[...], kc_ref[...]], axis=0).astype(F32)
    vall = jnp.concatenate([vp_ref[...], vc_ref[...]], axis=0).astype(F32)
    lane = lax.broadcasted_iota(jnp.int32, kall.shape, 1)
    lo = lane < HEAD_DIM
    kroll = pltpu.roll(kall, HEAD_DIM, 1)
    vroll = pltpu.roll(vall, HEAD_DIM, 1)

    def halves(a, aroll, g):
        if g == 0:
            return jnp.where(lo, a, 0.0).astype(BF16), jnp.where(lo, 0.0, aroll).astype(BF16)
        return jnp.where(lo, aroll, 0.0).astype(BF16), jnp.where(lo, 0.0, a).astype(BF16)

    kh = [halves(kall, kroll, g) for g in range(N_KV)]
    vh = [halves(vall, vroll, g) for g in range(N_KV)]
    col = lax.broadcasted_iota(jnp.int32, (w, 4 * w), 1)
    prev_col = (col % (2 * w)) < w
    tiles_per_g = Q_PER_KV // 2
    units = [(qi, tile) for qi in range(nq) for tile in range(N_HEADS // 2)]
    for u0 in range(0, len(units), ATT_GROUP):
        group = units[u0:u0 + ATT_GROUP]
        scores = []
        for qi, tile in group:
            r0, g = qi * w, tile // tiles_per_g
            q = q_ref[r0:r0 + w, tile * LANES:(tile + 1) * LANES]
            kk = jnp.concatenate([kh[g][0][r0:r0 + 2 * w], kh[g][1][r0:r0 + 2 * w]], axis=0)
            s = _dot_nt(q, kk) * (HEAD_DIM ** -0.5 * LOG2E) + bias_ref[tile]
            if qi == 0:
                s = jnp.where(jnp.logical_and(first, prev_col), NEG_INF, s)
            scores.append(s)
        probs = []
        for (qi, tile), s in zip(group, scores):
            ps = []
            for half in range(2):
                sh = s[:, half * 2 * w:(half + 1) * 2 * w]
                sink = sink_ref[l, 2 * tile + half]
                mx = jnp.maximum(jnp.max(sh, axis=-1, keepdims=True), sink)
                p = jnp.exp2(sh - mx)
                den = jnp.sum(p, axis=-1, keepdims=True) + jnp.exp2(sink - mx)
                ps.append((p * (1.0 / den)).astype(BF16))
            probs.append(jnp.concatenate(ps, axis=1))
        for (qi, tile), p in zip(group, probs):
            r0, g = qi * w, tile // tiles_per_g
            vv = jnp.concatenate([vh[g][0][r0:r0 + 2 * w], vh[g][1][r0:r0 + 2 * w]], axis=0)
            o_ref[r0:r0 + w, tile * LANES:(tile + 1) * LANES] = _dot(p, vv).astype(BF16)


def _attn_prompt(z, bias, sinks, l, bp, tp):
    w = WINDOW
    qb = ATT_QB if tp % (ATT_QB * w) == 0 else 1
    ns = tp // (qb * w)
    nb = tp // w
    kvw = N_KV * HEAD_DIM
    qw = N_HEADS * HEAD_DIM

    def cur(col):
        return lambda bb, i: (bb * ns + i, col)

    def prev(col):
        return lambda bb, i: (bb * nb + jnp.maximum(i * qb - 1, 0), col)

    return pl.pallas_call(
        functools.partial(_attn_prompt_kernel, l=l),
        grid=(bp, ns),
        in_specs=[pl.BlockSpec(memory_space=pltpu.SMEM),
                  pl.BlockSpec((qb * w, qw), cur(Z_Q // qw)),
                  pl.BlockSpec((qb * w, kvw), cur(Z_K // kvw)),
                  pl.BlockSpec((w, kvw), prev(Z_K // kvw)),
                  pl.BlockSpec((qb * w, kvw), cur(Z_V // kvw)),
                  pl.BlockSpec((w, kvw), prev(Z_V // kvw)),
                  pl.BlockSpec((N_HEADS // 2, w, 4 * w), lambda bb, i: (0, 0, 0))],
        out_specs=pl.BlockSpec((qb * w, qw), lambda bb, i: (bb * ns + i, 0)),
        out_shape=jax.ShapeDtypeStruct((bp * tp, qw), BF16),
        compiler_params=_cparams("parallel", "parallel"),
        name="attn_prompt",
    )(sinks, z, z, z, z, z, bias)


def _attn_sample_kernel(q_ref, kc_ref, vc_ref, kn_ref, vn_ref, bias_ref, sink_ref, kt_ref, vt_ref, *rest,
                        first_layer):
    o_ref, nk_ref, nv_ref = rest[-3:]
    wb = kc_ref.shape[1]
    ts = kn_ref.shape[2]
    for new_ref, old_ref, t_ref in ((nk_ref, kc_ref, kt_ref), (nv_ref, vc_ref, vt_ref)):
        if first_layer is not None:
            for dd in range(new_ref.shape[0]):
                if dd != first_layer:
                    new_ref[dd] = jnp.zeros(new_ref.shape[1:], F32)
            new_ref = new_ref.at[first_layer]
        new_ref[:, 0:wb - ts, :] = old_ref[:, ts:, :]
        for t in range(ts):
            new_ref[:, wb - ts + t, :] = t_ref[t].astype(F32)
    for g in range(N_KV):
        lo = g * HEAD_DIM
        qb = (q_ref[:, g].astype(F32) * (HEAD_DIM ** -0.5)).astype(BF16)
        kc = kc_ref[:, :, lo:lo + HEAD_DIM].astype(BF16)
        vc = vc_ref[:, :, lo:lo + HEAD_DIM].astype(BF16)
        kn = kn_ref[:, g].astype(F32)
        vn = vn_ref[:, g].astype(F32)
        bias = bias_ref[g]
        s_c = jnp.einsum("bqd,bkd->bqk", qb, kc, preferred_element_type=F32) + bias[None, :, :wb]
        qf = qb.astype(F32)
        s_n = [jnp.sum(qf * kn[:, j:j + 1, :], axis=-1, keepdims=True) + bias[None, :, wb + j:wb + j + 1]
               for j in range(ts)]
        sink = sink_ref[g][None]
        mx = jnp.maximum(jnp.max(s_c, axis=-1, keepdims=True), sink)
        for sj in s_n:
            mx = jnp.maximum(mx, sj)
        p_c = jnp.exp(s_c - mx)
        p_n = [jnp.exp(sj - mx) for sj in s_n]
        den = jnp.sum(p_c, axis=-1, keepdims=True) + jnp.exp(sink - mx)
        for pj in p_n:
            den = den + pj
        o = jnp.einsum("bqk,bkd->bqd", (p_c / den).astype(BF16), vc, preferred_element_type=F32)
        for j in range(ts):
            o = o + (p_n[j] / den).astype(BF16).astype(F32) * vn[:, j:j + 1, :]
        o_ref[:, g] = o


def _attn_sample(q4, kc, vc, kn, vn, bias, sinks, kt, vt, new_prev, l, bs_blk):
    bs, _, rt, _ = q4.shape
    depth, _, wb, kvw = kc.shape
    ts = kn.shape[2]
    tmaj = pl.BlockSpec((ts, bs_blk, kvw), lambda i: (0, i, 0))
    in_specs = [pl.BlockSpec((bs_blk, N_KV, rt, HEAD_DIM), lambda i: (i, 0, 0, 0)),
                pl.BlockSpec((None, bs_blk, wb, kvw), lambda i: (l, i, 0, 0)),
                pl.BlockSpec((None, bs_blk, wb, kvw), lambda i: (l, i, 0, 0)),
                pl.BlockSpec((bs_blk, N_KV, ts, HEAD_DIM), lambda i: (i, 0, 0, 0)),
                pl.BlockSpec((bs_blk, N_KV, ts, HEAD_DIM), lambda i: (i, 0, 0, 0)),
                pl.BlockSpec((N_KV, rt, wb + ts), lambda i: (0, 0, 0)),
                pl.BlockSpec((N_KV, rt, 1), lambda i: (0, 0, 0)),
                tmaj, tmaj]
    args = [q4, kc, vc, kn, vn, bias, sinks, kt, vt]
    aliases = {}
    if new_prev is None:
        new_spec = pl.BlockSpec((depth, bs_blk, wb, kvw), lambda i: (0, i, 0, 0))
    else:
        in_specs += [pl.BlockSpec(memory_space=pl.ANY)] * 2
        args += list(new_prev)
        aliases = {len(args) - 2: 1, len(args) - 1: 2}
        new_spec = pl.BlockSpec((None, bs_blk, wb, kvw), lambda i: (l, i, 0, 0))
    new_shape = jax.ShapeDtypeStruct((depth, bs, wb, kvw), F32)
    outs = pl.pallas_call(
        functools.partial(_attn_sample_kernel, first_layer=l if new_prev is None else None),
        grid=(bs // bs_blk,),
        in_specs=in_specs,
        out_specs=[pl.BlockSpec((bs_blk, N_KV, rt, HEAD_DIM), lambda i: (i, 0, 0, 0)), new_spec, new_spec],
        out_shape=[jax.ShapeDtypeStruct((bs, N_KV, rt, HEAD_DIM), F32), new_shape, new_shape],
        input_output_aliases=aliases,
        compiler_params=_cparams("parallel"),
        name="attn_sample",
    )(*args)
    return outs[0], (outs[1], outs[2])


def _mlstm_kernel(*refs, t_valid, nseq, aliased, carried, chunk_axis, per_seq_inputs=False, first_layer=None):
    if aliased:
        refs = refs[:10] + refs[11:]
    if per_seq_inputs:
        groups = [refs[6 * s:6 * s + 6] for s in range(nseq)]
        if_ref, ifr_ref, q_ref, k_ref, v_ref, o_ref = (tuple(g[k] for g in groups) for k in range(6))
        refs = (None,) * 6 + refs[6 * nseq:]
    else:
        if_ref, ifr_ref, q_ref, k_ref, v_ref, o_ref = refs[:6]
    g_ref, c0_ref, n0_ref, m0_ref, h_ref, c1_ref, n1_ref, m1_ref = refs[6:14]
    c = pl.program_id(chunk_axis)
    L = (ifr_ref[0] if per_seq_inputs else ifr_ref).shape[-1]
    if carried:
        c_in, n_in, m_in = c_out, n_out, m_out = refs[14:]

        @pl.when(c == 0)
        def _():
            c_in[...] = c0_ref[...]
            n_in[...] = n0_ref[...]
            m_in[...] = m0_ref[...]
    else:
        (c_in, n_in, m_in), (c_out, n_out, m_out) = (c0_ref, n0_ref, m0_ref), (c1_ref, n1_ref, m1_ref)
        if first_layer is not None:
            for dd in range(c1_ref.shape[0]):
                if dd != first_layer:
                    c1_ref[dd] = jnp.zeros(c1_ref.shape[1:], F32)
            c_out = c1_ref.at[first_layer]

    def seq(ref, s):
        return ref[s] if isinstance(ref, tuple) else ref.at[s]

    tt = lax.broadcasted_iota(jnp.int32, (L, L), 0)
    ss = lax.broadcasted_iota(jnp.int32, (L, L), 1)
    causal = ss <= tt
    tril = causal.astype(F32)
    triu = (tt <= ss).astype(F32)
    seqs = range(nseq)
    heads = [(s_i, h) for s_i in seqs for h in range(M_HEADS)]
    mxu_sums = L % LANES == 0
    gates = []
    for s_i in seqs:
        ifc = seq(if_ref, s_i)[...]
        ifr = ifr_ref[s_i][0] if per_seq_inputs else ifr_ref[s_i]
        lf_c = _log_sigmoid(ifc)
        lf_r = _log_sigmoid(ifr)
        i_c, i_r = ifc, ifr
        if t_valid < L:
            rc = lax.broadcasted_iota(jnp.int32, (L, LANES), 0) < t_valid
            rr = lax.broadcasted_iota(jnp.int32, (2 * M_HEADS, L), 1) < t_valid
            lf_c = jnp.where(rc, lf_c, 0.0)
            lf_r = jnp.where(rr, lf_r, 0.0)
            i_c = jnp.where(rc, i_c, NEG_INF)
            i_r = jnp.where(rr, i_r, NEG_INF)
        gates.append((lf_c, lf_r, i_c, i_r))
    f_cs = [_dot_hi(tril, g[0]) for g in gates]
    f_rs = [_dot_hi(g[1], triu) for g in gates]
    state = {(s_i, h): (c_in[s_i, h], n_in[s_i, h], m_in[s_i, h:h + 1, :]) for s_i, h in heads}

    def wide(col):
        return jnp.concatenate([col] * (L // LANES), axis=1) if L >= LANES else col[:, :L]

    st1 = {}
    for s_i, h in heads:
        _, nrow, m0 = state[s_i, h]
        fc = jnp.broadcast_to(f_cs[s_i][:, M_HEADS + h:M_HEADS + h + 1], (L, LANES))
        fr = f_rs[s_i][M_HEADS + h:M_HEADS + h + 1, :]
        ir = gates[s_i][3][h:h + 1, :]
        dm = jnp.where(causal, wide(fc) - fr + ir, NEG_INF)
        m_t = jnp.maximum(m0 + fc, jnp.max(dm, axis=-1, keepdims=True))
        st1[s_i, h] = (fc, dm, m_t, jnp.exp(m0 + fc - m_t))
    st2 = {}
    for s_i, h in heads:
        lo = h * M_DK
        fc, dm, m_t, inter = st1[s_i, h]
        qb = seq(q_ref, s_i)[:, lo:lo + M_DK]
        kf = seq(k_ref, s_i)[:, lo:lo + M_DK].astype(F32) * (M_DK ** -0.5)
        vf = seq(v_ref, s_i)[:, lo:lo + M_DV].astype(F32)
        kb, vb = kf.astype(BF16), vf.astype(BF16)
        if mxu_sums:
            n_rows = jnp.broadcast_to(state[s_i, h][1], (M_DK, M_DK)).astype(BF16)
            qk = _dot_nt(qb, jnp.concatenate([kb, n_rows], axis=0))
            sc, qn_rep = qk[:, :L] * jnp.exp(dm - wide(m_t)), qk[:, L:]
        else:
            sc, qn_rep = _dot_nt(qb, kb) * jnp.exp(dm - wide(m_t)), None
        st2[s_i, h] = (qb, kf, vf, kb, vb, sc, qn_rep)
    st3 = {}
    for s_i, h in heads:
        cm, nrow, m0 = state[s_i, h]
        fc, dm, m_t, inter = st1[s_i, h]
        qb, kf, vf, kb, vb, sc, qn_rep = st2[s_i, h]
        if mxu_sums:
            sv = _dot(sc.astype(BF16), jnp.concatenate([vb, jnp.ones((L, M_DV), BF16)], axis=1))
            num = inter * _dot(qb, cm.astype(BF16)) + sv[:, :M_DV]
            qn = inter * qn_rep + sv[:, M_DV:]
            floor = jnp.exp(-m_t)
        else:
            num = inter * _dot(qb, cm.astype(BF16)) + _dot(sc.astype(BF16), vb)
            qn = (inter * jnp.sum(qb.astype(F32) * nrow, axis=-1, keepdims=True)
                  + jnp.sum(sc, axis=-1, keepdims=True))
            floor = jnp.exp(-m_t)
        st3[s_i, h] = num / jnp.maximum(jnp.abs(qn), floor)
    new_state = {}
    for s_i, h in heads:
        cm, nrow, m0 = state[s_i, h]
        fc, dm, m_t, inter = st1[s_i, h]
        qb, kf, vf, kb, vb, sc, _ = st2[s_i, h]
        ic = jnp.broadcast_to(gates[s_i][2][:, h:h + 1], (L, LANES))
        m_end = m_t[L - 1:L, :]
        f_end = fc[L - 1:L, :]
        decay = jnp.exp(m0 + f_end - m_end)
        w_s = jnp.exp(f_end - fc + ic - m_end)
        new_state[s_i, h] = (decay * cm + _dot_tn(kb, (w_s * vf).astype(BF16)),
                             decay * nrow + jnp.sum(w_s * kf, axis=0, keepdims=True),
                             m_end)
    for s_i, h in heads:
        lo = h * M_DK
        hn = _norm(st3[s_i, h]) * g_ref[:, lo:lo + M_DV]
        gate = _sigmoid(seq(o_ref, s_i)[:, lo:lo + M_DV].astype(F32))
        seq(h_ref, s_i)[:, lo:lo + M_DV] = (gate * hn).astype(BF16)
    for s_i, h in heads:
        c_out[s_i, h], n_out[s_i, h], m_out[s_i, h:h + 1, :] = new_state[s_i, h]
    if not carried:
        for s_i in seqs:
            m_out[s_i, M_HEADS:, :] = jnp.zeros((M_HEADS, LANES), F32)

    if carried:
        @pl.when(c == pl.num_programs(chunk_axis) - 1)
        def _():
            c1_ref[...] = c_out[...]
            n1_ref[...] = n_out[...]
            m1_ref[...] = m_out[...]


def _mlstm_state_specs(nseq, l_state):
    if l_state is None:
        c_spec = pl.BlockSpec((nseq, M_HEADS, M_DK, M_DV), lambda b, c: (b, 0, 0, 0))
    else:
        c_spec = pl.BlockSpec((None, nseq, M_HEADS, M_DK, M_DV), lambda b, c: (l_state, b, 0, 0, 0))
    return (c_spec,
            pl.BlockSpec((nseq, M_HEADS, 1, M_DK), lambda b, c: (b, 0, 0, 0)),
            pl.BlockSpec((nseq, 2 * M_HEADS, LANES), lambda b, c: (b, 0, 0)))


def _mlstm_scratch(nseq):
    return [pltpu.VMEM((nseq, M_HEADS, M_DK, M_DV), F32),
            pltpu.VMEM((nseq, M_HEADS, 1, M_DK), F32),
            pltpu.VMEM((nseq, 2 * M_HEADS, LANES), F32)]


def _mlstm_prompt(z, zif, ifr, gamma, l, bp, nc, L):
    hw = M_HEADS * M_DK
    zero = lambda *s: jnp.zeros(s, F32)
    in_specs, args = [], []
    for b in range(bp):
        def rows(col, b=b):
            return lambda c: (b * nc + c, col)

        in_specs += [pl.BlockSpec((L, LANES), rows(0)),
                     pl.BlockSpec((1, 2 * M_HEADS, L), lambda c, b=b: (b * nc + c, 0, 0)),
                     pl.BlockSpec((L, hw), rows(Z_MQ // hw)),
                     pl.BlockSpec((L, hw), rows(Z_MK // hw)),
                     pl.BlockSpec((L, hw), rows(Z_MV // hw)),
                     pl.BlockSpec((L, hw), rows(Z_MO // hw))]
        args += [zif, ifr, z, z, z, z]
    state_specs = [pl.BlockSpec((bp, M_HEADS, M_DK, M_DV), lambda c: (0, 0, 0, 0)),
                   pl.BlockSpec((bp, M_HEADS, 1, M_DK), lambda c: (0, 0, 0, 0)),
                   pl.BlockSpec((bp, 2 * M_HEADS, LANES), lambda c: (0, 0, 0))]
    kern = functools.partial(_mlstm_kernel, t_valid=L, nseq=bp, aliased=False, carried=True, chunk_axis=0,
                             per_seq_inputs=True)
    outs = pl.pallas_call(
        kern,
        grid=(nc,),
        in_specs=in_specs + [pl.BlockSpec((None, 1, hw), lambda c: (l, 0, 0))] + state_specs,
        out_specs=[pl.BlockSpec((bp, L, hw), lambda c: (0, c, 0))] + state_specs,
        out_shape=[jax.ShapeDtypeStruct((bp, nc * L, hw), BF16),
                   jax.ShapeDtypeStruct((bp, M_HEADS, M_DK, M_DV), F32),
                   jax.ShapeDtypeStruct((bp, M_HEADS, 1, M_DK), F32),
                   jax.ShapeDtypeStruct((bp, 2 * M_HEADS, LANES), F32)],
        scratch_shapes=_mlstm_scratch(bp),
        compiler_params=_cparams("arbitrary"),
        name="mlstm_prompt",
    )(*args, gamma, zero(bp, M_HEADS, M_DK, M_DV), zero(bp, M_HEADS, 1, M_DK), zero(bp, 2 * M_HEADS, LANES))
    return (outs[0].reshape(bp * nc * L, hw),) + tuple(outs[1:])


def _mlstm_sample(zm3, if3, ifr, gamma, c_all, n0, m0x, c_out_prev, l, t_valid, nseq):
    bs, L, _ = zm3.shape
    depth = c_all.shape[0]
    hw = M_HEADS * M_DK
    aliased = c_out_prev is not None

    def blk(col):
        return pl.BlockSpec((nseq, L, hw), lambda b, c: (b, 0, col))

    c_in, n_spec, m_spec = _mlstm_state_specs(nseq, l)
    in_specs = [pl.BlockSpec((nseq, L, LANES), lambda b, c: (b, 0, 0)),
                pl.BlockSpec((nseq, 2 * M_HEADS, L), lambda b, c: (b, 0, 0)),
                blk(0), blk(1), blk(2), blk(3),
                pl.BlockSpec((None, 1, hw), lambda b, c: (l, 0, 0)),
                c_in, n_spec, m_spec]
    args = [if3, ifr, zm3, zm3, zm3, zm3, gamma, c_all, n0, m0x]
    aliases = {}
    if aliased:
        in_specs.append(pl.BlockSpec(memory_space=pl.ANY))
        args.append(c_out_prev)
        aliases = {len(args) - 1: 1}
        c_out = c_in
    else:
        c_out = pl.BlockSpec((depth, nseq, M_HEADS, M_DK, M_DV), lambda b, c: (0, b, 0, 0, 0))
    kern = functools.partial(_mlstm_kernel, t_valid=t_valid, nseq=nseq, aliased=aliased, carried=False,
                             chunk_axis=1, first_layer=None if aliased else l)
    return pl.pallas_call(
        kern,
        grid=(bs // nseq, 1),
        in_specs=in_specs,
        out_specs=[pl.BlockSpec((nseq, L, hw), lambda b, c: (b, 0, 0)), c_out, n_spec, m_spec],
        out_shape=[jax.ShapeDtypeStruct((bs, L, hw), BF16),
                   jax.ShapeDtypeStruct((depth, bs, M_HEADS, M_DK, M_DV), F32),
                   jax.ShapeDtypeStruct((bs, M_HEADS, 1, M_DK), F32),
                   jax.ShapeDtypeStruct((bs, 2 * M_HEADS, LANES), F32)],
        input_output_aliases=aliases,
        compiler_params=_cparams("parallel", "arbitrary"),
        name="mlstm_sample",
    )(*args)


def _m_state_in(m):
    lead = m.shape[:-1]
    mx = jnp.zeros(lead + (2 * M_HEADS, LANES), F32)
    return mx.at[..., :M_HEADS, :].set(jnp.broadcast_to(m[..., None], lead + (M_HEADS, LANES)))


def _merge_kernel(*refs, np_tiles, alpha, route):
    (cp_ref, cs_ref, ap_ref, as_ref, mp_ref, ms_ref, g0_ref, g1_ref, g2_ref, xp_ref, xs_ref,
     gtp_ref, gts_ref, shp_ref, shs_ref, scp_ref, scs_ref,
     wc_ref, wa_ref, wm_ref, wo_ref, pg_ref, pb_ref) = refs[:23]
    if route:
        rw_ref, rb_ref, x1_ref, h_ref, route_ref = refs[23:]
    else:
        x1_ref, h_ref = refs[23:]
    is_s = pl.program_id(0) >= np_tiles

    def gate(ref):
        return _sigmoid(ref[...].astype(F32))

    def run(c_ref, a_ref, m_ref, x_ref, gt_ref, sh_ref, sc_ref):
        y = (gate(g0_ref) * _dot(c_ref[...], wc_ref[...])
             + gate(g1_ref) * _dot(a_ref[...], wa_ref[...])
             + gate(g2_ref) * _dot(m_ref[...], wm_ref[...]))
        mix = _dot(y.astype(BF16), wo_ref[...])
        x1 = _norm(alpha * x_ref[...] + gt_ref[...] * mix) * pg_ref[...] + pb_ref[...]
        x1_ref[...] = x1
        h = _norm(x1) * (1.0 + sc_ref[...]) + sh_ref[...]
        h_ref[...] = h.astype(h_ref.dtype)
        if route:
            h_hi, h_lo = _split_bf16(h)
            w_hi, w_lo = _split_bf16(rw_ref[...])
            logits = _dot(h_hi, w_hi) + (_dot(h_hi, w_lo) + _dot(h_lo, w_hi)) + rb_ref[...]
            lane = lax.broadcasted_iota(jnp.int32, logits.shape, 1)
            m1 = jnp.max(logits, axis=-1, keepdims=True)
            e1 = jnp.min(jnp.where(logits == m1, lane, LANES), axis=-1, keepdims=True)
            l2 = jnp.where(lane == e1, NEG_INF, logits)
            m2 = jnp.max(l2, axis=-1, keepdims=True)
            e2 = jnp.min(jnp.where(l2 == m2, lane, LANES), axis=-1, keepdims=True)
            ex = jnp.exp(m2 - m1)
            w1 = 1.0 / (1.0 + ex)
            w2 = ex / (1.0 + ex)
            out = jnp.where(lane == 0, e1.astype(F32),
                            jnp.where(lane == 1, e2.astype(F32),
                                      jnp.where(lane == 2, w1, jnp.where(lane == 3, w2, 0.0))))
            route_ref[...] = out

    @pl.when(is_s)
    def _():
        run(cs_ref, as_ref, ms_ref, xs_ref, gts_ref, shs_ref, scs_ref)

    @pl.when(jnp.logical_not(is_s))
    def _():
        run(cp_ref, ap_ref, mp_ref, xp_ref, gtp_ref, shp_ref, scp_ref)


def _merge(cp, cs, ap, as_, mp, ms, z, xp, xs, mod_p, mod_s, wc, wa, wm, wo, pg, pb, l, rm, alpha, router):
    n, d = z.shape[0], xp.shape[1]
    hw = D_CONV
    route = router is not None
    tm = rm.t

    def pblk():
        return pl.BlockSpec((tm, hw), lambda i: (rm.prompt(i), 0))

    def sblk():
        return pl.BlockSpec((tm, hw), lambda i: (rm.sample(i), 0), pipeline_mode=_RARELY)

    def zg(k):
        return pl.BlockSpec((tm, d), lambda i: (i, Z_G // d + k))

    def lw(a):
        return pl.BlockSpec((None,) + a.shape[1:], lambda i: (l,) + (0,) * (a.ndim - 1), pipeline_mode=_RARELY)

    post = pl.BlockSpec((None, None, 1, d), lambda i: (l, 0, 0, 0))
    in_specs = [pblk(), sblk(), pblk(), sblk(), pblk(), sblk(), zg(0), zg(1), zg(2),
                *rm.x_specs(xs, d),
                *rm.mod_specs(l, 2, d), *rm.mod_specs(l, 3, d), *rm.mod_specs(l, 4, d),
                lw(wc), lw(wa), lw(wm), lw(wo), post, post]
    args = [cp, cs, ap, as_, mp, ms, z, z, z, xp, xs, mod_p, mod_s, mod_p, mod_s, mod_p, mod_s,
            wc, wa, wm, wo, pg, pb]
    out_specs = [pl.BlockSpec((tm, d), lambda i: (i, 0)), pl.BlockSpec((tm, d), lambda i: (i, 0))]
    out_shape = [jax.ShapeDtypeStruct((n, d), F32), jax.ShapeDtypeStruct((n, d), F32 if route else BF16)]
    if route:
        rw, rb, lj = router
        in_specs += [pl.BlockSpec((None, d, LANES), lambda i: (lj, 0, 0)),
                     pl.BlockSpec((None, 1, LANES), lambda i: (lj, 0, 0))]
        args += [rw, rb]
        out_specs.append(pl.BlockSpec((tm, LANES), lambda i: (i, 0)))
        out_shape.append(jax.ShapeDtypeStruct((n, LANES), F32))
    kern = functools.partial(_merge_kernel, np_tiles=rm.np_tiles, alpha=alpha, route=route)
    return pl.pallas_call(
        kern, grid=(n // tm,), in_specs=in_specs, out_specs=out_specs, out_shape=out_shape,
        compiler_params=_cparams("parallel"), name="merge_route" if route else "merge",
    )(*args)


def _post_residual(x_ref, gp_ref, gs_ref, f, pg_ref, pb_ref, is_s, alpha):
    return _norm(alpha * x_ref[...] + _pick(is_s, gp_ref, gs_ref) * f) * pg_ref[...] + pb_ref[...]


def _ffn_kernel(h_ref, wg_ref, wu_ref, wd_ref, x_ref, gp_ref, gs_ref, pg_ref, pb_ref, o_ref, *, alpha, np_tiles):
    h = h_ref[...]
    a = (_silu(_dot(h, wg_ref[...])) * _dot(h, wu_ref[...])).astype(BF16)
    is_s = pl.program_id(0) >= np_tiles
    o_ref[...] = _post_residual(x_ref, gp_ref, gs_ref, _dot(a, wd_ref[...]), pg_ref, pb_ref, is_s, alpha)


def _ffn_dense(h, wg, wu, wd, x, mod_p, mod_s, pg, pb, l, lj, rm, alpha):
    n, d = x.shape
    f = wg.shape[2]
    tm = rm.t
    post = pl.BlockSpec((None, None, 1, d), lambda i: (l, 1, 0, 0))
    return pl.pallas_call(
        functools.partial(_ffn_kernel, alpha=alpha, np_tiles=rm.np_tiles),
        grid=(n // tm,),
        in_specs=[pl.BlockSpec((tm, d), lambda i: (i, 0)),
                  pl.BlockSpec((None, d, f), lambda i: (lj, 0, 0), pipeline_mode=_RARELY),
                  pl.BlockSpec((None, d, f), lambda i: (lj, 0, 0), pipeline_mode=_RARELY),
                  pl.BlockSpec((None, f, d), lambda i: (lj, 0, 0), pipeline_mode=_RARELY),
                  pl.BlockSpec((tm, d), lambda i: (i, 0)),
                  *rm.mod_specs(l, 5, d), post, post],
        out_specs=pl.BlockSpec((tm, d), lambda i: (i, 0)),
        out_shape=jax.ShapeDtypeStruct((n, d), F32),
        compiler_params=_cparams("parallel"),
        name="ffn_dense",
    )(h, wg, wu, wd, x, mod_p, mod_s, pg, pb)


def _rank_kernel(route_ref, rank_ref, tot_ref, carry):
    i = pl.program_id(0)
    tm = route_ref.shape[0]

    @pl.when(i == 0)
    def _():
        carry[...] = jnp.zeros_like(carry)

    r = route_ref[...]
    lane = lax.broadcasted_iota(jnp.int32, (tm, LANES), 1)
    e1 = r[:, 0:1].astype(jnp.int32)
    e2 = r[:, 1:2].astype(jnp.int32)
    hit1 = lane == e1
    hit2 = lane == e2
    onehot = jnp.where(jnp.logical_or(hit1, hit2), 1.0, 0.0)
    tt = lax.broadcasted_iota(jnp.int32, (tm, tm), 0)
    ss = lax.broadcasted_iota(jnp.int32, (tm, tm), 1)
    before = jnp.where(ss < tt, 1.0, 0.0).astype(BF16)
    cnt = _dot(before, onehot.astype(BF16)) + carry[0:1, :]
    r1 = jnp.sum(jnp.where(hit1, cnt, 0.0), axis=-1, keepdims=True)
    r2 = jnp.sum(jnp.where(hit2, cnt, 0.0), axis=-1, keepdims=True)
    rank_ref[...] = jnp.where(lane == 0, r1, jnp.where(lane == 1, r2, 0.0))
    carry[...] = carry[...] + jnp.sum(onehot, axis=0, keepdims=True)
    tot_ref[...] = carry[...]


def _moe_rank(route, tm):
    n = route.shape[0]
    return pl.pallas_call(
        _rank_kernel,
        grid=(n // tm,),
        in_specs=[pl.BlockSpec((tm, LANES), lambda i: (i, 0))],
        out_specs=[pl.BlockSpec((tm, LANES), lambda i: (i, 0)),
                   pl.BlockSpec((8, LANES), lambda i: (0, 0))],
        out_shape=[jax.ShapeDtypeStruct((n, LANES), F32), jax.ShapeDtypeStruct((8, LANES), F32)],
        scratch_shapes=[pltpu.VMEM((8, LANES), F32)],
        compiler_params=_cparams("arbitrary"),
        name="moe_rank",
    )(route)


def _row_copy(src, s, dst, t, sem):
    return pltpu.make_async_copy(src.at[pl.ds(s, 1)], dst.at[pl.ds(t, 1)], sem)


def _dispatch_kernel(d1_ref, d2_ref, zl_ref, h_ref, xs_hbm, zbuf, stage, sem, zsem):
    i = pl.program_id(0)
    tm = h_ref.shape[0]

    @pl.when(i == 0)
    def _():
        zbuf[...] = jnp.zeros_like(zbuf)

        def zero_copy(k):
            row = pl.multiple_of(jnp.maximum(zl_ref[k], 0), MOE_SUB)
            return pltpu.make_async_copy(zbuf, xs_hbm.at[pl.ds(row, MOE_SUB)], zsem)

        def start(k, carry):
            @pl.when(zl_ref[k] >= 0)
            def _():
                zero_copy(k).start()
            return carry

        def wait(k, carry):
            @pl.when(zl_ref[k] >= 0)
            def _():
                zero_copy(k).wait()
            return carry

        lax.fori_loop(0, zl_ref.shape[0], start, 0)
        lax.fori_loop(0, zl_ref.shape[0], wait, 0)

    slot = i % 2
    stage[slot] = h_ref[...]

    def issue(r, carry):
        _row_copy(stage.at[slot], r, xs_hbm, d1_ref[i * tm + r], sem.at[slot]).start(priority=0)
        _row_copy(stage.at[slot], r, xs_hbm, d2_ref[i * tm + r], sem.at[slot]).start(priority=1)
        return carry

    lax.fori_loop(0, tm, issue, 0, unroll=8)

    def drain(s):
        for _ in range(2):
            pltpu.make_async_copy(stage.at[s], xs_hbm.at[pl.ds(0, tm)], sem.at[s]).wait()

    @pl.when(i > 0)
    def _():
        drain(1 - slot)

    @pl.when(i == pl.num_programs(0) - 1)
    def _():
        drain(slot)


def _moe_dispatch(dest1, dest2, zlist, h, n_rows, tm):
    n, d = h.shape
    grid_spec = pltpu.PrefetchScalarGridSpec(
        num_scalar_prefetch=3,
        grid=(n // tm,),
        in_specs=[pl.BlockSpec((tm, d), lambda i, *_: (i, 0))],
        out_specs=pl.BlockSpec(memory_space=pl.ANY),
        scratch_shapes=[pltpu.VMEM((MOE_SUB, d), F32), pltpu.VMEM((2, tm, d), F32),
                        pltpu.SemaphoreType.DMA((2,)), pltpu.SemaphoreType.DMA(())],
    )
    return pl.pallas_call(
        _dispatch_kernel, grid_spec=grid_spec,
        out_shape=jax.ShapeDtypeStruct((n_rows, d), F32),
        compiler_params=_cparams("arbitrary"),
        name="moe_dispatch",
    )(dest1, dest2, zlist, h)


def _expert_kernel(blk_e_ref, nvalid_ref, xs_ref, wg_ref, wu_ref, wd_ref, y_ref, xb):
    b = pl.program_id(0)
    j = pl.program_id(1)
    tb = xb.shape[0]
    nsub = tb // MOE_SUB
    nv = nvalid_ref[b]
    used = (nv + MOE_SUB - 1) // MOE_SUB

    def run(rows):
        wg = wg_ref[...].astype(BF16)
        wu = wu_ref[...].astype(BF16)
        wd = wd_ref[...].astype(BF16)

        @pl.when(j == 0)
        def _():
            xb[rows, :] = xs_ref[rows, :].astype(BF16)

        x = xb[rows, :]
        a = (_silu(_dot(x, wg)) * _dot(x, wu)).astype(BF16)
        part = _dot(a, wd)

        @pl.when(j == 0)
        def _():
            y_ref[rows, :] = part

        @pl.when(j > 0)
        def _():
            y_ref[rows, :] = y_ref[rows, :] + part

    for m in range(1, nsub + 1):
        @pl.when(used == m)
        def _():
            run(pl.ds(0, m * MOE_SUB))

    for s in range(nsub):
        @pl.when(jnp.logical_and(s >= used, j == pl.num_programs(1) - 1))
        def _():
            y_ref[pl.ds(s * MOE_SUB, MOE_SUB), :] = jnp.zeros((MOE_SUB, y_ref.shape[1]), F32)


def _moe_experts(blk_e, nvalid, xs, wg, wu, wd, lj, tb, tf):
    n_rows, d = xs.shape
    f = wg.shape[3]
    nj = f // tf

    def jm(b, j, nv):
        return jnp.where(nv[b] > 0, j, nj - 1)

    grid_spec = pltpu.PrefetchScalarGridSpec(
        num_scalar_prefetch=2,
        grid=(n_rows // tb, nj),
        in_specs=[pl.BlockSpec((tb, d), lambda b, j, be, nv: (b, 0)),
                  pl.BlockSpec((None, None, d, tf), lambda b, j, be, nv: (lj, be[b], 0, jm(b, j, nv))),
                  pl.BlockSpec((None, None, d, tf), lambda b, j, be, nv: (lj, be[b], 0, jm(b, j, nv))),
                  pl.BlockSpec((None, None, tf, d), lambda b, j, be, nv: (lj, be[b], jm(b, j, nv), 0))],
        out_specs=pl.BlockSpec((tb, d), lambda b, j, be, nv: (b, 0)),
        scratch_shapes=[pltpu.VMEM((tb, d), BF16)],
    )
    return pl.pallas_call(
        _expert_kernel, grid_spec=grid_spec,
        out_shape=jax.ShapeDtypeStruct((n_rows, d), F32),
        compiler_params=_cparams("parallel", "arbitrary"),
        name="moe_experts",
    )(blk_e, nvalid, xs, wg, wu, wd)


def _combine_kernel(d1_ref, d2_ref, ys_hbm, route_ref, x_ref, gp_ref, gs_ref, pg_ref, pb_ref, *rest,
                    alpha, np_tiles, split):
    (*outs, y1, y2, sem1, sem2) = rest
    i = pl.program_id(0)
    tm = x_ref.shape[0]
    slot = i % 2

    last = pl.num_programs(0) - 1

    def issue(tile, s, r):
        _row_copy(ys_hbm, d1_ref[tile * tm + r], y1.at[s], r, sem1.at[s]).start(priority=0)
        _row_copy(ys_hbm, d2_ref[tile * tm + r], y2.at[s], r, sem2.at[s]).start(priority=1)

    def drain(s):
        pltpu.make_async_copy(ys_hbm.at[pl.ds(0, tm)], y1.at[s], sem1.at[s]).wait()
        pltpu.make_async_copy(ys_hbm.at[pl.ds(0, tm)], y2.at[s], sem2.at[s]).wait()

    @pl.when(i == 0)
    def _():
        lax.fori_loop(0, tm, lambda r, c: (issue(0, 0, r), c)[1], 0, unroll=8)

    drain(slot)
    nxt = jnp.minimum(i + 1, last)
    for r in range(tm):
        issue(nxt, 1 - slot, r)
    r = route_ref[...]
    f = y1[slot] * r[:, 2:3] + y2[slot] * r[:, 3:4]
    out = _post_residual(x_ref, gp_ref, gs_ref, f, pg_ref, pb_ref, i >= np_tiles, alpha)
    if split:
        op_ref, os_ref = outs

        @pl.when(i < np_tiles)
        def _():
            op_ref[...] = out

        @pl.when(i >= np_tiles)
        def _():
            os_ref[...] = out
    else:
        outs[0][...] = out

    @pl.when(i == last)
    def _():
        drain(1 - slot)


def _moe_combine(dest1, dest2, ys, route, x, mod_p, mod_s, pg, pb, l, rm, alpha, split):
    n, d = x.shape
    tm = rm.t
    if split:
        n_p = rm.np_tiles * tm
        out_specs = [pl.BlockSpec((tm, d), lambda i, *_: (rm.prompt(i), 0)),
                     pl.BlockSpec((tm, d), lambda i, *_: (rm.sample(i), 0))]
        out_shape = [jax.ShapeDtypeStruct((n_p, d), F32), jax.ShapeDtypeStruct((n - n_p, d), F32)]
    else:
        out_specs = pl.BlockSpec((tm, d), lambda i, *_: (i, 0))
        out_shape = jax.ShapeDtypeStruct((n, d), F32)
    post = pl.BlockSpec((None, None, 1, d), lambda i, *_: (l, 1, 0, 0))
    grid_spec = pltpu.PrefetchScalarGridSpec(
        num_scalar_prefetch=2,
        grid=(n // tm,),
        in_specs=[pl.BlockSpec(memory_space=pl.ANY),
                  pl.BlockSpec((tm, LANES), lambda i, *_: (i, 0)),
                  pl.BlockSpec((tm, d), lambda i, *_: (i, 0)),
                  *rm.mod_specs(l, 5, d), post, post],
        out_specs=out_specs,
        scratch_shapes=[pltpu.VMEM((2, tm, d), F32), pltpu.VMEM((2, tm, d), F32),
                        pltpu.SemaphoreType.DMA((2,)), pltpu.SemaphoreType.DMA((2,))],
    )
    return pl.pallas_call(
        functools.partial(_combine_kernel, alpha=alpha, np_tiles=rm.np_tiles, split=split), grid_spec=grid_spec,
        out_shape=out_shape,
        compiler_params=_cparams("arbitrary"),
        name="moe_combine",
    )(dest1, dest2, ys, route, x, mod_p, mod_s, pg, pb)


def _moe_ffn(h, route, wg, wu, wd, x, mod_p, mod_s, pg, pb, l, lj, rm, tb, tf, alpha, split):
    n = h.shape[0]
    rank, tot = _moe_rank(route, rm.t)
    counts = tot[0, :N_EXPERTS].astype(jnp.int32)
    padded = (counts + tb - 1) // tb * tb
    pad_end = jnp.cumsum(padded)
    pad_start = pad_end - padded
    experts = jnp.arange(N_EXPERTS, dtype=jnp.int32)

    def slot(col):
        e = route[:, col].astype(jnp.int32)
        start = jnp.sum(jnp.where(e[:, None] == experts[None, :], pad_start[None, :], 0), axis=1)
        return start + rank[:, col].astype(jnp.int32)

    dest1, dest2 = slot(0), slot(1)
    n_blocks = -(-(2 * n + N_EXPERTS * (tb - 1)) // tb)
    blk_start = jnp.arange(n_blocks, dtype=jnp.int32) * tb
    blk_e = jnp.minimum(jnp.sum(pad_end[None, :] <= blk_start[:, None], axis=1), N_EXPERTS - 1).astype(jnp.int32)
    own_end = jnp.sum(jnp.where(blk_e[:, None] == experts[None, :], (pad_start + counts)[None, :], 0), axis=1)
    nvalid = jnp.clip(own_end - blk_start, 0, tb).astype(jnp.int32)
    nvalid = jnp.where(blk_start < pad_end[-1], nvalid, 0)
    per_blk = tb // MOE_SUB
    sub_start = jnp.arange(n_blocks * per_blk, dtype=jnp.int32) * MOE_SUB
    sub_room = jnp.repeat(blk_start + nvalid, per_blk) - sub_start
    zlist = jnp.where(sub_room < MOE_SUB, sub_start, -1).astype(jnp.int32)
    xs = _moe_dispatch(dest1, dest2, zlist, h, n_blocks * tb, rm.t)
    ys = _moe_experts(blk_e, nvalid, xs, wg, wu, wd, lj, tb, tf)
    return _moe_combine(dest1, dest2, ys, route, x, mod_p, mod_s, pg, pb, l, rm, alpha, split)


def _pack_w_in(w_in, b_in):
    depth, d, _ = w_in.shape
    q_end = 2 * D_CONV + N_HEADS * HEAD_DIM
    k_end = q_end + N_KV * HEAD_DIM
    a_end = k_end + N_KV * HEAD_DIM
    m_end = a_end + 4 * M_HEADS * M_DK
    if_end = m_end + 2 * M_HEADS

    def pack(a):
        return jnp.concatenate([a[..., if_end:], a[..., :q_end], a[..., a_end:m_end], a[..., q_end:a_end]], axis=-1)

    def gates(a):
        return jnp.pad(a[..., m_end:if_end], [(0, 0)] * (a.ndim - 1) + [(0, LANES - 2 * M_HEADS)])

    return (pack(w_in).astype(BF16), pack(b_in).reshape(depth, 1, Z_W),
            gates(w_in).astype(BF16), gates(b_in).reshape(depth, 1, LANES))


def kernel(x_prompt, x_sample, cache_swa_k, cache_swa_v, state_conv, state_mlstm_C, state_mlstm_n, state_mlstm_m, c_prompt, c_sample, w_ada, b_ada, w_in, b_in, conv_w, conv_b, conv_ln_g, conv_ln_b, w_conv_out, attn_sinks, rel_bias, w_attn_out, m_norm_g, w_m_out, w_out, post_ln_g, post_ln_b, ffn_w_gate, ffn_w_up, ffn_w_down, router_w, router_b, moe_w_gate, moe_w_up, moe_w_down):
    bp, tp, d = x_prompt.shape
    bs, ts, _ = x_sample.shape
    depth = w_ada.shape[0]
    alpha = (2 * depth) ** 0.25
    n_p, n_s = bp * tp, bs * ts
    tm = n_s
    assert d == D_MODEL and tp % tm == 0 and tp % WINDOW == 0 and tm % 32 == 0
    rm = _RowMap(tm, bp, tp, n_p, n_s)
    rm_half = _RowMap(tm // 2, bp, tp, n_p, n_s)
    wb = cache_swa_k.shape[2]
    big = n_p >= 4096
    tc = 1024 if big else tm
    lm = 256 if big else min(tp, 128)
    tb = 1024 if big else 2 * MOE_SUB
    bs_blk = 32 if bs % 32 == 0 else bs
    bs_att = 16 if bs % 16 == 0 else bs
    bs_m = 4 if bs % 4 == 0 else 1
    lts = 16

    n = n_p + n_s
    x = (x_prompt.reshape(n_p, d), jnp.transpose(x_sample, (1, 0, 2)).reshape(n_s, d))

    nc_rows = -(-(bp + bs) // 8) * 8
    c_all = jnp.zeros((nc_rows, d), F32).at[:bp].set(c_prompt).at[bp:bp + bs].set(c_sample)
    mod = _ada_mod(c_all, w_ada, b_ada)
    mod_p = mod[:, :bp].reshape(depth, bp, 1, 6 * d)
    mod_s = jnp.tile(mod[:, bp:bp + bs], (1, ts, 1))

    w_in_p, b_in_p, w_if, b_if = _pack_w_in(w_in, b_in)
    wc_b, wa_b, wm_b, wo_b = (w.astype(BF16) for w in (w_conv_out, w_attn_out, w_m_out, w_out))
    fg_b, fu_b, fd_b = (w.astype(BF16) for w in (ffn_w_gate, ffn_w_up, ffn_w_down))
    cw_pad = jnp.pad(conv_w, ((0, 0), (0, CONV_PAD - CONV_W), (0, 0)))
    cvecs = [v.reshape(depth, 1, D_CONV) for v in (conv_b, conv_ln_g, conv_ln_b)]
    rw_pad = jnp.pad(router_w, ((0, 0), (0, 0), (0, LANES - N_EXPERTS)))
    rb_pad = jnp.pad(router_b, ((0, 0), (0, LANES - N_EXPERTS)), constant_values=NEG_INF)[:, None, :]
    pg = post_ln_g.reshape(depth, 2, 1, d)
    pb = post_ln_b.reshape(depth, 2, 1, d)
    gamma = m_norm_g.reshape(depth, 1, M_HEADS * M_DV)
    sinks = attn_sinks.astype(F32)

    qi = jnp.arange(WINDOW)[:, None]
    kj = jnp.arange(2 * WINDOW)[None, :]
    dist_p = qi + WINDOW - kj
    bh = _bias_heads(rel_bias, dist_p, (dist_p >= 0) & (dist_p < WINDOW))
    bias_p = bh.reshape(N_HEADS // 2, 2, WINDOW, 2 * WINDOW).transpose(0, 2, 1, 3).reshape(
        N_HEADS // 2, WINDOW, 4 * WINDOW)
    dist_s = jnp.arange(ts)[:, None] + wb - jnp.arange(wb + ts)[None, :]
    bias_s = _bias_heads(rel_bias, dist_s, (dist_s >= 0) & (dist_s < WINDOW)).reshape(
        N_KV, Q_PER_KV * ts, wb + ts)
    kc_all = cache_swa_k.reshape(depth, bs, wb, N_KV * HEAD_DIM)
    vc_all = cache_swa_v.reshape(depth, bs, wb, N_KV * HEAD_DIM)

    f_moe = moe_w_gate.shape[3]
    tf_moe = 512 if f_moe % 512 == 0 else f_moe

    n0_all = state_mlstm_n[:, :, :, None, :]
    m0_all = _m_state_in(state_mlstm_m)
    s_c = s_conv = s_kv = None
    new_p = [[] for _ in range(6)]
    new_s = [[] for _ in range(6)]
    for l in range(depth):
        j = l // 2
        xp, xs = x if isinstance(x, tuple) else (x, x)
        z, zif = _ln_proj(xp, xs, mod_p, mod_s, w_in_p, b_in_p, w_if, b_if, l, rm, n)
        zs3 = z[n_p:].reshape(ts, bs, Z_W)

        cp, ns_p = _conv_prompt(z, jnp.zeros((bp, CONV_PAD, D_CONV), F32), cw_pad, *cvecs, l, bp, tp, tc)
        cs3, s_conv = _conv_sample(zs3, state_conv, cw_pad, *cvecs, s_conv, l, bs_blk)
        new_p[2].append(ns_p[:, CONV_PAD - CONV_W + 1:])

        sink_h = sinks[l].reshape(N_KV, Q_PER_KV, 1)
        sink_s = jnp.broadcast_to(sink_h, (N_KV, Q_PER_KV, ts)).reshape(N_KV, Q_PER_KV * ts, 1)
        ap = _attn_prompt(z, bias_p * LOG2E, sinks * LOG2E, l, bp, tp)
        nk = min(WINDOW, tp)
        kv_tail = jnp.stack([z[(b + 1) * tp - nk:(b + 1) * tp, Z_K:Z_K + 2 * N_KV * HEAD_DIM]
                             for b in range(bp)]).astype(F32)
        new_p[0].append(kv_tail[..., :N_KV * HEAD_DIM].reshape(bp, nk, N_KV, HEAD_DIM))
        new_p[1].append(kv_tail[..., N_KV * HEAD_DIM:].reshape(bp, nk, N_KV, HEAD_DIM))
        q_s = zs3[:, :, Z_Q:Z_Q + N_HEADS * HEAD_DIM].reshape(ts, bs, N_KV, Q_PER_KV, HEAD_DIM)
        q4 = jnp.transpose(q_s, (1, 2, 3, 0, 4)).reshape(bs, N_KV, Q_PER_KV * ts, HEAD_DIM)
        k_s = jnp.transpose(zs3[:, :, Z_K:Z_K + N_KV * HEAD_DIM].reshape(ts, bs, N_KV, HEAD_DIM), (1, 0, 2, 3))
        v_s = jnp.transpose(zs3[:, :, Z_V:Z_V + N_KV * HEAD_DIM].reshape(ts, bs, N_KV, HEAD_DIM), (1, 0, 2, 3))
        o4, s_kv = _attn_sample(q4, kc_all, vc_all, jnp.transpose(k_s, (0, 2, 1, 3)), jnp.transpose(v_s, (0, 2, 1, 3)),
                                bias_s, sink_s, zs3[:, :, Z_K:Z_K + N_KV * HEAD_DIM],
                                zs3[:, :, Z_V:Z_V + N_KV * HEAD_DIM], s_kv, l, bs_att)
        as_ = jnp.transpose(o4.reshape(bs, N_KV, Q_PER_KV, ts, HEAD_DIM), (3, 0, 1, 2, 4)).reshape(n_s, -1).astype(BF16)

        ncp = tp // lm
        if_p = zif[:n_p, :2 * M_HEADS].reshape(bp * ncp, lm, 2 * M_HEADS)
        mp, c1p, n1p, m1p = _mlstm_prompt(z, zif, jnp.transpose(if_p, (0, 2, 1)), gamma, l, bp, ncp, lm)
        new_p[3].append(c1p)
        new_p[4].append(n1p[:, :, 0])
        new_p[5].append(m1p[:, :M_HEADS, 0])
        tpad = ((0, 0), (0, lts - ts), (0, 0))
        zm3 = jnp.pad(jnp.transpose(zs3[:, :, Z_MQ:Z_K], (1, 0, 2)), tpad)
        if3 = jnp.pad(jnp.transpose(zif[n_p:].reshape(ts, bs, LANES), (1, 0, 2)), tpad)
        ms, s_c, n1s, m1s = _mlstm_sample(zm3, if3, jnp.transpose(if3[:, :, :2 * M_HEADS], (0, 2, 1)), gamma,
                                          state_mlstm_C, n0_all[l], m0_all[l], s_c, l, ts, bs_m)
        ms = jnp.transpose(ms[:, :ts], (1, 0, 2)).reshape(n_s, -1)
        new_s[4].append(n1s[:, :, 0])
        new_s[5].append(m1s[:, :M_HEADS, 0])

        moe = l % 2 == 1
        router = (rw_pad, rb_pad, j) if moe else None
        outs = _merge(cp, cs3.reshape(n_s, D_CONV), ap, as_, mp, ms, z, xp, xs, mod_p, mod_s,
                      wc_b, wa_b, wm_b, wo_b, pg, pb, l, rm, alpha, router)
        if moe:
            x1, h2, route = outs
            x = _moe_ffn(h2, route, moe_w_gate, moe_w_up, moe_w_down, x1, mod_p, mod_s, pg, pb,
                         l, j, rm, tb, tf_moe, alpha, split=l == depth - 1)
        else:
            x1, h2 = outs
            x = _ffn_dense(h2, fg_b, fu_b, fd_b, x1, mod_p, mod_s, pg, pb, l, j, rm, alpha)

    x_p, x_s = x if isinstance(x, (list, tuple)) else (x[:n_p], x[n_p:])
    y_p = x_p.reshape(bp, tp, d)
    y_s = jnp.transpose(x_s.reshape(ts, bs, d), (1, 0, 2))
    p_k, p_v, p_conv, p_c, p_n, p_m = [jnp.stack(a) for a in new_p]
    s_k, s_v = [a.reshape(cache_swa_k.shape) for a in s_kv]
    s_n, s_m = jnp.stack(new_s[4]), jnp.stack(new_s[5])
    return (y_p, y_s, p_k, p_v, p_conv, p_c, p_n, p_m, s_k, s_v, s_conv, s_c, s_n, s_m)
```

```python
import functools
import math

import jax
import jax.numpy as jnp
from jax import lax
from jax.experimental import pallas as pl
from jax.experimental.pallas import tpu as pltpu

F32 = jnp.float32
BF16 = jnp.bfloat16

D_MODEL = 1024
D_CONV = 512
CONV_W = 31
CONV_PAD = 32
N_HEADS = 8
N_KV = 2
HEAD_DIM = 64
Q_PER_KV = N_HEADS // N_KV
WINDOW = 128
N_BUCKETS = 32
MAX_DIST = 128
M_HEADS = 4
M_DK = 128
M_DV = 128
N_EXPERTS = 8
LN_EPS = 1e-5
LANES = 128
NEG_INF = float("-inf")
LOG2E = math.log2(math.e)
VMEM_LIMIT = 56 * 1024 * 1024

Z_G, Z_UA, Z_UB, Z_Q = 0, 3072, 3584, 4096
Z_MQ, Z_MK, Z_MV, Z_MO = 4608, 5120, 5632, 6144
Z_K, Z_V, Z_W = 6656, 6784, 6912
TN_IN = 6912
ATT_QB = 8
ATT_GROUP = 8
MOE_SUB = 256
CONV_CHUNK = 32

def _cparams(*sem):
    return pltpu.CompilerParams(dimension_semantics=sem, vmem_limit_bytes=VMEM_LIMIT)


def _sigmoid(x):
    return 1.0 / (1.0 + jnp.exp(-x))


def _silu(x):
    return x * _sigmoid(x)


def _log_sigmoid(x):
    return jnp.minimum(x, 0.0) - jnp.log(1.0 + jnp.exp(-jnp.abs(x)))


def _norm(x):
    mu = jnp.mean(x, axis=-1, keepdims=True)
    xc = x - mu
    var = jnp.mean(xc * xc, axis=-1, keepdims=True)
    return xc * lax.rsqrt(var + LN_EPS)


def _dot(a, b):
    return jnp.dot(a, b, preferred_element_type=F32)


def _dot_nt(a, b):
    return lax.dot_general(a, b, (((1,), (1,)), ((), ())), preferred_element_type=F32)


def _dot_tn(a, b):
    return lax.dot_general(a, b, (((0,), (0,)), ((), ())), preferred_element_type=F32)


def _dot_hi(a, b):
    return jnp.dot(a, b, preferred_element_type=F32, precision=lax.Precision.HIGHEST)


def _split_bf16(a):
    hi = a.astype(BF16)
    return hi, (a - hi.astype(F32)).astype(BF16)


_RARELY = pl.Buffered(1)


class _RowMap:
    def __init__(self, t, bp, tp, n_p, n_s):
        self.t = t
        self.bp = bp
        self.per_seq = tp // t
        self.np_tiles = n_p // t
        self.ns_tiles = n_s // t

    def seq(self, i):
        return jnp.minimum(i // self.per_seq, self.bp - 1)

    def prompt(self, i):
        return jnp.minimum(i, self.np_tiles - 1)

    def sample(self, i):
        return jnp.clip(i - self.np_tiles, 0, self.ns_tiles - 1)

    def mod_specs(self, l, k, d):
        return (pl.BlockSpec((None, None, 1, d), lambda i, *_: (l, self.seq(i), 0, k)),
                pl.BlockSpec((None, self.t, d), lambda i, *_: (l, self.sample(i), k), pipeline_mode=_RARELY))

    def x_specs(self, xs_arr, d):
        off = self.np_tiles if xs_arr.shape[0] > self.ns_tiles * self.t else 0
        return (pl.BlockSpec((self.t, d), lambda i, *_: (self.prompt(i), 0)),
                pl.BlockSpec((self.t, d), lambda i, *_: (off + self.sample(i), 0), pipeline_mode=_RARELY))


def _pick(is_s, p_ref, s_ref):
    return jnp.where(is_s, s_ref[...], p_ref[...])


def _ada_kernel(c_ref, w_ref, b_ref, o_ref):
    s = _silu(c_ref[...]).astype(BF16)
    o_ref[0] = _dot(s, w_ref[0].astype(BF16)) + b_ref[0]


def _ada_mod(c_all, w_ada, b_ada):
    depth, d, n6 = w_ada.shape
    rows = c_all.shape[0]
    return pl.pallas_call(
        _ada_kernel,
        grid=(depth, n6 // d),
        in_specs=[pl.BlockSpec((rows, d), lambda l, j: (0, 0)),
                  pl.BlockSpec((1, d, d), lambda l, j: (l, 0, j)),
                  pl.BlockSpec((1, 1, d), lambda l, j: (l, 0, j))],
        out_specs=pl.BlockSpec((1, rows, d), lambda l, j: (l, 0, j)),
        out_shape=jax.ShapeDtypeStruct((depth, rows, n6), F32),
        compiler_params=_cparams("parallel", "parallel"),
        name="ada_mod",
    )(c_all, w_ada, b_ada.reshape(depth, 1, n6))


def _ln_proj_kernel(xp_ref, xs_ref, shp_ref, shs_ref, scp_ref, scs_ref, w_ref, b_ref, wif_ref, bif_ref,
                    z_ref, zif_ref, h_scr, *, np_tiles):
    is_s = pl.program_id(0) >= np_tiles
    first = pl.program_id(1) == 0

    def prologue(x_ref, sh_ref, sc_ref):
        h = (_norm(x_ref[...]) * (1.0 + sc_ref[...]) + sh_ref[...]).astype(BF16)
        h_scr[...] = h
        zif_ref[...] = _dot(h, wif_ref[...]) + bif_ref[...]

    @pl.when(jnp.logical_and(first, is_s))
    def _():
        prologue(xs_ref, shs_ref, scs_ref)

    @pl.when(jnp.logical_and(first, jnp.logical_not(is_s)))
    def _():
        prologue(xp_ref, shp_ref, scp_ref)

    z_ref[...] = (_dot(h_scr[...], w_ref[...]) + b_ref[...]).astype(BF16)


def _ln_proj(xp, xs, mod_p, mod_s, w, b, wif, bif, l, rm, n):
    d = xp.shape[1]
    zw = w.shape[2]
    tm = rm.t
    shp, shs = rm.mod_specs(l, 0, d)
    scp, scs = rm.mod_specs(l, 1, d)
    wmode = _RARELY if zw == TN_IN else None
    return pl.pallas_call(
        functools.partial(_ln_proj_kernel, np_tiles=rm.np_tiles),
        grid=(n // tm, zw // TN_IN),
        in_specs=[*rm.x_specs(xs, d),
                  shp, shs, scp, scs,
                  pl.BlockSpec((None, d, TN_IN), lambda i, j: (l, 0, j), pipeline_mode=wmode),
                  pl.BlockSpec((None, 1, TN_IN), lambda i, j: (l, 0, j), pipeline_mode=wmode),
                  pl.BlockSpec((None, d, LANES), lambda i, j: (l, 0, 0), pipeline_mode=_RARELY),
                  pl.BlockSpec((None, 1, LANES), lambda i, j: (l, 0, 0), pipeline_mode=_RARELY)],
        out_specs=[pl.BlockSpec((tm, TN_IN), lambda i, j: (i, j)),
                   pl.BlockSpec((tm, LANES), lambda i, j: (i, 0))],
        out_shape=[jax.ShapeDtypeStruct((n, zw), BF16), jax.ShapeDtypeStruct((n, LANES), F32)],
        scratch_shapes=[pltpu.VMEM((tm, d), BF16)],
        compiler_params=_cparams("parallel", "arbitrary"),
        name="ln_proj",
    )(xp, xs, mod_p, mod_s, mod_p, mod_s, w, b, wif, bif)


def _conv_tail(yc, g_ref, b_ref):
    y = _norm(yc) * g_ref[...] + b_ref[...]
    return _silu(y).astype(BF16)


def _conv_prompt_kernel(ua_ref, ub_ref, st_ref, cw_ref, cb_ref, g_ref, b_ref, o_ref, ns_ref,
                        ext, shifted, yc, wrep):
    t = pl.program_id(1)
    tc = ua_ref.shape[0]
    sub = 8

    @pl.when(t == 0)
    def _():
        ext[0:CONV_PAD, :] = st_ref[0]

    @pl.when(t > 0)
    def _():
        ext[0:CONV_PAD, :] = ext[tc:tc + CONV_PAD, :]

    ext[CONV_PAD:, :] = ua_ref[...].astype(F32) * _sigmoid(ub_ref[...].astype(F32))
    for s in range(1, sub):
        shifted[s - 1] = ext[s:s + tc + CONV_PAD - sub, :]
    off = CONV_PAD - (CONV_W - 1)
    for w in range(CONV_W):
        wrep[w] = jnp.broadcast_to(cw_ref[w:w + 1, :], (sub, D_CONV))
    groups = CONV_CHUNK // sub

    for r0 in range(0, tc, CONV_CHUNK):
        acc = jnp.broadcast_to(cb_ref[...].reshape(1, 1, D_CONV), (groups, sub, D_CONV))
        for w in range(CONV_W):
            base, s = (off + w) // sub * sub, (off + w) % sub
            src = ext if s == 0 else shifted.at[s - 1]
            win = src[r0 + base:r0 + base + CONV_CHUNK, :]
            acc = acc + win.reshape(groups, sub, D_CONV) * wrep[w][None]
        yc[r0:r0 + CONV_CHUNK, :] = acc.reshape(CONV_CHUNK, D_CONV)
    o_ref[...] = _conv_tail(yc[...], g_ref, b_ref)

    @pl.when(t == pl.num_programs(1) - 1)
    def _():
        ns_ref[0] = ext[tc:tc + CONV_PAD, :]


def _conv_vec_specs(l, nargs):
    return [pl.BlockSpec((None, 1, D_CONV), lambda *_: (l, 0, 0)) for _ in range(nargs)]


def _conv_prompt(z, state_pad, cw, cb, g, b, l, bp, tp, tc):
    nt = tp // tc
    return pl.pallas_call(
        _conv_prompt_kernel,
        grid=(bp, nt),
        in_specs=[pl.BlockSpec((tc, D_CONV), lambda bb, t: (bb * nt + t, Z_UA // D_CONV)),
                  pl.BlockSpec((tc, D_CONV), lambda bb, t: (bb * nt + t, Z_UB // D_CONV)),
                  pl.BlockSpec((1, CONV_PAD, D_CONV), lambda bb, t: (bb, 0, 0)),
                  pl.BlockSpec((None, CONV_PAD, D_CONV), lambda bb, t: (l, 0, 0))] + _conv_vec_specs(l, 3),
        out_specs=[pl.BlockSpec((tc, D_CONV), lambda bb, t: (bb * nt + t, 0)),
                   pl.BlockSpec((1, CONV_PAD, D_CONV), lambda bb, t: (bb, 0, 0))],
        out_shape=[jax.ShapeDtypeStruct((bp * tp, D_CONV), BF16),
                   jax.ShapeDtypeStruct((bp, CONV_PAD, D_CONV), F32)],
        scratch_shapes=[pltpu.VMEM((tc + CONV_PAD, D_CONV), F32),
                        pltpu.VMEM((7, tc + CONV_PAD - 8, D_CONV), F32),
                        pltpu.VMEM((tc, D_CONV), F32),
                        pltpu.VMEM((CONV_PAD, 8, D_CONV), F32)],
        compiler_params=_cparams("parallel", "arbitrary"),
        name="conv_prompt",
    )(z, z, state_pad, cw, cb, g, b)


def _conv_sample_kernel(ua_ref, ub_ref, st_ref, cw_ref, cb_ref, g_ref, b_ref, *rest, first_layer):
    o_ref, ns_ref = rest[-2:]
    ts = ua_ref.shape[0]
    ns = CONV_W - 1
    a = ua_ref[...].astype(F32) * _sigmoid(ub_ref[...].astype(F32))
    st = st_ref[...]
    if first_layer is not None:
        for dd in range(ns_ref.shape[0]):
            if dd != first_layer:
                ns_ref[dd] = jnp.zeros(ns_ref.shape[1:], F32)
        ns_ref = ns_ref.at[first_layer]
    ns_ref[:, 0:ns - ts, :] = st[:, ts:, :]
    for t in range(ts):
        ns_ref[:, ns - ts + t, :] = a[t]
    row = lax.broadcasted_iota(jnp.int32, (ns, D_CONV), 0)
    for t in range(ts):
        wt = jnp.zeros((ns, D_CONV), F32)
        for j in range(t, ns):
            wt = jnp.where(row == j, cw_ref[j - t:j - t + 1, :], wt)
        yc = jnp.sum(st * wt[None], axis=1) + cb_ref[...]
        for t2 in range(t + 1):
            wi = CONV_W - 1 - (t - t2)
            yc = yc + a[t2] * cw_ref[wi:wi + 1, :]
        o_ref[t] = _conv_tail(yc, g_ref, b_ref)


def _conv_sample(zs3, state, cw, cb, g, b, ns_prev, l, bs_blk):
    ts, bs, _ = zs3.shape
    depth = state.shape[0]
    ns = CONV_W - 1
    in_specs = [pl.BlockSpec((ts, bs_blk, D_CONV), lambda i: (0, i, Z_UA // D_CONV)),
                pl.BlockSpec((ts, bs_blk, D_CONV), lambda i: (0, i, Z_UB // D_CONV)),
                pl.BlockSpec((None, bs_blk, ns, D_CONV), lambda i: (l, i, 0, 0)),
                pl.BlockSpec((None, CONV_PAD, D_CONV), lambda i: (l, 0, 0))] + _conv_vec_specs(l, 3)
    args = [zs3, zs3, state, cw, cb, g, b]
    aliases = {}
    if ns_prev is None:
        ns_spec = pl.BlockSpec((depth, bs_blk, ns, D_CONV), lambda i: (0, i, 0, 0))
    else:
        in_specs.append(pl.BlockSpec(memory_space=pl.ANY))
        args.append(ns_prev)
        aliases = {len(args) - 1: 1}
        ns_spec = pl.BlockSpec((None, bs_blk, ns, D_CONV), lambda i: (l, i, 0, 0))
    return pl.pallas_call(
        functools.partial(_conv_sample_kernel, first_layer=l if ns_prev is None else None),
        grid=(bs // bs_blk,),
        in_specs=in_specs,
        out_specs=[pl.BlockSpec((ts, bs_blk, D_CONV), lambda i: (0, i, 0)), ns_spec],
        out_shape=[jax.ShapeDtypeStruct((ts, bs, D_CONV), BF16),
                   jax.ShapeDtypeStruct((depth, bs, ns, D_CONV), F32)],
        input_output_aliases=aliases,
        compiler_params=_cparams("parallel"),
        name="conv_sample",
    )(*args)


def _t5_bucket(dist):
    max_exact = N_BUCKETS // 2
    d = jnp.maximum(dist, 0)
    large = max_exact + (jnp.log(jnp.maximum(d, 1).astype(F32) / max_exact)
                         / math.log(MAX_DIST / max_exact) * (N_BUCKETS - max_exact)).astype(jnp.int32)
    return jnp.where(d < max_exact, d, jnp.minimum(large, N_BUCKETS - 1))


def _bias_heads(rel_bias, dist, valid):
    onehot = (_t5_bucket(dist)[..., None] == jnp.arange(N_BUCKETS)).astype(F32)
    bias = jnp.einsum("qkb,bh->qkh", onehot, rel_bias.astype(F32), precision=lax.Precision.HIGHEST)
    bias = jnp.where(valid[..., None], bias, NEG_INF)
    return jnp.transpose(bias, (2, 0, 1))


def _attn_prompt_kernel(sink_ref, q_ref, kc_ref, kp_ref, vc_ref, vp_ref, bias_ref, o_ref, *, l):
    first = pl.program_id(1) == 0
    w = WINDOW
    nq = q_ref.shape[0] // w
    kall = jnp.concatenate([kp_ref[...], kc_ref[...]], axis=0).astype(F32)
    vall = jnp.concatenate([vp_ref[...], vc_ref[...]], axis=0).astype(F32)
    lane = lax.broadcasted_iota(jnp.int32, kall.shape, 1)
    lo = lane < HEAD_DIM
    kroll = pltpu.roll(kall, HEAD_DIM, 1)
    vroll = pltpu.roll(vall, HEAD_DIM, 1)

    def halves(a, aroll, g):
        if g == 0:
            return jnp.where(lo, a, 0.0).astype(BF16), jnp.where(lo, 0.0, aroll).astype(BF16)
        return jnp.where(lo, aroll, 0.0).astype(BF16), jnp.where(lo, 0.0, a).astype(BF16)

    kh = [halves(kall, kroll, g) for g in range(N_KV)]
    vh = [halves(vall, vroll, g) for g in range(N_KV)]
    col = lax.broadcasted_iota(jnp.int32, (w, 4 * w), 1)
    prev_col = (col % (2 * w)) < w
    tiles_per_g = Q_PER_KV // 2
    units = [(qi, tile) for qi in range(nq) for tile in range(N_HEADS // 2)]
    for u0 in range(0, len(units), ATT_GROUP):
        group = units[u0:u0 + ATT_GROUP]
        scores = []
        for qi, tile in group:
            r0, g = qi * w, tile // tiles_per_g
            q = q_ref[r0:r0 + w, tile * LANES:(tile + 1) * LANES]
            kk = jnp.concatenate([kh[g][0][r0:r0 + 2 * w], kh[g][1][r0:r0 + 2 * w]], axis=0)
            s = _dot_nt(q, kk) * (HEAD_DIM ** -0.5 * LOG2E) + bias_ref[tile]
            if qi == 0:
                s = jnp.where(jnp.logical_and(first, prev_col), NEG_INF, s)
            scores.append(s)
        probs = []
        for (qi, tile), s in zip(group, scores):
            ps = []
            for half in range(2):
                sh = s[:, half * 2 * w:(half + 1) * 2 * w]
                sink = sink_ref[l, 2 * tile + half]
                mx = jnp.maximum(jnp.max(sh, axis=-1, keepdims=True), sink)
                p = jnp.exp2(sh - mx)
                den = jnp.sum(p, axis=-1, keepdims=True) + jnp.exp2(sink - mx)
                ps.append((p * (1.0 / den)).astype(BF16))
            probs.append(jnp.concatenate(ps, axis=1))
        for (qi, tile), p in zip(group, probs):
            r0, g = qi * w, tile // tiles_per_g
            vv = jnp.concatenate([vh[g][0][r0:r0 + 2 * w], vh[g][1][r0:r0 + 2 * w]], axis=0)
            o_ref[r0:r0 + w, tile * LANES:(tile + 1) * LANES] = _dot(p, vv).astype(BF16)


def _attn_prompt(z, bias, sinks, l, bp, tp):
    w = WINDOW
    qb = ATT_QB if tp % (ATT_QB * w) == 0 else 1
    ns = tp // (qb * w)
    nb = tp // w
    kvw = N_KV * HEAD_DIM
    qw = N_HEADS * HEAD_DIM

    def cur(col):
        return lambda bb, i: (bb * ns + i, col)

    def prev(col):
        return lambda bb, i: (bb * nb + jnp.maximum(i * qb - 1, 0), col)

    return pl.pallas_call(
        functools.partial(_attn_prompt_kernel, l=l),
        grid=(bp, ns),
        in_specs=[pl.BlockSpec(memory_space=pltpu.SMEM),
                  pl.BlockSpec((qb * w, qw), cur(Z_Q // qw)),
                  pl.BlockSpec((qb * w, kvw), cur(Z_K // kvw)),
                  pl.BlockSpec((w, kvw), prev(Z_K // kvw)),
                  pl.BlockSpec((qb * w, kvw), cur(Z_V // kvw)),
                  pl.BlockSpec((w, kvw), prev(Z_V // kvw)),
                  pl.BlockSpec((N_HEADS // 2, w, 4 * w), lambda bb, i: (0, 0, 0))],
        out_specs=pl.BlockSpec((qb * w, qw), lambda bb, i: (bb * ns + i, 0)),
        out_shape=jax.ShapeDtypeStruct((bp * tp, qw), BF16),
        compiler_params=_cparams("parallel", "parallel"),
        name="attn_prompt",
    )(sinks, z, z, z, z, z, bias)


def _attn_sample_kernel(q_ref, kc_ref, vc_ref, kn_ref, vn_ref, bias_ref, sink_ref, kt_ref, vt_ref, *rest,
                        first_layer):
    o_ref, nk_ref, nv_ref = rest[-3:]
    wb = kc_ref.shape[1]
    ts = kn_ref.shape[2]
    for new_ref, old_ref, t_ref in ((nk_ref, kc_ref, kt_ref), (nv_ref, vc_ref, vt_ref)):
        if first_layer is not None:
            for dd in range(new_ref.shape[0]):
                if dd != first_layer:
                    new_ref[dd] = jnp.zeros(new_ref.shape[1:], F32)
            new_ref = new_ref.at[first_layer]
        new_ref[:, 0:wb - ts, :] = old_ref[:, ts:, :]
        for t in range(ts):
            new_ref[:, wb - ts + t, :] = t_ref[t].astype(F32)
    for g in range(N_KV):
        lo = g * HEAD_DIM
        qb = (q_ref[:, g].astype(F32) * (HEAD_DIM ** -0.5)).astype(BF16)
        kc = kc_ref[:, :, lo:lo + HEAD_DIM].astype(BF16)
        vc = vc_ref[:, :, lo:lo + HEAD_DIM].astype(BF16)
        kn = kn_ref[:, g].astype(F32)
        vn = vn_ref[:, g].astype(F32)
        bias = bias_ref[g]
        s_c = jnp.einsum("bqd,bkd->bqk", qb, kc, preferred_element_type=F32) + bias[None, :, :wb]
        qf = qb.astype(F32)
        s_n = [jnp.sum(qf * kn[:, j:j + 1, :], axis=-1, keepdims=True) + bias[None, :, wb + j:wb + j + 1]
               for j in range(ts)]
        sink = sink_ref[g][None]
        mx = jnp.maximum(jnp.max(s_c, axis=-1, keepdims=True), sink)
        for sj in s_n:
            mx = jnp.maximum(mx, sj)
        p_c = jnp.exp(s_c - mx)
        p_n = [jnp.exp(sj - mx) for sj in s_n]
        den = jnp.sum(p_c, axis=-1, keepdims=True) + jnp.exp(sink - mx)
        for pj in p_n:
            den = den + pj
        o = jnp.einsum("bqk,bkd->bqd", (p_c / den).astype(BF16), vc, preferred_element_type=F32)
        for j in range(ts):
            o = o + (p_n[j] / den).astype(BF16).astype(F32) * vn[:, j:j + 1, :]
        o_ref[:, g] = o


def _attn_sample(q4, kc, vc, kn, vn, bias, sinks, kt, vt, new_prev, l, bs_blk):
    bs, _, rt, _ = q4.shape
    depth, _, wb, kvw = kc.shape
    ts = kn.shape[2]
    tmaj = pl.BlockSpec((ts, bs_blk, kvw), lambda i: (0, i, 0))
    in_specs = [pl.BlockSpec((bs_blk, N_KV, rt, HEAD_DIM), lambda i: (i, 0, 0, 0)),
                pl.BlockSpec((None, bs_blk, wb, kvw), lambda i: (l, i, 0, 0)),
                pl.BlockSpec((None, bs_blk, wb, kvw), lambda i: (l, i, 0, 0)),
                pl.BlockSpec((bs_blk, N_KV, ts, HEAD_DIM), lambda i: (i, 0, 0, 0)),
                pl.BlockSpec((bs_blk, N_KV, ts, HEAD_DIM), lambda i: (i, 0, 0, 0)),
                pl.BlockSpec((N_KV, rt, wb + ts), lambda i: (0, 0, 0)),
                pl.BlockSpec((N_KV, rt, 1), lambda i: (0, 0, 0)),
                tmaj, tmaj]
    args = [q4, kc, vc, kn, vn, bias, sinks, kt, vt]
    aliases = {}
    if new_prev is None:
        new_spec = pl.BlockSpec((depth, bs_blk, wb, kvw), lambda i: (0, i, 0, 0))
    else:
        in_specs += [pl.BlockSpec(memory_space=pl.ANY)] * 2
        args += list(new_prev)
        aliases = {len(args) - 2: 1, len(args) - 1: 2}
        new_spec = pl.BlockSpec((None, bs_blk, wb, kvw), lambda i: (l, i, 0, 0))
    new_shape = jax.ShapeDtypeStruct((depth, bs, wb, kvw), F32)
    outs = pl.pallas_call(
        functools.partial(_attn_sample_kernel, first_layer=l if new_prev is None else None),
        grid=(bs // bs_blk,),
        in_specs=in_specs,
        out_specs=[pl.BlockSpec((bs_blk, N_KV, rt, HEAD_DIM), lambda i: (i, 0, 0, 0)), new_spec, new_spec],
        out_shape=[jax.ShapeDtypeStruct((bs, N_KV, rt, HEAD_DIM), F32), new_shape, new_shape],
        input_output_aliases=aliases,
        compiler_params=_cparams("parallel"),
        name="attn_sample",
    )(*args)
    return outs[0], (outs[1], outs[2])


def _mlstm_kernel(*refs, t_valid, nseq, aliased, carried, chunk_axis, per_seq_inputs=False, first_layer=None):
    if aliased:
        refs = refs[:10] + refs[11:]
    if per_seq_inputs:
        groups = [refs[6 * s:6 * s + 6] for s in range(nseq)]
        if_ref, ifr_ref, q_ref, k_ref, v_ref, o_ref = (tuple(g[k] for g in groups) for k in range(6))
        refs = (None,) * 6 + refs[6 * nseq:]
    else:
        if_ref, ifr_ref, q_ref, k_ref, v_ref, o_ref = refs[:6]
    g_ref, c0_ref, n0_ref, m0_ref, h_ref, c1_ref, n1_ref, m1_ref = refs[6:14]
    c = pl.program_id(chunk_axis)
    L = (ifr_ref[0] if per_seq_inputs else ifr_ref).shape[-1]
    if carried:
        c_in, n_in, m_in = c_out, n_out, m_out = refs[14:]

        @pl.when(c == 0)
        def _():
            c_in[...] = c0_ref[...]
            n_in[...] = n0_ref[...]
            m_in[...] = m0_ref[...]
    else:
        (c_in, n_in, m_in), (c_out, n_out, m_out) = (c0_ref, n0_ref, m0_ref), (c1_ref, n1_ref, m1_ref)
        if first_layer is not None:
            for dd in range(c1_ref.shape[0]):
                if dd != first_layer:
                    c1_ref[dd] = jnp.zeros(c1_ref.shape[1:], F32)
            c_out = c1_ref.at[first_layer]

    def seq(ref, s):
        return ref[s] if isinstance(ref, tuple) else ref.at[s]

    tt = lax.broadcasted_iota(jnp.int32, (L, L), 0)
    ss = lax.broadcasted_iota(jnp.int32, (L, L), 1)
    causal = ss <= tt
    tril = causal.astype(F32)
    triu = (tt <= ss).astype(F32)
    seqs = range(nseq)
    heads = [(s_i, h) for s_i in seqs for h in range(M_HEADS)]
    mxu_sums = L % LANES == 0
    gates = []
    for s_i in seqs:
        ifc = seq(if_ref, s_i)[...]
        ifr = ifr_ref[s_i][0] if per_seq_inputs else ifr_ref[s_i]
        lf_c = _log_sigmoid(ifc)
        lf_r = _log_sigmoid(ifr)
        i_c, i_r = ifc, ifr
        if t_valid < L:
            rc = lax.broadcasted_iota(jnp.int32, (L, LANES), 0) < t_valid
            rr = lax.broadcasted_iota(jnp.int32, (2 * M_HEADS, L), 1) < t_valid
            lf_c = jnp.where(rc, lf_c, 0.0)
            lf_r = jnp.where(rr, lf_r, 0.0)
            i_c = jnp.where(rc, i_c, NEG_INF)
            i_r = jnp.where(rr, i_r, NEG_INF)
        gates.append((lf_c, lf_r, i_c, i_r))
    f_cs = [_dot_hi(tril, g[0]) for g in gates]
    f_rs = [_dot_hi(g[1], triu) for g in gates]
    state = {(s_i, h): (c_in[s_i, h], n_in[s_i, h], m_in[s_i, h:h + 1, :]) for s_i, h in heads}

    def wide(col):
        return jnp.concatenate([col] * (L // LANES), axis=1) if L >= LANES else col[:, :L]

    st1 = {}
    for s_i, h in heads:
        _, nrow, m0 = state[s_i, h]
        fc = jnp.broadcast_to(f_cs[s_i][:, M_HEADS + h:M_HEADS + h + 1], (L, LANES))
        fr = f_rs[s_i][M_HEADS + h:M_HEADS + h + 1, :]
        ir = gates[s_i][3][h:h + 1, :]
        dm = jnp.where(causal, wide(fc) - fr + ir, NEG_INF)
        m_t = jnp.maximum(m0 + fc, jnp.max(dm, axis=-1, keepdims=True))
        st1[s_i, h] = (fc, dm, m_t, jnp.exp(m0 + fc - m_t))
    st2 = {}
    for s_i, h in heads:
        lo = h * M_DK
        fc, dm, m_t, inter = st1[s_i, h]
        qb = seq(q_ref, s_i)[:, lo:lo + M_DK]
        kf = seq(k_ref, s_i)[:, lo:lo + M_DK].astype(F32) * (M_DK ** -0.5)
        vf = seq(v_ref, s_i)[:, lo:lo + M_DV].astype(F32)
        kb, vb = kf.astype(BF16), vf.astype(BF16)
        if mxu_sums:
            n_rows = jnp.broadcast_to(state[s_i, h][1], (M_DK, M_DK)).astype(BF16)
            qk = _dot_nt(qb, jnp.concatenate([kb, n_rows], axis=0))
            sc, qn_rep = qk[:, :L] * jnp.exp(dm - wide(m_t)), qk[:, L:]
        else:
            sc, qn_rep = _dot_nt(qb, kb) * jnp.exp(dm - wide(m_t)), None
        st2[s_i, h] = (qb, kf, vf, kb, vb, sc, qn_rep)
    st3 = {}
    for s_i, h in heads:
        cm, nrow, m0 = state[s_i, h]
        fc, dm, m_t, inter = st1[s_i, h]
        qb, kf, vf, kb, vb, sc, qn_rep = st2[s_i, h]
        if mxu_sums:
            sv = _dot(sc.astype(BF16), jnp.concatenate([vb, jnp.ones((L, M_DV), BF16)], axis=1))
            num = inter * _dot(qb, cm.astype(BF16)) + sv[:, :M_DV]
            qn = inter * qn_rep + sv[:, M_DV:]
            floor = jnp.exp(-m_t)
        else:
            num = inter * _dot(qb, cm.astype(BF16)) + _dot(sc.astype(BF16), vb)
            qn = (inter * jnp.sum(qb.astype(F32) * nrow, axis=-1, keepdims=True)
                  + jnp.sum(sc, axis=-1, keepdims=True))
            floor = jnp.exp(-m_t)
        st3[s_i, h] = num / jnp.maximum(jnp.abs(qn), floor)
    new_state = {}
    for s_i, h in heads:
        cm, nrow, m0 = state[s_i, h]
        fc, dm, m_t, inter = st1[s_i, h]
        qb, kf, vf, kb, vb, sc, _ = st2[s_i, h]
        ic = jnp.broadcast_to(gates[s_i][2][:, h:h + 1], (L, LANES))
        m_end = m_t[L - 1:L, :]
        f_end = fc[L - 1:L, :]
        decay = jnp.exp(m0 + f_end - m_end)
        w_s = jnp.exp(f_end - fc + ic - m_end)
        new_state[s_i, h] = (decay * cm + _dot_tn(kb, (w_s * vf).astype(BF16)),
                             decay * nrow + jnp.sum(w_s * kf, axis=0, keepdims=True),
                             m_end)
    for s_i, h in heads:
        lo = h * M_DK
        hn = _norm(st3[s_i, h]) * g_ref[:, lo:lo + M_DV]
        gate = _sigmoid(seq(o_ref, s_i)[:, lo:lo + M_DV].astype(F32))
        seq(h_ref, s_i)[:, lo:lo + M_DV] = (gate * hn).astype(BF16)
    for s_i, h in heads:
        c_out[s_i, h], n_out[s_i, h], m_out[s_i, h:h + 1, :] = new_state[s_i, h]
    if not carried:
        for s_i in seqs:
            m_out[s_i, M_HEADS:, :] = jnp.zeros((M_HEADS, LANES), F32)

    if carried:
        @pl.when(c == pl.num_programs(chunk_axis) - 1)
        def _():
            c1_ref[...] = c_out[...]
            n1_ref[...] = n_out[...]
            m1_ref[...] = m_out[...]


def _mlstm_state_specs(nseq, l_state):
    if l_state is None:
        c_spec = pl.BlockSpec((nseq, M_HEADS, M_DK, M_DV), lambda b, c: (b, 0, 0, 0))
    else:
        c_spec = pl.BlockSpec((None, nseq, M_HEADS, M_DK, M_DV), lambda b, c: (l_state, b, 0, 0, 0))
    return (c_spec,
            pl.BlockSpec((nseq, M_HEADS, 1, M_DK), lambda b, c: (b, 0, 0, 0)),
            pl.BlockSpec((nseq, 2 * M_HEADS, LANES), lambda b, c: (b, 0, 0)))


def _mlstm_scratch(nseq):
    return [pltpu.VMEM((nseq, M_HEADS, M_DK, M_DV), F32),
            pltpu.VMEM((nseq, M_HEADS, 1, M_DK), F32),
            pltpu.VMEM((nseq, 2 * M_HEADS, LANES), F32)]


def _mlstm_prompt(z, zif, ifr, gamma, l, bp, nc, L):
    hw = M_HEADS * M_DK
    zero = lambda *s: jnp.zeros(s, F32)
    in_specs, args = [], []
    for b in range(bp):
        def rows(col, b=b):
            return lambda c: (b * nc + c, col)

        in_specs += [pl.BlockSpec((L, LANES), rows(0)),
                     pl.BlockSpec((1, 2 * M_HEADS, L), lambda c, b=b: (b * nc + c, 0, 0)),
                     pl.BlockSpec((L, hw), rows(Z_MQ // hw)),
                     pl.BlockSpec((L, hw), rows(Z_MK // hw)),
                     pl.BlockSpec((L, hw), rows(Z_MV // hw)),
                     pl.BlockSpec((L, hw), rows(Z_MO // hw))]
        args += [zif, ifr, z, z, z, z]
    state_specs = [pl.BlockSpec((bp, M_HEADS, M_DK, M_DV), lambda c: (0, 0, 0, 0)),
                   pl.BlockSpec((bp, M_HEADS, 1, M_DK), lambda c: (0, 0, 0, 0)),
                   pl.BlockSpec((bp, 2 * M_HEADS, LANES), lambda c: (0, 0, 0))]
    kern = functools.partial(_mlstm_kernel, t_valid=L, nseq=bp, aliased=False, carried=True, chunk_axis=0,
                             per_seq_inputs=True)
    outs = pl.pallas_call(
        kern,
        grid=(nc,),
        in_specs=in_specs + [pl.BlockSpec((None, 1, hw), lambda c: (l, 0, 0))] + state_specs,
        out_specs=[pl.BlockSpec((bp, L, hw), lambda c: (0, c, 0))] + state_specs,
        out_shape=[jax.ShapeDtypeStruct((bp, nc * L, hw), BF16),
                   jax.ShapeDtypeStruct((bp, M_HEADS, M_DK, M_DV), F32),
                   jax.ShapeDtypeStruct((bp, M_HEADS, 1, M_DK), F32),
                   jax.ShapeDtypeStruct((bp, 2 * M_HEADS, LANES), F32)],
        scratch_shapes=_mlstm_scratch(bp),
        compiler_params=_cparams("arbitrary"),
        name="mlstm_prompt",
    )(*args, gamma, zero(bp, M_HEADS, M_DK, M_DV), zero(bp, M_HEADS, 1, M_DK), zero(bp, 2 * M_HEADS, LANES))
    return (outs[0].reshape(bp * nc * L, hw),) + tuple(outs[1:])


def _mlstm_sample(zm3, if3, ifr, gamma, c_all, n0, m0x, c_out_prev, l, t_valid, nseq):
    bs, L, _ = zm3.shape
    depth = c_all.shape[0]
    hw = M_HEADS * M_DK
    aliased = c_out_prev is not None

    def blk(col):
        return pl.BlockSpec((nseq, L, hw), lambda b, c: (b, 0, col))

    c_in, n_spec, m_spec = _mlstm_state_specs(nseq, l)
    in_specs = [pl.BlockSpec((nseq, L, LANES), lambda b, c: (b, 0, 0)),
                pl.BlockSpec((nseq, 2 * M_HEADS, L), lambda b, c: (b, 0, 0)),
                blk(0), blk(1), blk(2), blk(3),
                pl.BlockSpec((None, 1, hw), lambda b, c: (l, 0, 0)),
                c_in, n_spec, m_spec]
    args = [if3, ifr, zm3, zm3, zm3, zm3, gamma, c_all, n0, m0x]
    aliases = {}
    if aliased:
        in_specs.append(pl.BlockSpec(memory_space=pl.ANY))
        args.append(c_out_prev)
        aliases = {len(args) - 1: 1}
        c_out = c_in
    else:
        c_out = pl.BlockSpec((depth, nseq, M_HEADS, M_DK, M_DV), lambda b, c: (0, b, 0, 0, 0))
    kern = functools.partial(_mlstm_kernel, t_valid=t_valid, nseq=nseq, aliased=aliased, carried=False,
                             chunk_axis=1, first_layer=None if aliased else l)
    return pl.pallas_call(
        kern,
        grid=(bs // nseq, 1),
        in_specs=in_specs,
        out_specs=[pl.BlockSpec((nseq, L, hw), lambda b, c: (b, 0, 0)), c_out, n_spec, m_spec],
        out_shape=[jax.ShapeDtypeStruct((bs, L, hw), BF16),
                   jax.ShapeDtypeStruct((depth, bs, M_HEADS, M_DK, M_DV), F32),
                   jax.ShapeDtypeStruct((bs, M_HEADS, 1, M_DK), F32),
                   jax.ShapeDtypeStruct((bs, 2 * M_HEADS, LANES), F32)],
        input_output_aliases=aliases,
        compiler_params=_cparams("parallel", "arbitrary"),
        name="mlstm_sample",
    )(*args)


def _m_state_in(m):
    lead = m.shape[:-1]
    mx = jnp.zeros(lead + (2 * M_HEADS, LANES), F32)
    return mx.at[..., :M_HEADS, :].set(jnp.broadcast_to(m[..., None], lead + (M_HEADS, LANES)))


def _merge_kernel(*refs, np_tiles, alpha, route):
    (cp_ref, cs_ref, ap_ref, as_ref, mp_ref, ms_ref, g0_ref, g1_ref, g2_ref, xp_ref, xs_ref,
     gtp_ref, gts_ref, shp_ref, shs_ref, scp_ref, scs_ref,
     wc_ref, wa_ref, wm_ref, wo_ref, pg_ref, pb_ref) = refs[:23]
    if route:
        rw_ref, rb_ref, x1_ref, h_ref, route_ref = refs[23:]
    else:
        x1_ref, h_ref = refs[23:]
    is_s = pl.program_id(0) >= np_tiles

    def gate(ref):
        return _sigmoid(ref[...].astype(F32))

    def run(c_ref, a_ref, m_ref, x_ref, gt_ref, sh_ref, sc_ref):
        y = (gate(g0_ref) * _dot(c_ref[...], wc_ref[...])
             + gate(g1_ref) * _dot(a_ref[...], wa_ref[...])
             + gate(g2_ref) * _dot(m_ref[...], wm_ref[...]))
        mix = _dot(y.astype(BF16), wo_ref[...])
        x1 = _norm(alpha * x_ref[...] + gt_ref[...] * mix) * pg_ref[...] + pb_ref[...]
        x1_ref[...] = x1
        h = _norm(x1) * (1.0 + sc_ref[...]) + sh_ref[...]
        h_ref[...] = h.astype(h_ref.dtype)
        if route:
            h_hi, h_lo = _split_bf16(h)
            w_hi, w_lo = _split_bf16(rw_ref[...])
            logits = _dot(h_hi, w_hi) + (_dot(h_hi, w_lo) + _dot(h_lo, w_hi)) + rb_ref[...]
            lane = lax.broadcasted_iota(jnp.int32, logits.shape, 1)
            m1 = jnp.max(logits, axis=-1, keepdims=True)
            e1 = jnp.min(jnp.where(logits == m1, lane, LANES), axis=-1, keepdims=True)
            l2 = jnp.where(lane == e1, NEG_INF, logits)
            m2 = jnp.max(l2, axis=-1, keepdims=True)
            e2 = jnp.min(jnp.where(l2 == m2, lane, LANES), axis=-1, keepdims=True)
            ex = jnp.exp(m2 - m1)
            w1 = 1.0 / (1.0 + ex)
            w2 = ex / (1.0 + ex)
            out = jnp.where(lane == 0, e1.astype(F32),
                            jnp.where(lane == 1, e2.astype(F32),
                                      jnp.where(lane == 2, w1, jnp.where(lane == 3, w2, 0.0))))
            route_ref[...] = out

    @pl.when(is_s)
    def _():
        run(cs_ref, as_ref, ms_ref, xs_ref, gts_ref, shs_ref, scs_ref)

    @pl.when(jnp.logical_not(is_s))
    def _():
        run(cp_ref, ap_ref, mp_ref, xp_ref, gtp_ref, shp_ref, scp_ref)


def _merge(cp, cs, ap, as_, mp, ms, z, xp, xs, mod_p, mod_s, wc, wa, wm, wo, pg, pb, l, rm, alpha, router):
    n, d = z.shape[0], xp.shape[1]
    hw = D_CONV
    route = router is not None
    tm = rm.t

    def pblk():
        return pl.BlockSpec((tm, hw), lambda i: (rm.prompt(i), 0))

    def sblk():
        return pl.BlockSpec((tm, hw), lambda i: (rm.sample(i), 0), pipeline_mode=_RARELY)

    def zg(k):
        return pl.BlockSpec((tm, d), lambda i: (i, Z_G // d + k))

    def lw(a):
        return pl.BlockSpec((None,) + a.shape[1:], lambda i: (l,) + (0,) * (a.ndim - 1), pipeline_mode=_RARELY)

    post = pl.BlockSpec((None, None, 1, d), lambda i: (l, 0, 0, 0))
    in_specs = [pblk(), sblk(), pblk(), sblk(), pblk(), sblk(), zg(0), zg(1), zg(2),
                *rm.x_specs(xs, d),
                *rm.mod_specs(l, 2, d), *rm.mod_specs(l, 3, d), *rm.mod_specs(l, 4, d),
                lw(wc), lw(wa), lw(wm), lw(wo), post, post]
    args = [cp, cs, ap, as_, mp, ms, z, z, z, xp, xs, mod_p, mod_s, mod_p, mod_s, mod_p, mod_s,
            wc, wa, wm, wo, pg, pb]
    out_specs = [pl.BlockSpec((tm, d), lambda i: (i, 0)), pl.BlockSpec((tm, d), lambda i: (i, 0))]
    out_shape = [jax.ShapeDtypeStruct((n, d), F32), jax.ShapeDtypeStruct((n, d), F32 if route else BF16)]
    if route:
        rw, rb, lj = router
        in_specs += [pl.BlockSpec((None, d, LANES), lambda i: (lj, 0, 0)),
                     pl.BlockSpec((None, 1, LANES), lambda i: (lj, 0, 0))]
        args += [rw, rb]
        out_specs.append(pl.BlockSpec((tm, LANES), lambda i: (i, 0)))
        out_shape.append(jax.ShapeDtypeStruct((n, LANES), F32))
    kern = functools.partial(_merge_kernel, np_tiles=rm.np_tiles, alpha=alpha, route=route)
    return pl.pallas_call(
        kern, grid=(n // tm,), in_specs=in_specs, out_specs=out_specs, out_shape=out_shape,
        compiler_params=_cparams("parallel"), name="merge_route" if route else "merge",
    )(*args)


def _post_residual(x_ref, gp_ref, gs_ref, f, pg_ref, pb_ref, is_s, alpha):
    return _norm(alpha * x_ref[...] + _pick(is_s, gp_ref, gs_ref) * f) * pg_ref[...] + pb_ref[...]


def _ffn_kernel(h_ref, wg_ref, wu_ref, wd_ref, x_ref, gp_ref, gs_ref, pg_ref, pb_ref, o_ref, *, alpha, np_tiles):
    h = h_ref[...]
    a = (_silu(_dot(h, wg_ref[...])) * _dot(h, wu_ref[...])).astype(BF16)
    is_s = pl.program_id(0) >= np_tiles
    o_ref[...] = _post_residual(x_ref, gp_ref, gs_ref, _dot(a, wd_ref[...]), pg_ref, pb_ref, is_s, alpha)


def _ffn_dense(h, wg, wu, wd, x, mod_p, mod_s, pg, pb, l, lj, rm, alpha):
    n, d = x.shape
    f = wg.shape[2]
    tm = rm.t
    post = pl.BlockSpec((None, None, 1, d), lambda i: (l, 1, 0, 0))
    return pl.pallas_call(
        functools.partial(_ffn_kernel, alpha=alpha, np_tiles=rm.np_tiles),
        grid=(n // tm,),
        in_specs=[pl.BlockSpec((tm, d), lambda i: (i, 0)),
                  pl.BlockSpec((None, d, f), lambda i: (lj, 0, 0), pipeline_mode=_RARELY),
                  pl.BlockSpec((None, d, f), lambda i: (lj, 0, 0), pipeline_mode=_RARELY),
                  pl.BlockSpec((None, f, d), lambda i: (lj, 0, 0), pipeline_mode=_RARELY),
                  pl.BlockSpec((tm, d), lambda i: (i, 0)),
                  *rm.mod_specs(l, 5, d), post, post],
        out_specs=pl.BlockSpec((tm, d), lambda i: (i, 0)),
        out_shape=jax.ShapeDtypeStruct((n, d), F32),
        compiler_params=_cparams("parallel"),
        name="ffn_dense",
    )(h, wg, wu, wd, x, mod_p, mod_s, pg, pb)


def _rank_kernel(route_ref, rank_ref, tot_ref, carry):
    i = pl.program_id(0)
    tm = route_ref.shape[0]

    @pl.when(i == 0)
    def _():
        carry[...] = jnp.zeros_like(carry)

    r = route_ref[...]
    lane = lax.broadcasted_iota(jnp.int32, (tm, LANES), 1)
    e1 = r[:, 0:1].astype(jnp.int32)
    e2 = r[:, 1:2].astype(jnp.int32)
    hit1 = lane == e1
    hit2 = lane == e2
    onehot = jnp.where(jnp.logical_or(hit1, hit2), 1.0, 0.0)
    tt = lax.broadcasted_iota(jnp.int32, (tm, tm), 0)
    ss = lax.broadcasted_iota(jnp.int32, (tm, tm), 1)
    before = jnp.where(ss < tt, 1.0, 0.0).astype(BF16)
    cnt = _dot(before, onehot.astype(BF16)) + carry[0:1, :]
    r1 = jnp.sum(jnp.where(hit1, cnt, 0.0), axis=-1, keepdims=True)
    r2 = jnp.sum(jnp.where(hit2, cnt, 0.0), axis=-1, keepdims=True)
    rank_ref[...] = jnp.where(lane == 0, r1, jnp.where(lane == 1, r2, 0.0))
    carry[...] = carry[...] + jnp.sum(onehot, axis=0, keepdims=True)
    tot_ref[...] = carry[...]


def _moe_rank(route, tm):
    n = route.shape[0]
    return pl.pallas_call(
        _rank_kernel,
        grid=(n // tm,),
        in_specs=[pl.BlockSpec((tm, LANES), lambda i: (i, 0))],
        out_specs=[pl.BlockSpec((tm, LANES), lambda i: (i, 0)),
                   pl.BlockSpec((8, LANES), lambda i: (0, 0))],
        out_shape=[jax.ShapeDtypeStruct((n, LANES), F32), jax.ShapeDtypeStruct((8, LANES), F32)],
        scratch_shapes=[pltpu.VMEM((8, LANES), F32)],
        compiler_params=_cparams("arbitrary"),
        name="moe_rank",
    )(route)


def _row_copy(src, s, dst, t, sem):
    return pltpu.make_async_copy(src.at[pl.ds(s, 1)], dst.at[pl.ds(t, 1)], sem)


def _dispatch_kernel(d1_ref, d2_ref, zl_ref, h_ref, xs_hbm, zbuf, stage, sem, zsem):
    i = pl.program_id(0)
    tm = h_ref.shape[0]

    @pl.when(i == 0)
    def _():
        zbuf[...] = jnp.zeros_like(zbuf)

        def zero_copy(k):
            row = pl.multiple_of(jnp.maximum(zl_ref[k], 0), MOE_SUB)
            return pltpu.make_async_copy(zbuf, xs_hbm.at[pl.ds(row, MOE_SUB)], zsem)

        def start(k, carry):
            @pl.when(zl_ref[k] >= 0)
            def _():
                zero_copy(k).start()
            return carry

        def wait(k, carry):
            @pl.when(zl_ref[k] >= 0)
            def _():
                zero_copy(k).wait()
            return carry

        lax.fori_loop(0, zl_ref.shape[0], start, 0)
        lax.fori_loop(0, zl_ref.shape[0], wait, 0)

    slot = i % 2
    stage[slot] = h_ref[...]

    def issue(r, carry):
        _row_copy(stage.at[slot], r, xs_hbm, d1_ref[i * tm + r], sem.at[slot]).start(priority=0)
        _row_copy(stage.at[slot], r, xs_hbm, d2_ref[i * tm + r], sem.at[slot]).start(priority=1)
        return carry

    for r in range(tm):
        issue(r, 0)

    def drain(s):
        for _ in range(2):
            pltpu.make_async_copy(stage.at[s], xs_hbm.at[pl.ds(0, tm)], sem.at[s]).wait()

    @pl.when(i > 0)
    def _():
        drain(1 - slot)

    @pl.when(i == pl.num_programs(0) - 1)
    def _():
        drain(slot)


def _moe_dispatch(dest1, dest2, zlist, h, n_rows, tm):
    n, d = h.shape
    grid_spec = pltpu.PrefetchScalarGridSpec(
        num_scalar_prefetch=3,
        grid=(n // tm,),
        in_specs=[pl.BlockSpec((tm, d), lambda i, *_: (i, 0))],
        out_specs=pl.BlockSpec(memory_space=pl.ANY),
        scratch_shapes=[pltpu.VMEM((MOE_SUB, d), F32), pltpu.VMEM((2, tm, d), F32),
                        pltpu.SemaphoreType.DMA((2,)), pltpu.SemaphoreType.DMA(())],
    )
    return pl.pallas_call(
        _dispatch_kernel, grid_spec=grid_spec,
        out_shape=jax.ShapeDtypeStruct((n_rows, d), F32),
        compiler_params=_cparams("arbitrary"),
        name="moe_dispatch",
    )(dest1, dest2, zlist, h)


def _expert_kernel(blk_e_ref, nvalid_ref, xs_ref, wg_ref, wu_ref, wd_ref, y_ref, xb):
    b = pl.program_id(0)
    j = pl.program_id(1)
    tb = xb.shape[0]
    nsub = tb // MOE_SUB
    nv = nvalid_ref[b]
    used = (nv + MOE_SUB - 1) // MOE_SUB

    def run(rows):
        wg = wg_ref[...].astype(BF16)
        wu = wu_ref[...].astype(BF16)
        wd = wd_ref[...].astype(BF16)

        @pl.when(j == 0)
        def _():
            xb[rows, :] = xs_ref[rows, :].astype(BF16)

        x = xb[rows, :]
        a = (_silu(_dot(x, wg)) * _dot(x, wu)).astype(BF16)
        part = _dot(a, wd)

        @pl.when(j == 0)
        def _():
            y_ref[rows, :] = part

        @pl.when(j > 0)
        def _():
            y_ref[rows, :] = y_ref[rows, :] + part

    for m in range(1, nsub + 1):
        @pl.when(used == m)
        def _():
            run(pl.ds(0, m * MOE_SUB))

    for s in range(nsub):
        @pl.when(jnp.logical_and(s >= used, j == pl.num_programs(1) - 1))
        def _():
            y_ref[pl.ds(s * MOE_SUB, MOE_SUB), :] = jnp.zeros((MOE_SUB, y_ref.shape[1]), F32)


def _moe_experts(blk_e, nvalid, xs, wg, wu, wd, lj, tb, tf):
    n_rows, d = xs.shape
    f = wg.shape[3]
    nj = f // tf

    def jm(b, j, nv):
        return jnp.where(nv[b] > 0, j, nj - 1)

    grid_spec = pltpu.PrefetchScalarGridSpec(
        num_scalar_prefetch=2,
        grid=(n_rows // tb, nj),
        in_specs=[pl.BlockSpec((tb, d), lambda b, j, be, nv: (b, 0)),
                  pl.BlockSpec((None, None, d, tf), lambda b, j, be, nv: (lj, be[b], 0, jm(b, j, nv))),
                  pl.BlockSpec((None, None, d, tf), lambda b, j, be, nv: (lj, be[b], 0, jm(b, j, nv))),
                  pl.BlockSpec((None, None, tf, d), lambda b, j, be, nv: (lj, be[b], jm(b, j, nv), 0))],
        out_specs=pl.BlockSpec((tb, d), lambda b, j, be, nv: (b, 0)),
        scratch_shapes=[pltpu.VMEM((tb, d), BF16)],
    )
    return pl.pallas_call(
        _expert_kernel, grid_spec=grid_spec,
        out_shape=jax.ShapeDtypeStruct((n_rows, d), F32),
        compiler_params=_cparams("parallel", "arbitrary"),
        name="moe_experts",
    )(blk_e, nvalid, xs, wg, wu, wd)


def _combine_kernel(d1_ref, d2_ref, ys_hbm, route_ref, x_ref, gp_ref, gs_ref, pg_ref, pb_ref, *rest,
                    alpha, np_tiles, split):
    (*outs, y1, y2, sem1, sem2) = rest
    i = pl.program_id(0)
    tm = x_ref.shape[0]
    slot = i % 2

    last = pl.num_programs(0) - 1

    def issue(tile, s, r):
        _row_copy(ys_hbm, d1_ref[tile * tm + r], y1.at[s], r, sem1.at[s]).start(priority=0)
        _row_copy(ys_hbm, d2_ref[tile * tm + r], y2.at[s], r, sem2.at[s]).start(priority=1)

    def drain(s):
        pltpu.make_async_copy(ys_hbm.at[pl.ds(0, tm)], y1.at[s], sem1.at[s]).wait()
        pltpu.make_async_copy(ys_hbm.at[pl.ds(0, tm)], y2.at[s], sem2.at[s]).wait()

    @pl.when(i == 0)
    def _():
        lax.fori_loop(0, tm, lambda r, c: (issue(0, 0, r), c)[1], 0, unroll=8)

    drain(slot)
    nxt = jnp.minimum(i + 1, last)
    for r in range(tm):
        issue(nxt, 1 - slot, r)
    r = route_ref[...]
    f = y1[slot] * r[:, 2:3] + y2[slot] * r[:, 3:4]
    out = _post_residual(x_ref, gp_ref, gs_ref, f, pg_ref, pb_ref, i >= np_tiles, alpha)
    if split:
        op_ref, os_ref = outs

        @pl.when(i < np_tiles)
        def _():
            op_ref[...] = out

        @pl.when(i >= np_tiles)
        def _():
            os_ref[...] = out
    else:
        outs[0][...] = out

    @pl.when(i == last)
    def _():
        drain(1 - slot)


def _moe_combine(dest1, dest2, ys, route, x, mod_p, mod_s, pg, pb, l, rm, alpha, split):
    n, d = x.shape
    tm = rm.t
    if split:
        n_p = rm.np_tiles * tm
        out_specs = [pl.BlockSpec((tm, d), lambda i, *_: (rm.prompt(i), 0)),
                     pl.BlockSpec((tm, d), lambda i, *_: (rm.sample(i), 0))]
        out_shape = [jax.ShapeDtypeStruct((n_p, d), F32), jax.ShapeDtypeStruct((n - n_p, d), F32)]
    else:
        out_specs = pl.BlockSpec((tm, d), lambda i, *_: (i, 0))
        out_shape = jax.ShapeDtypeStruct((n, d), F32)
    post = pl.BlockSpec((None, None, 1, d), lambda i, *_: (l, 1, 0, 0))
    grid_spec = pltpu.PrefetchScalarGridSpec(
        num_scalar_prefetch=2,
        grid=(n // tm,),
        in_specs=[pl.BlockSpec(memory_space=pl.ANY),
                  pl.BlockSpec((tm, LANES), lambda i, *_: (i, 0)),
                  pl.BlockSpec((tm, d), lambda i, *_: (i, 0)),
                  *rm.mod_specs(l, 5, d), post, post],
        out_specs=out_specs,
        scratch_shapes=[pltpu.VMEM((2, tm, d), F32), pltpu.VMEM((2, tm, d), F32),
                        pltpu.SemaphoreType.DMA((2,)), pltpu.SemaphoreType.DMA((2,))],
    )
    return pl.pallas_call(
        functools.partial(_combine_kernel, alpha=alpha, np_tiles=rm.np_tiles, split=split), grid_spec=grid_spec,
        out_shape=out_shape,
        compiler_params=_cparams("arbitrary"),
        name="moe_combine",
    )(dest1, dest2, ys, route, x, mod_p, mod_s, pg, pb)


def _moe_ffn(h, route, wg, wu, wd, x, mod_p, mod_s, pg, pb, l, lj, rm, tb, tf, alpha, split):
    n = h.shape[0]
    rank, tot = _moe_rank(route, rm.t)
    counts = tot[0, :N_EXPERTS].astype(jnp.int32)
    padded = (counts + tb - 1) // tb * tb
    pad_end = jnp.cumsum(padded)
    pad_start = pad_end - padded
    experts = jnp.arange(N_EXPERTS, dtype=jnp.int32)

    def slot(col):
        e = route[:, col].astype(jnp.int32)
        start = jnp.sum(jnp.where(e[:, None] == experts[None, :], pad_start[None, :], 0), axis=1)
        return start + rank[:, col].astype(jnp.int32)

    dest1, dest2 = slot(0), slot(1)
    n_blocks = -(-(2 * n + N_EXPERTS * (tb - 1)) // tb)
    blk_start = jnp.arange(n_blocks, dtype=jnp.int32) * tb
    blk_e = jnp.minimum(jnp.sum(pad_end[None, :] <= blk_start[:, None], axis=1), N_EXPERTS - 1).astype(jnp.int32)
    own_end = jnp.sum(jnp.where(blk_e[:, None] == experts[None, :], (pad_start + counts)[None, :], 0), axis=1)
    nvalid = jnp.clip(own_end - blk_start, 0, tb).astype(jnp.int32)
    nvalid = jnp.where(blk_start < pad_end[-1], nvalid, 0)
    per_blk = tb // MOE_SUB
    sub_start = jnp.arange(n_blocks * per_blk, dtype=jnp.int32) * MOE_SUB
    sub_room = jnp.repeat(blk_start + nvalid, per_blk) - sub_start
    zlist = jnp.where(sub_room < MOE_SUB, sub_start, -1).astype(jnp.int32)
    xs = _moe_dispatch(dest1, dest2, zlist, h, n_blocks * tb, rm.t)
    ys = _moe_experts(blk_e, nvalid, xs, wg, wu, wd, lj, tb, tf)
    return _moe_combine(dest1, dest2, ys, route, x, mod_p, mod_s, pg, pb, l, rm, alpha, split)


def _pack_w_in(w_in, b_in):
    depth, d, _ = w_in.shape
    q_end = 2 * D_CONV + N_HEADS * HEAD_DIM
    k_end = q_end + N_KV * HEAD_DIM
    a_end = k_end + N_KV * HEAD_DIM
    m_end = a_end + 4 * M_HEADS * M_DK
    if_end = m_end + 2 * M_HEADS

    def pack(a):
        return jnp.concatenate([a[..., if_end:], a[..., :q_end], a[..., a_end:m_end], a[..., q_end:a_end]], axis=-1)

    def gates(a):
        return jnp.pad(a[..., m_end:if_end], [(0, 0)] * (a.ndim - 1) + [(0, LANES - 2 * M_HEADS)])

    return (pack(w_in).astype(BF16), pack(b_in).reshape(depth, 1, Z_W),
            gates(w_in).astype(BF16), gates(b_in).reshape(depth, 1, LANES))


def kernel(x_prompt, x_sample, cache_swa_k, cache_swa_v, state_conv, state_mlstm_C, state_mlstm_n, state_mlstm_m, c_prompt, c_sample, w_ada, b_ada, w_in, b_in, conv_w, conv_b, conv_ln_g, conv_ln_b, w_conv_out, attn_sinks, rel_bias, w_attn_out, m_norm_g, w_m_out, w_out, post_ln_g, post_ln_b, ffn_w_gate, ffn_w_up, ffn_w_down, router_w, router_b, moe_w_gate, moe_w_up, moe_w_down):
    bp, tp, d = x_prompt.shape
    bs, ts, _ = x_sample.shape
    depth = w_ada.shape[0]
    alpha = (2 * depth) ** 0.25
    n_p, n_s = bp * tp, bs * ts
    tm = n_s
    assert d == D_MODEL and tp % tm == 0 and tp % WINDOW == 0 and tm % 32 == 0
    rm = _RowMap(tm, bp, tp, n_p, n_s)
    rm_half = _RowMap(tm // 2, bp, tp, n_p, n_s)
    wb = cache_swa_k.shape[2]
    big = n_p >= 4096
    tc = 1024 if big else tm
    lm = 256 if big else min(tp, 128)
    tb = 1024 if big else 2 * MOE_SUB
    bs_blk = 32 if bs % 32 == 0 else bs
    bs_att = 16 if bs % 16 == 0 else bs
    bs_m = 4 if bs % 4 == 0 else 1
    lts = 16

    n = n_p + n_s
    x = (x_prompt.reshape(n_p, d), jnp.transpose(x_sample, (1, 0, 2)).reshape(n_s, d))

    nc_rows = -(-(bp + bs) // 8) * 8
    c_all = jnp.zeros((nc_rows, d), F32).at[:bp].set(c_prompt).at[bp:bp + bs].set(c_sample)
    mod = _ada_mod(c_all, w_ada, b_ada)
    mod_p = mod[:, :bp].reshape(depth, bp, 1, 6 * d)
    mod_s = jnp.tile(mod[:, bp:bp + bs], (1, ts, 1))

    w_in_p, b_in_p, w_if, b_if = _pack_w_in(w_in, b_in)
    wc_b, wa_b, wm_b, wo_b = (w.astype(BF16) for w in (w_conv_out, w_attn_out, w_m_out, w_out))
    fg_b, fu_b, fd_b = (w.astype(BF16) for w in (ffn_w_gate, ffn_w_up, ffn_w_down))
    cw_pad = jnp.pad(conv_w, ((0, 0), (0, CONV_PAD - CONV_W), (0, 0)))
    cvecs = [v.reshape(depth, 1, D_CONV) for v in (conv_b, conv_ln_g, conv_ln_b)]
    rw_pad = jnp.pad(router_w, ((0, 0), (0, 0), (0, LANES - N_EXPERTS)))
    rb_pad = jnp.pad(router_b, ((0, 0), (0, LANES - N_EXPERTS)), constant_values=NEG_INF)[:, None, :]
    pg = post_ln_g.reshape(depth, 2, 1, d)
    pb = post_ln_b.reshape(depth, 2, 1, d)
    gamma = m_norm_g.reshape(depth, 1, M_HEADS * M_DV)
    sinks = attn_sinks.astype(F32)

    qi = jnp.arange(WINDOW)[:, None]
    kj = jnp.arange(2 * WINDOW)[None, :]
    dist_p = qi + WINDOW - kj
    bh = _bias_heads(rel_bias, dist_p, (dist_p >= 0) & (dist_p < WINDOW))
    bias_p = bh.reshape(N_HEADS // 2, 2, WINDOW, 2 * WINDOW).transpose(0, 2, 1, 3).reshape(
        N_HEADS // 2, WINDOW, 4 * WINDOW)
    dist_s = jnp.arange(ts)[:, None] + wb - jnp.arange(wb + ts)[None, :]
    bias_s = _bias_heads(rel_bias, dist_s, (dist_s >= 0) & (dist_s < WINDOW)).reshape(
        N_KV, Q_PER_KV * ts, wb + ts)
    kc_all = cache_swa_k.reshape(depth, bs, wb, N_KV * HEAD_DIM)
    vc_all = cache_swa_v.reshape(depth, bs, wb, N_KV * HEAD_DIM)

    f_moe = moe_w_gate.shape[3]
    tf_moe = 512 if f_moe % 512 == 0 else f_moe

    n0_all = state_mlstm_n[:, :, :, None, :]
    m0_all = _m_state_in(state_mlstm_m)
    s_c = s_conv = s_kv = None
    new_p = [[] for _ in range(6)]
    new_s = [[] for _ in range(6)]
    for l in range(depth):
        j = l // 2
        xp, xs = x if isinstance(x, tuple) else (x, x)
        z, zif = _ln_proj(xp, xs, mod_p, mod_s, w_in_p, b_in_p, w_if, b_if, l, rm, n)
        zs3 = z[n_p:].reshape(ts, bs, Z_W)

        cp, ns_p = _conv_prompt(z, jnp.zeros((bp, CONV_PAD, D_CONV), F32), cw_pad, *cvecs, l, bp, tp, tc)
        cs3, s_conv = _conv_sample(zs3, state_conv, cw_pad, *cvecs, s_conv, l, bs_blk)
        new_p[2].append(ns_p[:, CONV_PAD - CONV_W + 1:])

        sink_h = sinks[l].reshape(N_KV, Q_PER_KV, 1)
        sink_s = jnp.broadcast_to(sink_h, (N_KV, Q_PER_KV, ts)).reshape(N_KV, Q_PER_KV * ts, 1)
        ap = _attn_prompt(z, bias_p * LOG2E, sinks * LOG2E, l, bp, tp)
        nk = min(WINDOW, tp)
        kv_tail = jnp.stack([z[(b + 1) * tp - nk:(b + 1) * tp, Z_K:Z_K + 2 * N_KV * HEAD_DIM]
                             for b in range(bp)]).astype(F32)
        new_p[0].append(kv_tail[..., :N_KV * HEAD_DIM].reshape(bp, nk, N_KV, HEAD_DIM))
        new_p[1].append(kv_tail[..., N_KV * HEAD_DIM:].reshape(bp, nk, N_KV, HEAD_DIM))
        q_s = zs3[:, :, Z_Q:Z_Q + N_HEADS * HEAD_DIM].reshape(ts, bs, N_KV, Q_PER_KV, HEAD_DIM)
        q4 = jnp.transpose(q_s, (1, 2, 3, 0, 4)).reshape(bs, N_KV, Q_PER_KV * ts, HEAD_DIM)
        k_s = jnp.transpose(zs3[:, :, Z_K:Z_K + N_KV * HEAD_DIM].reshape(ts, bs, N_KV, HEAD_DIM), (1, 0, 2, 3))
        v_s = jnp.transpose(zs3[:, :, Z_V:Z_V + N_KV * HEAD_DIM].reshape(ts, bs, N_KV, HEAD_DIM), (1, 0, 2, 3))
        o4, s_kv = _attn_sample(q4, kc_all, vc_all, jnp.transpose(k_s, (0, 2, 1, 3)), jnp.transpose(v_s, (0, 2, 1, 3)),
                                bias_s, sink_s, zs3[:, :, Z_K:Z_K + N_KV * HEAD_DIM],
                                zs3[:, :, Z_V:Z_V + N_KV * HEAD_DIM], s_kv, l, bs_att)
        as_ = jnp.transpose(o4.reshape(bs, N_KV, Q_PER_KV, ts, HEAD_DIM), (3, 0, 1, 2, 4)).reshape(n_s, -1).astype(BF16)

        ncp = tp // lm
        if_p = zif[:n_p, :2 * M_HEADS].reshape(bp * ncp, lm, 2 * M_HEADS)
        mp, c1p, n1p, m1p = _mlstm_prompt(z, zif, jnp.transpose(if_p, (0, 2, 1)), gamma, l, bp, ncp, lm)
        new_p[3].append(c1p)
        new_p[4].append(n1p[:, :, 0])
        new_p[5].append(m1p[:, :M_HEADS, 0])
        tpad = ((0, 0), (0, lts - ts), (0, 0))
        zm3 = jnp.pad(jnp.transpose(zs3[:, :, Z_MQ:Z_K], (1, 0, 2)), tpad)
        if3 = jnp.pad(jnp.transpose(zif[n_p:].reshape(ts, bs, LANES), (1, 0, 2)), tpad)
        ms, s_c, n1s, m1s = _mlstm_sample(zm3, if3, jnp.transpose(if3[:, :, :2 * M_HEADS], (0, 2, 1)), gamma,
                                          state_mlstm_C, n0_all[l], m0_all[l], s_c, l, ts, bs_m)
        ms = jnp.transpose(ms[:, :ts], (1, 0, 2)).reshape(n_s, -1)
        new_s[4].append(n1s[:, :, 0])
        new_s[5].append(m1s[:, :M_HEADS, 0])

        moe = l % 2 == 1
        router = (rw_pad, rb_pad, j) if moe else None
        outs = _merge(cp, cs3.reshape(n_s, D_CONV), ap, as_, mp, ms, z, xp, xs, mod_p, mod_s,
                      wc_b, wa_b, wm_b, wo_b, pg, pb, l, rm, alpha, router)
        if moe:
            x1, h2, route = outs
            x = _moe_ffn(h2, route, moe_w_gate, moe_w_up, moe_w_down, x1, mod_p, mod_s, pg, pb,
                         l, j, rm, tb, tf_moe, alpha, split=l == depth - 1)
        else:
            x1, h2 = outs
            x = _ffn_dense(h2, fg_b, fu_b, fd_b, x1, mod_p, mod_s, pg, pb, l, j, rm, alpha)

    x_p, x_s = x if isinstance(x, (list, tuple)) else (x[:n_p], x[n_p:])
    y_p = x_p.reshape(bp, tp, d)
    y_s = jnp.transpose(x_s.reshape(ts, bs, d), (1, 0, 2))
    p_k, p_v, p_conv, p_c, p_n, p_m = [jnp.stack(a) for a in new_p]
    s_k, s_v = [a.reshape(cache_swa_k.shape) for a in s_kv]
    s_n, s_m = jnp.stack(new_s[4]), jnp.stack(new_s[5])
    return (y_p, y_s, p_k, p_v, p_conv, p_c, p_n, p_m, s_k, s_v, s_conv, s_c, s_n, s_m)
```

```python
import functools
import math

import jax
import jax.numpy as jnp
from jax import lax
from jax.experimental import pallas as pl
from jax.experimental.pallas import tpu as pltpu

F32 = jnp.float32
BF16 = jnp.bfloat16

D_MODEL = 1024
D_CONV = 512
CONV_W = 31
CONV_PAD = 32
N_HEADS = 8
N_KV = 2
HEAD_DIM = 64
Q_PER_KV = N_HEADS // N_KV
WINDOW = 128
N_BUCKETS = 32
MAX_DIST = 128
M_HEADS = 4
M_DK = 128
M_DV = 128
N_EXPERTS = 8
LN_EPS = 1e-5
LANES = 128
NEG_INF = float("-inf")
LOG2E = math.log2(math.e)
VMEM_LIMIT = 56 * 1024 * 1024

Z_G, Z_UA, Z_UB, Z_Q = 0, 3072, 3584, 4096
Z_MQ, Z_MK, Z_MV, Z_MO = 4608, 5120, 5632, 6144
Z_K, Z_V, Z_W = 6656, 6784, 6912
TN_IN = 6912
ATT_QB = 8
ATT_GROUP = 8
MOE_SUB = 256
CONV_CHUNK = 32

def _cparams(*sem):
    return pltpu.CompilerParams(dimension_semantics=sem, vmem_limit_bytes=VMEM_LIMIT)


def _sigmoid(x):
    return 1.0 / (1.0 + jnp.exp(-x))


def _silu(x):
    return x * _sigmoid(x)


def _log_sigmoid(x):
    return jnp.minimum(x, 0.0) - jnp.log(1.0 + jnp.exp(-jnp.abs(x)))


def _norm(x):
    mu = jnp.mean(x, axis=-1, keepdims=True)
    xc = x - mu
    var = jnp.mean(xc * xc, axis=-1, keepdims=True)
    return xc * lax.rsqrt(var + LN_EPS)


def _dot(a, b):
    return jnp.dot(a, b, preferred_element_type=F32)


def _dot_nt(a, b):
    return lax.dot_general(a, b, (((1,), (1,)), ((), ())), preferred_element_type=F32)


def _dot_tn(a, b):
    return lax.dot_general(a, b, (((0,), (0,)), ((), ())), preferred_element_type=F32)


def _dot_hi(a, b):
    return jnp.dot(a, b, preferred_element_type=F32, precision=lax.Precision.HIGHEST)


def _split_bf16(a):
    hi = a.astype(BF16)
    return hi, (a - hi.astype(F32)).astype(BF16)


_RARELY = pl.Buffered(1)


class _RowMap:
    def __init__(self, t, bp, tp, n_p, n_s, mod_rows=None):
        self.mod_rows = mod_rows or t
        self.t = t
        self.bp = bp
        self.per_seq = tp // t
        self.np_tiles = n_p // t
        self.ns_tiles = n_s // t

    def seq(self, i):
        return jnp.minimum(i // self.per_seq, self.bp - 1)

    def prompt(self, i):
        return jnp.minimum(i, self.np_tiles - 1)

    def sample(self, i):
        return jnp.clip(i - self.np_tiles, 0, self.ns_tiles - 1)

    def mod_specs(self, l, k, d):
        return (pl.BlockSpec((None, None, 1, d), lambda i, *_: (l, self.seq(i), 0, k)),
                pl.BlockSpec((None, self.mod_rows, d), lambda i, *_: (l, self.sample(i), k), pipeline_mode=_RARELY))

    def x_specs(self, xs_arr, d):
        off = self.np_tiles if xs_arr.shape[0] > self.ns_tiles * self.t else 0
        return (pl.BlockSpec((self.t, d), lambda i, *_: (self.prompt(i), 0)),
                pl.BlockSpec((self.t, d), lambda i, *_: (off + self.sample(i), 0), pipeline_mode=_RARELY))


def _rows(ref, t):
    v = ref[...]
    return v if v.shape[0] in (1, t) else jnp.concatenate([v] * (t // v.shape[0]), axis=0)


def _pick(is_s, p_ref, s_ref, t):
    return jnp.where(is_s, _rows(s_ref, t), p_ref[...])


def _ada_kernel(c_ref, w_ref, b_ref, o_ref):
    s = _silu(c_ref[...]).astype(BF16)
    o_ref[0] = _dot(s, w_ref[0].astype(BF16)) + b_ref[0]


def _ada_mod(c_all, w_ada, b_ada):
    depth, d, n6 = w_ada.shape
    rows = c_all.shape[0]
    return pl.pallas_call(
        _ada_kernel,
        grid=(depth, n6 // d),
        in_specs=[pl.BlockSpec((rows, d), lambda l, j: (0, 0)),
                  pl.BlockSpec((1, d, d), lambda l, j: (l, 0, j)),
                  pl.BlockSpec((1, 1, d), lambda l, j: (l, 0, j))],
        out_specs=pl.BlockSpec((1, rows, d), lambda l, j: (l, 0, j)),
        out_shape=jax.ShapeDtypeStruct((depth, rows, n6), F32),
        compiler_params=_cparams("parallel", "parallel"),
        name="ada_mod",
    )(c_all, w_ada, b_ada.reshape(depth, 1, n6))


def _ln_proj_kernel(xp_ref, xs_ref, shp_ref, shs_ref, scp_ref, scs_ref, w_ref, b_ref, wif_ref, bif_ref,
                    z_ref, zif_ref, h_scr, *, np_tiles):
    is_s = pl.program_id(0) >= np_tiles
    first = pl.program_id(1) == 0

    def prologue(x_ref, sh_ref, sc_ref):
        t = x_ref.shape[0]
        h = (_norm(x_ref[...]) * (1.0 + _rows(sc_ref, t)) + _rows(sh_ref, t)).astype(BF16)
        h_scr[...] = h
        zif_ref[...] = _dot(h, wif_ref[...]) + bif_ref[...]

    @pl.when(jnp.logical_and(first, is_s))
    def _():
        prologue(xs_ref, shs_ref, scs_ref)

    @pl.when(jnp.logical_and(first, jnp.logical_not(is_s)))
    def _():
        prologue(xp_ref, shp_ref, scp_ref)

    z_ref[...] = (_dot(h_scr[...], w_ref[...]) + b_ref[...]).astype(BF16)


def _ln_proj(xp, xs, mod_p, mod_s, w, b, wif, bif, l, rm, n):
    d = xp.shape[1]
    zw = w.shape[2]
    tm = rm.t
    shp, shs = rm.mod_specs(l, 0, d)
    scp, scs = rm.mod_specs(l, 1, d)
    wmode = _RARELY if zw == TN_IN else None
    return pl.pallas_call(
        functools.partial(_ln_proj_kernel, np_tiles=rm.np_tiles),
        grid=(n // tm, zw // TN_IN),
        in_specs=[*rm.x_specs(xs, d),
                  shp, shs, scp, scs,
                  pl.BlockSpec((None, d, TN_IN), lambda i, j: (l, 0, j), pipeline_mode=wmode),
                  pl.BlockSpec((None, 1, TN_IN), lambda i, j: (l, 0, j), pipeline_mode=wmode),
                  pl.BlockSpec((None, d, LANES), lambda i, j: (l, 0, 0), pipeline_mode=_RARELY),
                  pl.BlockSpec((None, 1, LANES), lambda i, j: (l, 0, 0), pipeline_mode=_RARELY)],
        out_specs=[pl.BlockSpec((tm, TN_IN), lambda i, j: (i, j)),
                   pl.BlockSpec((tm, LANES), lambda i, j: (i, 0))],
        out_shape=[jax.ShapeDtypeStruct((n, zw), BF16), jax.ShapeDtypeStruct((n, LANES), F32)],
        scratch_shapes=[pltpu.VMEM((tm, d), BF16)],
        compiler_params=_cparams("parallel", "arbitrary"),
        name="ln_proj",
    )(xp, xs, mod_p, mod_s, mod_p, mod_s, w, b, wif, bif)


def _conv_tail(yc, g_ref, b_ref):
    y = _norm(yc) * g_ref[...] + b_ref[...]
    return _silu(y).astype(BF16)


def _conv_prompt_kernel(ua_ref, ub_ref, st_ref, cw_ref, cb_ref, g_ref, b_ref, o_ref, ns_ref,
                        ext, shifted, yc, wrep):
    t = pl.program_id(1)
    tc = ua_ref.shape[0]
    sub = 8

    @pl.when(t == 0)
    def _():
        ext[0:CONV_PAD, :] = st_ref[0]

    @pl.when(t > 0)
    def _():
        ext[0:CONV_PAD, :] = ext[tc:tc + CONV_PAD, :]

    ext[CONV_PAD:, :] = ua_ref[...].astype(F32) * _sigmoid(ub_ref[...].astype(F32))
    for s in range(1, sub):
        shifted[s - 1] = ext[s:s + tc + CONV_PAD - sub, :]
    off = CONV_PAD - (CONV_W - 1)
    for w in range(CONV_W):
        wrep[w] = jnp.broadcast_to(cw_ref[w:w + 1, :], (sub, D_CONV))
    groups = CONV_CHUNK // sub

    for r0 in range(0, tc, CONV_CHUNK):
        acc = jnp.broadcast_to(cb_ref[...].reshape(1, 1, D_CONV), (groups, sub, D_CONV))
        for w in range(CONV_W):
            base, s = (off + w) // sub * sub, (off + w) % sub
            src = ext if s == 0 else shifted.at[s - 1]
            win = src[r0 + base:r0 + base + CONV_CHUNK, :]
            acc = acc + win.reshape(groups, sub, D_CONV) * wrep[w][None]
        yc[r0:r0 + CONV_CHUNK, :] = acc.reshape(CONV_CHUNK, D_CONV)
    o_ref[...] = _conv_tail(yc[...], g_ref, b_ref)

    @pl.when(t == pl.num_programs(1) - 1)
    def _():
        ns_ref[0] = ext[tc:tc + CONV_PAD, :]


def _conv_vec_specs(l, nargs):
    return [pl.BlockSpec((None, 1, D_CONV), lambda *_: (l, 0, 0)) for _ in range(nargs)]


def _conv_prompt(z, state_pad, cw, cb, g, b, l, bp, tp, tc):
    nt = tp // tc
    return pl.pallas_call(
        _conv_prompt_kernel,
        grid=(bp, nt),
        in_specs=[pl.BlockSpec((tc, D_CONV), lambda bb, t: (bb * nt + t, Z_UA // D_CONV)),
                  pl.BlockSpec((tc, D_CONV), lambda bb, t: (bb * nt + t, Z_UB // D_CONV)),
                  pl.BlockSpec((1, CONV_PAD, D_CONV), lambda bb, t: (bb, 0, 0)),
                  pl.BlockSpec((None, CONV_PAD, D_CONV), lambda bb, t: (l, 0, 0))] + _conv_vec_specs(l, 3),
        out_specs=[pl.BlockSpec((tc, D_CONV), lambda bb, t: (bb * nt + t, 0)),
                   pl.BlockSpec((1, CONV_PAD, D_CONV), lambda bb, t: (bb, 0, 0))],
        out_shape=[jax.ShapeDtypeStruct((bp * tp, D_CONV), BF16),
                   jax.ShapeDtypeStruct((bp, CONV_PAD, D_CONV), F32)],
        scratch_shapes=[pltpu.VMEM((tc + CONV_PAD, D_CONV), F32),
                        pltpu.VMEM((7, tc + CONV_PAD - 8, D_CONV), F32),
                        pltpu.VMEM((tc, D_CONV), F32),
                        pltpu.VMEM((CONV_PAD, 8, D_CONV), F32)],
        compiler_params=_cparams("parallel", "arbitrary"),
        name="conv_prompt",
    )(z, z, state_pad, cw, cb, g, b)


def _conv_sample_kernel(ua_ref, ub_ref, st_ref, cw_ref, cb_ref, g_ref, b_ref, *rest, first_layer):
    o_ref, ns_ref = rest[-2:]
    ts = ua_ref.shape[0]
    ns = CONV_W - 1
    a = ua_ref[...].astype(F32) * _sigmoid(ub_ref[...].astype(F32))
    st = st_ref[...]
    if first_layer is not None:
        for dd in range(ns_ref.shape[0]):
            if dd != first_layer:
                ns_ref[dd] = jnp.zeros(ns_ref.shape[1:], F32)
        ns_ref = ns_ref.at[first_layer]
    ns_ref[:, 0:ns - ts, :] = st[:, ts:, :]
    for t in range(ts):
        ns_ref[:, ns - ts + t, :] = a[t]
    row = lax.broadcasted_iota(jnp.int32, (ns, D_CONV), 0)
    for t in range(ts):
        wt = jnp.zeros((ns, D_CONV), F32)
        for j in range(t, ns):
            wt = jnp.where(row == j, cw_ref[j - t:j - t + 1, :], wt)
        yc = jnp.sum(st * wt[None], axis=1) + cb_ref[...]
        for t2 in range(t + 1):
            wi = CONV_W - 1 - (t - t2)
            yc = yc + a[t2] * cw_ref[wi:wi + 1, :]
        o_ref[t] = _conv_tail(yc, g_ref, b_ref)


def _conv_sample(zs3, state, cw, cb, g, b, ns_prev, l, bs_blk):
    ts, bs, _ = zs3.shape
    depth = state.shape[0]
    ns = CONV_W - 1
    in_specs = [pl.BlockSpec((ts, bs_blk, D_CONV), lambda i: (0, i, Z_UA // D_CONV)),
                pl.BlockSpec((ts, bs_blk, D_CONV), lambda i: (0, i, Z_UB // D_CONV)),
                pl.BlockSpec((None, bs_blk, ns, D_CONV), lambda i: (l, i, 0, 0)),
                pl.BlockSpec((None, CONV_PAD, D_CONV), lambda i: (l, 0, 0))] + _conv_vec_specs(l, 3)
    args = [zs3, zs3, state, cw, cb, g, b]
    aliases = {}
    if ns_prev is None:
        ns_spec = pl.BlockSpec((depth, bs_blk, ns, D_CONV), lambda i: (0, i, 0, 0))
    else:
        in_specs.append(pl.BlockSpec(memory_space=pl.ANY))
        args.append(ns_prev)
        aliases = {len(args) - 1: 1}
        ns_spec = pl.BlockSpec((None, bs_blk, ns, D_CONV), lambda i: (l, i, 0, 0))
    return pl.pallas_call(
        functools.partial(_conv_sample_kernel, first_layer=l if ns_prev is None else None),
        grid=(bs // bs_blk,),
        in_specs=in_specs,
        out_specs=[pl.BlockSpec((ts, bs_blk, D_CONV), lambda i: (0, i, 0)), ns_spec],
        out_shape=[jax.ShapeDtypeStruct((ts, bs, D_CONV), BF16),
                   jax.ShapeDtypeStruct((depth, bs, ns, D_CONV), F32)],
        input_output_aliases=aliases,
        compiler_params=_cparams("parallel"),
        name="conv_sample",
    )(*args)


def _t5_bucket(dist):
    max_exact = N_BUCKETS // 2
    d = jnp.maximum(dist, 0)
    large = max_exact + (jnp.log(jnp.maximum(d, 1).astype(F32) / max_exact)
                         / math.log(MAX_DIST / max_exact) * (N_BUCKETS - max_exact)).astype(jnp.int32)
    return jnp.where(d < max_exact, d, jnp.minimum(large, N_BUCKETS - 1))


def _bias_heads(rel_bias, dist, valid):
    onehot = (_t5_bucket(dist)[..., None] == jnp.arange(N_BUCKETS)).astype(F32)
    bias = jnp.einsum("qkb,bh->qkh", onehot, rel_bias.astype(F32), precision=lax.Precision.HIGHEST)
    bias = jnp.where(valid[..., None], bias, NEG_INF)
    return jnp.transpose(bias, (2, 0, 1))


def _attn_prompt_kernel(sink_ref, q_ref, kc_ref, kp_ref, vc_ref, vp_ref, bias_ref, o_ref, *, l):
    first = pl.program_id(1) == 0
    w = WINDOW
    nq = q_ref.shape[0] // w
    kall = jnp.concatenate([kp_ref[...], kc_ref[...]], axis=0).astype(F32)
    vall = jnp.concatenate([vp_ref[...], vc_ref[...]], axis=0).astype(F32)
    lane = lax.broadcasted_iota(jnp.int32, kall.shape, 1)
    lo = lane < HEAD_DIM
    kroll = pltpu.roll(kall, HEAD_DIM, 1)
    vroll = pltpu.roll(vall, HEAD_DIM, 1)

    def halves(a, aroll, g):
        if g == 0:
            return jnp.where(lo, a, 0.0).astype(BF16), jnp.where(lo, 0.0, aroll).astype(BF16)
        return jnp.where(lo, aroll, 0.0).astype(BF16), jnp.where(lo, 0.0, a).astype(BF16)

    kh = [halves(kall, kroll, g) for g in range(N_KV)]
    vh = [halves(vall, vroll, g) for g in range(N_KV)]
    col = lax.broadcasted_iota(jnp.int32, (w, 4 * w), 1)
    prev_col = (col % (2 * w)) < w
    tiles_per_g = Q_PER_KV // 2
    units = [(qi, tile) for qi in range(nq) for tile in range(N_HEADS // 2)]
    for u0 in range(0, len(units), ATT_GROUP):
        group = units[u0:u0 + ATT_GROUP]
        scores = []
        for qi, tile in group:
            r0, g = qi * w, tile // tiles_per_g
            q = q_ref[r0:r0 + w, tile * LANES:(tile + 1) * LANES]
            kk = jnp.concatenate([kh[g][0][r0:r0 + 2 * w], kh[g][1][r0:r0 + 2 * w]], axis=0)
            s = _dot_nt(q, kk) * (HEAD_DIM ** -0.5 * LOG2E) + bias_ref[tile]
            if qi == 0:
                s = jnp.where(jnp.logical_and(first, prev_col), NEG_INF, s)
            scores.append(s)
        probs = []
        for (qi, tile), s in zip(group, scores):
            ps = []
            for half in range(2):
                sh = s[:, half * 2 * w:(half + 1) * 2 * w]
                sink = sink_ref[l, 2 * tile + half]
                mx = jnp.maximum(jnp.max(sh, axis=-1, keepdims=True), sink)
                p = jnp.exp2(sh - mx)
                den = jnp.sum(p, axis=-1, keepdims=True) + jnp.exp2(sink - mx)
                ps.append((p * (1.0 / den)).astype(BF16))
            probs.append(jnp.concatenate(ps, axis=1))
        for (qi, tile), p in zip(group, probs):
            r0, g = qi * w, tile // tiles_per_g
            vv = jnp.concatenate([vh[g][0][r0:r0 + 2 * w], vh[g][1][r0:r0 + 2 * w]], axis=0)
            o_ref[r0:r0 + w, tile * LANES:(tile + 1) * LANES] = _dot(p, vv).astype(BF16)


def _attn_prompt(z, bias, sinks, l, bp, tp):
    w = WINDOW
    qb = ATT_QB if tp % (ATT_QB * w) == 0 else 1
    ns = tp // (qb * w)
    nb = tp // w
    kvw = N_KV * HEAD_DIM
    qw = N_HEADS * HEAD_DIM

    def cur(col):
        return lambda bb, i: (bb * ns + i, col)

    def prev(col):
        return lambda bb, i: (bb * nb + jnp.maximum(i * qb - 1, 0), col)

    return pl.pallas_call(
        functools.partial(_attn_prompt_kernel, l=l),
        grid=(bp, ns),
        in_specs=[pl.BlockSpec(memory_space=pltpu.SMEM),
                  pl.BlockSpec((qb * w, qw), cur(Z_Q // qw)),
                  pl.BlockSpec((qb * w, kvw), cur(Z_K // kvw)),
                  pl.BlockSpec((w, kvw), prev(Z_K // kvw)),
                  pl.BlockSpec((qb * w, kvw), cur(Z_V // kvw)),
                  pl.BlockSpec((w, kvw), prev(Z_V // kvw)),
                  pl.BlockSpec((N_HEADS // 2, w, 4 * w), lambda bb, i: (0, 0, 0))],
        out_specs=pl.BlockSpec((qb * w, qw), lambda bb, i: (bb * ns + i, 0)),
        out_shape=jax.ShapeDtypeStruct((bp * tp, qw), BF16),
        compiler_params=_cparams("parallel", "parallel"),
        name="attn_prompt",
    )(sinks, z, z, z, z, z, bias)


def _attn_sample_kernel(q_ref, kc_ref, vc_ref, kn_ref, vn_ref, bias_ref, sink_ref, kt_ref, vt_ref, *rest,
                        first_layer):
    o_ref, nk_ref, nv_ref = rest[-3:]
    wb = kc_ref.shape[1]
    ts = kn_ref.shape[2]
    for new_ref, old_ref, t_ref in ((nk_ref, kc_ref, kt_ref), (nv_ref, vc_ref, vt_ref)):
        if first_layer is not None:
            for dd in range(new_ref.shape[0]):
                if dd != first_layer:
                    new_ref[dd] = jnp.zeros(new_ref.shape[1:], F32)
            new_ref = new_ref.at[first_layer]
        new_ref[:, 0:wb - ts, :] = old_ref[:, ts:, :]
        for t in range(ts):
            new_ref[:, wb - ts + t, :] = t_ref[t].astype(F32)
    for g in range(N_KV):
        lo = g * HEAD_DIM
        qb = (q_ref[:, g].astype(F32) * (HEAD_DIM ** -0.5)).astype(BF16)
        kc = kc_ref[:, :, lo:lo + HEAD_DIM].astype(BF16)
        vc = vc_ref[:, :, lo:lo + HEAD_DIM].astype(BF16)
        kn = kn_ref[:, g].astype(F32)
        vn = vn_ref[:, g].astype(F32)
        bias = bias_ref[g]
        s_c = jnp.einsum("bqd,bkd->bqk", qb, kc, preferred_element_type=F32) + bias[None, :, :wb]
        qf = qb.astype(F32)
        s_n = [jnp.sum(qf * kn[:, j:j + 1, :], axis=-1, keepdims=True) + bias[None, :, wb + j:wb + j + 1]
               for j in range(ts)]
        sink = sink_ref[g][None]
        mx = jnp.maximum(jnp.max(s_c, axis=-1, keepdims=True), sink)
        for sj in s_n:
            mx = jnp.maximum(mx, sj)
        p_c = jnp.exp(s_c - mx)
        p_n = [jnp.exp(sj - mx) for sj in s_n]
        den = jnp.sum(p_c, axis=-1, keepdims=True) + jnp.exp(sink - mx)
        for pj in p_n:
            den = den + pj
        o = jnp.einsum("bqk,bkd->bqd", (p_c / den).astype(BF16), vc, preferred_element_type=F32)
        for j in range(ts):
            o = o + (p_n[j] / den).astype(BF16).astype(F32) * vn[:, j:j + 1, :]
        o_ref[:, g] = o


def _attn_sample(q4, kc, vc, kn, vn, bias, sinks, kt, vt, new_prev, l, bs_blk):
    bs, _, rt, _ = q4.shape
    depth, _, wb, kvw = kc.shape
    ts = kn.shape[2]
    tmaj = pl.BlockSpec((ts, bs_blk, kvw), lambda i: (0, i, 0))
    in_specs = [pl.BlockSpec((bs_blk, N_KV, rt, HEAD_DIM), lambda i: (i, 0, 0, 0)),
                pl.BlockSpec((None, bs_blk, wb, kvw), lambda i: (l, i, 0, 0)),
                pl.BlockSpec((None, bs_blk, wb, kvw), lambda i: (l, i, 0, 0)),
                pl.BlockSpec((bs_blk, N_KV, ts, HEAD_DIM), lambda i: (i, 0, 0, 0)),
                pl.BlockSpec((bs_blk, N_KV, ts, HEAD_DIM), lambda i: (i, 0, 0, 0)),
                pl.BlockSpec((N_KV, rt, wb + ts), lambda i: (0, 0, 0)),
                pl.BlockSpec((N_KV, rt, 1), lambda i: (0, 0, 0)),
                tmaj, tmaj]
    args = [q4, kc, vc, kn, vn, bias, sinks, kt, vt]
    aliases = {}
    if new_prev is None:
        new_spec = pl.BlockSpec((depth, bs_blk, wb, kvw), lambda i: (0, i, 0, 0))
    else:
        in_specs += [pl.BlockSpec(memory_space=pl.ANY)] * 2
        args += list(new_prev)
        aliases = {len(args) - 2: 1, len(args) - 1: 2}
        new_spec = pl.BlockSpec((None, bs_blk, wb, kvw), lambda i: (l, i, 0, 0))
    new_shape = jax.ShapeDtypeStruct((depth, bs, wb, kvw), F32)
    outs = pl.pallas_call(
        functools.partial(_attn_sample_kernel, first_layer=l if new_prev is None else None),
        grid=(bs // bs_blk,),
        in_specs=in_specs,
        out_specs=[pl.BlockSpec((bs_blk, N_KV, rt, HEAD_DIM), lambda i: (i, 0, 0, 0)), new_spec, new_spec],
        out_shape=[jax.ShapeDtypeStruct((bs, N_KV, rt, HEAD_DIM), F32), new_shape, new_shape],
        input_output_aliases=aliases,
        compiler_params=_cparams("parallel"),
        name="attn_sample",
    )(*args)
    return outs[0], (outs[1], outs[2])


def _mlstm_kernel(*refs, t_valid, nseq, aliased, carried, chunk_axis, per_seq_inputs=False, first_layer=None):
    if aliased:
        refs = refs[:10] + refs[11:]
    if per_seq_inputs:
        groups = [refs[6 * s:6 * s + 6] for s in range(nseq)]
        if_ref, ifr_ref, q_ref, k_ref, v_ref, o_ref = (tuple(g[k] for g in groups) for k in range(6))
        refs = (None,) * 6 + refs[6 * nseq:]
    else:
        if_ref, ifr_ref, q_ref, k_ref, v_ref, o_ref = refs[:6]
    g_ref, c0_ref, n0_ref, m0_ref, h_ref, c1_ref, n1_ref, m1_ref = refs[6:14]
    c = pl.program_id(chunk_axis)
    L = (ifr_ref[0] if per_seq_inputs else ifr_ref).shape[-1]
    if carried:
        c_in, n_in, m_in = c_out, n_out, m_out = refs[14:]

        @pl.when(c == 0)
        def _():
            c_in[...] = c0_ref[...]
            n_in[...] = n0_ref[...]
            m_in[...] = m0_ref[...]
    else:
        (c_in, n_in, m_in), (c_out, n_out, m_out) = (c0_ref, n0_ref, m0_ref), (c1_ref, n1_ref, m1_ref)
        if first_layer is not None:
            for dd in range(c1_ref.shape[0]):
                if dd != first_layer:
                    c1_ref[dd] = jnp.zeros(c1_ref.shape[1:], F32)
            c_out = c1_ref.at[first_layer]

    def seq(ref, s):
        return ref[s] if isinstance(ref, tuple) else ref.at[s]

    tt = lax.broadcasted_iota(jnp.int32, (L, L), 0)
    ss = lax.broadcasted_iota(jnp.int32, (L, L), 1)
    causal = ss <= tt
    tril = causal.astype(F32)
    triu = (tt <= ss).astype(F32)
    seqs = range(nseq)
    heads = [(s_i, h) for s_i in seqs for h in range(M_HEADS)]
    mxu_sums = L % LANES == 0
    gates = []
    for s_i in seqs:
        ifc = seq(if_ref, s_i)[...]
        ifr = ifr_ref[s_i][0] if per_seq_inputs else ifr_ref[s_i]
        lf_c = _log_sigmoid(ifc)
        lf_r = _log_sigmoid(ifr)
        i_c, i_r = ifc, ifr
        if t_valid < L:
            rc = lax.broadcasted_iota(jnp.int32, (L, LANES), 0) < t_valid
            rr = lax.broadcasted_iota(jnp.int32, (2 * M_HEADS, L), 1) < t_valid
            lf_c = jnp.where(rc, lf_c, 0.0)
            lf_r = jnp.where(rr, lf_r, 0.0)
            i_c = jnp.where(rc, i_c, NEG_INF)
            i_r = jnp.where(rr, i_r, NEG_INF)
        gates.append((lf_c, lf_r, i_c, i_r))
    f_cs = [_dot_hi(tril, g[0]) for g in gates]
    f_rs = [_dot_hi(g[1], triu) for g in gates]
    state = {(s_i, h): (c_in[s_i, h], n_in[s_i, h], m_in[s_i, h:h + 1, :]) for s_i, h in heads}

    def wide(col):
        return jnp.concatenate([col] * (L // LANES), axis=1) if L >= LANES else col[:, :L]

    st1 = {}
    for s_i, h in heads:
        _, nrow, m0 = state[s_i, h]
        fc = jnp.broadcast_to(f_cs[s_i][:, M_HEADS + h:M_HEADS + h + 1], (L, LANES))
        fr = f_rs[s_i][M_HEADS + h:M_HEADS + h + 1, :]
        ir = gates[s_i][3][h:h + 1, :]
        dm = jnp.where(causal, wide(fc) - fr + ir, NEG_INF)
        m_t = jnp.maximum(m0 + fc, jnp.max(dm, axis=-1, keepdims=True))
        st1[s_i, h] = (fc, dm, m_t, jnp.exp(m0 + fc - m_t))
    st2 = {}
    for s_i, h in heads:
        lo = h * M_DK
        fc, dm, m_t, inter = st1[s_i, h]
        qb = seq(q_ref, s_i)[:, lo:lo + M_DK]
        kf = seq(k_ref, s_i)[:, lo:lo + M_DK].astype(F32) * (M_DK ** -0.5)
        vf = seq(v_ref, s_i)[:, lo:lo + M_DV].astype(F32)
        kb, vb = kf.astype(BF16), vf.astype(BF16)
        if mxu_sums:
            n_rows = jnp.broadcast_to(state[s_i, h][1], (M_DK, M_DK)).astype(BF16)
            qk = _dot_nt(qb, jnp.concatenate([kb, n_rows], axis=0))
            sc, qn_rep = qk[:, :L] * jnp.exp(dm - wide(m_t)), qk[:, L:]
        else:
            sc, qn_rep = _dot_nt(qb, kb) * jnp.exp(dm - wide(m_t)), None
        st2[s_i, h] = (qb, kf, vf, kb, vb, sc, qn_rep)
    st3 = {}
    for s_i, h in heads:
        cm, nrow, m0 = state[s_i, h]
        fc, dm, m_t, inter = st1[s_i, h]
        qb, kf, vf, kb, vb, sc, qn_rep = st2[s_i, h]
        if mxu_sums:
            sv = _dot(sc.astype(BF16), jnp.concatenate([vb, jnp.ones((L, M_DV), BF16)], axis=1))
            num = inter * _dot(qb, cm.astype(BF16)) + sv[:, :M_DV]
            qn = inter * qn_rep + sv[:, M_DV:]
            floor = jnp.exp(-m_t)
        else:
            num = inter * _dot(qb, cm.astype(BF16)) + _dot(sc.astype(BF16), vb)
            qn = (inter * jnp.sum(qb.astype(F32) * nrow, axis=-1, keepdims=True)
                  + jnp.sum(sc, axis=-1, keepdims=True))
            floor = jnp.exp(-m_t)
        st3[s_i, h] = num / jnp.maximum(jnp.abs(qn), floor)
    new_state = {}
    for s_i, h in heads:
        cm, nrow, m0 = state[s_i, h]
        fc, dm, m_t, inter = st1[s_i, h]
        qb, kf, vf, kb, vb, sc, _ = st2[s_i, h]
        ic = jnp.broadcast_to(gates[s_i][2][:, h:h + 1], (L, LANES))
        m_end = m_t[L - 1:L, :]
        f_end = fc[L - 1:L, :]
        decay = jnp.exp(m0 + f_end - m_end)
        w_s = jnp.exp(f_end - fc + ic - m_end)
        new_state[s_i, h] = (decay * cm + _dot_tn(kb, (w_s * vf).astype(BF16)),
                             decay * nrow + jnp.sum(w_s * kf, axis=0, keepdims=True),
                             m_end)
    for s_i, h in heads:
        lo = h * M_DK
        hn = _norm(st3[s_i, h]) * g_ref[:, lo:lo + M_DV]
        gate = _sigmoid(seq(o_ref, s_i)[:, lo:lo + M_DV].astype(F32))
        seq(h_ref, s_i)[:, lo:lo + M_DV] = (gate * hn).astype(BF16)
    for s_i, h in heads:
        c_out[s_i, h], n_out[s_i, h], m_out[s_i, h:h + 1, :] = new_state[s_i, h]
    if not carried:
        for s_i in seqs:
            m_out[s_i, M_HEADS:, :] = jnp.zeros((M_HEADS, LANES), F32)

    if carried:
        @pl.when(c == pl.num_programs(chunk_axis) - 1)
        def _():
            c1_ref[...] = c_out[...]
            n1_ref[...] = n_out[...]
            m1_ref[...] = m_out[...]


def _mlstm_state_specs(nseq, l_state):
    if l_state is None:
        c_spec = pl.BlockSpec((nseq, M_HEADS, M_DK, M_DV), lambda b, c: (b, 0, 0, 0))
    else:
        c_spec = pl.BlockSpec((None, nseq, M_HEADS, M_DK, M_DV), lambda b, c: (l_state, b, 0, 0, 0))
    return (c_spec,
            pl.BlockSpec((nseq, M_HEADS, 1, M_DK), lambda b, c: (b, 0, 0, 0)),
            pl.BlockSpec((nseq, 2 * M_HEADS, LANES), lambda b, c: (b, 0, 0)))


def _mlstm_scratch(nseq):
    return [pltpu.VMEM((nseq, M_HEADS, M_DK, M_DV), F32),
            pltpu.VMEM((nseq, M_HEADS, 1, M_DK), F32),
            pltpu.VMEM((nseq, 2 * M_HEADS, LANES), F32)]


def _mlstm_prompt(z, zif, ifr, gamma, l, bp, nc, L):
    hw = M_HEADS * M_DK
    zero = lambda *s: jnp.zeros(s, F32)
    in_specs, args = [], []
    for b in range(bp):
        def rows(col, b=b):
            return lambda c: (b * nc + c, col)

        in_specs += [pl.BlockSpec((L, LANES), rows(0)),
                     pl.BlockSpec((1, 2 * M_HEADS, L), lambda c, b=b: (b * nc + c, 0, 0)),
                     pl.BlockSpec((L, hw), rows(Z_MQ // hw)),
                     pl.BlockSpec((L, hw), rows(Z_MK // hw)),
                     pl.BlockSpec((L, hw), rows(Z_MV // hw)),
                     pl.BlockSpec((L, hw), rows(Z_MO // hw))]
        args += [zif, ifr, z, z, z, z]
    state_specs = [pl.BlockSpec((bp, M_HEADS, M_DK, M_DV), lambda c: (0, 0, 0, 0)),
                   pl.BlockSpec((bp, M_HEADS, 1, M_DK), lambda c: (0, 0, 0, 0)),
                   pl.BlockSpec((bp, 2 * M_HEADS, LANES), lambda c: (0, 0, 0))]
    kern = functools.partial(_mlstm_kernel, t_valid=L, nseq=bp, aliased=False, carried=True, chunk_axis=0,
                             per_seq_inputs=True)
    outs = pl.pallas_call(
        kern,
        grid=(nc,),
        in_specs=in_specs + [pl.BlockSpec((None, 1, hw), lambda c: (l, 0, 0))] + state_specs,
        out_specs=[pl.BlockSpec((bp, L, hw), lambda c: (0, c, 0))] + state_specs,
        out_shape=[jax.ShapeDtypeStruct((bp, nc * L, hw), BF16),
                   jax.ShapeDtypeStruct((bp, M_HEADS, M_DK, M_DV), F32),
                   jax.ShapeDtypeStruct((bp, M_HEADS, 1, M_DK), F32),
                   jax.ShapeDtypeStruct((bp, 2 * M_HEADS, LANES), F32)],
        scratch_shapes=_mlstm_scratch(bp),
        compiler_params=_cparams("arbitrary"),
        name="mlstm_prompt",
    )(*args, gamma, zero(bp, M_HEADS, M_DK, M_DV), zero(bp, M_HEADS, 1, M_DK), zero(bp, 2 * M_HEADS, LANES))
    return (outs[0].reshape(bp * nc * L, hw),) + tuple(outs[1:])


def _mlstm_sample(zm3, if3, ifr, gamma, c_all, n0, m0x, c_out_prev, l, t_valid, nseq):
    bs, L, _ = zm3.shape
    depth = c_all.shape[0]
    hw = M_HEADS * M_DK
    aliased = c_out_prev is not None

    def blk(col):
        return pl.BlockSpec((nseq, L, hw), lambda b, c: (b, 0, col))

    c_in, n_spec, m_spec = _mlstm_state_specs(nseq, l)
    in_specs = [pl.BlockSpec((nseq, L, LANES), lambda b, c: (b, 0, 0)),
                pl.BlockSpec((nseq, 2 * M_HEADS, L), lambda b, c: (b, 0, 0)),
                blk(0), blk(1), blk(2), blk(3),
                pl.BlockSpec((None, 1, hw), lambda b, c: (l, 0, 0)),
                c_in, n_spec, m_spec]
    args = [if3, ifr, zm3, zm3, zm3, zm3, gamma, c_all, n0, m0x]
    aliases = {}
    if aliased:
        in_specs.append(pl.BlockSpec(memory_space=pl.ANY))
        args.append(c_out_prev)
        aliases = {len(args) - 1: 1}
        c_out = c_in
    else:
        c_out = pl.BlockSpec((depth, nseq, M_HEADS, M_DK, M_DV), lambda b, c: (0, b, 0, 0, 0))
    kern = functools.partial(_mlstm_kernel, t_valid=t_valid, nseq=nseq, aliased=aliased, carried=False,
                             chunk_axis=1, first_layer=None if aliased else l)
    return pl.pallas_call(
        kern,
        grid=(bs // nseq, 1),
        in_specs=in_specs,
        out_specs=[pl.BlockSpec((nseq, L, hw), lambda b, c: (b, 0, 0)), c_out, n_spec, m_spec],
        out_shape=[jax.ShapeDtypeStruct((bs, L, hw), BF16),
                   jax.ShapeDtypeStruct((depth, bs, M_HEADS, M_DK, M_DV), F32),
                   jax.ShapeDtypeStruct((bs, M_HEADS, 1, M_DK), F32),
                   jax.ShapeDtypeStruct((bs, 2 * M_HEADS, LANES), F32)],
        input_output_aliases=aliases,
        compiler_params=_cparams("parallel", "arbitrary"),
        name="mlstm_sample",
    )(*args)


def _m_state_in(m):
    lead = m.shape[:-1]
    mx = jnp.zeros(lead + (2 * M_HEADS, LANES), F32)
    return mx.at[..., :M_HEADS, :].set(jnp.broadcast_to(m[..., None], lead + (M_HEADS, LANES)))


def _merge_kernel(*refs, np_tiles, alpha, route):
    (cp_ref, cs_ref, ap_ref, as_ref, mp_ref, ms_ref, g0_ref, g1_ref, g2_ref, xp_ref, xs_ref,
     gtp_ref, gts_ref, shp_ref, shs_ref, scp_ref, scs_ref,
     wc_ref, wa_ref, wm_ref, wo_ref, pg_ref, pb_ref) = refs[:23]
    if route:
        rw_ref, rb_ref, x1_ref, h_ref, route_ref = refs[23:]
    else:
        x1_ref, h_ref = refs[23:]
    is_s = pl.program_id(0) >= np_tiles

    def gate(ref):
        return _sigmoid(ref[...].astype(F32))

    def run(c_ref, a_ref, m_ref, x_ref, gt_ref, sh_ref, sc_ref):
        y = (gate(g0_ref) * _dot(c_ref[...], wc_ref[...])
             + gate(g1_ref) * _dot(a_ref[...], wa_ref[...])
             + gate(g2_ref) * _dot(m_ref[...], wm_ref[...]))
        mix = _dot(y.astype(BF16), wo_ref[...])
        t = x_ref.shape[0]
        x1 = _norm(alpha * x_ref[...] + _rows(gt_ref, t) * mix) * pg_ref[...] + pb_ref[...]
        x1_ref[...] = x1
        h = _norm(x1) * (1.0 + _rows(sc_ref, t)) + _rows(sh_ref, t)
        h_ref[...] = h.astype(h_ref.dtype)
        if route:
            h_hi, h_lo = _split_bf16(h)
            w_hi, w_lo = _split_bf16(rw_ref[...])
            logits = _dot(h_hi, w_hi) + (_dot(h_hi, w_lo) + _dot(h_lo, w_hi)) + rb_ref[...]
            lane = lax.broadcasted_iota(jnp.int32, logits.shape, 1)
            m1 = jnp.max(logits, axis=-1, keepdims=True)
            e1 = jnp.min(jnp.where(logits == m1, lane, LANES), axis=-1, keepdims=True)
            l2 = jnp.where(lane == e1, NEG_INF, logits)
            m2 = jnp.max(l2, axis=-1, keepdims=True)
            e2 = jnp.min(jnp.where(l2 == m2, lane, LANES), axis=-1, keepdims=True)
            ex = jnp.exp(m2 - m1)
            w1 = 1.0 / (1.0 + ex)
            w2 = ex / (1.0 + ex)
            out = jnp.where(lane == 0, e1.astype(F32),
                            jnp.where(lane == 1, e2.astype(F32),
                                      jnp.where(lane == 2, w1, jnp.where(lane == 3, w2, 0.0))))
            route_ref[...] = out

    @pl.when(is_s)
    def _():
        run(cs_ref, as_ref, ms_ref, xs_ref, gts_ref, shs_ref, scs_ref)

    @pl.when(jnp.logical_not(is_s))
    def _():
        run(cp_ref, ap_ref, mp_ref, xp_ref, gtp_ref, shp_ref, scp_ref)


def _merge(cp, cs, ap, as_, mp, ms, z, xp, xs, mod_p, mod_s, wc, wa, wm, wo, pg, pb, l, rm, alpha, router):
    n, d = z.shape[0], xp.shape[1]
    hw = D_CONV
    route = router is not None
    tm = rm.t

    def pblk():
        return pl.BlockSpec((tm, hw), lambda i: (rm.prompt(i), 0))

    def sblk():
        return pl.BlockSpec((tm, hw), lambda i: (rm.sample(i), 0), pipeline_mode=_RARELY)

    def zg(k):
        return pl.BlockSpec((tm, d), lambda i: (i, Z_G // d + k))

    def lw(a):
        return pl.BlockSpec((None,) + a.shape[1:], lambda i: (l,) + (0,) * (a.ndim - 1), pipeline_mode=_RARELY)

    post = pl.BlockSpec((None, None, 1, d), lambda i: (l, 0, 0, 0))
    in_specs = [pblk(), sblk(), pblk(), sblk(), pblk(), sblk(), zg(0), zg(1), zg(2),
                *rm.x_specs(xs, d),
                *rm.mod_specs(l, 2, d), *rm.mod_specs(l, 3, d), *rm.mod_specs(l, 4, d),
                lw(wc), lw(wa), lw(wm), lw(wo), post, post]
    args = [cp, cs, ap, as_, mp, ms, z, z, z, xp, xs, mod_p, mod_s, mod_p, mod_s, mod_p, mod_s,
            wc, wa, wm, wo, pg, pb]
    out_specs = [pl.BlockSpec((tm, d), lambda i: (i, 0)), pl.BlockSpec((tm, d), lambda i: (i, 0))]
    out_shape = [jax.ShapeDtypeStruct((n, d), F32), jax.ShapeDtypeStruct((n, d), F32 if route else BF16)]
    if route:
        rw, rb, lj = router
        in_specs += [pl.BlockSpec((None, d, LANES), lambda i: (lj, 0, 0)),
                     pl.BlockSpec((None, 1, LANES), lambda i: (lj, 0, 0))]
        args += [rw, rb]
        out_specs.append(pl.BlockSpec((tm, LANES), lambda i: (i, 0)))
        out_shape.append(jax.ShapeDtypeStruct((n, LANES), F32))
    kern = functools.partial(_merge_kernel, np_tiles=rm.np_tiles, alpha=alpha, route=route)
    return pl.pallas_call(
        kern, grid=(n // tm,), in_specs=in_specs, out_specs=out_specs, out_shape=out_shape,
        compiler_params=_cparams("parallel"), name="merge_route" if route else "merge",
    )(*args)


def _post_residual(x_ref, gp_ref, gs_ref, f, pg_ref, pb_ref, is_s, alpha):
    return _norm(alpha * x_ref[...] + _pick(is_s, gp_ref, gs_ref, x_ref.shape[0]) * f) * pg_ref[...] + pb_ref[...]


def _ffn_kernel(h_ref, wg_ref, wu_ref, wd_ref, x_ref, gp_ref, gs_ref, pg_ref, pb_ref, o_ref, *, alpha, np_tiles):
    h = h_ref[...]
    a = (_silu(_dot(h, wg_ref[...])) * _dot(h, wu_ref[...])).astype(BF16)
    is_s = pl.program_id(0) >= np_tiles
    o_ref[...] = _post_residual(x_ref, gp_ref, gs_ref, _dot(a, wd_ref[...]), pg_ref, pb_ref, is_s, alpha)


def _ffn_dense(h, wg, wu, wd, x, mod_p, mod_s, pg, pb, l, lj, rm, alpha):
    n, d = x.shape
    f = wg.shape[2]
    tm = rm.t
    post = pl.BlockSpec((None, None, 1, d), lambda i: (l, 1, 0, 0))
    return pl.pallas_call(
        functools.partial(_ffn_kernel, alpha=alpha, np_tiles=rm.np_tiles),
        grid=(n // tm,),
        in_specs=[pl.BlockSpec((tm, d), lambda i: (i, 0)),
                  pl.BlockSpec((None, d, f), lambda i: (lj, 0, 0), pipeline_mode=_RARELY),
                  pl.BlockSpec((None, d, f), lambda i: (lj, 0, 0), pipeline_mode=_RARELY),
                  pl.BlockSpec((None, f, d), lambda i: (lj, 0, 0), pipeline_mode=_RARELY),
                  pl.BlockSpec((tm, d), lambda i: (i, 0)),
                  *rm.mod_specs(l, 5, d), post, post],
        out_specs=pl.BlockSpec((tm, d), lambda i: (i, 0)),
        out_shape=jax.ShapeDtypeStruct((n, d), F32),
        compiler_params=_cparams("parallel"),
        name="ffn_dense",
    )(h, wg, wu, wd, x, mod_p, mod_s, pg, pb)


def _rank_kernel(route_ref, rank_ref, tot_ref, carry):
    i = pl.program_id(0)
    tm = route_ref.shape[0]

    @pl.when(i == 0)
    def _():
        carry[...] = jnp.zeros_like(carry)

    r = route_ref[...]
    lane = lax.broadcasted_iota(jnp.int32, (tm, LANES), 1)
    e1 = r[:, 0:1].astype(jnp.int32)
    e2 = r[:, 1:2].astype(jnp.int32)
    hit1 = lane == e1
    hit2 = lane == e2
    onehot = jnp.where(jnp.logical_or(hit1, hit2), 1.0, 0.0)
    tt = lax.broadcasted_iota(jnp.int32, (tm, tm), 0)
    ss = lax.broadcasted_iota(jnp.int32, (tm, tm), 1)
    before = jnp.where(ss < tt, 1.0, 0.0).astype(BF16)
    cnt = _dot(before, onehot.astype(BF16)) + carry[0:1, :]
    r1 = jnp.sum(jnp.where(hit1, cnt, 0.0), axis=-1, keepdims=True)
    r2 = jnp.sum(jnp.where(hit2, cnt, 0.0), axis=-1, keepdims=True)
    rank_ref[...] = jnp.where(lane == 0, r1, jnp.where(lane == 1, r2, 0.0))
    carry[...] = carry[...] + jnp.sum(onehot, axis=0, keepdims=True)
    tot_ref[...] = carry[...]


def _moe_rank(route, tm):
    n = route.shape[0]
    return pl.pallas_call(
        _rank_kernel,
        grid=(n // tm,),
        in_specs=[pl.BlockSpec((tm, LANES), lambda i: (i, 0))],
        out_specs=[pl.BlockSpec((tm, LANES), lambda i: (i, 0)),
                   pl.BlockSpec((8, LANES), lambda i: (0, 0))],
        out_shape=[jax.ShapeDtypeStruct((n, LANES), F32), jax.ShapeDtypeStruct((8, LANES), F32)],
        scratch_shapes=[pltpu.VMEM((8, LANES), F32)],
        compiler_params=_cparams("arbitrary"),
        name="moe_rank",
    )(route)


def _row_copy(src, s, dst, t, sem):
    return pltpu.make_async_copy(src.at[pl.ds(s, 1)], dst.at[pl.ds(t, 1)], sem)


def _dispatch_kernel(d1_ref, d2_ref, zl_ref, h_ref, xs_hbm, zbuf, stage, sem, zsem):
    i = pl.program_id(0)
    tm = h_ref.shape[0]

    @pl.when(i == 0)
    def _():
        zbuf[...] = jnp.zeros_like(zbuf)

        def zero_copy(k):
            row = pl.multiple_of(jnp.maximum(zl_ref[k], 0), MOE_SUB)
            return pltpu.make_async_copy(zbuf, xs_hbm.at[pl.ds(row, MOE_SUB)], zsem)

        def start(k, carry):
            @pl.when(zl_ref[k] >= 0)
            def _():
                zero_copy(k).start()
            return carry

        def wait(k, carry):
            @pl.when(zl_ref[k] >= 0)
            def _():
                zero_copy(k).wait()
            return carry

        lax.fori_loop(0, zl_ref.shape[0], start, 0)
        lax.fori_loop(0, zl_ref.shape[0], wait, 0)

    slot = i % 2
    stage[slot] = h_ref[...]

    def issue(r, carry):
        _row_copy(stage.at[slot], r, xs_hbm, d1_ref[i * tm + r], sem.at[slot]).start(priority=0)
        _row_copy(stage.at[slot], r, xs_hbm, d2_ref[i * tm + r], sem.at[slot]).start(priority=1)
        return carry

    for r in range(tm):
        issue(r, 0)

    def drain(s):
        for _ in range(2):
            pltpu.make_async_copy(stage.at[s], xs_hbm.at[pl.ds(0, tm)], sem.at[s]).wait()

    @pl.when(i > 0)
    def _():
        drain(1 - slot)

    @pl.when(i == pl.num_programs(0) - 1)
    def _():
        drain(slot)


def _moe_dispatch(dest1, dest2, zlist, h, n_rows, tm):
    n, d = h.shape
    grid_spec = pltpu.PrefetchScalarGridSpec(
        num_scalar_prefetch=3,
        grid=(n // tm,),
        in_specs=[pl.BlockSpec((tm, d), lambda i, *_: (i, 0))],
        out_specs=pl.BlockSpec(memory_space=pl.ANY),
        scratch_shapes=[pltpu.VMEM((MOE_SUB, d), F32), pltpu.VMEM((2, tm, d), F32),
                        pltpu.SemaphoreType.DMA((2,)), pltpu.SemaphoreType.DMA(())],
    )
    return pl.pallas_call(
        _dispatch_kernel, grid_spec=grid_spec,
        out_shape=jax.ShapeDtypeStruct((n_rows, d), F32),
        compiler_params=_cparams("arbitrary"),
        name="moe_dispatch",
    )(dest1, dest2, zlist, h)


def _expert_kernel(blk_e_ref, nvalid_ref, xs_ref, wg_ref, wu_ref, wd_ref, y_ref, xb):
    b = pl.program_id(0)
    j = pl.program_id(1)
    tb = xb.shape[0]
    nsub = tb // MOE_SUB
    nv = nvalid_ref[b]
    used = (nv + MOE_SUB - 1) // MOE_SUB

    def run(rows):
        wg = wg_ref[...].astype(BF16)
        wu = wu_ref[...].astype(BF16)
        wd = wd_ref[...].astype(BF16)

        @pl.when(j == 0)
        def _():
            xb[rows, :] = xs_ref[rows, :].astype(BF16)

        x = xb[rows, :]
        a = (_silu(_dot(x, wg)) * _dot(x, wu)).astype(BF16)
        part = _dot(a, wd)

        @pl.when(j == 0)
        def _():
            y_ref[rows, :] = part

        @pl.when(j > 0)
        def _():
            y_ref[rows, :] = y_ref[rows, :] + part

    for m in range(1, nsub + 1):
        @pl.when(used == m)
        def _():
            run(pl.ds(0, m * MOE_SUB))

    for s in range(nsub):
        @pl.when(jnp.logical_and(s >= used, j == pl.num_programs(1) - 1))
        def _():
            y_ref[pl.ds(s * MOE_SUB, MOE_SUB), :] = jnp.zeros((MOE_SUB, y_ref.shape[1]), F32)


def _moe_experts(blk_e, nvalid, xs, wg, wu, wd, lj, tb, tf):
    n_rows, d = xs.shape
    f = wg.shape[3]
    nj = f // tf

    def jm(b, j, nv):
        return jnp.where(nv[b] > 0, j, nj - 1)

    grid_spec = pltpu.PrefetchScalarGridSpec(
        num_scalar_prefetch=2,
        grid=(n_rows // tb, nj),
        in_specs=[pl.BlockSpec((tb, d), lambda b, j, be, nv: (b, 0)),
                  pl.BlockSpec((None, None, d, tf), lambda b, j, be, nv: (lj, be[b], 0, jm(b, j, nv))),
                  pl.BlockSpec((None, None, d, tf), lambda b, j, be, nv: (lj, be[b], 0, jm(b, j, nv))),
                  pl.BlockSpec((None, None, tf, d), lambda b, j, be, nv: (lj, be[b], jm(b, j, nv), 0))],
        out_specs=pl.BlockSpec((tb, d), lambda b, j, be, nv: (b, 0)),
        scratch_shapes=[pltpu.VMEM((tb, d), BF16)],
    )
    return pl.pallas_call(
        _expert_kernel, grid_spec=grid_spec,
        out_shape=jax.ShapeDtypeStruct((n_rows, d), F32),
        compiler_params=_cparams("parallel", "arbitrary"),
        name="moe_experts",
    )(blk_e, nvalid, xs, wg, wu, wd)


def _combine_kernel(d1_ref, d2_ref, ys_hbm, route_ref, x_ref, gp_ref, gs_ref, pg_ref, pb_ref, *rest,
                    alpha, np_tiles, split):
    (*outs, y1, y2, sem1, sem2) = rest
    i = pl.program_id(0)
    tm = x_ref.shape[0]
    slot = i % 2

    last = pl.num_programs(0) - 1

    def issue(tile, s, r):
        _row_copy(ys_hbm, d1_ref[tile * tm + r], y1.at[s], r, sem1.at[s]).start(priority=0)
        _row_copy(ys_hbm, d2_ref[tile * tm + r], y2.at[s], r, sem2.at[s]).start(priority=1)

    def drain(s):
        pltpu.make_async_copy(ys_hbm.at[pl.ds(0, tm)], y1.at[s], sem1.at[s]).wait()
        pltpu.make_async_copy(ys_hbm.at[pl.ds(0, tm)], y2.at[s], sem2.at[s]).wait()

    @pl.when(i == 0)
    def _():
        lax.fori_loop(0, tm, lambda r, c: (issue(0, 0, r), c)[1], 0, unroll=8)

    drain(slot)
    nxt = jnp.minimum(i + 1, last)
    for r in range(tm):
        issue(nxt, 1 - slot, r)
    r = route_ref[...]
    f = y1[slot] * r[:, 2:3] + y2[slot] * r[:, 3:4]
    out = _post_residual(x_ref, gp_ref, gs_ref, f, pg_ref, pb_ref, i >= np_tiles, alpha)
    if split:
        op_ref, os_ref = outs

        @pl.when(i < np_tiles)
        def _():
            op_ref[...] = out

        @pl.when(i >= np_tiles)
        def _():
            os_ref[...] = out
    else:
        outs[0][...] = out

    @pl.when(i == last)
    def _():
        drain(1 - slot)


def _moe_combine(dest1, dest2, ys, route, x, mod_p, mod_s, pg, pb, l, rm, alpha, split):
    n, d = x.shape
    tm = rm.t
    if split:
        n_p = rm.np_tiles * tm
        out_specs = [pl.BlockSpec((tm, d), lambda i, *_: (rm.prompt(i), 0)),
                     pl.BlockSpec((tm, d), lambda i, *_: (rm.sample(i), 0))]
        out_shape = [jax.ShapeDtypeStruct((n_p, d), F32), jax.ShapeDtypeStruct((n - n_p, d), F32)]
    else:
        out_specs = pl.BlockSpec((tm, d), lambda i, *_: (i, 0))
        out_shape = jax.ShapeDtypeStruct((n, d), F32)
    post = pl.BlockSpec((None, None, 1, d), lambda i, *_: (l, 1, 0, 0))
    grid_spec = pltpu.PrefetchScalarGridSpec(
        num_scalar_prefetch=2,
        grid=(n // tm,),
        in_specs=[pl.BlockSpec(memory_space=pl.ANY),
                  pl.BlockSpec((tm, LANES), lambda i, *_: (i, 0)),
                  pl.BlockSpec((tm, d), lambda i, *_: (i, 0)),
                  *rm.mod_specs(l, 5, d), post, post],
        out_specs=out_specs,
        scratch_shapes=[pltpu.VMEM((2, tm, d), F32), pltpu.VMEM((2, tm, d), F32),
                        pltpu.SemaphoreType.DMA((2,)), pltpu.SemaphoreType.DMA((2,))],
    )
    return pl.pallas_call(
        functools.partial(_combine_kernel, alpha=alpha, np_tiles=rm.np_tiles, split=split), grid_spec=grid_spec,
        out_shape=out_shape,
        compiler_params=_cparams("arbitrary"),
        name="moe_combine",
    )(dest1, dest2, ys, route, x, mod_p, mod_s, pg, pb)


def _moe_ffn(h, route, wg, wu, wd, x, mod_p, mod_s, pg, pb, l, lj, rm, tb, tf, alpha, split):
    n = h.shape[0]
    rank, tot = _moe_rank(route, rm.t)
    counts = tot[0, :N_EXPERTS].astype(jnp.int32)
    padded = (counts + tb - 1) // tb * tb
    pad_end = jnp.cumsum(padded)
    pad_start = pad_end - padded
    experts = jnp.arange(N_EXPERTS, dtype=jnp.int32)

    def slot(col):
        e = route[:, col].astype(jnp.int32)
        start = jnp.sum(jnp.where(e[:, None] == experts[None, :], pad_start[None, :], 0), axis=1)
        return start + rank[:, col].astype(jnp.int32)

    dest1, dest2 = slot(0), slot(1)
    n_blocks = -(-(2 * n + N_EXPERTS * (tb - 1)) // tb)
    blk_start = jnp.arange(n_blocks, dtype=jnp.int32) * tb
    blk_e = jnp.minimum(jnp.sum(pad_end[None, :] <= blk_start[:, None], axis=1), N_EXPERTS - 1).astype(jnp.int32)
    own_end = jnp.sum(jnp.where(blk_e[:, None] == experts[None, :], (pad_start + counts)[None, :], 0), axis=1)
    nvalid = jnp.clip(own_end - blk_start, 0, tb).astype(jnp.int32)
    nvalid = jnp.where(blk_start < pad_end[-1], nvalid, 0)
    per_blk = tb // MOE_SUB
    sub_start = jnp.arange(n_blocks * per_blk, dtype=jnp.int32) * MOE_SUB
    sub_room = jnp.repeat(blk_start + nvalid, per_blk) - sub_start
    zlist = jnp.where(sub_room < MOE_SUB, sub_start, -1).astype(jnp.int32)
    xs = _moe_dispatch(dest1, dest2, zlist, h, n_blocks * tb, rm.t)
    ys = _moe_experts(blk_e, nvalid, xs, wg, wu, wd, lj, tb, tf)
    return _moe_combine(dest1, dest2, ys, route, x, mod_p, mod_s, pg, pb, l, rm, alpha, split)


def _pack_w_in(w_in, b_in):
    depth, d, _ = w_in.shape
    q_end = 2 * D_CONV + N_HEADS * HEAD_DIM
    k_end = q_end + N_KV * HEAD_DIM
    a_end = k_end + N_KV * HEAD_DIM
    m_end = a_end + 4 * M_HEADS * M_DK
    if_end = m_end + 2 * M_HEADS

    def pack(a):
        return jnp.concatenate([a[..., if_end:], a[..., :q_end], a[..., a_end:m_end], a[..., q_end:a_end]], axis=-1)

    def gates(a):
        return jnp.pad(a[..., m_end:if_end], [(0, 0)] * (a.ndim - 1) + [(0, LANES - 2 * M_HEADS)])

    return (pack(w_in).astype(BF16), pack(b_in).reshape(depth, 1, Z_W),
            gates(w_in).astype(BF16), gates(b_in).reshape(depth, 1, LANES))


def kernel(x_prompt, x_sample, cache_swa_k, cache_swa_v, state_conv, state_mlstm_C, state_mlstm_n, state_mlstm_m, c_prompt, c_sample, w_ada, b_ada, w_in, b_in, conv_w, conv_b, conv_ln_g, conv_ln_b, w_conv_out, attn_sinks, rel_bias, w_attn_out, m_norm_g, w_m_out, w_out, post_ln_g, post_ln_b, ffn_w_gate, ffn_w_up, ffn_w_down, router_w, router_b, moe_w_gate, moe_w_up, moe_w_down):
    bp, tp, d = x_prompt.shape
    bs, ts, _ = x_sample.shape
    depth = w_ada.shape[0]
    alpha = (2 * depth) ** 0.25
    n_p, n_s = bp * tp, bs * ts
    tm = n_s
    assert d == D_MODEL and tp % tm == 0 and tp % WINDOW == 0 and tm % 32 == 0
    rm = _RowMap(tm, bp, tp, n_p, n_s, mod_rows=bs)
    rm_half = _RowMap(tm // 2, bp, tp, n_p, n_s)
    wb = cache_swa_k.shape[2]
    big = n_p >= 4096
    tc = 1024 if big else tm
    lm = 256 if big else min(tp, 128)
    tb = 1024 if big else 2 * MOE_SUB
    bs_blk = 32 if bs % 32 == 0 else bs
    bs_att = 16 if bs % 16 == 0 else bs
    bs_m = 4 if bs % 4 == 0 else 1
    lts = 16

    n = n_p + n_s
    x = (x_prompt.reshape(n_p, d), jnp.transpose(x_sample, (1, 0, 2)).reshape(n_s, d))

    nc_rows = -(-(bp + bs) // 8) * 8
    c_all = jnp.zeros((nc_rows, d), F32).at[:bp].set(c_prompt).at[bp:bp + bs].set(c_sample)
    mod = _ada_mod(c_all, w_ada, b_ada)
    mod_p = mod[:, :bp].reshape(depth, bp, 1, 6 * d)
    mod_s = mod[:, bp:bp + bs]

    w_in_p, b_in_p, w_if, b_if = _pack_w_in(w_in, b_in)
    wc_b, wa_b, wm_b, wo_b = (w.astype(BF16) for w in (w_conv_out, w_attn_out, w_m_out, w_out))
    fg_b, fu_b, fd_b = (w.astype(BF16) for w in (ffn_w_gate, ffn_w_up, ffn_w_down))
    cw_pad = jnp.pad(conv_w, ((0, 0), (0, CONV_PAD - CONV_W), (0, 0)))
    cvecs = [v.reshape(depth, 1, D_CONV) for v in (conv_b, conv_ln_g, conv_ln_b)]
    rw_pad = jnp.pad(router_w, ((0, 0), (0, 0), (0, LANES - N_EXPERTS)))
    rb_pad = jnp.pad(router_b, ((0, 0), (0, LANES - N_EXPERTS)), constant_values=NEG_INF)[:, None, :]
    pg = post_ln_g.reshape(depth, 2, 1, d)
    pb = post_ln_b.reshape(depth, 2, 1, d)
    gamma = m_norm_g.reshape(depth, 1, M_HEADS * M_DV)
    sinks = attn_sinks.astype(F32)

    qi = jnp.arange(WINDOW)[:, None]
    kj = jnp.arange(2 * WINDOW)[None, :]
    dist_p = qi + WINDOW - kj
    bh = _bias_heads(rel_bias, dist_p, (dist_p >= 0) & (dist_p < WINDOW))
    bias_p = bh.reshape(N_HEADS // 2, 2, WINDOW, 2 * WINDOW).transpose(0, 2, 1, 3).reshape(
        N_HEADS // 2, WINDOW, 4 * WINDOW)
    dist_s = jnp.arange(ts)[:, None] + wb - jnp.arange(wb + ts)[None, :]
    bias_s = _bias_heads(rel_bias, dist_s, (dist_s >= 0) & (dist_s < WINDOW)).reshape(
        N_KV, Q_PER_KV * ts, wb + ts)
    kc_all = cache_swa_k.reshape(depth, bs, wb, N_KV * HEAD_DIM)
    vc_all = cache_swa_v.reshape(depth, bs, wb, N_KV * HEAD_DIM)

    f_moe = moe_w_gate.shape[3]
    tf_moe = 512 if f_moe % 512 == 0 else f_moe

    n0_all = state_mlstm_n[:, :, :, None, :]
    m0_all = _m_state_in(state_mlstm_m)
    s_c = s_conv = s_kv = None
    new_p = [[] for _ in range(6)]
    new_s = [[] for _ in range(6)]
    for l in range(depth):
        j = l // 2
        xp, xs = x if isinstance(x, tuple) else (x, x)
        z, zif = _ln_proj(xp, xs, mod_p, mod_s, w_in_p, b_in_p, w_if, b_if, l, rm, n)
        zs3 = z[n_p:].reshape(ts, bs, Z_W)

        cp, ns_p = _conv_prompt(z, jnp.zeros((bp, CONV_PAD, D_CONV), F32), cw_pad, *cvecs, l, bp, tp, tc)
        cs3, s_conv = _conv_sample(zs3, state_conv, cw_pad, *cvecs, s_conv, l, bs_blk)
        new_p[2].append(ns_p[:, CONV_PAD - CONV_W + 1:])

        sink_h = sinks[l].reshape(N_KV, Q_PER_KV, 1)
        sink_s = jnp.broadcast_to(sink_h, (N_KV, Q_PER_KV, ts)).reshape(N_KV, Q_PER_KV * ts, 1)
        ap = _attn_prompt(z, bias_p * LOG2E, sinks * LOG2E, l, bp, tp)
        nk = min(WINDOW, tp)
        kv_tail = jnp.stack([z[(b + 1) * tp - nk:(b + 1) * tp, Z_K:Z_K + 2 * N_KV * HEAD_DIM]
                             for b in range(bp)]).astype(F32)
        new_p[0].append(kv_tail[..., :N_KV * HEAD_DIM].reshape(bp, nk, N_KV, HEAD_DIM))
        new_p[1].append(kv_tail[..., N_KV * HEAD_DIM:].reshape(bp, nk, N_KV, HEAD_DIM))
        q_s = zs3[:, :, Z_Q:Z_Q + N_HEADS * HEAD_DIM].reshape(ts, bs, N_KV, Q_PER_KV, HEAD_DIM)
        q4 = jnp.transpose(q_s, (1, 2, 3, 0, 4)).reshape(bs, N_KV, Q_PER_KV * ts, HEAD_DIM)
        k_s = jnp.transpose(zs3[:, :, Z_K:Z_K + N_KV * HEAD_DIM].reshape(ts, bs, N_KV, HEAD_DIM), (1, 0, 2, 3))
        v_s = jnp.transpose(zs3[:, :, Z_V:Z_V + N_KV * HEAD_DIM].reshape(ts, bs, N_KV, HEAD_DIM), (1, 0, 2, 3))
        o4, s_kv = _attn_sample(q4, kc_all, vc_all, jnp.transpose(k_s, (0, 2, 1, 3)), jnp.transpose(v_s, (0, 2, 1, 3)),
                                bias_s, sink_s, zs3[:, :, Z_K:Z_K + N_KV * HEAD_DIM],
                                zs3[:, :, Z_V:Z_V + N_KV * HEAD_DIM], s_kv, l, bs_att)
        as_ = jnp.transpose(o4.reshape(bs, N_KV, Q_PER_KV, ts, HEAD_DIM), (3, 0, 1, 2, 4)).reshape(n_s, -1).astype(BF16)

        ncp = tp // lm
        if_p = zif[:n_p, :2 * M_HEADS].reshape(bp * ncp, lm, 2 * M_HEADS)
        mp, c1p, n1p, m1p = _mlstm_prompt(z, zif, jnp.transpose(if_p, (0, 2, 1)), gamma, l, bp, ncp, lm)
        new_p[3].append(c1p)
        new_p[4].append(n1p[:, :, 0])
        new_p[5].append(m1p[:, :M_HEADS, 0])
        tpad = ((0, 0), (0, lts - ts), (0, 0))
        zm3 = jnp.pad(jnp.transpose(zs3[:, :, Z_MQ:Z_K], (1, 0, 2)), tpad)
        if3 = jnp.pad(jnp.transpose(zif[n_p:].reshape(ts, bs, LANES), (1, 0, 2)), tpad)
        ms, s_c, n1s, m1s = _mlstm_sample(zm3, if3, jnp.transpose(if3[:, :, :2 * M_HEADS], (0, 2, 1)), gamma,
                                          state_mlstm_C, n0_all[l], m0_all[l], s_c, l, ts, bs_m)
        ms = jnp.transpose(ms[:, :ts], (1, 0, 2)).reshape(n_s, -1)
        new_s[4].append(n1s[:, :, 0])
        new_s[5].append(m1s[:, :M_HEADS, 0])

        moe = l % 2 == 1
        router = (rw_pad, rb_pad, j) if moe else None
        outs = _merge(cp, cs3.reshape(n_s, D_CONV), ap, as_, mp, ms, z, xp, xs, mod_p, mod_s,
                      wc_b, wa_b, wm_b, wo_b, pg, pb, l, rm, alpha, router)
        if moe:
            x1, h2, route = outs
            x = _moe_ffn(h2, route, moe_w_gate, moe_w_up, moe_w_down, x1, mod_p, mod_s, pg, pb,
                         l, j, rm, tb, tf_moe, alpha, split=l == depth - 1)
        else:
            x1, h2 = outs
            x = _ffn_dense(h2, fg_b, fu_b, fd_b, x1, mod_p, mod_s, pg, pb, l, j, rm, alpha)

    x_p, x_s = x if isinstance(x, (list, tuple)) else (x[:n_p], x[n_p:])
    y_p = x_p.reshape(bp, tp, d)
    y_s = jnp.transpose(x_s.reshape(ts, bs, d), (1, 0, 2))
    p_k, p_v, p_conv, p_c, p_n, p_m = [jnp.stack(a) for a in new_p]
    s_k, s_v = [a.reshape(cache_swa_k.shape) for a in s_kv]
    s_n, s_m = jnp.stack(new_s[4]), jnp.stack(new_s[5])
    return (y_p, y_s, p_k, p_v, p_conv, p_c, p_n, p_m, s_k, s_v, s_conv, s_c, s_n, s_m)
```
